```python
import jax
import jax.numpy as jnp
from jax import lax
import numpy as np


D_MODEL = 1024
BATCH = 8
SEQ = 4096
DEPTH = 2

N_META = 16
BLOCK = 128
EPS = 1e-6
FOX_HEADS = 8
FOX_HEAD_DIM = 64
MLA_HEADS = 8
MLA_Q_RANK = 256
MLA_KV_RANK = 128
MLA_NOPE_DIM = 64
MLA_ROPE_DIM = 32
MLA_V_DIM = 64
ROPE_THETA = 10000.0
SWA_Q_HEADS = 8
SWA_KV_HEADS = 2
SWA_HEAD_DIM = 64
WINDOW = 128
N_BRANCH = 3
BRANCH_WIDTH = FOX_HEADS * FOX_HEAD_DIM
D_FF = 2816
CONV_WIDTH = 3
IN_SPLITS = (FOX_HEADS * FOX_HEAD_DIM, FOX_HEADS * FOX_HEAD_DIM, FOX_HEADS * FOX_HEAD_DIM, FOX_HEADS,
             MLA_Q_RANK, MLA_KV_RANK, MLA_ROPE_DIM,
             SWA_Q_HEADS * SWA_HEAD_DIM, SWA_KV_HEADS * SWA_HEAD_DIM, SWA_KV_HEADS * SWA_HEAD_DIM,
             N_BRANCH * D_MODEL)
IN_WIDTH = sum(IN_SPLITS)

kernel_name = 'hybrid_fox_mla_swa_convffn'


def rms_norm(x, g):
    xf = x.astype(jnp.float32)
    y = xf * lax.rsqrt(jnp.mean(xf * xf, axis=-1, keepdims=True) + EPS)
    return (y * g.astype(jnp.float32)).astype(x.dtype)


def rope(x, pos):
    half = x.shape[-1] // 2
    freqs = ROPE_THETA ** (-jnp.arange(half, dtype=jnp.float32) / half)
    ang = pos.astype(jnp.float32)[:, None] * freqs[None, :]
    cos = jnp.cos(ang)[:, None, :]
    sin = jnp.sin(ang)[:, None, :]
    xf = x.astype(jnp.float32)
    x1, x2 = xf[..., :half], xf[..., half:]
    return jnp.concatenate([x1 * cos - x2 * sin, x2 * cos + x1 * sin], axis=-1).astype(x.dtype)


def alibi_slopes(n_heads):
    return jnp.exp2(-8.0 * jnp.arange(1, n_heads + 1, dtype=jnp.float32) / n_heads)


def causal_block_attention(q, k, v, log_decay=None):
    B, L, H, dk = q.shape
    M = N_META
    nb = (L - M) // BLOCK
    scale = dk ** -0.5
    pos = jnp.arange(L)

    def attend(qb, qpos, kk, vv, kpos, qdec=None, kdec=None):
        s = jnp.einsum('bqhd,bkhd->bhqk', qb, kk, preferred_element_type=jnp.float32) * scale
        if qdec is not None:
            s = s + jnp.swapaxes(qdec, 1, 2)[..., :, None] - jnp.swapaxes(kdec, 1, 2)[..., None, :]
        s = jnp.where(kpos[None, :] <= qpos[:, None], s, -jnp.inf)
        p = jax.nn.softmax(s, axis=-1).astype(vv.dtype)
        return jnp.einsum('bhqk,bkhd->bqhd', p, vv)

    qb = q[:, M:].reshape(B, nb, BLOCK, H, dk).swapaxes(0, 1)
    posb = pos[M:].reshape(nb, BLOCK)
    if log_decay is not None:
        meta = attend(q[:, :M], pos[:M], k[:, :M], v[:, :M], pos[:M], log_decay[:, :M], log_decay[:, :M])
        decb = log_decay[:, M:].reshape(B, nb, BLOCK, H).swapaxes(0, 1)
        real = lax.map(lambda a: attend(a[0], a[1], k, v, pos, a[2], log_decay), (qb, posb, decb))
    else:
        meta = attend(q[:, :M], pos[:M], k[:, :M], v[:, :M], pos[:M])
        real = lax.map(lambda a: attend(a[0], a[1], k, v, pos), (qb, posb))
    real = real.swapaxes(0, 1).reshape(B, L - M, H, v.shape[-1])
    return jnp.concatenate([meta, real], axis=1)


def sliding_window_attention(q, k, v, sinks, slopes):
    B, L, Hq, d = q.shape
    Hkv = k.shape[2]
    G = Hq // Hkv
    M = N_META
    nb = (L - M) // BLOCK
    scale = d ** -0.5
    q = q.reshape(B, L, Hkv, G, d)
    sinks = sinks.astype(jnp.float32).reshape(Hkv, G)
    slopes = slopes.reshape(Hkv, G, 1, 1)

    def attend(qq, kk, vv, valid, dist):
        s = jnp.einsum('...qhgd,...khd->...hgqk', qq, kk, preferred_element_type=jnp.float32) * scale - slopes * dist
        s = jnp.where(valid, s, -jnp.inf)
        sink = jnp.broadcast_to(sinks[:, :, None, None], s.shape[:-1] + (1,))
        p = jax.nn.softmax(jnp.concatenate([s, sink], axis=-1), axis=-1)[..., :-1].astype(vv.dtype)
        return jnp.einsum('...hgqk,...khd->...qhgd', p, vv)

    mpos = jnp.arange(M)
    mdist = (mpos[:, None] - mpos[None, :]).astype(jnp.float32)
    meta = attend(q[:, :M], k[:, :M], v[:, :M], mdist >= 0, mdist)

    def band(t):
        tr = t[:, M:].reshape(B, nb, BLOCK, Hkv, d)
        prev = jnp.concatenate([jnp.zeros_like(tr[:, :1]), tr[:, :-1]], axis=1)
        meta_t = jnp.broadcast_to(t[:, None, :M], (B, nb, M, Hkv, d))
        return jnp.concatenate([meta_t, prev, tr], axis=2)

    blk = jnp.arange(nb)
    qpos = M + blk[:, None] * BLOCK + jnp.arange(BLOCK)[None, :]
    band_pos = M + (blk[:, None] - 1) * BLOCK + jnp.arange(2 * BLOCK)[None, :]
    band_dist = qpos[:, :, None] - band_pos[:, None, :]
    band_ok = (band_pos[:, None, :] >= M) & (band_dist >= 0) & (band_dist < WINDOW)
    meta_dist = qpos[:, :, None] - mpos[None, None, :]
    valid = jnp.concatenate([jnp.ones((nb, BLOCK, M), dtype=bool), band_ok], axis=-1)
    dist = jnp.concatenate([meta_dist, band_dist], axis=-1).astype(jnp.float32)
    qr = q[:, M:].reshape(B, nb, BLOCK, Hkv, G, d)
    real = attend(qr, band(k), band(v), valid[:, None, None], dist[:, None, None])
    return jnp.concatenate([meta.reshape(B, M, Hq, d), real.reshape(B, L - M, Hq, d)], axis=1)


def mixer_block(h, norm1_g, w_in, fox_forget_b, fox_q_g, fox_k_g, mla_q_a_g, mla_w_q_up, mla_kv_a_g,
                mla_w_kv_up, mla_q_g, mla_k_g, swa_q_g, swa_k_g, swa_sinks, w_branch, w_o):
    B, L, _ = h.shape
    pos = jnp.arange(L)
    xn = rms_norm(h, norm1_g)
    proj = xn @ w_in
    offsets = [int(o) for o in np.cumsum(IN_SPLITS)[:-1]]
    fq, fk, fv, ff, cq, ckv, krope, sq, sk, sv, gates = jnp.split(proj, offsets, axis=-1)

    fq = rms_norm(fq.reshape(B, L, FOX_HEADS, FOX_HEAD_DIM), fox_q_g)
    fk = rms_norm(fk.reshape(B, L, FOX_HEADS, FOX_HEAD_DIM), fox_k_g)
    fv = fv.reshape(B, L, FOX_HEADS, FOX_HEAD_DIM)
    log_decay = jnp.cumsum(jax.nn.log_sigmoid((ff + fox_forget_b).astype(jnp.float32)), axis=1)
    out_a = causal_block_attention(fq, fk, fv, log_decay)

    q = (rms_norm(cq, mla_q_a_g) @ mla_w_q_up).reshape(B, L, MLA_HEADS, MLA_NOPE_DIM + MLA_ROPE_DIM)
    q = rms_norm(q, mla_q_g)
    q = jnp.concatenate([q[..., :MLA_NOPE_DIM], rope(q[..., MLA_NOPE_DIM:], pos)], axis=-1)
    kv = (rms_norm(ckv, mla_kv_a_g) @ mla_w_kv_up).reshape(B, L, MLA_HEADS, MLA_NOPE_DIM + MLA_V_DIM)
    k_nope, v_mla = kv[..., :MLA_NOPE_DIM], kv[..., MLA_NOPE_DIM:]
    k = jnp.concatenate([k_nope, jnp.broadcast_to(krope[:, :, None, :], (B, L, MLA_HEADS, MLA_ROPE_DIM))], axis=-1)
    k = rms_norm(k, mla_k_g)
    k = jnp.concatenate([k[..., :MLA_NOPE_DIM], rope(k[..., MLA_NOPE_DIM:], pos)], axis=-1)
    out_b = causal_block_attention(q, k, v_mla)

    sq = rms_norm(sq.reshape(B, L, SWA_Q_HEADS, SWA_HEAD_DIM), swa_q_g)
    sk = rms_norm(sk.reshape(B, L, SWA_KV_HEADS, SWA_HEAD_DIM), swa_k_g)
    sv = sv.reshape(B, L, SWA_KV_HEADS, SWA_HEAD_DIM)
    out_c = sliding_window_attention(sq, sk, sv, swa_sinks, alibi_slopes(SWA_Q_HEADS))

    branches = jnp.stack([out_a.reshape(B, L, BRANCH_WIDTH), out_b.reshape(B, L, BRANCH_WIDTH),
                          out_c.reshape(B, L, BRANCH_WIDTH)], axis=2)
    g = jax.nn.sigmoid(gates.reshape(B, L, N_BRANCH, D_MODEL))
    y = jnp.einsum('blnc,ncd->blnd', branches, w_branch)
    merged = jnp.sum(g * y, axis=2)
    return h + merged @ w_o


def conv_ffn(h, norm2_g, w_up, conv_w, conv_b, w_down):
    L = h.shape[1]
    u = rms_norm(h, norm2_g) @ w_up
    up = jnp.pad(u, ((0, 0), (CONV_WIDTH - 1, 0), (0, 0)))
    c = conv_b
    for i in range(CONV_WIDTH):
        c = c + conv_w[i] * up[:, i:i + L]
    gate, val = jnp.split(c, 2, axis=-1)
    return h + (jax.nn.silu(gate) * val) @ w_down


def _fwd_setup_inputs(seed: int = 0) -> dict:
    key = jax.random.key(seed)
    ks = jax.random.split(key, 23)
    f32 = jnp.float32
    nrm = lambda k, shape, s: jax.random.normal(k, shape, f32) * s
    gain = lambda k, shape: 1.0 + 0.1 * jax.random.normal(k, shape, f32)
    return {
        'x': nrm(ks[0], (BATCH, SEQ, D_MODEL), 1.0),
        'meta_tokens': nrm(ks[1], (N_META, D_MODEL), 1.0),
        'norm1_g': gain(ks[2], (DEPTH, D_MODEL)),
        'w_in': nrm(ks[3], (DEPTH, D_MODEL, IN_WIDTH), D_MODEL ** -0.5),
        'fox_forget_b': 3.0 + 0.1 * jax.random.normal(ks[4], (DEPTH, FOX_HEADS), f32),
        'fox_q_g': gain(ks[5], (DEPTH, FOX_HEAD_DIM)),
        'fox_k_g': gain(ks[6], (DEPTH, FOX_HEAD_DIM)),
        'mla_q_a_g': gain(ks[7], (DEPTH, MLA_Q_RANK)),
        'mla_w_q_up': nrm(ks[8], (DEPTH, MLA_Q_RANK, MLA_HEADS * (MLA_NOPE_DIM + MLA_ROPE_DIM)), MLA_Q_RANK ** -0.5),
        'mla_kv_a_g': gain(ks[9], (DEPTH, MLA_KV_RANK)),
        'mla_w_kv_up': nrm(ks[10], (DEPTH, MLA_KV_RANK, MLA_HEADS * (MLA_NOPE_DIM + MLA_V_DIM)), MLA_KV_RANK ** -0.5),
        'mla_q_g': gain(ks[11], (DEPTH, MLA_NOPE_DIM + MLA_ROPE_DIM)),
        'mla_k_g': gain(ks[12], (DEPTH, MLA_NOPE_DIM + MLA_ROPE_DIM)),
        'swa_q_g': gain(ks[13], (DEPTH, SWA_HEAD_DIM)),
        'swa_k_g': gain(ks[14], (DEPTH, SWA_HEAD_DIM)),
        'swa_sinks': nrm(ks[15], (DEPTH, SWA_Q_HEADS), 0.5),
        'w_branch': nrm(ks[16], (DEPTH, N_BRANCH, BRANCH_WIDTH, D_MODEL), BRANCH_WIDTH ** -0.5),
        'w_o': nrm(ks[17], (DEPTH, D_MODEL, D_MODEL), D_MODEL ** -0.5),
        'norm2_g': gain(ks[18], (DEPTH, D_MODEL)),
        'ffn_w_up': nrm(ks[19], (DEPTH, D_MODEL, 2 * D_FF), D_MODEL ** -0.5),
        'ffn_conv_w': nrm(ks[20], (DEPTH, CONV_WIDTH, 2 * D_FF), CONV_WIDTH ** -0.5),
        'ffn_conv_b': nrm(ks[21], (DEPTH, 2 * D_FF), 0.02),
        'ffn_w_down': nrm(ks[22], (DEPTH, D_FF, D_MODEL), D_FF ** -0.5),
    }


def _fwd_reference(x, meta_tokens, norm1_g, w_in, fox_forget_b, fox_q_g, fox_k_g, mla_q_a_g, mla_w_q_up,
              mla_kv_a_g, mla_w_kv_up, mla_q_g, mla_k_g, swa_q_g, swa_k_g, swa_sinks, w_branch, w_o,
              norm2_g, ffn_w_up, ffn_conv_w, ffn_conv_b, ffn_w_down):
    B = x.shape[0]
    meta = jnp.broadcast_to(meta_tokens[None].astype(x.dtype), (B, N_META, x.shape[-1]))
    h = jnp.concatenate([meta, x], axis=1)
    for l in range(DEPTH):
        h = mixer_block(h, norm1_g[l], w_in[l], fox_forget_b[l], fox_q_g[l], fox_k_g[l], mla_q_a_g[l],
                        mla_w_q_up[l], mla_kv_a_g[l], mla_w_kv_up[l], mla_q_g[l], mla_k_g[l], swa_q_g[l],
                        swa_k_g[l], swa_sinks[l], w_branch[l], w_o[l])
        h = conv_ffn(h, norm2_g[l], ffn_w_up[l], ffn_conv_w[l], ffn_conv_b[l], ffn_w_down[l])
    return h[:, N_META:]


import jax as _jax
import jax.numpy as _jnp

TWIN_FORMAT = 'train_step'
FWD_PARAMS = ['x', 'meta_tokens', 'norm1_g', 'w_in', 'fox_forget_b', 'fox_q_g', 'fox_k_g', 'mla_q_a_g', 'mla_w_q_up', 'mla_kv_a_g', 'mla_w_kv_up', 'mla_q_g', 'mla_k_g', 'swa_q_g', 'swa_k_g', 'swa_sinks', 'w_branch', 'w_o', 'norm2_g', 'ffn_w_up', 'ffn_conv_w', 'ffn_conv_b', 'ffn_w_down']
TWIN_WEIGHTS = ['meta_tokens', 'norm1_g', 'w_in', 'fox_forget_b', 'fox_q_g', 'fox_k_g', 'mla_q_a_g', 'mla_w_q_up', 'mla_kv_a_g', 'mla_w_kv_up', 'mla_q_g', 'mla_k_g', 'swa_q_g', 'swa_k_g', 'swa_sinks', 'w_branch', 'w_o', 'norm2_g', 'ffn_w_up', 'ffn_conv_w', 'ffn_conv_b', 'ffn_w_down']
TWIN_DIFF_INPUT = 'x'
TWIN_INPUTS = ['x', 'meta_tokens', 'norm1_g', 'w_in', 'fox_forget_b', 'fox_q_g', 'fox_k_g', 'mla_q_a_g', 'mla_w_q_up', 'mla_kv_a_g', 'mla_w_kv_up', 'mla_q_g', 'mla_k_g', 'swa_q_g', 'swa_k_g', 'swa_sinks', 'w_branch', 'w_o', 'norm2_g', 'ffn_w_up', 'ffn_conv_w', 'ffn_conv_b', 'ffn_w_down', 'loss_target', 'm_meta_tokens', 'm_norm1_g', 'm_w_in', 'm_fox_forget_b', 'm_fox_q_g', 'm_fox_k_g', 'm_mla_q_a_g', 'm_mla_w_q_up', 'm_mla_kv_a_g', 'm_mla_w_kv_up', 'm_mla_q_g', 'm_mla_k_g', 'm_swa_q_g', 'm_swa_k_g', 'm_swa_sinks', 'm_w_branch', 'm_w_o', 'm_norm2_g', 'm_ffn_w_up', 'm_ffn_conv_w', 'm_ffn_conv_b', 'm_ffn_w_down', 'v_meta_tokens', 'v_norm1_g', 'v_w_in', 'v_fox_forget_b', 'v_fox_q_g', 'v_fox_k_g', 'v_mla_q_a_g', 'v_mla_w_q_up', 'v_mla_kv_a_g', 'v_mla_w_kv_up', 'v_mla_q_g', 'v_mla_k_g', 'v_swa_q_g', 'v_swa_k_g', 'v_swa_sinks', 'v_w_branch', 'v_w_o', 'v_norm2_g', 'v_ffn_w_up', 'v_ffn_conv_w', 'v_ffn_conv_b', 'v_ffn_w_down']
TWIN_OUTPUTS = ['loss', 'grad_x', 'grad_meta_tokens', 'grad_norm1_g', 'grad_w_in', 'grad_fox_forget_b', 'grad_fox_q_g', 'grad_fox_k_g', 'grad_mla_q_a_g', 'grad_mla_w_q_up', 'grad_mla_kv_a_g', 'grad_mla_w_kv_up', 'grad_mla_q_g', 'grad_mla_k_g', 'grad_swa_q_g', 'grad_swa_k_g', 'grad_swa_sinks', 'grad_w_branch', 'grad_w_o', 'grad_norm2_g', 'grad_ffn_w_up', 'grad_ffn_conv_w', 'grad_ffn_conv_b', 'grad_ffn_w_down', 'delta_meta_tokens', 'delta_norm1_g', 'delta_w_in', 'delta_fox_forget_b', 'delta_fox_q_g', 'delta_fox_k_g', 'delta_mla_q_a_g', 'delta_mla_w_q_up', 'delta_mla_kv_a_g', 'delta_mla_w_kv_up', 'delta_mla_q_g', 'delta_mla_k_g', 'delta_swa_q_g', 'delta_swa_k_g', 'delta_swa_sinks', 'delta_w_branch', 'delta_w_o', 'delta_norm2_g', 'delta_ffn_w_up', 'delta_ffn_conv_w', 'delta_ffn_conv_b', 'delta_ffn_w_down', 'new_m_meta_tokens', 'new_m_norm1_g', 'new_m_w_in', 'new_m_fox_forget_b', 'new_m_fox_q_g', 'new_m_fox_k_g', 'new_m_mla_q_a_g', 'new_m_mla_w_q_up', 'new_m_mla_kv_a_g', 'new_m_mla_w_kv_up', 'new_m_mla_q_g', 'new_m_mla_k_g', 'new_m_swa_q_g', 'new_m_swa_k_g', 'new_m_swa_sinks', 'new_m_w_branch', 'new_m_w_o', 'new_m_norm2_g', 'new_m_ffn_w_up', 'new_m_ffn_conv_w', 'new_m_ffn_conv_b', 'new_m_ffn_w_down', 'new_v_meta_tokens', 'new_v_norm1_g', 'new_v_w_in', 'new_v_fox_forget_b', 'new_v_fox_q_g', 'new_v_fox_k_g', 'new_v_mla_q_a_g', 'new_v_mla_w_q_up', 'new_v_mla_kv_a_g', 'new_v_mla_w_kv_up', 'new_v_mla_q_g', 'new_v_mla_k_g', 'new_v_swa_q_g', 'new_v_swa_k_g', 'new_v_swa_sinks', 'new_v_w_branch', 'new_v_w_o', 'new_v_norm2_g', 'new_v_ffn_w_up', 'new_v_ffn_conv_w', 'new_v_ffn_conv_b', 'new_v_ffn_w_down']
TWIN_LEAF_KINDS = {'loss': 'loss', 'grad_x': 'grad_x', 'grad_meta_tokens': 'grad_w', 'grad_norm1_g': 'grad_w', 'grad_w_in': 'grad_w', 'grad_fox_forget_b': 'grad_w', 'grad_fox_q_g': 'grad_w', 'grad_fox_k_g': 'grad_w', 'grad_mla_q_a_g': 'grad_w', 'grad_mla_w_q_up': 'grad_w', 'grad_mla_kv_a_g': 'grad_w', 'grad_mla_w_kv_up': 'grad_w', 'grad_mla_q_g': 'grad_w', 'grad_mla_k_g': 'grad_w', 'grad_swa_q_g': 'grad_w', 'grad_swa_k_g': 'grad_w', 'grad_swa_sinks': 'grad_w', 'grad_w_branch': 'grad_w', 'grad_w_o': 'grad_w', 'grad_norm2_g': 'grad_w', 'grad_ffn_w_up': 'grad_w', 'grad_ffn_conv_w': 'grad_w', 'grad_ffn_conv_b': 'grad_w', 'grad_ffn_w_down': 'grad_w', 'delta_meta_tokens': 'delta_w', 'delta_norm1_g': 'delta_w', 'delta_w_in': 'delta_w', 'delta_fox_forget_b': 'delta_w', 'delta_fox_q_g': 'delta_w', 'delta_fox_k_g': 'delta_w', 'delta_mla_q_a_g': 'delta_w', 'delta_mla_w_q_up': 'delta_w', 'delta_mla_kv_a_g': 'delta_w', 'delta_mla_w_kv_up': 'delta_w', 'delta_mla_q_g': 'delta_w', 'delta_mla_k_g': 'delta_w', 'delta_swa_q_g': 'delta_w', 'delta_swa_k_g': 'delta_w', 'delta_swa_sinks': 'delta_w', 'delta_w_branch': 'delta_w', 'delta_w_o': 'delta_w', 'delta_norm2_g': 'delta_w', 'delta_ffn_w_up': 'delta_w', 'delta_ffn_conv_w': 'delta_w', 'delta_ffn_conv_b': 'delta_w', 'delta_ffn_w_down': 'delta_w', 'new_m_meta_tokens': 'new_m', 'new_m_norm1_g': 'new_m', 'new_m_w_in': 'new_m', 'new_m_fox_forget_b': 'new_m', 'new_m_fox_q_g': 'new_m', 'new_m_fox_k_g': 'new_m', 'new_m_mla_q_a_g': 'new_m', 'new_m_mla_w_q_up': 'new_m', 'new_m_mla_kv_a_g': 'new_m', 'new_m_mla_w_kv_up': 'new_m', 'new_m_mla_q_g': 'new_m', 'new_m_mla_k_g': 'new_m', 'new_m_swa_q_g': 'new_m', 'new_m_swa_k_g': 'new_m', 'new_m_swa_sinks': 'new_m', 'new_m_w_branch': 'new_m', 'new_m_w_o': 'new_m', 'new_m_norm2_g': 'new_m', 'new_m_ffn_w_up': 'new_m', 'new_m_ffn_conv_w': 'new_m', 'new_m_ffn_conv_b': 'new_m', 'new_m_ffn_w_down': 'new_m', 'new_v_meta_tokens': 'new_v', 'new_v_norm1_g': 'new_v', 'new_v_w_in': 'new_v', 'new_v_fox_forget_b': 'new_v', 'new_v_fox_q_g': 'new_v', 'new_v_fox_k_g': 'new_v', 'new_v_mla_q_a_g': 'new_v', 'new_v_mla_w_q_up': 'new_v', 'new_v_mla_kv_a_g': 'new_v', 'new_v_mla_w_kv_up': 'new_v', 'new_v_mla_q_g': 'new_v', 'new_v_mla_k_g': 'new_v', 'new_v_swa_q_g': 'new_v', 'new_v_swa_k_g': 'new_v', 'new_v_swa_sinks': 'new_v', 'new_v_w_branch': 'new_v', 'new_v_w_o': 'new_v', 'new_v_norm2_g': 'new_v', 'new_v_ffn_w_up': 'new_v', 'new_v_ffn_conv_w': 'new_v', 'new_v_ffn_conv_b': 'new_v', 'new_v_ffn_w_down': 'new_v'}


def _forward(args):
    return _fwd_reference(*[args[k] for k in FWD_PARAMS])


def _output_shape():
    def fwd():
        inp = _fwd_setup_inputs(0)
        return _fwd_reference(*[inp[k] for k in FWD_PARAMS])
    out = _jax.eval_shape(fwd)
    return out.shape, out.dtype

N_MICROBATCH = 1
ADAM_LR = 0.001
ADAM_B1 = 0.9
ADAM_B2 = 0.999
ADAM_EPS = 1e-08
ADAM_WD = 0.01
ADAM_STEP = 10
PER_EXAMPLE_BATCH_AXIS = {'x': 0, 'loss_target': 0}
SHARED_INPUTS = []
_WEIGHT_DTYPES = {'meta_tokens': _jnp.float32, 'norm1_g': _jnp.float32, 'w_in': _jnp.float32, 'fox_forget_b': _jnp.float32, 'fox_q_g': _jnp.float32, 'fox_k_g': _jnp.float32, 'mla_q_a_g': _jnp.float32, 'mla_w_q_up': _jnp.float32, 'mla_kv_a_g': _jnp.float32, 'mla_w_kv_up': _jnp.float32, 'mla_q_g': _jnp.float32, 'mla_k_g': _jnp.float32, 'swa_q_g': _jnp.float32, 'swa_k_g': _jnp.float32, 'swa_sinks': _jnp.float32, 'w_branch': _jnp.float32, 'w_o': _jnp.float32, 'norm2_g': _jnp.float32, 'ffn_w_up': _jnp.float32, 'ffn_conv_w': _jnp.float32, 'ffn_conv_b': _jnp.float32, 'ffn_w_down': _jnp.float32}
MOMENT_SCALE = {'meta_tokens': 5.748946e-02, 'norm1_g': 1.876743e+00, 'w_in': 1.317482e-01, 'fox_forget_b': 3.812588e+01, 'fox_q_g': 7.246283e+00, 'fox_k_g': 7.242996e+00, 'mla_q_a_g': 1.236010e-01, 'mla_w_q_up': 6.900168e-02, 'mla_kv_a_g': 9.409152e-01, 'mla_w_kv_up': 1.469727e-01, 'mla_q_g': 8.287560e-01, 'mla_k_g': 8.228594e-01, 'swa_q_g': 6.220561e+00, 'swa_k_g': 6.248779e+00, 'swa_sinks': 2.050373e+01, 'w_branch': 1.312976e-01, 'w_o': 2.241139e-01, 'norm2_g': 2.674513e+01, 'ffn_w_up': 2.946655e-01, 'ffn_conv_w': 3.758632e+00, 'ffn_conv_b': 3.402279e+00, 'ffn_w_down': 4.351883e-01}


def _to_microbatches(a, axis):
    t = _jnp.moveaxis(a, axis, 0)
    t = t.reshape((N_MICROBATCH, t.shape[0] // N_MICROBATCH) + t.shape[1:])
    return _jnp.moveaxis(t, 1, axis + 1)


def setup_inputs(seed: int = 0) -> dict:
    inp = _fwd_setup_inputs(seed)
    key = _jax.random.fold_in(_jax.random.key(seed), 7919)
    shape, _ = _output_shape()
    out = dict(inp)
    out["loss_target"] = _jax.random.normal(_jax.random.fold_in(key, 0), shape, _jnp.float32)
    for i, name in enumerate(TWIN_WEIGHTS):
        w = inp[name].astype(_jnp.float32)
        if MOMENT_SCALE is None:
            s = _jnp.sqrt(_jnp.mean(_jnp.square(w)) + 1e-30)
        else:
            s = MOMENT_SCALE[name]
        km, kv = _jax.random.split(_jax.random.fold_in(key, i + 1))
        out[name] = w
        out["m_" + name] = s * _jax.random.normal(km, w.shape, _jnp.float32)
        out["v_" + name] = (s * s) * _jax.random.uniform(kv, w.shape, _jnp.float32, 0.5, 1.5)
    if N_MICROBATCH > 1:
        for name, axis in PER_EXAMPLE_BATCH_AXIS.items():
            out[name] = _to_microbatches(out[name], axis)
    return {'x': out['x'], 'meta_tokens': out['meta_tokens'], 'norm1_g': out['norm1_g'], 'w_in': out['w_in'], 'fox_forget_b': out['fox_forget_b'], 'fox_q_g': out['fox_q_g'], 'fox_k_g': out['fox_k_g'], 'mla_q_a_g': out['mla_q_a_g'], 'mla_w_q_up': out['mla_w_q_up'], 'mla_kv_a_g': out['mla_kv_a_g'], 'mla_w_kv_up': out['mla_w_kv_up'], 'mla_q_g': out['mla_q_g'], 'mla_k_g': out['mla_k_g'], 'swa_q_g': out['swa_q_g'], 'swa_k_g': out['swa_k_g'], 'swa_sinks': out['swa_sinks'], 'w_branch': out['w_branch'], 'w_o': out['w_o'], 'norm2_g': out['norm2_g'], 'ffn_w_up': out['ffn_w_up'], 'ffn_conv_w': out['ffn_conv_w'], 'ffn_conv_b': out['ffn_conv_b'], 'ffn_w_down': out['ffn_w_down'], 'loss_target': out['loss_target'], 'm_meta_tokens': out['m_meta_tokens'], 'm_norm1_g': out['m_norm1_g'], 'm_w_in': out['m_w_in'], 'm_fox_forget_b': out['m_fox_forget_b'], 'm_fox_q_g': out['m_fox_q_g'], 'm_fox_k_g': out['m_fox_k_g'], 'm_mla_q_a_g': out['m_mla_q_a_g'], 'm_mla_w_q_up': out['m_mla_w_q_up'], 'm_mla_kv_a_g': out['m_mla_kv_a_g'], 'm_mla_w_kv_up': out['m_mla_w_kv_up'], 'm_mla_q_g': out['m_mla_q_g'], 'm_mla_k_g': out['m_mla_k_g'], 'm_swa_q_g': out['m_swa_q_g'], 'm_swa_k_g': out['m_swa_k_g'], 'm_swa_sinks': out['m_swa_sinks'], 'm_w_branch': out['m_w_branch'], 'm_w_o': out['m_w_o'], 'm_norm2_g': out['m_norm2_g'], 'm_ffn_w_up': out['m_ffn_w_up'], 'm_ffn_conv_w': out['m_ffn_conv_w'], 'm_ffn_conv_b': out['m_ffn_conv_b'], 'm_ffn_w_down': out['m_ffn_w_down'], 'v_meta_tokens': out['v_meta_tokens'], 'v_norm1_g': out['v_norm1_g'], 'v_w_in': out['v_w_in'], 'v_fox_forget_b': out['v_fox_forget_b'], 'v_fox_q_g': out['v_fox_q_g'], 'v_fox_k_g': out['v_fox_k_g'], 'v_mla_q_a_g': out['v_mla_q_a_g'], 'v_mla_w_q_up': out['v_mla_w_q_up'], 'v_mla_kv_a_g': out['v_mla_kv_a_g'], 'v_mla_w_kv_up': out['v_mla_w_kv_up'], 'v_mla_q_g': out['v_mla_q_g'], 'v_mla_k_g': out['v_mla_k_g'], 'v_swa_q_g': out['v_swa_q_g'], 'v_swa_k_g': out['v_swa_k_g'], 'v_swa_sinks': out['v_swa_sinks'], 'v_w_branch': out['v_w_branch'], 'v_w_o': out['v_w_o'], 'v_norm2_g': out['v_norm2_g'], 'v_ffn_w_up': out['v_ffn_w_up'], 'v_ffn_conv_w': out['v_ffn_conv_w'], 'v_ffn_conv_b': out['v_ffn_conv_b'], 'v_ffn_w_down': out['v_ffn_w_down']}


def _loss(weights, diff, rest, loss_target):
    with _jax.named_scope("forward"):
        args = {**rest, TWIN_DIFF_INPUT: diff, **{k: w.astype(_WEIGHT_DTYPES[k]) for k, w in weights.items()}}
        y = _forward(args)
    with _jax.named_scope("loss_head"):
        err = _jnp.square(y.astype(_jnp.float32) - loss_target)
        return 0.5 * _jnp.sum(_jnp.mean(err, axis=-1)) if err.ndim else 0.5 * err


def _adamw(w, g, m, v):
    m = ADAM_B1 * m + (1.0 - ADAM_B1) * g
    v = ADAM_B2 * v + (1.0 - ADAM_B2) * _jnp.square(g)
    m_hat = m / (1.0 - ADAM_B1 ** ADAM_STEP)
    v_hat = v / (1.0 - ADAM_B2 ** ADAM_STEP)
    delta = -ADAM_LR * (m_hat / (_jnp.sqrt(v_hat) + ADAM_EPS) + ADAM_WD * w)
    return delta, m, v


def reference(x, meta_tokens, norm1_g, w_in, fox_forget_b, fox_q_g, fox_k_g, mla_q_a_g, mla_w_q_up, mla_kv_a_g, mla_w_kv_up, mla_q_g, mla_k_g, swa_q_g, swa_k_g, swa_sinks, w_branch, w_o, norm2_g, ffn_w_up, ffn_conv_w, ffn_conv_b, ffn_w_down, loss_target, m_meta_tokens, m_norm1_g, m_w_in, m_fox_forget_b, m_fox_q_g, m_fox_k_g, m_mla_q_a_g, m_mla_w_q_up, m_mla_kv_a_g, m_mla_w_kv_up, m_mla_q_g, m_mla_k_g, m_swa_q_g, m_swa_k_g, m_swa_sinks, m_w_branch, m_w_o, m_norm2_g, m_ffn_w_up, m_ffn_conv_w, m_ffn_conv_b, m_ffn_w_down, v_meta_tokens, v_norm1_g, v_w_in, v_fox_forget_b, v_fox_q_g, v_fox_k_g, v_mla_q_a_g, v_mla_w_q_up, v_mla_kv_a_g, v_mla_w_kv_up, v_mla_q_g, v_mla_k_g, v_swa_q_g, v_swa_k_g, v_swa_sinks, v_w_branch, v_w_o, v_norm2_g, v_ffn_w_up, v_ffn_conv_w, v_ffn_conv_b, v_ffn_w_down):
    given = dict(x=x, meta_tokens=meta_tokens, norm1_g=norm1_g, w_in=w_in, fox_forget_b=fox_forget_b, fox_q_g=fox_q_g, fox_k_g=fox_k_g, mla_q_a_g=mla_q_a_g, mla_w_q_up=mla_w_q_up, mla_kv_a_g=mla_kv_a_g, mla_w_kv_up=mla_w_kv_up, mla_q_g=mla_q_g, mla_k_g=mla_k_g, swa_q_g=swa_q_g, swa_k_g=swa_k_g, swa_sinks=swa_sinks, w_branch=w_branch, w_o=w_o, norm2_g=norm2_g, ffn_w_up=ffn_w_up, ffn_conv_w=ffn_conv_w, ffn_conv_b=ffn_conv_b, ffn_w_down=ffn_w_down, loss_target=loss_target, m_meta_tokens=m_meta_tokens, m_norm1_g=m_norm1_g, m_w_in=m_w_in, m_fox_forget_b=m_fox_forget_b, m_fox_q_g=m_fox_q_g, m_fox_k_g=m_fox_k_g, m_mla_q_a_g=m_mla_q_a_g, m_mla_w_q_up=m_mla_w_q_up, m_mla_kv_a_g=m_mla_kv_a_g, m_mla_w_kv_up=m_mla_w_kv_up, m_mla_q_g=m_mla_q_g, m_mla_k_g=m_mla_k_g, m_swa_q_g=m_swa_q_g, m_swa_k_g=m_swa_k_g, m_swa_sinks=m_swa_sinks, m_w_branch=m_w_branch, m_w_o=m_w_o, m_norm2_g=m_norm2_g, m_ffn_w_up=m_ffn_w_up, m_ffn_conv_w=m_ffn_conv_w, m_ffn_conv_b=m_ffn_conv_b, m_ffn_w_down=m_ffn_w_down, v_meta_tokens=v_meta_tokens, v_norm1_g=v_norm1_g, v_w_in=v_w_in, v_fox_forget_b=v_fox_forget_b, v_fox_q_g=v_fox_q_g, v_fox_k_g=v_fox_k_g, v_mla_q_a_g=v_mla_q_a_g, v_mla_w_q_up=v_mla_w_q_up, v_mla_kv_a_g=v_mla_kv_a_g, v_mla_w_kv_up=v_mla_w_kv_up, v_mla_q_g=v_mla_q_g, v_mla_k_g=v_mla_k_g, v_swa_q_g=v_swa_q_g, v_swa_k_g=v_swa_k_g, v_swa_sinks=v_swa_sinks, v_w_branch=v_w_branch, v_w_o=v_w_o, v_norm2_g=v_norm2_g, v_ffn_w_up=v_ffn_w_up, v_ffn_conv_w=v_ffn_conv_w, v_ffn_conv_b=v_ffn_conv_b, v_ffn_w_down=v_ffn_w_down)
    weights = {n: given[n] for n in TWIN_WEIGHTS}
    shared = {n: given[n] for n in SHARED_INPUTS}
    per_example = {n: given[n] for n in ['x']}
    grad_fn = _jax.value_and_grad(_loss, argnums=(0, 1))

    def one_microbatch(ex, loss_target):
        ex = dict(ex)
        diff = ex.pop(TWIN_DIFF_INPUT)
        return grad_fn(weights, diff, {**shared, **ex}, loss_target)

    if N_MICROBATCH == 1:
        loss, (grad_w, grad_x) = one_microbatch(per_example, given["loss_target"])
    else:
        def body(carry, xs):
            loss_sum, grad_sum = carry
            l_k, (gw_k, gx_k) = one_microbatch(xs[0], xs[1])
            with _jax.named_scope("update"):
                return (loss_sum + l_k, _jax.tree.map(_jnp.add, grad_sum, gw_k)), gx_k

        init = (_jnp.zeros((), _jnp.float32), _jax.tree.map(_jnp.zeros_like, weights))
        (loss, grad_w), grad_x = _jax.lax.scan(body, init, (per_example, given["loss_target"]))
    with _jax.named_scope("update"):
        delta_w, new_m, new_v = {}, {}, {}
        for n in TWIN_WEIGHTS:
            delta_w[n], new_m[n], new_v[n] = _adamw(weights[n], grad_w[n], given["m_" + n], given["v_" + n])
    return (loss, grad_x, *[grad_w[n] for n in TWIN_WEIGHTS], *[delta_w[n] for n in TWIN_WEIGHTS],
            *[new_m[n] for n in TWIN_WEIGHTS], *[new_v[n] for n in TWIN_WEIGHTS])
```

```python
import functools

import jax
import jax.numpy as jnp
from jax import lax
from jax.experimental import pallas as pl
from jax.experimental.pallas import tpu as pltpu

F32 = jnp.float32
BF16 = jnp.bfloat16
SDS = jax.ShapeDtypeStruct
BS = pl.BlockSpec

D_MODEL = 1024
DEPTH = 2
N_META = 16
EPS = 1e-6
HEADS = 8
HEAD_DIM = 64
MLA_Q_RANK = 256
MLA_KV_RANK = 128
MLA_NOPE = 64
MLA_ROPE = 32
MLA_QK = MLA_NOPE + MLA_ROPE
ROPE_THETA = 10000.0
SWA_KV_HEADS = 2
WINDOW = 128
N_BRANCH = 3
BRANCH_WIDTH = 512
D_FF = 2816
IN_WIDTH = 5800
IN_PAD = 6144
N_CHIPS = 4
N_DEV = 8

ADAM_LR = 0.001
ADAM_B1 = 0.9
ADAM_B2 = 0.999
ADAM_EPS = 1e-08
ADAM_WD = 0.01
ADAM_STEP = 10

LANES = 128
SUBLANES = 8
TQ = 128
TK = 128
NEG = -1e30
VMEM_LIMIT = 56 * 1024 * 1024

O_FQ, O_FK, O_FV, O_FF = 0, 512, 1024, 1536
O_CQ, O_CKV, O_KR = 1664, 1920, 2048
O_SQ, O_SK, O_SV, O_G = 2176, 2688, 2816, 2944

SHARDED = ("meta_tokens", "w_in", "mla_w_q_up", "mla_w_kv_up", "w_branch", "w_o", "ffn_w_up", "ffn_conv_w",
           "ffn_w_down")
SHARD_AXIS = {"meta_tokens": 1, "w_in": 2, "mla_w_q_up": 2, "mla_w_kv_up": 2, "w_branch": 3, "w_o": 1,
              "ffn_w_up": 2, "ffn_conv_w": 2, "ffn_w_down": 1}
SHARD_SHAPE = {"meta_tokens": (16, 256), "w_in": (2, 1024, 1450), "mla_w_q_up": (2, 256, 192),
               "mla_w_kv_up": (2, 128, 256), "w_branch": (2, 3, 512, 256), "w_o": (2, 256, 1024),
               "ffn_w_up": (2, 1024, 1408), "ffn_conv_w": (2, 3, 1408), "ffn_w_down": (2, 704, 1024)}
BIG = ("w_in", "mla_w_q_up", "mla_w_kv_up", "w_branch", "w_o", "ffn_w_up", "ffn_w_down")
FINE = ("meta_tokens", "ffn_conv_w")
SMALL = ("norm1_g", "fox_forget_b", "fox_q_g", "fox_k_g", "mla_q_a_g", "mla_kv_a_g", "mla_q_g", "mla_k_g",
         "swa_q_g", "swa_k_g", "swa_sinks", "norm2_g", "ffn_conv_b")
WEIGHTS = ("meta_tokens", "norm1_g", "w_in", "fox_forget_b", "fox_q_g", "fox_k_g", "mla_q_a_g", "mla_w_q_up",
           "mla_kv_a_g", "mla_w_kv_up", "mla_q_g", "mla_k_g", "swa_q_g", "swa_k_g", "swa_sinks", "w_branch", "w_o",
           "norm2_g", "ffn_w_up", "ffn_conv_w", "ffn_conv_b", "ffn_w_down")
PACK_ROWS = 1024


def _cparams(sem):
    return pltpu.CompilerParams(dimension_semantics=sem, vmem_limit_bytes=VMEM_LIMIT)


def _div_tile(n, cap, mult=SUBLANES):
    best = None
    for t in range(mult, min(n, cap) + 1, mult):
        if n % t == 0:
            best = t
    return best if best is not None else n


def _rows_tile(n, width, budget=2 << 20):
    return _div_tile(n, max(SUBLANES, budget // (4 * max(width, LANES))))


def _op(fwd, bwd):
    @jax.custom_vjp
    def op(*args):
        return fwd(*args)[0]
    op.defvjp(fwd, bwd)
    return op


def _rms_fwd_call(x, g, denom, name):
    n, c = x.shape
    tr = _rows_tile(n, c)

    def body(x_ref, g_ref, y_ref):
        xv = x_ref[...]
        ms = jnp.sum(xv * xv, axis=-1, keepdims=True) * (1.0 / denom)
        y_ref[...] = xv * lax.rsqrt(ms + EPS) * g_ref[...]

    return pl.pallas_call(
        body, name=name, grid=(n // tr,),
        in_specs=[BS((tr, c), lambda i: (i, 0)), BS((1, c), lambda i: (0, 0))],
        out_specs=BS((tr, c), lambda i: (i, 0)), out_shape=SDS((n, c), F32),
        compiler_params=_cparams(("parallel",)))(x, g)


def _rms_bwd_call(x, g, dy, denom, name):
    n, c = x.shape
    tr = _rows_tile(n, c)

    def body(x_ref, g_ref, dy_ref, dx_ref, dg_ref):
        xv = x_ref[...]
        dy = dy_ref[...]
        ms = jnp.sum(xv * xv, axis=-1, keepdims=True) * (1.0 / denom)
        r = lax.rsqrt(ms + EPS)
        xh = xv * r
        dxh = dy * g_ref[...]
        dx_ref[...] = r * (dxh - xh * (jnp.sum(dxh * xh, axis=-1, keepdims=True) * (1.0 / denom)))

        @pl.when(pl.program_id(0) == 0)
        def _():
            dg_ref[...] = jnp.zeros_like(dg_ref)

        dg_ref[...] += jnp.sum(dy * xh, axis=0, keepdims=True)

    return pl.pallas_call(
        body, name=name, grid=(n // tr,),
        in_specs=[BS((tr, c), lambda i: (i, 0)), BS((1, c), lambda i: (0, 0)), BS((tr, c), lambda i: (i, 0))],
        out_specs=[BS((tr, c), lambda i: (i, 0)), BS((1, c), lambda i: (0, 0))],
        out_shape=[SDS((n, c), F32), SDS((1, c), F32)],
        compiler_params=_cparams(("arbitrary",)))(x, g, dy)


def rms_norm(x, g, denom, name):
    def fwd(x, g):
        return _rms_fwd_call(x, g, denom, name + "_f"), (x, g)

    def bwd(res, dy):
        return tuple(_rms_bwd_call(res[0], res[1], dy, denom, name + "_b"))

    return _op(fwd, bwd)(x, g)


def _mm_call(a, b, mode, res, name):
    if mode == "nn":
        (m, kc), n = a.shape, b.shape[1]
    elif mode == "nt":
        (m, kc), n = a.shape, b.shape[0]
    else:
        (kc, m), n = a.shape, b.shape[1]
    if mode == "tn":
        tk = _div_tile(kc, 1056)
        tm = _div_tile(m, 512, LANES)
        tn = _div_tile(n, 1024, LANES)
    else:
        tk = kc if kc <= 2816 else _div_tile(kc, 1024, LANES)
        tm = _div_tile(m, max(LANES, (9 << 19) // (4 * tk)))
        tn = _div_tile(n, 512, LANES)
    nk = kc // tk
    dims = {"nn": (((1,), (0,)), ((), ())), "nt": (((1,), (1,)), ((), ())), "tn": (((0,), (0,)), ((), ()))}[mode]

    def body(*refs):
        if res is None:
            a_ref, b_ref, o_ref, acc_ref = refs
            r_ref = None
        else:
            a_ref, b_ref, r_ref, o_ref, acc_ref = refs
        k = pl.program_id(2)

        @pl.when(k == 0)
        def _():
            acc_ref[...] = jnp.zeros_like(acc_ref)

        acc_ref[...] += lax.dot_general(a_ref[...].astype(BF16), b_ref[...].astype(BF16), dims,
                                        preferred_element_type=F32)

        @pl.when(k == nk - 1)
        def _():
            if r_ref is None:
                o_ref[...] = acc_ref[...]
            else:
                o_ref[...] = r_ref[...] + acc_ref[...]

    a_spec = BS((tk, tm), lambda i, j, k: (k, i)) if mode == "tn" else BS((tm, tk), lambda i, j, k: (i, k))
    b_spec = BS((tn, tk), lambda i, j, k: (j, k)) if mode == "nt" else BS((tk, tn), lambda i, j, k: (k, j))
    o_spec = BS((tm, tn), lambda i, j, k: (i, j))
    ins, args = [a_spec, b_spec], [a, b]
    if res is not None:
        ins.append(o_spec)
        args.append(res)
    return pl.pallas_call(
        body, name=name, grid=(m // tm, n // tn, nk), in_specs=ins, out_specs=o_spec,
        out_shape=SDS((m, n), F32), scratch_shapes=[pltpu.VMEM((tm, tn), F32)],
        compiler_params=_cparams(("parallel", "parallel", "arbitrary")))(*args)


def linear(a, w, eps, name, res=None):
    if res is None:
        def fwd(a, w, eps):
            return _mm_call(a, w, "nn", None, name + "_f"), (a, w)

        def bwd(r, dc):
            a, w = r
            return (_mm_call(dc, w, "nt", None, name + "_da"), jnp.zeros_like(w),
                    _mm_call(a, dc, "tn", None, name + "_dw"))

        return _op(fwd, bwd)(a, w, eps)

    def fwd_r(a, w, eps, res):
        return _mm_call(a, w, "nn", res, name + "_f"), (a, w)

    def bwd_r(r, dc):
        a, w = r
        return (_mm_call(dc, w, "nt", None, name + "_da"), jnp.zeros_like(w),
                _mm_call(a, dc, "tn", None, name + "_dw"), dc)

    return _op(fwd_r, bwd_r)(a, w, eps, res)


CT = 128


def _tri_dot(v, upper):
    r = lax.broadcasted_iota(jnp.int32, (CT, CT), 0)
    c = lax.broadcasted_iota(jnp.int32, (CT, CT), 1)
    tri = jnp.where((r <= c) if upper else (r >= c), 1.0, 0.0).astype(F32)
    return jnp.dot(v, tri, preferred_element_type=F32, precision=lax.Precision.HIGHEST)


def _gate_fwd_call(z, b, name):
    h, n = z.shape

    def body(z_ref, b_ref, c_ref, carry):
        @pl.when(pl.program_id(0) == 0)
        def _():
            carry[...] = jnp.zeros_like(carry)

        x = z_ref[...] + b_ref[...]
        ls = jnp.minimum(x, 0.0) - jnp.log(1.0 + jnp.exp(-jnp.abs(x)))
        c_ref[...] = _tri_dot(ls, True) + carry[...]
        carry[...] += jnp.sum(ls, axis=1, keepdims=True)

    return pl.pallas_call(
        body, name=name, grid=(n // CT,),
        in_specs=[BS((h, CT), lambda j: (0, j)), BS((h, 1), lambda j: (0, 0))],
        out_specs=BS((h, CT), lambda j: (0, j)), out_shape=SDS((h, n), F32),
        scratch_shapes=[pltpu.VMEM((h, 1), F32)],
        compiler_params=_cparams(("arbitrary",)))(z, b)


def _gate_bwd_call(z, b, dc, name):
    h, n = z.shape
    nt = n // CT

    def body(z_ref, b_ref, dc_ref, dz_ref, db_ref, carry):
        @pl.when(pl.program_id(0) == 0)
        def _():
            carry[...] = jnp.zeros_like(carry)
            db_ref[...] = jnp.zeros_like(db_ref)

        dcv = dc_ref[...]
        dls = _tri_dot(dcv, False) + carry[...]
        carry[...] += jnp.sum(dcv, axis=1, keepdims=True)
        x = z_ref[...] + b_ref[...]
        e = jnp.exp(-jnp.abs(x))
        dz = dls * jnp.where(x >= 0, e / (1.0 + e), 1.0 / (1.0 + e))
        dz_ref[...] = dz
        db_ref[...] += jnp.sum(dz, axis=1, keepdims=True)

    rev = lambda j: (0, nt - 1 - j)
    return pl.pallas_call(
        body, name=name, grid=(nt,),
        in_specs=[BS((h, CT), rev), BS((h, 1), lambda j: (0, 0)), BS((h, CT), rev)],
        out_specs=[BS((h, CT), rev), BS((h, 1), lambda j: (0, 0))],
        out_shape=[SDS((h, n), F32), SDS((h, 1), F32)],
        scratch_shapes=[pltpu.VMEM((h, 1), F32)],
        compiler_params=_cparams(("arbitrary",)))(z, b, dc)


def forget_cumsum(z, b, name):
    def fwd(z, b):
        return _gate_fwd_call(z, b, name + "_f"), (z, b)

    def bwd(res, dc):
        return tuple(_gate_bwd_call(res[0], res[1], dc, name + "_b"))

    return _op(fwd, bwd)(z, b)


def _rope_call(x, cos, sin, name):
    h, n, d = x.shape
    tr = _rows_tile(n, d)

    def body(x_ref, c_ref, s_ref, y_ref):
        xv = x_ref[0]
        lane = lax.broadcasted_iota(jnp.int32, xv.shape, 1)
        rot = jnp.where(lane < MLA_NOPE + MLA_ROPE // 2, -pltpu.roll(xv, d - MLA_ROPE // 2, 1),
                        pltpu.roll(xv, MLA_ROPE // 2, 1))
        y_ref[0] = xv * c_ref[...] + rot * s_ref[...]

    return pl.pallas_call(
        body, name=name, grid=(n // tr, h),
        in_specs=[BS((1, tr, d), lambda i, hh: (hh, i, 0)), BS((tr, d), lambda i, hh: (i, 0)),
                  BS((tr, d), lambda i, hh: (i, 0))],
        out_specs=BS((1, tr, d), lambda i, hh: (hh, i, 0)), out_shape=SDS((h, n, d), F32),
        compiler_params=_cparams(("parallel", "parallel")))(x, cos, sin)


def rope(x, cos, sin, name):
    def fwd(x, cos, sin):
        return _rope_call(x, cos, sin, name + "_f"), (cos, sin)

    def bwd(res, dy):
        cos, sin = res
        return _rope_call(dy, cos, -sin, name + "_b"), jnp.zeros_like(cos), jnp.zeros_like(sin)

    return _op(fwd, bwd)(x, cos, sin)


NT_DIMS = (((1,), (1,)), ((), ()))
TN_DIMS = (((0,), (0,)), ((), ()))


def _attn_fwd_call(q, k, v, cq_c, ck_r, sinks, slopes, kind, scale, name):
    h, n, dk = q.shape
    dv = v.shape[2]
    nq = n // TQ
    fox, swa = kind == "fox", kind == "swa"

    def body(*refs):
        it = iter(refs)
        q_ref, k_ref, v_ref = next(it), next(it), next(it)
        cq_ref = ck_ref = sink_ref = slope_ref = None
        if fox:
            cq_ref, ck_ref = next(it), next(it)
        if swa:
            sink_ref, slope_ref = next(it), next(it)
        o_ref, lse_ref, m_scr, l_scr, acc_scr = next(it), next(it), next(it), next(it), next(it)
        hh = pl.program_id(0)
        qi = pl.program_id(1)
        qb = q_ref[0].astype(BF16)
        if swa:
            m_scr[...] = jnp.full((TQ, 1), sink_ref[hh], F32)
            l_scr[...] = jnp.ones((TQ, 1), F32)
        else:
            m_scr[...] = jnp.full((TQ, 1), NEG, F32)
            l_scr[...] = jnp.zeros((TQ, 1), F32)
        acc_scr[...] = jnp.zeros_like(acc_scr)
        rows = qi * TQ + lax.broadcasted_iota(jnp.int32, (TQ, TK), 0)

        def process(j):
            off = pl.multiple_of(j * TK, TK)
            kb = k_ref[0, pl.ds(off, TK), :].astype(BF16)
            vb = v_ref[0, pl.ds(off, TK), :].astype(BF16)
            s = lax.dot_general(qb, kb, NT_DIMS, preferred_element_type=F32) * scale
            cols = j * TK + lax.broadcasted_iota(jnp.int32, (TQ, TK), 1)
            dist = rows - cols
            if fox:
                s = s + cq_ref[0] - ck_ref[0, j]
            if swa:
                s = s - slope_ref[hh] * dist.astype(F32)
                valid = (dist >= 0) & ((cols < N_META) | (dist < WINDOW))
            else:
                valid = dist >= 0
            s = jnp.where(valid, s, NEG)
            m_old = m_scr[...]
            m_new = jnp.maximum(m_old, jnp.max(s, axis=1, keepdims=True))
            alpha = jnp.exp(m_old - m_new)
            p = jnp.exp(s - m_new)
            l_scr[...] = alpha * l_scr[...] + jnp.sum(p, axis=1, keepdims=True)
            acc_scr[...] = alpha * acc_scr[...] + jnp.dot(p.astype(BF16), vb, preferred_element_type=F32)
            m_scr[...] = m_new

        if swa:
            process(0)
            pl.when(qi >= 2)(lambda: process(qi - 1))
            pl.when(qi >= 1)(lambda: process(qi))
        else:
            def step(j, carry):
                process(j)
                return carry
            lax.fori_loop(0, qi + 1, step, 0)
        l = l_scr[...]
        o_ref[0] = acc_scr[...] / l
        lse_ref[0] = m_scr[...] + jnp.log(l)

    ins = [BS((1, TQ, dk), lambda a, b: (a, b, 0)), BS((1, n, dk), lambda a, b: (a, 0, 0)),
           BS((1, n, dv), lambda a, b: (a, 0, 0))]
    args = [q, k, v]
    if fox:
        ins += [BS((1, TQ, 1), lambda a, b: (a, b, 0)), BS((1, nq, 1, TK), lambda a, b: (a, 0, 0, 0))]
        args += [cq_c, ck_r]
    if swa:
        ins += [BS(memory_space=pltpu.SMEM), BS(memory_space=pltpu.SMEM)]
        args += [sinks, slopes]
    return pl.pallas_call(
        body, name=name, grid=(h, nq), in_specs=ins,
        out_specs=[BS((1, TQ, dv), lambda a, b: (a, b, 0)), BS((1, TQ, 1), lambda a, b: (a, b, 0))],
        out_shape=[SDS((h, n, dv), F32), SDS((h, n, 1), F32)],
        scratch_shapes=[pltpu.VMEM((TQ, 1), F32), pltpu.VMEM((TQ, 1), F32), pltpu.VMEM((TQ, dv), F32)],
        compiler_params=_cparams(("parallel", "arbitrary")))(*args)


def _attn_bwd_call(q, k, v, do, lse_r, delta_r, cq_r, ck_c, sinks, slopes, kind, scale, name):
    h, n, dk = q.shape
    dv = v.shape[2]
    nq = n // TQ
    nk = n // TK
    fox, swa = kind == "fox", kind == "swa"

    def body(*refs):
        it = iter(refs)
        q_ref, k_ref, v_ref, do_ref, lse_ref, delta_ref = (next(it) for _ in range(6))
        cq_ref = ck_ref = sink_ref = slope_ref = dcq_ref = dck_ref = dsink_ref = dck_scr = None
        if fox:
            cq_ref, ck_ref = next(it), next(it)
        if swa:
            sink_ref, slope_ref = next(it), next(it)
        dq_ref, dk_ref, dv_ref = next(it), next(it), next(it)
        if fox:
            dcq_ref, dck_ref = next(it), next(it)
        if swa:
            dsink_ref = next(it)
        dk_scr, dv_scr = next(it), next(it)
        if fox:
            dck_scr = next(it)
        hh = pl.program_id(0)
        kj = pl.program_id(1)

        @pl.when(kj == 0)
        def _():
            dq_ref[...] = jnp.zeros_like(dq_ref)
            if fox:
                dcq_ref[...] = jnp.zeros_like(dcq_ref)
            if swa:
                dsink_ref[...] = jnp.zeros_like(dsink_ref)

        kb = k_ref[0].astype(BF16)
        vb = v_ref[0].astype(BF16)
        dk_scr[...] = jnp.zeros_like(dk_scr)
        dv_scr[...] = jnp.zeros_like(dv_scr)
        if fox:
            dck_scr[...] = jnp.zeros_like(dck_scr)
        krows = kj * TK + lax.broadcasted_iota(jnp.int32, (TK, TQ), 0)

        def step(qi, carry):
            off = pl.multiple_of(qi * TQ, TQ)
            qb = q_ref[0, pl.ds(off, TQ), :].astype(BF16)
            dob = do_ref[0, pl.ds(off, TQ), :].astype(BF16)
            st = lax.dot_general(kb, qb, NT_DIMS, preferred_element_type=F32) * scale
            qcols = qi * TQ + lax.broadcasted_iota(jnp.int32, (TK, TQ), 1)
            dist = qcols - krows
            if fox:
                st = st + cq_ref[0, qi] - ck_ref[0]
            if swa:
                st = st - slope_ref[hh] * dist.astype(F32)
                valid = (dist >= 0) & ((krows < N_META) | (dist < WINDOW))
            else:
                valid = dist >= 0
            lse = lse_ref[0, qi]
            delta = delta_ref[0, qi]
            pt = jnp.where(valid, jnp.exp(st - lse), 0.0)
            dv_scr[...] += jnp.dot(pt.astype(BF16), dob, preferred_element_type=F32)
            dpt = lax.dot_general(vb, dob, NT_DIMS, preferred_element_type=F32)
            dst = pt * (dpt - delta)
            if fox:
                dcq_ref[0, qi] += jnp.sum(dst, axis=0, keepdims=True)
                dck_scr[...] -= jnp.sum(dst, axis=1, keepdims=True)
            dsb = (dst * scale).astype(BF16)
            dk_scr[...] += jnp.dot(dsb, qb, preferred_element_type=F32)
            dq_ref[0, pl.ds(off, TQ), :] += lax.dot_general(dsb, kb, TN_DIMS, preferred_element_type=F32)
            if swa:
                contrib = -jnp.sum(jnp.exp(sink_ref[hh] - lse) * delta)
                dsink_ref[...] += jnp.where(kj == 0, contrib, 0.0)
            return carry

        hi = jnp.where(kj == 0, nq, jnp.minimum(kj + 2, nq)) if swa else nq
        lax.fori_loop(kj, hi, step, 0)
        dk_ref[0] = dk_scr[...]
        dv_ref[0] = dv_scr[...]
        if fox:
            dck_ref[0] = dck_scr[...]

    whole = lambda a, b: (a, 0, 0)
    tile = lambda a, b: (a, b, 0)
    rowv = lambda a, b: (a, 0, 0, 0)
    ins = [BS((1, n, dk), whole), BS((1, TK, dk), tile), BS((1, TK, dv), tile), BS((1, n, dv), whole),
           BS((1, nq, 1, TQ), rowv), BS((1, nq, 1, TQ), rowv)]
    args = [q, k, v, do, lse_r, delta_r]
    outs = [BS((1, n, dk), whole), BS((1, TK, dk), tile), BS((1, TK, dv), tile)]
    oshape = [SDS((h, n, dk), F32), SDS((h, n, dk), F32), SDS((h, n, dv), F32)]
    scratch = [pltpu.VMEM((TK, dk), F32), pltpu.VMEM((TK, dv), F32)]
    if fox:
        ins += [BS((1, nq, 1, TQ), rowv), BS((1, TK, 1), tile)]
        args += [cq_r, ck_c]
        outs += [BS((1, nq, 1, TQ), rowv), BS((1, TK, 1), tile)]
        oshape += [SDS((h, nq, 1, TQ), F32), SDS((h, n, 1), F32)]
        scratch += [pltpu.VMEM((TK, 1), F32)]
    if swa:
        ins += [BS(memory_space=pltpu.SMEM), BS(memory_space=pltpu.SMEM)]
        args += [sinks, slopes]
        outs += [BS((1, 1, LANES), whole)]
        oshape += [SDS((h, 1, LANES), F32)]
    return pl.pallas_call(
        body, name=name, grid=(h, nk), in_specs=ins, out_specs=outs, out_shape=oshape, scratch_shapes=scratch,
        compiler_params=_cparams(("arbitrary", "arbitrary")))(*args)


def _rowdot_call(a, b, name):
    n, d = a.shape
    tr = _rows_tile(n, d)

    def body(a_ref, b_ref, o_ref):
        o_ref[...] = jnp.sum(a_ref[...] * b_ref[...], axis=-1, keepdims=True)

    return pl.pallas_call(
        body, name=name, grid=(n // tr,),
        in_specs=[BS((tr, d), lambda i: (i, 0)), BS((tr, d), lambda i: (i, 0))],
        out_specs=BS((tr, 1), lambda i: (i, 0)), out_shape=SDS((n, 1), F32),
        compiler_params=_cparams(("parallel",)))(a, b)


def attention(q, k, v, c, sinks, slopes, kind, scale, name):
    h, n, _ = q.shape
    nq = n // TQ
    dv = v.shape[2]

    def run_fwd(q, k, v, c, sinks, slopes):
        cq_c = c.reshape(h, n, 1) if kind == "fox" else None
        ck_r = c.reshape(h, nq, 1, TK) if kind == "fox" else None
        o, lse = _attn_fwd_call(q, k, v, cq_c, ck_r, sinks, slopes, kind, scale, name + "_f")
        return o, (q, k, v, c, sinks, slopes, o, lse)

    def run_bwd(res, do):
        q, k, v, c, sinks, slopes, o, lse = res
        delta = _rowdot_call(do.reshape(h * n, dv), o.reshape(h * n, dv), name + "_dl")
        cq_r = c.reshape(h, nq, 1, TQ) if kind == "fox" else None
        ck_c = c.reshape(h, n, 1) if kind == "fox" else None
        outs = _attn_bwd_call(q, k, v, do, lse.reshape(h, nq, 1, TQ), delta.reshape(h, nq, 1, TQ), cq_r, ck_c, sinks,
                              slopes, kind, scale, name + "_b")
        dq, dk, dvv = outs[:3]
        dc = dsinks = None
        if kind == "fox":
            dc = outs[3].reshape(h, n) + outs[4].reshape(h, n)
        if kind == "swa":
            dsinks = outs[3][:, 0, 0]
        return dq, dk, dvv, dc, dsinks, (None if slopes is None else jnp.zeros_like(slopes))

    return _op(run_fwd, run_bwd)(q, k, v, c, sinks, slopes)


def _sigmoid(x):
    return 1.0 / (1.0 + jnp.exp(-x))


def _merge_fwd_call(gs, ys, name):
    n, c = ys[0].shape
    tr = _rows_tile(n, c, 1 << 20)

    def body(g0, g1, g2, y0, y1, y2, m_ref):
        m_ref[...] = (_sigmoid(g0[...]) * y0[...] + _sigmoid(g1[...]) * y1[...]) + _sigmoid(g2[...]) * y2[...]

    spec = BS((tr, c), lambda i: (i, 0))
    return pl.pallas_call(
        body, name=name, grid=(n // tr,), in_specs=[spec] * 6, out_specs=spec, out_shape=SDS((n, c), F32),
        compiler_params=_cparams(("parallel",)))(*gs, *ys)


def _merge_bwd_call(gs, ys, dm, name):
    n, c = ys[0].shape
    tr = _rows_tile(n, c, 1 << 20)

    def body(g0, g1, g2, y0, y1, y2, dm_ref, dg0, dg1, dg2, dy0, dy1, dy2):
        d = dm_ref[...]
        for g, y, dg, dy in ((g0, y0, dg0, dy0), (g1, y1, dg1, dy1), (g2, y2, dg2, dy2)):
            s = _sigmoid(g[...])
            dy[...] = d * s
            dg[...] = d * y[...] * (s * (1.0 - s))

    spec = BS((tr, c), lambda i: (i, 0))
    return pl.pallas_call(
        body, name=name, grid=(n // tr,), in_specs=[spec] * 7, out_specs=[spec] * 6,
        out_shape=[SDS((n, c), F32)] * 6, compiler_params=_cparams(("parallel",)))(*gs, *ys, dm)


def gated_merge(gs, ys, name):
    def fwd(gs, ys):
        return _merge_fwd_call(gs, ys, name + "_f"), (gs, ys)

    def bwd(res, dm):
        out = _merge_bwd_call(res[0], res[1], dm, name + "_b")
        return tuple(out[:3]), tuple(out[3:])

    return _op(fwd, bwd)(tuple(gs), tuple(ys))


CONV_TR = 264
CONV_TC = 1408


def _conv_tiles(n, f):
    tr = CONV_TR if n % CONV_TR == 0 else _div_tile(n, CONV_TR)
    tc = CONV_TC if f % CONV_TC == 0 else f
    return tr, tc


def _shift_down(cur, halo, first, tr):
    halo = jnp.where(first, 0.0, halo)
    row = lax.broadcasted_iota(jnp.int32, cur.shape, 0)
    h7, h6 = halo[7:8, :], halo[6:7, :]
    u1 = jnp.where(row == 0, h7, pltpu.roll(cur, 1, 0))
    u2 = jnp.where(row == 0, h6, jnp.where(row == 1, h7, pltpu.roll(cur, 2, 0)))
    return u1, u2


def _conv_lin(cur, u1, u2, w_ref, b_ref):
    return ((b_ref[...] + w_ref[0:1, :] * u2) + w_ref[1:2, :] * u1) + w_ref[2:3, :] * cur


def _conv_in_specs(tr, tc, nj):
    sub = tr // SUBLANES
    prev = lambda j, i: (jnp.maximum(i * sub - 1, 0), j)
    prev_v = lambda j, i: (jnp.maximum(i * sub - 1, 0), j + nj)
    return [BS((tr, tc), lambda j, i: (i, j)), BS((SUBLANES, tc), prev),
            BS((tr, tc), lambda j, i: (i, j + nj)), BS((SUBLANES, tc), prev_v),
            BS((3, tc), lambda j, i: (0, j)), BS((3, tc), lambda j, i: (0, j + nj)),
            BS((1, tc), lambda j, i: (0, j)), BS((1, tc), lambda j, i: (0, j + nj))]


def _conv_fwd_call(u, cw, cb, name):
    n, f2 = u.shape
    f = f2 // 2
    tr, tc = _conv_tiles(n, f)
    nj = f // tc

    def body(ug, ugh, uv, uvh, wg, wv, bg, bv, a_ref):
        first = pl.program_id(1) == 0
        g1, g2 = _shift_down(ug[...], ugh[...], first, tr)
        v1, v2 = _shift_down(uv[...], uvh[...], first, tr)
        cg = _conv_lin(ug[...], g1, g2, wg, bg)
        cv = _conv_lin(uv[...], v1, v2, wv, bv)
        a_ref[...] = cg * _sigmoid(cg) * cv

    return pl.pallas_call(
        body, name=name, grid=(nj, n // tr), in_specs=_conv_in_specs(tr, tc, nj),
        out_specs=BS((tr, tc), lambda j, i: (i, j)), out_shape=SDS((n, f), F32),
        compiler_params=_cparams(("parallel", "parallel")))(u, u, u, u, cw, cw, cb, cb)


def _conv_bwd_dc_call(u, cw, cb, da, name):
    n, f2 = u.shape
    f = f2 // 2
    tr, tc = _conv_tiles(n, f)
    nj = f // tc

    def body(ug, ugh, uv, uvh, wg, wv, bg, bv, da_ref, dc_ref, dw_ref, db_ref):
        first = pl.program_id(1) == 0
        g0, v0 = ug[...], uv[...]
        g1, g2 = _shift_down(g0, ugh[...], first, tr)
        v1, v2 = _shift_down(v0, uvh[...], first, tr)
        cg = _conv_lin(g0, g1, g2, wg, bg)
        cv = _conv_lin(v0, v1, v2, wv, bv)
        d = da_ref[...]
        s = _sigmoid(cg)
        dcg = d * cv * (s * (1.0 + cg * (1.0 - s)))
        dcv = d * (cg * s)
        dc_ref[0] = dcg
        dc_ref[1] = dcv

        @pl.when(first)
        def _():
            dw_ref[...] = jnp.zeros_like(dw_ref)
            db_ref[...] = jnp.zeros_like(db_ref)

        for p, dc, taps in ((0, dcg, (g2, g1, g0)), (1, dcv, (v2, v1, v0))):
            for t in range(3):
                dw_ref[p, t:t + 1, :] += jnp.sum(dc * taps[t], axis=0, keepdims=True)
            db_ref[p] += jnp.sum(dc, axis=0, keepdims=True)

    return pl.pallas_call(
        body, name=name, grid=(nj, n // tr),
        in_specs=_conv_in_specs(tr, tc, nj) + [BS((tr, tc), lambda j, i: (i, j))],
        out_specs=[BS((2, tr, tc), lambda j, i: (0, i, j)), BS((2, 3, tc), lambda j, i: (0, 0, j)),
                   BS((2, 1, tc), lambda j, i: (0, 0, j))],
        out_shape=[SDS((2, n, f), F32), SDS((2, 3, f), F32), SDS((2, 1, f), F32)],
        compiler_params=_cparams(("arbitrary", "arbitrary")))(u, u, u, u, cw, cw, cb, cb, da)


def _conv_bwd_du_call(dc, cw, name):
    _, n, f = dc.shape
    tr, tc = _conv_tiles(n, f)
    nj = f // tc
    ni = n // tr
    sub = tr // SUBLANES

    def body(c_ref, nx_ref, w_ref, du_ref):
        cur = c_ref[0]
        nxt = jnp.where(pl.program_id(2) == ni - 1, 0.0, nx_ref[0])
        row = lax.broadcasted_iota(jnp.int32, cur.shape, 0)
        n0, n1 = nxt[0:1, :], nxt[1:2, :]
        d1 = jnp.where(row == tr - 1, n0, pltpu.roll(cur, tr - 1, 0))
        d2 = jnp.where(row == tr - 1, n1, jnp.where(row == tr - 2, n0, pltpu.roll(cur, tr - 2, 0)))
        du_ref[...] = (w_ref[2:3, :] * cur + w_ref[1:2, :] * d1) + w_ref[0:1, :] * d2

    nxt_map = lambda p, j, i: (p, jnp.minimum((i + 1) * sub, n // SUBLANES - 1), j)
    return pl.pallas_call(
        body, name=name, grid=(2, nj, ni),
        in_specs=[BS((1, tr, tc), lambda p, j, i: (p, i, j)), BS((1, SUBLANES, tc), nxt_map),
                  BS((3, tc), lambda p, j, i: (0, p * nj + j))],
        out_specs=BS((tr, tc), lambda p, j, i: (i, p * nj + j)), out_shape=SDS((n, 2 * f), F32),
        compiler_params=_cparams(("parallel", "parallel", "parallel")))(dc, dc, cw)


def conv_glu(u, cw, cb, name):
    def fwd(u, cw, cb):
        return _conv_fwd_call(u, cw, cb, name + "_f"), (u, cw, cb)

    def bwd(res, da):
        u, cw, cb = res
        dc, dw, db = _conv_bwd_dc_call(u, cw, cb, da, name + "_bc")
        du = _conv_bwd_du_call(dc, cw, name + "_bu")
        return du, jnp.concatenate([dw[0], dw[1]], axis=-1), jnp.concatenate([db[0], db[1]], axis=-1)

    return _op(fwd, bwd)(u, cw, cb)


def _loss_call(y, t, n_real, name):
    n, c = y.shape
    tr = _rows_tile(n, c, 1 << 20)

    def body(y_ref, t_ref, dy_ref, l_ref):
        i = pl.program_id(0)
        row = i * tr + lax.broadcasted_iota(jnp.int32, (tr, c), 0)
        real = (row >= N_META) & (row < N_META + n_real)
        e = jnp.where(real, y_ref[...] - t_ref[...], 0.0)
        dy_ref[...] = e * (1.0 / c)

        @pl.when(i == 0)
        def _():
            l_ref[...] = jnp.zeros_like(l_ref)

        l_ref[...] += 0.5 * jnp.sum(jnp.sum(e * e, axis=-1, keepdims=True) * (1.0 / c), axis=0, keepdims=True)

    spec = BS((tr, c), lambda i: (i, 0))
    return pl.pallas_call(
        body, name=name, grid=(n // tr,), in_specs=[spec, spec],
        out_specs=[spec, BS((1, 1), lambda i: (0, 0))], out_shape=[SDS((n, c), F32), SDS((1, 1), F32)],
        compiler_params=_cparams(("arbitrary",)))(y, t)


def _to_heads(x, nh):
    n = x.shape[0]
    return x.reshape(n, nh, x.shape[1] // nh).transpose(1, 0, 2)


def _from_heads(x):
    h, n, d = x.shape
    return x.transpose(1, 0, 2).reshape(n, h * d)


def _head_norm(x, g, denom, name):
    h, n, d = x.shape
    return rms_norm(x.reshape(h * n, d), g, denom, name).reshape(h, n, d)


def _pad_in_cols(w):
    z = lambda k: jnp.zeros(w.shape[:-1] + (k,), w.dtype)
    return jnp.concatenate([w[..., :1544], z(120), w[..., 1544:1960], z(96), w[..., 1960:], z(128)], axis=-1)


def _pad_q_up(w):
    s = w.shape[:-1]
    w = w.reshape(s + (HEADS, MLA_QK))
    w = jnp.concatenate([w, jnp.zeros(s + (HEADS, LANES - MLA_QK), w.dtype)], axis=-1)
    return w.reshape(s + (HEADS * LANES,))


def _assemble(shards):
    full = {k: jnp.concatenate([v[i] for i in range(N_CHIPS)], axis=SHARD_AXIS[k]) for k, v in shards.items()}
    out = dict(full)
    if "w_in" in out:
        out["w_in"] = _pad_in_cols(out["w_in"])
    if "mla_w_q_up" in out:
        out["mla_w_q_up"] = _pad_q_up(out["mla_w_q_up"])
    return out


def _rope_tables(n):
    half = MLA_ROPE // 2
    freqs = ROPE_THETA ** (-jnp.arange(half, dtype=F32) / half)
    ang = jnp.arange(n).astype(F32)[:, None] * freqs[None, :]
    cos, sin = jnp.cos(ang), jnp.sin(ang)
    one, zero = jnp.ones((n, MLA_NOPE), F32), jnp.zeros((n, MLA_NOPE), F32)
    tail1, tail0 = jnp.ones((n, LANES - MLA_QK), F32), jnp.zeros((n, LANES - MLA_QK), F32)
    return (jnp.concatenate([one, cos, cos, tail1], axis=1), jnp.concatenate([zero, sin, sin, tail0], axis=1))


def _pad_lanes(g, width):
    return jnp.concatenate([g, jnp.zeros((width - g.shape[0],), g.dtype)]).reshape(1, width)


def _trunk(eps, fine, small, x, wb):
    seq = x.shape[0]
    n = -(-(N_META + seq) // TQ) * TQ
    ew = _assemble(eps)
    cos, sin = _rope_tables(n)
    slopes = jnp.exp2(-8.0 * jnp.arange(1, HEADS + 1, dtype=F32) / HEADS)
    h = jnp.concatenate([fine["meta_tokens"], x, jnp.zeros((n - N_META - seq, D_MODEL), F32)], axis=0)
    for l in range(DEPTH):
        p = f"l{l}_"
        row = lambda name: small[name][l].reshape(1, -1)
        xn = rms_norm(h, row("norm1_g"), D_MODEL, p + "norm1")
        proj = linear(xn, wb["w_in"][l], ew["w_in"][l], p + "win")
        fq = _head_norm(_to_heads(proj[:, O_FQ:O_FQ + 512], HEADS), row("fox_q_g"), HEAD_DIM, p + "fqn")
        fk = _head_norm(_to_heads(proj[:, O_FK:O_FK + 512], HEADS), row("fox_k_g"), HEAD_DIM, p + "fkn")
        fv = _to_heads(proj[:, O_FV:O_FV + 512], HEADS)
        c = forget_cumsum(proj[:, O_FF:O_FF + HEADS].T, small["fox_forget_b"][l].reshape(HEADS, 1), p + "fgate")
        out_a = attention(fq, fk, fv, c, None, None, "fox", HEAD_DIM ** -0.5, p + "fox")
        cqn = rms_norm(proj[:, O_CQ:O_CQ + MLA_Q_RANK], row("mla_q_a_g"), MLA_Q_RANK, p + "cqn")
        q = _to_heads(linear(cqn, wb["mla_w_q_up"][l], ew["mla_w_q_up"][l], p + "qup"), HEADS)
        q = rope(_head_norm(q, _pad_lanes(small["mla_q_g"][l], LANES), MLA_QK, p + "mqn"), cos, sin, p + "qrope")
        ckvn = rms_norm(proj[:, O_CKV:O_CKV + MLA_KV_RANK], row("mla_kv_a_g"), MLA_KV_RANK, p + "ckvn")
        kv = _to_heads(linear(ckvn, wb["mla_w_kv_up"][l], ew["mla_w_kv_up"][l], p + "kvup"), HEADS)
        kr = jnp.broadcast_to(proj[None, :, O_KR:O_KR + MLA_ROPE], (HEADS, n, MLA_ROPE))
        k = jnp.concatenate([kv[..., :MLA_NOPE], kr, jnp.zeros((HEADS, n, LANES - MLA_QK), F32)], axis=-1)
        k = rope(_head_norm(k, _pad_lanes(small["mla_k_g"][l], LANES), MLA_QK, p + "mkn"), cos, sin, p + "krope")
        out_b = attention(q, k, kv[..., MLA_NOPE:], None, None, None, "mla", MLA_QK ** -0.5, p + "mla")
        sq = _head_norm(_to_heads(proj[:, O_SQ:O_SQ + 512], HEADS), row("swa_q_g"), HEAD_DIM, p + "sqn")
        sk = _head_norm(_to_heads(proj[:, O_SK:O_SK + 128], SWA_KV_HEADS), row("swa_k_g"), HEAD_DIM, p + "skn")
        sv = _to_heads(proj[:, O_SV:O_SV + 128], SWA_KV_HEADS)
        rep = HEADS // SWA_KV_HEADS
        out_c = attention(sq, jnp.repeat(sk, rep, axis=0), jnp.repeat(sv, rep, axis=0), None, small["swa_sinks"][l],
                          slopes, "swa", HEAD_DIM ** -0.5, p + "swa")
        ys = [linear(_from_heads(o), wb["w_branch"][l, i], ew["w_branch"][l, i], p + f"br{i}")
              for i, o in enumerate((out_a, out_b, out_c))]
        gs = [proj[:, O_G + i * D_MODEL:O_G + (i + 1) * D_MODEL] for i in range(N_BRANCH)]
        merged = gated_merge(gs, ys, p + "merge")
        h = linear(merged, wb["w_o"][l], ew["w_o"][l], p + "wo", res=h)
        xn2 = rms_norm(h, row("norm2_g"), D_MODEL, p + "norm2")
        u = linear(xn2, wb["ffn_w_up"][l], ew["ffn_w_up"][l], p + "wup")
        act = conv_glu(u, fine["ffn_conv_w"][l], row("ffn_conv_b"), p + "conv")
        h = linear(act, wb["ffn_w_down"][l], ew["ffn_w_down"][l], p + "wdown", res=h)
    return h


def _local_step(x, target, wb, fine, small):
    seq = x.shape[0]
    eps = {k: jnp.zeros((N_CHIPS,) + SHARD_SHAPE[k], F32) for k in BIG}
    y, vjp = jax.vjp(lambda e, f, s, xx: _trunk(e, f, s, xx, wb), eps, fine, small, x)
    n = y.shape[0]
    tpad = jnp.concatenate([jnp.zeros((N_META, D_MODEL), F32), target, jnp.zeros((n - N_META - seq, D_MODEL), F32)])
    dy, loss = _loss_call(y, tpad, seq, "loss")
    g_eps, g_fine, g_small, g_x = vjp(dy)
    return loss[0, 0], g_x, g_eps, g_fine, g_small


def _pack_rows(shapes, mult):
    total = sum(_size(s) for s in shapes)
    rows = -(-total // LANES)
    return -(-rows // mult) * mult


def _size(shape):
    n = 1
    for d in shape:
        n *= d
    return n


def _pack(arrs, rows, dtype):
    flat = [a.reshape(-1).astype(dtype) for a in arrs]
    used = sum(a.size for a in flat)
    flat.append(jnp.zeros((rows * LANES - used,), dtype))
    return jnp.concatenate(flat).reshape(rows, LANES)


def _unpack(p, shapes):
    flat = p.reshape(-1)
    out, off = [], 0
    for s in shapes:
        out.append(flat[off:off + _size(s)].reshape(s))
        off += _size(s)
    return out


MESH = pl.DeviceIdType.MESH
ANY = pl.BlockSpec(memory_space=pl.ANY)


def _me():
    return lax.axis_index("x"), lax.axis_index("y"), lax.axis_index("c")


def _remote(src, dst, send_sems, recv_sems, idx, dev):
    return pltpu.make_async_remote_copy(src_ref=src, dst_ref=dst, send_sem=send_sems.at[idx], recv_sem=recv_sems.at[idx],
                                        device_id=dev, device_id_type=MESH)


def _all_gather(packs, name):
    npk = len(packs)

    def body(*refs):
        ins, outs = refs[:npk], refs[npk:2 * npk]
        send_sems, recv_sems, loc_sems = refs[2 * npk:]
        x, y, c = _me()
        j = 2 * x + y
        sibling = (x, y, 1 - c)
        chips = [(1 - x, y), (x, 1 - y), (1 - x, 1 - y)]
        locals_, sends = [], []
        for p in range(npk):
            half = packs[p].shape[0] // 2
            mine = pl.ds(pl.multiple_of(c * half, 16), half)
            lc = pltpu.make_async_copy(ins[p], outs[p].at[j], loc_sems.at[p])
            lc.start()
            locals_.append(lc)
            for r, (cx, cy) in enumerate(chips):
                cp = _remote(ins[p].at[mine], outs[p].at[j, mine], send_sems, recv_sems, 6 * p + r, (cx, cy, c))
                cp.start()
                sends.append(cp)
        for p in range(npk):
            half = packs[p].shape[0] // 2
            mine = pl.ds(pl.multiple_of(c * half, 16), half)
            for r, (cx, cy) in enumerate(chips):
                blk = outs[p].at[2 * cx + cy, mine]
                _remote(blk, blk, send_sems, recv_sems, 6 * p + r, sibling).wait_recv()
                fw = _remote(blk, blk, send_sems, recv_sems, 6 * p + 3 + r, sibling)
                fw.start()
                sends.append(fw)
        for p in range(npk):
            half = packs[p].shape[0] // 2
            other = pl.ds(pl.multiple_of((1 - c) * half, 16), half)
            for r, (cx, cy) in enumerate(chips):
                blk = outs[p].at[2 * cx + cy, other]
                _remote(blk, blk, send_sems, recv_sems, 6 * p + 3 + r, sibling).wait_recv()
        for cp in sends:
            cp.wait_send()
        for lc in locals_:
            lc.wait()

    return pl.pallas_call(
        body, name=name, in_specs=[ANY] * npk, out_specs=[ANY] * npk,
        out_shape=[SDS((N_CHIPS,) + p.shape, p.dtype) for p in packs],
        scratch_shapes=[pltpu.SemaphoreType.DMA((6 * npk,)), pltpu.SemaphoreType.DMA((6 * npk,)),
                        pltpu.SemaphoreType.DMA((npk,))],
        compiler_params=pltpu.CompilerParams(has_side_effects=True))(*packs)


def _pair_exchange(g, name):
    _, rows, _ = g.shape
    half = rows // 2

    def body(g_ref, o_ref, send_sem, recv_sem):
        x, y, c = _me()
        other = pl.ds(pl.multiple_of((1 - c) * half, 16), half)
        cp = _remote(g_ref.at[:, other], o_ref, send_sem, recv_sem, 0, (x, y, 1 - c))
        cp.start()
        cp.wait()

    return pl.pallas_call(
        body, name=name, in_specs=[ANY], out_specs=ANY, out_shape=SDS((N_CHIPS, half, LANES), g.dtype),
        scratch_shapes=[pltpu.SemaphoreType.DMA((1,)), pltpu.SemaphoreType.DMA((1,))],
        compiler_params=pltpu.CompilerParams(has_side_effects=True))(g)


def _chip_exchange(s1, small, name):
    _, half, _ = s1.shape

    def body(s_ref, sm_ref, r_ref, sa_ref, send_sems, recv_sems, loc_sem):
        x, y, c = _me()
        me = 4 * x + 2 * y + c
        lc = pltpu.make_async_copy(sm_ref, sa_ref.at[me], loc_sem.at[0])
        lc.start()
        cps = []
        for r, (cx, cy) in enumerate([(1 - x, y), (x, 1 - y), (1 - x, 1 - y)]):
            cp = _remote(s_ref.at[2 * cx + cy], r_ref.at[r], send_sems, recv_sems, r, (cx, cy, c))
            cp.start()
            cps.append(cp)
        for mask in range(1, N_DEV):
            px, py, pc = x ^ (mask >> 2), y ^ ((mask >> 1) & 1), c ^ (mask & 1)
            cp = _remote(sm_ref, sa_ref.at[me], send_sems, recv_sems, 2 + mask, (px, py, pc))
            cp.start()
            cps.append(cp)
        for r in range(3):
            _remote(r_ref.at[r], r_ref.at[r], send_sems, recv_sems, r, (x, y, c)).wait_recv()
        for mask in range(1, N_DEV):
            src = 4 * (x ^ (mask >> 2)) + 2 * (y ^ ((mask >> 1) & 1)) + (c ^ (mask & 1))
            _remote(sa_ref.at[src], sa_ref.at[src], send_sems, recv_sems, 2 + mask, (x, y, c)).wait_recv()
        for cp in cps:
            cp.wait_send()
        lc.wait()

    return pl.pallas_call(
        body, name=name, in_specs=[ANY, ANY], out_specs=[ANY, ANY],
        out_shape=[SDS((3, half, LANES), s1.dtype), SDS((N_DEV,) + small.shape, small.dtype)],
        scratch_shapes=[pltpu.SemaphoreType.DMA((10,)), pltpu.SemaphoreType.DMA((10,)), pltpu.SemaphoreType.DMA((1,))],
        compiler_params=pltpu.CompilerParams(has_side_effects=True))(s1, small)


def _half_exchange(gh, name):
    half = gh.shape[0]

    def body(g_ref, o_ref, send_sem, recv_sem, loc_sem):
        x, y, c = _me()
        lc = pltpu.make_async_copy(g_ref, o_ref.at[c], loc_sem.at[0])
        lc.start()
        cp = _remote(g_ref, o_ref.at[c], send_sem, recv_sem, 0, (x, y, 1 - c))
        cp.start()
        _remote(o_ref.at[1 - c], o_ref.at[1 - c], send_sem, recv_sem, 0, (x, y, c)).wait_recv()
        cp.wait_send()
        lc.wait()

    return pl.pallas_call(
        body, name=name, in_specs=[ANY], out_specs=ANY, out_shape=SDS((2, half, LANES), gh.dtype),
        scratch_shapes=[pltpu.SemaphoreType.DMA((1,)), pltpu.SemaphoreType.DMA((1,)), pltpu.SemaphoreType.DMA((1,))],
        compiler_params=pltpu.CompilerParams(has_side_effects=True))(gh)


ADD_TR = 1024


def _pair_add(g, r1, c_idx, name):
    _, half, _ = r1.shape
    nb = half // ADD_TR

    def body(c_ref, g_ref, r_ref, o_ref):
        o_ref[...] = g_ref[...] + r_ref[...]

    blk = (1, ADD_TR, LANES)
    return pl.pallas_call(
        body, name=name,
        grid_spec=pltpu.PrefetchScalarGridSpec(
            num_scalar_prefetch=1, grid=(N_CHIPS, nb),
            in_specs=[BS(blk, lambda k, i, c: (k, c[0] * nb + i, 0)), BS(blk, lambda k, i, c: (k, i, 0))],
            out_specs=BS(blk, lambda k, i, c: (k, i, 0))),
        out_shape=SDS(r1.shape, F32), compiler_params=_cparams(("parallel", "parallel")))(c_idx, g, r1)


def _chip_add(s1, r2, j_idx, name):
    _, half, _ = s1.shape

    def body(j_ref, s_ref, r_ref, o_ref):
        o_ref[...] = ((s_ref[0] + r_ref[0]) + r_ref[1]) + r_ref[2]

    return pl.pallas_call(
        body, name=name,
        grid_spec=pltpu.PrefetchScalarGridSpec(
            num_scalar_prefetch=1, grid=(half // ADD_TR,),
            in_specs=[BS((1, ADD_TR, LANES), lambda i, j: (j[0], i, 0)), BS((3, ADD_TR, LANES), lambda i, j: (0, i, 0))],
            out_specs=BS((ADD_TR, LANES), lambda i, j: (i, 0))),
        out_shape=SDS((half, LANES), F32), compiler_params=_cparams(("parallel",)))(j_idx, s1, r2)


def _adamw_math(w, g, m, v):
    m = ADAM_B1 * m + (1.0 - ADAM_B1) * g
    v = ADAM_B2 * v + (1.0 - ADAM_B2) * (g * g)
    m_hat = m / (1.0 - ADAM_B1 ** ADAM_STEP)
    v_hat = v / (1.0 - ADAM_B2 ** ADAM_STEP)
    delta = -ADAM_LR * (m_hat / (jnp.sqrt(v_hat) + ADAM_EPS) + ADAM_WD * w)
    return delta, m, v


def _adamw(w, g, m, v, name):
    rows = w.shape[0]
    tr = _div_tile(rows, ADD_TR)

    def body(w_ref, g_ref, m_ref, v_ref, d_out, m_out, v_out):
        d_out[...], m_out[...], v_out[...] = _adamw_math(w_ref[...], g_ref[...], m_ref[...], v_ref[...])

    spec = BS((tr, LANES), lambda i: (i, 0))
    return pl.pallas_call(
        body, name=name, grid=(rows // tr,), in_specs=[spec] * 4, out_specs=[spec] * 3,
        out_shape=[SDS(w.shape, F32)] * 3, compiler_params=_cparams(("parallel",)))(w, g, m, v)


def _adamw_small(sa, w, m, v, name):
    def body(sa_ref, w_ref, m_ref, v_ref, g_out, d_out, m_out, v_out):
        g = sa_ref[0]
        for d in range(1, N_DEV):
            g = g + sa_ref[d]
        g_out[...] = g
        d_out[...], m_out[...], v_out[...] = _adamw_math(w_ref[...], g, m_ref[...], v_ref[...])

    return pl.pallas_call(body, name=name, out_shape=[SDS(w.shape, F32)] * 4,
                          compiler_params=pltpu.CompilerParams(vmem_limit_bytes=VMEM_LIMIT))(sa, w, m, v)


BIG_ROWS = _pack_rows([SHARD_SHAPE[k] for k in BIG], PACK_ROWS)
FINE_ROWS = _pack_rows([SHARD_SHAPE[k] for k in FINE], 32)
GRAD_ROWS = _pack_rows([SHARD_SHAPE[k] for k in BIG + FINE], PACK_ROWS)
SMALL_SHAPE = {"norm1_g": (2, 1024), "fox_forget_b": (2, 8), "fox_q_g": (2, 64), "fox_k_g": (2, 64),
               "mla_q_a_g": (2, 256), "mla_kv_a_g": (2, 128), "mla_q_g": (2, 96), "mla_k_g": (2, 96),
               "swa_q_g": (2, 64), "swa_k_g": (2, 64), "swa_sinks": (2, 8), "norm2_g": (2, 1024),
               "ffn_conv_b": (2, 5632)}
SMALL_ROWS = _pack_rows([SMALL_SHAPE[k] for k in SMALL] + [(1,)], SUBLANES)


def kernel(x, meta_tokens, norm1_g, w_in, fox_forget_b, fox_q_g, fox_k_g, mla_q_a_g, mla_w_q_up, mla_kv_a_g, mla_w_kv_up, mla_q_g, mla_k_g, swa_q_g, swa_k_g, swa_sinks, w_branch, w_o, norm2_g, ffn_w_up, ffn_conv_w, ffn_conv_b, ffn_w_down, loss_target, m_meta_tokens, m_norm1_g, m_w_in, m_fox_forget_b, m_fox_q_g, m_fox_k_g, m_mla_q_a_g, m_mla_w_q_up, m_mla_kv_a_g, m_mla_w_kv_up, m_mla_q_g, m_mla_k_g, m_swa_q_g, m_swa_k_g, m_swa_sinks, m_w_branch, m_w_o, m_norm2_g, m_ffn_w_up, m_ffn_conv_w, m_ffn_conv_b, m_ffn_w_down, v_meta_tokens, v_norm1_g, v_w_in, v_fox_forget_b, v_fox_q_g, v_fox_k_g, v_mla_q_a_g, v_mla_w_q_up, v_mla_kv_a_g, v_mla_w_kv_up, v_mla_q_g, v_mla_k_g, v_swa_q_g, v_swa_k_g, v_swa_sinks, v_w_branch, v_w_o, v_norm2_g, v_ffn_w_up, v_ffn_conv_w, v_ffn_conv_b, v_ffn_w_down):
    w = dict(meta_tokens=meta_tokens, norm1_g=norm1_g, w_in=w_in, fox_forget_b=fox_forget_b, fox_q_g=fox_q_g,
             fox_k_g=fox_k_g, mla_q_a_g=mla_q_a_g, mla_w_q_up=mla_w_q_up, mla_kv_a_g=mla_kv_a_g,
             mla_w_kv_up=mla_w_kv_up, mla_q_g=mla_q_g, mla_k_g=mla_k_g, swa_q_g=swa_q_g, swa_k_g=swa_k_g,
             swa_sinks=swa_sinks, w_branch=w_branch, w_o=w_o, norm2_g=norm2_g, ffn_w_up=ffn_w_up,
             ffn_conv_w=ffn_conv_w, ffn_conv_b=ffn_conv_b, ffn_w_down=ffn_w_down)
    m = dict(meta_tokens=m_meta_tokens, norm1_g=m_norm1_g, w_in=m_w_in, fox_forget_b=m_fox_forget_b,
             fox_q_g=m_fox_q_g, fox_k_g=m_fox_k_g, mla_q_a_g=m_mla_q_a_g, mla_w_q_up=m_mla_w_q_up,
             mla_kv_a_g=m_mla_kv_a_g, mla_w_kv_up=m_mla_w_kv_up, mla_q_g=m_mla_q_g, mla_k_g=m_mla_k_g,
             swa_q_g=m_swa_q_g, swa_k_g=m_swa_k_g, swa_sinks=m_swa_sinks, w_branch=m_w_branch, w_o=m_w_o,
             norm2_g=m_norm2_g, ffn_w_up=m_ffn_w_up, ffn_conv_w=m_ffn_conv_w, ffn_conv_b=m_ffn_conv_b,
             ffn_w_down=m_ffn_w_down)
    v = dict(meta_tokens=v_meta_tokens, norm1_g=v_norm1_g, w_in=v_w_in, fox_forget_b=v_fox_forget_b,
             fox_q_g=v_fox_q_g, fox_k_g=v_fox_k_g, mla_q_a_g=v_mla_q_a_g, mla_w_q_up=v_mla_w_q_up,
             mla_kv_a_g=v_mla_kv_a_g, mla_w_kv_up=v_mla_w_kv_up, mla_q_g=v_mla_q_g, mla_k_g=v_mla_k_g,
             swa_q_g=v_swa_q_g, swa_k_g=v_swa_k_g, swa_sinks=v_swa_sinks, w_branch=v_w_branch, w_o=v_w_o,
             norm2_g=v_norm2_g, ffn_w_up=v_ffn_w_up, ffn_conv_w=v_ffn_conv_w, ffn_conv_b=v_ffn_conv_b,
             ffn_w_down=v_ffn_w_down)
    xi, yi, ci = _me()
    c_idx = ci.astype(jnp.int32).reshape(1)
    j_idx = (2 * xi + yi).astype(jnp.int32).reshape(1)

    big_shapes = [SHARD_SHAPE[k] for k in BIG]
    fine_shapes = [SHARD_SHAPE[k] for k in FINE]
    gb, gf = _all_gather([_pack([w[k] for k in BIG], BIG_ROWS, BF16), _pack([w[k] for k in FINE], FINE_ROWS, F32)],
                         "gather_weights")
    parts_b = [_unpack(gb[i], big_shapes) for i in range(N_CHIPS)]
    parts_f = [_unpack(gf[i], fine_shapes) for i in range(N_CHIPS)]
    shards_b = {k: jnp.stack([parts_b[i][n] for i in range(N_CHIPS)]) for n, k in enumerate(BIG)}
    shards_f = {k: jnp.stack([parts_f[i][n] for i in range(N_CHIPS)]) for n, k in enumerate(FINE)}
    wb = _assemble(shards_b)
    fine = _assemble(shards_f)
    small = {k: w[k] for k in SMALL}

    loss, g_x, g_eps, g_fine, g_small = _local_step(x[0], loss_target[0], wb, fine, small)

    g_fine_sh = {k: jnp.stack(jnp.split(g_fine[k], N_CHIPS, axis=SHARD_AXIS[k])) for k in FINE}
    gpack = jnp.stack([_pack([g_eps[k][i] for k in BIG] + [g_fine_sh[k][i] for k in FINE], GRAD_ROWS, F32)
                       for i in range(N_CHIPS)])
    spack = _pack([g_small[k] for k in SMALL] + [loss.reshape(1)], SMALL_ROWS, F32)
    r1 = _pair_exchange(gpack, "grads_pair_exchange")
    s1 = _pair_add(gpack, r1, c_idx, "grads_pair_add")
    r2, sa = _chip_exchange(s1, spack, "grads_chip_exchange")
    gh = _chip_add(s1, r2, j_idx, "grads_chip_add")
    g_shard = _half_exchange(gh, "grads_half_exchange").reshape(GRAD_ROWS, LANES)

    sh_names = BIG + FINE
    sh_shapes = [SHARD_SHAPE[k] for k in sh_names]
    d_p, m_p, v_p = _adamw(_pack([w[k] for k in sh_names], GRAD_ROWS, F32), g_shard,
                           _pack([m[k] for k in sh_names], GRAD_ROWS, F32),
                           _pack([v[k] for k in sh_names], GRAD_ROWS, F32), "adamw_shard")
    sm_shapes = [SMALL_SHAPE[k] for k in SMALL] + [(1,)]
    zero1 = jnp.zeros((1,), F32)
    gs_p, ds_p, ms_p, vs_p = _adamw_small(sa, _pack([w[k] for k in SMALL] + [zero1], SMALL_ROWS, F32),
                                          _pack([m[k] for k in SMALL] + [zero1], SMALL_ROWS, F32),
                                          _pack([v[k] for k in SMALL] + [zero1 + 1.0], SMALL_ROWS, F32), "adamw_small")
    grads, deltas, new_m, new_v = {}, {}, {}, {}
    for dst, pk in ((grads, g_shard), (deltas, d_p), (new_m, m_p), (new_v, v_p)):
        dst.update(zip(sh_names, _unpack(pk, sh_shapes)))
    smalls = [_unpack(pk, sm_shapes) for pk in (gs_p, ds_p, ms_p, vs_p)]
    for dst, vals in zip((grads, deltas, new_m, new_v), smalls):
        dst.update(zip(SMALL, vals[:-1]))
    total_loss = smalls[0][-1][0]
    return (total_loss, g_x[None], *[grads[k] for k in WEIGHTS], *[deltas[k] for k in WEIGHTS],
            *[new_m[k] for k in WEIGHTS], *[new_v[k] for k in WEIGHTS])
```

```python
import functools

import jax
import jax.numpy as jnp
from jax import lax
from jax.experimental import pallas as pl
from jax.experimental.pallas import tpu as pltpu

F32 = jnp.float32
BF16 = jnp.bfloat16
SDS = jax.ShapeDtypeStruct
BS = pl.BlockSpec

D_MODEL = 1024
DEPTH = 2
N_META = 16
EPS = 1e-6
HEADS = 8
HEAD_DIM = 64
MLA_Q_RANK = 256
MLA_KV_RANK = 128
MLA_NOPE = 64
MLA_ROPE = 32
MLA_QK = MLA_NOPE + MLA_ROPE
ROPE_THETA = 10000.0
SWA_KV_HEADS = 2
WINDOW = 128
N_BRANCH = 3
BRANCH_WIDTH = 512
D_FF = 2816
IN_WIDTH = 5800
IN_PAD = 6144
N_CHIPS = 4
N_DEV = 8

ADAM_LR = 0.001
ADAM_B1 = 0.9
ADAM_B2 = 0.999
ADAM_EPS = 1e-08
ADAM_WD = 0.01
ADAM_STEP = 10

LANES = 128
SUBLANES = 8
ROW_PAD = 128
CAUSAL_TILE = 384
WINDOW_TILE = 128
NEG = -1e30
VMEM_LIMIT = 56 * 1024 * 1024

O_FQ, O_FK, O_FV, O_FF = 0, 512, 1024, 1536
O_CQ, O_CKV, O_KR = 1664, 1920, 2048
O_SQ, O_SK, O_SV, O_G = 2176, 2688, 2816, 2944

SHARDED = ("meta_tokens", "w_in", "mla_w_q_up", "mla_w_kv_up", "w_branch", "w_o", "ffn_w_up", "ffn_conv_w",
           "ffn_w_down")
SHARD_AXIS = {"meta_tokens": 1, "w_in": 2, "mla_w_q_up": 2, "mla_w_kv_up": 2, "w_branch": 3, "w_o": 1,
              "ffn_w_up": 2, "ffn_conv_w": 2, "ffn_w_down": 1}
SHARD_SHAPE = {"meta_tokens": (16, 256), "w_in": (2, 1024, 1450), "mla_w_q_up": (2, 256, 192),
               "mla_w_kv_up": (2, 128, 256), "w_branch": (2, 3, 512, 256), "w_o": (2, 256, 1024),
               "ffn_w_up": (2, 1024, 1408), "ffn_conv_w": (2, 3, 1408), "ffn_w_down": (2, 704, 1024)}
BIG = ("w_in", "mla_w_q_up", "mla_w_kv_up", "w_branch", "w_o", "ffn_w_up", "ffn_w_down")
FINE = ("meta_tokens", "ffn_conv_w")
SMALL = ("norm1_g", "fox_forget_b", "fox_q_g", "fox_k_g", "mla_q_a_g", "mla_kv_a_g", "mla_q_g", "mla_k_g",
         "swa_q_g", "swa_k_g", "swa_sinks", "norm2_g", "ffn_conv_b")
WEIGHTS = ("meta_tokens", "norm1_g", "w_in", "fox_forget_b", "fox_q_g", "fox_k_g", "mla_q_a_g", "mla_w_q_up",
           "mla_kv_a_g", "mla_w_kv_up", "mla_q_g", "mla_k_g", "swa_q_g", "swa_k_g", "swa_sinks", "w_branch", "w_o",
           "norm2_g", "ffn_w_up", "ffn_conv_w", "ffn_conv_b", "ffn_w_down")
PACK_ROWS = 1024


def _cparams(sem):
    return pltpu.CompilerParams(dimension_semantics=sem, vmem_limit_bytes=VMEM_LIMIT)


def _div_tile(n, cap, mult=SUBLANES):
    best = None
    for t in range(mult, min(n, cap) + 1, mult):
        if n % t == 0:
            best = t
    return best if best is not None else n


def _rows_tile(n, width, budget=2 << 20):
    return _div_tile(n, max(SUBLANES, budget // (4 * max(width, LANES))))


def _op(fwd, bwd):
    @jax.custom_vjp
    def op(*args):
        return fwd(*args)[0]
    op.defvjp(fwd, bwd)
    return op


def _rms_fwd_call(x, g, denom, name):
    n, c = x.shape
    tr = _rows_tile(n, c)

    def body(x_ref, g_ref, y_ref):
        xv = x_ref[...]
        ms = jnp.sum(xv * xv, axis=-1, keepdims=True) * (1.0 / denom)
        y_ref[...] = xv * lax.rsqrt(ms + EPS) * g_ref[...]

    return pl.pallas_call(
        body, name=name, grid=(n // tr,),
        in_specs=[BS((tr, c), lambda i: (i, 0)), BS((1, c), lambda i: (0, 0))],
        out_specs=BS((tr, c), lambda i: (i, 0)), out_shape=SDS((n, c), F32),
        compiler_params=_cparams(("parallel",)))(x, g)


def _rms_bwd_call(x, g, dy, denom, name):
    n, c = x.shape
    tr = _rows_tile(n, c)

    def body(x_ref, g_ref, dy_ref, dx_ref, dg_ref):
        xv = x_ref[...]
        dy = dy_ref[...]
        ms = jnp.sum(xv * xv, axis=-1, keepdims=True) * (1.0 / denom)
        r = lax.rsqrt(ms + EPS)
        xh = xv * r
        dxh = dy * g_ref[...]
        dx_ref[...] = r * (dxh - xh * (jnp.sum(dxh * xh, axis=-1, keepdims=True) * (1.0 / denom)))

        @pl.when(pl.program_id(0) == 0)
        def _():
            dg_ref[...] = jnp.zeros_like(dg_ref)

        dg_ref[...] += jnp.sum(dy * xh, axis=0, keepdims=True)

    return pl.pallas_call(
        body, name=name, grid=(n // tr,),
        in_specs=[BS((tr, c), lambda i: (i, 0)), BS((1, c), lambda i: (0, 0)), BS((tr, c), lambda i: (i, 0))],
        out_specs=[BS((tr, c), lambda i: (i, 0)), BS((1, c), lambda i: (0, 0))],
        out_shape=[SDS((n, c), F32), SDS((1, c), F32)],
        compiler_params=_cparams(("arbitrary",)))(x, g, dy)


def rms_norm(x, g, denom, name):
    def fwd(x, g):
        return _rms_fwd_call(x, g, denom, name + "_f"), (x, g)

    def bwd(res, dy):
        return tuple(_rms_bwd_call(res[0], res[1], dy, denom, name + "_b"))

    return _op(fwd, bwd)(x, g)


def _mm_call(a, b, mode, res, name):
    if mode == "nn":
        (m, kc), n = a.shape, b.shape[1]
    elif mode == "nt":
        (m, kc), n = a.shape, b.shape[0]
    else:
        (kc, m), n = a.shape, b.shape[1]
    if mode == "tn":
        tk = _div_tile(kc, 1056)
        tm = _div_tile(m, 512, LANES)
        tn = _div_tile(n, 1024, LANES)
    else:
        tk = kc if kc <= 2816 else _div_tile(kc, 1024, LANES)
        tm = _div_tile(m, max(LANES, (9 << 19) // (4 * tk)))
        tn = _div_tile(n, 512, LANES)
    nk = kc // tk
    dims = {"nn": (((1,), (0,)), ((), ())), "nt": (((1,), (1,)), ((), ())), "tn": (((0,), (0,)), ((), ()))}[mode]

    def body(*refs):
        if res is None:
            a_ref, b_ref, o_ref, acc_ref = refs
            r_ref = None
        else:
            a_ref, b_ref, r_ref, o_ref, acc_ref = refs
        k = pl.program_id(2)

        @pl.when(k == 0)
        def _():
            acc_ref[...] = jnp.zeros_like(acc_ref)

        acc_ref[...] += lax.dot_general(a_ref[...].astype(BF16), b_ref[...].astype(BF16), dims,
                                        preferred_element_type=F32)

        @pl.when(k == nk - 1)
        def _():
            if r_ref is None:
                o_ref[...] = acc_ref[...]
            else:
                o_ref[...] = r_ref[...] + acc_ref[...]

    a_spec = BS((tk, tm), lambda i, j, k: (k, i)) if mode == "tn" else BS((tm, tk), lambda i, j, k: (i, k))
    b_spec = BS((tn, tk), lambda i, j, k: (j, k)) if mode == "nt" else BS((tk, tn), lambda i, j, k: (k, j))
    o_spec = BS((tm, tn), lambda i, j, k: (i, j))
    ins, args = [a_spec, b_spec], [a, b]
    if res is not None:
        ins.append(o_spec)
        args.append(res)
    return pl.pallas_call(
        body, name=name, grid=(m // tm, n // tn, nk), in_specs=ins, out_specs=o_spec,
        out_shape=SDS((m, n), F32), scratch_shapes=[pltpu.VMEM((tm, tn), F32)],
        compiler_params=_cparams(("parallel", "parallel", "arbitrary")))(*args)


def linear(a, w, eps, name, res=None):
    if res is None:
        def fwd(a, w, eps):
            return _mm_call(a, w, "nn", None, name + "_f"), (a, w)

        def bwd(r, dc):
            a, w = r
            return (_mm_call(dc, w, "nt", None, name + "_da"), jnp.zeros_like(w),
                    _mm_call(a, dc, "tn", None, name + "_dw"))

        return _op(fwd, bwd)(a, w, eps)

    def fwd_r(a, w, eps, res):
        return _mm_call(a, w, "nn", res, name + "_f"), (a, w)

    def bwd_r(r, dc):
        a, w = r
        return (_mm_call(dc, w, "nt", None, name + "_da"), jnp.zeros_like(w),
                _mm_call(a, dc, "tn", None, name + "_dw"), dc)

    return _op(fwd_r, bwd_r)(a, w, eps, res)


CT = 128


def _tri_dot(v, upper):
    r = lax.broadcasted_iota(jnp.int32, (CT, CT), 0)
    c = lax.broadcasted_iota(jnp.int32, (CT, CT), 1)
    tri = jnp.where((r <= c) if upper else (r >= c), 1.0, 0.0).astype(F32)
    return jnp.dot(v, tri, preferred_element_type=F32, precision=lax.Precision.HIGHEST)


def _gate_fwd_call(z, b, name):
    h, n = z.shape

    def body(z_ref, b_ref, c_ref, carry):
        @pl.when(pl.program_id(0) == 0)
        def _():
            carry[...] = jnp.zeros_like(carry)

        x = z_ref[...] + b_ref[...]
        ls = jnp.minimum(x, 0.0) - jnp.log(1.0 + jnp.exp(-jnp.abs(x)))
        c_ref[...] = _tri_dot(ls, True) + carry[...]
        carry[...] += jnp.sum(ls, axis=1, keepdims=True)

    return pl.pallas_call(
        body, name=name, grid=(n // CT,),
        in_specs=[BS((h, CT), lambda j: (0, j)), BS((h, 1), lambda j: (0, 0))],
        out_specs=BS((h, CT), lambda j: (0, j)), out_shape=SDS((h, n), F32),
        scratch_shapes=[pltpu.VMEM((h, 1), F32)],
        compiler_params=_cparams(("arbitrary",)))(z, b)


def _gate_bwd_call(z, b, dc, name):
    h, n = z.shape
    nt = n // CT

    def body(z_ref, b_ref, dc_ref, dz_ref, db_ref, carry):
        @pl.when(pl.program_id(0) == 0)
        def _():
            carry[...] = jnp.zeros_like(carry)
            db_ref[...] = jnp.zeros_like(db_ref)

        dcv = dc_ref[...]
        dls = _tri_dot(dcv, False) + carry[...]
        carry[...] += jnp.sum(dcv, axis=1, keepdims=True)
        x = z_ref[...] + b_ref[...]
        e = jnp.exp(-jnp.abs(x))
        dz = dls * jnp.where(x >= 0, e / (1.0 + e), 1.0 / (1.0 + e))
        dz_ref[...] = dz
        db_ref[...] += jnp.sum(dz, axis=1, keepdims=True)

    rev = lambda j: (0, nt - 1 - j)
    return pl.pallas_call(
        body, name=name, grid=(nt,),
        in_specs=[BS((h, CT), rev), BS((h, 1), lambda j: (0, 0)), BS((h, CT), rev)],
        out_specs=[BS((h, CT), rev), BS((h, 1), lambda j: (0, 0))],
        out_shape=[SDS((h, n), F32), SDS((h, 1), F32)],
        scratch_shapes=[pltpu.VMEM((h, 1), F32)],
        compiler_params=_cparams(("arbitrary",)))(z, b, dc)


def forget_cumsum(z, b, name):
    def fwd(z, b):
        return _gate_fwd_call(z, b, name + "_f"), (z, b)

    def bwd(res, dc):
        return tuple(_gate_bwd_call(res[0], res[1], dc, name + "_b"))

    return _op(fwd, bwd)(z, b)


def _rope_call(x, cos, sin, name):
    h, n, d = x.shape
    tr = _rows_tile(n, d)

    def body(x_ref, c_ref, s_ref, y_ref):
        xv = x_ref[0]
        lane = lax.broadcasted_iota(jnp.int32, xv.shape, 1)
        rot = jnp.where(lane < MLA_NOPE + MLA_ROPE // 2, -pltpu.roll(xv, d - MLA_ROPE // 2, 1),
                        pltpu.roll(xv, MLA_ROPE // 2, 1))
        y_ref[0] = xv * c_ref[...] + rot * s_ref[...]

    return pl.pallas_call(
        body, name=name, grid=(n // tr, h),
        in_specs=[BS((1, tr, d), lambda i, hh: (hh, i, 0)), BS((tr, d), lambda i, hh: (i, 0)),
                  BS((tr, d), lambda i, hh: (i, 0))],
        out_specs=BS((1, tr, d), lambda i, hh: (hh, i, 0)), out_shape=SDS((h, n, d), F32),
        compiler_params=_cparams(("parallel", "parallel")))(x, cos, sin)


def rope(x, cos, sin, name):
    def fwd(x, cos, sin):
        return _rope_call(x, cos, sin, name + "_f"), (cos, sin)

    def bwd(res, dy):
        cos, sin = res
        return _rope_call(dy, cos, -sin, name + "_b"), jnp.zeros_like(cos), jnp.zeros_like(sin)

    return _op(fwd, bwd)(x, cos, sin)


NT_DIMS = (((1,), (1,)), ((), ()))
TN_DIMS = (((0,), (0,)), ((), ()))


def _attn_tile(kind, n):
    if kind == "swa" or n % CAUSAL_TILE:
        return WINDOW_TILE
    return CAUSAL_TILE


def _attn_fwd_call(q, k, v, cq_c, ck_r, sinks, slopes, kind, scale, name):
    h, n, dk = q.shape
    dv = v.shape[2]
    t = _attn_tile(kind, n)
    nq = n // t
    fox, swa = kind == "fox", kind == "swa"

    def body(*refs):
        it = iter(refs)
        q_ref, k_ref, v_ref = next(it), next(it), next(it)
        cq_ref = ck_ref = sink_ref = slope_ref = None
        if fox:
            cq_ref, ck_ref = next(it), next(it)
        if swa:
            sink_ref, slope_ref = next(it), next(it)
        o_ref, lse_ref, m_scr, l_scr, acc_scr = next(it), next(it), next(it), next(it), next(it)
        hh = pl.program_id(0)
        qi = pl.program_id(1)
        qb = q_ref[0].astype(BF16)
        if swa:
            m_scr[...] = jnp.full((t, 1), sink_ref[hh], F32)
            l_scr[...] = jnp.ones((t, 1), F32)
        else:
            m_scr[...] = jnp.full((t, 1), NEG, F32)
            l_scr[...] = jnp.zeros((t, 1), F32)
        acc_scr[...] = jnp.zeros_like(acc_scr)
        rows = qi * t + lax.broadcasted_iota(jnp.int32, (t, t), 0)

        def process(j, masked=True):
            off = pl.multiple_of(j * t, t)
            kb = k_ref[0, pl.ds(off, t), :].astype(BF16)
            vb = v_ref[0, pl.ds(off, t), :].astype(BF16)
            s = lax.dot_general(qb, kb, NT_DIMS, preferred_element_type=F32) * scale
            if fox:
                s = s + cq_ref[0] - ck_ref[0, j]
            if masked:
                cols = j * t + lax.broadcasted_iota(jnp.int32, (t, t), 1)
                dist = rows - cols
                if swa:
                    s = s - slope_ref[hh] * dist.astype(F32)
                    valid = (dist >= 0) & ((cols < N_META) | (dist < WINDOW))
                else:
                    valid = dist >= 0
                s = jnp.where(valid, s, NEG)
            m_old = m_scr[...]
            m_new = jnp.maximum(m_old, jnp.max(s, axis=1, keepdims=True))
            alpha = jnp.exp(m_old - m_new)
            p = jnp.exp(s - m_new)
            l_scr[...] = alpha * l_scr[...] + jnp.sum(p, axis=1, keepdims=True)
            acc_scr[...] = alpha * acc_scr[...] + jnp.dot(p.astype(BF16), vb, preferred_element_type=F32)
            m_scr[...] = m_new

        if swa:
            process(0)
            pl.when(qi >= 2)(lambda: process(qi - 1))
            pl.when(qi >= 1)(lambda: process(qi))
        else:
            def step(j, carry):
                process(j, masked=False)
                return carry
            lax.fori_loop(0, qi, step, 0)
            process(qi)
        l = l_scr[...]
        o_ref[0] = acc_scr[...] / l
        lse_ref[0] = m_scr[...] + jnp.log(l)

    ins = [BS((1, t, dk), lambda a, b: (a, b, 0)), BS((1, n, dk), lambda a, b: (a, 0, 0)),
           BS((1, n, dv), lambda a, b: (a, 0, 0))]
    args = [q, k, v]
    if fox:
        ins += [BS((1, t, 1), lambda a, b: (a, b, 0)), BS((1, nq, 1, t), lambda a, b: (a, 0, 0, 0))]
        args += [cq_c, ck_r]
    if swa:
        ins += [BS(memory_space=pltpu.SMEM), BS(memory_space=pltpu.SMEM)]
        args += [sinks, slopes]
    return pl.pallas_call(
        body, name=name, grid=(h, nq), in_specs=ins,
        out_specs=[BS((1, t, dv), lambda a, b: (a, b, 0)), BS((1, t, 1), lambda a, b: (a, b, 0))],
        out_shape=[SDS((h, n, dv), F32), SDS((h, n, 1), F32)],
        scratch_shapes=[pltpu.VMEM((t, 1), F32), pltpu.VMEM((t, 1), F32), pltpu.VMEM((t, dv), F32)],
        compiler_params=_cparams(("parallel", "arbitrary")))(*args)


def _attn_bwd_call(q, k, v, do, lse_r, delta_r, cq_r, ck_c, sinks, slopes, kind, scale, name):
    h, n, dk = q.shape
    dv = v.shape[2]
    t = _attn_tile(kind, n)
    nq = nk = n // t
    fox, swa = kind == "fox", kind == "swa"

    def body(*refs):
        it = iter(refs)
        q_ref, k_ref, v_ref, do_ref, lse_ref, delta_ref = (next(it) for _ in range(6))
        cq_ref = ck_ref = sink_ref = slope_ref = dcq_ref = dck_ref = dsink_ref = dck_scr = None
        if fox:
            cq_ref, ck_ref = next(it), next(it)
        if swa:
            sink_ref, slope_ref = next(it), next(it)
        dq_ref, dk_ref, dv_ref = next(it), next(it), next(it)
        if fox:
            dcq_ref, dck_ref = next(it), next(it)
        if swa:
            dsink_ref = next(it)
        dk_scr, dv_scr = next(it), next(it)
        if fox:
            dck_scr = next(it)
        hh = pl.program_id(0)
        kj = pl.program_id(1)

        @pl.when(kj == 0)
        def _():
            dq_ref[...] = jnp.zeros_like(dq_ref)
            if fox:
                dcq_ref[...] = jnp.zeros_like(dcq_ref)
            if swa:
                dsink_ref[...] = jnp.zeros_like(dsink_ref)

        kb = k_ref[0].astype(BF16)
        vb = v_ref[0].astype(BF16)
        dk_scr[...] = jnp.zeros_like(dk_scr)
        dv_scr[...] = jnp.zeros_like(dv_scr)
        if fox:
            dck_scr[...] = jnp.zeros_like(dck_scr)
        krows = kj * t + lax.broadcasted_iota(jnp.int32, (t, t), 0)

        def process(qi, masked=True):
            off = pl.multiple_of(qi * t, t)
            qb = q_ref[0, pl.ds(off, t), :].astype(BF16)
            dob = do_ref[0, pl.ds(off, t), :].astype(BF16)
            st = lax.dot_general(kb, qb, NT_DIMS, preferred_element_type=F32) * scale
            if fox:
                st = st + cq_ref[0, qi] - ck_ref[0]
            lse = lse_ref[0, qi]
            delta = delta_ref[0, qi]
            if masked:
                qcols = qi * t + lax.broadcasted_iota(jnp.int32, (t, t), 1)
                dist = qcols - krows
                if swa:
                    st = st - slope_ref[hh] * dist.astype(F32)
                    valid = (dist >= 0) & ((krows < N_META) | (dist < WINDOW))
                else:
                    valid = dist >= 0
                pt = jnp.where(valid, jnp.exp(st - lse), 0.0)
            else:
                pt = jnp.exp(st - lse)
            dv_scr[...] += jnp.dot(pt.astype(BF16), dob, preferred_element_type=F32)
            dpt = lax.dot_general(vb, dob, NT_DIMS, preferred_element_type=F32)
            dst = pt * (dpt - delta)
            if fox:
                dcq_ref[0, qi] += jnp.sum(dst, axis=0, keepdims=True)
                dck_scr[...] -= jnp.sum(dst, axis=1, keepdims=True)
            dsb = (dst * scale).astype(BF16)
            dk_scr[...] += jnp.dot(dsb, qb, preferred_element_type=F32)
            dq_ref[0, pl.ds(off, t), :] += lax.dot_general(dsb, kb, TN_DIMS, preferred_element_type=F32)
            if swa:
                contrib = -jnp.sum(jnp.exp(sink_ref[hh] - lse) * delta)
                dsink_ref[...] += jnp.where(kj == 0, contrib, 0.0)

        if swa:
            def step(qi, carry):
                process(qi)
                return carry
            lax.fori_loop(kj, jnp.where(kj == 0, nq, jnp.minimum(kj + 2, nq)), step, 0)
        else:
            def step(qi, carry):
                process(qi, masked=False)
                return carry
            process(kj)
            lax.fori_loop(kj + 1, nq, step, 0)
        dk_ref[0] = dk_scr[...]
        dv_ref[0] = dv_scr[...]
        if fox:
            dck_ref[0] = dck_scr[...]

    whole = lambda a, b: (a, 0, 0)
    tile = lambda a, b: (a, b, 0)
    rowv = lambda a, b: (a, 0, 0, 0)
    ins = [BS((1, n, dk), whole), BS((1, t, dk), tile), BS((1, t, dv), tile), BS((1, n, dv), whole),
           BS((1, nq, 1, t), rowv), BS((1, nq, 1, t), rowv)]
    args = [q, k, v, do, lse_r, delta_r]
    outs = [BS((1, n, dk), whole), BS((1, t, dk), tile), BS((1, t, dv), tile)]
    oshape = [SDS((h, n, dk), F32), SDS((h, n, dk), F32), SDS((h, n, dv), F32)]
    scratch = [pltpu.VMEM((t, dk), F32), pltpu.VMEM((t, dv), F32)]
    if fox:
        ins += [BS((1, nq, 1, t), rowv), BS((1, t, 1), tile)]
        args += [cq_r, ck_c]
        outs += [BS((1, nq, 1, t), rowv), BS((1, t, 1), tile)]
        oshape += [SDS((h, nq, 1, t), F32), SDS((h, n, 1), F32)]
        scratch += [pltpu.VMEM((t, 1), F32)]
    if swa:
        ins += [BS(memory_space=pltpu.SMEM), BS(memory_space=pltpu.SMEM)]
        args += [sinks, slopes]
        outs += [BS((1, 1, LANES), whole)]
        oshape += [SDS((h, 1, LANES), F32)]
    return pl.pallas_call(
        body, name=name, grid=(h, nk), in_specs=ins, out_specs=outs, out_shape=oshape, scratch_shapes=scratch,
        compiler_params=_cparams(("arbitrary", "arbitrary")))(*args)


def _rowdot_call(a, b, name):
    n, d = a.shape
    tr = _rows_tile(n, d)

    def body(a_ref, b_ref, o_ref):
        o_ref[...] = jnp.sum(a_ref[...] * b_ref[...], axis=-1, keepdims=True)

    return pl.pallas_call(
        body, name=name, grid=(n // tr,),
        in_specs=[BS((tr, d), lambda i: (i, 0)), BS((tr, d), lambda i: (i, 0))],
        out_specs=BS((tr, 1), lambda i: (i, 0)), out_shape=SDS((n, 1), F32),
        compiler_params=_cparams(("parallel",)))(a, b)


def attention(q, k, v, c, sinks, slopes, kind, scale, name):
    h, n, _ = q.shape
    t = _attn_tile(kind, n)
    nq = n // t
    dv = v.shape[2]

    def run_fwd(q, k, v, c, sinks, slopes):
        cq_c = c.reshape(h, n, 1) if kind == "fox" else None
        ck_r = c.reshape(h, nq, 1, t) if kind == "fox" else None
        o, lse = _attn_fwd_call(q, k, v, cq_c, ck_r, sinks, slopes, kind, scale, name + "_f")
        return o, (q, k, v, c, sinks, slopes, o, lse)

    def run_bwd(res, do):
        q, k, v, c, sinks, slopes, o, lse = res
        delta = _rowdot_call(do.reshape(h * n, dv), o.reshape(h * n, dv), name + "_dl")
        cq_r = c.reshape(h, nq, 1, t) if kind == "fox" else None
        ck_c = c.reshape(h, n, 1) if kind == "fox" else None
        outs = _attn_bwd_call(q, k, v, do, lse.reshape(h, nq, 1, t), delta.reshape(h, nq, 1, t), cq_r, ck_c, sinks,
                              slopes, kind, scale, name + "_b")
        dq, dk, dvv = outs[:3]
        dc = dsinks = None
        if kind == "fox":
            dc = outs[3].reshape(h, n) + outs[4].reshape(h, n)
        if kind == "swa":
            dsinks = outs[3][:, 0, 0]
        return dq, dk, dvv, dc, dsinks, (None if slopes is None else jnp.zeros_like(slopes))

    return _op(run_fwd, run_bwd)(q, k, v, c, sinks, slopes)


def _sigmoid(x):
    return 1.0 / (1.0 + jnp.exp(-x))


def _merge_fwd_call(gs, ys, name):
    n, c = ys[0].shape
    tr = _rows_tile(n, c, 1 << 20)

    def body(g0, g1, g2, y0, y1, y2, m_ref):
        m_ref[...] = (_sigmoid(g0[...]) * y0[...] + _sigmoid(g1[...]) * y1[...]) + _sigmoid(g2[...]) * y2[...]

    spec = BS((tr, c), lambda i: (i, 0))
    return pl.pallas_call(
        body, name=name, grid=(n // tr,), in_specs=[spec] * 6, out_specs=spec, out_shape=SDS((n, c), F32),
        compiler_params=_cparams(("parallel",)))(*gs, *ys)


def _merge_bwd_call(gs, ys, dm, name):
    n, c = ys[0].shape
    tr = _rows_tile(n, c, 1 << 20)

    def body(g0, g1, g2, y0, y1, y2, dm_ref, dg0, dg1, dg2, dy0, dy1, dy2):
        d = dm_ref[...]
        for g, y, dg, dy in ((g0, y0, dg0, dy0), (g1, y1, dg1, dy1), (g2, y2, dg2, dy2)):
            s = _sigmoid(g[...])
            dy[...] = d * s
            dg[...] = d * y[...] * (s * (1.0 - s))

    spec = BS((tr, c), lambda i: (i, 0))
    return pl.pallas_call(
        body, name=name, grid=(n // tr,), in_specs=[spec] * 7, out_specs=[spec] * 6,
        out_shape=[SDS((n, c), F32)] * 6, compiler_params=_cparams(("parallel",)))(*gs, *ys, dm)


def gated_merge(gs, ys, name):
    def fwd(gs, ys):
        return _merge_fwd_call(gs, ys, name + "_f"), (gs, ys)

    def bwd(res, dm):
        out = _merge_bwd_call(res[0], res[1], dm, name + "_b")
        return tuple(out[:3]), tuple(out[3:])

    return _op(fwd, bwd)(tuple(gs), tuple(ys))


CONV_TR = 264
CONV_TC = 1408


def _conv_tiles(n, f):
    tr = CONV_TR if n % CONV_TR == 0 else _div_tile(n, CONV_TR)
    tc = CONV_TC if f % CONV_TC == 0 else f
    return tr, tc


def _shift_down(cur, halo, first, tr):
    halo = jnp.where(first, 0.0, halo)
    row = lax.broadcasted_iota(jnp.int32, cur.shape, 0)
    h7, h6 = halo[7:8, :], halo[6:7, :]
    u1 = jnp.where(row == 0, h7, pltpu.roll(cur, 1, 0))
    u2 = jnp.where(row == 0, h6, jnp.where(row == 1, h7, pltpu.roll(cur, 2, 0)))
    return u1, u2


def _conv_lin(cur, u1, u2, w_ref, b_ref):
    return ((b_ref[...] + w_ref[0:1, :] * u2) + w_ref[1:2, :] * u1) + w_ref[2:3, :] * cur


def _conv_in_specs(tr, tc, nj):
    sub = tr // SUBLANES
    prev = lambda j, i: (jnp.maximum(i * sub - 1, 0), j)
    prev_v = lambda j, i: (jnp.maximum(i * sub - 1, 0), j + nj)
    return [BS((tr, tc), lambda j, i: (i, j)), BS((SUBLANES, tc), prev),
            BS((tr, tc), lambda j, i: (i, j + nj)), BS((SUBLANES, tc), prev_v),
            BS((3, tc), lambda j, i: (0, j)), BS((3, tc), lambda j, i: (0, j + nj)),
            BS((1, tc), lambda j, i: (0, j)), BS((1, tc), lambda j, i: (0, j + nj))]


def _conv_fwd_call(u, cw, cb, name):
    n, f2 = u.shape
    f = f2 // 2
    tr, tc = _conv_tiles(n, f)
    nj = f // tc

    def body(ug, ugh, uv, uvh, wg, wv, bg, bv, a_ref):
        first = pl.program_id(1) == 0
        g1, g2 = _shift_down(ug[...], ugh[...], first, tr)
        v1, v2 = _shift_down(uv[...], uvh[...], first, tr)
        cg = _conv_lin(ug[...], g1, g2, wg, bg)
        cv = _conv_lin(uv[...], v1, v2, wv, bv)
        a_ref[...] = cg * _sigmoid(cg) * cv

    return pl.pallas_call(
        body, name=name, grid=(nj, n // tr), in_specs=_conv_in_specs(tr, tc, nj),
        out_specs=BS((tr, tc), lambda j, i: (i, j)), out_shape=SDS((n, f), F32),
        compiler_params=_cparams(("parallel", "parallel")))(u, u, u, u, cw, cw, cb, cb)


def _conv_bwd_dc_call(u, cw, cb, da, name):
    n, f2 = u.shape
    f = f2 // 2
    tr, tc = _conv_tiles(n, f)
    nj = f // tc

    def body(ug, ugh, uv, uvh, wg, wv, bg, bv, da_ref, dc_ref, dw_ref, db_ref):
        first = pl.program_id(1) == 0
        g0, v0 = ug[...], uv[...]
        g1, g2 = _shift_down(g0, ugh[...], first, tr)
        v1, v2 = _shift_down(v0, uvh[...], first, tr)
        cg = _conv_lin(g0, g1, g2, wg, bg)
        cv = _conv_lin(v0, v1, v2, wv, bv)
        d = da_ref[...]
        s = _sigmoid(cg)
        dcg = d * cv * (s * (1.0 + cg * (1.0 - s)))
        dcv = d * (cg * s)
        dc_ref[0] = dcg
        dc_ref[1] = dcv

        @pl.when(first)
        def _():
            dw_ref[...] = jnp.zeros_like(dw_ref)
            db_ref[...] = jnp.zeros_like(db_ref)

        for p, dc, taps in ((0, dcg, (g2, g1, g0)), (1, dcv, (v2, v1, v0))):
            for t in range(3):
                dw_ref[p, t:t + 1, :] += jnp.sum(dc * taps[t], axis=0, keepdims=True)
            db_ref[p] += jnp.sum(dc, axis=0, keepdims=True)

    return pl.pallas_call(
        body, name=name, grid=(nj, n // tr),
        in_specs=_conv_in_specs(tr, tc, nj) + [BS((tr, tc), lambda j, i: (i, j))],
        out_specs=[BS((2, tr, tc), lambda j, i: (0, i, j)), BS((2, 3, tc), lambda j, i: (0, 0, j)),
                   BS((2, 1, tc), lambda j, i: (0, 0, j))],
        out_shape=[SDS((2, n, f), F32), SDS((2, 3, f), F32), SDS((2, 1, f), F32)],
        compiler_params=_cparams(("arbitrary", "arbitrary")))(u, u, u, u, cw, cw, cb, cb, da)


def _conv_bwd_du_call(dc, cw, name):
    _, n, f = dc.shape
    tr, tc = _conv_tiles(n, f)
    nj = f // tc
    ni = n // tr
    sub = tr // SUBLANES

    def body(c_ref, nx_ref, w_ref, du_ref):
        cur = c_ref[0]
        nxt = jnp.where(pl.program_id(2) == ni - 1, 0.0, nx_ref[0])
        row = lax.broadcasted_iota(jnp.int32, cur.shape, 0)
        n0, n1 = nxt[0:1, :], nxt[1:2, :]
        d1 = jnp.where(row == tr - 1, n0, pltpu.roll(cur, tr - 1, 0))
        d2 = jnp.where(row == tr - 1, n1, jnp.where(row == tr - 2, n0, pltpu.roll(cur, tr - 2, 0)))
        du_ref[...] = (w_ref[2:3, :] * cur + w_ref[1:2, :] * d1) + w_ref[0:1, :] * d2

    nxt_map = lambda p, j, i: (p, jnp.minimum((i + 1) * sub, n // SUBLANES - 1), j)
    return pl.pallas_call(
        body, name=name, grid=(2, nj, ni),
        in_specs=[BS((1, tr, tc), lambda p, j, i: (p, i, j)), BS((1, SUBLANES, tc), nxt_map),
                  BS((3, tc), lambda p, j, i: (0, p * nj + j))],
        out_specs=BS((tr, tc), lambda p, j, i: (i, p * nj + j)), out_shape=SDS((n, 2 * f), F32),
        compiler_params=_cparams(("parallel", "parallel", "parallel")))(dc, dc, cw)


def conv_glu(u, cw, cb, name):
    def fwd(u, cw, cb):
        return _conv_fwd_call(u, cw, cb, name + "_f"), (u, cw, cb)

    def bwd(res, da):
        u, cw, cb = res
        dc, dw, db = _conv_bwd_dc_call(u, cw, cb, da, name + "_bc")
        du = _conv_bwd_du_call(dc, cw, name + "_bu")
        return du, jnp.concatenate([dw[0], dw[1]], axis=-1), jnp.concatenate([db[0], db[1]], axis=-1)

    return _op(fwd, bwd)(u, cw, cb)


def _loss_call(y, t, n_real, name):
    n, c = y.shape
    tr = _rows_tile(n, c, 1 << 20)

    def body(y_ref, t_ref, dy_ref, l_ref):
        i = pl.program_id(0)
        row = i * tr + lax.broadcasted_iota(jnp.int32, (tr, c), 0)
        real = (row >= N_META) & (row < N_META + n_real)
        e = jnp.where(real, y_ref[...] - t_ref[...], 0.0)
        dy_ref[...] = e * (1.0 / c)

        @pl.when(i == 0)
        def _():
            l_ref[...] = jnp.zeros_like(l_ref)

        l_ref[...] += 0.5 * jnp.sum(jnp.sum(e * e, axis=-1, keepdims=True) * (1.0 / c), axis=0, keepdims=True)

    spec = BS((tr, c), lambda i: (i, 0))
    return pl.pallas_call(
        body, name=name, grid=(n // tr,), in_specs=[spec, spec],
        out_specs=[spec, BS((1, 1), lambda i: (0, 0))], out_shape=[SDS((n, c), F32), SDS((1, 1), F32)],
        compiler_params=_cparams(("arbitrary",)))(y, t)


def _to_heads(x, nh):
    n = x.shape[0]
    return x.reshape(n, nh, x.shape[1] // nh).transpose(1, 0, 2)


def _from_heads(x):
    h, n, d = x.shape
    return x.transpose(1, 0, 2).reshape(n, h * d)


def _head_norm(x, g, denom, name):
    h, n, d = x.shape
    return rms_norm(x.reshape(h * n, d), g, denom, name).reshape(h, n, d)


def _pad_in_cols(w):
    z = lambda k: jnp.zeros(w.shape[:-1] + (k,), w.dtype)
    return jnp.concatenate([w[..., :1544], z(120), w[..., 1544:1960], z(96), w[..., 1960:], z(128)], axis=-1)


def _pad_q_up(w):
    s = w.shape[:-1]
    w = w.reshape(s + (HEADS, MLA_QK))
    w = jnp.concatenate([w, jnp.zeros(s + (HEADS, LANES - MLA_QK), w.dtype)], axis=-1)
    return w.reshape(s + (HEADS * LANES,))


def _assemble(shards):
    full = {k: jnp.concatenate([v[i] for i in range(N_CHIPS)], axis=SHARD_AXIS[k]) for k, v in shards.items()}
    out = dict(full)
    if "w_in" in out:
        out["w_in"] = _pad_in_cols(out["w_in"])
    if "mla_w_q_up" in out:
        out["mla_w_q_up"] = _pad_q_up(out["mla_w_q_up"])
    return out


def _rope_tables(n):
    half = MLA_ROPE // 2
    freqs = ROPE_THETA ** (-jnp.arange(half, dtype=F32) / half)
    ang = jnp.arange(n).astype(F32)[:, None] * freqs[None, :]
    cos, sin = jnp.cos(ang), jnp.sin(ang)
    one, zero = jnp.ones((n, MLA_NOPE), F32), jnp.zeros((n, MLA_NOPE), F32)
    tail1, tail0 = jnp.ones((n, LANES - MLA_QK), F32), jnp.zeros((n, LANES - MLA_QK), F32)
    return (jnp.concatenate([one, cos, cos, tail1], axis=1), jnp.concatenate([zero, sin, sin, tail0], axis=1))


def _pad_lanes(g, width):
    return jnp.concatenate([g, jnp.zeros((width - g.shape[0],), g.dtype)]).reshape(1, width)


def _trunk(eps, fine, small, x, wb):
    seq = x.shape[0]
    n = -(-(N_META + seq) // ROW_PAD) * ROW_PAD
    ew = _assemble(eps)
    cos, sin = _rope_tables(n)
    slopes = jnp.exp2(-8.0 * jnp.arange(1, HEADS + 1, dtype=F32) / HEADS)
    h = jnp.concatenate([fine["meta_tokens"], x, jnp.zeros((n - N_META - seq, D_MODEL), F32)], axis=0)
    for l in range(DEPTH):
        p = f"l{l}_"
        row = lambda name: small[name][l].reshape(1, -1)
        xn = rms_norm(h, row("norm1_g"), D_MODEL, p + "norm1")
        proj = linear(xn, wb["w_in"][l], ew["w_in"][l], p + "win")
        fq = _head_norm(_to_heads(proj[:, O_FQ:O_FQ + 512], HEADS), row("fox_q_g"), HEAD_DIM, p + "fqn")
        fk = _head_norm(_to_heads(proj[:, O_FK:O_FK + 512], HEADS), row("fox_k_g"), HEAD_DIM, p + "fkn")
        fv = _to_heads(proj[:, O_FV:O_FV + 512], HEADS)
        c = forget_cumsum(proj[:, O_FF:O_FF + HEADS].T, small["fox_forget_b"][l].reshape(HEADS, 1), p + "fgate")
        out_a = attention(fq, fk, fv, c, None, None, "fox", HEAD_DIM ** -0.5, p + "fox")
        cqn = rms_norm(proj[:, O_CQ:O_CQ + MLA_Q_RANK], row("mla_q_a_g"), MLA_Q_RANK, p + "cqn")
        q = _to_heads(linear(cqn, wb["mla_w_q_up"][l], ew["mla_w_q_up"][l], p + "qup"), HEADS)
        q = rope(_head_norm(q, _pad_lanes(small["mla_q_g"][l], LANES), MLA_QK, p + "mqn"), cos, sin, p + "qrope")
        ckvn = rms_norm(proj[:, O_CKV:O_CKV + MLA_KV_RANK], row("mla_kv_a_g"), MLA_KV_RANK, p + "ckvn")
        kv = _to_heads(linear(ckvn, wb["mla_w_kv_up"][l], ew["mla_w_kv_up"][l], p + "kvup"), HEADS)
        kr = jnp.broadcast_to(proj[None, :, O_KR:O_KR + MLA_ROPE], (HEADS, n, MLA_ROPE))
        k = jnp.concatenate([kv[..., :MLA_NOPE], kr, jnp.zeros((HEADS, n, LANES - MLA_QK), F32)], axis=-1)
        k = rope(_head_norm(k, _pad_lanes(small["mla_k_g"][l], LANES), MLA_QK, p + "mkn"), cos, sin, p + "krope")
        out_b = attention(q, k, kv[..., MLA_NOPE:], None, None, None, "mla", MLA_QK ** -0.5, p + "mla")
        sq = _head_norm(_to_heads(proj[:, O_SQ:O_SQ + 512], HEADS), row("swa_q_g"), HEAD_DIM, p + "sqn")
        sk = _head_norm(_to_heads(proj[:, O_SK:O_SK + 128], SWA_KV_HEADS), row("swa_k_g"), HEAD_DIM, p + "skn")
        sv = _to_heads(proj[:, O_SV:O_SV + 128], SWA_KV_HEADS)
        rep = HEADS // SWA_KV_HEADS
        out_c = attention(sq, jnp.repeat(sk, rep, axis=0), jnp.repeat(sv, rep, axis=0), None, small["swa_sinks"][l],
                          slopes, "swa", HEAD_DIM ** -0.5, p + "swa")
        ys = [linear(_from_heads(o), wb["w_branch"][l, i], ew["w_branch"][l, i], p + f"br{i}")
              for i, o in enumerate((out_a, out_b, out_c))]
        gs = [proj[:, O_G + i * D_MODEL:O_G + (i + 1) * D_MODEL] for i in range(N_BRANCH)]
        merged = gated_merge(gs, ys, p + "merge")
        h = linear(merged, wb["w_o"][l], ew["w_o"][l], p + "wo", res=h)
        xn2 = rms_norm(h, row("norm2_g"), D_MODEL, p + "norm2")
        u = linear(xn2, wb["ffn_w_up"][l], ew["ffn_w_up"][l], p + "wup")
        act = conv_glu(u, fine["ffn_conv_w"][l], row("ffn_conv_b"), p + "conv")
        h = linear(act, wb["ffn_w_down"][l], ew["ffn_w_down"][l], p + "wdown", res=h)
    return h


def _local_step(x, target, wb, fine, small):
    seq = x.shape[0]
    eps = {k: jnp.zeros((N_CHIPS,) + SHARD_SHAPE[k], F32) for k in BIG}
    y, vjp = jax.vjp(lambda e, f, s, xx: _trunk(e, f, s, xx, wb), eps, fine, small, x)
    n = y.shape[0]
    tpad = jnp.concatenate([jnp.zeros((N_META, D_MODEL), F32), target, jnp.zeros((n - N_META - seq, D_MODEL), F32)])
    dy, loss = _loss_call(y, tpad, seq, "loss")
    g_eps, g_fine, g_small, g_x = vjp(dy)
    return loss[0, 0], g_x, g_eps, g_fine, g_small


def _pack_rows(shapes, mult):
    total = sum(_size(s) for s in shapes)
    rows = -(-total // LANES)
    return -(-rows // mult) * mult


def _size(shape):
    n = 1
    for d in shape:
        n *= d
    return n


def _pack(arrs, rows, dtype):
    flat = [a.reshape(-1).astype(dtype) for a in arrs]
    used = sum(a.size for a in flat)
    flat.append(jnp.zeros((rows * LANES - used,), dtype))
    return jnp.concatenate(flat).reshape(rows, LANES)


def _unpack(p, shapes):
    flat = p.reshape(-1)
    out, off = [], 0
    for s in shapes:
        out.append(flat[off:off + _size(s)].reshape(s))
        off += _size(s)
    return out


MESH = pl.DeviceIdType.MESH
ANY = pl.BlockSpec(memory_space=pl.ANY)


def _me():
    return lax.axis_index("x"), lax.axis_index("y"), lax.axis_index("c")


def _remote(src, dst, send_sems, recv_sems, idx, dev):
    return pltpu.make_async_remote_copy(src_ref=src, dst_ref=dst, send_sem=send_sems.at[idx], recv_sem=recv_sems.at[idx],
                                        device_id=dev, device_id_type=MESH)


def _all_gather(packs, name):
    npk = len(packs)

    def body(*refs):
        ins, outs = refs[:npk], refs[npk:2 * npk]
        send_sems, recv_sems = refs[2 * npk:]
        x, y, c = _me()
        j = 2 * x + y
        sibling = (x, y, 1 - c)
        chips = [(1 - x, y), (x, 1 - y), (1 - x, 1 - y)]
        sends = []
        for p in range(npk):
            half = packs[p].shape[0] // 2
            mine = pl.ds(pl.multiple_of(c * half, 16), half)
            for r, (cx, cy) in enumerate(chips):
                cp = _remote(ins[p].at[mine], outs[p].at[j, mine], send_sems, recv_sems, 6 * p + r, (cx, cy, c))
                cp.start()
                sends.append(cp)
        for p in range(npk):
            half = packs[p].shape[0] // 2
            mine = pl.ds(pl.multiple_of(c * half, 16), half)
            for r, (cx, cy) in enumerate(chips):
                blk = outs[p].at[2 * cx + cy, mine]
                _remote(blk, blk, send_sems, recv_sems, 6 * p + r, sibling).wait_recv()
                fw = _remote(blk, blk, send_sems, recv_sems, 6 * p + 3 + r, sibling)
                fw.start()
                sends.append(fw)
        for p in range(npk):
            half = packs[p].shape[0] // 2
            other = pl.ds(pl.multiple_of((1 - c) * half, 16), half)
            for r, (cx, cy) in enumerate(chips):
                blk = outs[p].at[2 * cx + cy, other]
                _remote(blk, blk, send_sems, recv_sems, 6 * p + 3 + r, sibling).wait_recv()
        for cp in sends:
            cp.wait_send()

    outs = pl.pallas_call(
        body, name=name, in_specs=[ANY] * npk, out_specs=[ANY] * npk,
        out_shape=[SDS((N_CHIPS,) + p.shape, p.dtype) for p in packs],
        scratch_shapes=[pltpu.SemaphoreType.DMA((6 * npk,)), pltpu.SemaphoreType.DMA((6 * npk,))],
        compiler_params=pltpu.CompilerParams(has_side_effects=True))(*packs)
    j = 2 * lax.axis_index("x") + lax.axis_index("y")
    return [lax.dynamic_update_index_in_dim(o, p, j, 0) for o, p in zip(outs, packs)]


def _pair_exchange(g, name):
    _, rows, _ = g.shape
    half = rows // 2

    def body(g_ref, o_ref, send_sem, recv_sem):
        x, y, c = _me()
        other = pl.ds(pl.multiple_of((1 - c) * half, 16), half)
        cp = _remote(g_ref.at[:, other], o_ref, send_sem, recv_sem, 0, (x, y, 1 - c))
        cp.start()
        cp.wait()

    return pl.pallas_call(
        body, name=name, in_specs=[ANY], out_specs=ANY, out_shape=SDS((N_CHIPS, half, LANES), g.dtype),
        scratch_shapes=[pltpu.SemaphoreType.DMA((1,)), pltpu.SemaphoreType.DMA((1,))],
        compiler_params=pltpu.CompilerParams(has_side_effects=True))(g)


def _chip_exchange(s1, small, name):
    _, half, _ = s1.shape

    def body(s_ref, sm_ref, r_ref, sa_ref, send_sems, recv_sems, loc_sem):
        x, y, c = _me()
        me = 4 * x + 2 * y + c
        lc = pltpu.make_async_copy(sm_ref, sa_ref.at[me], loc_sem.at[0])
        lc.start()
        cps = []
        for r, (cx, cy) in enumerate([(1 - x, y), (x, 1 - y), (1 - x, 1 - y)]):
            cp = _remote(s_ref.at[2 * cx + cy], r_ref.at[r], send_sems, recv_sems, r, (cx, cy, c))
            cp.start()
            cps.append(cp)
        for mask in range(1, N_DEV):
            px, py, pc = x ^ (mask >> 2), y ^ ((mask >> 1) & 1), c ^ (mask & 1)
            cp = _remote(sm_ref, sa_ref.at[me], send_sems, recv_sems, 2 + mask, (px, py, pc))
            cp.start()
            cps.append(cp)
        for r in range(3):
            _remote(r_ref.at[r], r_ref.at[r], send_sems, recv_sems, r, (x, y, c)).wait_recv()
        for mask in range(1, N_DEV):
            src = 4 * (x ^ (mask >> 2)) + 2 * (y ^ ((mask >> 1) & 1)) + (c ^ (mask & 1))
            _remote(sa_ref.at[src], sa_ref.at[src], send_sems, recv_sems, 2 + mask, (x, y, c)).wait_recv()
        for cp in cps:
            cp.wait_send()
        lc.wait()

    return pl.pallas_call(
        body, name=name, in_specs=[ANY, ANY], out_specs=[ANY, ANY],
        out_shape=[SDS((3, half, LANES), s1.dtype), SDS((N_DEV,) + small.shape, small.dtype)],
        scratch_shapes=[pltpu.SemaphoreType.DMA((10,)), pltpu.SemaphoreType.DMA((10,)), pltpu.SemaphoreType.DMA((1,))],
        compiler_params=pltpu.CompilerParams(has_side_effects=True))(s1, small)


def _half_exchange(gh, name):
    def body(g_ref, o_ref, send_sem, recv_sem):
        x, y, c = _me()
        cp = _remote(g_ref, o_ref, send_sem, recv_sem, 0, (x, y, 1 - c))
        cp.start()
        cp.wait()

    return pl.pallas_call(
        body, name=name, in_specs=[ANY], out_specs=ANY, out_shape=SDS(gh.shape, gh.dtype),
        scratch_shapes=[pltpu.SemaphoreType.DMA((1,)), pltpu.SemaphoreType.DMA((1,))],
        compiler_params=pltpu.CompilerParams(has_side_effects=True))(gh)


ADD_TR = 1024


def _pair_add(g, r1, c_idx, name):
    _, half, _ = r1.shape
    nb = half // ADD_TR

    def body(c_ref, g_ref, r_ref, o_ref, ob_ref):
        s = g_ref[...] + r_ref[...]
        o_ref[...] = s
        ob_ref[...] = s.astype(BF16)

    blk = (1, ADD_TR, LANES)
    own = BS(blk, lambda k, i, c: (k, i, 0))
    return pl.pallas_call(
        body, name=name,
        grid_spec=pltpu.PrefetchScalarGridSpec(
            num_scalar_prefetch=1, grid=(N_CHIPS, nb),
            in_specs=[BS(blk, lambda k, i, c: (k, c[0] * nb + i, 0)), own], out_specs=[own, own]),
        out_shape=[SDS(r1.shape, F32), SDS(r1.shape, BF16)],
        compiler_params=_cparams(("parallel", "parallel")))(c_idx, g, r1)


def _chip_add(s1, r2, j_idx, name):
    _, half, _ = s1.shape

    def body(j_ref, s_ref, r_ref, o_ref):
        o_ref[...] = ((s_ref[0] + r_ref[0].astype(F32)) + r_ref[1].astype(F32)) + r_ref[2].astype(F32)

    return pl.pallas_call(
        body, name=name,
        grid_spec=pltpu.PrefetchScalarGridSpec(
            num_scalar_prefetch=1, grid=(half // ADD_TR,),
            in_specs=[BS((1, ADD_TR, LANES), lambda i, j: (j[0], i, 0)), BS((3, ADD_TR, LANES), lambda i, j: (0, i, 0))],
            out_specs=BS((ADD_TR, LANES), lambda i, j: (i, 0))),
        out_shape=SDS((half, LANES), F32), compiler_params=_cparams(("parallel",)))(j_idx, s1, r2)


def _adamw_math(w, g, m, v):
    m = ADAM_B1 * m + (1.0 - ADAM_B1) * g
    v = ADAM_B2 * v + (1.0 - ADAM_B2) * (g * g)
    m_hat = m / (1.0 - ADAM_B1 ** ADAM_STEP)
    v_hat = v / (1.0 - ADAM_B2 ** ADAM_STEP)
    delta = -ADAM_LR * (m_hat / (jnp.sqrt(v_hat) + ADAM_EPS) + ADAM_WD * w)
    return delta, m, v


def _adamw(w, g, m, v, name):
    rows = w.shape[0]
    tr = _div_tile(rows, ADD_TR)

    def body(w_ref, g_ref, m_ref, v_ref, d_out, m_out, v_out):
        d_out[...], m_out[...], v_out[...] = _adamw_math(w_ref[...], g_ref[...], m_ref[...], v_ref[...])

    spec = BS((tr, LANES), lambda i: (i, 0))
    return pl.pallas_call(
        body, name=name, grid=(rows // tr,), in_specs=[spec] * 4, out_specs=[spec] * 3,
        out_shape=[SDS(w.shape, F32)] * 3, compiler_params=_cparams(("parallel",)))(w, g, m, v)


def _adamw_small(sa, w, m, v, name):
    def body(sa_ref, w_ref, m_ref, v_ref, g_out, d_out, m_out, v_out):
        g = sa_ref[0]
        for d in range(1, N_DEV):
            g = g + sa_ref[d]
        g_out[...] = g
        d_out[...], m_out[...], v_out[...] = _adamw_math(w_ref[...], g, m_ref[...], v_ref[...])

    return pl.pallas_call(body, name=name, out_shape=[SDS(w.shape, F32)] * 4,
                          compiler_params=pltpu.CompilerParams(vmem_limit_bytes=VMEM_LIMIT))(sa, w, m, v)


BIG_ROWS = _pack_rows([SHARD_SHAPE[k] for k in BIG], PACK_ROWS)
FINE_ROWS = _pack_rows([SHARD_SHAPE[k] for k in FINE], 32)
GRAD_ROWS = _pack_rows([SHARD_SHAPE[k] for k in BIG + FINE], PACK_ROWS)
SMALL_SHAPE = {"norm1_g": (2, 1024), "fox_forget_b": (2, 8), "fox_q_g": (2, 64), "fox_k_g": (2, 64),
               "mla_q_a_g": (2, 256), "mla_kv_a_g": (2, 128), "mla_q_g": (2, 96), "mla_k_g": (2, 96),
               "swa_q_g": (2, 64), "swa_k_g": (2, 64), "swa_sinks": (2, 8), "norm2_g": (2, 1024),
               "ffn_conv_b": (2, 5632)}
SMALL_ROWS = _pack_rows([SMALL_SHAPE[k] for k in SMALL] + [(1,)], SUBLANES)


def kernel(x, meta_tokens, norm1_g, w_in, fox_forget_b, fox_q_g, fox_k_g, mla_q_a_g, mla_w_q_up, mla_kv_a_g, mla_w_kv_up, mla_q_g, mla_k_g, swa_q_g, swa_k_g, swa_sinks, w_branch, w_o, norm2_g, ffn_w_up, ffn_conv_w, ffn_conv_b, ffn_w_down, loss_target, m_meta_tokens, m_norm1_g, m_w_in, m_fox_forget_b, m_fox_q_g, m_fox_k_g, m_mla_q_a_g, m_mla_w_q_up, m_mla_kv_a_g, m_mla_w_kv_up, m_mla_q_g, m_mla_k_g, m_swa_q_g, m_swa_k_g, m_swa_sinks, m_w_branch, m_w_o, m_norm2_g, m_ffn_w_up, m_ffn_conv_w, m_ffn_conv_b, m_ffn_w_down, v_meta_tokens, v_norm1_g, v_w_in, v_fox_forget_b, v_fox_q_g, v_fox_k_g, v_mla_q_a_g, v_mla_w_q_up, v_mla_kv_a_g, v_mla_w_kv_up, v_mla_q_g, v_mla_k_g, v_swa_q_g, v_swa_k_g, v_swa_sinks, v_w_branch, v_w_o, v_norm2_g, v_ffn_w_up, v_ffn_conv_w, v_ffn_conv_b, v_ffn_w_down):
    w = dict(meta_tokens=meta_tokens, norm1_g=norm1_g, w_in=w_in, fox_forget_b=fox_forget_b, fox_q_g=fox_q_g,
             fox_k_g=fox_k_g, mla_q_a_g=mla_q_a_g, mla_w_q_up=mla_w_q_up, mla_kv_a_g=mla_kv_a_g,
             mla_w_kv_up=mla_w_kv_up, mla_q_g=mla_q_g, mla_k_g=mla_k_g, swa_q_g=swa_q_g, swa_k_g=swa_k_g,
             swa_sinks=swa_sinks, w_branch=w_branch, w_o=w_o, norm2_g=norm2_g, ffn_w_up=ffn_w_up,
             ffn_conv_w=ffn_conv_w, ffn_conv_b=ffn_conv_b, ffn_w_down=ffn_w_down)
    m = dict(meta_tokens=m_meta_tokens, norm1_g=m_norm1_g, w_in=m_w_in, fox_forget_b=m_fox_forget_b,
             fox_q_g=m_fox_q_g, fox_k_g=m_fox_k_g, mla_q_a_g=m_mla_q_a_g, mla_w_q_up=m_mla_w_q_up,
             mla_kv_a_g=m_mla_kv_a_g, mla_w_kv_up=m_mla_w_kv_up, mla_q_g=m_mla_q_g, mla_k_g=m_mla_k_g,
             swa_q_g=m_swa_q_g, swa_k_g=m_swa_k_g, swa_sinks=m_swa_sinks, w_branch=m_w_branch, w_o=m_w_o,
             norm2_g=m_norm2_g, ffn_w_up=m_ffn_w_up, ffn_conv_w=m_ffn_conv_w, ffn_conv_b=m_ffn_conv_b,
             ffn_w_down=m_ffn_w_down)
    v = dict(meta_tokens=v_meta_tokens, norm1_g=v_norm1_g, w_in=v_w_in, fox_forget_b=v_fox_forget_b,
             fox_q_g=v_fox_q_g, fox_k_g=v_fox_k_g, mla_q_a_g=v_mla_q_a_g, mla_w_q_up=v_mla_w_q_up,
             mla_kv_a_g=v_mla_kv_a_g, mla_w_kv_up=v_mla_w_kv_up, mla_q_g=v_mla_q_g, mla_k_g=v_mla_k_g,
             swa_q_g=v_swa_q_g, swa_k_g=v_swa_k_g, swa_sinks=v_swa_sinks, w_branch=v_w_branch, w_o=v_w_o,
             norm2_g=v_norm2_g, ffn_w_up=v_ffn_w_up, ffn_conv_w=v_ffn_conv_w, ffn_conv_b=v_ffn_conv_b,
             ffn_w_down=v_ffn_w_down)
    xi, yi, ci = _me()
    c_idx = ci.astype(jnp.int32).reshape(1)
    j_idx = (2 * xi + yi).astype(jnp.int32).reshape(1)

    big_shapes = [SHARD_SHAPE[k] for k in BIG]
    fine_shapes = [SHARD_SHAPE[k] for k in FINE]
    gb, gf = _all_gather([_pack([w[k] for k in BIG], BIG_ROWS, BF16), _pack([w[k] for k in FINE], FINE_ROWS, F32)],
                         "gather_weights")
    parts_b = [_unpack(gb[i], big_shapes) for i in range(N_CHIPS)]
    parts_f = [_unpack(gf[i], fine_shapes) for i in range(N_CHIPS)]
    shards_b = {k: jnp.stack([parts_b[i][n] for i in range(N_CHIPS)]) for n, k in enumerate(BIG)}
    shards_f = {k: jnp.stack([parts_f[i][n] for i in range(N_CHIPS)]) for n, k in enumerate(FINE)}
    wb = _assemble(shards_b)
    fine = _assemble(shards_f)
    small = {k: w[k] for k in SMALL}

    loss, g_x, g_eps, g_fine, g_small = _local_step(x[0], loss_target[0], wb, fine, small)

    g_fine_sh = {k: jnp.stack(jnp.split(g_fine[k], N_CHIPS, axis=SHARD_AXIS[k])) for k in FINE}
    gpack = jnp.stack([_pack([g_eps[k][i] for k in BIG] + [g_fine_sh[k][i] for k in FINE], GRAD_ROWS, F32)
                       for i in range(N_CHIPS)])
    spack = _pack([g_small[k] for k in SMALL] + [loss.reshape(1)], SMALL_ROWS, F32)
    r1 = _pair_exchange(gpack, "grads_pair_exchange")
    s1, s1_b = _pair_add(gpack, r1, c_idx, "grads_pair_add")
    r2, sa = _chip_exchange(s1_b, spack, "grads_chip_exchange")
    gh = _chip_add(s1, r2, j_idx, "grads_chip_add")
    go = _half_exchange(gh, "grads_half_exchange")
    g_shard = jnp.where(ci == 0, jnp.concatenate([gh, go]), jnp.concatenate([go, gh]))

    sh_names = BIG + FINE
    sh_shapes = [SHARD_SHAPE[k] for k in sh_names]
    d_p, m_p, v_p = _adamw(_pack([w[k] for k in sh_names], GRAD_ROWS, F32), g_shard,
                           _pack([m[k] for k in sh_names], GRAD_ROWS, F32),
                           _pack([v[k] for k in sh_names], GRAD_ROWS, F32), "adamw_shard")
    sm_shapes = [SMALL_SHAPE[k] for k in SMALL] + [(1,)]
    zero1 = jnp.zeros((1,), F32)
    gs_p, ds_p, ms_p, vs_p = _adamw_small(sa, _pack([w[k] for k in SMALL] + [zero1], SMALL_ROWS, F32),
                                          _pack([m[k] for k in SMALL] + [zero1], SMALL_ROWS, F32),
                                          _pack([v[k] for k in SMALL] + [zero1 + 1.0], SMALL_ROWS, F32), "adamw_small")
    grads, deltas, new_m, new_v = {}, {}, {}, {}
    for dst, pk in ((grads, g_shard), (deltas, d_p), (new_m, m_p), (new_v, v_p)):
        dst.update(zip(sh_names, _unpack(pk, sh_shapes)))
    smalls = [_unpack(pk, sm_shapes) for pk in (gs_p, ds_p, ms_p, vs_p)]
    for dst, vals in zip((grads, deltas, new_m, new_v), smalls):
        dst.update(zip(SMALL, vals[:-1]))
    total_loss = smalls[0][-1][0]
    return (total_loss, g_x[None], *[grads[k] for k in WEIGHTS], *[deltas[k] for k in WEIGHTS],
            *[new_m[k] for k in WEIGHTS], *[new_v[k] for k in WEIGHTS])
```

```python
import functools

import jax
import jax.numpy as jnp
from jax import lax
from jax.experimental import pallas as pl
from jax.experimental.pallas import tpu as pltpu

F32 = jnp.float32
BF16 = jnp.bfloat16
SDS = jax.ShapeDtypeStruct
BS = pl.BlockSpec

D_MODEL = 1024
DEPTH = 2
N_META = 16
EPS = 1e-6
HEADS = 8
HEAD_DIM = 64
MLA_Q_RANK = 256
MLA_KV_RANK = 128
MLA_NOPE = 64
MLA_ROPE = 32
MLA_QK = MLA_NOPE + MLA_ROPE
ROPE_THETA = 10000.0
SWA_KV_HEADS = 2
WINDOW = 128
N_BRANCH = 3
BRANCH_WIDTH = 512
D_FF = 2816
IN_WIDTH = 5800
IN_PAD = 6144
N_CHIPS = 4
N_DEV = 8

ADAM_LR = 0.001
ADAM_B1 = 0.9
ADAM_B2 = 0.999
ADAM_EPS = 1e-08
ADAM_WD = 0.01
ADAM_STEP = 10

LANES = 128
SUBLANES = 8
ROW_PAD = 128
CAUSAL_TILE = 384
NEG = -1e30
VMEM_LIMIT = 56 * 1024 * 1024

O_FQ, O_FK, O_FV, O_FF = 0, 512, 1024, 1536
O_CQ, O_CKV, O_KR = 1664, 1920, 2048
O_SQ, O_SK, O_SV, O_G = 2176, 2688, 2816, 2944

SHARDED = ("meta_tokens", "w_in", "mla_w_q_up", "mla_w_kv_up", "w_branch", "w_o", "ffn_w_up", "ffn_conv_w",
           "ffn_w_down")
SHARD_AXIS = {"meta_tokens": 1, "w_in": 2, "mla_w_q_up": 2, "mla_w_kv_up": 2, "w_branch": 3, "w_o": 1,
              "ffn_w_up": 2, "ffn_conv_w": 2, "ffn_w_down": 1}
SHARD_SHAPE = {"meta_tokens": (16, 256), "w_in": (2, 1024, 1450), "mla_w_q_up": (2, 256, 192),
               "mla_w_kv_up": (2, 128, 256), "w_branch": (2, 3, 512, 256), "w_o": (2, 256, 1024),
               "ffn_w_up": (2, 1024, 1408), "ffn_conv_w": (2, 3, 1408), "ffn_w_down": (2, 704, 1024)}
BIG = ("w_in", "mla_w_q_up", "mla_w_kv_up", "w_branch", "w_o", "ffn_w_up", "ffn_w_down")
FINE = ("meta_tokens", "ffn_conv_w")
SMALL = ("norm1_g", "fox_forget_b", "fox_q_g", "fox_k_g", "mla_q_a_g", "mla_kv_a_g", "mla_q_g", "mla_k_g",
         "swa_q_g", "swa_k_g", "swa_sinks", "norm2_g", "ffn_conv_b")
WEIGHTS = ("meta_tokens", "norm1_g", "w_in", "fox_forget_b", "fox_q_g", "fox_k_g", "mla_q_a_g", "mla_w_q_up",
           "mla_kv_a_g", "mla_w_kv_up", "mla_q_g", "mla_k_g", "swa_q_g", "swa_k_g", "swa_sinks", "w_branch", "w_o",
           "norm2_g", "ffn_w_up", "ffn_conv_w", "ffn_conv_b", "ffn_w_down")
PACK_ROWS = 1024


def _cparams(sem):
    return pltpu.CompilerParams(dimension_semantics=sem, vmem_limit_bytes=VMEM_LIMIT)


def _div_tile(n, cap, mult=SUBLANES):
    best = None
    for t in range(mult, min(n, cap) + 1, mult):
        if n % t == 0:
            best = t
    return best if best is not None else n


def _rows_tile(n, width, budget=2 << 20):
    return _div_tile(n, max(SUBLANES, budget // (4 * max(width, LANES))))


def _op(fwd, bwd):
    @jax.custom_vjp
    def op(*args):
        return fwd(*args)[0]
    op.defvjp(fwd, bwd)
    return op


def _rms_fwd_call(x, g, denom, name):
    n, c = x.shape
    tr = _rows_tile(n, c)

    def body(x_ref, g_ref, y_ref):
        xv = x_ref[...]
        ms = jnp.sum(xv * xv, axis=-1, keepdims=True) * (1.0 / denom)
        y_ref[...] = xv * lax.rsqrt(ms + EPS) * g_ref[...]

    return pl.pallas_call(
        body, name=name, grid=(n // tr,),
        in_specs=[BS((tr, c), lambda i: (i, 0)), BS((1, c), lambda i: (0, 0))],
        out_specs=BS((tr, c), lambda i: (i, 0)), out_shape=SDS((n, c), F32),
        compiler_params=_cparams(("parallel",)))(x, g)


def _rms_bwd_call(x, g, dy, denom, name):
    n, c = x.shape
    tr = _rows_tile(n, c)

    def body(x_ref, g_ref, dy_ref, dx_ref, dg_ref):
        xv = x_ref[...]
        dy = dy_ref[...]
        ms = jnp.sum(xv * xv, axis=-1, keepdims=True) * (1.0 / denom)
        r = lax.rsqrt(ms + EPS)
        xh = xv * r
        dxh = dy * g_ref[...]
        dx_ref[...] = r * (dxh - xh * (jnp.sum(dxh * xh, axis=-1, keepdims=True) * (1.0 / denom)))

        @pl.when(pl.program_id(0) == 0)
        def _():
            dg_ref[...] = jnp.zeros_like(dg_ref)

        dg_ref[...] += jnp.sum(dy * xh, axis=0, keepdims=True)

    return pl.pallas_call(
        body, name=name, grid=(n // tr,),
        in_specs=[BS((tr, c), lambda i: (i, 0)), BS((1, c), lambda i: (0, 0)), BS((tr, c), lambda i: (i, 0))],
        out_specs=[BS((tr, c), lambda i: (i, 0)), BS((1, c), lambda i: (0, 0))],
        out_shape=[SDS((n, c), F32), SDS((1, c), F32)],
        compiler_params=_cparams(("arbitrary",)))(x, g, dy)


def rms_norm(x, g, denom, name):
    def fwd(x, g):
        return _rms_fwd_call(x, g, denom, name + "_f"), (x, g)

    def bwd(res, dy):
        return tuple(_rms_bwd_call(res[0], res[1], dy, denom, name + "_b"))

    return _op(fwd, bwd)(x, g)


def _mm_call(a, b, mode, res, name):
    if mode == "nn":
        (m, kc), n = a.shape, b.shape[1]
    elif mode == "nt":
        (m, kc), n = a.shape, b.shape[0]
    else:
        (kc, m), n = a.shape, b.shape[1]
    if mode == "tn":
        tk = _div_tile(kc, 1056)
        tm = _div_tile(m, 512, LANES)
        tn = _div_tile(n, 1024, LANES)
    else:
        tk = kc if kc <= 2816 else _div_tile(kc, 1024, LANES)
        tm = _div_tile(m, max(LANES, (9 << 19) // (4 * tk)))
        tn = _div_tile(n, 512, LANES)
    nk = kc // tk
    dims = {"nn": (((1,), (0,)), ((), ())), "nt": (((1,), (1,)), ((), ())), "tn": (((0,), (0,)), ((), ()))}[mode]

    def body(*refs):
        if res is None:
            a_ref, b_ref, o_ref, acc_ref = refs
            r_ref = None
        else:
            a_ref, b_ref, r_ref, o_ref, acc_ref = refs
        k = pl.program_id(2)

        @pl.when(k == 0)
        def _():
            acc_ref[...] = jnp.zeros_like(acc_ref)

        acc_ref[...] += lax.dot_general(a_ref[...].astype(BF16), b_ref[...].astype(BF16), dims,
                                        preferred_element_type=F32)

        @pl.when(k == nk - 1)
        def _():
            if r_ref is None:
                o_ref[...] = acc_ref[...]
            else:
                o_ref[...] = r_ref[...] + acc_ref[...]

    a_spec = BS((tk, tm), lambda i, j, k: (k, i)) if mode == "tn" else BS((tm, tk), lambda i, j, k: (i, k))
    b_spec = BS((tn, tk), lambda i, j, k: (j, k)) if mode == "nt" else BS((tk, tn), lambda i, j, k: (k, j))
    o_spec = BS((tm, tn), lambda i, j, k: (i, j))
    ins, args = [a_spec, b_spec], [a, b]
    if res is not None:
        ins.append(o_spec)
        args.append(res)
    return pl.pallas_call(
        body, name=name, grid=(m // tm, n // tn, nk), in_specs=ins, out_specs=o_spec,
        out_shape=SDS((m, n), F32), scratch_shapes=[pltpu.VMEM((tm, tn), F32)],
        compiler_params=_cparams(("parallel", "parallel", "arbitrary")))(*args)


def linear(a, w, eps, name, res=None):
    if res is None:
        def fwd(a, w, eps):
            return _mm_call(a, w, "nn", None, name + "_f"), (a, w)

        def bwd(r, dc):
            a, w = r
            return (_mm_call(dc, w, "nt", None, name + "_da"), jnp.zeros_like(w),
                    _mm_call(a, dc, "tn", None, name + "_dw"))

        return _op(fwd, bwd)(a, w, eps)

    def fwd_r(a, w, eps, res):
        return _mm_call(a, w, "nn", res, name + "_f"), (a, w)

    def bwd_r(r, dc):
        a, w = r
        return (_mm_call(dc, w, "nt", None, name + "_da"), jnp.zeros_like(w),
                _mm_call(a, dc, "tn", None, name + "_dw"), dc)

    return _op(fwd_r, bwd_r)(a, w, eps, res)


CT = 128


def _tri_dot(v, upper):
    r = lax.broadcasted_iota(jnp.int32, (CT, CT), 0)
    c = lax.broadcasted_iota(jnp.int32, (CT, CT), 1)
    tri = jnp.where((r <= c) if upper else (r >= c), 1.0, 0.0).astype(F32)
    return jnp.dot(v, tri, preferred_element_type=F32, precision=lax.Precision.HIGHEST)


def _gate_fwd_call(z, b, name):
    h, n = z.shape

    def body(z_ref, b_ref, c_ref, carry):
        @pl.when(pl.program_id(0) == 0)
        def _():
            carry[...] = jnp.zeros_like(carry)

        x = z_ref[...] + b_ref[...]
        ls = jnp.minimum(x, 0.0) - jnp.log(1.0 + jnp.exp(-jnp.abs(x)))
        c_ref[...] = _tri_dot(ls, True) + carry[...]
        carry[...] += jnp.sum(ls, axis=1, keepdims=True)

    return pl.pallas_call(
        body, name=name, grid=(n // CT,),
        in_specs=[BS((h, CT), lambda j: (0, j)), BS((h, 1), lambda j: (0, 0))],
        out_specs=BS((h, CT), lambda j: (0, j)), out_shape=SDS((h, n), F32),
        scratch_shapes=[pltpu.VMEM((h, 1), F32)],
        compiler_params=_cparams(("arbitrary",)))(z, b)


def _gate_bwd_call(z, b, dc, name):
    h, n = z.shape
    nt = n // CT

    def body(z_ref, b_ref, dc_ref, dz_ref, db_ref, carry):
        @pl.when(pl.program_id(0) == 0)
        def _():
            carry[...] = jnp.zeros_like(carry)
            db_ref[...] = jnp.zeros_like(db_ref)

        dcv = dc_ref[...]
        dls = _tri_dot(dcv, False) + carry[...]
        carry[...] += jnp.sum(dcv, axis=1, keepdims=True)
        x = z_ref[...] + b_ref[...]
        e = jnp.exp(-jnp.abs(x))
        dz = dls * jnp.where(x >= 0, e / (1.0 + e), 1.0 / (1.0 + e))
        dz_ref[...] = dz
        db_ref[...] += jnp.sum(dz, axis=1, keepdims=True)

    rev = lambda j: (0, nt - 1 - j)
    return pl.pallas_call(
        body, name=name, grid=(nt,),
        in_specs=[BS((h, CT), rev), BS((h, 1), lambda j: (0, 0)), BS((h, CT), rev)],
        out_specs=[BS((h, CT), rev), BS((h, 1), lambda j: (0, 0))],
        out_shape=[SDS((h, n), F32), SDS((h, 1), F32)],
        scratch_shapes=[pltpu.VMEM((h, 1), F32)],
        compiler_params=_cparams(("arbitrary",)))(z, b, dc)


def forget_cumsum(z, b, name):
    def fwd(z, b):
        return _gate_fwd_call(z, b, name + "_f"), (z, b)

    def bwd(res, dc):
        return tuple(_gate_bwd_call(res[0], res[1], dc, name + "_b"))

    return _op(fwd, bwd)(z, b)


def _rope_call(x, cos, sin, name):
    h, n, d = x.shape
    tr = _rows_tile(n, d)

    def body(x_ref, c_ref, s_ref, y_ref):
        xv = x_ref[0]
        lane = lax.broadcasted_iota(jnp.int32, xv.shape, 1)
        rot = jnp.where(lane < MLA_NOPE + MLA_ROPE // 2, -pltpu.roll(xv, d - MLA_ROPE // 2, 1),
                        pltpu.roll(xv, MLA_ROPE // 2, 1))
        y_ref[0] = xv * c_ref[...] + rot * s_ref[...]

    return pl.pallas_call(
        body, name=name, grid=(n // tr, h),
        in_specs=[BS((1, tr, d), lambda i, hh: (hh, i, 0)), BS((tr, d), lambda i, hh: (i, 0)),
                  BS((tr, d), lambda i, hh: (i, 0))],
        out_specs=BS((1, tr, d), lambda i, hh: (hh, i, 0)), out_shape=SDS((h, n, d), F32),
        compiler_params=_cparams(("parallel", "parallel")))(x, cos, sin)


def rope(x, cos, sin, name):
    def fwd(x, cos, sin):
        return _rope_call(x, cos, sin, name + "_f"), (cos, sin)

    def bwd(res, dy):
        cos, sin = res
        return _rope_call(dy, cos, -sin, name + "_b"), jnp.zeros_like(cos), jnp.zeros_like(sin)

    return _op(fwd, bwd)(x, cos, sin)


NT_DIMS = (((1,), (1,)), ((), ()))
TN_DIMS = (((0,), (0,)), ((), ()))
HEADS_PER_STEP = 2


def _causal_tile(n):
    return CAUSAL_TILE if n % CAUSAL_TILE == 0 else ROW_PAD


def _causal_fwd_call(q, k, v, ck_r, fox, scale, name):
    h, n, dk = q.shape
    dv = v.shape[2]
    t = _causal_tile(n)
    nq = n // t
    hb = HEADS_PER_STEP

    def body(*refs):
        if fox:
            q_ref, k_ref, v_ref, ck_ref, o_ref, lse_ref, m_scr, l_scr, acc_scr = refs
        else:
            q_ref, k_ref, v_ref, o_ref, lse_ref, m_scr, l_scr, acc_scr = refs
            ck_ref = None
        qi = pl.program_id(1)
        qbs = [q_ref[e].astype(BF16) for e in range(hb)]
        m_scr[...] = jnp.full(m_scr.shape, NEG, F32)
        l_scr[...] = jnp.zeros_like(l_scr)
        acc_scr[...] = jnp.zeros_like(acc_scr)

        def process(j, masked):
            off = pl.multiple_of(j * t, t)
            if masked:
                rows = lax.broadcasted_iota(jnp.int32, (t, t), 0)
                cols = lax.broadcasted_iota(jnp.int32, (t, t), 1)
                valid = cols <= rows
            for e in range(hb):
                kb = k_ref[e, pl.ds(off, t), :].astype(BF16)
                vb = v_ref[e, pl.ds(off, t), :].astype(BF16)
                s = lax.dot_general(qbs[e], kb, NT_DIMS, preferred_element_type=F32) * scale
                if fox:
                    s = s - ck_ref[e, j]
                if masked:
                    s = jnp.where(valid, s, NEG)
                m_old = m_scr[e]
                m_new = jnp.maximum(m_old, jnp.max(s, axis=1, keepdims=True))
                alpha = jnp.exp(m_old - m_new)
                p = jnp.exp(s - m_new)
                l_scr[e] = alpha * l_scr[e] + jnp.sum(p, axis=1, keepdims=True)
                acc_scr[e] = alpha * acc_scr[e] + jnp.dot(p.astype(BF16), vb, preferred_element_type=F32)
                m_scr[e] = m_new

        def step(j, carry):
            process(j, False)
            return carry

        lax.fori_loop(0, qi, step, 0)
        process(qi, True)
        for e in range(hb):
            l = l_scr[e]
            o_ref[e] = acc_scr[e] / l
            lse_ref[e] = m_scr[e] + jnp.log(l)

    ins = [BS((hb, t, dk), lambda a, b: (a, b, 0)), BS((hb, n, dk), lambda a, b: (a, 0, 0)),
           BS((hb, n, dv), lambda a, b: (a, 0, 0))]
    args = [q, k, v]
    if fox:
        ins.append(BS((hb, nq, 1, t), lambda a, b: (a, 0, 0, 0)))
        args.append(ck_r)
    return pl.pallas_call(
        body, name=name, grid=(h // hb, nq), in_specs=ins,
        out_specs=[BS((hb, t, dv), lambda a, b: (a, b, 0)), BS((hb, t, 1), lambda a, b: (a, b, 0))],
        out_shape=[SDS((h, n, dv), F32), SDS((h, n, 1), F32)],
        scratch_shapes=[pltpu.VMEM((hb, t, 1), F32), pltpu.VMEM((hb, t, 1), F32), pltpu.VMEM((hb, t, dv), F32)],
        compiler_params=_cparams(("parallel", "arbitrary")))(*args)


def _causal_bwd_call(q, k, v, do, lse_r, delta_r, ck_c, fox, scale, name):
    h, n, dk = q.shape
    dv = v.shape[2]
    t = _causal_tile(n)
    nq = n // t
    hb = HEADS_PER_STEP

    def body(*refs):
        it = iter(refs)
        q_ref, k_ref, v_ref, do_ref, lse_ref, delta_ref = (next(it) for _ in range(6))
        ck_ref = next(it) if fox else None
        dq_ref, dk_ref, dv_ref = next(it), next(it), next(it)
        dck_ref, dcq_ref = (next(it), next(it)) if fox else (None, None)
        dk_scr, dv_scr = next(it), next(it)
        dck_scr = next(it) if fox else None
        kj = pl.program_id(1)

        @pl.when(kj == 0)
        def _():
            dq_ref[...] = jnp.zeros_like(dq_ref)
            if fox:
                dcq_ref[...] = jnp.zeros_like(dcq_ref)

        kbs = [k_ref[e].astype(BF16) for e in range(hb)]
        vbs = [v_ref[e].astype(BF16) for e in range(hb)]
        dk_scr[...] = jnp.zeros_like(dk_scr)
        dv_scr[...] = jnp.zeros_like(dv_scr)
        if fox:
            dck_scr[...] = jnp.zeros_like(dck_scr)

        def process(qi, masked):
            off = pl.multiple_of(qi * t, t)
            if masked:
                krows = lax.broadcasted_iota(jnp.int32, (t, t), 0)
                qcols = lax.broadcasted_iota(jnp.int32, (t, t), 1)
                valid = krows <= qcols
            for e in range(hb):
                qb = q_ref[e, pl.ds(off, t), :].astype(BF16)
                dob = do_ref[e, pl.ds(off, t), :].astype(BF16)
                st = lax.dot_general(kbs[e], qb, NT_DIMS, preferred_element_type=F32) * scale
                if fox:
                    st = st - ck_ref[e]
                pt = jnp.exp(st - lse_ref[e, qi])
                if masked:
                    pt = jnp.where(valid, pt, 0.0)
                dv_scr[e] += jnp.dot(pt.astype(BF16), dob, preferred_element_type=F32)
                dpt = lax.dot_general(vbs[e], dob, NT_DIMS, preferred_element_type=F32)
                dst = pt * (dpt - delta_ref[e, qi])
                if fox:
                    dck_scr[e] -= jnp.sum(dst, axis=1, keepdims=True)
                    dcq_ref[e, qi] += jnp.sum(dst, axis=0, keepdims=True)
                dsb = (dst * scale).astype(BF16)
                dk_scr[e] += jnp.dot(dsb, qb, preferred_element_type=F32)
                dq_ref[e, pl.ds(off, t), :] += lax.dot_general(dsb, kbs[e], TN_DIMS, preferred_element_type=F32)

        def step(qi, carry):
            process(qi, False)
            return carry

        process(kj, True)
        lax.fori_loop(kj + 1, nq, step, 0)
        dk_ref[...] = dk_scr[...]
        dv_ref[...] = dv_scr[...]
        if fox:
            dck_ref[...] = dck_scr[...]

    whole = lambda a, b: (a, 0, 0)
    tile = lambda a, b: (a, b, 0)
    rowv = lambda a, b: (a, 0, 0, 0)
    ins = [BS((hb, n, dk), whole), BS((hb, t, dk), tile), BS((hb, t, dv), tile), BS((hb, n, dv), whole),
           BS((hb, nq, 1, t), rowv), BS((hb, nq, 1, t), rowv)]
    args = [q, k, v, do, lse_r, delta_r]
    outs = [BS((hb, n, dk), whole), BS((hb, t, dk), tile), BS((hb, t, dv), tile)]
    oshape = [SDS((h, n, dk), F32), SDS((h, n, dk), F32), SDS((h, n, dv), F32)]
    scratch = [pltpu.VMEM((hb, t, dk), F32), pltpu.VMEM((hb, t, dv), F32)]
    if fox:
        ins.append(BS((hb, t, 1), tile))
        args.append(ck_c)
        outs += [BS((hb, t, 1), tile), BS((hb, nq, 1, t), rowv)]
        oshape += [SDS((h, n, 1), F32), SDS((h, nq, 1, t), F32)]
        scratch.append(pltpu.VMEM((hb, t, 1), F32))
    return pl.pallas_call(
        body, name=name, grid=(h // hb, nq), in_specs=ins, out_specs=outs, out_shape=oshape, scratch_shapes=scratch,
        compiler_params=_cparams(("arbitrary", "arbitrary")))(*args)


def causal_attention(q, k, v, c, scale, name):
    h, n, _ = q.shape
    t = _causal_tile(n)
    nq = n // t
    dv = v.shape[2]
    fox = c is not None

    def run_fwd(q, k, v, c):
        ck_r = c.reshape(h, nq, 1, t) if fox else None
        o, lse = _causal_fwd_call(q, k, v, ck_r, fox, scale, name + "_f")
        return o, (q, k, v, c, o, lse)

    def run_bwd(res, do):
        q, k, v, c, o, lse = res
        delta = _rowdot_call(do.reshape(h * n, dv), o.reshape(h * n, dv), name + "_dl")
        ck_c = c.reshape(h, n, 1) if fox else None
        outs = _causal_bwd_call(q, k, v, do, lse.reshape(h, nq, 1, t), delta.reshape(h, nq, 1, t), ck_c, fox, scale,
                                name + "_b")
        return outs[0], outs[1], outs[2], (outs[3].reshape(h, n) + outs[4].reshape(h, n) if fox else None)

    return _op(run_fwd, run_bwd)(q, k, v, c)


SWA_T = 128


def _swa_masks(qi):
    t = SWA_T
    r = lax.broadcasted_iota(jnp.int32, (t, 3 * t), 0)
    c = lax.broadcasted_iota(jnp.int32, (t, 3 * t), 1)
    seg0 = c < t
    seg1 = (c >= t) & (c < 2 * t)
    jp = jnp.maximum(qi - 1, 0)
    kpos = jnp.where(seg0, c, jnp.where(seg1, jp * t + c - t, qi * t + c - 2 * t))
    dist = qi * t + r - kpos
    band = (dist >= 0) & ((dist < WINDOW) | (kpos < N_META))
    valid = (seg0 & (kpos < N_META) & (qi >= 2)) | (jnp.logical_not(seg0) & band & (jnp.logical_not(seg1) | (qi >= 1)))
    return valid, dist.astype(F32)


def _swa_cat(ref, qi):
    t = SWA_T
    jp = jnp.maximum(qi - 1, 0)
    return jnp.concatenate([ref[0, 0:t, :], ref[0, pl.ds(pl.multiple_of(jp * t, t), t), :],
                            ref[0, pl.ds(pl.multiple_of(qi * t, t), t), :]], axis=0).astype(BF16)


def _swa_fwd_call(q, k, v, sinks, slopes, scale, name):
    hq, n, d = q.shape
    hkv = k.shape[0]
    g = hq // hkv
    t = SWA_T
    nq = n // t

    def body(q_ref, k_ref, v_ref, sink_ref, slope_ref, o_ref, lse_ref):
        grp = pl.program_id(0)
        qi = pl.program_id(1)
        valid, dist = _swa_masks(qi)
        kc = _swa_cat(k_ref, qi)
        vc = _swa_cat(v_ref, qi)
        qs = jnp.concatenate([q_ref[e] for e in range(g)], axis=0).astype(BF16)
        s_all = lax.dot_general(qs, kc, NT_DIMS, preferred_element_type=F32) * scale
        ps, ls, ms = [], [], []
        for e in range(g):
            hh = grp * g + e
            s = jnp.where(valid, s_all[e * t:(e + 1) * t] - slope_ref[hh] * dist, NEG)
            m = jnp.maximum(jnp.max(s, axis=1, keepdims=True), sink_ref[hh])
            p = jnp.exp(s - m)
            ls.append(jnp.sum(p, axis=1, keepdims=True) + jnp.exp(sink_ref[hh] - m))
            ms.append(m)
            ps.append(p.astype(BF16))
        acc = jnp.dot(jnp.concatenate(ps, axis=0), vc, preferred_element_type=F32)
        for e in range(g):
            o_ref[e] = acc[e * t:(e + 1) * t] / ls[e]
            lse_ref[e] = ms[e] + jnp.log(ls[e])

    return pl.pallas_call(
        body, name=name, grid=(hkv, nq),
        in_specs=[BS((g, t, d), lambda a, b: (a, b, 0)), BS((1, n, d), lambda a, b: (a, 0, 0)),
                  BS((1, n, d), lambda a, b: (a, 0, 0)), BS(memory_space=pltpu.SMEM), BS(memory_space=pltpu.SMEM)],
        out_specs=[BS((g, t, d), lambda a, b: (a, b, 0)), BS((g, t, 1), lambda a, b: (a, b, 0))],
        out_shape=[SDS((hq, n, d), F32), SDS((hq, n, 1), F32)],
        compiler_params=_cparams(("parallel", "parallel")))(q, k, v, sinks, slopes)


def _swa_bwd_call(q, k, v, o, lse, do, sinks, slopes, scale, name):
    hq, n, d = q.shape
    hkv = k.shape[0]
    g = hq // hkv
    t = SWA_T
    nq = n // t

    def body(q_ref, k_ref, v_ref, o_ref, lse_ref, do_ref, sink_ref, slope_ref, dq_ref, dk_ref, dv_ref, ds_ref):
        grp = pl.program_id(0)
        qi = pl.program_id(1)

        @pl.when(qi == 0)
        def _():
            dk_ref[...] = jnp.zeros_like(dk_ref)
            dv_ref[...] = jnp.zeros_like(dv_ref)
            ds_ref[...] = jnp.zeros_like(ds_ref)

        valid, dist = _swa_masks(qi)
        kc = _swa_cat(k_ref, qi)
        vc = _swa_cat(v_ref, qi)
        qs = jnp.concatenate([q_ref[e] for e in range(g)], axis=0).astype(BF16)
        dos = jnp.concatenate([do_ref[e] for e in range(g)], axis=0).astype(BF16)
        s_all = lax.dot_general(qs, kc, NT_DIMS, preferred_element_type=F32) * scale
        dp_all = lax.dot_general(dos, vc, NT_DIMS, preferred_element_type=F32)
        ps, dss = [], []
        for e in range(g):
            hh = grp * g + e
            lse_e = lse_ref[e]
            delta = jnp.sum(do_ref[e] * o_ref[e], axis=1, keepdims=True)
            s = s_all[e * t:(e + 1) * t] - slope_ref[hh] * dist
            p = jnp.where(valid, jnp.exp(s - lse_e), 0.0)
            ds = p * (dp_all[e * t:(e + 1) * t] - delta)
            ps.append(p.astype(BF16))
            dss.append((ds * scale).astype(BF16))
            ds_ref[e] += -jnp.sum(jnp.exp(sink_ref[hh] - lse_e) * delta)
        p_st = jnp.concatenate(ps, axis=0)
        ds_st = jnp.concatenate(dss, axis=0)
        dq = jnp.dot(ds_st, kc, preferred_element_type=F32)
        for e in range(g):
            dq_ref[e] = dq[e * t:(e + 1) * t]
        dkc = lax.dot_general(ds_st, qs, TN_DIMS, preferred_element_type=F32)
        dvc = lax.dot_general(p_st, dos, TN_DIMS, preferred_element_type=F32)
        jp = jnp.maximum(qi - 1, 0)
        for seg, off in enumerate((0, pl.multiple_of(jp * t, t), pl.multiple_of(qi * t, t))):
            dk_ref[0, pl.ds(off, t), :] += dkc[seg * t:(seg + 1) * t]
            dv_ref[0, pl.ds(off, t), :] += dvc[seg * t:(seg + 1) * t]

    tile = lambda a, b: (a, b, 0)
    whole = lambda a, b: (a, 0, 0)
    return pl.pallas_call(
        body, name=name, grid=(hkv, nq),
        in_specs=[BS((g, t, d), tile), BS((1, n, d), whole), BS((1, n, d), whole), BS((g, t, d), tile),
                  BS((g, t, 1), tile), BS((g, t, d), tile), BS(memory_space=pltpu.SMEM), BS(memory_space=pltpu.SMEM)],
        out_specs=[BS((g, t, d), tile), BS((1, n, d), whole), BS((1, n, d), whole), BS((g, 1, LANES), whole)],
        out_shape=[SDS((hq, n, d), F32), SDS((hkv, n, d), F32), SDS((hkv, n, d), F32), SDS((hq, 1, LANES), F32)],
        compiler_params=_cparams(("arbitrary", "arbitrary")))(q, k, v, o, lse, do, sinks, slopes)


def window_attention(q, k, v, sinks, slopes, scale, name):
    def run_fwd(q, k, v, sinks, slopes):
        o, lse = _swa_fwd_call(q, k, v, sinks, slopes, scale, name + "_f")
        return o, (q, k, v, sinks, slopes, o, lse)

    def run_bwd(res, do):
        q, k, v, sinks, slopes, o, lse = res
        dq, dk, dv, ds = _swa_bwd_call(q, k, v, o, lse, do, sinks, slopes, scale, name + "_b")
        return dq, dk, dv, ds[:, 0, 0], jnp.zeros_like(slopes)

    return _op(run_fwd, run_bwd)(q, k, v, sinks, slopes)


def _rowdot_call(a, b, name):
    n, d = a.shape
    tr = _rows_tile(n, d)

    def body(a_ref, b_ref, o_ref):
        o_ref[...] = jnp.sum(a_ref[...] * b_ref[...], axis=-1, keepdims=True)

    return pl.pallas_call(
        body, name=name, grid=(n // tr,),
        in_specs=[BS((tr, d), lambda i: (i, 0)), BS((tr, d), lambda i: (i, 0))],
        out_specs=BS((tr, 1), lambda i: (i, 0)), out_shape=SDS((n, 1), F32),
        compiler_params=_cparams(("parallel",)))(a, b)


def _sigmoid(x):
    return 1.0 / (1.0 + jnp.exp(-x))


def _merge_fwd_call(gs, ys, name):
    n, c = ys[0].shape
    tr = _rows_tile(n, c, 1 << 20)

    def body(g0, g1, g2, y0, y1, y2, m_ref):
        m_ref[...] = (_sigmoid(g0[...]) * y0[...] + _sigmoid(g1[...]) * y1[...]) + _sigmoid(g2[...]) * y2[...]

    spec = BS((tr, c), lambda i: (i, 0))
    return pl.pallas_call(
        body, name=name, grid=(n // tr,), in_specs=[spec] * 6, out_specs=spec, out_shape=SDS((n, c), F32),
        compiler_params=_cparams(("parallel",)))(*gs, *ys)


def _merge_bwd_call(gs, ys, dm, name):
    n, c = ys[0].shape
    tr = _rows_tile(n, c, 1 << 20)

    def body(g0, g1, g2, y0, y1, y2, dm_ref, dg0, dg1, dg2, dy0, dy1, dy2):
        d = dm_ref[...]
        for g, y, dg, dy in ((g0, y0, dg0, dy0), (g1, y1, dg1, dy1), (g2, y2, dg2, dy2)):
            s = _sigmoid(g[...])
            dy[...] = d * s
            dg[...] = d * y[...] * (s * (1.0 - s))

    spec = BS((tr, c), lambda i: (i, 0))
    return pl.pallas_call(
        body, name=name, grid=(n // tr,), in_specs=[spec] * 7, out_specs=[spec] * 6,
        out_shape=[SDS((n, c), F32)] * 6, compiler_params=_cparams(("parallel",)))(*gs, *ys, dm)


def gated_merge(gs, ys, name):
    def fwd(gs, ys):
        return _merge_fwd_call(gs, ys, name + "_f"), (gs, ys)

    def bwd(res, dm):
        out = _merge_bwd_call(res[0], res[1], dm, name + "_b")
        return tuple(out[:3]), tuple(out[3:])

    return _op(fwd, bwd)(tuple(gs), tuple(ys))


CONV_TR = 264
CONV_TC = 1408


def _conv_tiles(n, f):
    tr = CONV_TR if n % CONV_TR == 0 else _div_tile(n, CONV_TR)
    tc = CONV_TC if f % CONV_TC == 0 else f
    return tr, tc


def _shift_down(cur, halo, first, tr):
    halo = jnp.where(first, 0.0, halo)
    row = lax.broadcasted_iota(jnp.int32, cur.shape, 0)
    h7, h6 = halo[7:8, :], halo[6:7, :]
    u1 = jnp.where(row == 0, h7, pltpu.roll(cur, 1, 0))
    u2 = jnp.where(row == 0, h6, jnp.where(row == 1, h7, pltpu.roll(cur, 2, 0)))
    return u1, u2


def _conv_lin(cur, u1, u2, w_ref, b_ref):
    return ((b_ref[...] + w_ref[0:1, :] * u2) + w_ref[1:2, :] * u1) + w_ref[2:3, :] * cur


def _conv_in_specs(tr, tc, nj):
    sub = tr // SUBLANES
    prev = lambda j, i: (jnp.maximum(i * sub - 1, 0), j)
    prev_v = lambda j, i: (jnp.maximum(i * sub - 1, 0), j + nj)
    return [BS((tr, tc), lambda j, i: (i, j)), BS((SUBLANES, tc), prev),
            BS((tr, tc), lambda j, i: (i, j + nj)), BS((SUBLANES, tc), prev_v),
            BS((3, tc), lambda j, i: (0, j)), BS((3, tc), lambda j, i: (0, j + nj)),
            BS((1, tc), lambda j, i: (0, j)), BS((1, tc), lambda j, i: (0, j + nj))]


def _conv_fwd_call(u, cw, cb, name):
    n, f2 = u.shape
    f = f2 // 2
    tr, tc = _conv_tiles(n, f)
    nj = f // tc

    def body(ug, ugh, uv, uvh, wg, wv, bg, bv, a_ref):
        first = pl.program_id(1) == 0
        g1, g2 = _shift_down(ug[...], ugh[...], first, tr)
        v1, v2 = _shift_down(uv[...], uvh[...], first, tr)
        cg = _conv_lin(ug[...], g1, g2, wg, bg)
        cv = _conv_lin(uv[...], v1, v2, wv, bv)
        a_ref[...] = cg * _sigmoid(cg) * cv

    return pl.pallas_call(
        body, name=name, grid=(nj, n // tr), in_specs=_conv_in_specs(tr, tc, nj),
        out_specs=BS((tr, tc), lambda j, i: (i, j)), out_shape=SDS((n, f), F32),
        compiler_params=_cparams(("parallel", "parallel")))(u, u, u, u, cw, cw, cb, cb)


def _conv_bwd_dc_call(u, cw, cb, da, name):
    n, f2 = u.shape
    f = f2 // 2
    tr, tc = _conv_tiles(n, f)
    nj = f // tc

    def body(ug, ugh, uv, uvh, wg, wv, bg, bv, da_ref, dc_ref, dw_ref, db_ref):
        first = pl.program_id(1) == 0
        g0, v0 = ug[...], uv[...]
        g1, g2 = _shift_down(g0, ugh[...], first, tr)
        v1, v2 = _shift_down(v0, uvh[...], first, tr)
        cg = _conv_lin(g0, g1, g2, wg, bg)
        cv = _conv_lin(v0, v1, v2, wv, bv)
        d = da_ref[...]
        s = _sigmoid(cg)
        dcg = d * cv * (s * (1.0 + cg * (1.0 - s)))
        dcv = d * (cg * s)
        dc_ref[0] = dcg
        dc_ref[1] = dcv

        @pl.when(first)
        def _():
            dw_ref[...] = jnp.zeros_like(dw_ref)
            db_ref[...] = jnp.zeros_like(db_ref)

        for p, dc, taps in ((0, dcg, (g2, g1, g0)), (1, dcv, (v2, v1, v0))):
            for t in range(3):
                dw_ref[p, t:t + 1, :] += jnp.sum(dc * taps[t], axis=0, keepdims=True)
            db_ref[p] += jnp.sum(dc, axis=0, keepdims=True)

    return pl.pallas_call(
        body, name=name, grid=(nj, n // tr),
        in_specs=_conv_in_specs(tr, tc, nj) + [BS((tr, tc), lambda j, i: (i, j))],
        out_specs=[BS((2, tr, tc), lambda j, i: (0, i, j)), BS((2, 3, tc), lambda j, i: (0, 0, j)),
                   BS((2, 1, tc), lambda j, i: (0, 0, j))],
        out_shape=[SDS((2, n, f), F32), SDS((2, 3, f), F32), SDS((2, 1, f), F32)],
        compiler_params=_cparams(("arbitrary", "arbitrary")))(u, u, u, u, cw, cw, cb, cb, da)


def _conv_bwd_du_call(dc, cw, name):
    _, n, f = dc.shape
    tr, tc = _conv_tiles(n, f)
    nj = f // tc
    ni = n // tr
    sub = tr // SUBLANES

    def body(c_ref, nx_ref, w_ref, du_ref):
        cur = c_ref[0]
        nxt = jnp.where(pl.program_id(2) == ni - 1, 0.0, nx_ref[0])
        row = lax.broadcasted_iota(jnp.int32, cur.shape, 0)
        n0, n1 = nxt[0:1, :], nxt[1:2, :]
        d1 = jnp.where(row == tr - 1, n0, pltpu.roll(cur, tr - 1, 0))
        d2 = jnp.where(row == tr - 1, n1, jnp.where(row == tr - 2, n0, pltpu.roll(cur, tr - 2, 0)))
        du_ref[...] = (w_ref[2:3, :] * cur + w_ref[1:2, :] * d1) + w_ref[0:1, :] * d2

    nxt_map = lambda p, j, i: (p, jnp.minimum((i + 1) * sub, n // SUBLANES - 1), j)
    return pl.pallas_call(
        body, name=name, grid=(2, nj, ni),
        in_specs=[BS((1, tr, tc), lambda p, j, i: (p, i, j)), BS((1, SUBLANES, tc), nxt_map),
                  BS((3, tc), lambda p, j, i: (0, p * nj + j))],
        out_specs=BS((tr, tc), lambda p, j, i: (i, p * nj + j)), out_shape=SDS((n, 2 * f), F32),
        compiler_params=_cparams(("parallel", "parallel", "parallel")))(dc, dc, cw)


def conv_glu(u, cw, cb, name):
    def fwd(u, cw, cb):
        return _conv_fwd_call(u, cw, cb, name + "_f"), (u, cw, cb)

    def bwd(res, da):
        u, cw, cb = res
        dc, dw, db = _conv_bwd_dc_call(u, cw, cb, da, name + "_bc")
        du = _conv_bwd_du_call(dc, cw, name + "_bu")
        return du, jnp.concatenate([dw[0], dw[1]], axis=-1), jnp.concatenate([db[0], db[1]], axis=-1)

    return _op(fwd, bwd)(u, cw, cb)


def _loss_call(y, t, n_real, name):
    n, c = y.shape
    tr = _rows_tile(n, c, 1 << 20)

    def body(y_ref, t_ref, dy_ref, l_ref):
        i = pl.program_id(0)
        row = i * tr + lax.broadcasted_iota(jnp.int32, (tr, c), 0)
        real = (row >= N_META) & (row < N_META + n_real)
        e = jnp.where(real, y_ref[...] - t_ref[...], 0.0)
        dy_ref[...] = e * (1.0 / c)

        @pl.when(i == 0)
        def _():
            l_ref[...] = jnp.zeros_like(l_ref)

        l_ref[...] += 0.5 * jnp.sum(jnp.sum(e * e, axis=-1, keepdims=True) * (1.0 / c), axis=0, keepdims=True)

    spec = BS((tr, c), lambda i: (i, 0))
    return pl.pallas_call(
        body, name=name, grid=(n // tr,), in_specs=[spec, spec],
        out_specs=[spec, BS((1, 1), lambda i: (0, 0))], out_shape=[SDS((n, c), F32), SDS((1, 1), F32)],
        compiler_params=_cparams(("arbitrary",)))(y, t)


def _to_heads(x, nh):
    n = x.shape[0]
    return x.reshape(n, nh, x.shape[1] // nh).transpose(1, 0, 2)


def _from_heads(x):
    h, n, d = x.shape
    return x.transpose(1, 0, 2).reshape(n, h * d)


def _head_norm(x, g, denom, name):
    h, n, d = x.shape
    return rms_norm(x.reshape(h * n, d), g, denom, name).reshape(h, n, d)


def _pad_in_cols(w):
    z = lambda k: jnp.zeros(w.shape[:-1] + (k,), w.dtype)
    return jnp.concatenate([w[..., :1544], z(120), w[..., 1544:1960], z(96), w[..., 1960:], z(128)], axis=-1)


def _pad_q_up(w):
    s = w.shape[:-1]
    w = w.reshape(s + (HEADS, MLA_QK))
    w = jnp.concatenate([w, jnp.zeros(s + (HEADS, LANES - MLA_QK), w.dtype)], axis=-1)
    return w.reshape(s + (HEADS * LANES,))


def _assemble(shards):
    full = {k: jnp.concatenate([v[i] for i in range(N_CHIPS)], axis=SHARD_AXIS[k]) for k, v in shards.items()}
    out = dict(full)
    if "w_in" in out:
        out["w_in"] = _pad_in_cols(out["w_in"])
    if "mla_w_q_up" in out:
        out["mla_w_q_up"] = _pad_q_up(out["mla_w_q_up"])
    return out


def _rope_tables(n):
    half = MLA_ROPE // 2
    freqs = ROPE_THETA ** (-jnp.arange(half, dtype=F32) / half)
    ang = jnp.arange(n).astype(F32)[:, None] * freqs[None, :]
    cos, sin = jnp.cos(ang), jnp.sin(ang)
    one, zero = jnp.ones((n, MLA_NOPE), F32), jnp.zeros((n, MLA_NOPE), F32)
    tail1, tail0 = jnp.ones((n, LANES - MLA_QK), F32), jnp.zeros((n, LANES - MLA_QK), F32)
    return (jnp.concatenate([one, cos, cos, tail1], axis=1), jnp.concatenate([zero, sin, sin, tail0], axis=1))


def _pad_lanes(g, width):
    return jnp.concatenate([g, jnp.zeros((width - g.shape[0],), g.dtype)]).reshape(1, width)


def _trunk(eps, fine, small, x, wb):
    seq = x.shape[0]
    n = -(-(N_META + seq) // ROW_PAD) * ROW_PAD
    ew = _assemble(eps)
    cos, sin = _rope_tables(n)
    slopes = jnp.exp2(-8.0 * jnp.arange(1, HEADS + 1, dtype=F32) / HEADS)
    h = jnp.concatenate([fine["meta_tokens"], x, jnp.zeros((n - N_META - seq, D_MODEL), F32)], axis=0)
    for l in range(DEPTH):
        p = f"l{l}_"
        row = lambda name: small[name][l].reshape(1, -1)
        xn = rms_norm(h, row("norm1_g"), D_MODEL, p + "norm1")
        proj = linear(xn, wb["w_in"][l], ew["w_in"][l], p + "win")
        fq = _head_norm(_to_heads(proj[:, O_FQ:O_FQ + 512], HEADS), row("fox_q_g"), HEAD_DIM, p + "fqn")
        fk = _head_norm(_to_heads(proj[:, O_FK:O_FK + 512], HEADS), row("fox_k_g"), HEAD_DIM, p + "fkn")
        fv = _to_heads(proj[:, O_FV:O_FV + 512], HEADS)
        c = forget_cumsum(proj[:, O_FF:O_FF + HEADS].T, small["fox_forget_b"][l].reshape(HEADS, 1), p + "fgate")
        out_a = causal_attention(fq, fk, fv, c, HEAD_DIM ** -0.5, p + "fox")
        cqn = rms_norm(proj[:, O_CQ:O_CQ + MLA_Q_RANK], row("mla_q_a_g"), MLA_Q_RANK, p + "cqn")
        q = _to_heads(linear(cqn, wb["mla_w_q_up"][l], ew["mla_w_q_up"][l], p + "qup"), HEADS)
        q = rope(_head_norm(q, _pad_lanes(small["mla_q_g"][l], LANES), MLA_QK, p + "mqn"), cos, sin, p + "qrope")
        ckvn = rms_norm(proj[:, O_CKV:O_CKV + MLA_KV_RANK], row("mla_kv_a_g"), MLA_KV_RANK, p + "ckvn")
        kv = _to_heads(linear(ckvn, wb["mla_w_kv_up"][l], ew["mla_w_kv_up"][l], p + "kvup"), HEADS)
        kr = jnp.broadcast_to(proj[None, :, O_KR:O_KR + MLA_ROPE], (HEADS, n, MLA_ROPE))
        k = jnp.concatenate([kv[..., :MLA_NOPE], kr, jnp.zeros((HEADS, n, LANES - MLA_QK), F32)], axis=-1)
        k = rope(_head_norm(k, _pad_lanes(small["mla_k_g"][l], LANES), MLA_QK, p + "mkn"), cos, sin, p + "krope")
        out_b = causal_attention(q, k, kv[..., MLA_NOPE:], None, MLA_QK ** -0.5, p + "mla")
        sq = _head_norm(_to_heads(proj[:, O_SQ:O_SQ + 512], HEADS), row("swa_q_g"), HEAD_DIM, p + "sqn")
        sk = _head_norm(_to_heads(proj[:, O_SK:O_SK + 128], SWA_KV_HEADS), row("swa_k_g"), HEAD_DIM, p + "skn")
        sv = _to_heads(proj[:, O_SV:O_SV + 128], SWA_KV_HEADS)
        out_c = window_attention(sq, sk, sv, small["swa_sinks"][l], slopes, HEAD_DIM ** -0.5, p + "swa")
        ys = [linear(_from_heads(o), wb["w_branch"][l, i], ew["w_branch"][l, i], p + f"br{i}")
              for i, o in enumerate((out_a, out_b, out_c))]
        gs = [proj[:, O_G + i * D_MODEL:O_G + (i + 1) * D_MODEL] for i in range(N_BRANCH)]
        merged = gated_merge(gs, ys, p + "merge")
        h = linear(merged, wb["w_o"][l], ew["w_o"][l], p + "wo", res=h)
        xn2 = rms_norm(h, row("norm2_g"), D_MODEL, p + "norm2")
        u = linear(xn2, wb["ffn_w_up"][l], ew["ffn_w_up"][l], p + "wup")
        act = conv_glu(u, fine["ffn_conv_w"][l], row("ffn_conv_b"), p + "conv")
        h = linear(act, wb["ffn_w_down"][l], ew["ffn_w_down"][l], p + "wdown", res=h)
    return h


def _local_step(x, target, wb, fine, small):
    seq = x.shape[0]
    eps = {k: jnp.zeros((N_CHIPS,) + SHARD_SHAPE[k], F32) for k in BIG}
    y, vjp = jax.vjp(lambda e, f, s, xx: _trunk(e, f, s, xx, wb), eps, fine, small, x)
    n = y.shape[0]
    tpad = jnp.concatenate([jnp.zeros((N_META, D_MODEL), F32), target, jnp.zeros((n - N_META - seq, D_MODEL), F32)])
    dy, loss = _loss_call(y, tpad, seq, "loss")
    g_eps, g_fine, g_small, g_x = vjp(dy)
    return loss[0, 0], g_x, g_eps, g_fine, g_small


def _pack_rows(shapes, mult):
    total = sum(_size(s) for s in shapes)
    rows = -(-total // LANES)
    return -(-rows // mult) * mult


def _size(shape):
    n = 1
    for d in shape:
        n *= d
    return n


def _pack(arrs, rows, dtype):
    flat = [a.reshape(-1).astype(dtype) for a in arrs]
    used = sum(a.size for a in flat)
    flat.append(jnp.zeros((rows * LANES - used,), dtype))
    return jnp.concatenate(flat).reshape(rows, LANES)


def _unpack(p, shapes):
    flat = p.reshape(-1)
    out, off = [], 0
    for s in shapes:
        out.append(flat[off:off + _size(s)].reshape(s))
        off += _size(s)
    return out


MESH = pl.DeviceIdType.MESH
ANY = pl.BlockSpec(memory_space=pl.ANY)


def _me():
    return lax.axis_index("x"), lax.axis_index("y"), lax.axis_index("c")


def _remote(src, dst, send_sems, recv_sems, idx, dev):
    return pltpu.make_async_remote_copy(src_ref=src, dst_ref=dst, send_sem=send_sems.at[idx], recv_sem=recv_sems.at[idx],
                                        device_id=dev, device_id_type=MESH)


def _all_gather(packs, name):
    npk = len(packs)

    def body(*refs):
        ins, outs = refs[:npk], refs[npk:2 * npk]
        send_sems, recv_sems = refs[2 * npk:]
        x, y, c = _me()
        j = 2 * x + y
        sibling = (x, y, 1 - c)
        chips = [(1 - x, y), (x, 1 - y), (1 - x, 1 - y)]
        sends = []
        for p in range(npk):
            half = packs[p].shape[0] // 2
            mine = pl.ds(pl.multiple_of(c * half, 16), half)
            for r, (cx, cy) in enumerate(chips):
                cp = _remote(ins[p].at[mine], outs[p].at[j, mine], send_sems, recv_sems, 6 * p + r, (cx, cy, c))
                cp.start()
                sends.append(cp)
        for p in range(npk):
            half = packs[p].shape[0] // 2
            mine = pl.ds(pl.multiple_of(c * half, 16), half)
            for r, (cx, cy) in enumerate(chips):
                blk = outs[p].at[2 * cx + cy, mine]
                _remote(blk, blk, send_sems, recv_sems, 6 * p + r, sibling).wait_recv()
                fw = _remote(blk, blk, send_sems, recv_sems, 6 * p + 3 + r, sibling)
                fw.start()
                sends.append(fw)
        for p in range(npk):
            half = packs[p].shape[0] // 2
            other = pl.ds(pl.multiple_of((1 - c) * half, 16), half)
            for r, (cx, cy) in enumerate(chips):
                blk = outs[p].at[2 * cx + cy, other]
                _remote(blk, blk, send_sems, recv_sems, 6 * p + 3 + r, sibling).wait_recv()
        for cp in sends:
            cp.wait_send()

    outs = pl.pallas_call(
        body, name=name, in_specs=[ANY] * npk, out_specs=[ANY] * npk,
        out_shape=[SDS((N_CHIPS,) + p.shape, p.dtype) for p in packs],
        scratch_shapes=[pltpu.SemaphoreType.DMA((6 * npk,)), pltpu.SemaphoreType.DMA((6 * npk,))],
        compiler_params=pltpu.CompilerParams(has_side_effects=True))(*packs)
    j = 2 * lax.axis_index("x") + lax.axis_index("y")
    return [lax.dynamic_update_index_in_dim(o, p, j, 0) for o, p in zip(outs, packs)]


def _pair_exchange(g, name):
    _, rows, _ = g.shape
    half = rows // 2

    def body(g_ref, o_ref, send_sem, recv_sem):
        x, y, c = _me()
        other = pl.ds(pl.multiple_of((1 - c) * half, 16), half)
        cp = _remote(g_ref.at[:, other], o_ref, send_sem, recv_sem, 0, (x, y, 1 - c))
        cp.start()
        cp.wait()

    return pl.pallas_call(
        body, name=name, in_specs=[ANY], out_specs=ANY, out_shape=SDS((N_CHIPS, half, LANES), g.dtype),
        scratch_shapes=[pltpu.SemaphoreType.DMA((1,)), pltpu.SemaphoreType.DMA((1,))],
        compiler_params=pltpu.CompilerParams(has_side_effects=True))(g)


def _chip_exchange(s1, small, name):
    _, half, _ = s1.shape

    def body(s_ref, sm_ref, r_ref, sa_ref, send_sems, recv_sems, loc_sem):
        x, y, c = _me()
        me = 4 * x + 2 * y + c
        lc = pltpu.make_async_copy(sm_ref, sa_ref.at[me], loc_sem.at[0])
        lc.start()
        cps = []
        for r, (cx, cy) in enumerate([(1 - x, y), (x, 1 - y), (1 - x, 1 - y)]):
            cp = _remote(s_ref.at[2 * cx + cy], r_ref.at[r], send_sems, recv_sems, r, (cx, cy, c))
            cp.start()
            cps.append(cp)
        for mask in range(1, N_DEV):
            px, py, pc = x ^ (mask >> 2), y ^ ((mask >> 1) & 1), c ^ (mask & 1)
            cp = _remote(sm_ref, sa_ref.at[me], send_sems, recv_sems, 2 + mask, (px, py, pc))
            cp.start()
            cps.append(cp)
        for r in range(3):
            _remote(r_ref.at[r], r_ref.at[r], send_sems, recv_sems, r, (x, y, c)).wait_recv()
        for mask in range(1, N_DEV):
            src = 4 * (x ^ (mask >> 2)) + 2 * (y ^ ((mask >> 1) & 1)) + (c ^ (mask & 1))
            _remote(sa_ref.at[src], sa_ref.at[src], send_sems, recv_sems, 2 + mask, (x, y, c)).wait_recv()
        for cp in cps:
            cp.wait_send()
        lc.wait()

    return pl.pallas_call(
        body, name=name, in_specs=[ANY, ANY], out_specs=[ANY, ANY],
        out_shape=[SDS((3, half, LANES), s1.dtype), SDS((N_DEV,) + small.shape, small.dtype)],
        scratch_shapes=[pltpu.SemaphoreType.DMA((10,)), pltpu.SemaphoreType.DMA((10,)), pltpu.SemaphoreType.DMA((1,))],
        compiler_params=pltpu.CompilerParams(has_side_effects=True))(s1, small)


def _half_exchange(gh, name):
    def body(g_ref, o_ref, send_sem, recv_sem):
        x, y, c = _me()
        cp = _remote(g_ref, o_ref, send_sem, recv_sem, 0, (x, y, 1 - c))
        cp.start()
        cp.wait()

    return pl.pallas_call(
        body, name=name, in_specs=[ANY], out_specs=ANY, out_shape=SDS(gh.shape, gh.dtype),
        scratch_shapes=[pltpu.SemaphoreType.DMA((1,)), pltpu.SemaphoreType.DMA((1,))],
        compiler_params=pltpu.CompilerParams(has_side_effects=True))(gh)


ADD_TR = 1024


def _pair_add(g, r1, c_idx, name):
    _, half, _ = r1.shape
    nb = half // ADD_TR

    def body(c_ref, g_ref, r_ref, o_ref, ob_ref):
        s = g_ref[...] + r_ref[...]
        o_ref[...] = s
        ob_ref[...] = s.astype(BF16)

    blk = (1, ADD_TR, LANES)
    own = BS(blk, lambda k, i, c: (k, i, 0))
    return pl.pallas_call(
        body, name=name,
        grid_spec=pltpu.PrefetchScalarGridSpec(
            num_scalar_prefetch=1, grid=(N_CHIPS, nb),
            in_specs=[BS(blk, lambda k, i, c: (k, c[0] * nb + i, 0)), own], out_specs=[own, own]),
        out_shape=[SDS(r1.shape, F32), SDS(r1.shape, BF16)],
        compiler_params=_cparams(("parallel", "parallel")))(c_idx, g, r1)


def _chip_add(s1, r2, j_idx, name):
    _, half, _ = s1.shape

    def body(j_ref, s_ref, r_ref, o_ref):
        o_ref[...] = ((s_ref[0] + r_ref[0].astype(F32)) + r_ref[1].astype(F32)) + r_ref[2].astype(F32)

    return pl.pallas_call(
        body, name=name,
        grid_spec=pltpu.PrefetchScalarGridSpec(
            num_scalar_prefetch=1, grid=(half // ADD_TR,),
            in_specs=[BS((1, ADD_TR, LANES), lambda i, j: (j[0], i, 0)), BS((3, ADD_TR, LANES), lambda i, j: (0, i, 0))],
            out_specs=BS((ADD_TR, LANES), lambda i, j: (i, 0))),
        out_shape=SDS((half, LANES), F32), compiler_params=_cparams(("parallel",)))(j_idx, s1, r2)


def _adamw_math(w, g, m, v):
    m = ADAM_B1 * m + (1.0 - ADAM_B1) * g
    v = ADAM_B2 * v + (1.0 - ADAM_B2) * (g * g)
    m_hat = m / (1.0 - ADAM_B1 ** ADAM_STEP)
    v_hat = v / (1.0 - ADAM_B2 ** ADAM_STEP)
    delta = -ADAM_LR * (m_hat / (jnp.sqrt(v_hat) + ADAM_EPS) + ADAM_WD * w)
    return delta, m, v


def _adamw(w, g, m, v, name):
    rows = w.shape[0]
    tr = _div_tile(rows, ADD_TR)

    def body(w_ref, g_ref, m_ref, v_ref, d_out, m_out, v_out):
        d_out[...], m_out[...], v_out[...] = _adamw_math(w_ref[...], g_ref[...], m_ref[...], v_ref[...])

    spec = BS((tr, LANES), lambda i: (i, 0))
    return pl.pallas_call(
        body, name=name, grid=(rows // tr,), in_specs=[spec] * 4, out_specs=[spec] * 3,
        out_shape=[SDS(w.shape, F32)] * 3, compiler_params=_cparams(("parallel",)))(w, g, m, v)


def _adamw_small(sa, w, m, v, name):
    def body(sa_ref, w_ref, m_ref, v_ref, g_out, d_out, m_out, v_out):
        g = sa_ref[0]
        for d in range(1, N_DEV):
            g = g + sa_ref[d]
        g_out[...] = g
        d_out[...], m_out[...], v_out[...] = _adamw_math(w_ref[...], g, m_ref[...], v_ref[...])

    return pl.pallas_call(body, name=name, out_shape=[SDS(w.shape, F32)] * 4,
                          compiler_params=pltpu.CompilerParams(vmem_limit_bytes=VMEM_LIMIT))(sa, w, m, v)


BIG_ROWS = _pack_rows([SHARD_SHAPE[k] for k in BIG], PACK_ROWS)
FINE_ROWS = _pack_rows([SHARD_SHAPE[k] for k in FINE], 32)
GRAD_ROWS = _pack_rows([SHARD_SHAPE[k] for k in BIG + FINE], PACK_ROWS)
SMALL_SHAPE = {"norm1_g": (2, 1024), "fox_forget_b": (2, 8), "fox_q_g": (2, 64), "fox_k_g": (2, 64),
               "mla_q_a_g": (2, 256), "mla_kv_a_g": (2, 128), "mla_q_g": (2, 96), "mla_k_g": (2, 96),
               "swa_q_g": (2, 64), "swa_k_g": (2, 64), "swa_sinks": (2, 8), "norm2_g": (2, 1024),
               "ffn_conv_b": (2, 5632)}
SMALL_ROWS = _pack_rows([SMALL_SHAPE[k] for k in SMALL] + [(1,)], SUBLANES)


def kernel(x, meta_tokens, norm1_g, w_in, fox_forget_b, fox_q_g, fox_k_g, mla_q_a_g, mla_w_q_up, mla_kv_a_g, mla_w_kv_up, mla_q_g, mla_k_g, swa_q_g, swa_k_g, swa_sinks, w_branch, w_o, norm2_g, ffn_w_up, ffn_conv_w, ffn_conv_b, ffn_w_down, loss_target, m_meta_tokens, m_norm1_g, m_w_in, m_fox_forget_b, m_fox_q_g, m_fox_k_g, m_mla_q_a_g, m_mla_w_q_up, m_mla_kv_a_g, m_mla_w_kv_up, m_mla_q_g, m_mla_k_g, m_swa_q_g, m_swa_k_g, m_swa_sinks, m_w_branch, m_w_o, m_norm2_g, m_ffn_w_up, m_ffn_conv_w, m_ffn_conv_b, m_ffn_w_down, v_meta_tokens, v_norm1_g, v_w_in, v_fox_forget_b, v_fox_q_g, v_fox_k_g, v_mla_q_a_g, v_mla_w_q_up, v_mla_kv_a_g, v_mla_w_kv_up, v_mla_q_g, v_mla_k_g, v_swa_q_g, v_swa_k_g, v_swa_sinks, v_w_branch, v_w_o, v_norm2_g, v_ffn_w_up, v_ffn_conv_w, v_ffn_conv_b, v_ffn_w_down):
    w = dict(meta_tokens=meta_tokens, norm1_g=norm1_g, w_in=w_in, fox_forget_b=fox_forget_b, fox_q_g=fox_q_g,
             fox_k_g=fox_k_g, mla_q_a_g=mla_q_a_g, mla_w_q_up=mla_w_q_up, mla_kv_a_g=mla_kv_a_g,
             mla_w_kv_up=mla_w_kv_up, mla_q_g=mla_q_g, mla_k_g=mla_k_g, swa_q_g=swa_q_g, swa_k_g=swa_k_g,
             swa_sinks=swa_sinks, w_branch=w_branch, w_o=w_o, norm2_g=norm2_g, ffn_w_up=ffn_w_up,
             ffn_conv_w=ffn_conv_w, ffn_conv_b=ffn_conv_b, ffn_w_down=ffn_w_down)
    m = dict(meta_tokens=m_meta_tokens, norm1_g=m_norm1_g, w_in=m_w_in, fox_forget_b=m_fox_forget_b,
             fox_q_g=m_fox_q_g, fox_k_g=m_fox_k_g, mla_q_a_g=m_mla_q_a_g, mla_w_q_up=m_mla_w_q_up,
             mla_kv_a_g=m_mla_kv_a_g, mla_w_kv_up=m_mla_w_kv_up, mla_q_g=m_mla_q_g, mla_k_g=m_mla_k_g,
             swa_q_g=m_swa_q_g, swa_k_g=m_swa_k_g, swa_sinks=m_swa_sinks, w_branch=m_w_branch, w_o=m_w_o,
             norm2_g=m_norm2_g, ffn_w_up=m_ffn_w_up, ffn_conv_w=m_ffn_conv_w, ffn_conv_b=m_ffn_conv_b,
             ffn_w_down=m_ffn_w_down)
    v = dict(meta_tokens=v_meta_tokens, norm1_g=v_norm1_g, w_in=v_w_in, fox_forget_b=v_fox_forget_b,
             fox_q_g=v_fox_q_g, fox_k_g=v_fox_k_g, mla_q_a_g=v_mla_q_a_g, mla_w_q_up=v_mla_w_q_up,
             mla_kv_a_g=v_mla_kv_a_g, mla_w_kv_up=v_mla_w_kv_up, mla_q_g=v_mla_q_g, mla_k_g=v_mla_k_g,
             swa_q_g=v_swa_q_g, swa_k_g=v_swa_k_g, swa_sinks=v_swa_sinks, w_branch=v_w_branch, w_o=v_w_o,
             norm2_g=v_norm2_g, ffn_w_up=v_ffn_w_up, ffn_conv_w=v_ffn_conv_w, ffn_conv_b=v_ffn_conv_b,
             ffn_w_down=v_ffn_w_down)
    xi, yi, ci = _me()
    c_idx = ci.astype(jnp.int32).reshape(1)
    j_idx = (2 * xi + yi).astype(jnp.int32).reshape(1)

    big_shapes = [SHARD_SHAPE[k] for k in BIG]
    fine_shapes = [SHARD_SHAPE[k] for k in FINE]
    gb, gf = _all_gather([_pack([w[k] for k in BIG], BIG_ROWS, BF16), _pack([w[k] for k in FINE], FINE_ROWS, F32)],
                         "gather_weights")
    parts_b = [_unpack(gb[i], big_shapes) for i in range(N_CHIPS)]
    parts_f = [_unpack(gf[i], fine_shapes) for i in range(N_CHIPS)]
    shards_b = {k: jnp.stack([parts_b[i][n] for i in range(N_CHIPS)]) for n, k in enumerate(BIG)}
    shards_f = {k: jnp.stack([parts_f[i][n] for i in range(N_CHIPS)]) for n, k in enumerate(FINE)}
    wb = _assemble(shards_b)
    fine = _assemble(shards_f)
    small = {k: w[k] for k in SMALL}

    loss, g_x, g_eps, g_fine, g_small = _local_step(x[0], loss_target[0], wb, fine, small)

    g_fine_sh = {k: jnp.stack(jnp.split(g_fine[k], N_CHIPS, axis=SHARD_AXIS[k])) for k in FINE}
    gpack = jnp.stack([_pack([g_eps[k][i] for k in BIG] + [g_fine_sh[k][i] for k in FINE], GRAD_ROWS, F32)
                       for i in range(N_CHIPS)])
    spack = _pack([g_small[k] for k in SMALL] + [loss.reshape(1)], SMALL_ROWS, F32)
    r1 = _pair_exchange(gpack, "grads_pair_exchange")
    s1, s1_b = _pair_add(gpack, r1, c_idx, "grads_pair_add")
    r2, sa = _chip_exchange(s1_b, spack, "grads_chip_exchange")
    gh = _chip_add(s1, r2, j_idx, "grads_chip_add")
    go = _half_exchange(gh, "grads_half_exchange")
    g_shard = jnp.where(ci == 0, jnp.concatenate([gh, go]), jnp.concatenate([go, gh]))

    sh_names = BIG + FINE
    sh_shapes = [SHARD_SHAPE[k] for k in sh_names]
    d_p, m_p, v_p = _adamw(_pack([w[k] for k in sh_names], GRAD_ROWS, F32), g_shard,
                           _pack([m[k] for k in sh_names], GRAD_ROWS, F32),
                           _pack([v[k] for k in sh_names], GRAD_ROWS, F32), "adamw_shard")
    sm_shapes = [SMALL_SHAPE[k] for k in SMALL] + [(1,)]
    zero1 = jnp.zeros((1,), F32)
    gs_p, ds_p, ms_p, vs_p = _adamw_small(sa, _pack([w[k] for k in SMALL] + [zero1], SMALL_ROWS, F32),
                                          _pack([m[k] for k in SMALL] + [zero1], SMALL_ROWS, F32),
                                          _pack([v[k] for k in SMALL] + [zero1 + 1.0], SMALL_ROWS, F32), "adamw_small")
    grads, deltas, new_m, new_v = {}, {}, {}, {}
    for dst, pk in ((grads, g_shard), (deltas, d_p), (new_m, m_p), (new_v, v_p)):
        dst.update(zip(sh_names, _unpack(pk, sh_shapes)))
    smalls = [_unpack(pk, sm_shapes) for pk in (gs_p, ds_p, ms_p, vs_p)]
    for dst, vals in zip((grads, deltas, new_m, new_v), smalls):
        dst.update(zip(SMALL, vals[:-1]))
    total_loss = smalls[0][-1][0]
    return (total_loss, g_x[None], *[grads[k] for k in WEIGHTS], *[deltas[k] for k in WEIGHTS],
            *[new_m[k] for k in WEIGHTS], *[new_v[k] for k in WEIGHTS])
```

```python
import functools

import jax
import jax.numpy as jnp
from jax import lax
from jax.experimental import pallas as pl
from jax.experimental.pallas import tpu as pltpu

F32 = jnp.float32
BF16 = jnp.bfloat16
SDS = jax.ShapeDtypeStruct
BS = pl.BlockSpec

D_MODEL = 1024
DEPTH = 2
N_META = 16
EPS = 1e-6
HEADS = 8
HEAD_DIM = 64
MLA_Q_RANK = 256
MLA_KV_RANK = 128
MLA_NOPE = 64
MLA_ROPE = 32
MLA_QK = MLA_NOPE + MLA_ROPE
ROPE_THETA = 10000.0
SWA_KV_HEADS = 2
WINDOW = 128
N_BRANCH = 3
BRANCH_WIDTH = 512
D_FF = 2816
IN_WIDTH = 5800
IN_PAD = 6144
N_CHIPS = 4
N_DEV = 8

ADAM_LR = 0.001
ADAM_B1 = 0.9
ADAM_B2 = 0.999
ADAM_EPS = 1e-08
ADAM_WD = 0.01
ADAM_STEP = 10

LANES = 128
SUBLANES = 8
ROW_PAD = 128
CAUSAL_TILE = 384
NEG = -1e30
VMEM_LIMIT = 56 * 1024 * 1024

O_FQ, O_FK, O_FV, O_FF = 0, 512, 1024, 1536
O_CQ, O_CKV, O_KR = 1664, 1920, 2048
O_SQ, O_SK, O_SV, O_G = 2176, 2688, 2816, 2944

SHARDED = ("meta_tokens", "w_in", "mla_w_q_up", "mla_w_kv_up", "w_branch", "w_o", "ffn_w_up", "ffn_conv_w",
           "ffn_w_down")
SHARD_AXIS = {"meta_tokens": 1, "w_in": 2, "mla_w_q_up": 2, "mla_w_kv_up": 2, "w_branch": 3, "w_o": 1,
              "ffn_w_up": 2, "ffn_conv_w": 2, "ffn_w_down": 1}
SHARD_SHAPE = {"meta_tokens": (16, 256), "w_in": (2, 1024, 1450), "mla_w_q_up": (2, 256, 192),
               "mla_w_kv_up": (2, 128, 256), "w_branch": (2, 3, 512, 256), "w_o": (2, 256, 1024),
               "ffn_w_up": (2, 1024, 1408), "ffn_conv_w": (2, 3, 1408), "ffn_w_down": (2, 704, 1024)}
BIG = ("w_in", "mla_w_q_up", "mla_w_kv_up", "w_branch", "w_o", "ffn_w_up", "ffn_w_down")
FINE = ("meta_tokens", "ffn_conv_w")
SMALL = ("norm1_g", "fox_forget_b", "fox_q_g", "fox_k_g", "mla_q_a_g", "mla_kv_a_g", "mla_q_g", "mla_k_g",
         "swa_q_g", "swa_k_g", "swa_sinks", "norm2_g", "ffn_conv_b")
WEIGHTS = ("meta_tokens", "norm1_g", "w_in", "fox_forget_b", "fox_q_g", "fox_k_g", "mla_q_a_g", "mla_w_q_up",
           "mla_kv_a_g", "mla_w_kv_up", "mla_q_g", "mla_k_g", "swa_q_g", "swa_k_g", "swa_sinks", "w_branch", "w_o",
           "norm2_g", "ffn_w_up", "ffn_conv_w", "ffn_conv_b", "ffn_w_down")
PACK_ROWS = 1024


def _cparams(sem):
    return pltpu.CompilerParams(dimension_semantics=sem, vmem_limit_bytes=VMEM_LIMIT)


def _div_tile(n, cap, mult=SUBLANES):
    best = None
    for t in range(mult, min(n, cap) + 1, mult):
        if n % t == 0:
            best = t
    return best if best is not None else n


def _rows_tile(n, width, budget=2 << 20):
    return _div_tile(n, max(SUBLANES, budget // (4 * max(width, LANES))))


def _op(fwd, bwd):
    @jax.custom_vjp
    def op(*args):
        return fwd(*args)[0]
    op.defvjp(fwd, bwd)
    return op


def _rms_fwd_call(x, g, denom, name):
    n, c = x.shape
    tr = _rows_tile(n, c)

    def body(x_ref, g_ref, y_ref):
        xv = x_ref[...]
        ms = jnp.sum(xv * xv, axis=-1, keepdims=True) * (1.0 / denom)
        y_ref[...] = xv * lax.rsqrt(ms + EPS) * g_ref[...]

    return pl.pallas_call(
        body, name=name, grid=(n // tr,),
        in_specs=[BS((tr, c), lambda i: (i, 0)), BS((1, c), lambda i: (0, 0))],
        out_specs=BS((tr, c), lambda i: (i, 0)), out_shape=SDS((n, c), F32),
        compiler_params=_cparams(("parallel",)))(x, g)


def _rms_bwd_call(x, g, dy, denom, name):
    n, c = x.shape
    tr = _rows_tile(n, c)

    def body(x_ref, g_ref, dy_ref, dx_ref, dg_ref):
        xv = x_ref[...]
        dy = dy_ref[...]
        ms = jnp.sum(xv * xv, axis=-1, keepdims=True) * (1.0 / denom)
        r = lax.rsqrt(ms + EPS)
        xh = xv * r
        dxh = dy * g_ref[...]
        dx_ref[...] = r * (dxh - xh * (jnp.sum(dxh * xh, axis=-1, keepdims=True) * (1.0 / denom)))

        @pl.when(pl.program_id(0) == 0)
        def _():
            dg_ref[...] = jnp.zeros_like(dg_ref)

        dg_ref[...] += jnp.sum(dy * xh, axis=0, keepdims=True)

    return pl.pallas_call(
        body, name=name, grid=(n // tr,),
        in_specs=[BS((tr, c), lambda i: (i, 0)), BS((1, c), lambda i: (0, 0)), BS((tr, c), lambda i: (i, 0))],
        out_specs=[BS((tr, c), lambda i: (i, 0)), BS((1, c), lambda i: (0, 0))],
        out_shape=[SDS((n, c), F32), SDS((1, c), F32)],
        compiler_params=_cparams(("arbitrary",)))(x, g, dy)


def rms_norm(x, g, denom, name):
    def fwd(x, g):
        return _rms_fwd_call(x, g, denom, name + "_f"), (x, g)

    def bwd(res, dy):
        return tuple(_rms_bwd_call(res[0], res[1], dy, denom, name + "_b"))

    return _op(fwd, bwd)(x, g)


def _mm_call(a, b, mode, res, name):
    if mode == "nn":
        (m, kc), n = a.shape, b.shape[1]
    elif mode == "nt":
        (m, kc), n = a.shape, b.shape[0]
    else:
        (kc, m), n = a.shape, b.shape[1]
    if mode == "tn":
        tk = _div_tile(kc, 1056)
        tm = _div_tile(m, 512, LANES)
        tn = _div_tile(n, 1024, LANES)
    else:
        tk = kc if kc <= 2816 else _div_tile(kc, 1024, LANES)
        tm = _div_tile(m, max(LANES, (9 << 19) // (4 * tk)))
        tn = _div_tile(n, 512, LANES)
    nk = kc // tk
    dims = {"nn": (((1,), (0,)), ((), ())), "nt": (((1,), (1,)), ((), ())), "tn": (((0,), (0,)), ((), ()))}[mode]

    def body(*refs):
        if res is None:
            a_ref, b_ref, o_ref, acc_ref = refs
            r_ref = None
        else:
            a_ref, b_ref, r_ref, o_ref, acc_ref = refs
        k = pl.program_id(2)

        @pl.when(k == 0)
        def _():
            acc_ref[...] = jnp.zeros_like(acc_ref)

        acc_ref[...] += lax.dot_general(a_ref[...].astype(BF16), b_ref[...].astype(BF16), dims,
                                        preferred_element_type=F32)

        @pl.when(k == nk - 1)
        def _():
            if r_ref is None:
                o_ref[...] = acc_ref[...]
            else:
                o_ref[...] = r_ref[...] + acc_ref[...]

    a_spec = BS((tk, tm), lambda i, j, k: (k, i)) if mode == "tn" else BS((tm, tk), lambda i, j, k: (i, k))
    b_spec = BS((tn, tk), lambda i, j, k: (j, k)) if mode == "nt" else BS((tk, tn), lambda i, j, k: (k, j))
    o_spec = BS((tm, tn), lambda i, j, k: (i, j))
    ins, args = [a_spec, b_spec], [a, b]
    if res is not None:
        ins.append(o_spec)
        args.append(res)
    return pl.pallas_call(
        body, name=name, grid=(m // tm, n // tn, nk), in_specs=ins, out_specs=o_spec,
        out_shape=SDS((m, n), F32), scratch_shapes=[pltpu.VMEM((tm, tn), F32)],
        compiler_params=_cparams(("parallel", "parallel", "arbitrary")))(*args)


def linear(a, w, eps, name, res=None):
    if res is None:
        def fwd(a, w, eps):
            return _mm_call(a, w, "nn", None, name + "_f"), (a, w)

        def bwd(r, dc):
            a, w = r
            return (_mm_call(dc, w, "nt", None, name + "_da"), jnp.zeros_like(w),
                    _mm_call(a, dc, "tn", None, name + "_dw"))

        return _op(fwd, bwd)(a, w, eps)

    def fwd_r(a, w, eps, res):
        return _mm_call(a, w, "nn", res, name + "_f"), (a, w)

    def bwd_r(r, dc):
        a, w = r
        return (_mm_call(dc, w, "nt", None, name + "_da"), jnp.zeros_like(w),
                _mm_call(a, dc, "tn", None, name + "_dw"), dc)

    return _op(fwd_r, bwd_r)(a, w, eps, res)


CT = 128


def _tri_dot(v, upper):
    r = lax.broadcasted_iota(jnp.int32, (CT, CT), 0)
    c = lax.broadcasted_iota(jnp.int32, (CT, CT), 1)
    tri = jnp.where((r <= c) if upper else (r >= c), 1.0, 0.0).astype(F32)
    return jnp.dot(v, tri, preferred_element_type=F32, precision=lax.Precision.HIGHEST)


def _gate_fwd_call(z, b, name):
    h, n = z.shape

    def body(z_ref, b_ref, c_ref, carry):
        @pl.when(pl.program_id(0) == 0)
        def _():
            carry[...] = jnp.zeros_like(carry)

        x = z_ref[...] + b_ref[...]
        ls = jnp.minimum(x, 0.0) - jnp.log(1.0 + jnp.exp(-jnp.abs(x)))
        c_ref[...] = _tri_dot(ls, True) + carry[...]
        carry[...] += jnp.sum(ls, axis=1, keepdims=True)

    return pl.pallas_call(
        body, name=name, grid=(n // CT,),
        in_specs=[BS((h, CT), lambda j: (0, j)), BS((h, 1), lambda j: (0, 0))],
        out_specs=BS((h, CT), lambda j: (0, j)), out_shape=SDS((h, n), F32),
        scratch_shapes=[pltpu.VMEM((h, 1), F32)],
        compiler_params=_cparams(("arbitrary",)))(z, b)


def _gate_bwd_call(z, b, dc, name):
    h, n = z.shape
    nt = n // CT

    def body(z_ref, b_ref, dc_ref, dz_ref, db_ref, carry):
        @pl.when(pl.program_id(0) == 0)
        def _():
            carry[...] = jnp.zeros_like(carry)
            db_ref[...] = jnp.zeros_like(db_ref)

        dcv = dc_ref[...]
        dls = _tri_dot(dcv, False) + carry[...]
        carry[...] += jnp.sum(dcv, axis=1, keepdims=True)
        x = z_ref[...] + b_ref[...]
        e = jnp.exp(-jnp.abs(x))
        dz = dls * jnp.where(x >= 0, e / (1.0 + e), 1.0 / (1.0 + e))
        dz_ref[...] = dz
        db_ref[...] += jnp.sum(dz, axis=1, keepdims=True)

    rev = lambda j: (0, nt - 1 - j)
    return pl.pallas_call(
        body, name=name, grid=(nt,),
        in_specs=[BS((h, CT), rev), BS((h, 1), lambda j: (0, 0)), BS((h, CT), rev)],
        out_specs=[BS((h, CT), rev), BS((h, 1), lambda j: (0, 0))],
        out_shape=[SDS((h, n), F32), SDS((h, 1), F32)],
        scratch_shapes=[pltpu.VMEM((h, 1), F32)],
        compiler_params=_cparams(("arbitrary",)))(z, b, dc)


def forget_cumsum(z, b, name):
    def fwd(z, b):
        return _gate_fwd_call(z, b, name + "_f"), (z, b)

    def bwd(res, dc):
        return tuple(_gate_bwd_call(res[0], res[1], dc, name + "_b"))

    return _op(fwd, bwd)(z, b)


def _rope_call(x, cos, sin, name):
    h, n, d = x.shape
    tr = _rows_tile(n, d)

    def body(x_ref, c_ref, s_ref, y_ref):
        xv = x_ref[0]
        lane = lax.broadcasted_iota(jnp.int32, xv.shape, 1)
        rot = jnp.where(lane < MLA_NOPE + MLA_ROPE // 2, -pltpu.roll(xv, d - MLA_ROPE // 2, 1),
                        pltpu.roll(xv, MLA_ROPE // 2, 1))
        y_ref[0] = xv * c_ref[...] + rot * s_ref[...]

    return pl.pallas_call(
        body, name=name, grid=(n // tr, h),
        in_specs=[BS((1, tr, d), lambda i, hh: (hh, i, 0)), BS((tr, d), lambda i, hh: (i, 0)),
                  BS((tr, d), lambda i, hh: (i, 0))],
        out_specs=BS((1, tr, d), lambda i, hh: (hh, i, 0)), out_shape=SDS((h, n, d), F32),
        compiler_params=_cparams(("parallel", "parallel")))(x, cos, sin)


def rope(x, cos, sin, name):
    def fwd(x, cos, sin):
        return _rope_call(x, cos, sin, name + "_f"), (cos, sin)

    def bwd(res, dy):
        cos, sin = res
        return _rope_call(dy, cos, -sin, name + "_b"), jnp.zeros_like(cos), jnp.zeros_like(sin)

    return _op(fwd, bwd)(x, cos, sin)


NT_DIMS = (((1,), (1,)), ((), ()))
TN_DIMS = (((0,), (0,)), ((), ()))
HEADS_PER_STEP = 2


def _causal_tile(n):
    return CAUSAL_TILE if n % CAUSAL_TILE == 0 else ROW_PAD


def _causal_fwd_call(q, k, v, ck_r, fox, scale, name):
    h, n, dk = q.shape
    dv = v.shape[2]
    t = _causal_tile(n)
    nq = n // t
    hb = HEADS_PER_STEP

    def body(*refs):
        if fox:
            q_ref, k_ref, v_ref, ck_ref, o_ref, lse_ref, m_scr, l_scr, acc_scr = refs
        else:
            q_ref, k_ref, v_ref, o_ref, lse_ref, m_scr, l_scr, acc_scr = refs
            ck_ref = None
        qi = pl.program_id(1)
        qbs = [q_ref[e].astype(BF16) for e in range(hb)]
        m_scr[...] = jnp.full(m_scr.shape, NEG, F32)
        l_scr[...] = jnp.zeros_like(l_scr)
        acc_scr[...] = jnp.zeros_like(acc_scr)

        def process(j, masked):
            off = pl.multiple_of(j * t, t)
            if masked:
                rows = lax.broadcasted_iota(jnp.int32, (t, t), 0)
                cols = lax.broadcasted_iota(jnp.int32, (t, t), 1)
                valid = cols <= rows
            for e in range(hb):
                kb = k_ref[e, pl.ds(off, t), :].astype(BF16)
                vb = v_ref[e, pl.ds(off, t), :].astype(BF16)
                s = lax.dot_general(qbs[e], kb, NT_DIMS, preferred_element_type=F32) * scale
                if fox:
                    s = s - ck_ref[e, j]
                if masked:
                    s = jnp.where(valid, s, NEG)
                m_old = m_scr[e]
                m_new = jnp.maximum(m_old, jnp.max(s, axis=1, keepdims=True))
                alpha = jnp.exp(m_old - m_new)
                p = jnp.exp(s - jnp.tile(m_new, (1, t // LANES)))
                l_scr[e] = alpha * l_scr[e] + jnp.sum(p, axis=1, keepdims=True)
                acc_scr[e] = alpha[:, :dv] * acc_scr[e] + jnp.dot(p.astype(BF16), vb, preferred_element_type=F32)
                m_scr[e] = m_new

        def step(j, carry):
            process(j, False)
            return carry

        lax.fori_loop(0, qi, step, 0)
        process(qi, True)
        for e in range(hb):
            l = l_scr[e]
            o_ref[e] = acc_scr[e] / l[:, :dv]
            lse_ref[e, 0] = jnp.transpose(m_scr[e] + jnp.log(l))[0:1, :]

    ins = [BS((hb, t, dk), lambda a, b: (a, b, 0)), BS((hb, n, dk), lambda a, b: (a, 0, 0)),
           BS((hb, n, dv), lambda a, b: (a, 0, 0))]
    args = [q, k, v]
    if fox:
        ins.append(BS((hb, nq, 1, t), lambda a, b: (a, 0, 0, 0)))
        args.append(ck_r)
    return pl.pallas_call(
        body, name=name, grid=(h // hb, nq), in_specs=ins,
        out_specs=[BS((hb, t, dv), lambda a, b: (a, b, 0)), BS((hb, 1, 1, t), lambda a, b: (a, b, 0, 0))],
        out_shape=[SDS((h, n, dv), F32), SDS((h, nq, 1, t), F32)],
        scratch_shapes=[pltpu.VMEM((hb, t, LANES), F32), pltpu.VMEM((hb, t, LANES), F32), pltpu.VMEM((hb, t, dv), F32)],
        compiler_params=_cparams(("parallel", "arbitrary")))(*args)


def _causal_bwd_call(q, k, v, do, o, lse_r, ck_r, fox, scale, name):
    h, n, dk = q.shape
    dv = v.shape[2]
    t = _causal_tile(n)
    nq = n // t
    hb = HEADS_PER_STEP

    def body(*refs):
        it = iter(refs)
        q_ref, k_ref, v_ref, do_ref, o_ref, lse_ref = (next(it) for _ in range(6))
        ck_ref = next(it) if fox else None
        dq_ref, dk_ref, dv_ref = next(it), next(it), next(it)
        dck_ref, dcq_ref = (next(it), next(it)) if fox else (None, None)
        delta_scr, dk_scr, dv_scr = next(it), next(it), next(it)
        dck_scr = next(it) if fox else None
        kj = pl.program_id(1)

        @pl.when(kj == 0)
        def _():
            dq_ref[...] = jnp.zeros_like(dq_ref)
            if fox:
                dcq_ref[...] = jnp.zeros_like(dcq_ref)
            ones = jnp.ones((SUBLANES, dv), F32)

            def fill(qi, carry):
                off = pl.multiple_of(qi * t, t)
                for e in range(hb):
                    prod = do_ref[e, pl.ds(off, t), :] * o_ref[e, pl.ds(off, t), :]
                    delta_scr[e, qi] = lax.dot_general(ones, prod, NT_DIMS, preferred_element_type=F32,
                                                       precision=lax.Precision.HIGHEST)[0:1, :]
                return carry

            lax.fori_loop(0, nq, fill, 0)

        kbs = [k_ref[e].astype(BF16) for e in range(hb)]
        vbs = [v_ref[e].astype(BF16) for e in range(hb)]
        dk_scr[...] = jnp.zeros_like(dk_scr)
        dv_scr[...] = jnp.zeros_like(dv_scr)
        if fox:
            dck_scr[...] = jnp.zeros_like(dck_scr)
            ckcs = [jnp.tile(jnp.transpose(jnp.broadcast_to(ck_ref[e, 0], (LANES, t))), (1, t // LANES))
                    for e in range(hb)]

        def process(qi, masked):
            off = pl.multiple_of(qi * t, t)
            if masked:
                krows = lax.broadcasted_iota(jnp.int32, (t, t), 0)
                qcols = lax.broadcasted_iota(jnp.int32, (t, t), 1)
                valid = krows <= qcols
            for e in range(hb):
                qb = q_ref[e, pl.ds(off, t), :].astype(BF16)
                dob = do_ref[e, pl.ds(off, t), :].astype(BF16)
                st = lax.dot_general(kbs[e], qb, NT_DIMS, preferred_element_type=F32) * scale
                if fox:
                    st = st - ckcs[e]
                pt = jnp.exp(st - lse_ref[e, qi])
                if masked:
                    pt = jnp.where(valid, pt, 0.0)
                dv_scr[e] += jnp.dot(pt.astype(BF16), dob, preferred_element_type=F32)
                dpt = lax.dot_general(vbs[e], dob, NT_DIMS, preferred_element_type=F32)
                dst = pt * (dpt - delta_scr[e, qi])
                if fox:
                    dck_scr[e] -= jnp.sum(dst, axis=1, keepdims=True)
                    dcq_ref[e, qi] += jnp.sum(dst, axis=0, keepdims=True)
                dsb = (dst * scale).astype(BF16)
                dk_scr[e] += jnp.dot(dsb, qb, preferred_element_type=F32)
                dq_ref[e, pl.ds(off, t), :] += lax.dot_general(dsb, kbs[e], TN_DIMS, preferred_element_type=F32)

        def step(qi, carry):
            process(qi, False)
            return carry

        process(kj, True)
        lax.fori_loop(kj + 1, nq, step, 0)
        dk_ref[...] = dk_scr[...]
        dv_ref[...] = dv_scr[...]
        if fox:
            for e in range(hb):
                dck_ref[e, 0] = jnp.transpose(jnp.broadcast_to(dck_scr[e], (t, LANES)))[0:1, :]

    whole = lambda a, b: (a, 0, 0)
    tile = lambda a, b: (a, b, 0)
    rowv = lambda a, b: (a, 0, 0, 0)
    rowt = lambda a, b: (a, b, 0, 0)
    ins = [BS((hb, n, dk), whole), BS((hb, t, dk), tile), BS((hb, t, dv), tile), BS((hb, n, dv), whole),
           BS((hb, n, dv), whole), BS((hb, nq, 1, t), rowv)]
    args = [q, k, v, do, o, lse_r]
    outs = [BS((hb, n, dk), whole), BS((hb, t, dk), tile), BS((hb, t, dv), tile)]
    oshape = [SDS((h, n, dk), F32), SDS((h, n, dk), F32), SDS((h, n, dv), F32)]
    scratch = [pltpu.VMEM((hb, nq, 1, t), F32), pltpu.VMEM((hb, t, dk), F32), pltpu.VMEM((hb, t, dv), F32)]
    if fox:
        ins.append(BS((hb, 1, 1, t), rowt))
        args.append(ck_r)
        outs += [BS((hb, 1, 1, t), rowt), BS((hb, nq, 1, t), rowv)]
        oshape += [SDS((h, nq, 1, t), F32), SDS((h, nq, 1, t), F32)]
        scratch.append(pltpu.VMEM((hb, t, 1), F32))
    return pl.pallas_call(
        body, name=name, grid=(h // hb, nq), in_specs=ins, out_specs=outs, out_shape=oshape, scratch_shapes=scratch,
        compiler_params=_cparams(("arbitrary", "arbitrary")))(*args)


def causal_attention(q, k, v, c, scale, name):
    h, n, _ = q.shape
    t = _causal_tile(n)
    nq = n // t
    dv = v.shape[2]
    fox = c is not None

    def run_fwd(q, k, v, c):
        ck_r = c.reshape(h, nq, 1, t) if fox else None
        o, lse = _causal_fwd_call(q, k, v, ck_r, fox, scale, name + "_f")
        return o, (q, k, v, c, o, lse)

    def run_bwd(res, do):
        q, k, v, c, o, lse = res
        ck_r = c.reshape(h, nq, 1, t) if fox else None
        outs = _causal_bwd_call(q, k, v, do, o, lse, ck_r, fox, scale, name + "_b")
        return outs[0], outs[1], outs[2], ((outs[3] + outs[4]).reshape(h, n) if fox else None)

    return _op(run_fwd, run_bwd)(q, k, v, c)


SWA_T = 128


def _swa_masks(qi):
    t = SWA_T
    r = lax.broadcasted_iota(jnp.int32, (t, 3 * t), 0)
    c = lax.broadcasted_iota(jnp.int32, (t, 3 * t), 1)
    seg0 = c < t
    seg1 = (c >= t) & (c < 2 * t)
    jp = jnp.maximum(qi - 1, 0)
    kpos = jnp.where(seg0, c, jnp.where(seg1, jp * t + c - t, qi * t + c - 2 * t))
    dist = qi * t + r - kpos
    band = (dist >= 0) & ((dist < WINDOW) | (kpos < N_META))
    valid = (seg0 & (kpos < N_META) & (qi >= 2)) | (jnp.logical_not(seg0) & band & (jnp.logical_not(seg1) | (qi >= 1)))
    return valid, dist.astype(F32)


def _swa_cat(ref, qi):
    t = SWA_T
    jp = jnp.maximum(qi - 1, 0)
    return jnp.concatenate([ref[0, 0:t, :], ref[0, pl.ds(pl.multiple_of(jp * t, t), t), :],
                            ref[0, pl.ds(pl.multiple_of(qi * t, t), t), :]], axis=0).astype(BF16)


def _swa_fwd_call(q, k, v, sinks, slopes, scale, name):
    hq, n, d = q.shape
    hkv = k.shape[0]
    g = hq // hkv
    t = SWA_T
    nq = n // t

    def body(q_ref, k_ref, v_ref, sink_ref, slope_ref, o_ref, lse_ref):
        grp = pl.program_id(0)
        qi = pl.program_id(1)
        valid, dist = _swa_masks(qi)
        kc = _swa_cat(k_ref, qi)
        vc = _swa_cat(v_ref, qi)
        qs = jnp.concatenate([q_ref[e] for e in range(g)], axis=0).astype(BF16)
        s_all = lax.dot_general(qs, kc, NT_DIMS, preferred_element_type=F32) * scale
        ps, ls, ms = [], [], []
        for e in range(g):
            hh = grp * g + e
            s = jnp.where(valid, s_all[e * t:(e + 1) * t] - slope_ref[hh] * dist, NEG)
            m = jnp.maximum(jnp.max(s, axis=1, keepdims=True), sink_ref[hh])
            p = jnp.exp(s - m)
            ls.append(jnp.sum(p, axis=1, keepdims=True) + jnp.exp(sink_ref[hh] - m))
            ms.append(m)
            ps.append(p.astype(BF16))
        acc = jnp.dot(jnp.concatenate(ps, axis=0), vc, preferred_element_type=F32)
        for e in range(g):
            o_ref[e] = acc[e * t:(e + 1) * t] / ls[e]
            lse_ref[e] = ms[e] + jnp.log(ls[e])

    return pl.pallas_call(
        body, name=name, grid=(hkv, nq),
        in_specs=[BS((g, t, d), lambda a, b: (a, b, 0)), BS((1, n, d), lambda a, b: (a, 0, 0)),
                  BS((1, n, d), lambda a, b: (a, 0, 0)), BS(memory_space=pltpu.SMEM), BS(memory_space=pltpu.SMEM)],
        out_specs=[BS((g, t, d), lambda a, b: (a, b, 0)), BS((g, t, 1), lambda a, b: (a, b, 0))],
        out_shape=[SDS((hq, n, d), F32), SDS((hq, n, 1), F32)],
        compiler_params=_cparams(("parallel", "parallel")))(q, k, v, sinks, slopes)


def _swa_bwd_call(q, k, v, o, lse, do, sinks, slopes, scale, name):
    hq, n, d = q.shape
    hkv = k.shape[0]
    g = hq // hkv
    t = SWA_T
    nq = n // t

    def body(q_ref, k_ref, v_ref, o_ref, lse_ref, do_ref, sink_ref, slope_ref, dq_ref, dk_ref, dv_ref, ds_ref):
        grp = pl.program_id(0)
        qi = pl.program_id(1)

        @pl.when(qi == 0)
        def _():
            dk_ref[...] = jnp.zeros_like(dk_ref)
            dv_ref[...] = jnp.zeros_like(dv_ref)
            ds_ref[...] = jnp.zeros_like(ds_ref)

        valid, dist = _swa_masks(qi)
        kc = _swa_cat(k_ref, qi)
        vc = _swa_cat(v_ref, qi)
        qs = jnp.concatenate([q_ref[e] for e in range(g)], axis=0).astype(BF16)
        dos = jnp.concatenate([do_ref[e] for e in range(g)], axis=0).astype(BF16)
        s_all = lax.dot_general(qs, kc, NT_DIMS, preferred_element_type=F32) * scale
        dp_all = lax.dot_general(dos, vc, NT_DIMS, preferred_element_type=F32)
        ps, dss = [], []
        for e in range(g):
            hh = grp * g + e
            lse_e = lse_ref[e]
            delta = jnp.sum(do_ref[e] * o_ref[e], axis=1, keepdims=True)
            s = s_all[e * t:(e + 1) * t] - slope_ref[hh] * dist
            p = jnp.where(valid, jnp.exp(s - lse_e), 0.0)
            ds = p * (dp_all[e * t:(e + 1) * t] - delta)
            ps.append(p.astype(BF16))
            dss.append((ds * scale).astype(BF16))
            ds_ref[e] += -jnp.sum(jnp.exp(sink_ref[hh] - lse_e) * delta)
        p_st = jnp.concatenate(ps, axis=0)
        ds_st = jnp.concatenate(dss, axis=0)
        dq = jnp.dot(ds_st, kc, preferred_element_type=F32)
        for e in range(g):
            dq_ref[e] = dq[e * t:(e + 1) * t]
        dkc = lax.dot_general(ds_st, qs, TN_DIMS, preferred_element_type=F32)
        dvc = lax.dot_general(p_st, dos, TN_DIMS, preferred_element_type=F32)
        jp = jnp.maximum(qi - 1, 0)
        for seg, off in enumerate((0, pl.multiple_of(jp * t, t), pl.multiple_of(qi * t, t))):
            dk_ref[0, pl.ds(off, t), :] += dkc[seg * t:(seg + 1) * t]
            dv_ref[0, pl.ds(off, t), :] += dvc[seg * t:(seg + 1) * t]

    tile = lambda a, b: (a, b, 0)
    whole = lambda a, b: (a, 0, 0)
    return pl.pallas_call(
        body, name=name, grid=(hkv, nq),
        in_specs=[BS((g, t, d), tile), BS((1, n, d), whole), BS((1, n, d), whole), BS((g, t, d), tile),
                  BS((g, t, 1), tile), BS((g, t, d), tile), BS(memory_space=pltpu.SMEM), BS(memory_space=pltpu.SMEM)],
        out_specs=[BS((g, t, d), tile), BS((1, n, d), whole), BS((1, n, d), whole), BS((g, 1, LANES), whole)],
        out_shape=[SDS((hq, n, d), F32), SDS((hkv, n, d), F32), SDS((hkv, n, d), F32), SDS((hq, 1, LANES), F32)],
        compiler_params=_cparams(("arbitrary", "arbitrary")))(q, k, v, o, lse, do, sinks, slopes)


def window_attention(q, k, v, sinks, slopes, scale, name):
    def run_fwd(q, k, v, sinks, slopes):
        o, lse = _swa_fwd_call(q, k, v, sinks, slopes, scale, name + "_f")
        return o, (q, k, v, sinks, slopes, o, lse)

    def run_bwd(res, do):
        q, k, v, sinks, slopes, o, lse = res
        dq, dk, dv, ds = _swa_bwd_call(q, k, v, o, lse, do, sinks, slopes, scale, name + "_b")
        return dq, dk, dv, ds[:, 0, 0], jnp.zeros_like(slopes)

    return _op(run_fwd, run_bwd)(q, k, v, sinks, slopes)


def _sigmoid(x):
    return 1.0 / (1.0 + jnp.exp(-x))


def _merge_fwd_call(gs, ys, name):
    n, c = ys[0].shape
    tr = _rows_tile(n, c, 1 << 20)

    def body(g0, g1, g2, y0, y1, y2, m_ref):
        m_ref[...] = (_sigmoid(g0[...]) * y0[...] + _sigmoid(g1[...]) * y1[...]) + _sigmoid(g2[...]) * y2[...]

    spec = BS((tr, c), lambda i: (i, 0))
    return pl.pallas_call(
        body, name=name, grid=(n // tr,), in_specs=[spec] * 6, out_specs=spec, out_shape=SDS((n, c), F32),
        compiler_params=_cparams(("parallel",)))(*gs, *ys)


def _merge_bwd_call(gs, ys, dm, name):
    n, c = ys[0].shape
    tr = _rows_tile(n, c, 1 << 20)

    def body(g0, g1, g2, y0, y1, y2, dm_ref, dg0, dg1, dg2, dy0, dy1, dy2):
        d = dm_ref[...]
        for g, y, dg, dy in ((g0, y0, dg0, dy0), (g1, y1, dg1, dy1), (g2, y2, dg2, dy2)):
            s = _sigmoid(g[...])
            dy[...] = d * s
            dg[...] = d * y[...] * (s * (1.0 - s))

    spec = BS((tr, c), lambda i: (i, 0))
    return pl.pallas_call(
        body, name=name, grid=(n // tr,), in_specs=[spec] * 7, out_specs=[spec] * 6,
        out_shape=[SDS((n, c), F32)] * 6, compiler_params=_cparams(("parallel",)))(*gs, *ys, dm)


def gated_merge(gs, ys, name):
    def fwd(gs, ys):
        return _merge_fwd_call(gs, ys, name + "_f"), (gs, ys)

    def bwd(res, dm):
        out = _merge_bwd_call(res[0], res[1], dm, name + "_b")
        return tuple(out[:3]), tuple(out[3:])

    return _op(fwd, bwd)(tuple(gs), tuple(ys))


CONV_TR = 264
CONV_TC = 1408


def _conv_tiles(n, f):
    tr = CONV_TR if n % CONV_TR == 0 else _div_tile(n, CONV_TR)
    tc = CONV_TC if f % CONV_TC == 0 else f
    return tr, tc


def _shift_down(cur, halo, first, tr):
    halo = jnp.where(first, 0.0, halo)
    row = lax.broadcasted_iota(jnp.int32, cur.shape, 0)
    h7, h6 = halo[7:8, :], halo[6:7, :]
    u1 = jnp.where(row == 0, h7, pltpu.roll(cur, 1, 0))
    u2 = jnp.where(row == 0, h6, jnp.where(row == 1, h7, pltpu.roll(cur, 2, 0)))
    return u1, u2


def _conv_lin(cur, u1, u2, w_ref, b_ref):
    return ((b_ref[...] + w_ref[0:1, :] * u2) + w_ref[1:2, :] * u1) + w_ref[2:3, :] * cur


def _conv_in_specs(tr, tc, nj):
    sub = tr // SUBLANES
    prev = lambda j, i: (jnp.maximum(i * sub - 1, 0), j)
    prev_v = lambda j, i: (jnp.maximum(i * sub - 1, 0), j + nj)
    return [BS((tr, tc), lambda j, i: (i, j)), BS((SUBLANES, tc), prev),
            BS((tr, tc), lambda j, i: (i, j + nj)), BS((SUBLANES, tc), prev_v),
            BS((3, tc), lambda j, i: (0, j)), BS((3, tc), lambda j, i: (0, j + nj)),
            BS((1, tc), lambda j, i: (0, j)), BS((1, tc), lambda j, i: (0, j + nj))]


def _conv_fwd_call(u, cw, cb, name):
    n, f2 = u.shape
    f = f2 // 2
    tr, tc = _conv_tiles(n, f)
    nj = f // tc

    def body(ug, ugh, uv, uvh, wg, wv, bg, bv, a_ref):
        first = pl.program_id(1) == 0
        g1, g2 = _shift_down(ug[...], ugh[...], first, tr)
        v1, v2 = _shift_down(uv[...], uvh[...], first, tr)
        cg = _conv_lin(ug[...], g1, g2, wg, bg)
        cv = _conv_lin(uv[...], v1, v2, wv, bv)
        a_ref[...] = cg * _sigmoid(cg) * cv

    return pl.pallas_call(
        body, name=name, grid=(nj, n // tr), in_specs=_conv_in_specs(tr, tc, nj),
        out_specs=BS((tr, tc), lambda j, i: (i, j)), out_shape=SDS((n, f), F32),
        compiler_params=_cparams(("parallel", "parallel")))(u, u, u, u, cw, cw, cb, cb)


def _conv_bwd_dc_call(u, cw, cb, da, name):
    n, f2 = u.shape
    f = f2 // 2
    tr, tc = _conv_tiles(n, f)
    nj = f // tc

    def body(ug, ugh, uv, uvh, wg, wv, bg, bv, da_ref, dc_ref, dw_ref, db_ref):
        first = pl.program_id(1) == 0
        g0, v0 = ug[...], uv[...]
        g1, g2 = _shift_down(g0, ugh[...], first, tr)
        v1, v2 = _shift_down(v0, uvh[...], first, tr)
        cg = _conv_lin(g0, g1, g2, wg, bg)
        cv = _conv_lin(v0, v1, v2, wv, bv)
        d = da_ref[...]
        s = _sigmoid(cg)
        dcg = d * cv * (s * (1.0 + cg * (1.0 - s)))
        dcv = d * (cg * s)
        dc_ref[0] = dcg
        dc_ref[1] = dcv

        @pl.when(first)
        def _():
            dw_ref[...] = jnp.zeros_like(dw_ref)
            db_ref[...] = jnp.zeros_like(db_ref)

        for p, dc, taps in ((0, dcg, (g2, g1, g0)), (1, dcv, (v2, v1, v0))):
            for t in range(3):
                dw_ref[p, t:t + 1, :] += jnp.sum(dc * taps[t], axis=0, keepdims=True)
            db_ref[p] += jnp.sum(dc, axis=0, keepdims=True)

    return pl.pallas_call(
        body, name=name, grid=(nj, n // tr),
        in_specs=_conv_in_specs(tr, tc, nj) + [BS((tr, tc), lambda j, i: (i, j))],
        out_specs=[BS((2, tr, tc), lambda j, i: (0, i, j)), BS((2, 3, tc), lambda j, i: (0, 0, j)),
                   BS((2, 1, tc), lambda j, i: (0, 0, j))],
        out_shape=[SDS((2, n, f), F32), SDS((2, 3, f), F32), SDS((2, 1, f), F32)],
        compiler_params=_cparams(("arbitrary", "arbitrary")))(u, u, u, u, cw, cw, cb, cb, da)


def _conv_bwd_du_call(dc, cw, name):
    _, n, f = dc.shape
    tr, tc = _conv_tiles(n, f)
    nj = f // tc
    ni = n // tr
    sub = tr // SUBLANES

    def body(c_ref, nx_ref, w_ref, du_ref):
        cur = c_ref[0]
        nxt = jnp.where(pl.program_id(2) == ni - 1, 0.0, nx_ref[0])
        row = lax.broadcasted_iota(jnp.int32, cur.shape, 0)
        n0, n1 = nxt[0:1, :], nxt[1:2, :]
        d1 = jnp.where(row == tr - 1, n0, pltpu.roll(cur, tr - 1, 0))
        d2 = jnp.where(row == tr - 1, n1, jnp.where(row == tr - 2, n0, pltpu.roll(cur, tr - 2, 0)))
        du_ref[...] = (w_ref[2:3, :] * cur + w_ref[1:2, :] * d1) + w_ref[0:1, :] * d2

    nxt_map = lambda p, j, i: (p, jnp.minimum((i + 1) * sub, n // SUBLANES - 1), j)
    return pl.pallas_call(
        body, name=name, grid=(2, nj, ni),
        in_specs=[BS((1, tr, tc), lambda p, j, i: (p, i, j)), BS((1, SUBLANES, tc), nxt_map),
                  BS((3, tc), lambda p, j, i: (0, p * nj + j))],
        out_specs=BS((tr, tc), lambda p, j, i: (i, p * nj + j)), out_shape=SDS((n, 2 * f), F32),
        compiler_params=_cparams(("parallel", "parallel", "parallel")))(dc, dc, cw)


def conv_glu(u, cw, cb, name):
    def fwd(u, cw, cb):
        return _conv_fwd_call(u, cw, cb, name + "_f"), (u, cw, cb)

    def bwd(res, da):
        u, cw, cb = res
        dc, dw, db = _conv_bwd_dc_call(u, cw, cb, da, name + "_bc")
        du = _conv_bwd_du_call(dc, cw, name + "_bu")
        return du, jnp.concatenate([dw[0], dw[1]], axis=-1), jnp.concatenate([db[0], db[1]], axis=-1)

    return _op(fwd, bwd)(u, cw, cb)


def _loss_call(y, t, n_real, name):
    n, c = y.shape
    tr = _rows_tile(n, c, 1 << 20)

    def body(y_ref, t_ref, dy_ref, l_ref):
        i = pl.program_id(0)
        row = i * tr + lax.broadcasted_iota(jnp.int32, (tr, c), 0)
        real = (row >= N_META) & (row < N_META + n_real)
        e = jnp.where(real, y_ref[...] - t_ref[...], 0.0)
        dy_ref[...] = e * (1.0 / c)

        @pl.when(i == 0)
        def _():
            l_ref[...] = jnp.zeros_like(l_ref)

        l_ref[...] += 0.5 * jnp.sum(jnp.sum(e * e, axis=-1, keepdims=True) * (1.0 / c), axis=0, keepdims=True)

    spec = BS((tr, c), lambda i: (i, 0))
    return pl.pallas_call(
        body, name=name, grid=(n // tr,), in_specs=[spec, spec],
        out_specs=[spec, BS((1, 1), lambda i: (0, 0))], out_shape=[SDS((n, c), F32), SDS((1, 1), F32)],
        compiler_params=_cparams(("arbitrary",)))(y, t)


def _to_heads(x, nh):
    n = x.shape[0]
    return x.reshape(n, nh, x.shape[1] // nh).transpose(1, 0, 2)


def _from_heads(x):
    h, n, d = x.shape
    return x.transpose(1, 0, 2).reshape(n, h * d)


def _head_norm(x, g, denom, name):
    h, n, d = x.shape
    return rms_norm(x.reshape(h * n, d), g, denom, name).reshape(h, n, d)


def _pad_in_cols(w):
    z = lambda k: jnp.zeros(w.shape[:-1] + (k,), w.dtype)
    return jnp.concatenate([w[..., :1544], z(120), w[..., 1544:1960], z(96), w[..., 1960:], z(128)], axis=-1)


def _pad_q_up(w):
    s = w.shape[:-1]
    w = w.reshape(s + (HEADS, MLA_QK))
    w = jnp.concatenate([w, jnp.zeros(s + (HEADS, LANES - MLA_QK), w.dtype)], axis=-1)
    return w.reshape(s + (HEADS * LANES,))


def _assemble(shards):
    full = {k: jnp.concatenate([v[i] for i in range(N_CHIPS)], axis=SHARD_AXIS[k]) for k, v in shards.items()}
    out = dict(full)
    if "w_in" in out:
        out["w_in"] = _pad_in_cols(out["w_in"])
    if "mla_w_q_up" in out:
        out["mla_w_q_up"] = _pad_q_up(out["mla_w_q_up"])
    return out


def _rope_tables(n):
    half = MLA_ROPE // 2
    freqs = ROPE_THETA ** (-jnp.arange(half, dtype=F32) / half)
    ang = jnp.arange(n).astype(F32)[:, None] * freqs[None, :]
    cos, sin = jnp.cos(ang), jnp.sin(ang)
    one, zero = jnp.ones((n, MLA_NOPE), F32), jnp.zeros((n, MLA_NOPE), F32)
    tail1, tail0 = jnp.ones((n, LANES - MLA_QK), F32), jnp.zeros((n, LANES - MLA_QK), F32)
    return (jnp.concatenate([one, cos, cos, tail1], axis=1), jnp.concatenate([zero, sin, sin, tail0], axis=1))


def _pad_lanes(g, width):
    return jnp.concatenate([g, jnp.zeros((width - g.shape[0],), g.dtype)]).reshape(1, width)


def _trunk(eps, fine, small, x, wb):
    seq = x.shape[0]
    n = -(-(N_META + seq) // ROW_PAD) * ROW_PAD
    ew = _assemble(eps)
    cos, sin = _rope_tables(n)
    slopes = jnp.exp2(-8.0 * jnp.arange(1, HEADS + 1, dtype=F32) / HEADS)
    h = jnp.concatenate([fine["meta_tokens"], x, jnp.zeros((n - N_META - seq, D_MODEL), F32)], axis=0)
    for l in range(DEPTH):
        p = f"l{l}_"
        row = lambda name: small[name][l].reshape(1, -1)
        xn = rms_norm(h, row("norm1_g"), D_MODEL, p + "norm1")
        proj = linear(xn, wb["w_in"][l], ew["w_in"][l], p + "win")
        fq = _head_norm(_to_heads(proj[:, O_FQ:O_FQ + 512], HEADS), row("fox_q_g"), HEAD_DIM, p + "fqn")
        fk = _head_norm(_to_heads(proj[:, O_FK:O_FK + 512], HEADS), row("fox_k_g"), HEAD_DIM, p + "fkn")
        fv = _to_heads(proj[:, O_FV:O_FV + 512], HEADS)
        c = forget_cumsum(proj[:, O_FF:O_FF + HEADS].T, small["fox_forget_b"][l].reshape(HEADS, 1), p + "fgate")
        out_a = causal_attention(fq, fk, fv, c, HEAD_DIM ** -0.5, p + "fox")
        cqn = rms_norm(proj[:, O_CQ:O_CQ + MLA_Q_RANK], row("mla_q_a_g"), MLA_Q_RANK, p + "cqn")
        q = _to_heads(linear(cqn, wb["mla_w_q_up"][l], ew["mla_w_q_up"][l], p + "qup"), HEADS)
        q = rope(_head_norm(q, _pad_lanes(small["mla_q_g"][l], LANES), MLA_QK, p + "mqn"), cos, sin, p + "qrope")
        ckvn = rms_norm(proj[:, O_CKV:O_CKV + MLA_KV_RANK], row("mla_kv_a_g"), MLA_KV_RANK, p + "ckvn")
        kv = _to_heads(linear(ckvn, wb["mla_w_kv_up"][l], ew["mla_w_kv_up"][l], p + "kvup"), HEADS)
        kr = jnp.broadcast_to(proj[None, :, O_KR:O_KR + MLA_ROPE], (HEADS, n, MLA_ROPE))
        k = jnp.concatenate([kv[..., :MLA_NOPE], kr, jnp.zeros((HEADS, n, LANES - MLA_QK), F32)], axis=-1)
        k = rope(_head_norm(k, _pad_lanes(small["mla_k_g"][l], LANES), MLA_QK, p + "mkn"), cos, sin, p + "krope")
        out_b = causal_attention(q, k, kv[..., MLA_NOPE:], None, MLA_QK ** -0.5, p + "mla")
        sq = _head_norm(_to_heads(proj[:, O_SQ:O_SQ + 512], HEADS), row("swa_q_g"), HEAD_DIM, p + "sqn")
        sk = _head_norm(_to_heads(proj[:, O_SK:O_SK + 128], SWA_KV_HEADS), row("swa_k_g"), HEAD_DIM, p + "skn")
        sv = _to_heads(proj[:, O_SV:O_SV + 128], SWA_KV_HEADS)
        out_c = window_attention(sq, sk, sv, small["swa_sinks"][l], slopes, HEAD_DIM ** -0.5, p + "swa")
        ys = [linear(_from_heads(o), wb["w_branch"][l, i], ew["w_branch"][l, i], p + f"br{i}")
              for i, o in enumerate((out_a, out_b, out_c))]
        gs = [proj[:, O_G + i * D_MODEL:O_G + (i + 1) * D_MODEL] for i in range(N_BRANCH)]
        merged = gated_merge(gs, ys, p + "merge")
        h = linear(merged, wb["w_o"][l], ew["w_o"][l], p + "wo", res=h)
        xn2 = rms_norm(h, row("norm2_g"), D_MODEL, p + "norm2")
        u = linear(xn2, wb["ffn_w_up"][l], ew["ffn_w_up"][l], p + "wup")
        act = conv_glu(u, fine["ffn_conv_w"][l], row("ffn_conv_b"), p + "conv")
        h = linear(act, wb["ffn_w_down"][l], ew["ffn_w_down"][l], p + "wdown", res=h)
    return h


def _local_step(x, target, wb, fine, small):
    seq = x.shape[0]
    eps = {k: jnp.zeros((N_CHIPS,) + SHARD_SHAPE[k], F32) for k in BIG}
    y, vjp = jax.vjp(lambda e, f, s, xx: _trunk(e, f, s, xx, wb), eps, fine, small, x)
    n = y.shape[0]
    tpad = jnp.concatenate([jnp.zeros((N_META, D_MODEL), F32), target, jnp.zeros((n - N_META - seq, D_MODEL), F32)])
    dy, loss = _loss_call(y, tpad, seq, "loss")
    g_eps, g_fine, g_small, g_x = vjp(dy)
    return loss[0, 0], g_x, g_eps, g_fine, g_small


def _pack_rows(shapes, mult):
    total = sum(_size(s) for s in shapes)
    rows = -(-total // LANES)
    return -(-rows // mult) * mult


def _size(shape):
    n = 1
    for d in shape:
        n *= d
    return n


def _pack(arrs, rows, dtype):
    flat = [a.reshape(-1).astype(dtype) for a in arrs]
    used = sum(a.size for a in flat)
    flat.append(jnp.zeros((rows * LANES - used,), dtype))
    return jnp.concatenate(flat).reshape(rows, LANES)


def _unpack(p, shapes):
    flat = p.reshape(-1)
    out, off = [], 0
    for s in shapes:
        out.append(flat[off:off + _size(s)].reshape(s))
        off += _size(s)
    return out


MESH = pl.DeviceIdType.MESH
ANY = pl.BlockSpec(memory_space=pl.ANY)


def _me():
    return lax.axis_index("x"), lax.axis_index("y"), lax.axis_index("c")


def _remote(src, dst, send_sems, recv_sems, idx, dev):
    return pltpu.make_async_remote_copy(src_ref=src, dst_ref=dst, send_sem=send_sems.at[idx], recv_sem=recv_sems.at[idx],
                                        device_id=dev, device_id_type=MESH)


def _all_gather(packs, name):
    npk = len(packs)

    def body(*refs):
        ins, outs = refs[:npk], refs[npk:2 * npk]
        send_sems, recv_sems = refs[2 * npk:]
        x, y, c = _me()
        j = 2 * x + y
        sibling = (x, y, 1 - c)
        chips = [(1 - x, y), (x, 1 - y), (1 - x, 1 - y)]
        sends = []
        for p in range(npk):
            half = packs[p].shape[0] // 2
            mine = pl.ds(pl.multiple_of(c * half, 16), half)
            for r, (cx, cy) in enumerate(chips):
                cp = _remote(ins[p].at[mine], outs[p].at[j, mine], send_sems, recv_sems, 6 * p + r, (cx, cy, c))
                cp.start()
                sends.append(cp)
        for p in range(npk):
            half = packs[p].shape[0] // 2
            mine = pl.ds(pl.multiple_of(c * half, 16), half)
            for r, (cx, cy) in enumerate(chips):
                blk = outs[p].at[2 * cx + cy, mine]
                _remote(blk, blk, send_sems, recv_sems, 6 * p + r, sibling).wait_recv()
                fw = _remote(blk, blk, send_sems, recv_sems, 6 * p + 3 + r, sibling)
                fw.start()
                sends.append(fw)
        for p in range(npk):
            half = packs[p].shape[0] // 2
            other = pl.ds(pl.multiple_of((1 - c) * half, 16), half)
            for r, (cx, cy) in enumerate(chips):
                blk = outs[p].at[2 * cx + cy, other]
                _remote(blk, blk, send_sems, recv_sems, 6 * p + 3 + r, sibling).wait_recv()
        for cp in sends:
            cp.wait_send()

    outs = pl.pallas_call(
        body, name=name, in_specs=[ANY] * npk, out_specs=[ANY] * npk,
        out_shape=[SDS((N_CHIPS,) + p.shape, p.dtype) for p in packs],
        scratch_shapes=[pltpu.SemaphoreType.DMA((6 * npk,)), pltpu.SemaphoreType.DMA((6 * npk,))],
        compiler_params=pltpu.CompilerParams(has_side_effects=True))(*packs)
    j = 2 * lax.axis_index("x") + lax.axis_index("y")
    return [lax.dynamic_update_index_in_dim(o, p, j, 0) for o, p in zip(outs, packs)]


def _pair_exchange(g, name):
    _, rows, _ = g.shape
    half = rows // 2

    def body(g_ref, o_ref, send_sem, recv_sem):
        x, y, c = _me()
        other = pl.ds(pl.multiple_of((1 - c) * half, 16), half)
        cp = _remote(g_ref.at[:, other], o_ref, send_sem, recv_sem, 0, (x, y, 1 - c))
        cp.start()
        cp.wait()

    return pl.pallas_call(
        body, name=name, in_specs=[ANY], out_specs=ANY, out_shape=SDS((N_CHIPS, half, LANES), g.dtype),
        scratch_shapes=[pltpu.SemaphoreType.DMA((1,)), pltpu.SemaphoreType.DMA((1,))],
        compiler_params=pltpu.CompilerParams(has_side_effects=True))(g)


def _chip_exchange(s1, small, name):
    _, half, _ = s1.shape

    def body(s_ref, sm_ref, r_ref, sa_ref, send_sems, recv_sems, loc_sem):
        x, y, c = _me()
        me = 4 * x + 2 * y + c
        lc = pltpu.make_async_copy(sm_ref, sa_ref.at[me], loc_sem.at[0])
        lc.start()
        cps = []
        for r, (cx, cy) in enumerate([(1 - x, y), (x, 1 - y), (1 - x, 1 - y)]):
            cp = _remote(s_ref.at[2 * cx + cy], r_ref.at[r], send_sems, recv_sems, r, (cx, cy, c))
            cp.start()
            cps.append(cp)
        for mask in range(1, N_DEV):
            px, py, pc = x ^ (mask >> 2), y ^ ((mask >> 1) & 1), c ^ (mask & 1)
            cp = _remote(sm_ref, sa_ref.at[me], send_sems, recv_sems, 2 + mask, (px, py, pc))
            cp.start()
            cps.append(cp)
        for r in range(3):
            _remote(r_ref.at[r], r_ref.at[r], send_sems, recv_sems, r, (x, y, c)).wait_recv()
        for mask in range(1, N_DEV):
            src = 4 * (x ^ (mask >> 2)) + 2 * (y ^ ((mask >> 1) & 1)) + (c ^ (mask & 1))
            _remote(sa_ref.at[src], sa_ref.at[src], send_sems, recv_sems, 2 + mask, (x, y, c)).wait_recv()
        for cp in cps:
            cp.wait_send()
        lc.wait()

    return pl.pallas_call(
        body, name=name, in_specs=[ANY, ANY], out_specs=[ANY, ANY],
        out_shape=[SDS((3, half, LANES), s1.dtype), SDS((N_DEV,) + small.shape, small.dtype)],
        scratch_shapes=[pltpu.SemaphoreType.DMA((10,)), pltpu.SemaphoreType.DMA((10,)), pltpu.SemaphoreType.DMA((1,))],
        compiler_params=pltpu.CompilerParams(has_side_effects=True))(s1, small)


def _half_exchange(gh, name):
    def body(g_ref, o_ref, send_sem, recv_sem):
        x, y, c = _me()
        cp = _remote(g_ref, o_ref, send_sem, recv_sem, 0, (x, y, 1 - c))
        cp.start()
        cp.wait()

    return pl.pallas_call(
        body, name=name, in_specs=[ANY], out_specs=ANY, out_shape=SDS(gh.shape, gh.dtype),
        scratch_shapes=[pltpu.SemaphoreType.DMA((1,)), pltpu.SemaphoreType.DMA((1,))],
        compiler_params=pltpu.CompilerParams(has_side_effects=True))(gh)


ADD_TR = 1024


def _pair_add(g, r1, c_idx, name):
    _, half, _ = r1.shape
    nb = half // ADD_TR

    def body(c_ref, g_ref, r_ref, o_ref, ob_ref):
        s = g_ref[...] + r_ref[...]
        o_ref[...] = s
        ob_ref[...] = s.astype(BF16)

    blk = (1, ADD_TR, LANES)
    own = BS(blk, lambda k, i, c: (k, i, 0))
    return pl.pallas_call(
        body, name=name,
        grid_spec=pltpu.PrefetchScalarGridSpec(
            num_scalar_prefetch=1, grid=(N_CHIPS, nb),
            in_specs=[BS(blk, lambda k, i, c: (k, c[0] * nb + i, 0)), own], out_specs=[own, own]),
        out_shape=[SDS(r1.shape, F32), SDS(r1.shape, BF16)],
        compiler_params=_cparams(("parallel", "parallel")))(c_idx, g, r1)


def _chip_add(s1, r2, j_idx, name):
    _, half, _ = s1.shape

    def body(j_ref, s_ref, r_ref, o_ref):
        o_ref[...] = ((s_ref[0] + r_ref[0].astype(F32)) + r_ref[1].astype(F32)) + r_ref[2].astype(F32)

    return pl.pallas_call(
        body, name=name,
        grid_spec=pltpu.PrefetchScalarGridSpec(
            num_scalar_prefetch=1, grid=(half // ADD_TR,),
            in_specs=[BS((1, ADD_TR, LANES), lambda i, j: (j[0], i, 0)), BS((3, ADD_TR, LANES), lambda i, j: (0, i, 0))],
            out_specs=BS((ADD_TR, LANES), lambda i, j: (i, 0))),
        out_shape=SDS((half, LANES), F32), compiler_params=_cparams(("parallel",)))(j_idx, s1, r2)


def _adamw_math(w, g, m, v):
    m = ADAM_B1 * m + (1.0 - ADAM_B1) * g
    v = ADAM_B2 * v + (1.0 - ADAM_B2) * (g * g)
    m_hat = m / (1.0 - ADAM_B1 ** ADAM_STEP)
    v_hat = v / (1.0 - ADAM_B2 ** ADAM_STEP)
    delta = -ADAM_LR * (m_hat / (jnp.sqrt(v_hat) + ADAM_EPS) + ADAM_WD * w)
    return delta, m, v


def _adamw(w, g, m, v, name):
    rows = w.shape[0]
    tr = _div_tile(rows, ADD_TR)

    def body(w_ref, g_ref, m_ref, v_ref, d_out, m_out, v_out):
        d_out[...], m_out[...], v_out[...] = _adamw_math(w_ref[...], g_ref[...], m_ref[...], v_ref[...])

    spec = BS((tr, LANES), lambda i: (i, 0))
    return pl.pallas_call(
        body, name=name, grid=(rows // tr,), in_specs=[spec] * 4, out_specs=[spec] * 3,
        out_shape=[SDS(w.shape, F32)] * 3, compiler_params=_cparams(("parallel",)))(w, g, m, v)


def _adamw_small(sa, w, m, v, name):
    def body(sa_ref, w_ref, m_ref, v_ref, g_out, d_out, m_out, v_out):
        g = sa_ref[0]
        for d in range(1, N_DEV):
            g = g + sa_ref[d]
        g_out[...] = g
        d_out[...], m_out[...], v_out[...] = _adamw_math(w_ref[...], g, m_ref[...], v_ref[...])

    return pl.pallas_call(body, name=name, out_shape=[SDS(w.shape, F32)] * 4,
                          compiler_params=pltpu.CompilerParams(vmem_limit_bytes=VMEM_LIMIT))(sa, w, m, v)


BIG_ROWS = _pack_rows([SHARD_SHAPE[k] for k in BIG], PACK_ROWS)
FINE_ROWS = _pack_rows([SHARD_SHAPE[k] for k in FINE], 32)
GRAD_ROWS = _pack_rows([SHARD_SHAPE[k] for k in BIG + FINE], PACK_ROWS)
SMALL_SHAPE = {"norm1_g": (2, 1024), "fox_forget_b": (2, 8), "fox_q_g": (2, 64), "fox_k_g": (2, 64),
               "mla_q_a_g": (2, 256), "mla_kv_a_g": (2, 128), "mla_q_g": (2, 96), "mla_k_g": (2, 96),
               "swa_q_g": (2, 64), "swa_k_g": (2, 64), "swa_sinks": (2, 8), "norm2_g": (2, 1024),
               "ffn_conv_b": (2, 5632)}
SMALL_ROWS = _pack_rows([SMALL_SHAPE[k] for k in SMALL] + [(1,)], SUBLANES)


def kernel(x, meta_tokens, norm1_g, w_in, fox_forget_b, fox_q_g, fox_k_g, mla_q_a_g, mla_w_q_up, mla_kv_a_g, mla_w_kv_up, mla_q_g, mla_k_g, swa_q_g, swa_k_g, swa_sinks, w_branch, w_o, norm2_g, ffn_w_up, ffn_conv_w, ffn_conv_b, ffn_w_down, loss_target, m_meta_tokens, m_norm1_g, m_w_in, m_fox_forget_b, m_fox_q_g, m_fox_k_g, m_mla_q_a_g, m_mla_w_q_up, m_mla_kv_a_g, m_mla_w_kv_up, m_mla_q_g, m_mla_k_g, m_swa_q_g, m_swa_k_g, m_swa_sinks, m_w_branch, m_w_o, m_norm2_g, m_ffn_w_up, m_ffn_conv_w, m_ffn_conv_b, m_ffn_w_down, v_meta_tokens, v_norm1_g, v_w_in, v_fox_forget_b, v_fox_q_g, v_fox_k_g, v_mla_q_a_g, v_mla_w_q_up, v_mla_kv_a_g, v_mla_w_kv_up, v_mla_q_g, v_mla_k_g, v_swa_q_g, v_swa_k_g, v_swa_sinks, v_w_branch, v_w_o, v_norm2_g, v_ffn_w_up, v_ffn_conv_w, v_ffn_conv_b, v_ffn_w_down):
    w = dict(meta_tokens=meta_tokens, norm1_g=norm1_g, w_in=w_in, fox_forget_b=fox_forget_b, fox_q_g=fox_q_g,
             fox_k_g=fox_k_g, mla_q_a_g=mla_q_a_g, mla_w_q_up=mla_w_q_up, mla_kv_a_g=mla_kv_a_g,
             mla_w_kv_up=mla_w_kv_up, mla_q_g=mla_q_g, mla_k_g=mla_k_g, swa_q_g=swa_q_g, swa_k_g=swa_k_g,
             swa_sinks=swa_sinks, w_branch=w_branch, w_o=w_o, norm2_g=norm2_g, ffn_w_up=ffn_w_up,
             ffn_conv_w=ffn_conv_w, ffn_conv_b=ffn_conv_b, ffn_w_down=ffn_w_down)
    m = dict(meta_tokens=m_meta_tokens, norm1_g=m_norm1_g, w_in=m_w_in, fox_forget_b=m_fox_forget_b,
             fox_q_g=m_fox_q_g, fox_k_g=m_fox_k_g, mla_q_a_g=m_mla_q_a_g, mla_w_q_up=m_mla_w_q_up,
             mla_kv_a_g=m_mla_kv_a_g, mla_w_kv_up=m_mla_w_kv_up, mla_q_g=m_mla_q_g, mla_k_g=m_mla_k_g,
             swa_q_g=m_swa_q_g, swa_k_g=m_swa_k_g, swa_sinks=m_swa_sinks, w_branch=m_w_branch, w_o=m_w_o,
             norm2_g=m_norm2_g, ffn_w_up=m_ffn_w_up, ffn_conv_w=m_ffn_conv_w, ffn_conv_b=m_ffn_conv_b,
             ffn_w_down=m_ffn_w_down)
    v = dict(meta_tokens=v_meta_tokens, norm1_g=v_norm1_g, w_in=v_w_in, fox_forget_b=v_fox_forget_b,
             fox_q_g=v_fox_q_g, fox_k_g=v_fox_k_g, mla_q_a_g=v_mla_q_a_g, mla_w_q_up=v_mla_w_q_up,
             mla_kv_a_g=v_mla_kv_a_g, mla_w_kv_up=v_mla_w_kv_up, mla_q_g=v_mla_q_g, mla_k_g=v_mla_k_g,
             swa_q_g=v_swa_q_g, swa_k_g=v_swa_k_g, swa_sinks=v_swa_sinks, w_branch=v_w_branch, w_o=v_w_o,
             norm2_g=v_norm2_g, ffn_w_up=v_ffn_w_up, ffn_conv_w=v_ffn_conv_w, ffn_conv_b=v_ffn_conv_b,
             ffn_w_down=v_ffn_w_down)
    xi, yi, ci = _me()
    c_idx = ci.astype(jnp.int32).reshape(1)
    j_idx = (2 * xi + yi).astype(jnp.int32).reshape(1)

    big_shapes = [SHARD_SHAPE[k] for k in BIG]
    fine_shapes = [SHARD_SHAPE[k] for k in FINE]
    gb, gf = _all_gather([_pack([w[k] for k in BIG], BIG_ROWS, BF16), _pack([w[k] for k in FINE], FINE_ROWS, F32)],
                         "gather_weights")
    parts_b = [_unpack(gb[i], big_shapes) for i in range(N_CHIPS)]
    parts_f = [_unpack(gf[i], fine_shapes) for i in range(N_CHIPS)]
    shards_b = {k: jnp.stack([parts_b[i][n] for i in range(N_CHIPS)]) for n, k in enumerate(BIG)}
    shards_f = {k: jnp.stack([parts_f[i][n] for i in range(N_CHIPS)]) for n, k in enumerate(FINE)}
    wb = _assemble(shards_b)
    fine = _assemble(shards_f)
    small = {k: w[k] for k in SMALL}

    loss, g_x, g_eps, g_fine, g_small = _local_step(x[0], loss_target[0], wb, fine, small)

    g_fine_sh = {k: jnp.stack(jnp.split(g_fine[k], N_CHIPS, axis=SHARD_AXIS[k])) for k in FINE}
    gpack = jnp.stack([_pack([g_eps[k][i] for k in BIG] + [g_fine_sh[k][i] for k in FINE], GRAD_ROWS, F32)
                       for i in range(N_CHIPS)])
    spack = _pack([g_small[k] for k in SMALL] + [loss.reshape(1)], SMALL_ROWS, F32)
    r1 = _pair_exchange(gpack, "grads_pair_exchange")
    s1, s1_b = _pair_add(gpack, r1, c_idx, "grads_pair_add")
    r2, sa = _chip_exchange(s1_b, spack, "grads_chip_exchange")
    gh = _chip_add(s1, r2, j_idx, "grads_chip_add")
    go = _half_exchange(gh, "grads_half_exchange")
    g_shard = jnp.where(ci == 0, jnp.concatenate([gh, go]), jnp.concatenate([go, gh]))

    sh_names = BIG + FINE
    sh_shapes = [SHARD_SHAPE[k] for k in sh_names]
    d_p, m_p, v_p = _adamw(_pack([w[k] for k in sh_names], GRAD_ROWS, F32), g_shard,
                           _pack([m[k] for k in sh_names], GRAD_ROWS, F32),
                           _pack([v[k] for k in sh_names], GRAD_ROWS, F32), "adamw_shard")
    sm_shapes = [SMALL_SHAPE[k] for k in SMALL] + [(1,)]
    zero1 = jnp.zeros((1,), F32)
    gs_p, ds_p, ms_p, vs_p = _adamw_small(sa, _pack([w[k] for k in SMALL] + [zero1], SMALL_ROWS, F32),
                                          _pack([m[k] for k in SMALL] + [zero1], SMALL_ROWS, F32),
                                          _pack([v[k] for k in SMALL] + [zero1 + 1.0], SMALL_ROWS, F32), "adamw_small")
    grads, deltas, new_m, new_v = {}, {}, {}, {}
    for dst, pk in ((grads, g_shard), (deltas, d_p), (new_m, m_p), (new_v, v_p)):
        dst.update(zip(sh_names, _unpack(pk, sh_shapes)))
    smalls = [_unpack(pk, sm_shapes) for pk in (gs_p, ds_p, ms_p, vs_p)]
    for dst, vals in zip((grads, deltas, new_m, new_v), smalls):
        dst.update(zip(SMALL, vals[:-1]))
    total_loss = smalls[0][-1][0]
    return (total_loss, g_x[None], *[grads[k] for k in WEIGHTS], *[deltas[k] for k in WEIGHTS],
            *[new_m[k] for k in WEIGHTS], *[new_v[k] for k in WEIGHTS])
```

```python
import functools

import jax
import jax.numpy as jnp
from jax import lax
from jax.experimental import pallas as pl
from jax.experimental.pallas import tpu as pltpu

F32 = jnp.float32
BF16 = jnp.bfloat16
SDS = jax.ShapeDtypeStruct
BS = pl.BlockSpec

D_MODEL = 1024
DEPTH = 2
N_META = 16
EPS = 1e-6
HEADS = 8
HEAD_DIM = 64
MLA_Q_RANK = 256
MLA_KV_RANK = 128
MLA_NOPE = 64
MLA_ROPE = 32
MLA_QK = MLA_NOPE + MLA_ROPE
ROPE_THETA = 10000.0
SWA_KV_HEADS = 2
WINDOW = 128
N_BRANCH = 3
BRANCH_WIDTH = 512
D_FF = 2816
IN_WIDTH = 5800
IN_PAD = 6144
N_CHIPS = 4
N_DEV = 8

ADAM_LR = 0.001
ADAM_B1 = 0.9
ADAM_B2 = 0.999
ADAM_EPS = 1e-08
ADAM_WD = 0.01
ADAM_STEP = 10

LANES = 128
SUBLANES = 8
ROW_PAD = 128
CAUSAL_TILE = 384
NEG = -1e30
VMEM_LIMIT = 56 * 1024 * 1024

O_FQ, O_FK, O_FV, O_FF = 0, 512, 1024, 1536
O_CQ, O_CKV, O_KR = 1664, 1920, 2048
O_SQ, O_SK, O_SV, O_G = 2176, 2688, 2816, 2944

SHARDED = ("meta_tokens", "w_in", "mla_w_q_up", "mla_w_kv_up", "w_branch", "w_o", "ffn_w_up", "ffn_conv_w",
           "ffn_w_down")
SHARD_AXIS = {"meta_tokens": 1, "w_in": 2, "mla_w_q_up": 2, "mla_w_kv_up": 2, "w_branch": 3, "w_o": 1,
              "ffn_w_up": 2, "ffn_conv_w": 2, "ffn_w_down": 1}
SHARD_SHAPE = {"meta_tokens": (16, 256), "w_in": (2, 1024, 1450), "mla_w_q_up": (2, 256, 192),
               "mla_w_kv_up": (2, 128, 256), "w_branch": (2, 3, 512, 256), "w_o": (2, 256, 1024),
               "ffn_w_up": (2, 1024, 1408), "ffn_conv_w": (2, 3, 1408), "ffn_w_down": (2, 704, 1024)}
BIG = ("w_in", "mla_w_q_up", "mla_w_kv_up", "w_branch", "w_o", "ffn_w_up", "ffn_w_down")
FINE = ("meta_tokens", "ffn_conv_w")
SMALL = ("norm1_g", "fox_forget_b", "fox_q_g", "fox_k_g", "mla_q_a_g", "mla_kv_a_g", "mla_q_g", "mla_k_g",
         "swa_q_g", "swa_k_g", "swa_sinks", "norm2_g", "ffn_conv_b")
WEIGHTS = ("meta_tokens", "norm1_g", "w_in", "fox_forget_b", "fox_q_g", "fox_k_g", "mla_q_a_g", "mla_w_q_up",
           "mla_kv_a_g", "mla_w_kv_up", "mla_q_g", "mla_k_g", "swa_q_g", "swa_k_g", "swa_sinks", "w_branch", "w_o",
           "norm2_g", "ffn_w_up", "ffn_conv_w", "ffn_conv_b", "ffn_w_down")
PACK_ROWS = 1024


def _cparams(sem):
    return pltpu.CompilerParams(dimension_semantics=sem, vmem_limit_bytes=VMEM_LIMIT)


def _div_tile(n, cap, mult=SUBLANES):
    best = None
    for t in range(mult, min(n, cap) + 1, mult):
        if n % t == 0:
            best = t
    return best if best is not None else n


def _rows_tile(n, width, budget=2 << 20):
    return _div_tile(n, max(SUBLANES, budget // (4 * max(width, LANES))))


def _op(fwd, bwd):
    @jax.custom_vjp
    def op(*args):
        return fwd(*args)[0]
    op.defvjp(fwd, bwd)
    return op


def _rms_fwd_call(x, g, denom, name):
    n, c = x.shape
    tr = _rows_tile(n, c)

    def body(x_ref, g_ref, y_ref):
        xv = x_ref[...]
        ms = jnp.sum(xv * xv, axis=-1, keepdims=True) * (1.0 / denom)
        y_ref[...] = xv * lax.rsqrt(ms + EPS) * g_ref[...]

    return pl.pallas_call(
        body, name=name, grid=(n // tr,),
        in_specs=[BS((tr, c), lambda i: (i, 0)), BS((1, c), lambda i: (0, 0))],
        out_specs=BS((tr, c), lambda i: (i, 0)), out_shape=SDS((n, c), F32),
        compiler_params=_cparams(("parallel",)))(x, g)


def _rms_bwd_call(x, g, dy, denom, name):
    n, c = x.shape
    tr = _rows_tile(n, c)

    def body(x_ref, g_ref, dy_ref, dx_ref, dg_ref):
        xv = x_ref[...]
        dy = dy_ref[...]
        ms = jnp.sum(xv * xv, axis=-1, keepdims=True) * (1.0 / denom)
        r = lax.rsqrt(ms + EPS)
        xh = xv * r
        dxh = dy * g_ref[...]
        dx_ref[...] = r * (dxh - xh * (jnp.sum(dxh * xh, axis=-1, keepdims=True) * (1.0 / denom)))

        @pl.when(pl.program_id(0) == 0)
        def _():
            dg_ref[...] = jnp.zeros_like(dg_ref)

        dg_ref[...] += jnp.sum(dy * xh, axis=0, keepdims=True)

    return pl.pallas_call(
        body, name=name, grid=(n // tr,),
        in_specs=[BS((tr, c), lambda i: (i, 0)), BS((1, c), lambda i: (0, 0)), BS((tr, c), lambda i: (i, 0))],
        out_specs=[BS((tr, c), lambda i: (i, 0)), BS((1, c), lambda i: (0, 0))],
        out_shape=[SDS((n, c), F32), SDS((1, c), F32)],
        compiler_params=_cparams(("arbitrary",)))(x, g, dy)


def rms_norm(x, g, denom, name):
    def fwd(x, g):
        return _rms_fwd_call(x, g, denom, name + "_f"), (x, g)

    def bwd(res, dy):
        return tuple(_rms_bwd_call(res[0], res[1], dy, denom, name + "_b"))

    return _op(fwd, bwd)(x, g)


def _mm_call(a, b, mode, res, name):
    if mode == "nn":
        (m, kc), n = a.shape, b.shape[1]
    elif mode == "nt":
        (m, kc), n = a.shape, b.shape[0]
    else:
        (kc, m), n = a.shape, b.shape[1]
    if mode == "tn":
        tk = _div_tile(kc, 1056)
        tm = _div_tile(m, 512, LANES)
        tn = _div_tile(n, 1024, LANES)
    else:
        tk = kc if kc <= 2816 else _div_tile(kc, 1024, LANES)
        tm = _div_tile(m, max(LANES, (9 << 19) // (4 * tk)))
        tn = _div_tile(n, 512, LANES)
    nk = kc // tk
    dims = {"nn": (((1,), (0,)), ((), ())), "nt": (((1,), (1,)), ((), ())), "tn": (((0,), (0,)), ((), ()))}[mode]

    def body(*refs):
        if res is None:
            a_ref, b_ref, o_ref, acc_ref = refs
            r_ref = None
        else:
            a_ref, b_ref, r_ref, o_ref, acc_ref = refs
        k = pl.program_id(2)

        @pl.when(k == 0)
        def _():
            acc_ref[...] = jnp.zeros_like(acc_ref)

        acc_ref[...] += lax.dot_general(a_ref[...].astype(BF16), b_ref[...].astype(BF16), dims,
                                        preferred_element_type=F32)

        @pl.when(k == nk - 1)
        def _():
            if r_ref is None:
                o_ref[...] = acc_ref[...]
            else:
                o_ref[...] = r_ref[...] + acc_ref[...]

    a_spec = BS((tk, tm), lambda i, j, k: (k, i)) if mode == "tn" else BS((tm, tk), lambda i, j, k: (i, k))
    b_spec = BS((tn, tk), lambda i, j, k: (j, k)) if mode == "nt" else BS((tk, tn), lambda i, j, k: (k, j))
    o_spec = BS((tm, tn), lambda i, j, k: (i, j))
    ins, args = [a_spec, b_spec], [a, b]
    if res is not None:
        ins.append(o_spec)
        args.append(res)
    return pl.pallas_call(
        body, name=name, grid=(m // tm, n // tn, nk), in_specs=ins, out_specs=o_spec,
        out_shape=SDS((m, n), F32), scratch_shapes=[pltpu.VMEM((tm, tn), F32)],
        compiler_params=_cparams(("parallel", "parallel", "arbitrary")))(*args)


def linear(a, w, eps, name, res=None):
    if res is None:
        def fwd(a, w, eps):
            return _mm_call(a, w, "nn", None, name + "_f"), (a, w)

        def bwd(r, dc):
            a, w = r
            return (_mm_call(dc, w, "nt", None, name + "_da"), jnp.zeros_like(w),
                    _mm_call(a, dc, "tn", None, name + "_dw"))

        return _op(fwd, bwd)(a, w, eps)

    def fwd_r(a, w, eps, res):
        return _mm_call(a, w, "nn", res, name + "_f"), (a, w)

    def bwd_r(r, dc):
        a, w = r
        return (_mm_call(dc, w, "nt", None, name + "_da"), jnp.zeros_like(w),
                _mm_call(a, dc, "tn", None, name + "_dw"), dc)

    return _op(fwd_r, bwd_r)(a, w, eps, res)


CT = 128


def _tri_dot(v, upper):
    r = lax.broadcasted_iota(jnp.int32, (CT, CT), 0)
    c = lax.broadcasted_iota(jnp.int32, (CT, CT), 1)
    tri = jnp.where((r <= c) if upper else (r >= c), 1.0, 0.0).astype(F32)
    return jnp.dot(v, tri, preferred_element_type=F32, precision=lax.Precision.HIGHEST)


def _gate_fwd_call(z, b, name):
    h, n = z.shape

    def body(z_ref, b_ref, c_ref, carry):
        @pl.when(pl.program_id(0) == 0)
        def _():
            carry[...] = jnp.zeros_like(carry)

        x = z_ref[...] + b_ref[...]
        ls = jnp.minimum(x, 0.0) - jnp.log(1.0 + jnp.exp(-jnp.abs(x)))
        c_ref[...] = _tri_dot(ls, True) + carry[...]
        carry[...] += jnp.sum(ls, axis=1, keepdims=True)

    return pl.pallas_call(
        body, name=name, grid=(n // CT,),
        in_specs=[BS((h, CT), lambda j: (0, j)), BS((h, 1), lambda j: (0, 0))],
        out_specs=BS((h, CT), lambda j: (0, j)), out_shape=SDS((h, n), F32),
        scratch_shapes=[pltpu.VMEM((h, 1), F32)],
        compiler_params=_cparams(("arbitrary",)))(z, b)


def _gate_bwd_call(z, b, dc, name):
    h, n = z.shape
    nt = n // CT

    def body(z_ref, b_ref, dc_ref, dz_ref, db_ref, carry):
        @pl.when(pl.program_id(0) == 0)
        def _():
            carry[...] = jnp.zeros_like(carry)
            db_ref[...] = jnp.zeros_like(db_ref)

        dcv = dc_ref[...]
        dls = _tri_dot(dcv, False) + carry[...]
        carry[...] += jnp.sum(dcv, axis=1, keepdims=True)
        x = z_ref[...] + b_ref[...]
        e = jnp.exp(-jnp.abs(x))
        dz = dls * jnp.where(x >= 0, e / (1.0 + e), 1.0 / (1.0 + e))
        dz_ref[...] = dz
        db_ref[...] += jnp.sum(dz, axis=1, keepdims=True)

    rev = lambda j: (0, nt - 1 - j)
    return pl.pallas_call(
        body, name=name, grid=(nt,),
        in_specs=[BS((h, CT), rev), BS((h, 1), lambda j: (0, 0)), BS((h, CT), rev)],
        out_specs=[BS((h, CT), rev), BS((h, 1), lambda j: (0, 0))],
        out_shape=[SDS((h, n), F32), SDS((h, 1), F32)],
        scratch_shapes=[pltpu.VMEM((h, 1), F32)],
        compiler_params=_cparams(("arbitrary",)))(z, b, dc)


def forget_cumsum(z, b, name):
    def fwd(z, b):
        return _gate_fwd_call(z, b, name + "_f"), (z, b)

    def bwd(res, dc):
        return tuple(_gate_bwd_call(res[0], res[1], dc, name + "_b"))

    return _op(fwd, bwd)(z, b)


def _rope_call(x, cos, sin, name):
    h, n, d = x.shape
    tr = _rows_tile(n, d)

    def body(x_ref, c_ref, s_ref, y_ref):
        xv = x_ref[0]
        lane = lax.broadcasted_iota(jnp.int32, xv.shape, 1)
        rot = jnp.where(lane < MLA_NOPE + MLA_ROPE // 2, -pltpu.roll(xv, d - MLA_ROPE // 2, 1),
                        pltpu.roll(xv, MLA_ROPE // 2, 1))
        y_ref[0] = xv * c_ref[...] + rot * s_ref[...]

    return pl.pallas_call(
        body, name=name, grid=(n // tr, h),
        in_specs=[BS((1, tr, d), lambda i, hh: (hh, i, 0)), BS((tr, d), lambda i, hh: (i, 0)),
                  BS((tr, d), lambda i, hh: (i, 0))],
        out_specs=BS((1, tr, d), lambda i, hh: (hh, i, 0)), out_shape=SDS((h, n, d), F32),
        compiler_params=_cparams(("parallel", "parallel")))(x, cos, sin)


def rope(x, cos, sin, name):
    def fwd(x, cos, sin):
        return _rope_call(x, cos, sin, name + "_f"), (cos, sin)

    def bwd(res, dy):
        cos, sin = res
        return _rope_call(dy, cos, -sin, name + "_b"), jnp.zeros_like(cos), jnp.zeros_like(sin)

    return _op(fwd, bwd)(x, cos, sin)


NT_DIMS = (((1,), (1,)), ((), ()))
TN_DIMS = (((0,), (0,)), ((), ()))
HEADS_PER_STEP = 2


def _causal_tile(n):
    return CAUSAL_TILE if n % CAUSAL_TILE == 0 else ROW_PAD


def _causal_fwd_call(q, k, v, ck_r, fox, scale, name):
    h, n, dk = q.shape
    dv = v.shape[2]
    t = _causal_tile(n)
    nq = n // t
    hb = HEADS_PER_STEP

    def body(*refs):
        if fox:
            q_ref, k_ref, v_ref, ck_ref, o_ref, lse_ref, m_scr, l_scr, acc_scr = refs
        else:
            q_ref, k_ref, v_ref, o_ref, lse_ref, m_scr, l_scr, acc_scr = refs
            ck_ref = None
        qi = pl.program_id(1)
        qbs = [q_ref[e].astype(BF16) for e in range(hb)]
        m_scr[...] = jnp.full(m_scr.shape, NEG, F32)
        l_scr[...] = jnp.zeros_like(l_scr)
        acc_scr[...] = jnp.zeros_like(acc_scr)

        def process(j, masked):
            off = pl.multiple_of(j * t, t)
            if masked:
                rows = lax.broadcasted_iota(jnp.int32, (t, t), 0)
                cols = lax.broadcasted_iota(jnp.int32, (t, t), 1)
                valid = cols <= rows
            for e in range(hb):
                kb = k_ref[e, pl.ds(off, t), :].astype(BF16)
                vb = v_ref[e, pl.ds(off, t), :].astype(BF16)
                s = lax.dot_general(qbs[e], kb, NT_DIMS, preferred_element_type=F32) * scale
                if fox:
                    s = s - ck_ref[e, j]
                if masked:
                    s = jnp.where(valid, s, NEG)
                m_old = m_scr[e]
                m_new = jnp.maximum(m_old, jnp.max(s, axis=1, keepdims=True))
                alpha = jnp.exp(m_old - m_new)
                p = jnp.exp(s - jnp.tile(m_new, (1, t // LANES)))
                l_scr[e] = alpha * l_scr[e] + jnp.sum(p, axis=1, keepdims=True)
                acc_scr[e] = alpha[:, :dv] * acc_scr[e] + jnp.dot(p.astype(BF16), vb, preferred_element_type=F32)
                m_scr[e] = m_new

        def step(j, carry):
            process(j, False)
            return carry

        lax.fori_loop(0, qi, step, 0)
        process(qi, True)
        for e in range(hb):
            l = l_scr[e]
            o_ref[e] = acc_scr[e] / l[:, :dv]
            lse_ref[e, 0] = jnp.transpose(m_scr[e] + jnp.log(l))[0:1, :]

    ins = [BS((hb, t, dk), lambda a, b: (a, b, 0)), BS((hb, n, dk), lambda a, b: (a, 0, 0)),
           BS((hb, n, dv), lambda a, b: (a, 0, 0))]
    args = [q, k, v]
    if fox:
        ins.append(BS((hb, nq, 1, t), lambda a, b: (a, 0, 0, 0)))
        args.append(ck_r)
    return pl.pallas_call(
        body, name=name, grid=(h // hb, nq), in_specs=ins,
        out_specs=[BS((hb, t, dv), lambda a, b: (a, b, 0)), BS((hb, 1, 1, t), lambda a, b: (a, b, 0, 0))],
        out_shape=[SDS((h, n, dv), F32), SDS((h, nq, 1, t), F32)],
        scratch_shapes=[pltpu.VMEM((hb, t, LANES), F32), pltpu.VMEM((hb, t, LANES), F32), pltpu.VMEM((hb, t, dv), F32)],
        compiler_params=_cparams(("parallel", "arbitrary")))(*args)


def _causal_bwd_call(q, k, v, do, o, lse_r, ck_r, fox, scale, name):
    h, n, dk = q.shape
    dv = v.shape[2]
    t = _causal_tile(n)
    nq = n // t
    hb = HEADS_PER_STEP

    def body(*refs):
        it = iter(refs)
        q_ref, k_ref, v_ref, do_ref, o_ref, lse_ref = (next(it) for _ in range(6))
        ck_ref = next(it) if fox else None
        dq_ref, dk_ref, dv_ref = next(it), next(it), next(it)
        dck_ref, dcq_ref = (next(it), next(it)) if fox else (None, None)
        delta_scr, dk_scr, dv_scr = next(it), next(it), next(it)
        dck_scr = next(it) if fox else None
        kj = pl.program_id(1)

        @pl.when(kj == 0)
        def _():
            dq_ref[...] = jnp.zeros_like(dq_ref)
            if fox:
                dcq_ref[...] = jnp.zeros_like(dcq_ref)
            ones = jnp.ones((SUBLANES, dv), F32)

            def fill(qi, carry):
                off = pl.multiple_of(qi * t, t)
                for e in range(hb):
                    prod = do_ref[e, pl.ds(off, t), :] * o_ref[e, pl.ds(off, t), :]
                    delta_scr[e, qi] = lax.dot_general(ones, prod, NT_DIMS, preferred_element_type=F32,
                                                       precision=lax.Precision.HIGHEST)[0:1, :]
                return carry

            lax.fori_loop(0, nq, fill, 0)

        kbs = [k_ref[e].astype(BF16) for e in range(hb)]
        vbs = [v_ref[e].astype(BF16) for e in range(hb)]
        dk_scr[...] = jnp.zeros_like(dk_scr)
        dv_scr[...] = jnp.zeros_like(dv_scr)
        if fox:
            dck_scr[...] = jnp.zeros_like(dck_scr)
            ckcs = [jnp.tile(jnp.transpose(jnp.broadcast_to(ck_ref[e, 0], (LANES, t))), (1, t // LANES))
                    for e in range(hb)]

        def process(qi, masked):
            off = pl.multiple_of(qi * t, t)
            if masked:
                krows = lax.broadcasted_iota(jnp.int32, (t, t), 0)
                qcols = lax.broadcasted_iota(jnp.int32, (t, t), 1)
                valid = krows <= qcols
            for e in range(hb):
                qb = q_ref[e, pl.ds(off, t), :].astype(BF16)
                dob = do_ref[e, pl.ds(off, t), :].astype(BF16)
                st = lax.dot_general(kbs[e], qb, NT_DIMS, preferred_element_type=F32) * scale
                if fox:
                    st = st - ckcs[e]
                pt = jnp.exp(st - lse_ref[e, qi])
                if masked:
                    pt = jnp.where(valid, pt, 0.0)
                dv_scr[e] += jnp.dot(pt.astype(BF16), dob, preferred_element_type=F32)
                dpt = lax.dot_general(vbs[e], dob, NT_DIMS, preferred_element_type=F32)
                dst = pt * (dpt - delta_scr[e, qi])
                if fox:
                    dck_scr[e] -= jnp.sum(dst, axis=1, keepdims=True)
                    dcq_ref[e, qi] += jnp.sum(dst, axis=0, keepdims=True)
                dsb = (dst * scale).astype(BF16)
                dk_scr[e] += jnp.dot(dsb, qb, preferred_element_type=F32)
                dq_ref[e, pl.ds(off, t), :] += lax.dot_general(dsb, kbs[e], TN_DIMS, preferred_element_type=F32)

        def step(qi, carry):
            process(qi, False)
            return carry

        process(kj, True)
        lax.fori_loop(kj + 1, nq, step, 0)
        dk_ref[...] = dk_scr[...]
        dv_ref[...] = dv_scr[...]
        if fox:
            for e in range(hb):
                dck_ref[e, 0] = jnp.transpose(jnp.broadcast_to(dck_scr[e], (t, LANES)))[0:1, :]

    whole = lambda a, b: (a, 0, 0)
    tile = lambda a, b: (a, b, 0)
    rowv = lambda a, b: (a, 0, 0, 0)
    rowt = lambda a, b: (a, b, 0, 0)
    ins = [BS((hb, n, dk), whole), BS((hb, t, dk), tile), BS((hb, t, dv), tile), BS((hb, n, dv), whole),
           BS((hb, n, dv), whole), BS((hb, nq, 1, t), rowv)]
    args = [q, k, v, do, o, lse_r]
    outs = [BS((hb, n, dk), whole), BS((hb, t, dk), tile), BS((hb, t, dv), tile)]
    oshape = [SDS((h, n, dk), F32), SDS((h, n, dk), F32), SDS((h, n, dv), F32)]
    scratch = [pltpu.VMEM((hb, nq, 1, t), F32), pltpu.VMEM((hb, t, dk), F32), pltpu.VMEM((hb, t, dv), F32)]
    if fox:
        ins.append(BS((hb, 1, 1, t), rowt))
        args.append(ck_r)
        outs += [BS((hb, 1, 1, t), rowt), BS((hb, nq, 1, t), rowv)]
        oshape += [SDS((h, nq, 1, t), F32), SDS((h, nq, 1, t), F32)]
        scratch.append(pltpu.VMEM((hb, t, 1), F32))
    return pl.pallas_call(
        body, name=name, grid=(h // hb, nq), in_specs=ins, out_specs=outs, out_shape=oshape, scratch_shapes=scratch,
        compiler_params=_cparams(("arbitrary", "arbitrary")))(*args)


def causal_attention(q, k, v, c, scale, name):
    h, n, _ = q.shape
    t = _causal_tile(n)
    nq = n // t
    dv = v.shape[2]
    fox = c is not None

    def run_fwd(q, k, v, c):
        ck_r = c.reshape(h, nq, 1, t) if fox else None
        o, lse = _causal_fwd_call(q, k, v, ck_r, fox, scale, name + "_f")
        return o, (q, k, v, c, o, lse)

    def run_bwd(res, do):
        q, k, v, c, o, lse = res
        ck_r = c.reshape(h, nq, 1, t) if fox else None
        outs = _causal_bwd_call(q, k, v, do, o, lse, ck_r, fox, scale, name + "_b")
        return outs[0], outs[1], outs[2], ((outs[3] + outs[4]).reshape(h, n) if fox else None)

    return _op(run_fwd, run_bwd)(q, k, v, c)


SWA_T = 128


def _swa_masks(qi):
    t = SWA_T
    r = lax.broadcasted_iota(jnp.int32, (t, 3 * t), 0)
    c = lax.broadcasted_iota(jnp.int32, (t, 3 * t), 1)
    seg0 = c < t
    seg1 = (c >= t) & (c < 2 * t)
    jp = jnp.maximum(qi - 1, 0)
    kpos = jnp.where(seg0, c, jnp.where(seg1, jp * t + c - t, qi * t + c - 2 * t))
    dist = qi * t + r - kpos
    band = (dist >= 0) & ((dist < WINDOW) | (kpos < N_META))
    valid = (seg0 & (kpos < N_META) & (qi >= 2)) | (jnp.logical_not(seg0) & band & (jnp.logical_not(seg1) | (qi >= 1)))
    return valid, dist.astype(F32)


def _swa_cat(ref, qi):
    t = SWA_T
    jp = jnp.maximum(qi - 1, 0)
    return jnp.concatenate([ref[0, 0:t, :], ref[0, pl.ds(pl.multiple_of(jp * t, t), t), :],
                            ref[0, pl.ds(pl.multiple_of(qi * t, t), t), :]], axis=0).astype(BF16)


def _swa_fwd_call(q, k, v, sinks, slopes, scale, name):
    hq, n, d = q.shape
    hkv = k.shape[0]
    g = hq // hkv
    t = SWA_T
    nq = n // t

    def body(q_ref, k_ref, v_ref, sink_ref, slope_ref, o_ref, lse_ref):
        grp = pl.program_id(0)
        qi = pl.program_id(1)
        valid, dist = _swa_masks(qi)
        kc = _swa_cat(k_ref, qi)
        vc = _swa_cat(v_ref, qi)
        qs = jnp.concatenate([q_ref[e] for e in range(g)], axis=0).astype(BF16)
        s_all = lax.dot_general(qs, kc, NT_DIMS, preferred_element_type=F32) * scale
        ps, ls, ms = [], [], []
        for e in range(g):
            hh = grp * g + e
            s = jnp.where(valid, s_all[e * t:(e + 1) * t] - slope_ref[hh] * dist, NEG)
            m = jnp.maximum(jnp.max(s, axis=1, keepdims=True), sink_ref[hh])
            p = jnp.exp(s - m)
            ls.append(jnp.sum(p, axis=1, keepdims=True) + jnp.exp(sink_ref[hh] - m))
            ms.append(m)
            ps.append(p.astype(BF16))
        acc = jnp.dot(jnp.concatenate(ps, axis=0), vc, preferred_element_type=F32)
        for e in range(g):
            o_ref[e] = acc[e * t:(e + 1) * t] / ls[e]
            lse_ref[e] = ms[e] + jnp.log(ls[e])

    return pl.pallas_call(
        body, name=name, grid=(hkv, nq),
        in_specs=[BS((g, t, d), lambda a, b: (a, b, 0)), BS((1, n, d), lambda a, b: (a, 0, 0)),
                  BS((1, n, d), lambda a, b: (a, 0, 0)), BS(memory_space=pltpu.SMEM), BS(memory_space=pltpu.SMEM)],
        out_specs=[BS((g, t, d), lambda a, b: (a, b, 0)), BS((g, t, 1), lambda a, b: (a, b, 0))],
        out_shape=[SDS((hq, n, d), F32), SDS((hq, n, 1), F32)],
        compiler_params=_cparams(("parallel", "parallel")))(q, k, v, sinks, slopes)


def _swa_bwd_call(q, k, v, o, lse, do, sinks, slopes, scale, name):
    hq, n, d = q.shape
    hkv = k.shape[0]
    g = hq // hkv
    t = SWA_T
    nq = n // t

    def body(q_ref, k_ref, v_ref, o_ref, lse_ref, do_ref, sink_ref, slope_ref, dq_ref, dk_ref, dv_ref, ds_ref):
        grp = pl.program_id(0)
        qi = pl.program_id(1)

        @pl.when(qi == 0)
        def _():
            dk_ref[...] = jnp.zeros_like(dk_ref)
            dv_ref[...] = jnp.zeros_like(dv_ref)
            ds_ref[...] = jnp.zeros_like(ds_ref)

        valid, dist = _swa_masks(qi)
        kc = _swa_cat(k_ref, qi)
        vc = _swa_cat(v_ref, qi)
        qs = jnp.concatenate([q_ref[e] for e in range(g)], axis=0).astype(BF16)
        dos = jnp.concatenate([do_ref[e] for e in range(g)], axis=0).astype(BF16)
        s_all = lax.dot_general(qs, kc, NT_DIMS, preferred_element_type=F32) * scale
        dp_all = lax.dot_general(dos, vc, NT_DIMS, preferred_element_type=F32)
        ps, dss = [], []
        for e in range(g):
            hh = grp * g + e
            lse_e = lse_ref[e]
            delta = jnp.sum(do_ref[e] * o_ref[e], axis=1, keepdims=True)
            s = s_all[e * t:(e + 1) * t] - slope_ref[hh] * dist
            p = jnp.where(valid, jnp.exp(s - lse_e), 0.0)
            ds = p * (dp_all[e * t:(e + 1) * t] - delta)
            ps.append(p.astype(BF16))
            dss.append((ds * scale).astype(BF16))
            ds_ref[e] += -jnp.sum(jnp.exp(sink_ref[hh] - lse_e) * delta)
        p_st = jnp.concatenate(ps, axis=0)
        ds_st = jnp.concatenate(dss, axis=0)
        dq = jnp.dot(ds_st, kc, preferred_element_type=F32)
        for e in range(g):
            dq_ref[e] = dq[e * t:(e + 1) * t]
        dkc = lax.dot_general(ds_st, qs, TN_DIMS, preferred_element_type=F32)
        dvc = lax.dot_general(p_st, dos, TN_DIMS, preferred_element_type=F32)
        jp = jnp.maximum(qi - 1, 0)
        for seg, off in enumerate((0, pl.multiple_of(jp * t, t), pl.multiple_of(qi * t, t))):
            dk_ref[0, pl.ds(off, t), :] += dkc[seg * t:(seg + 1) * t]
            dv_ref[0, pl.ds(off, t), :] += dvc[seg * t:(seg + 1) * t]

    tile = lambda a, b: (a, b, 0)
    whole = lambda a, b: (a, 0, 0)
    return pl.pallas_call(
        body, name=name, grid=(hkv, nq),
        in_specs=[BS((g, t, d), tile), BS((1, n, d), whole), BS((1, n, d), whole), BS((g, t, d), tile),
                  BS((g, t, 1), tile), BS((g, t, d), tile), BS(memory_space=pltpu.SMEM), BS(memory_space=pltpu.SMEM)],
        out_specs=[BS((g, t, d), tile), BS((1, n, d), whole), BS((1, n, d), whole), BS((g, 1, LANES), whole)],
        out_shape=[SDS((hq, n, d), F32), SDS((hkv, n, d), F32), SDS((hkv, n, d), F32), SDS((hq, 1, LANES), F32)],
        compiler_params=_cparams(("arbitrary", "arbitrary")))(q, k, v, o, lse, do, sinks, slopes)


def window_attention(q, k, v, sinks, slopes, scale, name):
    def run_fwd(q, k, v, sinks, slopes):
        o, lse = _swa_fwd_call(q, k, v, sinks, slopes, scale, name + "_f")
        return o, (q, k, v, sinks, slopes, o, lse)

    def run_bwd(res, do):
        q, k, v, sinks, slopes, o, lse = res
        dq, dk, dv, ds = _swa_bwd_call(q, k, v, o, lse, do, sinks, slopes, scale, name + "_b")
        return dq, dk, dv, ds[:, 0, 0], jnp.zeros_like(slopes)

    return _op(run_fwd, run_bwd)(q, k, v, sinks, slopes)


def _sigmoid(x):
    return 1.0 / (1.0 + jnp.exp(-x))


def _merge_fwd_call(gs, ys, name):
    n, c = ys[0].shape
    tr = _rows_tile(n, c, 1 << 20)

    def body(g0, g1, g2, y0, y1, y2, m_ref):
        m_ref[...] = (_sigmoid(g0[...]) * y0[...] + _sigmoid(g1[...]) * y1[...]) + _sigmoid(g2[...]) * y2[...]

    spec = BS((tr, c), lambda i: (i, 0))
    return pl.pallas_call(
        body, name=name, grid=(n // tr,), in_specs=[spec] * 6, out_specs=spec, out_shape=SDS((n, c), F32),
        compiler_params=_cparams(("parallel",)))(*gs, *ys)


def _merge_bwd_call(gs, ys, dm, name):
    n, c = ys[0].shape
    tr = _rows_tile(n, c, 1 << 20)

    def body(g0, g1, g2, y0, y1, y2, dm_ref, dg0, dg1, dg2, dy0, dy1, dy2):
        d = dm_ref[...]
        for g, y, dg, dy in ((g0, y0, dg0, dy0), (g1, y1, dg1, dy1), (g2, y2, dg2, dy2)):
            s = _sigmoid(g[...])
            dy[...] = d * s
            dg[...] = d * y[...] * (s * (1.0 - s))

    spec = BS((tr, c), lambda i: (i, 0))
    return pl.pallas_call(
        body, name=name, grid=(n // tr,), in_specs=[spec] * 7, out_specs=[spec] * 6,
        out_shape=[SDS((n, c), F32)] * 6, compiler_params=_cparams(("parallel",)))(*gs, *ys, dm)


def gated_merge(gs, ys, name):
    def fwd(gs, ys):
        return _merge_fwd_call(gs, ys, name + "_f"), (gs, ys)

    def bwd(res, dm):
        out = _merge_bwd_call(res[0], res[1], dm, name + "_b")
        return tuple(out[:3]), tuple(out[3:])

    return _op(fwd, bwd)(tuple(gs), tuple(ys))


CONV_TR = 264
CONV_TC = 1408


def _conv_tiles(n, f):
    tr = CONV_TR if n % CONV_TR == 0 else _div_tile(n, CONV_TR)
    tc = CONV_TC if f % CONV_TC == 0 else f
    return tr, tc


def _shift_down(cur, halo, first, tr):
    halo = jnp.where(first, 0.0, halo)
    row = lax.broadcasted_iota(jnp.int32, cur.shape, 0)
    h7, h6 = halo[7:8, :], halo[6:7, :]
    u1 = jnp.where(row == 0, h7, pltpu.roll(cur, 1, 0))
    u2 = jnp.where(row == 0, h6, jnp.where(row == 1, h7, pltpu.roll(cur, 2, 0)))
    return u1, u2


def _conv_lin(cur, u1, u2, w_ref, b_ref):
    return ((b_ref[...] + w_ref[0:1, :] * u2) + w_ref[1:2, :] * u1) + w_ref[2:3, :] * cur


def _conv_in_specs(tr, tc, nj):
    sub = tr // SUBLANES
    prev = lambda j, i: (jnp.maximum(i * sub - 1, 0), j)
    prev_v = lambda j, i: (jnp.maximum(i * sub - 1, 0), j + nj)
    return [BS((tr, tc), lambda j, i: (i, j)), BS((SUBLANES, tc), prev),
            BS((tr, tc), lambda j, i: (i, j + nj)), BS((SUBLANES, tc), prev_v),
            BS((3, tc), lambda j, i: (0, j)), BS((3, tc), lambda j, i: (0, j + nj)),
            BS((1, tc), lambda j, i: (0, j)), BS((1, tc), lambda j, i: (0, j + nj))]


def _conv_fwd_call(u, cw, cb, name):
    n, f2 = u.shape
    f = f2 // 2
    tr, tc = _conv_tiles(n, f)
    nj = f // tc

    def body(ug, ugh, uv, uvh, wg, wv, bg, bv, a_ref):
        first = pl.program_id(1) == 0
        g1, g2 = _shift_down(ug[...], ugh[...], first, tr)
        v1, v2 = _shift_down(uv[...], uvh[...], first, tr)
        cg = _conv_lin(ug[...], g1, g2, wg, bg)
        cv = _conv_lin(uv[...], v1, v2, wv, bv)
        a_ref[...] = cg * _sigmoid(cg) * cv

    return pl.pallas_call(
        body, name=name, grid=(nj, n // tr), in_specs=_conv_in_specs(tr, tc, nj),
        out_specs=BS((tr, tc), lambda j, i: (i, j)), out_shape=SDS((n, f), F32),
        compiler_params=_cparams(("parallel", "parallel")))(u, u, u, u, cw, cw, cb, cb)


def _conv_bwd_dc_call(u, cw, cb, da, name):
    n, f2 = u.shape
    f = f2 // 2
    tr, tc = _conv_tiles(n, f)
    nj = f // tc

    def body(ug, ugh, uv, uvh, wg, wv, bg, bv, da_ref, dc_ref, dw_ref, db_ref):
        first = pl.program_id(1) == 0
        g0, v0 = ug[...], uv[...]
        g1, g2 = _shift_down(g0, ugh[...], first, tr)
        v1, v2 = _shift_down(v0, uvh[...], first, tr)
        cg = _conv_lin(g0, g1, g2, wg, bg)
        cv = _conv_lin(v0, v1, v2, wv, bv)
        d = da_ref[...]
        s = _sigmoid(cg)
        dcg = d * cv * (s * (1.0 + cg * (1.0 - s)))
        dcv = d * (cg * s)
        dc_ref[0] = dcg
        dc_ref[1] = dcv

        @pl.when(first)
        def _():
            dw_ref[...] = jnp.zeros_like(dw_ref)
            db_ref[...] = jnp.zeros_like(db_ref)

        for p, dc, taps in ((0, dcg, (g2, g1, g0)), (1, dcv, (v2, v1, v0))):
            for t in range(3):
                dw_ref[p, t:t + 1, :] += jnp.sum(dc * taps[t], axis=0, keepdims=True)
            db_ref[p] += jnp.sum(dc, axis=0, keepdims=True)

    return pl.pallas_call(
        body, name=name, grid=(nj, n // tr),
        in_specs=_conv_in_specs(tr, tc, nj) + [BS((tr, tc), lambda j, i: (i, j))],
        out_specs=[BS((2, tr, tc), lambda j, i: (0, i, j)), BS((2, 3, tc), lambda j, i: (0, 0, j)),
                   BS((2, 1, tc), lambda j, i: (0, 0, j))],
        out_shape=[SDS((2, n, f), F32), SDS((2, 3, f), F32), SDS((2, 1, f), F32)],
        compiler_params=_cparams(("arbitrary", "arbitrary")))(u, u, u, u, cw, cw, cb, cb, da)


def _conv_bwd_du_call(dc, cw, name):
    _, n, f = dc.shape
    tr, tc = _conv_tiles(n, f)
    nj = f // tc
    ni = n // tr
    sub = tr // SUBLANES

    def body(c_ref, nx_ref, w_ref, du_ref):
        cur = c_ref[0]
        nxt = jnp.where(pl.program_id(2) == ni - 1, 0.0, nx_ref[0])
        row = lax.broadcasted_iota(jnp.int32, cur.shape, 0)
        n0, n1 = nxt[0:1, :], nxt[1:2, :]
        d1 = jnp.where(row == tr - 1, n0, pltpu.roll(cur, tr - 1, 0))
        d2 = jnp.where(row == tr - 1, n1, jnp.where(row == tr - 2, n0, pltpu.roll(cur, tr - 2, 0)))
        du_ref[...] = (w_ref[2:3, :] * cur + w_ref[1:2, :] * d1) + w_ref[0:1, :] * d2

    nxt_map = lambda p, j, i: (p, jnp.minimum((i + 1) * sub, n // SUBLANES - 1), j)
    return pl.pallas_call(
        body, name=name, grid=(2, nj, ni),
        in_specs=[BS((1, tr, tc), lambda p, j, i: (p, i, j)), BS((1, SUBLANES, tc), nxt_map),
                  BS((3, tc), lambda p, j, i: (0, p * nj + j))],
        out_specs=BS((tr, tc), lambda p, j, i: (i, p * nj + j)), out_shape=SDS((n, 2 * f), F32),
        compiler_params=_cparams(("parallel", "parallel", "parallel")))(dc, dc, cw)


def conv_glu(u, cw, cb, name):
    def fwd(u, cw, cb):
        return _conv_fwd_call(u, cw, cb, name + "_f"), (u, cw, cb)

    def bwd(res, da):
        u, cw, cb = res
        dc, dw, db = _conv_bwd_dc_call(u, cw, cb, da, name + "_bc")
        du = _conv_bwd_du_call(dc, cw, name + "_bu")
        return du, jnp.concatenate([dw[0], dw[1]], axis=-1), jnp.concatenate([db[0], db[1]], axis=-1)

    return _op(fwd, bwd)(u, cw, cb)


def _loss_call(y, t, n_real, name):
    n, c = y.shape
    tr = _rows_tile(n, c, 1 << 20)

    def body(y_ref, t_ref, dy_ref, l_ref):
        i = pl.program_id(0)
        row = i * tr + lax.broadcasted_iota(jnp.int32, (tr, c), 0)
        real = (row >= N_META) & (row < N_META + n_real)
        e = jnp.where(real, y_ref[...] - t_ref[...], 0.0)
        dy_ref[...] = e * (1.0 / c)

        @pl.when(i == 0)
        def _():
            l_ref[...] = jnp.zeros_like(l_ref)

        l_ref[...] += 0.5 * jnp.sum(jnp.sum(e * e, axis=-1, keepdims=True) * (1.0 / c), axis=0, keepdims=True)

    spec = BS((tr, c), lambda i: (i, 0))
    return pl.pallas_call(
        body, name=name, grid=(n // tr,), in_specs=[spec, spec],
        out_specs=[spec, BS((1, 1), lambda i: (0, 0))], out_shape=[SDS((n, c), F32), SDS((1, 1), F32)],
        compiler_params=_cparams(("arbitrary",)))(y, t)


def _to_heads(x, nh):
    n = x.shape[0]
    return x.reshape(n, nh, x.shape[1] // nh).transpose(1, 0, 2)


def _from_heads(x):
    h, n, d = x.shape
    return x.transpose(1, 0, 2).reshape(n, h * d)


def _head_norm(x, g, denom, name):
    h, n, d = x.shape
    return rms_norm(x.reshape(h * n, d), g, denom, name).reshape(h, n, d)


def _pad_in_cols(w):
    z = lambda k: jnp.zeros(w.shape[:-1] + (k,), w.dtype)
    return jnp.concatenate([w[..., :1544], z(120), w[..., 1544:1960], z(96), w[..., 1960:], z(128)], axis=-1)


def _pad_q_up(w):
    s = w.shape[:-1]
    w = w.reshape(s + (HEADS, MLA_QK))
    w = jnp.concatenate([w, jnp.zeros(s + (HEADS, LANES - MLA_QK), w.dtype)], axis=-1)
    return w.reshape(s + (HEADS * LANES,))


def _assemble(shards):
    full = {k: jnp.concatenate([v[i] for i in range(N_CHIPS)], axis=SHARD_AXIS[k]) for k, v in shards.items()}
    out = dict(full)
    if "w_in" in out:
        out["w_in"] = _pad_in_cols(out["w_in"])
    if "mla_w_q_up" in out:
        out["mla_w_q_up"] = _pad_q_up(out["mla_w_q_up"])
    return out


def _rope_tables(n):
    half = MLA_ROPE // 2
    freqs = ROPE_THETA ** (-jnp.arange(half, dtype=F32) / half)
    ang = jnp.arange(n).astype(F32)[:, None] * freqs[None, :]
    cos, sin = jnp.cos(ang), jnp.sin(ang)
    one, zero = jnp.ones((n, MLA_NOPE), F32), jnp.zeros((n, MLA_NOPE), F32)
    tail1, tail0 = jnp.ones((n, LANES - MLA_QK), F32), jnp.zeros((n, LANES - MLA_QK), F32)
    return (jnp.concatenate([one, cos, cos, tail1], axis=1), jnp.concatenate([zero, sin, sin, tail0], axis=1))


def _pad_lanes(g, width):
    return jnp.concatenate([g, jnp.zeros((width - g.shape[0],), g.dtype)]).reshape(1, width)


def _trunk(eps, fine, small, x, wb):
    seq = x.shape[0]
    n = -(-(N_META + seq) // ROW_PAD) * ROW_PAD
    ew = _assemble(eps)
    cos, sin = _rope_tables(n)
    slopes = jnp.exp2(-8.0 * jnp.arange(1, HEADS + 1, dtype=F32) / HEADS)
    h = jnp.concatenate([fine["meta_tokens"], x, jnp.zeros((n - N_META - seq, D_MODEL), F32)], axis=0)
    for l in range(DEPTH):
        p = f"l{l}_"
        row = lambda name: small[name][l].reshape(1, -1)
        xn = rms_norm(h, row("norm1_g"), D_MODEL, p + "norm1")
        proj = linear(xn, wb["w_in"][l], ew["w_in"][l], p + "win")
        fq = _head_norm(_to_heads(proj[:, O_FQ:O_FQ + 512], HEADS), row("fox_q_g"), HEAD_DIM, p + "fqn")
        fk = _head_norm(_to_heads(proj[:, O_FK:O_FK + 512], HEADS), row("fox_k_g"), HEAD_DIM, p + "fkn")
        fv = _to_heads(proj[:, O_FV:O_FV + 512], HEADS)
        c = forget_cumsum(proj[:, O_FF:O_FF + HEADS].T, small["fox_forget_b"][l].reshape(HEADS, 1), p + "fgate")
        out_a = causal_attention(fq, fk, fv, c, HEAD_DIM ** -0.5, p + "fox")
        cqn = rms_norm(proj[:, O_CQ:O_CQ + MLA_Q_RANK], row("mla_q_a_g"), MLA_Q_RANK, p + "cqn")
        q = _to_heads(linear(cqn, wb["mla_w_q_up"][l], ew["mla_w_q_up"][l], p + "qup"), HEADS)
        q = rope(_head_norm(q, _pad_lanes(small["mla_q_g"][l], LANES), MLA_QK, p + "mqn"), cos, sin, p + "qrope")
        ckvn = rms_norm(proj[:, O_CKV:O_CKV + MLA_KV_RANK], row("mla_kv_a_g"), MLA_KV_RANK, p + "ckvn")
        kv = _to_heads(linear(ckvn, wb["mla_w_kv_up"][l], ew["mla_w_kv_up"][l], p + "kvup"), HEADS)
        kr = jnp.broadcast_to(proj[None, :, O_KR:O_KR + MLA_ROPE], (HEADS, n, MLA_ROPE))
        k = jnp.concatenate([kv[..., :MLA_NOPE], kr, jnp.zeros((HEADS, n, LANES - MLA_QK), F32)], axis=-1)
        k = rope(_head_norm(k, _pad_lanes(small["mla_k_g"][l], LANES), MLA_QK, p + "mkn"), cos, sin, p + "krope")
        out_b = causal_attention(q, k, kv[..., MLA_NOPE:], None, MLA_QK ** -0.5, p + "mla")
        sq = _head_norm(_to_heads(proj[:, O_SQ:O_SQ + 512], HEADS), row("swa_q_g"), HEAD_DIM, p + "sqn")
        sk = _head_norm(_to_heads(proj[:, O_SK:O_SK + 128], SWA_KV_HEADS), row("swa_k_g"), HEAD_DIM, p + "skn")
        sv = _to_heads(proj[:, O_SV:O_SV + 128], SWA_KV_HEADS)
        out_c = window_attention(sq, sk, sv, small["swa_sinks"][l], slopes, HEAD_DIM ** -0.5, p + "swa")
        ys = [linear(_from_heads(o), wb["w_branch"][l, i], ew["w_branch"][l, i], p + f"br{i}")
              for i, o in enumerate((out_a, out_b, out_c))]
        gs = [proj[:, O_G + i * D_MODEL:O_G + (i + 1) * D_MODEL] for i in range(N_BRANCH)]
        merged = gated_merge(gs, ys, p + "merge")
        h = linear(merged, wb["w_o"][l], ew["w_o"][l], p + "wo", res=h)
        xn2 = rms_norm(h, row("norm2_g"), D_MODEL, p + "norm2")
        u = linear(xn2, wb["ffn_w_up"][l], ew["ffn_w_up"][l], p + "wup")
        act = conv_glu(u, fine["ffn_conv_w"][l], row("ffn_conv_b"), p + "conv")
        h = linear(act, wb["ffn_w_down"][l], ew["ffn_w_down"][l], p + "wdown", res=h)
    return h


def _local_step(x, target, wb, fine, small):
    seq = x.shape[0]
    eps = {k: jnp.zeros((N_CHIPS,) + SHARD_SHAPE[k], F32) for k in BIG}
    y, vjp = jax.vjp(lambda e, f, s, xx: _trunk(e, f, s, xx, wb), eps, fine, small, x)
    n = y.shape[0]
    tpad = jnp.concatenate([jnp.zeros((N_META, D_MODEL), F32), target, jnp.zeros((n - N_META - seq, D_MODEL), F32)])
    dy, loss = _loss_call(y, tpad, seq, "loss")
    g_eps, g_fine, g_small, g_x = vjp(dy)
    return loss[0, 0], g_x, g_eps, g_fine, g_small


def _pack_rows(shapes, mult):
    total = sum(_size(s) for s in shapes)
    rows = -(-total // LANES)
    return -(-rows // mult) * mult


def _size(shape):
    n = 1
    for d in shape:
        n *= d
    return n


def _pack(arrs, rows, dtype):
    flat = [a.reshape(-1).astype(dtype) for a in arrs]
    used = sum(a.size for a in flat)
    flat.append(jnp.zeros((rows * LANES - used,), dtype))
    return jnp.concatenate(flat).reshape(rows, LANES)


def _unpack(p, shapes):
    flat = p.reshape(-1)
    out, off = [], 0
    for s in shapes:
        out.append(flat[off:off + _size(s)].reshape(s))
        off += _size(s)
    return out


MESH = pl.DeviceIdType.MESH
ANY = pl.BlockSpec(memory_space=pl.ANY)


def _me():
    return lax.axis_index("x"), lax.axis_index("y"), lax.axis_index("c")


def _remote(src, dst, send_sems, recv_sems, idx, dev):
    return pltpu.make_async_remote_copy(src_ref=src, dst_ref=dst, send_sem=send_sems.at[idx], recv_sem=recv_sems.at[idx],
                                        device_id=dev, device_id_type=MESH)


def _all_gather(arrs, name):
    npk = len(arrs)

    def body(*refs):
        ins, outs = refs[:npk], refs[npk:2 * npk]
        send_sems, recv_sems = refs[2 * npk:]
        x, y, c = _me()
        j = 2 * x + y
        sibling = (x, y, 1 - c)
        chips = [(1 - x, y), (x, 1 - y), (1 - x, 1 - y)]
        sends = []
        for p in range(npk):
            for r, (cx, cy) in enumerate(chips):
                cp = _remote(ins[p].at[c], outs[p].at[j, c], send_sems, recv_sems, 6 * p + r, (cx, cy, c))
                cp.start()
                sends.append(cp)
        for p in range(npk):
            for r, (cx, cy) in enumerate(chips):
                blk = outs[p].at[2 * cx + cy, c]
                _remote(blk, blk, send_sems, recv_sems, 6 * p + r, sibling).wait_recv()
                fw = _remote(blk, blk, send_sems, recv_sems, 6 * p + 3 + r, sibling)
                fw.start()
                sends.append(fw)
        for p in range(npk):
            for r, (cx, cy) in enumerate(chips):
                blk = outs[p].at[2 * cx + cy, 1 - c]
                _remote(blk, blk, send_sems, recv_sems, 6 * p + 3 + r, sibling).wait_recv()
        for cp in sends:
            cp.wait_send()

    outs = pl.pallas_call(
        body, name=name, in_specs=[ANY] * npk, out_specs=[ANY] * npk,
        out_shape=[SDS((N_CHIPS,) + a.shape, a.dtype) for a in arrs],
        scratch_shapes=[pltpu.SemaphoreType.DMA((6 * npk,)), pltpu.SemaphoreType.DMA((6 * npk,))],
        compiler_params=pltpu.CompilerParams(has_side_effects=True))(*arrs)
    j = 2 * lax.axis_index("x") + lax.axis_index("y")
    return [lax.dynamic_update_index_in_dim(o, a, j, 0) for o, a in zip(outs, arrs)]


def _pair_exchange(gs, name):
    npk = len(gs)

    def body(*refs):
        ins, outs = refs[:npk], refs[npk:2 * npk]
        send_sems, recv_sems = refs[2 * npk:]
        x, y, c = _me()
        cps = [_remote(ins[p].at[:, 1 - c], outs[p], send_sems, recv_sems, p, (x, y, 1 - c)) for p in range(npk)]
        for cp in cps:
            cp.start()
        for cp in cps:
            cp.wait()

    return pl.pallas_call(
        body, name=name, in_specs=[ANY] * npk, out_specs=[ANY] * npk,
        out_shape=[SDS((N_CHIPS,) + g.shape[2:], g.dtype) for g in gs],
        scratch_shapes=[pltpu.SemaphoreType.DMA((npk,)), pltpu.SemaphoreType.DMA((npk,))],
        compiler_params=pltpu.CompilerParams(has_side_effects=True))(*gs)


def _chip_exchange(ss, small, name):
    npk = len(ss)

    def body(*refs):
        ins, sm_ref = refs[:npk], refs[npk]
        outs, sa_ref = refs[npk + 1:2 * npk + 1], refs[2 * npk + 1]
        send_sems, recv_sems, loc_sem = refs[2 * npk + 2:]
        x, y, c = _me()
        me = 4 * x + 2 * y + c
        lc = pltpu.make_async_copy(sm_ref, sa_ref.at[me], loc_sem.at[0])
        lc.start()
        cps = []
        for p in range(npk):
            for r, (cx, cy) in enumerate([(1 - x, y), (x, 1 - y), (1 - x, 1 - y)]):
                cp = _remote(ins[p].at[2 * cx + cy], outs[p].at[r], send_sems, recv_sems, 3 * p + r, (cx, cy, c))
                cp.start()
                cps.append(cp)
        base = 3 * npk - 1
        for mask in range(1, N_DEV):
            px, py, pc = x ^ (mask >> 2), y ^ ((mask >> 1) & 1), c ^ (mask & 1)
            cp = _remote(sm_ref, sa_ref.at[me], send_sems, recv_sems, base + mask, (px, py, pc))
            cp.start()
            cps.append(cp)
        for p in range(npk):
            for r in range(3):
                _remote(outs[p].at[r], outs[p].at[r], send_sems, recv_sems, 3 * p + r, (x, y, c)).wait_recv()
        for mask in range(1, N_DEV):
            src = 4 * (x ^ (mask >> 2)) + 2 * (y ^ ((mask >> 1) & 1)) + (c ^ (mask & 1))
            _remote(sa_ref.at[src], sa_ref.at[src], send_sems, recv_sems, base + mask, (x, y, c)).wait_recv()
        for cp in cps:
            cp.wait_send()
        lc.wait()

    nsem = 3 * npk + N_DEV - 1
    res = pl.pallas_call(
        body, name=name, in_specs=[ANY] * (npk + 1), out_specs=[ANY] * (npk + 1),
        out_shape=[SDS((3,) + s.shape[1:], s.dtype) for s in ss] + [SDS((N_DEV,) + small.shape, small.dtype)],
        scratch_shapes=[pltpu.SemaphoreType.DMA((nsem,)), pltpu.SemaphoreType.DMA((nsem,)),
                        pltpu.SemaphoreType.DMA((1,))],
        compiler_params=pltpu.CompilerParams(has_side_effects=True))(*ss, small)
    return res[:npk], res[npk]


def _half_exchange(ghs, name):
    npk = len(ghs)

    def body(*refs):
        ins, outs = refs[:npk], refs[npk:2 * npk]
        send_sems, recv_sems = refs[2 * npk:]
        x, y, c = _me()
        cps = [_remote(ins[p], outs[p], send_sems, recv_sems, p, (x, y, 1 - c)) for p in range(npk)]
        for cp in cps:
            cp.start()
        for cp in cps:
            cp.wait()

    return pl.pallas_call(
        body, name=name, in_specs=[ANY] * npk, out_specs=[ANY] * npk, out_shape=[SDS(g.shape, g.dtype) for g in ghs],
        scratch_shapes=[pltpu.SemaphoreType.DMA((npk,)), pltpu.SemaphoreType.DMA((npk,))],
        compiler_params=pltpu.CompilerParams(has_side_effects=True))(*ghs)


def _add_tile(rows, cols):
    return _div_tile(rows, max(16, (1 << 19) // max(cols, LANES)), 16)


def _pair_add(g, r1, c_idx, name):
    _, rows, cols = r1.shape
    tr = _add_tile(rows, cols)

    def body(c_ref, g_ref, r_ref, o_ref, ob_ref):
        s = g_ref[0] + r_ref[...]
        o_ref[...] = s
        ob_ref[...] = s.astype(BF16)

    own = BS((1, tr, cols), lambda k, i, c: (k, i, 0))
    return pl.pallas_call(
        body, name=name,
        grid_spec=pltpu.PrefetchScalarGridSpec(
            num_scalar_prefetch=1, grid=(N_CHIPS, rows // tr),
            in_specs=[BS((1, 1, tr, cols), lambda k, i, c: (k, c[0], i, 0)), own], out_specs=[own, own]),
        out_shape=[SDS(r1.shape, F32), SDS(r1.shape, BF16)],
        compiler_params=_cparams(("parallel", "parallel")))(c_idx, g, r1)


def _chip_add(s1, r2, j_idx, name):
    _, rows, cols = s1.shape
    tr = _add_tile(rows, cols)

    def body(j_ref, s_ref, r_ref, o_ref):
        o_ref[...] = ((s_ref[0] + r_ref[0].astype(F32)) + r_ref[1].astype(F32)) + r_ref[2].astype(F32)

    return pl.pallas_call(
        body, name=name,
        grid_spec=pltpu.PrefetchScalarGridSpec(
            num_scalar_prefetch=1, grid=(rows // tr,),
            in_specs=[BS((1, tr, cols), lambda i, j: (j[0], i, 0)), BS((3, tr, cols), lambda i, j: (0, i, 0))],
            out_specs=BS((tr, cols), lambda i, j: (i, 0))),
        out_shape=SDS((rows, cols), F32), compiler_params=_cparams(("parallel",)))(j_idx, s1, r2)


def _adamw_math(w, g, m, v):
    m = ADAM_B1 * m + (1.0 - ADAM_B1) * g
    v = ADAM_B2 * v + (1.0 - ADAM_B2) * (g * g)
    m_hat = m / (1.0 - ADAM_B1 ** ADAM_STEP)
    v_hat = v / (1.0 - ADAM_B2 ** ADAM_STEP)
    delta = -ADAM_LR * (m_hat / (jnp.sqrt(v_hat) + ADAM_EPS) + ADAM_WD * w)
    return delta, m, v


def _adamw(w, gh, go, m, v, c_idx, name):
    _, rows, cols = w.shape
    tr = _add_tile(rows, cols)

    def body(c_ref, w_ref, gh_ref, go_ref, m_ref, v_ref, g_out, d_out, m_out, v_out):
        g = jnp.where(pl.program_id(0) == c_ref[0], gh_ref[...], go_ref[...])
        g_out[0] = g
        d_out[0], m_out[0], v_out[0] = _adamw_math(w_ref[0], g, m_ref[0], v_ref[0])

    full = BS((1, tr, cols), lambda hf, i, c: (hf, i, 0))
    half = BS((tr, cols), lambda hf, i, c: (i, 0))
    return pl.pallas_call(
        body, name=name,
        grid_spec=pltpu.PrefetchScalarGridSpec(
            num_scalar_prefetch=1, grid=(2, rows // tr), in_specs=[full, half, half, full, full],
            out_specs=[full] * 4),
        out_shape=[SDS(w.shape, F32)] * 4, compiler_params=_cparams(("parallel", "parallel")))(c_idx, w, gh, go, m, v)


def _sum_devices(sa, name):
    def body(sa_ref, g_out):
        g = sa_ref[0]
        for d in range(1, N_DEV):
            g = g + sa_ref[d]
        g_out[...] = g

    return pl.pallas_call(body, name=name, out_shape=SDS(sa.shape[1:], F32),
                          compiler_params=pltpu.CompilerParams(vmem_limit_bytes=VMEM_LIMIT))(sa)


def _adamw_small(ws, gs, ms, vs, name):
    k = len(ws)

    def body(*refs):
        ins, outs = refs[:4 * k], refs[4 * k:]
        for i in range(k):
            d, m, v = _adamw_math(ins[i][...], ins[k + i][...], ins[2 * k + i][...], ins[3 * k + i][...])
            outs[i][...], outs[k + i][...], outs[2 * k + i][...] = d, m, v

    return pl.pallas_call(body, name=name, out_shape=[SDS(w.shape, F32) for w in ws] * 3,
                          compiler_params=pltpu.CompilerParams(vmem_limit_bytes=VMEM_LIMIT))(*ws, *gs, *ms, *vs)


HALVED = {"w_in": (2, 1024, 1450), "mla_w_q_up": (2, 256, 192), "mla_w_kv_up": (2, 128, 256),
          "w_branch": (2, 1536, 256), "w_o": (2, 256, 1024), "ffn_w_up": (2, 1024, 1408),
          "ffn_w_down": (2, 704, 1024), "ffn_conv_w": (2, 3, 1408), "meta_tokens": (2, 8, 256)}
SMALL_SHAPE = {"norm1_g": (2, 1024), "fox_forget_b": (2, 8), "fox_q_g": (2, 64), "fox_k_g": (2, 64),
               "mla_q_a_g": (2, 256), "mla_kv_a_g": (2, 128), "mla_q_g": (2, 96), "mla_k_g": (2, 96),
               "swa_q_g": (2, 64), "swa_k_g": (2, 64), "swa_sinks": (2, 8), "norm2_g": (2, 1024),
               "ffn_conv_b": (2, 5632)}
SMALL_ROWS = _pack_rows([SMALL_SHAPE[k] for k in SMALL] + [(1,)], SUBLANES)


def kernel(x, meta_tokens, norm1_g, w_in, fox_forget_b, fox_q_g, fox_k_g, mla_q_a_g, mla_w_q_up, mla_kv_a_g, mla_w_kv_up, mla_q_g, mla_k_g, swa_q_g, swa_k_g, swa_sinks, w_branch, w_o, norm2_g, ffn_w_up, ffn_conv_w, ffn_conv_b, ffn_w_down, loss_target, m_meta_tokens, m_norm1_g, m_w_in, m_fox_forget_b, m_fox_q_g, m_fox_k_g, m_mla_q_a_g, m_mla_w_q_up, m_mla_kv_a_g, m_mla_w_kv_up, m_mla_q_g, m_mla_k_g, m_swa_q_g, m_swa_k_g, m_swa_sinks, m_w_branch, m_w_o, m_norm2_g, m_ffn_w_up, m_ffn_conv_w, m_ffn_conv_b, m_ffn_w_down, v_meta_tokens, v_norm1_g, v_w_in, v_fox_forget_b, v_fox_q_g, v_fox_k_g, v_mla_q_a_g, v_mla_w_q_up, v_mla_kv_a_g, v_mla_w_kv_up, v_mla_q_g, v_mla_k_g, v_swa_q_g, v_swa_k_g, v_swa_sinks, v_w_branch, v_w_o, v_norm2_g, v_ffn_w_up, v_ffn_conv_w, v_ffn_conv_b, v_ffn_w_down):
    w = dict(meta_tokens=meta_tokens, norm1_g=norm1_g, w_in=w_in, fox_forget_b=fox_forget_b, fox_q_g=fox_q_g,
             fox_k_g=fox_k_g, mla_q_a_g=mla_q_a_g, mla_w_q_up=mla_w_q_up, mla_kv_a_g=mla_kv_a_g,
             mla_w_kv_up=mla_w_kv_up, mla_q_g=mla_q_g, mla_k_g=mla_k_g, swa_q_g=swa_q_g, swa_k_g=swa_k_g,
             swa_sinks=swa_sinks, w_branch=w_branch, w_o=w_o, norm2_g=norm2_g, ffn_w_up=ffn_w_up,
             ffn_conv_w=ffn_conv_w, ffn_conv_b=ffn_conv_b, ffn_w_down=ffn_w_down)
    m = dict(meta_tokens=m_meta_tokens, norm1_g=m_norm1_g, w_in=m_w_in, fox_forget_b=m_fox_forget_b,
             fox_q_g=m_fox_q_g, fox_k_g=m_fox_k_g, mla_q_a_g=m_mla_q_a_g, mla_w_q_up=m_mla_w_q_up,
             mla_kv_a_g=m_mla_kv_a_g, mla_w_kv_up=m_mla_w_kv_up, mla_q_g=m_mla_q_g, mla_k_g=m_mla_k_g,
             swa_q_g=m_swa_q_g, swa_k_g=m_swa_k_g, swa_sinks=m_swa_sinks, w_branch=m_w_branch, w_o=m_w_o,
             norm2_g=m_norm2_g, ffn_w_up=m_ffn_w_up, ffn_conv_w=m_ffn_conv_w, ffn_conv_b=m_ffn_conv_b,
             ffn_w_down=m_ffn_w_down)
    v = dict(meta_tokens=v_meta_tokens, norm1_g=v_norm1_g, w_in=v_w_in, fox_forget_b=v_fox_forget_b,
             fox_q_g=v_fox_q_g, fox_k_g=v_fox_k_g, mla_q_a_g=v_mla_q_a_g, mla_w_q_up=v_mla_w_q_up,
             mla_kv_a_g=v_mla_kv_a_g, mla_w_kv_up=v_mla_w_kv_up, mla_q_g=v_mla_q_g, mla_k_g=v_mla_k_g,
             swa_q_g=v_swa_q_g, swa_k_g=v_swa_k_g, swa_sinks=v_swa_sinks, w_branch=v_w_branch, w_o=v_w_o,
             norm2_g=v_norm2_g, ffn_w_up=v_ffn_w_up, ffn_conv_w=v_ffn_conv_w, ffn_conv_b=v_ffn_conv_b,
             ffn_w_down=v_ffn_w_down)
    xi, yi, ci = _me()
    c_idx = ci.astype(jnp.int32).reshape(1)
    j_idx = (2 * xi + yi).astype(jnp.int32).reshape(1)

    sh_names = BIG + FINE
    gathered = _all_gather([w[k].astype(BF16).reshape(HALVED[k]) for k in BIG]
                           + [w[k].reshape(HALVED[k]) for k in FINE], "gather_weights")
    shards = {k: g.reshape((N_CHIPS,) + SHARD_SHAPE[k]) for k, g in zip(sh_names, gathered)}
    wb = _assemble({k: shards[k] for k in BIG})
    fine = _assemble({k: shards[k] for k in FINE})
    small = {k: w[k] for k in SMALL}

    loss, g_x, g_eps, g_fine, g_small = _local_step(x[0], loss_target[0], wb, fine, small)

    g_sh = dict(g_eps)
    g_sh.update({k: jnp.stack(jnp.split(g_fine[k], N_CHIPS, axis=SHARD_AXIS[k])) for k in FINE})
    gs = [g_sh[k].reshape((N_CHIPS,) + HALVED[k]) for k in sh_names]
    spack = _pack([g_small[k] for k in SMALL] + [loss.reshape(1)], SMALL_ROWS, F32)
    r1 = _pair_exchange(gs, "grads_pair_exchange")
    s1 = [_pair_add(g, r, c_idx, "grads_pair_add_" + k) for g, r, k in zip(gs, r1, sh_names)]
    r2, sa = _chip_exchange([s[1] for s in s1], spack, "grads_chip_exchange")
    gh = [_chip_add(s[0], r, j_idx, "grads_chip_add_" + k) for s, r, k in zip(s1, r2, sh_names)]
    go = _half_exchange(gh, "grads_half_exchange")

    grads, deltas, new_m, new_v = {}, {}, {}, {}
    for k, a, b in zip(sh_names, gh, go):
        outs = _adamw(w[k].reshape(HALVED[k]), a, b, m[k].reshape(HALVED[k]), v[k].reshape(HALVED[k]), c_idx,
                      "adamw_" + k)
        for dst, o in zip((grads, deltas, new_m, new_v), outs):
            dst[k] = o.reshape(SHARD_SHAPE[k])
    sm_shapes = [SMALL_SHAPE[k] for k in SMALL] + [(1,)]
    g_sum = _unpack(_sum_devices(sa, "sum_small"), sm_shapes)
    res = _adamw_small([w[k] for k in SMALL], g_sum[:-1], [m[k] for k in SMALL], [v[k] for k in SMALL], "adamw_small")
    ns = len(SMALL)
    grads.update(zip(SMALL, g_sum[:-1]))
    for dst, vals in zip((deltas, new_m, new_v), (res[:ns], res[ns:2 * ns], res[2 * ns:])):
        dst.update(zip(SMALL, vals))
    total_loss = g_sum[-1][0]
    return (total_loss, g_x[None], *[grads[k] for k in WEIGHTS], *[deltas[k] for k in WEIGHTS],
            *[new_m[k] for k in WEIGHTS], *[new_v[k] for k in WEIGHTS])
```

```python
import functools

import jax
import jax.numpy as jnp
from jax import lax
from jax.experimental import pallas as pl
from jax.experimental.pallas import tpu as pltpu

F32 = jnp.float32
BF16 = jnp.bfloat16
SDS = jax.ShapeDtypeStruct
BS = pl.BlockSpec

D_MODEL = 1024
DEPTH = 2
N_META = 16
EPS = 1e-6
HEADS = 8
HEAD_DIM = 64
MLA_Q_RANK = 256
MLA_KV_RANK = 128
MLA_NOPE = 64
MLA_ROPE = 32
MLA_QK = MLA_NOPE + MLA_ROPE
ROPE_THETA = 10000.0
SWA_KV_HEADS = 2
WINDOW = 128
N_BRANCH = 3
BRANCH_WIDTH = 512
D_FF = 2816
IN_WIDTH = 5800
IN_PAD = 6144
N_CHIPS = 4
N_DEV = 8

ADAM_LR = 0.001
ADAM_B1 = 0.9
ADAM_B2 = 0.999
ADAM_EPS = 1e-08
ADAM_WD = 0.01
ADAM_STEP = 10

LANES = 128
SUBLANES = 8
ROW_PAD = 128
CAUSAL_TILE = 384
NEG = -1e30
VMEM_LIMIT = 56 * 1024 * 1024

O_FQ, O_FK, O_FV, O_FF = 0, 512, 1024, 1536
O_CQ, O_CKV, O_KR = 1664, 1920, 2048
O_SQ, O_SK, O_SV, O_G = 2176, 2688, 2816, 2944

SHARDED = ("meta_tokens", "w_in", "mla_w_q_up", "mla_w_kv_up", "w_branch", "w_o", "ffn_w_up", "ffn_conv_w",
           "ffn_w_down")
SHARD_AXIS = {"meta_tokens": 1, "w_in": 2, "mla_w_q_up": 2, "mla_w_kv_up": 2, "w_branch": 3, "w_o": 1,
              "ffn_w_up": 2, "ffn_conv_w": 2, "ffn_w_down": 1}
SHARD_SHAPE = {"meta_tokens": (16, 256), "w_in": (2, 1024, 1450), "mla_w_q_up": (2, 256, 192),
               "mla_w_kv_up": (2, 128, 256), "w_branch": (2, 3, 512, 256), "w_o": (2, 256, 1024),
               "ffn_w_up": (2, 1024, 1408), "ffn_conv_w": (2, 3, 1408), "ffn_w_down": (2, 704, 1024)}
BIG = ("w_in", "mla_w_q_up", "mla_w_kv_up", "w_branch", "w_o", "ffn_w_up", "ffn_w_down")
FINE = ("meta_tokens", "ffn_conv_w")
SMALL = ("norm1_g", "fox_forget_b", "fox_q_g", "fox_k_g", "mla_q_a_g", "mla_kv_a_g", "mla_q_g", "mla_k_g",
         "swa_q_g", "swa_k_g", "swa_sinks", "norm2_g", "ffn_conv_b")
WEIGHTS = ("meta_tokens", "norm1_g", "w_in", "fox_forget_b", "fox_q_g", "fox_k_g", "mla_q_a_g", "mla_w_q_up",
           "mla_kv_a_g", "mla_w_kv_up", "mla_q_g", "mla_k_g", "swa_q_g", "swa_k_g", "swa_sinks", "w_branch", "w_o",
           "norm2_g", "ffn_w_up", "ffn_conv_w", "ffn_conv_b", "ffn_w_down")


def _cparams(sem):
    return pltpu.CompilerParams(dimension_semantics=sem, vmem_limit_bytes=VMEM_LIMIT)


def _div_tile(n, cap, mult=SUBLANES):
    best = None
    for t in range(mult, min(n, cap) + 1, mult):
        if n % t == 0:
            best = t
    return best if best is not None else n


def _rows_tile(n, width, budget=2 << 20):
    return _div_tile(n, max(SUBLANES, budget // (4 * max(width, LANES))))


def _op(fwd, bwd):
    @jax.custom_vjp
    def op(*args):
        return fwd(*args)[0]
    op.defvjp(fwd, bwd)
    return op


def _rms_fwd_call(x, g, denom, name):
    n, c = x.shape
    tr = _rows_tile(n, c)

    def body(x_ref, g_ref, y_ref):
        xv = x_ref[...]
        ms = jnp.sum(xv * xv, axis=-1, keepdims=True) * (1.0 / denom)
        y_ref[...] = xv * lax.rsqrt(ms + EPS) * g_ref[...]

    return pl.pallas_call(
        body, name=name, grid=(n // tr,),
        in_specs=[BS((tr, c), lambda i: (i, 0)), BS((1, c), lambda i: (0, 0))],
        out_specs=BS((tr, c), lambda i: (i, 0)), out_shape=SDS((n, c), F32),
        compiler_params=_cparams(("parallel",)))(x, g)


def _rms_bwd_call(x, g, dy, denom, name):
    n, c = x.shape
    tr = _rows_tile(n, c)

    def body(x_ref, g_ref, dy_ref, dx_ref, dg_ref):
        xv = x_ref[...]
        dy = dy_ref[...]
        ms = jnp.sum(xv * xv, axis=-1, keepdims=True) * (1.0 / denom)
        r = lax.rsqrt(ms + EPS)
        xh = xv * r
        dxh = dy * g_ref[...]
        dx_ref[...] = r * (dxh - xh * (jnp.sum(dxh * xh, axis=-1, keepdims=True) * (1.0 / denom)))

        @pl.when(pl.program_id(0) == 0)
        def _():
            dg_ref[...] = jnp.zeros_like(dg_ref)

        dg_ref[...] += jnp.sum(dy * xh, axis=0, keepdims=True)

    return pl.pallas_call(
        body, name=name, grid=(n // tr,),
        in_specs=[BS((tr, c), lambda i: (i, 0)), BS((1, c), lambda i: (0, 0)), BS((tr, c), lambda i: (i, 0))],
        out_specs=[BS((tr, c), lambda i: (i, 0)), BS((1, c), lambda i: (0, 0))],
        out_shape=[SDS((n, c), F32), SDS((1, c), F32)],
        compiler_params=_cparams(("arbitrary",)))(x, g, dy)


def rms_norm(x, g, denom, name):
    def fwd(x, g):
        return _rms_fwd_call(x, g, denom, name + "_f"), (x, g)

    def bwd(res, dy):
        return tuple(_rms_bwd_call(res[0], res[1], dy, denom, name + "_b"))

    return _op(fwd, bwd)(x, g)


def _mm_call(a, b, mode, res, name):
    if mode == "nn":
        (m, kc), n = a.shape, b.shape[1]
    elif mode == "nt":
        (m, kc), n = a.shape, b.shape[0]
    else:
        (kc, m), n = a.shape, b.shape[1]
    if mode == "tn":
        tk = _div_tile(kc, 528)
        tm = _div_tile(m, 1408, LANES)
        tn = _div_tile(n, 2048, LANES)
    else:
        tk = kc if kc <= 2816 else _div_tile(kc, 1024, LANES)
        tm = _div_tile(m, max(LANES, (9 << 19) // (4 * tk)))
        tn = _div_tile(n, 1408 if mode == "nt" else 512, LANES)
    nk = kc // tk
    dims = {"nn": (((1,), (0,)), ((), ())), "nt": (((1,), (1,)), ((), ())), "tn": (((0,), (0,)), ((), ()))}[mode]

    def body(*refs):
        if res is None:
            a_ref, b_ref, o_ref, acc_ref = refs
            r_ref = None
        else:
            a_ref, b_ref, r_ref, o_ref, acc_ref = refs
        k = pl.program_id(2)

        @pl.when(k == 0)
        def _():
            acc_ref[...] = jnp.zeros_like(acc_ref)

        acc_ref[...] += lax.dot_general(a_ref[...].astype(BF16), b_ref[...].astype(BF16), dims,
                                        preferred_element_type=F32)

        @pl.when(k == nk - 1)
        def _():
            if r_ref is None:
                o_ref[...] = acc_ref[...]
            else:
                o_ref[...] = r_ref[...] + acc_ref[...]

    a_spec = BS((tk, tm), lambda i, j, k: (k, i)) if mode == "tn" else BS((tm, tk), lambda i, j, k: (i, k))
    b_spec = BS((tn, tk), lambda i, j, k: (j, k)) if mode == "nt" else BS((tk, tn), lambda i, j, k: (k, j))
    o_spec = BS((tm, tn), lambda i, j, k: (i, j))
    ins, args = [a_spec, b_spec], [a, b]
    if res is not None:
        ins.append(o_spec)
        args.append(res)
    return pl.pallas_call(
        body, name=name, grid=(m // tm, n // tn, nk), in_specs=ins, out_specs=o_spec,
        out_shape=SDS((m, n), F32), scratch_shapes=[pltpu.VMEM((tm, tn), F32)],
        compiler_params=_cparams(("parallel", "parallel", "arbitrary")))(*args)


def linear(a, w, eps, name, res=None):
    if res is None:
        def fwd(a, w, eps):
            return _mm_call(a, w, "nn", None, name + "_f"), (a, w)

        def bwd(r, dc):
            a, w = r
            return (_mm_call(dc, w, "nt", None, name + "_da"), jnp.zeros_like(w),
                    _mm_call(a, dc, "tn", None, name + "_dw"))

        return _op(fwd, bwd)(a, w, eps)

    def fwd_r(a, w, eps, res):
        return _mm_call(a, w, "nn", res, name + "_f"), (a, w)

    def bwd_r(r, dc):
        a, w = r
        return (_mm_call(dc, w, "nt", None, name + "_da"), jnp.zeros_like(w),
                _mm_call(a, dc, "tn", None, name + "_dw"), dc)

    return _op(fwd_r, bwd_r)(a, w, eps, res)


CT = 128


def _tri_dot(v, upper):
    r = lax.broadcasted_iota(jnp.int32, (CT, CT), 0)
    c = lax.broadcasted_iota(jnp.int32, (CT, CT), 1)
    tri = jnp.where((r <= c) if upper else (r >= c), 1.0, 0.0).astype(F32)
    return jnp.dot(v, tri, preferred_element_type=F32, precision=lax.Precision.HIGHEST)


def _gate_fwd_call(z, b, name):
    h, n = z.shape

    def body(z_ref, b_ref, c_ref, carry):
        @pl.when(pl.program_id(0) == 0)
        def _():
            carry[...] = jnp.zeros_like(carry)

        x = z_ref[...] + b_ref[...]
        ls = jnp.minimum(x, 0.0) - jnp.log(1.0 + jnp.exp(-jnp.abs(x)))
        c_ref[...] = _tri_dot(ls, True) + carry[...]
        carry[...] += jnp.sum(ls, axis=1, keepdims=True)

    return pl.pallas_call(
        body, name=name, grid=(n // CT,),
        in_specs=[BS((h, CT), lambda j: (0, j)), BS((h, 1), lambda j: (0, 0))],
        out_specs=BS((h, CT), lambda j: (0, j)), out_shape=SDS((h, n), F32),
        scratch_shapes=[pltpu.VMEM((h, 1), F32)],
        compiler_params=_cparams(("arbitrary",)))(z, b)


def _gate_bwd_call(z, b, dc, name):
    h, n = z.shape
    nt = n // CT

    def body(z_ref, b_ref, dc_ref, dz_ref, db_ref, carry):
        @pl.when(pl.program_id(0) == 0)
        def _():
            carry[...] = jnp.zeros_like(carry)
            db_ref[...] = jnp.zeros_like(db_ref)

        dcv = dc_ref[...]
        dls = _tri_dot(dcv, False) + carry[...]
        carry[...] += jnp.sum(dcv, axis=1, keepdims=True)
        x = z_ref[...] + b_ref[...]
        e = jnp.exp(-jnp.abs(x))
        dz = dls * jnp.where(x >= 0, e / (1.0 + e), 1.0 / (1.0 + e))
        dz_ref[...] = dz
        db_ref[...] += jnp.sum(dz, axis=1, keepdims=True)

    rev = lambda j: (0, nt - 1 - j)
    return pl.pallas_call(
        body, name=name, grid=(nt,),
        in_specs=[BS((h, CT), rev), BS((h, 1), lambda j: (0, 0)), BS((h, CT), rev)],
        out_specs=[BS((h, CT), rev), BS((h, 1), lambda j: (0, 0))],
        out_shape=[SDS((h, n), F32), SDS((h, 1), F32)],
        scratch_shapes=[pltpu.VMEM((h, 1), F32)],
        compiler_params=_cparams(("arbitrary",)))(z, b, dc)


def forget_cumsum(z, b, name):
    def fwd(z, b):
        return _gate_fwd_call(z, b, name + "_f"), (z, b)

    def bwd(res, dc):
        return tuple(_gate_bwd_call(res[0], res[1], dc, name + "_b"))

    return _op(fwd, bwd)(z, b)


def _rope_call(x, cos, sin, name):
    h, n, d = x.shape
    tr = _rows_tile(n, d)

    def body(x_ref, c_ref, s_ref, y_ref):
        xv = x_ref[0]
        lane = lax.broadcasted_iota(jnp.int32, xv.shape, 1)
        rot = jnp.where(lane < MLA_NOPE + MLA_ROPE // 2, -pltpu.roll(xv, d - MLA_ROPE // 2, 1),
                        pltpu.roll(xv, MLA_ROPE // 2, 1))
        y_ref[0] = xv * c_ref[...] + rot * s_ref[...]

    return pl.pallas_call(
        body, name=name, grid=(n // tr, h),
        in_specs=[BS((1, tr, d), lambda i, hh: (hh, i, 0)), BS((tr, d), lambda i, hh: (i, 0)),
                  BS((tr, d), lambda i, hh: (i, 0))],
        out_specs=BS((1, tr, d), lambda i, hh: (hh, i, 0)), out_shape=SDS((h, n, d), F32),
        compiler_params=_cparams(("parallel", "parallel")))(x, cos, sin)


def rope(x, cos, sin, name):
    def fwd(x, cos, sin):
        return _rope_call(x, cos, sin, name + "_f"), (cos, sin)

    def bwd(res, dy):
        cos, sin = res
        return _rope_call(dy, cos, -sin, name + "_b"), jnp.zeros_like(cos), jnp.zeros_like(sin)

    return _op(fwd, bwd)(x, cos, sin)


NT_DIMS = (((1,), (1,)), ((), ()))
TN_DIMS = (((0,), (0,)), ((), ()))
HEADS_PER_STEP = 2


def _causal_tile(n):
    return CAUSAL_TILE if n % CAUSAL_TILE == 0 else ROW_PAD


def _causal_fwd_call(q, k, v, ck_r, fox, scale, name):
    h, n, dk = q.shape
    dv = v.shape[2]
    t = _causal_tile(n)
    nq = n // t
    hb = HEADS_PER_STEP

    def body(*refs):
        if fox:
            q_ref, k_ref, v_ref, ck_ref, o_ref, lse_ref, m_scr, l_scr, acc_scr = refs
        else:
            q_ref, k_ref, v_ref, o_ref, lse_ref, m_scr, l_scr, acc_scr = refs
            ck_ref = None
        qi = pl.program_id(1)
        qbs = [q_ref[e].astype(BF16) for e in range(hb)]
        m_scr[...] = jnp.full(m_scr.shape, NEG, F32)
        l_scr[...] = jnp.zeros_like(l_scr)
        acc_scr[...] = jnp.zeros_like(acc_scr)

        def process(j, masked):
            off = pl.multiple_of(j * t, t)
            if masked:
                rows = lax.broadcasted_iota(jnp.int32, (t, t), 0)
                cols = lax.broadcasted_iota(jnp.int32, (t, t), 1)
                valid = cols <= rows
            for e in range(hb):
                kb = k_ref[e, pl.ds(off, t), :].astype(BF16)
                vb = v_ref[e, pl.ds(off, t), :].astype(BF16)
                s = lax.dot_general(qbs[e], kb, NT_DIMS, preferred_element_type=F32) * scale
                if fox:
                    s = s - ck_ref[e, j]
                if masked:
                    s = jnp.where(valid, s, NEG)
                m_old = m_scr[e]
                m_new = jnp.maximum(m_old, jnp.max(s, axis=1, keepdims=True))
                alpha = jnp.exp(m_old - m_new)
                p = jnp.exp(s - jnp.tile(m_new, (1, t // LANES)))
                l_scr[e] = alpha * l_scr[e] + jnp.sum(p, axis=1, keepdims=True)
                acc_scr[e] = alpha[:, :dv] * acc_scr[e] + jnp.dot(p.astype(BF16), vb, preferred_element_type=F32)
                m_scr[e] = m_new

        def step(j, carry):
            process(j, False)
            return carry

        lax.fori_loop(0, qi, step, 0)
        process(qi, True)
        for e in range(hb):
            l = l_scr[e]
            o_ref[e] = acc_scr[e] / l[:, :dv]
            lse_ref[e, 0] = jnp.transpose(m_scr[e] + jnp.log(l))[0:1, :]

    ins = [BS((hb, t, dk), lambda a, b: (a, b, 0)), BS((hb, n, dk), lambda a, b: (a, 0, 0)),
           BS((hb, n, dv), lambda a, b: (a, 0, 0))]
    args = [q, k, v]
    if fox:
        ins.append(BS((hb, nq, 1, t), lambda a, b: (a, 0, 0, 0)))
        args.append(ck_r)
    return pl.pallas_call(
        body, name=name, grid=(h // hb, nq), in_specs=ins,
        out_specs=[BS((hb, t, dv), lambda a, b: (a, b, 0)), BS((hb, 1, 1, t), lambda a, b: (a, b, 0, 0))],
        out_shape=[SDS((h, n, dv), F32), SDS((h, nq, 1, t), F32)],
        scratch_shapes=[pltpu.VMEM((hb, t, LANES), F32), pltpu.VMEM((hb, t, LANES), F32), pltpu.VMEM((hb, t, dv), F32)],
        compiler_params=_cparams(("parallel", "arbitrary")))(*args)


def _causal_bwd_call(q, k, v, do, o, lse_r, ck_r, fox, scale, name):
    h, n, dk = q.shape
    dv = v.shape[2]
    t = _causal_tile(n)
    nq = n // t
    hb = HEADS_PER_STEP

    def body(*refs):
        it = iter(refs)
        q_ref, k_ref, v_ref, do_ref, o_ref, lse_ref = (next(it) for _ in range(6))
        ck_ref = next(it) if fox else None
        dq_ref, dk_ref, dv_ref = next(it), next(it), next(it)
        dck_ref, dcq_ref = (next(it), next(it)) if fox else (None, None)
        delta_scr, dk_scr, dv_scr = next(it), next(it), next(it)
        dck_scr = next(it) if fox else None
        kj = pl.program_id(1)

        @pl.when(kj == 0)
        def _():
            dq_ref[...] = jnp.zeros_like(dq_ref)
            if fox:
                dcq_ref[...] = jnp.zeros_like(dcq_ref)
            ones = jnp.ones((SUBLANES, dv), F32)

            def fill(qi, carry):
                off = pl.multiple_of(qi * t, t)
                for e in range(hb):
                    prod = do_ref[e, pl.ds(off, t), :] * o_ref[e, pl.ds(off, t), :]
                    delta_scr[e, qi] = lax.dot_general(ones, prod, NT_DIMS, preferred_element_type=F32,
                                                       precision=lax.Precision.HIGHEST)[0:1, :]
                return carry

            lax.fori_loop(0, nq, fill, 0)

        kbs = [k_ref[e].astype(BF16) for e in range(hb)]
        vbs = [v_ref[e].astype(BF16) for e in range(hb)]
        dk_scr[...] = jnp.zeros_like(dk_scr)
        dv_scr[...] = jnp.zeros_like(dv_scr)
        if fox:
            dck_scr[...] = jnp.zeros_like(dck_scr)
            ckcs = [jnp.tile(jnp.transpose(jnp.broadcast_to(ck_ref[e, 0], (LANES, t))), (1, t // LANES))
                    for e in range(hb)]

        def process(qi, masked):
            off = pl.multiple_of(qi * t, t)
            if masked:
                krows = lax.broadcasted_iota(jnp.int32, (t, t), 0)
                qcols = lax.broadcasted_iota(jnp.int32, (t, t), 1)
                valid = krows <= qcols
            for e in range(hb):
                qb = q_ref[e, pl.ds(off, t), :].astype(BF16)
                dob = do_ref[e, pl.ds(off, t), :].astype(BF16)
                st = lax.dot_general(kbs[e], qb, NT_DIMS, preferred_element_type=F32) * scale
                if fox:
                    st = st - ckcs[e]
                pt = jnp.exp(st - lse_ref[e, qi])
                if masked:
                    pt = jnp.where(valid, pt, 0.0)
                dv_scr[e] += jnp.dot(pt.astype(BF16), dob, preferred_element_type=F32)
                dpt = lax.dot_general(vbs[e], dob, NT_DIMS, preferred_element_type=F32)
                dst = pt * (dpt - delta_scr[e, qi])
                if fox:
                    dck_scr[e] -= jnp.sum(dst, axis=1, keepdims=True)
                    dcq_ref[e, qi] += jnp.sum(dst, axis=0, keepdims=True)
                dsb = (dst * scale).astype(BF16)
                dk_scr[e] += jnp.dot(dsb, qb, preferred_element_type=F32)
                dq_ref[e, pl.ds(off, t), :] += lax.dot_general(dsb, kbs[e], TN_DIMS, preferred_element_type=F32)

        def step(qi, carry):
            process(qi, False)
            return carry

        process(kj, True)
        lax.fori_loop(kj + 1, nq, step, 0)
        dk_ref[...] = dk_scr[...]
        dv_ref[...] = dv_scr[...]
        if fox:
            for e in range(hb):
                dck_ref[e, 0] = jnp.transpose(jnp.broadcast_to(dck_scr[e], (t, LANES)))[0:1, :]

    whole = lambda a, b: (a, 0, 0)
    tile = lambda a, b: (a, b, 0)
    rowv = lambda a, b: (a, 0, 0, 0)
    rowt = lambda a, b: (a, b, 0, 0)
    ins = [BS((hb, n, dk), whole), BS((hb, t, dk), tile), BS((hb, t, dv), tile), BS((hb, n, dv), whole),
           BS((hb, n, dv), whole), BS((hb, nq, 1, t), rowv)]
    args = [q, k, v, do, o, lse_r]
    outs = [BS((hb, n, dk), whole), BS((hb, t, dk), tile), BS((hb, t, dv), tile)]
    oshape = [SDS((h, n, dk), F32), SDS((h, n, dk), F32), SDS((h, n, dv), F32)]
    scratch = [pltpu.VMEM((hb, nq, 1, t), F32), pltpu.VMEM((hb, t, dk), F32), pltpu.VMEM((hb, t, dv), F32)]
    if fox:
        ins.append(BS((hb, 1, 1, t), rowt))
        args.append(ck_r)
        outs += [BS((hb, 1, 1, t), rowt), BS((hb, nq, 1, t), rowv)]
        oshape += [SDS((h, nq, 1, t), F32), SDS((h, nq, 1, t), F32)]
        scratch.append(pltpu.VMEM((hb, t, 1), F32))
    return pl.pallas_call(
        body, name=name, grid=(h // hb, nq), in_specs=ins, out_specs=outs, out_shape=oshape, scratch_shapes=scratch,
        compiler_params=_cparams(("arbitrary", "arbitrary")))(*args)


def causal_attention(q, k, v, c, scale, name):
    h, n, _ = q.shape
    t = _causal_tile(n)
    nq = n // t
    dv = v.shape[2]
    fox = c is not None

    def run_fwd(q, k, v, c):
        ck_r = c.reshape(h, nq, 1, t) if fox else None
        o, lse = _causal_fwd_call(q, k, v, ck_r, fox, scale, name + "_f")
        return o, (q, k, v, c, o, lse)

    def run_bwd(res, do):
        q, k, v, c, o, lse = res
        ck_r = c.reshape(h, nq, 1, t) if fox else None
        outs = _causal_bwd_call(q, k, v, do, o, lse, ck_r, fox, scale, name + "_b")
        return outs[0], outs[1], outs[2], ((outs[3] + outs[4]).reshape(h, n) if fox else None)

    return _op(run_fwd, run_bwd)(q, k, v, c)


SWA_T = 128


def _swa_masks(qi):
    t = SWA_T
    r = lax.broadcasted_iota(jnp.int32, (t, 3 * t), 0)
    c = lax.broadcasted_iota(jnp.int32, (t, 3 * t), 1)
    seg0 = c < t
    seg1 = (c >= t) & (c < 2 * t)
    jp = jnp.maximum(qi - 1, 0)
    kpos = jnp.where(seg0, c, jnp.where(seg1, jp * t + c - t, qi * t + c - 2 * t))
    dist = qi * t + r - kpos
    band = (dist >= 0) & ((dist < WINDOW) | (kpos < N_META))
    valid = (seg0 & (kpos < N_META) & (qi >= 2)) | (jnp.logical_not(seg0) & band & (jnp.logical_not(seg1) | (qi >= 1)))
    return valid, dist.astype(F32)


def _swa_cat(ref, qi):
    t = SWA_T
    jp = jnp.maximum(qi - 1, 0)
    return jnp.concatenate([ref[0, 0:t, :], ref[0, pl.ds(pl.multiple_of(jp * t, t), t), :],
                            ref[0, pl.ds(pl.multiple_of(qi * t, t), t), :]], axis=0).astype(BF16)


def _swa_fwd_call(q, k, v, sinks, slopes, scale, name):
    hq, n, d = q.shape
    hkv = k.shape[0]
    g = hq // hkv
    t = SWA_T
    nq = n // t

    def body(q_ref, k_ref, v_ref, sink_ref, slope_ref, o_ref, lse_ref):
        grp = pl.program_id(0)
        qi = pl.program_id(1)
        valid, dist = _swa_masks(qi)
        kc = _swa_cat(k_ref, qi)
        vc = _swa_cat(v_ref, qi)
        qs = jnp.concatenate([q_ref[e] for e in range(g)], axis=0).astype(BF16)
        s_all = lax.dot_general(qs, kc, NT_DIMS, preferred_element_type=F32) * scale
        ps, ls, ms = [], [], []
        for e in range(g):
            hh = grp * g + e
            s = jnp.where(valid, s_all[e * t:(e + 1) * t] - slope_ref[hh] * dist, NEG)
            m = jnp.maximum(jnp.max(s, axis=1, keepdims=True), sink_ref[hh])
            p = jnp.exp(s - m)
            ls.append(jnp.sum(p, axis=1, keepdims=True) + jnp.exp(sink_ref[hh] - m))
            ms.append(m)
            ps.append(p.astype(BF16))
        acc = jnp.dot(jnp.concatenate(ps, axis=0), vc, preferred_element_type=F32)
        for e in range(g):
            o_ref[e] = acc[e * t:(e + 1) * t] / ls[e]
            lse_ref[e] = ms[e] + jnp.log(ls[e])

    return pl.pallas_call(
        body, name=name, grid=(hkv, nq),
        in_specs=[BS((g, t, d), lambda a, b: (a, b, 0)), BS((1, n, d), lambda a, b: (a, 0, 0)),
                  BS((1, n, d), lambda a, b: (a, 0, 0)), BS(memory_space=pltpu.SMEM), BS(memory_space=pltpu.SMEM)],
        out_specs=[BS((g, t, d), lambda a, b: (a, b, 0)), BS((g, t, 1), lambda a, b: (a, b, 0))],
        out_shape=[SDS((hq, n, d), F32), SDS((hq, n, 1), F32)],
        compiler_params=_cparams(("parallel", "parallel")))(q, k, v, sinks, slopes)


def _swa_bwd_call(q, k, v, o, lse, do, sinks, slopes, scale, name):
    hq, n, d = q.shape
    hkv = k.shape[0]
    g = hq // hkv
    t = SWA_T
    nq = n // t

    def body(q_ref, k_ref, v_ref, o_ref, lse_ref, do_ref, sink_ref, slope_ref, dq_ref, dk_ref, dv_ref, ds_ref):
        grp = pl.program_id(0)
        qi = pl.program_id(1)

        @pl.when(qi == 0)
        def _():
            dk_ref[...] = jnp.zeros_like(dk_ref)
            dv_ref[...] = jnp.zeros_like(dv_ref)
            ds_ref[...] = jnp.zeros_like(ds_ref)

        valid, dist = _swa_masks(qi)
        kc = _swa_cat(k_ref, qi)
        vc = _swa_cat(v_ref, qi)
        qs = jnp.concatenate([q_ref[e] for e in range(g)], axis=0).astype(BF16)
        dos = jnp.concatenate([do_ref[e] for e in range(g)], axis=0).astype(BF16)
        s_all = lax.dot_general(qs, kc, NT_DIMS, preferred_element_type=F32) * scale
        dp_all = lax.dot_general(dos, vc, NT_DIMS, preferred_element_type=F32)
        ps, dss = [], []
        for e in range(g):
            hh = grp * g + e
            lse_e = lse_ref[e]
            delta = jnp.sum(do_ref[e] * o_ref[e], axis=1, keepdims=True)
            s = s_all[e * t:(e + 1) * t] - slope_ref[hh] * dist
            p = jnp.where(valid, jnp.exp(s - lse_e), 0.0)
            ds = p * (dp_all[e * t:(e + 1) * t] - delta)
            ps.append(p.astype(BF16))
            dss.append((ds * scale).astype(BF16))
            ds_ref[e] += -jnp.sum(jnp.exp(sink_ref[hh] - lse_e) * delta)
        p_st = jnp.concatenate(ps, axis=0)
        ds_st = jnp.concatenate(dss, axis=0)
        dq = jnp.dot(ds_st, kc, preferred_element_type=F32)
        for e in range(g):
            dq_ref[e] = dq[e * t:(e + 1) * t]
        dkc = lax.dot_general(ds_st, qs, TN_DIMS, preferred_element_type=F32)
        dvc = lax.dot_general(p_st, dos, TN_DIMS, preferred_element_type=F32)
        jp = jnp.maximum(qi - 1, 0)
        for seg, off in enumerate((0, pl.multiple_of(jp * t, t), pl.multiple_of(qi * t, t))):
            dk_ref[0, pl.ds(off, t), :] += dkc[seg * t:(seg + 1) * t]
            dv_ref[0, pl.ds(off, t), :] += dvc[seg * t:(seg + 1) * t]

    tile = lambda a, b: (a, b, 0)
    whole = lambda a, b: (a, 0, 0)
    return pl.pallas_call(
        body, name=name, grid=(hkv, nq),
        in_specs=[BS((g, t, d), tile), BS((1, n, d), whole), BS((1, n, d), whole), BS((g, t, d), tile),
                  BS((g, t, 1), tile), BS((g, t, d), tile), BS(memory_space=pltpu.SMEM), BS(memory_space=pltpu.SMEM)],
        out_specs=[BS((g, t, d), tile), BS((1, n, d), whole), BS((1, n, d), whole), BS((g, 1, LANES), whole)],
        out_shape=[SDS((hq, n, d), F32), SDS((hkv, n, d), F32), SDS((hkv, n, d), F32), SDS((hq, 1, LANES), F32)],
        compiler_params=_cparams(("arbitrary", "arbitrary")))(q, k, v, o, lse, do, sinks, slopes)


def window_attention(q, k, v, sinks, slopes, scale, name):
    def run_fwd(q, k, v, sinks, slopes):
        o, lse = _swa_fwd_call(q, k, v, sinks, slopes, scale, name + "_f")
        return o, (q, k, v, sinks, slopes, o, lse)

    def run_bwd(res, do):
        q, k, v, sinks, slopes, o, lse = res
        dq, dk, dv, ds = _swa_bwd_call(q, k, v, o, lse, do, sinks, slopes, scale, name + "_b")
        return dq, dk, dv, ds[:, 0, 0], jnp.zeros_like(slopes)

    return _op(run_fwd, run_bwd)(q, k, v, sinks, slopes)


def _sigmoid(x):
    return 1.0 / (1.0 + jnp.exp(-x))


def _merge_fwd_call(gs, ys, name):
    n, c = ys[0].shape
    tr = _rows_tile(n, c, 1 << 20)

    def body(g0, g1, g2, y0, y1, y2, m_ref):
        m_ref[...] = (_sigmoid(g0[...]) * y0[...] + _sigmoid(g1[...]) * y1[...]) + _sigmoid(g2[...]) * y2[...]

    spec = BS((tr, c), lambda i: (i, 0))
    return pl.pallas_call(
        body, name=name, grid=(n // tr,), in_specs=[spec] * 6, out_specs=spec, out_shape=SDS((n, c), F32),
        compiler_params=_cparams(("parallel",)))(*gs, *ys)


def _merge_bwd_call(gs, ys, dm, name):
    n, c = ys[0].shape
    tr = _rows_tile(n, c, 1 << 20)

    def body(g0, g1, g2, y0, y1, y2, dm_ref, dg0, dg1, dg2, dy0, dy1, dy2):
        d = dm_ref[...]
        for g, y, dg, dy in ((g0, y0, dg0, dy0), (g1, y1, dg1, dy1), (g2, y2, dg2, dy2)):
            s = _sigmoid(g[...])
            dy[...] = d * s
            dg[...] = d * y[...] * (s * (1.0 - s))

    spec = BS((tr, c), lambda i: (i, 0))
    return pl.pallas_call(
        body, name=name, grid=(n // tr,), in_specs=[spec] * 7, out_specs=[spec] * 6,
        out_shape=[SDS((n, c), F32)] * 6, compiler_params=_cparams(("parallel",)))(*gs, *ys, dm)


def gated_merge(gs, ys, name):
    def fwd(gs, ys):
        return _merge_fwd_call(gs, ys, name + "_f"), (gs, ys)

    def bwd(res, dm):
        out = _merge_bwd_call(res[0], res[1], dm, name + "_b")
        return tuple(out[:3]), tuple(out[3:])

    return _op(fwd, bwd)(tuple(gs), tuple(ys))


CONV_TR = 264
CONV_TC = 1408


def _conv_tiles(n, f):
    tr = CONV_TR if n % CONV_TR == 0 else _div_tile(n, CONV_TR)
    tc = CONV_TC if f % CONV_TC == 0 else f
    return tr, tc


def _shift_down(cur, halo, first, tr):
    halo = jnp.where(first, 0.0, halo)
    row = lax.broadcasted_iota(jnp.int32, cur.shape, 0)
    h7, h6 = halo[7:8, :], halo[6:7, :]
    u1 = jnp.where(row == 0, h7, pltpu.roll(cur, 1, 0))
    u2 = jnp.where(row == 0, h6, jnp.where(row == 1, h7, pltpu.roll(cur, 2, 0)))
    return u1, u2


def _conv_lin(cur, u1, u2, w_ref, b_ref):
    return ((b_ref[...] + w_ref[0:1, :] * u2) + w_ref[1:2, :] * u1) + w_ref[2:3, :] * cur


def _conv_in_specs(tr, tc, nj):
    sub = tr // SUBLANES
    prev = lambda j, i: (jnp.maximum(i * sub - 1, 0), j)
    prev_v = lambda j, i: (jnp.maximum(i * sub - 1, 0), j + nj)
    return [BS((tr, tc), lambda j, i: (i, j)), BS((SUBLANES, tc), prev),
            BS((tr, tc), lambda j, i: (i, j + nj)), BS((SUBLANES, tc), prev_v),
            BS((3, tc), lambda j, i: (0, j)), BS((3, tc), lambda j, i: (0, j + nj)),
            BS((1, tc), lambda j, i: (0, j)), BS((1, tc), lambda j, i: (0, j + nj))]


def _conv_fwd_call(u, cw, cb, name):
    n, f2 = u.shape
    f = f2 // 2
    tr, tc = _conv_tiles(n, f)
    nj = f // tc

    def body(ug, ugh, uv, uvh, wg, wv, bg, bv, a_ref):
        first = pl.program_id(1) == 0
        g1, g2 = _shift_down(ug[...], ugh[...], first, tr)
        v1, v2 = _shift_down(uv[...], uvh[...], first, tr)
        cg = _conv_lin(ug[...], g1, g2, wg, bg)
        cv = _conv_lin(uv[...], v1, v2, wv, bv)
        a_ref[...] = cg * _sigmoid(cg) * cv

    return pl.pallas_call(
        body, name=name, grid=(nj, n // tr), in_specs=_conv_in_specs(tr, tc, nj),
        out_specs=BS((tr, tc), lambda j, i: (i, j)), out_shape=SDS((n, f), F32),
        compiler_params=_cparams(("parallel", "parallel")))(u, u, u, u, cw, cw, cb, cb)


def _conv_bwd_dc_call(u, cw, cb, da, name):
    n, f2 = u.shape
    f = f2 // 2
    tr, tc = _conv_tiles(n, f)
    nj = f // tc

    def body(ug, ugh, uv, uvh, wg, wv, bg, bv, da_ref, dc_ref, dw_ref, db_ref):
        first = pl.program_id(1) == 0
        g0, v0 = ug[...], uv[...]
        g1, g2 = _shift_down(g0, ugh[...], first, tr)
        v1, v2 = _shift_down(v0, uvh[...], first, tr)
        cg = _conv_lin(g0, g1, g2, wg, bg)
        cv = _conv_lin(v0, v1, v2, wv, bv)
        d = da_ref[...]
        s = _sigmoid(cg)
        dcg = d * cv * (s * (1.0 + cg * (1.0 - s)))
        dcv = d * (cg * s)
        dc_ref[0] = dcg
        dc_ref[1] = dcv

        @pl.when(first)
        def _():
            dw_ref[...] = jnp.zeros_like(dw_ref)
            db_ref[...] = jnp.zeros_like(db_ref)

        for p, dc, taps in ((0, dcg, (g2, g1, g0)), (1, dcv, (v2, v1, v0))):
            for t in range(3):
                dw_ref[p, t:t + 1, :] += jnp.sum(dc * taps[t], axis=0, keepdims=True)
            db_ref[p] += jnp.sum(dc, axis=0, keepdims=True)

    return pl.pallas_call(
        body, name=name, grid=(nj, n // tr),
        in_specs=_conv_in_specs(tr, tc, nj) + [BS((tr, tc), lambda j, i: (i, j))],
        out_specs=[BS((2, tr, tc), lambda j, i: (0, i, j)), BS((2, 3, tc), lambda j, i: (0, 0, j)),
                   BS((2, 1, tc), lambda j, i: (0, 0, j))],
        out_shape=[SDS((2, n, f), F32), SDS((2, 3, f), F32), SDS((2, 1, f), F32)],
        compiler_params=_cparams(("arbitrary", "arbitrary")))(u, u, u, u, cw, cw, cb, cb, da)


def _conv_bwd_du_call(dc, cw, name):
    _, n, f = dc.shape
    tr, tc = _conv_tiles(n, f)
    nj = f // tc
    ni = n // tr
    sub = tr // SUBLANES

    def body(c_ref, nx_ref, w_ref, du_ref):
        cur = c_ref[0]
        nxt = jnp.where(pl.program_id(2) == ni - 1, 0.0, nx_ref[0])
        row = lax.broadcasted_iota(jnp.int32, cur.shape, 0)
        n0, n1 = nxt[0:1, :], nxt[1:2, :]
        d1 = jnp.where(row == tr - 1, n0, pltpu.roll(cur, tr - 1, 0))
        d2 = jnp.where(row == tr - 1, n1, jnp.where(row == tr - 2, n0, pltpu.roll(cur, tr - 2, 0)))
        du_ref[...] = (w_ref[2:3, :] * cur + w_ref[1:2, :] * d1) + w_ref[0:1, :] * d2

    nxt_map = lambda p, j, i: (p, jnp.minimum((i + 1) * sub, n // SUBLANES - 1), j)
    return pl.pallas_call(
        body, name=name, grid=(2, nj, ni),
        in_specs=[BS((1, tr, tc), lambda p, j, i: (p, i, j)), BS((1, SUBLANES, tc), nxt_map),
                  BS((3, tc), lambda p, j, i: (0, p * nj + j))],
        out_specs=BS((tr, tc), lambda p, j, i: (i, p * nj + j)), out_shape=SDS((n, 2 * f), F32),
        compiler_params=_cparams(("parallel", "parallel", "parallel")))(dc, dc, cw)


def conv_glu(u, cw, cb, name):
    def fwd(u, cw, cb):
        return _conv_fwd_call(u, cw, cb, name + "_f"), (u, cw, cb)

    def bwd(res, da):
        u, cw, cb = res
        dc, dw, db = _conv_bwd_dc_call(u, cw, cb, da, name + "_bc")
        du = _conv_bwd_du_call(dc, cw, name + "_bu")
        return du, jnp.concatenate([dw[0], dw[1]], axis=-1), jnp.concatenate([db[0], db[1]], axis=-1)

    return _op(fwd, bwd)(u, cw, cb)


def _loss_call(y, t, n_real, name):
    n, c = y.shape
    tr = _rows_tile(n, c, 1 << 20)

    def body(y_ref, t_ref, dy_ref, l_ref):
        i = pl.program_id(0)
        row = i * tr + lax.broadcasted_iota(jnp.int32, (tr, c), 0)
        real = (row >= N_META) & (row < N_META + n_real)
        e = jnp.where(real, y_ref[...] - t_ref[...], 0.0)
        dy_ref[...] = e * (1.0 / c)

        @pl.when(i == 0)
        def _():
            l_ref[...] = jnp.zeros_like(l_ref)

        l_ref[...] += 0.5 * jnp.sum(jnp.sum(e * e, axis=-1, keepdims=True) * (1.0 / c), axis=0, keepdims=True)

    spec = BS((tr, c), lambda i: (i, 0))
    return pl.pallas_call(
        body, name=name, grid=(n // tr,), in_specs=[spec, spec],
        out_specs=[spec, BS((1, 1), lambda i: (0, 0))], out_shape=[SDS((n, c), F32), SDS((1, 1), F32)],
        compiler_params=_cparams(("arbitrary",)))(y, t)


def _to_heads(x, nh):
    n = x.shape[0]
    return x.reshape(n, nh, x.shape[1] // nh).transpose(1, 0, 2)


def _from_heads(x):
    h, n, d = x.shape
    return x.transpose(1, 0, 2).reshape(n, h * d)


def _head_norm(x, g, denom, name):
    h, n, d = x.shape
    return rms_norm(x.reshape(h * n, d), g, denom, name).reshape(h, n, d)


def _pad_in_cols(w):
    z = lambda k: jnp.zeros(w.shape[:-1] + (k,), w.dtype)
    return jnp.concatenate([w[..., :1544], z(120), w[..., 1544:1960], z(96), w[..., 1960:], z(128)], axis=-1)


def _pad_q_up(w):
    s = w.shape[:-1]
    w = w.reshape(s + (HEADS, MLA_QK))
    w = jnp.concatenate([w, jnp.zeros(s + (HEADS, LANES - MLA_QK), w.dtype)], axis=-1)
    return w.reshape(s + (HEADS * LANES,))


def _assemble(shards):
    full = {k: jnp.concatenate([v[i] for i in range(N_CHIPS)], axis=SHARD_AXIS[k]) for k, v in shards.items()}
    out = dict(full)
    if "w_in" in out:
        out["w_in"] = _pad_in_cols(out["w_in"])
    if "mla_w_q_up" in out:
        out["mla_w_q_up"] = _pad_q_up(out["mla_w_q_up"])
    return out


def _rope_tables(n):
    half = MLA_ROPE // 2
    freqs = ROPE_THETA ** (-jnp.arange(half, dtype=F32) / half)
    ang = jnp.arange(n).astype(F32)[:, None] * freqs[None, :]
    cos, sin = jnp.cos(ang), jnp.sin(ang)
    one, zero = jnp.ones((n, MLA_NOPE), F32), jnp.zeros((n, MLA_NOPE), F32)
    tail1, tail0 = jnp.ones((n, LANES - MLA_QK), F32), jnp.zeros((n, LANES - MLA_QK), F32)
    return (jnp.concatenate([one, cos, cos, tail1], axis=1), jnp.concatenate([zero, sin, sin, tail0], axis=1))


def _pad_lanes(g, width):
    return jnp.concatenate([g, jnp.zeros((width - g.shape[0],), g.dtype)]).reshape(1, width)


PROJ_SEGMENTS = ((O_FQ, 512), (O_FK, 512), (O_FV, 512), (O_FF, HEADS), (O_CQ, MLA_Q_RANK), (O_CKV, MLA_KV_RANK),
                 (O_KR, MLA_ROPE), (O_SQ, 512), (O_SK, 128), (O_SV, 128), (O_G, D_MODEL), (O_G + D_MODEL, D_MODEL),
                 (O_G + 2 * D_MODEL, D_MODEL))


def _split_proj(proj):
    def fwd(x):
        return tuple(x[:, s:s + w] for s, w in PROJ_SEGMENTS), None

    def bwd(_, cts):
        rows, parts, pos = cts[0].shape[0], [], 0
        for (s, w), ct in zip(PROJ_SEGMENTS, cts):
            if s > pos:
                parts.append(jnp.zeros((rows, s - pos), F32))
            parts.append(ct)
            pos = s + w
        parts.append(jnp.zeros((rows, IN_PAD - pos), F32))
        return (jnp.concatenate(parts, axis=1),)

    return _op(fwd, bwd)(proj)


def _trunk(eps, fine, small, x, wb):
    seq = x.shape[0]
    n = -(-(N_META + seq) // ROW_PAD) * ROW_PAD
    ew = _assemble(eps)
    cos, sin = _rope_tables(n)
    slopes = jnp.exp2(-8.0 * jnp.arange(1, HEADS + 1, dtype=F32) / HEADS)
    h = jnp.concatenate([fine["meta_tokens"], x, jnp.zeros((n - N_META - seq, D_MODEL), F32)], axis=0)
    for l in range(DEPTH):
        p = f"l{l}_"
        row = lambda name: small[name][l].reshape(1, -1)
        xn = rms_norm(h, row("norm1_g"), D_MODEL, p + "norm1")
        proj = linear(xn, wb["w_in"][l], ew["w_in"][l], p + "win")
        p_fq, p_fk, p_fv, p_ff, p_cq, p_ckv, p_kr, p_sq, p_sk, p_sv, g0, g1, g2 = _split_proj(proj)
        fq = _head_norm(_to_heads(p_fq, HEADS), row("fox_q_g"), HEAD_DIM, p + "fqn")
        fk = _head_norm(_to_heads(p_fk, HEADS), row("fox_k_g"), HEAD_DIM, p + "fkn")
        fv = _to_heads(p_fv, HEADS)
        c = forget_cumsum(p_ff.T, small["fox_forget_b"][l].reshape(HEADS, 1), p + "fgate")
        out_a = causal_attention(fq, fk, fv, c, HEAD_DIM ** -0.5, p + "fox")
        cqn = rms_norm(p_cq, row("mla_q_a_g"), MLA_Q_RANK, p + "cqn")
        q = _to_heads(linear(cqn, wb["mla_w_q_up"][l], ew["mla_w_q_up"][l], p + "qup"), HEADS)
        q = rope(_head_norm(q, _pad_lanes(small["mla_q_g"][l], LANES), MLA_QK, p + "mqn"), cos, sin, p + "qrope")
        ckvn = rms_norm(p_ckv, row("mla_kv_a_g"), MLA_KV_RANK, p + "ckvn")
        kv = _to_heads(linear(ckvn, wb["mla_w_kv_up"][l], ew["mla_w_kv_up"][l], p + "kvup"), HEADS)
        kr = jnp.broadcast_to(p_kr[None], (HEADS, n, MLA_ROPE))
        k = jnp.concatenate([kv[..., :MLA_NOPE], kr, jnp.zeros((HEADS, n, LANES - MLA_QK), F32)], axis=-1)
        k = rope(_head_norm(k, _pad_lanes(small["mla_k_g"][l], LANES), MLA_QK, p + "mkn"), cos, sin, p + "krope")
        out_b = causal_attention(q, k, kv[..., MLA_NOPE:], None, MLA_QK ** -0.5, p + "mla")
        sq = _head_norm(_to_heads(p_sq, HEADS), row("swa_q_g"), HEAD_DIM, p + "sqn")
        sk = _head_norm(_to_heads(p_sk, SWA_KV_HEADS), row("swa_k_g"), HEAD_DIM, p + "skn")
        sv = _to_heads(p_sv, SWA_KV_HEADS)
        out_c = window_attention(sq, sk, sv, small["swa_sinks"][l], slopes, HEAD_DIM ** -0.5, p + "swa")
        ys = [linear(_from_heads(o), wb["w_branch"][l, i], ew["w_branch"][l, i], p + f"br{i}")
              for i, o in enumerate((out_a, out_b, out_c))]
        merged = gated_merge([g0, g1, g2], ys, p + "merge")
        h = linear(merged, wb["w_o"][l], ew["w_o"][l], p + "wo", res=h)
        xn2 = rms_norm(h, row("norm2_g"), D_MODEL, p + "norm2")
        u = linear(xn2, wb["ffn_w_up"][l], ew["ffn_w_up"][l], p + "wup")
        act = conv_glu(u, fine["ffn_conv_w"][l], row("ffn_conv_b"), p + "conv")
        h = linear(act, wb["ffn_w_down"][l], ew["ffn_w_down"][l], p + "wdown", res=h)
    return h


def _local_step(x, target, wb, fine, small):
    seq = x.shape[0]
    eps = {k: jnp.zeros((N_CHIPS,) + SHARD_SHAPE[k], F32) for k in BIG}
    y, vjp = jax.vjp(lambda e, f, s, xx: _trunk(e, f, s, xx, wb), eps, fine, small, x)
    n = y.shape[0]
    tpad = jnp.concatenate([jnp.zeros((N_META, D_MODEL), F32), target, jnp.zeros((n - N_META - seq, D_MODEL), F32)])
    dy, loss = _loss_call(y, tpad, seq, "loss")
    g_eps, g_fine, g_small, g_x = vjp(dy)
    return loss[0, 0], g_x, g_eps, g_fine, g_small


def _pack_rows(shapes, mult):
    total = sum(_size(s) for s in shapes)
    rows = -(-total // LANES)
    return -(-rows // mult) * mult


def _size(shape):
    n = 1
    for d in shape:
        n *= d
    return n


def _pack(arrs, rows, dtype):
    flat = [a.reshape(-1).astype(dtype) for a in arrs]
    used = sum(a.size for a in flat)
    flat.append(jnp.zeros((rows * LANES - used,), dtype))
    return jnp.concatenate(flat).reshape(rows, LANES)


def _unpack(p, shapes):
    flat = p.reshape(-1)
    out, off = [], 0
    for s in shapes:
        out.append(flat[off:off + _size(s)].reshape(s))
        off += _size(s)
    return out


MESH = pl.DeviceIdType.MESH
ANY = pl.BlockSpec(memory_space=pl.ANY)


def _me():
    return lax.axis_index("x"), lax.axis_index("y"), lax.axis_index("c")


def _remote(src, dst, send_sems, recv_sems, idx, dev):
    return pltpu.make_async_remote_copy(src_ref=src, dst_ref=dst, send_sem=send_sems.at[idx], recv_sem=recv_sems.at[idx],
                                        device_id=dev, device_id_type=MESH)


def _all_gather(arrs, name):
    npk = len(arrs)

    def body(*refs):
        ins, outs = refs[:npk], refs[npk:2 * npk]
        send_sems, recv_sems = refs[2 * npk:]
        x, y, c = _me()
        j = 2 * x + y
        sibling = (x, y, 1 - c)
        chips = [(1 - x, y), (x, 1 - y), (1 - x, 1 - y)]
        sends = []
        for p in range(npk):
            for r, (cx, cy) in enumerate(chips):
                cp = _remote(ins[p].at[c], outs[p].at[j, c], send_sems, recv_sems, 6 * p + r, (cx, cy, c))
                cp.start()
                sends.append(cp)
        for p in range(npk):
            for r, (cx, cy) in enumerate(chips):
                blk = outs[p].at[2 * cx + cy, c]
                _remote(blk, blk, send_sems, recv_sems, 6 * p + r, sibling).wait_recv()
                fw = _remote(blk, blk, send_sems, recv_sems, 6 * p + 3 + r, sibling)
                fw.start()
                sends.append(fw)
        for p in range(npk):
            for r, (cx, cy) in enumerate(chips):
                blk = outs[p].at[2 * cx + cy, 1 - c]
                _remote(blk, blk, send_sems, recv_sems, 6 * p + 3 + r, sibling).wait_recv()
        for cp in sends:
            cp.wait_send()

    outs = pl.pallas_call(
        body, name=name, in_specs=[ANY] * npk, out_specs=[ANY] * npk,
        out_shape=[SDS((N_CHIPS,) + a.shape, a.dtype) for a in arrs],
        scratch_shapes=[pltpu.SemaphoreType.DMA((6 * npk,)), pltpu.SemaphoreType.DMA((6 * npk,))],
        compiler_params=pltpu.CompilerParams(has_side_effects=True))(*arrs)
    j = 2 * lax.axis_index("x") + lax.axis_index("y")
    return [lax.dynamic_update_index_in_dim(o, a, j, 0) for o, a in zip(outs, arrs)]


def _pair_exchange(gs, name):
    npk = len(gs)

    def body(*refs):
        ins, outs = refs[:npk], refs[npk:2 * npk]
        send_sems, recv_sems = refs[2 * npk:]
        x, y, c = _me()
        cps = [_remote(ins[p].at[:, 1 - c], outs[p], send_sems, recv_sems, p, (x, y, 1 - c)) for p in range(npk)]
        for cp in cps:
            cp.start()
        for cp in cps:
            cp.wait()

    return pl.pallas_call(
        body, name=name, in_specs=[ANY] * npk, out_specs=[ANY] * npk,
        out_shape=[SDS((N_CHIPS,) + g.shape[2:], g.dtype) for g in gs],
        scratch_shapes=[pltpu.SemaphoreType.DMA((npk,)), pltpu.SemaphoreType.DMA((npk,))],
        compiler_params=pltpu.CompilerParams(has_side_effects=True))(*gs)


def _chip_exchange(ss, small, name):
    npk = len(ss)

    def body(*refs):
        ins, sm_ref = refs[:npk], refs[npk]
        outs, sa_ref = refs[npk + 1:2 * npk + 1], refs[2 * npk + 1]
        send_sems, recv_sems, loc_sem = refs[2 * npk + 2:]
        x, y, c = _me()
        me = 4 * x + 2 * y + c
        lc = pltpu.make_async_copy(sm_ref, sa_ref.at[me], loc_sem.at[0])
        lc.start()
        cps = []
        for p in range(npk):
            for r, (cx, cy) in enumerate([(1 - x, y), (x, 1 - y), (1 - x, 1 - y)]):
                cp = _remote(ins[p].at[2 * cx + cy], outs[p].at[r], send_sems, recv_sems, 3 * p + r, (cx, cy, c))
                cp.start()
                cps.append(cp)
        base = 3 * npk - 1
        for mask in range(1, N_DEV):
            px, py, pc = x ^ (mask >> 2), y ^ ((mask >> 1) & 1), c ^ (mask & 1)
            cp = _remote(sm_ref, sa_ref.at[me], send_sems, recv_sems, base + mask, (px, py, pc))
            cp.start()
            cps.append(cp)
        for p in range(npk):
            for r in range(3):
                _remote(outs[p].at[r], outs[p].at[r], send_sems, recv_sems, 3 * p + r, (x, y, c)).wait_recv()
        for mask in range(1, N_DEV):
            src = 4 * (x ^ (mask >> 2)) + 2 * (y ^ ((mask >> 1) & 1)) + (c ^ (mask & 1))
            _remote(sa_ref.at[src], sa_ref.at[src], send_sems, recv_sems, base + mask, (x, y, c)).wait_recv()
        for cp in cps:
            cp.wait_send()
        lc.wait()

    nsem = 3 * npk + N_DEV - 1
    res = pl.pallas_call(
        body, name=name, in_specs=[ANY] * (npk + 1), out_specs=[ANY] * (npk + 1),
        out_shape=[SDS((3,) + s.shape[1:], s.dtype) for s in ss] + [SDS((N_DEV,) + small.shape, small.dtype)],
        scratch_shapes=[pltpu.SemaphoreType.DMA((nsem,)), pltpu.SemaphoreType.DMA((nsem,)),
                        pltpu.SemaphoreType.DMA((1,))],
        compiler_params=pltpu.CompilerParams(has_side_effects=True))(*ss, small)
    return res[:npk], res[npk]


def _half_exchange(ghs, name):
    npk = len(ghs)

    def body(*refs):
        ins, outs = refs[:npk], refs[npk:2 * npk]
        send_sems, recv_sems = refs[2 * npk:]
        x, y, c = _me()
        cps = [_remote(ins[p], outs[p], send_sems, recv_sems, p, (x, y, 1 - c)) for p in range(npk)]
        for cp in cps:
            cp.start()
        for cp in cps:
            cp.wait()

    return pl.pallas_call(
        body, name=name, in_specs=[ANY] * npk, out_specs=[ANY] * npk, out_shape=[SDS(g.shape, g.dtype) for g in ghs],
        scratch_shapes=[pltpu.SemaphoreType.DMA((npk,)), pltpu.SemaphoreType.DMA((npk,))],
        compiler_params=pltpu.CompilerParams(has_side_effects=True))(*ghs)


def _add_tile(rows, cols):
    return _div_tile(rows, max(16, (1 << 19) // max(cols, LANES)), 16)


def _pair_add(g, r1, c_idx, name):
    _, rows, cols = r1.shape
    tr = _add_tile(rows, cols)

    def body(c_ref, g_ref, r_ref, o_ref, ob_ref):
        s = g_ref[0] + r_ref[...]
        o_ref[...] = s
        ob_ref[...] = s.astype(BF16)

    own = BS((1, tr, cols), lambda k, i, c: (k, i, 0))
    return pl.pallas_call(
        body, name=name,
        grid_spec=pltpu.PrefetchScalarGridSpec(
            num_scalar_prefetch=1, grid=(N_CHIPS, rows // tr),
            in_specs=[BS((1, 1, tr, cols), lambda k, i, c: (k, c[0], i, 0)), own], out_specs=[own, own]),
        out_shape=[SDS(r1.shape, F32), SDS(r1.shape, BF16)],
        compiler_params=_cparams(("parallel", "parallel")))(c_idx, g, r1)


def _chip_add(s1, r2, j_idx, name):
    _, rows, cols = s1.shape
    tr = _add_tile(rows, cols)

    def body(j_ref, s_ref, r_ref, o_ref):
        o_ref[...] = ((s_ref[0] + r_ref[0].astype(F32)) + r_ref[1].astype(F32)) + r_ref[2].astype(F32)

    return pl.pallas_call(
        body, name=name,
        grid_spec=pltpu.PrefetchScalarGridSpec(
            num_scalar_prefetch=1, grid=(rows // tr,),
            in_specs=[BS((1, tr, cols), lambda i, j: (j[0], i, 0)), BS((3, tr, cols), lambda i, j: (0, i, 0))],
            out_specs=BS((tr, cols), lambda i, j: (i, 0))),
        out_shape=SDS((rows, cols), F32), compiler_params=_cparams(("parallel",)))(j_idx, s1, r2)


def _adamw_math(w, g, m, v):
    m = ADAM_B1 * m + (1.0 - ADAM_B1) * g
    v = ADAM_B2 * v + (1.0 - ADAM_B2) * (g * g)
    m_hat = m / (1.0 - ADAM_B1 ** ADAM_STEP)
    v_hat = v / (1.0 - ADAM_B2 ** ADAM_STEP)
    delta = -ADAM_LR * (m_hat / (jnp.sqrt(v_hat) + ADAM_EPS) + ADAM_WD * w)
    return delta, m, v


def _adamw(w, gh, go, m, v, c_idx, name):
    _, rows, cols = w.shape
    tr = _add_tile(rows, cols)

    def body(c_ref, w_ref, gh_ref, go_ref, m_ref, v_ref, g_out, d_out, m_out, v_out):
        g = jnp.where(pl.program_id(0) == c_ref[0], gh_ref[...], go_ref[...])
        g_out[0] = g
        d_out[0], m_out[0], v_out[0] = _adamw_math(w_ref[0], g, m_ref[0], v_ref[0])

    full = BS((1, tr, cols), lambda hf, i, c: (hf, i, 0))
    half = BS((tr, cols), lambda hf, i, c: (i, 0))
    return pl.pallas_call(
        body, name=name,
        grid_spec=pltpu.PrefetchScalarGridSpec(
            num_scalar_prefetch=1, grid=(2, rows // tr), in_specs=[full, half, half, full, full],
            out_specs=[full] * 4),
        out_shape=[SDS(w.shape, F32)] * 4, compiler_params=_cparams(("parallel", "parallel")))(c_idx, w, gh, go, m, v)


def _sum_devices(sa, name):
    def body(sa_ref, g_out):
        g = sa_ref[0]
        for d in range(1, N_DEV):
            g = g + sa_ref[d]
        g_out[...] = g

    return pl.pallas_call(body, name=name, out_shape=SDS(sa.shape[1:], F32),
                          compiler_params=pltpu.CompilerParams(vmem_limit_bytes=VMEM_LIMIT))(sa)


def _adamw_small(ws, gs, ms, vs, name):
    k = len(ws)

    def body(*refs):
        ins, outs = refs[:4 * k], refs[4 * k:]
        for i in range(k):
            d, m, v = _adamw_math(ins[i][...], ins[k + i][...], ins[2 * k + i][...], ins[3 * k + i][...])
            outs[i][...], outs[k + i][...], outs[2 * k + i][...] = d, m, v

    return pl.pallas_call(body, name=name, out_shape=[SDS(w.shape, F32) for w in ws] * 3,
                          compiler_params=pltpu.CompilerParams(vmem_limit_bytes=VMEM_LIMIT))(*ws, *gs, *ms, *vs)


HALVED = {"w_in": (2, 1024, 1450), "mla_w_q_up": (2, 256, 192), "mla_w_kv_up": (2, 128, 256),
          "w_branch": (2, 1536, 256), "w_o": (2, 256, 1024), "ffn_w_up": (2, 1024, 1408),
          "ffn_w_down": (2, 704, 1024), "ffn_conv_w": (2, 3, 1408), "meta_tokens": (2, 8, 256)}
SMALL_SHAPE = {"norm1_g": (2, 1024), "fox_forget_b": (2, 8), "fox_q_g": (2, 64), "fox_k_g": (2, 64),
               "mla_q_a_g": (2, 256), "mla_kv_a_g": (2, 128), "mla_q_g": (2, 96), "mla_k_g": (2, 96),
               "swa_q_g": (2, 64), "swa_k_g": (2, 64), "swa_sinks": (2, 8), "norm2_g": (2, 1024),
               "ffn_conv_b": (2, 5632)}
SMALL_ROWS = _pack_rows([SMALL_SHAPE[k] for k in SMALL] + [(1,)], SUBLANES)


def kernel(x, meta_tokens, norm1_g, w_in, fox_forget_b, fox_q_g, fox_k_g, mla_q_a_g, mla_w_q_up, mla_kv_a_g, mla_w_kv_up, mla_q_g, mla_k_g, swa_q_g, swa_k_g, swa_sinks, w_branch, w_o, norm2_g, ffn_w_up, ffn_conv_w, ffn_conv_b, ffn_w_down, loss_target, m_meta_tokens, m_norm1_g, m_w_in, m_fox_forget_b, m_fox_q_g, m_fox_k_g, m_mla_q_a_g, m_mla_w_q_up, m_mla_kv_a_g, m_mla_w_kv_up, m_mla_q_g, m_mla_k_g, m_swa_q_g, m_swa_k_g, m_swa_sinks, m_w_branch, m_w_o, m_norm2_g, m_ffn_w_up, m_ffn_conv_w, m_ffn_conv_b, m_ffn_w_down, v_meta_tokens, v_norm1_g, v_w_in, v_fox_forget_b, v_fox_q_g, v_fox_k_g, v_mla_q_a_g, v_mla_w_q_up, v_mla_kv_a_g, v_mla_w_kv_up, v_mla_q_g, v_mla_k_g, v_swa_q_g, v_swa_k_g, v_swa_sinks, v_w_branch, v_w_o, v_norm2_g, v_ffn_w_up, v_ffn_conv_w, v_ffn_conv_b, v_ffn_w_down):
    w = dict(meta_tokens=meta_tokens, norm1_g=norm1_g, w_in=w_in, fox_forget_b=fox_forget_b, fox_q_g=fox_q_g,
             fox_k_g=fox_k_g, mla_q_a_g=mla_q_a_g, mla_w_q_up=mla_w_q_up, mla_kv_a_g=mla_kv_a_g,
             mla_w_kv_up=mla_w_kv_up, mla_q_g=mla_q_g, mla_k_g=mla_k_g, swa_q_g=swa_q_g, swa_k_g=swa_k_g,
             swa_sinks=swa_sinks, w_branch=w_branch, w_o=w_o, norm2_g=norm2_g, ffn_w_up=ffn_w_up,
             ffn_conv_w=ffn_conv_w, ffn_conv_b=ffn_conv_b, ffn_w_down=ffn_w_down)
    m = dict(meta_tokens=m_meta_tokens, norm1_g=m_norm1_g, w_in=m_w_in, fox_forget_b=m_fox_forget_b,
             fox_q_g=m_fox_q_g, fox_k_g=m_fox_k_g, mla_q_a_g=m_mla_q_a_g, mla_w_q_up=m_mla_w_q_up,
             mla_kv_a_g=m_mla_kv_a_g, mla_w_kv_up=m_mla_w_kv_up, mla_q_g=m_mla_q_g, mla_k_g=m_mla_k_g,
             swa_q_g=m_swa_q_g, swa_k_g=m_swa_k_g, swa_sinks=m_swa_sinks, w_branch=m_w_branch, w_o=m_w_o,
             norm2_g=m_norm2_g, ffn_w_up=m_ffn_w_up, ffn_conv_w=m_ffn_conv_w, ffn_conv_b=m_ffn_conv_b,
             ffn_w_down=m_ffn_w_down)
    v = dict(meta_tokens=v_meta_tokens, norm1_g=v_norm1_g, w_in=v_w_in, fox_forget_b=v_fox_forget_b,
             fox_q_g=v_fox_q_g, fox_k_g=v_fox_k_g, mla_q_a_g=v_mla_q_a_g, mla_w_q_up=v_mla_w_q_up,
             mla_kv_a_g=v_mla_kv_a_g, mla_w_kv_up=v_mla_w_kv_up, mla_q_g=v_mla_q_g, mla_k_g=v_mla_k_g,
             swa_q_g=v_swa_q_g, swa_k_g=v_swa_k_g, swa_sinks=v_swa_sinks, w_branch=v_w_branch, w_o=v_w_o,
             norm2_g=v_norm2_g, ffn_w_up=v_ffn_w_up, ffn_conv_w=v_ffn_conv_w, ffn_conv_b=v_ffn_conv_b,
             ffn_w_down=v_ffn_w_down)
    xi, yi, ci = _me()
    c_idx = ci.astype(jnp.int32).reshape(1)
    j_idx = (2 * xi + yi).astype(jnp.int32).reshape(1)

    sh_names = BIG + FINE
    gathered = _all_gather([w[k].astype(BF16).reshape(HALVED[k]) for k in BIG]
                           + [w[k].reshape(HALVED[k]) for k in FINE], "gather_weights")
    shards = {k: g.reshape((N_CHIPS,) + SHARD_SHAPE[k]) for k, g in zip(sh_names, gathered)}
    wb = _assemble({k: shards[k] for k in BIG})
    fine = _assemble({k: shards[k] for k in FINE})
    small = {k: w[k] for k in SMALL}

    loss, g_x, g_eps, g_fine, g_small = _local_step(x[0], loss_target[0], wb, fine, small)

    g_sh = dict(g_eps)
    g_sh.update({k: jnp.stack(jnp.split(g_fine[k], N_CHIPS, axis=SHARD_AXIS[k])) for k in FINE})
    gs = [g_sh[k].reshape((N_CHIPS,) + HALVED[k]) for k in sh_names]
    spack = _pack([g_small[k] for k in SMALL] + [loss.reshape(1)], SMALL_ROWS, F32)
    r1 = _pair_exchange(gs, "grads_pair_exchange")
    s1 = [_pair_add(g, r, c_idx, "grads_pair_add_" + k) for g, r, k in zip(gs, r1, sh_names)]
    r2, sa = _chip_exchange([s[1] for s in s1], spack, "grads_chip_exchange")
    gh = [_chip_add(s[0], r, j_idx, "grads_chip_add_" + k) for s, r, k in zip(s1, r2, sh_names)]
    go = _half_exchange(gh, "grads_half_exchange")

    grads, deltas, new_m, new_v = {}, {}, {}, {}
    for k, a, b in zip(sh_names, gh, go):
        outs = _adamw(w[k].reshape(HALVED[k]), a, b, m[k].reshape(HALVED[k]), v[k].reshape(HALVED[k]), c_idx,
                      "adamw_" + k)
        for dst, o in zip((grads, deltas, new_m, new_v), outs):
            dst[k] = o.reshape(SHARD_SHAPE[k])
    sm_shapes = [SMALL_SHAPE[k] for k in SMALL] + [(1,)]
    g_sum = _unpack(_sum_devices(sa, "sum_small"), sm_shapes)
    res = _adamw_small([w[k] for k in SMALL], g_sum[:-1], [m[k] for k in SMALL], [v[k] for k in SMALL], "adamw_small")
    ns = len(SMALL)
    grads.update(zip(SMALL, g_sum[:-1]))
    for dst, vals in zip((deltas, new_m, new_v), (res[:ns], res[ns:2 * ns], res[2 * ns:])):
        dst.update(zip(SMALL, vals))
    total_loss = g_sum[-1][0]
    return (total_loss, g_x[None], *[grads[k] for k in WEIGHTS], *[deltas[k] for k in WEIGHTS],
            *[new_m[k] for k in WEIGHTS], *[new_v[k] for k in WEIGHTS])
```

```python
import functools

import jax
import jax.numpy as jnp
from jax import lax
from jax.experimental import pallas as pl
from jax.experimental.pallas import tpu as pltpu

F32 = jnp.float32
BF16 = jnp.bfloat16
SDS = jax.ShapeDtypeStruct
BS = pl.BlockSpec

D_MODEL = 1024
DEPTH = 2
N_META = 16
EPS = 1e-6
HEADS = 8
HEAD_DIM = 64
MLA_Q_RANK = 256
MLA_KV_RANK = 128
MLA_NOPE = 64
MLA_ROPE = 32
MLA_QK = MLA_NOPE + MLA_ROPE
ROPE_THETA = 10000.0
SWA_KV_HEADS = 2
WINDOW = 128
N_BRANCH = 3
BRANCH_WIDTH = 512
D_FF = 2816
IN_WIDTH = 5800
IN_PAD = 6144
N_CHIPS = 4
N_DEV = 8

ADAM_LR = 0.001
ADAM_B1 = 0.9
ADAM_B2 = 0.999
ADAM_EPS = 1e-08
ADAM_WD = 0.01
ADAM_STEP = 10

LANES = 128
SUBLANES = 8
ROW_PAD = 128
CAUSAL_TILE = 384
NEG = -1e30
VMEM_LIMIT = 56 * 1024 * 1024

O_FQ, O_FK, O_FV, O_FF = 0, 512, 1024, 1536
O_CQ, O_CKV, O_KR = 1664, 1920, 2048
O_SQ, O_SK, O_SV, O_G = 2176, 2688, 2816, 2944

SHARDED = ("meta_tokens", "w_in", "mla_w_q_up", "mla_w_kv_up", "w_branch", "w_o", "ffn_w_up", "ffn_conv_w",
           "ffn_w_down")
SHARD_AXIS = {"meta_tokens": 1, "w_in": 2, "mla_w_q_up": 2, "mla_w_kv_up": 2, "w_branch": 3, "w_o": 1,
              "ffn_w_up": 2, "ffn_conv_w": 2, "ffn_w_down": 1}
SHARD_SHAPE = {"meta_tokens": (16, 256), "w_in": (2, 1024, 1450), "mla_w_q_up": (2, 256, 192),
               "mla_w_kv_up": (2, 128, 256), "w_branch": (2, 3, 512, 256), "w_o": (2, 256, 1024),
               "ffn_w_up": (2, 1024, 1408), "ffn_conv_w": (2, 3, 1408), "ffn_w_down": (2, 704, 1024)}
BIG = ("w_in", "mla_w_q_up", "mla_w_kv_up", "w_branch", "w_o", "ffn_w_up", "ffn_w_down")
FINE = ("meta_tokens", "ffn_conv_w")
SMALL = ("norm1_g", "fox_forget_b", "fox_q_g", "fox_k_g", "mla_q_a_g", "mla_kv_a_g", "mla_q_g", "mla_k_g",
         "swa_q_g", "swa_k_g", "swa_sinks", "norm2_g", "ffn_conv_b")
WEIGHTS = ("meta_tokens", "norm1_g", "w_in", "fox_forget_b", "fox_q_g", "fox_k_g", "mla_q_a_g", "mla_w_q_up",
           "mla_kv_a_g", "mla_w_kv_up", "mla_q_g", "mla_k_g", "swa_q_g", "swa_k_g", "swa_sinks", "w_branch", "w_o",
           "norm2_g", "ffn_w_up", "ffn_conv_w", "ffn_conv_b", "ffn_w_down")


def _cparams(sem):
    return pltpu.CompilerParams(dimension_semantics=sem, vmem_limit_bytes=VMEM_LIMIT)


def _div_tile(n, cap, mult=SUBLANES):
    best = None
    for t in range(mult, min(n, cap) + 1, mult):
        if n % t == 0:
            best = t
    return best if best is not None else n


def _rows_tile(n, width, budget=2 << 20):
    return _div_tile(n, max(SUBLANES, budget // (4 * max(width, LANES))))


def _op(fwd, bwd):
    @jax.custom_vjp
    def op(*args):
        return fwd(*args)[0]
    op.defvjp(fwd, bwd)
    return op


def _rotate(v, cos, sin):
    lane = lax.broadcasted_iota(jnp.int32, v.shape, 1)
    rot = jnp.where(lane < MLA_NOPE + MLA_ROPE // 2, -pltpu.roll(v, LANES - MLA_ROPE // 2, 1),
                    pltpu.roll(v, MLA_ROPE // 2, 1))
    return v * cos + rot * sin


def _rms_fwd_call(x, g, denom, name, rot=None):
    n, c = x.shape
    tr = _rows_tile(n if rot is None else rot[0].shape[0], c)
    nt = None if rot is None else rot[0].shape[0] // tr

    def body(x_ref, g_ref, *rest):
        y_ref = rest[-1]
        xv = x_ref[...]
        ms = jnp.sum(xv * xv, axis=-1, keepdims=True) * (1.0 / denom)
        y = xv * lax.rsqrt(ms + EPS) * g_ref[...]
        y_ref[...] = y if rot is None else _rotate(y, rest[0][...], rest[1][...])

    ins, args = [BS((tr, c), lambda i: (i, 0)), BS((1, c), lambda i: (0, 0))], [x, g]
    if rot is not None:
        ins += [BS((tr, c), lambda i: (i % nt, 0))] * 2
        args += list(rot)
    return pl.pallas_call(
        body, name=name, grid=(n // tr,), in_specs=ins,
        out_specs=BS((tr, c), lambda i: (i, 0)), out_shape=SDS((n, c), F32),
        compiler_params=_cparams(("parallel",)))(*args)


def _rms_bwd_call(x, g, dy, denom, name, rot=None):
    n, c = x.shape
    tr = _rows_tile(n if rot is None else rot[0].shape[0], c)
    nt = None if rot is None else rot[0].shape[0] // tr

    def body(x_ref, g_ref, dy_ref, *rest):
        dx_ref, dg_ref = rest[-2:]
        xv = x_ref[...]
        dy = dy_ref[...]
        if rot is not None:
            dy = _rotate(dy, rest[0][...], -rest[1][...])
        ms = jnp.sum(xv * xv, axis=-1, keepdims=True) * (1.0 / denom)
        r = lax.rsqrt(ms + EPS)
        xh = xv * r
        dxh = dy * g_ref[...]
        dx_ref[...] = r * (dxh - xh * (jnp.sum(dxh * xh, axis=-1, keepdims=True) * (1.0 / denom)))

        @pl.when(pl.program_id(0) == 0)
        def _():
            dg_ref[...] = jnp.zeros_like(dg_ref)

        dg_ref[...] += jnp.sum(dy * xh, axis=0, keepdims=True)

    ins = [BS((tr, c), lambda i: (i, 0)), BS((1, c), lambda i: (0, 0)), BS((tr, c), lambda i: (i, 0))]
    args = [x, g, dy]
    if rot is not None:
        ins += [BS((tr, c), lambda i: (i % nt, 0))] * 2
        args += list(rot)
    return pl.pallas_call(
        body, name=name, grid=(n // tr,), in_specs=ins,
        out_specs=[BS((tr, c), lambda i: (i, 0)), BS((1, c), lambda i: (0, 0))],
        out_shape=[SDS((n, c), F32), SDS((1, c), F32)],
        compiler_params=_cparams(("arbitrary",)))(*args)


def rms_norm(x, g, denom, name):
    def fwd(x, g):
        return _rms_fwd_call(x, g, denom, name + "_f"), (x, g)

    def bwd(res, dy):
        return tuple(_rms_bwd_call(res[0], res[1], dy, denom, name + "_b"))

    return _op(fwd, bwd)(x, g)


def rms_norm_rope(x, g, cos, sin, denom, name):
    def fwd(x, g, cos, sin):
        return _rms_fwd_call(x, g, denom, name + "_f", (cos, sin)), (x, g, cos, sin)

    def bwd(res, dy):
        x, g, cos, sin = res
        dx, dg = _rms_bwd_call(x, g, dy, denom, name + "_b", (cos, sin))
        return dx, dg, jnp.zeros_like(cos), jnp.zeros_like(sin)

    return _op(fwd, bwd)(x, g, cos, sin)


def _mm_call(a, b, mode, res, name):
    if mode == "nn":
        (m, kc), n = a.shape, b.shape[1]
    elif mode == "nt":
        (m, kc), n = a.shape, b.shape[0]
    else:
        (kc, m), n = a.shape, b.shape[1]
    if mode == "tn":
        tk = _div_tile(kc, 528)
        tm = _div_tile(m, 1408, LANES)
        tn = _div_tile(n, 2048, LANES)
    else:
        tk = kc if kc <= 2816 else _div_tile(kc, 1024, LANES)
        tm = _div_tile(m, max(LANES, (9 << 19) // (4 * tk)))
        tn = _div_tile(n, 1408 if mode == "nt" else 512, LANES)
    nk = kc // tk
    dims = {"nn": (((1,), (0,)), ((), ())), "nt": (((1,), (1,)), ((), ())), "tn": (((0,), (0,)), ((), ()))}[mode]

    def body(*refs):
        if res is None:
            a_ref, b_ref, o_ref, acc_ref = refs
            r_ref = None
        else:
            a_ref, b_ref, r_ref, o_ref, acc_ref = refs
        k = pl.program_id(2)

        @pl.when(k == 0)
        def _():
            acc_ref[...] = jnp.zeros_like(acc_ref)

        acc_ref[...] += lax.dot_general(a_ref[...].astype(BF16), b_ref[...].astype(BF16), dims,
                                        preferred_element_type=F32)

        @pl.when(k == nk - 1)
        def _():
            if r_ref is None:
                o_ref[...] = acc_ref[...]
            else:
                o_ref[...] = r_ref[...] + acc_ref[...]

    a_spec = BS((tk, tm), lambda i, j, k: (k, i)) if mode == "tn" else BS((tm, tk), lambda i, j, k: (i, k))
    b_spec = BS((tn, tk), lambda i, j, k: (j, k)) if mode == "nt" else BS((tk, tn), lambda i, j, k: (k, j))
    o_spec = BS((tm, tn), lambda i, j, k: (i, j))
    ins, args = [a_spec, b_spec], [a, b]
    if res is not None:
        ins.append(o_spec)
        args.append(res)
    return pl.pallas_call(
        body, name=name, grid=(m // tm, n // tn, nk), in_specs=ins, out_specs=o_spec,
        out_shape=SDS((m, n), F32), scratch_shapes=[pltpu.VMEM((tm, tn), F32)],
        compiler_params=_cparams(("parallel", "parallel", "arbitrary")))(*args)


def linear(a, w, eps, name, res=None):
    if res is None:
        def fwd(a, w, eps):
            return _mm_call(a, w, "nn", None, name + "_f"), (a, w)

        def bwd(r, dc):
            a, w = r
            return (_mm_call(dc, w, "nt", None, name + "_da"), jnp.zeros_like(w),
                    _mm_call(a, dc, "tn", None, name + "_dw"))

        return _op(fwd, bwd)(a, w, eps)

    def fwd_r(a, w, eps, res):
        return _mm_call(a, w, "nn", res, name + "_f"), (a, w)

    def bwd_r(r, dc):
        a, w = r
        return (_mm_call(dc, w, "nt", None, name + "_da"), jnp.zeros_like(w),
                _mm_call(a, dc, "tn", None, name + "_dw"), dc)

    return _op(fwd_r, bwd_r)(a, w, eps, res)


CT = 128


def _tri_dot(v, upper):
    r = lax.broadcasted_iota(jnp.int32, (CT, CT), 0)
    c = lax.broadcasted_iota(jnp.int32, (CT, CT), 1)
    tri = jnp.where((r <= c) if upper else (r >= c), 1.0, 0.0).astype(F32)
    return jnp.dot(v, tri, preferred_element_type=F32, precision=lax.Precision.HIGHEST)


def _gate_fwd_call(z, b, name):
    h, n = z.shape

    def body(z_ref, b_ref, c_ref, carry):
        @pl.when(pl.program_id(0) == 0)
        def _():
            carry[...] = jnp.zeros_like(carry)

        x = z_ref[...] + b_ref[...]
        ls = jnp.minimum(x, 0.0) - jnp.log(1.0 + jnp.exp(-jnp.abs(x)))
        c_ref[...] = _tri_dot(ls, True) + carry[...]
        carry[...] += jnp.sum(ls, axis=1, keepdims=True)

    return pl.pallas_call(
        body, name=name, grid=(n // CT,),
        in_specs=[BS((h, CT), lambda j: (0, j)), BS((h, 1), lambda j: (0, 0))],
        out_specs=BS((h, CT), lambda j: (0, j)), out_shape=SDS((h, n), F32),
        scratch_shapes=[pltpu.VMEM((h, 1), F32)],
        compiler_params=_cparams(("arbitrary",)))(z, b)


def _gate_bwd_call(z, b, dc, name):
    h, n = z.shape
    nt = n // CT

    def body(z_ref, b_ref, dc_ref, dz_ref, db_ref, carry):
        @pl.when(pl.program_id(0) == 0)
        def _():
            carry[...] = jnp.zeros_like(carry)
            db_ref[...] = jnp.zeros_like(db_ref)

        dcv = dc_ref[...]
        dls = _tri_dot(dcv, False) + carry[...]
        carry[...] += jnp.sum(dcv, axis=1, keepdims=True)
        x = z_ref[...] + b_ref[...]
        e = jnp.exp(-jnp.abs(x))
        dz = dls * jnp.where(x >= 0, e / (1.0 + e), 1.0 / (1.0 + e))
        dz_ref[...] = dz
        db_ref[...] += jnp.sum(dz, axis=1, keepdims=True)

    rev = lambda j: (0, nt - 1 - j)
    return pl.pallas_call(
        body, name=name, grid=(nt,),
        in_specs=[BS((h, CT), rev), BS((h, 1), lambda j: (0, 0)), BS((h, CT), rev)],
        out_specs=[BS((h, CT), rev), BS((h, 1), lambda j: (0, 0))],
        out_shape=[SDS((h, n), F32), SDS((h, 1), F32)],
        scratch_shapes=[pltpu.VMEM((h, 1), F32)],
        compiler_params=_cparams(("arbitrary",)))(z, b, dc)


def forget_cumsum(z, b, name):
    def fwd(z, b):
        return _gate_fwd_call(z, b, name + "_f"), (z, b)

    def bwd(res, dc):
        return tuple(_gate_bwd_call(res[0], res[1], dc, name + "_b"))

    return _op(fwd, bwd)(z, b)


NT_DIMS = (((1,), (1,)), ((), ()))
TN_DIMS = (((0,), (0,)), ((), ()))
HEADS_PER_STEP = 2


def _causal_tile(n):
    return CAUSAL_TILE if n % CAUSAL_TILE == 0 else ROW_PAD


def _causal_fwd_call(q, k, v, ck_r, fox, scale, name, late=None):
    h, n, dk = q.shape
    dv = v.shape[2]
    t = _causal_tile(n)
    nq = n // t
    hb = HEADS_PER_STEP
    nl = 0 if late is None else len(late)
    n_in = 3 + int(fox) + nl

    def body(*refs):
        q_ref, k_ref, v_ref = refs[:3]
        ck_ref = refs[3] if fox else None
        o_ref, lse_ref = refs[n_in:n_in + 2]
        m_scr, l_scr, acc_scr = refs[n_in + 2 + nl:n_in + 5 + nl]
        qi = pl.program_id(1)
        if nl:
            start, forward, drain, receive = _layer_gather(refs[n_in - nl:n_in], refs[n_in + 2:n_in + 2 + nl],
                                                           refs[-2], refs[-1], 1, 0)
            hp, core = pl.program_id(0), lax.axis_index("c")
            last = (hp == h // hb - 1) & (qi == nq - 1)
            pl.when((hp == 0) & (qi == 0) & (core == 1))(start)
            pl.when((hp == h // hb // 2) & (qi == 0) & (core == 1))(forward)
            pl.when(last & (core == 1))(drain)
            pl.when(last & (core == 0))(receive)
        qbs = [q_ref[e].astype(BF16) for e in range(hb)]
        m_scr[...] = jnp.full(m_scr.shape, NEG, F32)
        l_scr[...] = jnp.zeros_like(l_scr)
        acc_scr[...] = jnp.zeros_like(acc_scr)

        def process(j, masked):
            off = pl.multiple_of(j * t, t)
            if masked:
                rows = lax.broadcasted_iota(jnp.int32, (t, t), 0)
                cols = lax.broadcasted_iota(jnp.int32, (t, t), 1)
                valid = cols <= rows
            for e in range(hb):
                kb = k_ref[e, pl.ds(off, t), :].astype(BF16)
                vb = v_ref[e, pl.ds(off, t), :].astype(BF16)
                s = lax.dot_general(qbs[e], kb, NT_DIMS, preferred_element_type=F32) * scale
                if fox:
                    s = s - ck_ref[e, j]
                if masked:
                    s = jnp.where(valid, s, NEG)
                m_old = m_scr[e]
                m_new = jnp.maximum(m_old, jnp.max(s, axis=1, keepdims=True))
                alpha = jnp.exp(m_old - m_new)
                p = jnp.exp(s - jnp.tile(m_new, (1, t // LANES)))
                l_scr[e] = alpha * l_scr[e] + jnp.sum(p, axis=1, keepdims=True)
                acc_scr[e] = alpha[:, :dv] * acc_scr[e] + jnp.dot(p.astype(BF16), vb, preferred_element_type=F32)
                m_scr[e] = m_new

        def step(j, carry):
            process(j, False)
            return carry

        lax.fori_loop(0, qi, step, 0)
        process(qi, True)
        for e in range(hb):
            l = l_scr[e]
            o_ref[e] = acc_scr[e] / l[:, :dv]
            lse_ref[e, 0] = jnp.transpose(m_scr[e] + jnp.log(l))[0:1, :]

    ins = [BS((hb, t, dk), lambda a, b: (a, b, 0)), BS((hb, n, dk), lambda a, b: (a, 0, 0)),
           BS((hb, n, dv), lambda a, b: (a, 0, 0))]
    args = [q, k, v]
    if fox:
        ins.append(BS((hb, nq, 1, t), lambda a, b: (a, 0, 0, 0)))
        args.append(ck_r)
    outs = [BS((hb, t, dv), lambda a, b: (a, b, 0)), BS((hb, 1, 1, t), lambda a, b: (a, b, 0, 0))]
    oshape = [SDS((h, n, dv), F32), SDS((h, nq, 1, t), F32)]
    scratch = [pltpu.VMEM((hb, t, LANES), F32), pltpu.VMEM((hb, t, LANES), F32), pltpu.VMEM((hb, t, dv), F32)]
    if nl:
        ins += [ANY] * nl
        args += list(late)
        outs += [ANY] * nl
        oshape += [SDS((N_CHIPS,) + a.shape[1:], a.dtype) for a in late]
        scratch += [pltpu.SemaphoreType.DMA((6 * nl,)), pltpu.SemaphoreType.DMA((6 * nl,))]
    res = pl.pallas_call(
        body, name=name, grid=(h // hb, nq), in_specs=ins, out_specs=outs, out_shape=oshape, scratch_shapes=scratch,
        compiler_params=pltpu.CompilerParams(dimension_semantics=("arbitrary", "arbitrary"),
                                             vmem_limit_bytes=VMEM_LIMIT, has_side_effects=bool(nl)))(*args)
    return res[0], res[1], list(res[2:])


def _causal_bwd_call(q, k, v, do, o, lse_r, ck_r, fox, scale, name):
    h, n, dk = q.shape
    dv = v.shape[2]
    t = _causal_tile(n)
    nq = n // t
    hb = HEADS_PER_STEP

    def body(*refs):
        it = iter(refs)
        q_ref, k_ref, v_ref, do_ref, o_ref, lse_ref = (next(it) for _ in range(6))
        ck_ref = next(it) if fox else None
        dq_ref, dk_ref, dv_ref = next(it), next(it), next(it)
        dck_ref, dcq_ref = (next(it), next(it)) if fox else (None, None)
        delta_scr, dk_scr, dv_scr = next(it), next(it), next(it)
        dck_scr = next(it) if fox else None
        kj = pl.program_id(1)

        @pl.when(kj == 0)
        def _():
            dq_ref[...] = jnp.zeros_like(dq_ref)
            if fox:
                dcq_ref[...] = jnp.zeros_like(dcq_ref)
            ones = jnp.ones((SUBLANES, dv), F32)

            def fill(qi, carry):
                off = pl.multiple_of(qi * t, t)
                for e in range(hb):
                    prod = do_ref[e, pl.ds(off, t), :] * o_ref[e, pl.ds(off, t), :]
                    delta_scr[e, qi] = lax.dot_general(ones, prod, NT_DIMS, preferred_element_type=F32,
                                                       precision=lax.Precision.HIGHEST)[0:1, :]
                return carry

            lax.fori_loop(0, nq, fill, 0)

        kbs = [k_ref[e].astype(BF16) for e in range(hb)]
        vbs = [v_ref[e].astype(BF16) for e in range(hb)]
        dk_scr[...] = jnp.zeros_like(dk_scr)
        dv_scr[...] = jnp.zeros_like(dv_scr)
        if fox:
            dck_scr[...] = jnp.zeros_like(dck_scr)
            ckcs = [jnp.tile(jnp.transpose(jnp.broadcast_to(ck_ref[e, 0], (LANES, t))), (1, t // LANES))
                    for e in range(hb)]

        def process(qi, masked):
            off = pl.multiple_of(qi * t, t)
            if masked:
                krows = lax.broadcasted_iota(jnp.int32, (t, t), 0)
                qcols = lax.broadcasted_iota(jnp.int32, (t, t), 1)
                valid = krows <= qcols
            for e in range(hb):
                qb = q_ref[e, pl.ds(off, t), :].astype(BF16)
                dob = do_ref[e, pl.ds(off, t), :].astype(BF16)
                st = lax.dot_general(kbs[e], qb, NT_DIMS, preferred_element_type=F32) * scale
                if fox:
                    st = st - ckcs[e]
                pt = jnp.exp(st - lse_ref[e, qi])
                if masked:
                    pt = jnp.where(valid, pt, 0.0)
                dv_scr[e] += jnp.dot(pt.astype(BF16), dob, preferred_element_type=F32)
                dpt = lax.dot_general(vbs[e], dob, NT_DIMS, preferred_element_type=F32)
                dst = pt * (dpt - delta_scr[e, qi])
                if fox:
                    dck_scr[e] -= jnp.sum(dst, axis=1, keepdims=True)
                    dcq_ref[e, qi] += jnp.sum(dst, axis=0, keepdims=True)
                dsb = (dst * scale).astype(BF16)
                dk_scr[e] += jnp.dot(dsb, qb, preferred_element_type=F32)
                dq_ref[e, pl.ds(off, t), :] += lax.dot_general(dsb, kbs[e], TN_DIMS, preferred_element_type=F32)

        def step(qi, carry):
            process(qi, False)
            return carry

        process(kj, True)
        lax.fori_loop(kj + 1, nq, step, 0)
        dk_ref[...] = dk_scr[...]
        dv_ref[...] = dv_scr[...]
        if fox:
            for e in range(hb):
                dck_ref[e, 0] = jnp.transpose(jnp.broadcast_to(dck_scr[e], (t, LANES)))[0:1, :]

    whole = lambda a, b: (a, 0, 0)
    tile = lambda a, b: (a, b, 0)
    rowv = lambda a, b: (a, 0, 0, 0)
    rowt = lambda a, b: (a, b, 0, 0)
    ins = [BS((hb, n, dk), whole), BS((hb, t, dk), tile), BS((hb, t, dv), tile), BS((hb, n, dv), whole),
           BS((hb, n, dv), whole), BS((hb, nq, 1, t), rowv)]
    args = [q, k, v, do, o, lse_r]
    outs = [BS((hb, n, dk), whole), BS((hb, t, dk), tile), BS((hb, t, dv), tile)]
    oshape = [SDS((h, n, dk), F32), SDS((h, n, dk), F32), SDS((h, n, dv), F32)]
    scratch = [pltpu.VMEM((hb, nq, 1, t), F32), pltpu.VMEM((hb, t, dk), F32), pltpu.VMEM((hb, t, dv), F32)]
    if fox:
        ins.append(BS((hb, 1, 1, t), rowt))
        args.append(ck_r)
        outs += [BS((hb, 1, 1, t), rowt), BS((hb, nq, 1, t), rowv)]
        oshape += [SDS((h, nq, 1, t), F32), SDS((h, nq, 1, t), F32)]
        scratch.append(pltpu.VMEM((hb, t, 1), F32))
    return pl.pallas_call(
        body, name=name, grid=(h // hb, nq), in_specs=ins, out_specs=outs, out_shape=oshape, scratch_shapes=scratch,
        compiler_params=_cparams(("arbitrary", "arbitrary")))(*args)


def causal_attention(q, k, v, c, scale, name, late=None):
    h, n, _ = q.shape
    t = _causal_tile(n)
    nq = n // t
    fox = c is not None

    def run_fwd(q, k, v, c, late):
        ck_r = c.reshape(h, nq, 1, t) if fox else None
        o, lse, got = _causal_fwd_call(q, k, v, ck_r, fox, scale, name + "_f", late)
        return (o if late is None else (o, got)), (q, k, v, c, late, o, lse)

    def run_bwd(res, ct):
        q, k, v, c, late, o, lse = res
        do = ct if late is None else ct[0]
        ck_r = c.reshape(h, nq, 1, t) if fox else None
        outs = _causal_bwd_call(q, k, v, do, o, lse, ck_r, fox, scale, name + "_b")
        dlate = None if late is None else [jnp.zeros_like(a) for a in late]
        return outs[0], outs[1], outs[2], ((outs[3] + outs[4]).reshape(h, n) if fox else None), dlate

    return _op(run_fwd, run_bwd)(q, k, v, c, late)


SWA_T = 128


def _swa_masks(qi):
    t = SWA_T
    r = lax.broadcasted_iota(jnp.int32, (t, 3 * t), 0)
    c = lax.broadcasted_iota(jnp.int32, (t, 3 * t), 1)
    seg0 = c < t
    seg1 = (c >= t) & (c < 2 * t)
    jp = jnp.maximum(qi - 1, 0)
    kpos = jnp.where(seg0, c, jnp.where(seg1, jp * t + c - t, qi * t + c - 2 * t))
    dist = qi * t + r - kpos
    band = (dist >= 0) & ((dist < WINDOW) | (kpos < N_META))
    valid = (seg0 & (kpos < N_META) & (qi >= 2)) | (jnp.logical_not(seg0) & band & (jnp.logical_not(seg1) | (qi >= 1)))
    return valid, dist.astype(F32)


def _swa_cat(ref, qi):
    t = SWA_T
    jp = jnp.maximum(qi - 1, 0)
    return jnp.concatenate([ref[0, 0:t, :], ref[0, pl.ds(pl.multiple_of(jp * t, t), t), :],
                            ref[0, pl.ds(pl.multiple_of(qi * t, t), t), :]], axis=0).astype(BF16)


def _swa_fwd_call(q, k, v, sinks, slopes, scale, name):
    hq, n, d = q.shape
    hkv = k.shape[0]
    g = hq // hkv
    t = SWA_T
    nq = n // t

    def body(q_ref, k_ref, v_ref, sink_ref, slope_ref, o_ref, lse_ref):
        grp = pl.program_id(0)
        qi = pl.program_id(1)
        valid, dist = _swa_masks(qi)
        kc = _swa_cat(k_ref, qi)
        vc = _swa_cat(v_ref, qi)
        qs = jnp.concatenate([q_ref[e] for e in range(g)], axis=0).astype(BF16)
        s_all = lax.dot_general(qs, kc, NT_DIMS, preferred_element_type=F32) * scale
        ps, ls, ms = [], [], []
        for e in range(g):
            hh = grp * g + e
            s = jnp.where(valid, s_all[e * t:(e + 1) * t] - slope_ref[hh] * dist, NEG)
            m = jnp.maximum(jnp.max(s, axis=1, keepdims=True), sink_ref[hh])
            p = jnp.exp(s - m)
            ls.append(jnp.sum(p, axis=1, keepdims=True) + jnp.exp(sink_ref[hh] - m))
            ms.append(m)
            ps.append(p.astype(BF16))
        acc = jnp.dot(jnp.concatenate(ps, axis=0), vc, preferred_element_type=F32)
        for e in range(g):
            o_ref[e] = acc[e * t:(e + 1) * t] / ls[e]
            lse_ref[e] = ms[e] + jnp.log(ls[e])

    return pl.pallas_call(
        body, name=name, grid=(hkv, nq),
        in_specs=[BS((g, t, d), lambda a, b: (a, b, 0)), BS((1, n, d), lambda a, b: (a, 0, 0)),
                  BS((1, n, d), lambda a, b: (a, 0, 0)), BS(memory_space=pltpu.SMEM), BS(memory_space=pltpu.SMEM)],
        out_specs=[BS((g, t, d), lambda a, b: (a, b, 0)), BS((g, t, 1), lambda a, b: (a, b, 0))],
        out_shape=[SDS((hq, n, d), F32), SDS((hq, n, 1), F32)],
        compiler_params=_cparams(("parallel", "parallel")))(q, k, v, sinks, slopes)


def _swa_bwd_call(q, k, v, o, lse, do, sinks, slopes, scale, name):
    hq, n, d = q.shape
    hkv = k.shape[0]
    g = hq // hkv
    t = SWA_T
    nq = n // t

    def body(q_ref, k_ref, v_ref, o_ref, lse_ref, do_ref, sink_ref, slope_ref, dq_ref, dk_ref, dv_ref, ds_ref):
        grp = pl.program_id(0)
        qi = pl.program_id(1)

        @pl.when(qi == 0)
        def _():
            dk_ref[...] = jnp.zeros_like(dk_ref)
            dv_ref[...] = jnp.zeros_like(dv_ref)
            ds_ref[...] = jnp.zeros_like(ds_ref)

        valid, dist = _swa_masks(qi)
        kc = _swa_cat(k_ref, qi)
        vc = _swa_cat(v_ref, qi)
        qs = jnp.concatenate([q_ref[e] for e in range(g)], axis=0).astype(BF16)
        dos = jnp.concatenate([do_ref[e] for e in range(g)], axis=0).astype(BF16)
        s_all = lax.dot_general(qs, kc, NT_DIMS, preferred_element_type=F32) * scale
        dp_all = lax.dot_general(dos, vc, NT_DIMS, preferred_element_type=F32)
        ps, dss = [], []
        for e in range(g):
            hh = grp * g + e
            lse_e = lse_ref[e]
            delta = jnp.sum(do_ref[e] * o_ref[e], axis=1, keepdims=True)
            s = s_all[e * t:(e + 1) * t] - slope_ref[hh] * dist
            p = jnp.where(valid, jnp.exp(s - lse_e), 0.0)
            ds = p * (dp_all[e * t:(e + 1) * t] - delta)
            ps.append(p.astype(BF16))
            dss.append((ds * scale).astype(BF16))
            ds_ref[e] += -jnp.sum(jnp.exp(sink_ref[hh] - lse_e) * delta)
        p_st = jnp.concatenate(ps, axis=0)
        ds_st = jnp.concatenate(dss, axis=0)
        dq = jnp.dot(ds_st, kc, preferred_element_type=F32)
        for e in range(g):
            dq_ref[e] = dq[e * t:(e + 1) * t]
        dkc = lax.dot_general(ds_st, qs, TN_DIMS, preferred_element_type=F32)
        dvc = lax.dot_general(p_st, dos, TN_DIMS, preferred_element_type=F32)
        jp = jnp.maximum(qi - 1, 0)
        for seg, off in enumerate((0, pl.multiple_of(jp * t, t), pl.multiple_of(qi * t, t))):
            dk_ref[0, pl.ds(off, t), :] += dkc[seg * t:(seg + 1) * t]
            dv_ref[0, pl.ds(off, t), :] += dvc[seg * t:(seg + 1) * t]

    tile = lambda a, b: (a, b, 0)
    whole = lambda a, b: (a, 0, 0)
    return pl.pallas_call(
        body, name=name, grid=(hkv, nq),
        in_specs=[BS((g, t, d), tile), BS((1, n, d), whole), BS((1, n, d), whole), BS((g, t, d), tile),
                  BS((g, t, 1), tile), BS((g, t, d), tile), BS(memory_space=pltpu.SMEM), BS(memory_space=pltpu.SMEM)],
        out_specs=[BS((g, t, d), tile), BS((1, n, d), whole), BS((1, n, d), whole), BS((g, 1, LANES), whole)],
        out_shape=[SDS((hq, n, d), F32), SDS((hkv, n, d), F32), SDS((hkv, n, d), F32), SDS((hq, 1, LANES), F32)],
        compiler_params=_cparams(("arbitrary", "arbitrary")))(q, k, v, o, lse, do, sinks, slopes)


def window_attention(q, k, v, sinks, slopes, scale, name):
    def run_fwd(q, k, v, sinks, slopes):
        o, lse = _swa_fwd_call(q, k, v, sinks, slopes, scale, name + "_f")
        return o, (q, k, v, sinks, slopes, o, lse)

    def run_bwd(res, do):
        q, k, v, sinks, slopes, o, lse = res
        dq, dk, dv, ds = _swa_bwd_call(q, k, v, o, lse, do, sinks, slopes, scale, name + "_b")
        return dq, dk, dv, ds[:, 0, 0], jnp.zeros_like(slopes)

    return _op(run_fwd, run_bwd)(q, k, v, sinks, slopes)


def _sigmoid(x):
    return 1.0 / (1.0 + jnp.exp(-x))


def _merge_fwd_call(gs, ys, name):
    n, c = ys[0].shape
    tr = _rows_tile(n, c, 1 << 20)

    def body(g0, g1, g2, y0, y1, y2, m_ref):
        m_ref[...] = (_sigmoid(g0[...]) * y0[...] + _sigmoid(g1[...]) * y1[...]) + _sigmoid(g2[...]) * y2[...]

    spec = BS((tr, c), lambda i: (i, 0))
    return pl.pallas_call(
        body, name=name, grid=(n // tr,), in_specs=[spec] * 6, out_specs=spec, out_shape=SDS((n, c), F32),
        compiler_params=_cparams(("parallel",)))(*gs, *ys)


def _merge_bwd_call(gs, ys, dm, name):
    n, c = ys[0].shape
    tr = _rows_tile(n, c, 1 << 20)

    def body(g0, g1, g2, y0, y1, y2, dm_ref, dg0, dg1, dg2, dy0, dy1, dy2):
        d = dm_ref[...]
        for g, y, dg, dy in ((g0, y0, dg0, dy0), (g1, y1, dg1, dy1), (g2, y2, dg2, dy2)):
            s = _sigmoid(g[...])
            dy[...] = d * s
            dg[...] = d * y[...] * (s * (1.0 - s))

    spec = BS((tr, c), lambda i: (i, 0))
    return pl.pallas_call(
        body, name=name, grid=(n // tr,), in_specs=[spec] * 7, out_specs=[spec] * 6,
        out_shape=[SDS((n, c), F32)] * 6, compiler_params=_cparams(("parallel",)))(*gs, *ys, dm)


def gated_merge(gs, ys, name):
    def fwd(gs, ys):
        return _merge_fwd_call(gs, ys, name + "_f"), (gs, ys)

    def bwd(res, dm):
        out = _merge_bwd_call(res[0], res[1], dm, name + "_b")
        return tuple(out[:3]), tuple(out[3:])

    return _op(fwd, bwd)(tuple(gs), tuple(ys))


CONV_TR = 264
CONV_TC = 1408


def _conv_tiles(n, f):
    tr = CONV_TR if n % CONV_TR == 0 else _div_tile(n, CONV_TR)
    tc = CONV_TC if f % CONV_TC == 0 else f
    return tr, tc


def _shift_down(cur, halo, first, tr):
    halo = jnp.where(first, 0.0, halo)
    row = lax.broadcasted_iota(jnp.int32, cur.shape, 0)
    h7, h6 = halo[7:8, :], halo[6:7, :]
    u1 = jnp.where(row == 0, h7, pltpu.roll(cur, 1, 0))
    u2 = jnp.where(row == 0, h6, jnp.where(row == 1, h7, pltpu.roll(cur, 2, 0)))
    return u1, u2


def _conv_lin(cur, u1, u2, w_ref, b_ref):
    return ((b_ref[...] + w_ref[0:1, :] * u2) + w_ref[1:2, :] * u1) + w_ref[2:3, :] * cur


def _conv_in_specs(tr, tc, nj):
    sub = tr // SUBLANES
    prev = lambda j, i: (jnp.maximum(i * sub - 1, 0), j)
    prev_v = lambda j, i: (jnp.maximum(i * sub - 1, 0), j + nj)
    return [BS((tr, tc), lambda j, i: (i, j)), BS((SUBLANES, tc), prev),
            BS((tr, tc), lambda j, i: (i, j + nj)), BS((SUBLANES, tc), prev_v),
            BS((3, tc), lambda j, i: (0, j)), BS((3, tc), lambda j, i: (0, j + nj)),
            BS((1, tc), lambda j, i: (0, j)), BS((1, tc), lambda j, i: (0, j + nj))]


def _conv_fwd_call(u, cw, cb, name):
    n, f2 = u.shape
    f = f2 // 2
    tr, tc = _conv_tiles(n, f)
    nj = f // tc

    def body(ug, ugh, uv, uvh, wg, wv, bg, bv, a_ref):
        first = pl.program_id(1) == 0
        g1, g2 = _shift_down(ug[...], ugh[...], first, tr)
        v1, v2 = _shift_down(uv[...], uvh[...], first, tr)
        cg = _conv_lin(ug[...], g1, g2, wg, bg)
        cv = _conv_lin(uv[...], v1, v2, wv, bv)
        a_ref[...] = cg * _sigmoid(cg) * cv

    return pl.pallas_call(
        body, name=name, grid=(nj, n // tr), in_specs=_conv_in_specs(tr, tc, nj),
        out_specs=BS((tr, tc), lambda j, i: (i, j)), out_shape=SDS((n, f), F32),
        compiler_params=_cparams(("parallel", "parallel")))(u, u, u, u, cw, cw, cb, cb)


def _conv_bwd_dc_call(u, cw, cb, da, name):
    n, f2 = u.shape
    f = f2 // 2
    tr, tc = _conv_tiles(n, f)
    nj = f // tc

    def body(ug, ugh, uv, uvh, wg, wv, bg, bv, da_ref, dc_ref, dw_ref, db_ref):
        first = pl.program_id(1) == 0
        g0, v0 = ug[...], uv[...]
        g1, g2 = _shift_down(g0, ugh[...], first, tr)
        v1, v2 = _shift_down(v0, uvh[...], first, tr)
        cg = _conv_lin(g0, g1, g2, wg, bg)
        cv = _conv_lin(v0, v1, v2, wv, bv)
        d = da_ref[...]
        s = _sigmoid(cg)
        dcg = d * cv * (s * (1.0 + cg * (1.0 - s)))
        dcv = d * (cg * s)
        dc_ref[0] = dcg
        dc_ref[1] = dcv

        @pl.when(first)
        def _():
            dw_ref[...] = jnp.zeros_like(dw_ref)
            db_ref[...] = jnp.zeros_like(db_ref)

        for p, dc, taps in ((0, dcg, (g2, g1, g0)), (1, dcv, (v2, v1, v0))):
            for t in range(3):
                dw_ref[p, t:t + 1, :] += jnp.sum(dc * taps[t], axis=0, keepdims=True)
            db_ref[p] += jnp.sum(dc, axis=0, keepdims=True)

    return pl.pallas_call(
        body, name=name, grid=(nj, n // tr),
        in_specs=_conv_in_specs(tr, tc, nj) + [BS((tr, tc), lambda j, i: (i, j))],
        out_specs=[BS((2, tr, tc), lambda j, i: (0, i, j)), BS((2, 3, tc), lambda j, i: (0, 0, j)),
                   BS((2, 1, tc), lambda j, i: (0, 0, j))],
        out_shape=[SDS((2, n, f), F32), SDS((2, 3, f), F32), SDS((2, 1, f), F32)],
        compiler_params=_cparams(("arbitrary", "arbitrary")))(u, u, u, u, cw, cw, cb, cb, da)


def _conv_bwd_du_call(dc, cw, name):
    _, n, f = dc.shape
    tr, tc = _conv_tiles(n, f)
    nj = f // tc
    ni = n // tr
    sub = tr // SUBLANES

    def body(c_ref, nx_ref, w_ref, du_ref):
        cur = c_ref[0]
        nxt = jnp.where(pl.program_id(2) == ni - 1, 0.0, nx_ref[0])
        row = lax.broadcasted_iota(jnp.int32, cur.shape, 0)
        n0, n1 = nxt[0:1, :], nxt[1:2, :]
        d1 = jnp.where(row == tr - 1, n0, pltpu.roll(cur, tr - 1, 0))
        d2 = jnp.where(row == tr - 1, n1, jnp.where(row == tr - 2, n0, pltpu.roll(cur, tr - 2, 0)))
        du_ref[...] = (w_ref[2:3, :] * cur + w_ref[1:2, :] * d1) + w_ref[0:1, :] * d2

    nxt_map = lambda p, j, i: (p, jnp.minimum((i + 1) * sub, n // SUBLANES - 1), j)
    return pl.pallas_call(
        body, name=name, grid=(2, nj, ni),
        in_specs=[BS((1, tr, tc), lambda p, j, i: (p, i, j)), BS((1, SUBLANES, tc), nxt_map),
                  BS((3, tc), lambda p, j, i: (0, p * nj + j))],
        out_specs=BS((tr, tc), lambda p, j, i: (i, p * nj + j)), out_shape=SDS((n, 2 * f), F32),
        compiler_params=_cparams(("parallel", "parallel", "parallel")))(dc, dc, cw)


def conv_glu(u, cw, cb, name):
    def fwd(u, cw, cb):
        return _conv_fwd_call(u, cw, cb, name + "_f"), (u, cw, cb)

    def bwd(res, da):
        u, cw, cb = res
        dc, dw, db = _conv_bwd_dc_call(u, cw, cb, da, name + "_bc")
        du = _conv_bwd_du_call(dc, cw, name + "_bu")
        return du, jnp.concatenate([dw[0], dw[1]], axis=-1), jnp.concatenate([db[0], db[1]], axis=-1)

    return _op(fwd, bwd)(u, cw, cb)


def _loss_call(y, t, n_real, name):
    n, c = y.shape
    tr = _rows_tile(n, c, 1 << 20)

    def body(y_ref, t_ref, dy_ref, l_ref):
        i = pl.program_id(0)
        row = i * tr + lax.broadcasted_iota(jnp.int32, (tr, c), 0)
        real = (row >= N_META) & (row < N_META + n_real)
        e = jnp.where(real, y_ref[...] - t_ref[...], 0.0)
        dy_ref[...] = e * (1.0 / c)

        @pl.when(i == 0)
        def _():
            l_ref[...] = jnp.zeros_like(l_ref)

        l_ref[...] += 0.5 * jnp.sum(jnp.sum(e * e, axis=-1, keepdims=True) * (1.0 / c), axis=0, keepdims=True)

    spec = BS((tr, c), lambda i: (i, 0))
    return pl.pallas_call(
        body, name=name, grid=(n // tr,), in_specs=[spec, spec],
        out_specs=[spec, BS((1, 1), lambda i: (0, 0))], out_shape=[SDS((n, c), F32), SDS((1, 1), F32)],
        compiler_params=_cparams(("arbitrary",)))(y, t)


def _to_heads(x, nh):
    n = x.shape[0]
    return x.reshape(n, nh, x.shape[1] // nh).transpose(1, 0, 2)


def _from_heads(x):
    h, n, d = x.shape
    return x.transpose(1, 0, 2).reshape(n, h * d)


def _head_norm(x, g, denom, name):
    h, n, d = x.shape
    return rms_norm(x.reshape(h * n, d), g, denom, name).reshape(h, n, d)


def _head_norm_rope(x, g, cos, sin, name):
    h, n, d = x.shape
    return rms_norm_rope(x.reshape(h * n, d), g, cos, sin, MLA_QK, name).reshape(h, n, d)


def _pad_in_cols(w):
    z = lambda k: jnp.zeros(w.shape[:-1] + (k,), w.dtype)
    return jnp.concatenate([w[..., :1544], z(120), w[..., 1544:1960], z(96), w[..., 1960:], z(128)], axis=-1)


def _pad_q_up(w):
    s = w.shape[:-1]
    w = w.reshape(s + (HEADS, MLA_QK))
    w = jnp.concatenate([w, jnp.zeros(s + (HEADS, LANES - MLA_QK), w.dtype)], axis=-1)
    return w.reshape(s + (HEADS * LANES,))


def _assemble(shards):
    full = {k: jnp.concatenate([v[i] for i in range(N_CHIPS)], axis=SHARD_AXIS[k]) for k, v in shards.items()}
    out = dict(full)
    if "w_in" in out:
        out["w_in"] = _pad_in_cols(out["w_in"])
    if "mla_w_q_up" in out:
        out["mla_w_q_up"] = _pad_q_up(out["mla_w_q_up"])
    return out


LAYERED = BIG + ("ffn_conv_w",)


def _assemble_layer(parts):
    out = {k: jnp.concatenate([v[i] for i in range(N_CHIPS)], axis=SHARD_AXIS[k] - 1) for k, v in parts.items()}
    out["w_in"] = _pad_in_cols(out["w_in"])
    out["mla_w_q_up"] = _pad_q_up(out["mla_w_q_up"])
    return out


def _layer_parts(gathered):
    return {k: g.reshape((N_CHIPS,) + SHARD_SHAPE[k][1:]) for k, g in zip(LAYERED, gathered)}


def _rope_tables(n):
    half = MLA_ROPE // 2
    freqs = ROPE_THETA ** (-jnp.arange(half, dtype=F32) / half)
    ang = jnp.arange(n).astype(F32)[:, None] * freqs[None, :]
    cos, sin = jnp.cos(ang), jnp.sin(ang)
    one, zero = jnp.ones((n, MLA_NOPE), F32), jnp.zeros((n, MLA_NOPE), F32)
    tail1, tail0 = jnp.ones((n, LANES - MLA_QK), F32), jnp.zeros((n, LANES - MLA_QK), F32)
    return (jnp.concatenate([one, cos, cos, tail1], axis=1), jnp.concatenate([zero, sin, sin, tail0], axis=1))


def _pad_lanes(g, width):
    return jnp.concatenate([g, jnp.zeros((width - g.shape[0],), g.dtype)]).reshape(1, width)


PROJ_SEGMENTS = ((O_FQ, 512), (O_FK, 512), (O_FV, 512), (O_FF, HEADS), (O_CQ, MLA_Q_RANK), (O_CKV, MLA_KV_RANK),
                 (O_KR, MLA_ROPE), (O_SQ, 512), (O_SK, 128), (O_SV, 128), (O_G, D_MODEL), (O_G + D_MODEL, D_MODEL),
                 (O_G + 2 * D_MODEL, D_MODEL))


def _split_proj(proj):
    def fwd(x):
        return tuple(x[:, s:s + w] for s, w in PROJ_SEGMENTS), None

    def bwd(_, cts):
        rows, parts, pos = cts[0].shape[0], [], 0
        for (s, w), ct in zip(PROJ_SEGMENTS, cts):
            if s > pos:
                parts.append(jnp.zeros((rows, s - pos), F32))
            parts.append(ct)
            pos = s + w
        parts.append(jnp.zeros((rows, IN_PAD - pos), F32))
        return (jnp.concatenate(parts, axis=1),)

    return _op(fwd, bwd)(proj)


def _trunk(eps, eps_cw, meta, small, x, w0, late):
    assert DEPTH == 2
    seq = x.shape[0]
    n = -(-(N_META + seq) // ROW_PAD) * ROW_PAD
    ew = _assemble(eps)
    cos, sin = _rope_tables(n)
    slopes = jnp.exp2(-8.0 * jnp.arange(1, HEADS + 1, dtype=F32) / HEADS)
    h = jnp.concatenate([meta, x, jnp.zeros((n - N_META - seq, D_MODEL), F32)], axis=0)
    wb = w0
    for l in range(DEPTH):
        p = f"l{l}_"
        row = lambda name: small[name][l].reshape(1, -1)
        xn = rms_norm(h, row("norm1_g"), D_MODEL, p + "norm1")
        proj = linear(xn, wb["w_in"], ew["w_in"][l], p + "win")
        p_fq, p_fk, p_fv, p_ff, p_cq, p_ckv, p_kr, p_sq, p_sk, p_sv, g0, g1, g2 = _split_proj(proj)
        fq = _head_norm(_to_heads(p_fq, HEADS), row("fox_q_g"), HEAD_DIM, p + "fqn")
        fk = _head_norm(_to_heads(p_fk, HEADS), row("fox_k_g"), HEAD_DIM, p + "fkn")
        fv = _to_heads(p_fv, HEADS)
        c = forget_cumsum(p_ff.T, small["fox_forget_b"][l].reshape(HEADS, 1), p + "fgate")
        if l == 0:
            out_a, got = causal_attention(fq, fk, fv, c, HEAD_DIM ** -0.5, p + "fox", late)
            w1 = _assemble_layer(_layer_parts(_fill_own(got, [a[1] for a in late])))
        else:
            out_a = causal_attention(fq, fk, fv, c, HEAD_DIM ** -0.5, p + "fox")
        cqn = rms_norm(p_cq, row("mla_q_a_g"), MLA_Q_RANK, p + "cqn")
        q = _to_heads(linear(cqn, wb["mla_w_q_up"], ew["mla_w_q_up"][l], p + "qup"), HEADS)
        q = _head_norm_rope(q, _pad_lanes(small["mla_q_g"][l], LANES), cos, sin, p + "mqn")
        ckvn = rms_norm(p_ckv, row("mla_kv_a_g"), MLA_KV_RANK, p + "ckvn")
        kv = _to_heads(linear(ckvn, wb["mla_w_kv_up"], ew["mla_w_kv_up"][l], p + "kvup"), HEADS)
        kr = jnp.broadcast_to(p_kr[None], (HEADS, n, MLA_ROPE))
        k = jnp.concatenate([kv[..., :MLA_NOPE], kr, jnp.zeros((HEADS, n, LANES - MLA_QK), F32)], axis=-1)
        k = _head_norm_rope(k, _pad_lanes(small["mla_k_g"][l], LANES), cos, sin, p + "mkn")
        out_b = causal_attention(q, k, kv[..., MLA_NOPE:], None, MLA_QK ** -0.5, p + "mla")
        sq = _head_norm(_to_heads(p_sq, HEADS), row("swa_q_g"), HEAD_DIM, p + "sqn")
        sk = _head_norm(_to_heads(p_sk, SWA_KV_HEADS), row("swa_k_g"), HEAD_DIM, p + "skn")
        sv = _to_heads(p_sv, SWA_KV_HEADS)
        out_c = window_attention(sq, sk, sv, small["swa_sinks"][l], slopes, HEAD_DIM ** -0.5, p + "swa")
        ys = [linear(_from_heads(o), wb["w_branch"][i], ew["w_branch"][l, i], p + f"br{i}")
              for i, o in enumerate((out_a, out_b, out_c))]
        merged = gated_merge([g0, g1, g2], ys, p + "merge")
        h = linear(merged, wb["w_o"], ew["w_o"][l], p + "wo", res=h)
        xn2 = rms_norm(h, row("norm2_g"), D_MODEL, p + "norm2")
        u = linear(xn2, wb["ffn_w_up"], ew["ffn_w_up"][l], p + "wup")
        act = conv_glu(u, lax.stop_gradient(wb["ffn_conv_w"]) + eps_cw[l], row("ffn_conv_b"), p + "conv")
        h = linear(act, wb["ffn_w_down"], ew["ffn_w_down"][l], p + "wdown", res=h)
        wb = w1
    return h


def _local_step(x, target, w0, late, meta, small):
    seq = x.shape[0]
    eps = {k: jnp.zeros((N_CHIPS,) + SHARD_SHAPE[k], F32) for k in BIG}
    eps_cw = jnp.zeros((DEPTH, 3, 2 * D_FF), F32)
    y, vjp = jax.vjp(lambda e, ec, mt, s, xx: _trunk(e, ec, mt, s, xx, w0, late), eps, eps_cw, meta, small, x)
    n = y.shape[0]
    tpad = jnp.concatenate([jnp.zeros((N_META, D_MODEL), F32), target, jnp.zeros((n - N_META - seq, D_MODEL), F32)])
    dy, loss = _loss_call(y, tpad, seq, "loss")
    g_eps, g_cw, g_meta, g_small, g_x = vjp(dy)
    return loss[0, 0], g_x, g_eps, g_cw, g_meta, g_small


def _pack_rows(shapes, mult):
    total = sum(_size(s) for s in shapes)
    rows = -(-total // LANES)
    return -(-rows // mult) * mult


def _size(shape):
    n = 1
    for d in shape:
        n *= d
    return n


def _pack(arrs, rows, dtype):
    flat = [a.reshape(-1).astype(dtype) for a in arrs]
    used = sum(a.size for a in flat)
    flat.append(jnp.zeros((rows * LANES - used,), dtype))
    return jnp.concatenate(flat).reshape(rows, LANES)


def _unpack(p, shapes):
    flat = p.reshape(-1)
    out, off = [], 0
    for s in shapes:
        out.append(flat[off:off + _size(s)].reshape(s))
        off += _size(s)
    return out


MESH = pl.DeviceIdType.MESH
ANY = pl.BlockSpec(memory_space=pl.ANY)


def _me():
    return lax.axis_index("x"), lax.axis_index("y"), lax.axis_index("c")


def _remote(src, dst, send_sems, recv_sems, idx, dev):
    return pltpu.make_async_remote_copy(src_ref=src, dst_ref=dst, send_sem=send_sems.at[idx], recv_sem=recv_sems.at[idx],
                                        device_id=dev, device_id_type=MESH)


def _layer_gather(ins, outs, send_sems, recv_sems, layer, base):
    x, y, _ = _me()
    j = 2 * x + y
    sibling = (x, y, 1 - layer)
    chips = [(1 - x, y), (x, 1 - y), (1 - x, 1 - y)]

    def ici(p, r):
        cx, cy = chips[r]
        return _remote(ins[p].at[layer], outs[p].at[j], send_sems, recv_sems, base + 6 * p + r, (cx, cy, layer))

    def d2d(p, r):
        cx, cy = chips[r]
        blk = outs[p].at[2 * cx + cy]
        return _remote(blk, blk, send_sems, recv_sems, base + 6 * p + 3 + r, sibling)

    pairs = [(p, r) for p in range(len(ins)) for r in range(3)]

    def start():
        for p, r in pairs:
            ici(p, r).start()

    def forward():
        for p, r in pairs:
            ici(p, r).wait_recv()
            d2d(p, r).start()

    def drain():
        for p, r in pairs:
            ici(p, r).wait_send()
            d2d(p, r).wait_send()

    def receive():
        for p, r in pairs:
            d2d(p, r).wait_recv()

    return start, forward, drain, receive


def _fill_own(outs, own):
    j = 2 * lax.axis_index("x") + lax.axis_index("y")
    return [lax.dynamic_update_index_in_dim(o, a, j, 0) for o, a in zip(outs, own)]


def _gather_early(meta, arrs, name):
    npk = len(arrs)

    def body(*refs):
        m_in, ins = refs[0], refs[1:npk + 1]
        m_out, outs = refs[npk + 1], refs[npk + 2:2 * npk + 2]
        send_sems, recv_sems = refs[2 * npk + 2:]
        x, y, c = _me()
        j = 2 * x + y
        sibling = (x, y, 1 - c)
        chips = [(1 - x, y), (x, 1 - y), (1 - x, 1 - y)]
        start, forward, drain, receive = _layer_gather(ins, outs, send_sems, recv_sems, 0, 6)
        sends = []
        for r, (cx, cy) in enumerate(chips):
            cp = _remote(m_in.at[c], m_out.at[j, c], send_sems, recv_sems, r, (cx, cy, c))
            cp.start()
            sends.append(cp)
        pl.when(c == 0)(start)
        for r, (cx, cy) in enumerate(chips):
            blk = m_out.at[2 * cx + cy, c]
            _remote(blk, blk, send_sems, recv_sems, r, sibling).wait_recv()
            fw = _remote(blk, blk, send_sems, recv_sems, 3 + r, sibling)
            fw.start()
            sends.append(fw)
        for r, (cx, cy) in enumerate(chips):
            blk = m_out.at[2 * cx + cy, 1 - c]
            _remote(blk, blk, send_sems, recv_sems, 3 + r, sibling).wait_recv()
        for cp in sends:
            cp.wait_send()

        @pl.when(c == 0)
        def _():
            forward()
            drain()

        pl.when(c == 1)(receive)

    nsem = 6 + 6 * npk
    res = pl.pallas_call(
        body, name=name, in_specs=[ANY] * (npk + 1), out_specs=[ANY] * (npk + 1),
        out_shape=[SDS((N_CHIPS,) + meta.shape, meta.dtype)] + [SDS((N_CHIPS,) + a.shape[1:], a.dtype) for a in arrs],
        scratch_shapes=[pltpu.SemaphoreType.DMA((nsem,)), pltpu.SemaphoreType.DMA((nsem,))],
        compiler_params=pltpu.CompilerParams(has_side_effects=True))(meta, *arrs)
    return _fill_own(res[:1], [meta])[0], _fill_own(res[1:], [a[0] for a in arrs])


def _pair_exchange(gs, name):
    npk = len(gs)

    def body(*refs):
        ins, outs = refs[:npk], refs[npk:2 * npk]
        send_sems, recv_sems = refs[2 * npk:]
        x, y, c = _me()
        cps = [_remote(ins[p].at[:, 1 - c], outs[p], send_sems, recv_sems, p, (x, y, 1 - c)) for p in range(npk)]
        for cp in cps:
            cp.start()
        for cp in cps:
            cp.wait()

    return pl.pallas_call(
        body, name=name, in_specs=[ANY] * npk, out_specs=[ANY] * npk,
        out_shape=[SDS((N_CHIPS,) + g.shape[2:], g.dtype) for g in gs],
        scratch_shapes=[pltpu.SemaphoreType.DMA((npk,)), pltpu.SemaphoreType.DMA((npk,))],
        compiler_params=pltpu.CompilerParams(has_side_effects=True))(*gs)


def _chip_exchange(ss, small, name):
    npk = len(ss)

    def body(*refs):
        ins, sm_ref = refs[:npk], refs[npk]
        outs, sa_ref = refs[npk + 1:2 * npk + 1], refs[2 * npk + 1]
        send_sems, recv_sems, loc_sem = refs[2 * npk + 2:]
        x, y, c = _me()
        me = 4 * x + 2 * y + c
        lc = pltpu.make_async_copy(sm_ref, sa_ref.at[me], loc_sem.at[0])
        lc.start()
        cps = []
        for p in range(npk):
            for r, (cx, cy) in enumerate([(1 - x, y), (x, 1 - y), (1 - x, 1 - y)]):
                cp = _remote(ins[p].at[2 * cx + cy], outs[p].at[r], send_sems, recv_sems, 3 * p + r, (cx, cy, c))
                cp.start()
                cps.append(cp)
        base = 3 * npk - 1
        for mask in range(1, N_DEV):
            px, py, pc = x ^ (mask >> 2), y ^ ((mask >> 1) & 1), c ^ (mask & 1)
            cp = _remote(sm_ref, sa_ref.at[me], send_sems, recv_sems, base + mask, (px, py, pc))
            cp.start()
            cps.append(cp)
        for p in range(npk):
            for r in range(3):
                _remote(outs[p].at[r], outs[p].at[r], send_sems, recv_sems, 3 * p + r, (x, y, c)).wait_recv()
        for mask in range(1, N_DEV):
            src = 4 * (x ^ (mask >> 2)) + 2 * (y ^ ((mask >> 1) & 1)) + (c ^ (mask & 1))
            _remote(sa_ref.at[src], sa_ref.at[src], send_sems, recv_sems, base + mask, (x, y, c)).wait_recv()
        for cp in cps:
            cp.wait_send()
        lc.wait()

    nsem = 3 * npk + N_DEV - 1
    res = pl.pallas_call(
        body, name=name, in_specs=[ANY] * (npk + 1), out_specs=[ANY] * (npk + 1),
        out_shape=[SDS((3,) + s.shape[1:], s.dtype) for s in ss] + [SDS((N_DEV,) + small.shape, small.dtype)],
        scratch_shapes=[pltpu.SemaphoreType.DMA((nsem,)), pltpu.SemaphoreType.DMA((nsem,)),
                        pltpu.SemaphoreType.DMA((1,))],
        compiler_params=pltpu.CompilerParams(has_side_effects=True))(*ss, small)
    return res[:npk], res[npk]


def _half_exchange(ghs, name):
    npk = len(ghs)

    def body(*refs):
        ins, outs = refs[:npk], refs[npk:2 * npk]
        send_sems, recv_sems = refs[2 * npk:]
        x, y, c = _me()
        cps = [_remote(ins[p], outs[p], send_sems, recv_sems, p, (x, y, 1 - c)) for p in range(npk)]
        for cp in cps:
            cp.start()
        for cp in cps:
            cp.wait()

    return pl.pallas_call(
        body, name=name, in_specs=[ANY] * npk, out_specs=[ANY] * npk, out_shape=[SDS(g.shape, g.dtype) for g in ghs],
        scratch_shapes=[pltpu.SemaphoreType.DMA((npk,)), pltpu.SemaphoreType.DMA((npk,))],
        compiler_params=pltpu.CompilerParams(has_side_effects=True))(*ghs)


def _add_tile(rows, cols):
    return _div_tile(rows, max(16, (1 << 19) // max(cols, LANES)), 16)


def _pair_add(g, r1, c_idx, name):
    _, rows, cols = r1.shape
    tr = _add_tile(rows, cols)

    def body(c_ref, g_ref, r_ref, o_ref, ob_ref):
        s = g_ref[0] + r_ref[...]
        o_ref[...] = s
        ob_ref[...] = s.astype(BF16)

    own = BS((1, tr, cols), lambda k, i, c: (k, i, 0))
    return pl.pallas_call(
        body, name=name,
        grid_spec=pltpu.PrefetchScalarGridSpec(
            num_scalar_prefetch=1, grid=(N_CHIPS, rows // tr),
            in_specs=[BS((1, 1, tr, cols), lambda k, i, c: (k, c[0], i, 0)), own], out_specs=[own, own]),
        out_shape=[SDS(r1.shape, F32), SDS(r1.shape, BF16)],
        compiler_params=_cparams(("parallel", "parallel")))(c_idx, g, r1)


def _chip_add(s1, r2, j_idx, name):
    _, rows, cols = s1.shape
    tr = _add_tile(rows, cols)

    def body(j_ref, s_ref, r_ref, o_ref):
        o_ref[...] = ((s_ref[0] + r_ref[0].astype(F32)) + r_ref[1].astype(F32)) + r_ref[2].astype(F32)

    return pl.pallas_call(
        body, name=name,
        grid_spec=pltpu.PrefetchScalarGridSpec(
            num_scalar_prefetch=1, grid=(rows // tr,),
            in_specs=[BS((1, tr, cols), lambda i, j: (j[0], i, 0)), BS((3, tr, cols), lambda i, j: (0, i, 0))],
            out_specs=BS((tr, cols), lambda i, j: (i, 0))),
        out_shape=SDS((rows, cols), F32), compiler_params=_cparams(("parallel",)))(j_idx, s1, r2)


def _adamw_math(w, g, m, v):
    m = ADAM_B1 * m + (1.0 - ADAM_B1) * g
    v = ADAM_B2 * v + (1.0 - ADAM_B2) * (g * g)
    m_hat = m / (1.0 - ADAM_B1 ** ADAM_STEP)
    v_hat = v / (1.0 - ADAM_B2 ** ADAM_STEP)
    delta = -ADAM_LR * (m_hat / (jnp.sqrt(v_hat) + ADAM_EPS) + ADAM_WD * w)
    return delta, m, v


def _adamw(w, gh, go, m, v, c_idx, name):
    _, rows, cols = w.shape
    tr = _add_tile(rows, cols)

    def body(c_ref, w_ref, gh_ref, go_ref, m_ref, v_ref, g_out, d_out, m_out, v_out):
        g = jnp.where(pl.program_id(0) == c_ref[0], gh_ref[...], go_ref[...])
        g_out[0] = g
        d_out[0], m_out[0], v_out[0] = _adamw_math(w_ref[0], g, m_ref[0], v_ref[0])

    full = BS((1, tr, cols), lambda hf, i, c: (hf, i, 0))
    half = BS((tr, cols), lambda hf, i, c: (i, 0))
    return pl.pallas_call(
        body, name=name,
        grid_spec=pltpu.PrefetchScalarGridSpec(
            num_scalar_prefetch=1, grid=(2, rows // tr), in_specs=[full, half, half, full, full],
            out_specs=[full] * 4),
        out_shape=[SDS(w.shape, F32)] * 4, compiler_params=_cparams(("parallel", "parallel")))(c_idx, w, gh, go, m, v)


def _sum_devices(sa, name):
    def body(sa_ref, g_out):
        g = sa_ref[0]
        for d in range(1, N_DEV):
            g = g + sa_ref[d]
        g_out[...] = g

    return pl.pallas_call(body, name=name, out_shape=SDS(sa.shape[1:], F32),
                          compiler_params=pltpu.CompilerParams(vmem_limit_bytes=VMEM_LIMIT))(sa)


def _adamw_small(ws, gs, ms, vs, name):
    k = len(ws)

    def body(*refs):
        ins, outs = refs[:4 * k], refs[4 * k:]
        for i in range(k):
            d, m, v = _adamw_math(ins[i][...], ins[k + i][...], ins[2 * k + i][...], ins[3 * k + i][...])
            outs[i][...], outs[k + i][...], outs[2 * k + i][...] = d, m, v

    return pl.pallas_call(body, name=name, out_shape=[SDS(w.shape, F32) for w in ws] * 3,
                          compiler_params=pltpu.CompilerParams(vmem_limit_bytes=VMEM_LIMIT))(*ws, *gs, *ms, *vs)


HALVED = {"w_in": (2, 1024, 1450), "mla_w_q_up": (2, 256, 192), "mla_w_kv_up": (2, 128, 256),
          "w_branch": (2, 1536, 256), "w_o": (2, 256, 1024), "ffn_w_up": (2, 1024, 1408),
          "ffn_w_down": (2, 704, 1024), "ffn_conv_w": (2, 3, 1408), "meta_tokens": (2, 8, 256)}
SMALL_SHAPE = {"norm1_g": (2, 1024), "fox_forget_b": (2, 8), "fox_q_g": (2, 64), "fox_k_g": (2, 64),
               "mla_q_a_g": (2, 256), "mla_kv_a_g": (2, 128), "mla_q_g": (2, 96), "mla_k_g": (2, 96),
               "swa_q_g": (2, 64), "swa_k_g": (2, 64), "swa_sinks": (2, 8), "norm2_g": (2, 1024),
               "ffn_conv_b": (2, 5632)}
SMALL_ROWS = _pack_rows([SMALL_SHAPE[k] for k in SMALL] + [(1,)], SUBLANES)


def kernel(x, meta_tokens, norm1_g, w_in, fox_forget_b, fox_q_g, fox_k_g, mla_q_a_g, mla_w_q_up, mla_kv_a_g, mla_w_kv_up, mla_q_g, mla_k_g, swa_q_g, swa_k_g, swa_sinks, w_branch, w_o, norm2_g, ffn_w_up, ffn_conv_w, ffn_conv_b, ffn_w_down, loss_target, m_meta_tokens, m_norm1_g, m_w_in, m_fox_forget_b, m_fox_q_g, m_fox_k_g, m_mla_q_a_g, m_mla_w_q_up, m_mla_kv_a_g, m_mla_w_kv_up, m_mla_q_g, m_mla_k_g, m_swa_q_g, m_swa_k_g, m_swa_sinks, m_w_branch, m_w_o, m_norm2_g, m_ffn_w_up, m_ffn_conv_w, m_ffn_conv_b, m_ffn_w_down, v_meta_tokens, v_norm1_g, v_w_in, v_fox_forget_b, v_fox_q_g, v_fox_k_g, v_mla_q_a_g, v_mla_w_q_up, v_mla_kv_a_g, v_mla_w_kv_up, v_mla_q_g, v_mla_k_g, v_swa_q_g, v_swa_k_g, v_swa_sinks, v_w_branch, v_w_o, v_norm2_g, v_ffn_w_up, v_ffn_conv_w, v_ffn_conv_b, v_ffn_w_down):
    w = dict(meta_tokens=meta_tokens, norm1_g=norm1_g, w_in=w_in, fox_forget_b=fox_forget_b, fox_q_g=fox_q_g,
             fox_k_g=fox_k_g, mla_q_a_g=mla_q_a_g, mla_w_q_up=mla_w_q_up, mla_kv_a_g=mla_kv_a_g,
             mla_w_kv_up=mla_w_kv_up, mla_q_g=mla_q_g, mla_k_g=mla_k_g, swa_q_g=swa_q_g, swa_k_g=swa_k_g,
             swa_sinks=swa_sinks, w_branch=w_branch, w_o=w_o, norm2_g=norm2_g, ffn_w_up=ffn_w_up,
             ffn_conv_w=ffn_conv_w, ffn_conv_b=ffn_conv_b, ffn_w_down=ffn_w_down)
    m = dict(meta_tokens=m_meta_tokens, norm1_g=m_norm1_g, w_in=m_w_in, fox_forget_b=m_fox_forget_b,
             fox_q_g=m_fox_q_g, fox_k_g=m_fox_k_g, mla_q_a_g=m_mla_q_a_g, mla_w_q_up=m_mla_w_q_up,
             mla_kv_a_g=m_mla_kv_a_g, mla_w_kv_up=m_mla_w_kv_up, mla_q_g=m_mla_q_g, mla_k_g=m_mla_k_g,
             swa_q_g=m_swa_q_g, swa_k_g=m_swa_k_g, swa_sinks=m_swa_sinks, w_branch=m_w_branch, w_o=m_w_o,
             norm2_g=m_norm2_g, ffn_w_up=m_ffn_w_up, ffn_conv_w=m_ffn_conv_w, ffn_conv_b=m_ffn_conv_b,
             ffn_w_down=m_ffn_w_down)
    v = dict(meta_tokens=v_meta_tokens, norm1_g=v_norm1_g, w_in=v_w_in, fox_forget_b=v_fox_forget_b,
             fox_q_g=v_fox_q_g, fox_k_g=v_fox_k_g, mla_q_a_g=v_mla_q_a_g, mla_w_q_up=v_mla_w_q_up,
             mla_kv_a_g=v_mla_kv_a_g, mla_w_kv_up=v_mla_w_kv_up, mla_q_g=v_mla_q_g, mla_k_g=v_mla_k_g,
             swa_q_g=v_swa_q_g, swa_k_g=v_swa_k_g, swa_sinks=v_swa_sinks, w_branch=v_w_branch, w_o=v_w_o,
             norm2_g=v_norm2_g, ffn_w_up=v_ffn_w_up, ffn_conv_w=v_ffn_conv_w, ffn_conv_b=v_ffn_conv_b,
             ffn_w_down=v_ffn_w_down)
    xi, yi, ci = _me()
    c_idx = ci.astype(jnp.int32).reshape(1)
    j_idx = (2 * xi + yi).astype(jnp.int32).reshape(1)

    sh_names = BIG + FINE
    local = {k: (w[k].astype(BF16) if k in BIG else w[k]).reshape(HALVED[k]) for k in sh_names}
    late = [local[k] for k in LAYERED]
    meta_g, early = _gather_early(local["meta_tokens"], late, "gather_early")
    meta = jnp.concatenate([meta_g[i].reshape(SHARD_SHAPE["meta_tokens"]) for i in range(N_CHIPS)], axis=1)
    w0 = _assemble_layer(_layer_parts(early))
    small = {k: w[k] for k in SMALL}

    loss, g_x, g_eps, g_cw, g_meta, g_small = _local_step(x[0], loss_target[0], w0, late, meta, small)

    g_sh = dict(g_eps)
    for k, g in (("meta_tokens", g_meta), ("ffn_conv_w", g_cw)):
        g_sh[k] = jnp.stack(jnp.split(g, N_CHIPS, axis=SHARD_AXIS[k]))
    gs =[g_sh[k].reshape((N_CHIPS,) + HALVED[k]) for k in sh_names]
    spack = _pack([g_small[k] for k in SMALL] + [loss.reshape(1)], SMALL_ROWS, F32)
    r1 = _pair_exchange(gs, "grads_pair_exchange")
    s1 = [_pair_add(g, r, c_idx, "grads_pair_add_" + k) for g, r, k in zip(gs, r1, sh_names)]
    r2, sa = _chip_exchange([s[1] for s in s1], spack, "grads_chip_exchange")
    gh = [_chip_add(s[0], r, j_idx, "grads_chip_add_" + k) for s, r, k in zip(s1, r2, sh_names)]
    go = _half_exchange(gh, "grads_half_exchange")

    grads, deltas, new_m, new_v = {}, {}, {}, {}
    for k, a, b in zip(sh_names, gh, go):
        outs = _adamw(w[k].reshape(HALVED[k]), a, b, m[k].reshape(HALVED[k]), v[k].reshape(HALVED[k]), c_idx,
                      "adamw_" + k)
        for dst, o in zip((grads, deltas, new_m, new_v), outs):
            dst[k] = o.reshape(SHARD_SHAPE[k])
    sm_shapes = [SMALL_SHAPE[k] for k in SMALL] + [(1,)]
    g_sum = _unpack(_sum_devices(sa, "sum_small"), sm_shapes)
    res = _adamw_small([w[k] for k in SMALL], g_sum[:-1], [m[k] for k in SMALL], [v[k] for k in SMALL], "adamw_small")
    ns = len(SMALL)
    grads.update(zip(SMALL, g_sum[:-1]))
    for dst, vals in zip((deltas, new_m, new_v), (res[:ns], res[ns:2 * ns], res[2 * ns:])):
        dst.update(zip(SMALL, vals))
    total_loss = g_sum[-1][0]
    return (total_loss, g_x[None], *[grads[k] for k in WEIGHTS], *[deltas[k] for k in WEIGHTS],
            *[new_m[k] for k in WEIGHTS], *[new_v[k] for k in WEIGHTS])
```

```python
import functools

import jax
import jax.numpy as jnp
from jax import lax
from jax.experimental import pallas as pl
from jax.experimental.pallas import tpu as pltpu

F32 = jnp.float32
BF16 = jnp.bfloat16
SDS = jax.ShapeDtypeStruct
BS = pl.BlockSpec

D_MODEL = 1024
DEPTH = 2
N_META = 16
EPS = 1e-6
HEADS = 8
HEAD_DIM = 64
MLA_Q_RANK = 256
MLA_KV_RANK = 128
MLA_NOPE = 64
MLA_ROPE = 32
MLA_QK = MLA_NOPE + MLA_ROPE
ROPE_THETA = 10000.0
SWA_KV_HEADS = 2
WINDOW = 128
N_BRANCH = 3
BRANCH_WIDTH = 512
D_FF = 2816
IN_WIDTH = 5800
IN_PAD = 6144
N_CHIPS = 4
N_DEV = 8

ADAM_LR = 0.001
ADAM_B1 = 0.9
ADAM_B2 = 0.999
ADAM_EPS = 1e-08
ADAM_WD = 0.01
ADAM_STEP = 10

LANES = 128
SUBLANES = 8
ROW_PAD = 128
CAUSAL_TILE = 384
NEG = -1e30
VMEM_LIMIT = 56 * 1024 * 1024

O_FQ, O_FK, O_FV, O_FF = 0, 512, 1024, 1536
O_CQ, O_CKV, O_KR = 1664, 1920, 2048
O_SQ, O_SK, O_SV, O_G = 2176, 2688, 2816, 2944

SHARDED = ("meta_tokens", "w_in", "mla_w_q_up", "mla_w_kv_up", "w_branch", "w_o", "ffn_w_up", "ffn_conv_w",
           "ffn_w_down")
SHARD_AXIS = {"meta_tokens": 1, "w_in": 2, "mla_w_q_up": 2, "mla_w_kv_up": 2, "w_branch": 3, "w_o": 1,
              "ffn_w_up": 2, "ffn_conv_w": 2, "ffn_w_down": 1}
SHARD_SHAPE = {"meta_tokens": (16, 256), "w_in": (2, 1024, 1450), "mla_w_q_up": (2, 256, 192),
               "mla_w_kv_up": (2, 128, 256), "w_branch": (2, 3, 512, 256), "w_o": (2, 256, 1024),
               "ffn_w_up": (2, 1024, 1408), "ffn_conv_w": (2, 3, 1408), "ffn_w_down": (2, 704, 1024)}
BIG = ("w_in", "mla_w_q_up", "mla_w_kv_up", "w_branch", "w_o", "ffn_w_up", "ffn_w_down")
FINE = ("meta_tokens", "ffn_conv_w")
SMALL = ("norm1_g", "fox_forget_b", "fox_q_g", "fox_k_g", "mla_q_a_g", "mla_kv_a_g", "mla_q_g", "mla_k_g",
         "swa_q_g", "swa_k_g", "swa_sinks", "norm2_g", "ffn_conv_b")
WEIGHTS = ("meta_tokens", "norm1_g", "w_in", "fox_forget_b", "fox_q_g", "fox_k_g", "mla_q_a_g", "mla_w_q_up",
           "mla_kv_a_g", "mla_w_kv_up", "mla_q_g", "mla_k_g", "swa_q_g", "swa_k_g", "swa_sinks", "w_branch", "w_o",
           "norm2_g", "ffn_w_up", "ffn_conv_w", "ffn_conv_b", "ffn_w_down")


def _cparams(sem):
    return pltpu.CompilerParams(dimension_semantics=sem, vmem_limit_bytes=VMEM_LIMIT)


def _div_tile(n, cap, mult=SUBLANES):
    best = None
    for t in range(mult, min(n, cap) + 1, mult):
        if n % t == 0:
            best = t
    return best if best is not None else n


def _rows_tile(n, width, budget=2 << 20):
    return _div_tile(n, max(SUBLANES, budget // (4 * max(width, LANES))))


def _op(fwd, bwd):
    @jax.custom_vjp
    def op(*args):
        return fwd(*args)[0]
    op.defvjp(fwd, bwd)
    return op


def _rotate(v, cos, sin):
    lane = lax.broadcasted_iota(jnp.int32, v.shape, 1)
    rot = jnp.where(lane < MLA_NOPE + MLA_ROPE // 2, -pltpu.roll(v, LANES - MLA_ROPE // 2, 1),
                    pltpu.roll(v, MLA_ROPE // 2, 1))
    return v * cos + rot * sin


def _rms_fwd_call(x, g, denom, name, rot=None):
    n, c = x.shape
    tr = _rows_tile(n if rot is None else rot[0].shape[0], c)
    nt = None if rot is None else rot[0].shape[0] // tr

    def body(x_ref, g_ref, *rest):
        y_ref = rest[-1]
        xv = x_ref[...]
        ms = jnp.sum(xv * xv, axis=-1, keepdims=True) * (1.0 / denom)
        y = xv * lax.rsqrt(ms + EPS) * g_ref[...]
        y_ref[...] = y if rot is None else _rotate(y, rest[0][...], rest[1][...])

    ins, args = [BS((tr, c), lambda i: (i, 0)), BS((1, c), lambda i: (0, 0))], [x, g]
    if rot is not None:
        ins += [BS((tr, c), lambda i: (i % nt, 0))] * 2
        args += list(rot)
    return pl.pallas_call(
        body, name=name, grid=(n // tr,), in_specs=ins,
        out_specs=BS((tr, c), lambda i: (i, 0)), out_shape=SDS((n, c), F32),
        compiler_params=_cparams(("parallel",)))(*args)


def _rms_bwd_call(x, g, dy, denom, name, rot=None):
    n, c = x.shape
    tr = _rows_tile(n if rot is None else rot[0].shape[0], c)
    nt = None if rot is None else rot[0].shape[0] // tr

    def body(x_ref, g_ref, dy_ref, *rest):
        dx_ref, dg_ref = rest[-2:]
        xv = x_ref[...]
        dy = dy_ref[...]
        if rot is not None:
            dy = _rotate(dy, rest[0][...], -rest[1][...])
        ms = jnp.sum(xv * xv, axis=-1, keepdims=True) * (1.0 / denom)
        r = lax.rsqrt(ms + EPS)
        xh = xv * r
        dxh = dy * g_ref[...]
        dx_ref[...] = r * (dxh - xh * (jnp.sum(dxh * xh, axis=-1, keepdims=True) * (1.0 / denom)))

        @pl.when(pl.program_id(0) == 0)
        def _():
            dg_ref[...] = jnp.zeros_like(dg_ref)

        dg_ref[...] += jnp.sum(dy * xh, axis=0, keepdims=True)

    ins = [BS((tr, c), lambda i: (i, 0)), BS((1, c), lambda i: (0, 0)), BS((tr, c), lambda i: (i, 0))]
    args = [x, g, dy]
    if rot is not None:
        ins += [BS((tr, c), lambda i: (i % nt, 0))] * 2
        args += list(rot)
    return pl.pallas_call(
        body, name=name, grid=(n // tr,), in_specs=ins,
        out_specs=[BS((tr, c), lambda i: (i, 0)), BS((1, c), lambda i: (0, 0))],
        out_shape=[SDS((n, c), F32), SDS((1, c), F32)],
        compiler_params=_cparams(("arbitrary",)))(*args)


def rms_norm(x, g, denom, name):
    def fwd(x, g):
        return _rms_fwd_call(x, g, denom, name + "_f"), (x, g)

    def bwd(res, dy):
        return tuple(_rms_bwd_call(res[0], res[1], dy, denom, name + "_b"))

    return _op(fwd, bwd)(x, g)


def rms_norm_rope(x, g, cos, sin, denom, name):
    def fwd(x, g, cos, sin):
        return _rms_fwd_call(x, g, denom, name + "_f", (cos, sin)), (x, g, cos, sin)

    def bwd(res, dy):
        x, g, cos, sin = res
        dx, dg = _rms_bwd_call(x, g, dy, denom, name + "_b", (cos, sin))
        return dx, dg, jnp.zeros_like(cos), jnp.zeros_like(sin)

    return _op(fwd, bwd)(x, g, cos, sin)


def _mm_call(a, b, mode, res, name):
    if mode == "nn":
        (m, kc), n = a.shape, b.shape[1]
    elif mode == "nt":
        (m, kc), n = a.shape, b.shape[0]
    else:
        (kc, m), n = a.shape, b.shape[1]
    if mode == "tn":
        tk = _div_tile(kc, 528)
        tm = _div_tile(m, 1408, LANES)
        tn = _div_tile(n, 2048, LANES)
    else:
        tk = kc if kc <= 2816 else _div_tile(kc, 1024, LANES)
        tm = _div_tile(m, max(LANES, (9 << 19) // (4 * tk)))
        tn = _div_tile(n, 1408 if mode == "nt" else 512, LANES)
    nk = kc // tk
    dims = {"nn": (((1,), (0,)), ((), ())), "nt": (((1,), (1,)), ((), ())), "tn": (((0,), (0,)), ((), ()))}[mode]

    def body(*refs):
        if res is None:
            a_ref, b_ref, o_ref, acc_ref = refs
            r_ref = None
        else:
            a_ref, b_ref, r_ref, o_ref, acc_ref = refs
        k = pl.program_id(2)

        @pl.when(k == 0)
        def _():
            acc_ref[...] = jnp.zeros_like(acc_ref)

        acc_ref[...] += lax.dot_general(a_ref[...].astype(BF16), b_ref[...].astype(BF16), dims,
                                        preferred_element_type=F32)

        @pl.when(k == nk - 1)
        def _():
            if r_ref is None:
                o_ref[...] = acc_ref[...]
            else:
                o_ref[...] = r_ref[...] + acc_ref[...]

    a_spec = BS((tk, tm), lambda i, j, k: (k, i)) if mode == "tn" else BS((tm, tk), lambda i, j, k: (i, k))
    b_spec = BS((tn, tk), lambda i, j, k: (j, k)) if mode == "nt" else BS((tk, tn), lambda i, j, k: (k, j))
    o_spec = BS((tm, tn), lambda i, j, k: (i, j))
    ins, args = [a_spec, b_spec], [a, b]
    if res is not None:
        ins.append(o_spec)
        args.append(res)
    return pl.pallas_call(
        body, name=name, grid=(m // tm, n // tn, nk), in_specs=ins, out_specs=o_spec,
        out_shape=SDS((m, n), F32), scratch_shapes=[pltpu.VMEM((tm, tn), F32)],
        compiler_params=_cparams(("parallel", "parallel", "arbitrary")))(*args)


def linear(a, w, eps, name, res=None):
    if res is None:
        def fwd(a, w, eps):
            return _mm_call(a, w, "nn", None, name + "_f"), (a, w)

        def bwd(r, dc):
            a, w = r
            return (_mm_call(dc, w, "nt", None, name + "_da"), jnp.zeros_like(w),
                    _mm_call(a, dc, "tn", None, name + "_dw"))

        return _op(fwd, bwd)(a, w, eps)

    def fwd_r(a, w, eps, res):
        return _mm_call(a, w, "nn", res, name + "_f"), (a, w)

    def bwd_r(r, dc):
        a, w = r
        return (_mm_call(dc, w, "nt", None, name + "_da"), jnp.zeros_like(w),
                _mm_call(a, dc, "tn", None, name + "_dw"), dc)

    return _op(fwd_r, bwd_r)(a, w, eps, res)


CT = 128


def _tri_dot(v, upper):
    r = lax.broadcasted_iota(jnp.int32, (CT, CT), 0)
    c = lax.broadcasted_iota(jnp.int32, (CT, CT), 1)
    tri = jnp.where((r <= c) if upper else (r >= c), 1.0, 0.0).astype(F32)
    return jnp.dot(v, tri, preferred_element_type=F32, precision=lax.Precision.HIGHEST)


def _gate_fwd_call(z, b, name):
    h, n = z.shape

    def body(z_ref, b_ref, c_ref, carry):
        @pl.when(pl.program_id(0) == 0)
        def _():
            carry[...] = jnp.zeros_like(carry)

        x = z_ref[...] + b_ref[...]
        ls = jnp.minimum(x, 0.0) - jnp.log(1.0 + jnp.exp(-jnp.abs(x)))
        c_ref[...] = _tri_dot(ls, True) + carry[...]
        carry[...] += jnp.sum(ls, axis=1, keepdims=True)

    return pl.pallas_call(
        body, name=name, grid=(n // CT,),
        in_specs=[BS((h, CT), lambda j: (0, j)), BS((h, 1), lambda j: (0, 0))],
        out_specs=BS((h, CT), lambda j: (0, j)), out_shape=SDS((h, n), F32),
        scratch_shapes=[pltpu.VMEM((h, 1), F32)],
        compiler_params=_cparams(("arbitrary",)))(z, b)


def _gate_bwd_call(z, b, dc, name):
    h, n = z.shape
    nt = n // CT

    def body(z_ref, b_ref, dc_ref, dz_ref, db_ref, carry):
        @pl.when(pl.program_id(0) == 0)
        def _():
            carry[...] = jnp.zeros_like(carry)
            db_ref[...] = jnp.zeros_like(db_ref)

        dcv = dc_ref[...]
        dls = _tri_dot(dcv, False) + carry[...]
        carry[...] += jnp.sum(dcv, axis=1, keepdims=True)
        x = z_ref[...] + b_ref[...]
        e = jnp.exp(-jnp.abs(x))
        dz = dls * jnp.where(x >= 0, e / (1.0 + e), 1.0 / (1.0 + e))
        dz_ref[...] = dz
        db_ref[...] += jnp.sum(dz, axis=1, keepdims=True)

    rev = lambda j: (0, nt - 1 - j)
    return pl.pallas_call(
        body, name=name, grid=(nt,),
        in_specs=[BS((h, CT), rev), BS((h, 1), lambda j: (0, 0)), BS((h, CT), rev)],
        out_specs=[BS((h, CT), rev), BS((h, 1), lambda j: (0, 0))],
        out_shape=[SDS((h, n), F32), SDS((h, 1), F32)],
        scratch_shapes=[pltpu.VMEM((h, 1), F32)],
        compiler_params=_cparams(("arbitrary",)))(z, b, dc)


def forget_cumsum(z, b, name):
    def fwd(z, b):
        return _gate_fwd_call(z, b, name + "_f"), (z, b)

    def bwd(res, dc):
        return tuple(_gate_bwd_call(res[0], res[1], dc, name + "_b"))

    return _op(fwd, bwd)(z, b)


NT_DIMS = (((1,), (1,)), ((), ()))
TN_DIMS = (((0,), (0,)), ((), ()))
HEADS_PER_STEP = 2


def _causal_tile(n):
    return CAUSAL_TILE if n % CAUSAL_TILE == 0 else ROW_PAD


def _causal_fwd_call(q, k, v, ck_r, fox, scale, name, late=None):
    h, n, dk = q.shape
    dv = v.shape[2]
    t = _causal_tile(n)
    nq = n // t
    hb = HEADS_PER_STEP
    nl = 0 if late is None else len(late)
    n_in = 3 + int(fox) + nl

    def body(*refs):
        q_ref, k_ref, v_ref = refs[:3]
        ck_ref = refs[3] if fox else None
        o_ref, lse_ref = refs[n_in:n_in + 2]
        m_scr, l_scr, acc_scr = refs[n_in + 2 + nl:n_in + 5 + nl]
        qi = pl.program_id(1)
        if nl:
            start, forward, drain, receive = _layer_gather(refs[n_in - nl:n_in], refs[n_in + 2:n_in + 2 + nl],
                                                           refs[-2], refs[-1], 1, 0)
            hp, core = pl.program_id(0), lax.axis_index("c")
            last = (hp == h // hb - 1) & (qi == nq - 1)
            pl.when((hp == 0) & (qi == 0) & (core == 1))(start)
            pl.when((hp == h // hb // 2) & (qi == 0) & (core == 1))(forward)
            pl.when(last & (core == 1))(drain)
            pl.when(last & (core == 0))(receive)
        qbs = [q_ref[e].astype(BF16) for e in range(hb)]
        m_scr[...] = jnp.full(m_scr.shape, NEG, F32)
        l_scr[...] = jnp.zeros_like(l_scr)
        acc_scr[...] = jnp.zeros_like(acc_scr)

        def process(j, masked):
            off = pl.multiple_of(j * t, t)
            if masked:
                rows = lax.broadcasted_iota(jnp.int32, (t, t), 0)
                cols = lax.broadcasted_iota(jnp.int32, (t, t), 1)
                valid = cols <= rows
            for e in range(hb):
                kb = k_ref[e, pl.ds(off, t), :].astype(BF16)
                vb = v_ref[e, pl.ds(off, t), :].astype(BF16)
                s = lax.dot_general(qbs[e], kb, NT_DIMS, preferred_element_type=F32) * scale
                if fox:
                    s = s - ck_ref[e, j]
                if masked:
                    s = jnp.where(valid, s, NEG)
                m_old = m_scr[e]
                m_new = jnp.maximum(m_old, jnp.max(s, axis=1, keepdims=True))
                alpha = jnp.exp(m_old - m_new)
                p = jnp.exp(s - jnp.tile(m_new, (1, t // LANES)))
                l_scr[e] = alpha * l_scr[e] + jnp.sum(p, axis=1, keepdims=True)
                acc_scr[e] = alpha[:, :dv] * acc_scr[e] + jnp.dot(p.astype(BF16), vb, preferred_element_type=F32)
                m_scr[e] = m_new

        def step(j, carry):
            process(j, False)
            return carry

        lax.fori_loop(0, qi, step, 0)
        process(qi, True)
        for e in range(hb):
            l = l_scr[e]
            o_ref[e] = acc_scr[e] / l[:, :dv]
            lse_ref[e, 0] = jnp.transpose(m_scr[e] + jnp.log(l))[0:1, :]

    ins = [BS((hb, t, dk), lambda a, b: (a, b, 0)), BS((hb, n, dk), lambda a, b: (a, 0, 0)),
           BS((hb, n, dv), lambda a, b: (a, 0, 0))]
    args = [q, k, v]
    if fox:
        ins.append(BS((hb, nq, 1, t), lambda a, b: (a, 0, 0, 0)))
        args.append(ck_r)
    outs = [BS((hb, t, dv), lambda a, b: (a, b, 0)), BS((hb, 1, 1, t), lambda a, b: (a, b, 0, 0))]
    oshape = [SDS((h, n, dv), F32), SDS((h, nq, 1, t), F32)]
    scratch = [pltpu.VMEM((hb, t, LANES), F32), pltpu.VMEM((hb, t, LANES), F32), pltpu.VMEM((hb, t, dv), F32)]
    if nl:
        ins += [ANY] * nl
        args += list(late)
        outs += [ANY] * nl
        oshape += [SDS((N_CHIPS,) + a.shape[1:], a.dtype) for a in late]
        scratch += [pltpu.SemaphoreType.DMA((6 * nl,)), pltpu.SemaphoreType.DMA((6 * nl,))]
    res = pl.pallas_call(
        body, name=name, grid=(h // hb, nq), in_specs=ins, out_specs=outs, out_shape=oshape, scratch_shapes=scratch,
        compiler_params=pltpu.CompilerParams(dimension_semantics=("arbitrary", "arbitrary"),
                                             vmem_limit_bytes=VMEM_LIMIT, has_side_effects=bool(nl)))(*args)
    return res[0], res[1], list(res[2:])


def _causal_bwd_call(q, k, v, do, o, lse_r, ck_r, fox, scale, name, side=None):
    h, n, dk = q.shape
    dv = v.shape[2]
    t = _causal_tile(n)
    nq = n // t
    hb = HEADS_PER_STEP
    ns = 0 if side is None else len(side)

    def body(*refs):
        it = iter(refs)
        q_ref, k_ref, v_ref, do_ref, o_ref, lse_ref = (next(it) for _ in range(6))
        ck_ref = next(it) if fox else None
        side_in = [next(it) for _ in range(ns)]
        dq_ref, dk_ref, dv_ref = next(it), next(it), next(it)
        dck_ref, dcq_ref = (next(it), next(it)) if fox else (None, None)
        side_out = [next(it) for _ in range(ns)]
        delta_scr, dk_scr, dv_scr = next(it), next(it), next(it)
        dck_scr = next(it) if fox else None
        kj = pl.program_id(1)
        if ns:
            send_sems, recv_sems = next(it), next(it)
            x, y, core = _me()
            chips = [(1 - x, y), (x, 1 - y), (1 - x, 1 - y)]
            copies = [_remote(side_in[p].at[2 * cx + cy], side_out[p].at[r], send_sems, recv_sems, 3 * p + r,
                              (cx, cy, core)) for p in range(ns) for r, (cx, cy) in enumerate(chips)]

            @pl.when((pl.program_id(0) == 0) & (kj == 0))
            def _():
                for cp in copies:
                    cp.start()

            @pl.when((pl.program_id(0) == h // hb - 1) & (kj == nq - 1))
            def _():
                for cp in copies:
                    cp.wait()

        @pl.when(kj == 0)
        def _():
            dq_ref[...] = jnp.zeros_like(dq_ref)
            if fox:
                dcq_ref[...] = jnp.zeros_like(dcq_ref)
            ones = jnp.ones((SUBLANES, dv), F32)

            def fill(qi, carry):
                off = pl.multiple_of(qi * t, t)
                for e in range(hb):
                    prod = do_ref[e, pl.ds(off, t), :] * o_ref[e, pl.ds(off, t), :]
                    delta_scr[e, qi] = lax.dot_general(ones, prod, NT_DIMS, preferred_element_type=F32,
                                                       precision=lax.Precision.HIGHEST)[0:1, :]
                return carry

            lax.fori_loop(0, nq, fill, 0)

        kbs = [k_ref[e].astype(BF16) for e in range(hb)]
        vbs = [v_ref[e].astype(BF16) for e in range(hb)]
        dk_scr[...] = jnp.zeros_like(dk_scr)
        dv_scr[...] = jnp.zeros_like(dv_scr)
        if fox:
            dck_scr[...] = jnp.zeros_like(dck_scr)
            ckcs = [jnp.tile(jnp.transpose(jnp.broadcast_to(ck_ref[e, 0], (LANES, t))), (1, t // LANES))
                    for e in range(hb)]

        def process(qi, masked):
            off = pl.multiple_of(qi * t, t)
            if masked:
                krows = lax.broadcasted_iota(jnp.int32, (t, t), 0)
                qcols = lax.broadcasted_iota(jnp.int32, (t, t), 1)
                valid = krows <= qcols
            for e in range(hb):
                qb = q_ref[e, pl.ds(off, t), :].astype(BF16)
                dob = do_ref[e, pl.ds(off, t), :].astype(BF16)
                st = lax.dot_general(kbs[e], qb, NT_DIMS, preferred_element_type=F32) * scale
                if fox:
                    st = st - ckcs[e]
                pt = jnp.exp(st - lse_ref[e, qi])
                if masked:
                    pt = jnp.where(valid, pt, 0.0)
                dv_scr[e] += jnp.dot(pt.astype(BF16), dob, preferred_element_type=F32)
                dpt = lax.dot_general(vbs[e], dob, NT_DIMS, preferred_element_type=F32)
                dst = pt * (dpt - delta_scr[e, qi])
                if fox:
                    dck_scr[e] -= jnp.sum(dst, axis=1, keepdims=True)
                    dcq_ref[e, qi] += jnp.sum(dst, axis=0, keepdims=True)
                dsb = (dst * scale).astype(BF16)
                dk_scr[e] += jnp.dot(dsb, qb, preferred_element_type=F32)
                dq_ref[e, pl.ds(off, t), :] += lax.dot_general(dsb, kbs[e], TN_DIMS, preferred_element_type=F32)

        def step(qi, carry):
            process(qi, False)
            return carry

        process(kj, True)
        lax.fori_loop(kj + 1, nq, step, 0)
        dk_ref[...] = dk_scr[...]
        dv_ref[...] = dv_scr[...]
        if fox:
            for e in range(hb):
                dck_ref[e, 0] = jnp.transpose(jnp.broadcast_to(dck_scr[e], (t, LANES)))[0:1, :]

    whole = lambda a, b: (a, 0, 0)
    tile = lambda a, b: (a, b, 0)
    rowv = lambda a, b: (a, 0, 0, 0)
    rowt = lambda a, b: (a, b, 0, 0)
    ins = [BS((hb, n, dk), whole), BS((hb, t, dk), tile), BS((hb, t, dv), tile), BS((hb, n, dv), whole),
           BS((hb, n, dv), whole), BS((hb, nq, 1, t), rowv)]
    args = [q, k, v, do, o, lse_r]
    outs = [BS((hb, n, dk), whole), BS((hb, t, dk), tile), BS((hb, t, dv), tile)]
    oshape = [SDS((h, n, dk), F32), SDS((h, n, dk), F32), SDS((h, n, dv), F32)]
    scratch = [pltpu.VMEM((hb, nq, 1, t), F32), pltpu.VMEM((hb, t, dk), F32), pltpu.VMEM((hb, t, dv), F32)]
    if fox:
        ins.append(BS((hb, 1, 1, t), rowt))
        args.append(ck_r)
        outs += [BS((hb, 1, 1, t), rowt), BS((hb, nq, 1, t), rowv)]
        oshape += [SDS((h, nq, 1, t), F32), SDS((h, nq, 1, t), F32)]
        scratch.append(pltpu.VMEM((hb, t, 1), F32))
    if ns:
        ins += [ANY] * ns
        args += list(side)
        outs += [ANY] * ns
        oshape += [SDS((3,) + s.shape[1:], s.dtype) for s in side]
        scratch += [pltpu.SemaphoreType.DMA((3 * ns,)), pltpu.SemaphoreType.DMA((3 * ns,))]
    return pl.pallas_call(
        body, name=name, grid=(h // hb, nq), in_specs=ins, out_specs=outs, out_shape=oshape, scratch_shapes=scratch,
        compiler_params=pltpu.CompilerParams(dimension_semantics=("arbitrary", "arbitrary"),
                                             vmem_limit_bytes=VMEM_LIMIT, has_side_effects=bool(ns)))(*args)


def causal_attention(q, k, v, c, scale, name, late=None, sinks=None):
    h, n, _ = q.shape
    t = _causal_tile(n)
    nq = n // t
    fox = c is not None

    def run_fwd(q, k, v, c, late, sinks):
        ck_r = c.reshape(h, nq, 1, t) if fox else None
        o, lse, got = _causal_fwd_call(q, k, v, ck_r, fox, scale, name + "_f", late)
        res = (q, k, v, c, late, o, lse)
        if late is None:
            return o, res
        return (o, got, [jnp.zeros((N_CHIPS,) + SHARD_SHAPE[w][1:], F32) for w in BIG]), res

    def run_bwd(res, ct):
        q, k, v, c, late, o, lse = res
        ck_r = c.reshape(h, nq, 1, t) if fox else None
        if late is None:
            outs = _causal_bwd_call(q, k, v, ct, o, lse, ck_r, fox, scale, name + "_b")
            return outs[0], outs[1], outs[2], ((outs[3] + outs[4]).reshape(h, n) if fox else None), None, None
        do, _, g1 = ct
        xi, yi, ci = _me()
        c_idx = ci.astype(jnp.int32).reshape(1)
        j_idx = (2 * xi + yi).astype(jnp.int32).reshape(1)
        gs = [g.reshape(N_CHIPS, 2, HALVED[w][1] // 2, HALVED[w][2]) for g, w in zip(g1, BIG)]
        r1 = _pair_exchange(gs, name + "_pair_exchange")
        s1 = [_pair_add(g, r, c_idx, name + "_pair_add_" + w) for g, r, w in zip(gs, r1, BIG)]
        outs = _causal_bwd_call(q, k, v, do, o, lse, ck_r, fox, scale, name + "_b", [s[1] for s in s1])
        gh = [_chip_add(s[0], r, j_idx, name + "_chip_add_" + w) for s, r, w in zip(s1, outs[-len(BIG):], BIG)]
        go = _half_exchange(gh, name + "_half_exchange")
        full = [jnp.where(ci == 0, jnp.stack([a, b]), jnp.stack([b, a])) for a, b in zip(gh, go)]
        return (outs[0], outs[1], outs[2], ((outs[3] + outs[4]).reshape(h, n) if fox else None),
                [jnp.zeros_like(a) for a in late], full)

    return _op(run_fwd, run_bwd)(q, k, v, c, late, sinks)


SWA_T = 128


def _swa_masks(qi):
    t = SWA_T
    r = lax.broadcasted_iota(jnp.int32, (t, 3 * t), 0)
    c = lax.broadcasted_iota(jnp.int32, (t, 3 * t), 1)
    seg0 = c < t
    seg1 = (c >= t) & (c < 2 * t)
    jp = jnp.maximum(qi - 1, 0)
    kpos = jnp.where(seg0, c, jnp.where(seg1, jp * t + c - t, qi * t + c - 2 * t))
    dist = qi * t + r - kpos
    band = (dist >= 0) & ((dist < WINDOW) | (kpos < N_META))
    valid = (seg0 & (kpos < N_META) & (qi >= 2)) | (jnp.logical_not(seg0) & band & (jnp.logical_not(seg1) | (qi >= 1)))
    return valid, dist.astype(F32)


def _swa_cat(ref, qi):
    t = SWA_T
    jp = jnp.maximum(qi - 1, 0)
    return jnp.concatenate([ref[0, 0:t, :], ref[0, pl.ds(pl.multiple_of(jp * t, t), t), :],
                            ref[0, pl.ds(pl.multiple_of(qi * t, t), t), :]], axis=0).astype(BF16)


def _swa_fwd_call(q, k, v, sinks, slopes, scale, name):
    hq, n, d = q.shape
    hkv = k.shape[0]
    g = hq // hkv
    t = SWA_T
    nq = n // t

    def body(q_ref, k_ref, v_ref, sink_ref, slope_ref, o_ref, lse_ref):
        grp = pl.program_id(0)
        qi = pl.program_id(1)
        valid, dist = _swa_masks(qi)
        kc = _swa_cat(k_ref, qi)
        vc = _swa_cat(v_ref, qi)
        qs = jnp.concatenate([q_ref[e] for e in range(g)], axis=0).astype(BF16)
        s_all = lax.dot_general(qs, kc, NT_DIMS, preferred_element_type=F32) * scale
        ps, ls, ms = [], [], []
        for e in range(g):
            hh = grp * g + e
            s = jnp.where(valid, s_all[e * t:(e + 1) * t] - slope_ref[hh] * dist, NEG)
            m = jnp.maximum(jnp.max(s, axis=1, keepdims=True), sink_ref[hh])
            p = jnp.exp(s - m)
            ls.append(jnp.sum(p, axis=1, keepdims=True) + jnp.exp(sink_ref[hh] - m))
            ms.append(m)
            ps.append(p.astype(BF16))
        acc = jnp.dot(jnp.concatenate(ps, axis=0), vc, preferred_element_type=F32)
        for e in range(g):
            o_ref[e] = acc[e * t:(e + 1) * t] / ls[e]
            lse_ref[e] = ms[e] + jnp.log(ls[e])

    return pl.pallas_call(
        body, name=name, grid=(hkv, nq),
        in_specs=[BS((g, t, d), lambda a, b: (a, b, 0)), BS((1, n, d), lambda a, b: (a, 0, 0)),
                  BS((1, n, d), lambda a, b: (a, 0, 0)), BS(memory_space=pltpu.SMEM), BS(memory_space=pltpu.SMEM)],
        out_specs=[BS((g, t, d), lambda a, b: (a, b, 0)), BS((g, t, 1), lambda a, b: (a, b, 0))],
        out_shape=[SDS((hq, n, d), F32), SDS((hq, n, 1), F32)],
        compiler_params=_cparams(("parallel", "parallel")))(q, k, v, sinks, slopes)


def _swa_bwd_call(q, k, v, o, lse, do, sinks, slopes, scale, name):
    hq, n, d = q.shape
    hkv = k.shape[0]
    g = hq // hkv
    t = SWA_T
    nq = n // t

    def body(q_ref, k_ref, v_ref, o_ref, lse_ref, do_ref, sink_ref, slope_ref, dq_ref, dk_ref, dv_ref, ds_ref):
        grp = pl.program_id(0)
        qi = pl.program_id(1)

        @pl.when(qi == 0)
        def _():
            dk_ref[...] = jnp.zeros_like(dk_ref)
            dv_ref[...] = jnp.zeros_like(dv_ref)
            ds_ref[...] = jnp.zeros_like(ds_ref)

        valid, dist = _swa_masks(qi)
        kc = _swa_cat(k_ref, qi)
        vc = _swa_cat(v_ref, qi)
        qs = jnp.concatenate([q_ref[e] for e in range(g)], axis=0).astype(BF16)
        dos = jnp.concatenate([do_ref[e] for e in range(g)], axis=0).astype(BF16)
        s_all = lax.dot_general(qs, kc, NT_DIMS, preferred_element_type=F32) * scale
        dp_all = lax.dot_general(dos, vc, NT_DIMS, preferred_element_type=F32)
        ps, dss = [], []
        for e in range(g):
            hh = grp * g + e
            lse_e = lse_ref[e]
            delta = jnp.sum(do_ref[e] * o_ref[e], axis=1, keepdims=True)
            s = s_all[e * t:(e + 1) * t] - slope_ref[hh] * dist
            p = jnp.where(valid, jnp.exp(s - lse_e), 0.0)
            ds = p * (dp_all[e * t:(e + 1) * t] - delta)
            ps.append(p.astype(BF16))
            dss.append((ds * scale).astype(BF16))
            ds_ref[e] += -jnp.sum(jnp.exp(sink_ref[hh] - lse_e) * delta)
        p_st = jnp.concatenate(ps, axis=0)
        ds_st = jnp.concatenate(dss, axis=0)
        dq = jnp.dot(ds_st, kc, preferred_element_type=F32)
        for e in range(g):
            dq_ref[e] = dq[e * t:(e + 1) * t]
        dkc = lax.dot_general(ds_st, qs, TN_DIMS, preferred_element_type=F32)
        dvc = lax.dot_general(p_st, dos, TN_DIMS, preferred_element_type=F32)
        jp = jnp.maximum(qi - 1, 0)
        for seg, off in enumerate((0, pl.multiple_of(jp * t, t), pl.multiple_of(qi * t, t))):
            dk_ref[0, pl.ds(off, t), :] += dkc[seg * t:(seg + 1) * t]
            dv_ref[0, pl.ds(off, t), :] += dvc[seg * t:(seg + 1) * t]

    tile = lambda a, b: (a, b, 0)
    whole = lambda a, b: (a, 0, 0)
    return pl.pallas_call(
        body, name=name, grid=(hkv, nq),
        in_specs=[BS((g, t, d), tile), BS((1, n, d), whole), BS((1, n, d), whole), BS((g, t, d), tile),
                  BS((g, t, 1), tile), BS((g, t, d), tile), BS(memory_space=pltpu.SMEM), BS(memory_space=pltpu.SMEM)],
        out_specs=[BS((g, t, d), tile), BS((1, n, d), whole), BS((1, n, d), whole), BS((g, 1, LANES), whole)],
        out_shape=[SDS((hq, n, d), F32), SDS((hkv, n, d), F32), SDS((hkv, n, d), F32), SDS((hq, 1, LANES), F32)],
        compiler_params=_cparams(("arbitrary", "arbitrary")))(q, k, v, o, lse, do, sinks, slopes)


def window_attention(q, k, v, sinks, slopes, scale, name):
    def run_fwd(q, k, v, sinks, slopes):
        o, lse = _swa_fwd_call(q, k, v, sinks, slopes, scale, name + "_f")
        return o, (q, k, v, sinks, slopes, o, lse)

    def run_bwd(res, do):
        q, k, v, sinks, slopes, o, lse = res
        dq, dk, dv, ds = _swa_bwd_call(q, k, v, o, lse, do, sinks, slopes, scale, name + "_b")
        return dq, dk, dv, ds[:, 0, 0], jnp.zeros_like(slopes)

    return _op(run_fwd, run_bwd)(q, k, v, sinks, slopes)


def _sigmoid(x):
    return 1.0 / (1.0 + jnp.exp(-x))


def _merge_fwd_call(gs, ys, name):
    n, c = ys[0].shape
    tr = _rows_tile(n, c, 1 << 20)

    def body(g0, g1, g2, y0, y1, y2, m_ref):
        m_ref[...] = (_sigmoid(g0[...]) * y0[...] + _sigmoid(g1[...]) * y1[...]) + _sigmoid(g2[...]) * y2[...]

    spec = BS((tr, c), lambda i: (i, 0))
    return pl.pallas_call(
        body, name=name, grid=(n // tr,), in_specs=[spec] * 6, out_specs=spec, out_shape=SDS((n, c), F32),
        compiler_params=_cparams(("parallel",)))(*gs, *ys)


def _merge_bwd_call(gs, ys, dm, name):
    n, c = ys[0].shape
    tr = _rows_tile(n, c, 1 << 20)

    def body(g0, g1, g2, y0, y1, y2, dm_ref, dg0, dg1, dg2, dy0, dy1, dy2):
        d = dm_ref[...]
        for g, y, dg, dy in ((g0, y0, dg0, dy0), (g1, y1, dg1, dy1), (g2, y2, dg2, dy2)):
            s = _sigmoid(g[...])
            dy[...] = d * s
            dg[...] = d * y[...] * (s * (1.0 - s))

    spec = BS((tr, c), lambda i: (i, 0))
    return pl.pallas_call(
        body, name=name, grid=(n // tr,), in_specs=[spec] * 7, out_specs=[spec] * 6,
        out_shape=[SDS((n, c), F32)] * 6, compiler_params=_cparams(("parallel",)))(*gs, *ys, dm)


def gated_merge(gs, ys, name):
    def fwd(gs, ys):
        return _merge_fwd_call(gs, ys, name + "_f"), (gs, ys)

    def bwd(res, dm):
        out = _merge_bwd_call(res[0], res[1], dm, name + "_b")
        return tuple(out[:3]), tuple(out[3:])

    return _op(fwd, bwd)(tuple(gs), tuple(ys))


CONV_TR = 264
CONV_TC = 1408


def _conv_tiles(n, f):
    tr = CONV_TR if n % CONV_TR == 0 else _div_tile(n, CONV_TR)
    tc = CONV_TC if f % CONV_TC == 0 else f
    return tr, tc


def _shift_down(cur, halo, first, tr):
    halo = jnp.where(first, 0.0, halo)
    row = lax.broadcasted_iota(jnp.int32, cur.shape, 0)
    h7, h6 = halo[7:8, :], halo[6:7, :]
    u1 = jnp.where(row == 0, h7, pltpu.roll(cur, 1, 0))
    u2 = jnp.where(row == 0, h6, jnp.where(row == 1, h7, pltpu.roll(cur, 2, 0)))
    return u1, u2


def _conv_lin(cur, u1, u2, w_ref, b_ref):
    return ((b_ref[...] + w_ref[0:1, :] * u2) + w_ref[1:2, :] * u1) + w_ref[2:3, :] * cur


def _conv_in_specs(tr, tc, nj):
    sub = tr // SUBLANES
    prev = lambda j, i: (jnp.maximum(i * sub - 1, 0), j)
    prev_v = lambda j, i: (jnp.maximum(i * sub - 1, 0), j + nj)
    return [BS((tr, tc), lambda j, i: (i, j)), BS((SUBLANES, tc), prev),
            BS((tr, tc), lambda j, i: (i, j + nj)), BS((SUBLANES, tc), prev_v),
            BS((3, tc), lambda j, i: (0, j)), BS((3, tc), lambda j, i: (0, j + nj)),
            BS((1, tc), lambda j, i: (0, j)), BS((1, tc), lambda j, i: (0, j + nj))]


def _conv_fwd_call(u, cw, cb, name):
    n, f2 = u.shape
    f = f2 // 2
    tr, tc = _conv_tiles(n, f)
    nj = f // tc

    def body(ug, ugh, uv, uvh, wg, wv, bg, bv, a_ref):
        first = pl.program_id(1) == 0
        g1, g2 = _shift_down(ug[...], ugh[...], first, tr)
        v1, v2 = _shift_down(uv[...], uvh[...], first, tr)
        cg = _conv_lin(ug[...], g1, g2, wg, bg)
        cv = _conv_lin(uv[...], v1, v2, wv, bv)
        a_ref[...] = cg * _sigmoid(cg) * cv

    return pl.pallas_call(
        body, name=name, grid=(nj, n // tr), in_specs=_conv_in_specs(tr, tc, nj),
        out_specs=BS((tr, tc), lambda j, i: (i, j)), out_shape=SDS((n, f), F32),
        compiler_params=_cparams(("parallel", "parallel")))(u, u, u, u, cw, cw, cb, cb)


def _conv_bwd_dc_call(u, cw, cb, da, name):
    n, f2 = u.shape
    f = f2 // 2
    tr, tc = _conv_tiles(n, f)
    nj = f // tc

    def body(ug, ugh, uv, uvh, wg, wv, bg, bv, da_ref, dc_ref, dw_ref, db_ref):
        first = pl.program_id(1) == 0
        g0, v0 = ug[...], uv[...]
        g1, g2 = _shift_down(g0, ugh[...], first, tr)
        v1, v2 = _shift_down(v0, uvh[...], first, tr)
        cg = _conv_lin(g0, g1, g2, wg, bg)
        cv = _conv_lin(v0, v1, v2, wv, bv)
        d = da_ref[...]
        s = _sigmoid(cg)
        dcg = d * cv * (s * (1.0 + cg * (1.0 - s)))
        dcv = d * (cg * s)
        dc_ref[0] = dcg
        dc_ref[1] = dcv

        @pl.when(first)
        def _():
            dw_ref[...] = jnp.zeros_like(dw_ref)
            db_ref[...] = jnp.zeros_like(db_ref)

        for p, dc, taps in ((0, dcg, (g2, g1, g0)), (1, dcv, (v2, v1, v0))):
            for t in range(3):
                dw_ref[p, t:t + 1, :] += jnp.sum(dc * taps[t], axis=0, keepdims=True)
            db_ref[p] += jnp.sum(dc, axis=0, keepdims=True)

    return pl.pallas_call(
        body, name=name, grid=(nj, n // tr),
        in_specs=_conv_in_specs(tr, tc, nj) + [BS((tr, tc), lambda j, i: (i, j))],
        out_specs=[BS((2, tr, tc), lambda j, i: (0, i, j)), BS((2, 3, tc), lambda j, i: (0, 0, j)),
                   BS((2, 1, tc), lambda j, i: (0, 0, j))],
        out_shape=[SDS((2, n, f), F32), SDS((2, 3, f), F32), SDS((2, 1, f), F32)],
        compiler_params=_cparams(("arbitrary", "arbitrary")))(u, u, u, u, cw, cw, cb, cb, da)


def _conv_bwd_du_call(dc, cw, name):
    _, n, f = dc.shape
    tr, tc = _conv_tiles(n, f)
    nj = f // tc
    ni = n // tr
    sub = tr // SUBLANES

    def body(c_ref, nx_ref, w_ref, du_ref):
        cur = c_ref[0]
        nxt = jnp.where(pl.program_id(2) == ni - 1, 0.0, nx_ref[0])
        row = lax.broadcasted_iota(jnp.int32, cur.shape, 0)
        n0, n1 = nxt[0:1, :], nxt[1:2, :]
        d1 = jnp.where(row == tr - 1, n0, pltpu.roll(cur, tr - 1, 0))
        d2 = jnp.where(row == tr - 1, n1, jnp.where(row == tr - 2, n0, pltpu.roll(cur, tr - 2, 0)))
        du_ref[...] = (w_ref[2:3, :] * cur + w_ref[1:2, :] * d1) + w_ref[0:1, :] * d2

    nxt_map = lambda p, j, i: (p, jnp.minimum((i + 1) * sub, n // SUBLANES - 1), j)
    return pl.pallas_call(
        body, name=name, grid=(2, nj, ni),
        in_specs=[BS((1, tr, tc), lambda p, j, i: (p, i, j)), BS((1, SUBLANES, tc), nxt_map),
                  BS((3, tc), lambda p, j, i: (0, p * nj + j))],
        out_specs=BS((tr, tc), lambda p, j, i: (i, p * nj + j)), out_shape=SDS((n, 2 * f), F32),
        compiler_params=_cparams(("parallel", "parallel", "parallel")))(dc, dc, cw)


def conv_glu(u, cw, cb, name):
    def fwd(u, cw, cb):
        return _conv_fwd_call(u, cw, cb, name + "_f"), (u, cw, cb)

    def bwd(res, da):
        u, cw, cb = res
        dc, dw, db = _conv_bwd_dc_call(u, cw, cb, da, name + "_bc")
        du = _conv_bwd_du_call(dc, cw, name + "_bu")
        return du, jnp.concatenate([dw[0], dw[1]], axis=-1), jnp.concatenate([db[0], db[1]], axis=-1)

    return _op(fwd, bwd)(u, cw, cb)


def _loss_call(y, t, n_real, name):
    n, c = y.shape
    tr = _rows_tile(n, c, 1 << 20)

    def body(y_ref, t_ref, dy_ref, l_ref):
        i = pl.program_id(0)
        row = i * tr + lax.broadcasted_iota(jnp.int32, (tr, c), 0)
        real = (row >= N_META) & (row < N_META + n_real)
        e = jnp.where(real, y_ref[...] - t_ref[...], 0.0)
        dy_ref[...] = e * (1.0 / c)

        @pl.when(i == 0)
        def _():
            l_ref[...] = jnp.zeros_like(l_ref)

        l_ref[...] += 0.5 * jnp.sum(jnp.sum(e * e, axis=-1, keepdims=True) * (1.0 / c), axis=0, keepdims=True)

    spec = BS((tr, c), lambda i: (i, 0))
    return pl.pallas_call(
        body, name=name, grid=(n // tr,), in_specs=[spec, spec],
        out_specs=[spec, BS((1, 1), lambda i: (0, 0))], out_shape=[SDS((n, c), F32), SDS((1, 1), F32)],
        compiler_params=_cparams(("arbitrary",)))(y, t)


def _to_heads(x, nh):
    n = x.shape[0]
    return x.reshape(n, nh, x.shape[1] // nh).transpose(1, 0, 2)


def _from_heads(x):
    h, n, d = x.shape
    return x.transpose(1, 0, 2).reshape(n, h * d)


def _head_norm(x, g, denom, name):
    h, n, d = x.shape
    return rms_norm(x.reshape(h * n, d), g, denom, name).reshape(h, n, d)


def _head_norm_rope(x, g, cos, sin, name):
    h, n, d = x.shape
    return rms_norm_rope(x.reshape(h * n, d), g, cos, sin, MLA_QK, name).reshape(h, n, d)


def _pad_in_cols(w):
    z = lambda k: jnp.zeros(w.shape[:-1] + (k,), w.dtype)
    return jnp.concatenate([w[..., :1544], z(120), w[..., 1544:1960], z(96), w[..., 1960:], z(128)], axis=-1)


def _pad_q_up(w):
    s = w.shape[:-1]
    w = w.reshape(s + (HEADS, MLA_QK))
    w = jnp.concatenate([w, jnp.zeros(s + (HEADS, LANES - MLA_QK), w.dtype)], axis=-1)
    return w.reshape(s + (HEADS * LANES,))


LAYERED = BIG + ("ffn_conv_w",)


def _assemble_layer(parts):
    out = {k: jnp.concatenate([v[i] for i in range(N_CHIPS)], axis=SHARD_AXIS[k] - 1) for k, v in parts.items()}
    out["w_in"] = _pad_in_cols(out["w_in"])
    out["mla_w_q_up"] = _pad_q_up(out["mla_w_q_up"])
    return out


def _layer_parts(gathered):
    return {k: g.reshape((N_CHIPS,) + SHARD_SHAPE[k][1:]) for k, g in zip(LAYERED, gathered)}


def _rope_tables(n):
    half = MLA_ROPE // 2
    freqs = ROPE_THETA ** (-jnp.arange(half, dtype=F32) / half)
    ang = jnp.arange(n).astype(F32)[:, None] * freqs[None, :]
    cos, sin = jnp.cos(ang), jnp.sin(ang)
    one, zero = jnp.ones((n, MLA_NOPE), F32), jnp.zeros((n, MLA_NOPE), F32)
    tail1, tail0 = jnp.ones((n, LANES - MLA_QK), F32), jnp.zeros((n, LANES - MLA_QK), F32)
    return (jnp.concatenate([one, cos, cos, tail1], axis=1), jnp.concatenate([zero, sin, sin, tail0], axis=1))


def _pad_lanes(g, width):
    return jnp.concatenate([g, jnp.zeros((width - g.shape[0],), g.dtype)]).reshape(1, width)


PROJ_SEGMENTS = ((O_FQ, 512), (O_FK, 512), (O_FV, 512), (O_FF, HEADS), (O_CQ, MLA_Q_RANK), (O_CKV, MLA_KV_RANK),
                 (O_KR, MLA_ROPE), (O_SQ, 512), (O_SK, 128), (O_SV, 128), (O_G, D_MODEL), (O_G + D_MODEL, D_MODEL),
                 (O_G + 2 * D_MODEL, D_MODEL))


def _split_proj(proj):
    def fwd(x):
        return tuple(x[:, s:s + w] for s, w in PROJ_SEGMENTS), None

    def bwd(_, cts):
        rows, parts, pos = cts[0].shape[0], [], 0
        for (s, w), ct in zip(PROJ_SEGMENTS, cts):
            if s > pos:
                parts.append(jnp.zeros((rows, s - pos), F32))
            parts.append(ct)
            pos = s + w
        parts.append(jnp.zeros((rows, IN_PAD - pos), F32))
        return (jnp.concatenate(parts, axis=1),)

    return _op(fwd, bwd)(proj)


def _trunk(eps, eps_cw, sinks, meta, small, x, w0, late):
    assert DEPTH == 2
    seq = x.shape[0]
    n = -(-(N_META + seq) // ROW_PAD) * ROW_PAD
    ew = _assemble_layer(eps)
    cos, sin = _rope_tables(n)
    slopes = jnp.exp2(-8.0 * jnp.arange(1, HEADS + 1, dtype=F32) / HEADS)
    h = jnp.concatenate([meta, x, jnp.zeros((n - N_META - seq, D_MODEL), F32)], axis=0)
    wb = w0
    for l in range(DEPTH):
        p = f"l{l}_"
        row = lambda name: small[name][l].reshape(1, -1)
        xn = rms_norm(h, row("norm1_g"), D_MODEL, p + "norm1")
        proj = linear(xn, wb["w_in"], ew["w_in"], p + "win")
        p_fq, p_fk, p_fv, p_ff, p_cq, p_ckv, p_kr, p_sq, p_sk, p_sv, g0, g1, g2 = _split_proj(proj)
        fq = _head_norm(_to_heads(p_fq, HEADS), row("fox_q_g"), HEAD_DIM, p + "fqn")
        fk = _head_norm(_to_heads(p_fk, HEADS), row("fox_k_g"), HEAD_DIM, p + "fkn")
        fv = _to_heads(p_fv, HEADS)
        c = forget_cumsum(p_ff.T, small["fox_forget_b"][l].reshape(HEADS, 1), p + "fgate")
        if l == 0:
            out_a, got, carriers = causal_attention(fq, fk, fv, c, HEAD_DIM ** -0.5, p + "fox", late, sinks)
            w1 = _assemble_layer(_layer_parts(_fill_own(got, [a[1] for a in late])))
            ew1 = _assemble_layer(dict(zip(BIG, carriers)))
        else:
            out_a = causal_attention(fq, fk, fv, c, HEAD_DIM ** -0.5, p + "fox")
        cqn = rms_norm(p_cq, row("mla_q_a_g"), MLA_Q_RANK, p + "cqn")
        q = _to_heads(linear(cqn, wb["mla_w_q_up"], ew["mla_w_q_up"], p + "qup"), HEADS)
        q = _head_norm_rope(q, _pad_lanes(small["mla_q_g"][l], LANES), cos, sin, p + "mqn")
        ckvn = rms_norm(p_ckv, row("mla_kv_a_g"), MLA_KV_RANK, p + "ckvn")
        kv = _to_heads(linear(ckvn, wb["mla_w_kv_up"], ew["mla_w_kv_up"], p + "kvup"), HEADS)
        kr = jnp.broadcast_to(p_kr[None], (HEADS, n, MLA_ROPE))
        k = jnp.concatenate([kv[..., :MLA_NOPE], kr, jnp.zeros((HEADS, n, LANES - MLA_QK), F32)], axis=-1)
        k = _head_norm_rope(k, _pad_lanes(small["mla_k_g"][l], LANES), cos, sin, p + "mkn")
        out_b = causal_attention(q, k, kv[..., MLA_NOPE:], None, MLA_QK ** -0.5, p + "mla")
        sq = _head_norm(_to_heads(p_sq, HEADS), row("swa_q_g"), HEAD_DIM, p + "sqn")
        sk = _head_norm(_to_heads(p_sk, SWA_KV_HEADS), row("swa_k_g"), HEAD_DIM, p + "skn")
        sv = _to_heads(p_sv, SWA_KV_HEADS)
        out_c = window_attention(sq, sk, sv, small["swa_sinks"][l], slopes, HEAD_DIM ** -0.5, p + "swa")
        ys = [linear(_from_heads(o), wb["w_branch"][i], ew["w_branch"][i], p + f"br{i}")
              for i, o in enumerate((out_a, out_b, out_c))]
        merged = gated_merge([g0, g1, g2], ys, p + "merge")
        h = linear(merged, wb["w_o"], ew["w_o"], p + "wo", res=h)
        xn2 = rms_norm(h, row("norm2_g"), D_MODEL, p + "norm2")
        u = linear(xn2, wb["ffn_w_up"], ew["ffn_w_up"], p + "wup")
        act = conv_glu(u, lax.stop_gradient(wb["ffn_conv_w"]) + eps_cw[l], row("ffn_conv_b"), p + "conv")
        h = linear(act, wb["ffn_w_down"], ew["ffn_w_down"], p + "wdown", res=h)
        wb, ew = w1, ew1
    return h


def _local_step(x, target, w0, late, meta, small):
    seq = x.shape[0]
    eps = {k: jnp.zeros((N_CHIPS,) + SHARD_SHAPE[k][1:], F32) for k in BIG}
    eps_cw = jnp.zeros((DEPTH, 3, 2 * D_FF), F32)
    sinks = [jnp.zeros((2, HALVED[k][1] // 2, HALVED[k][2]), F32) for k in BIG]
    y, vjp = jax.vjp(lambda e, ec, sk, mt, s, xx: _trunk(e, ec, sk, mt, s, xx, w0, late),
                     eps, eps_cw, sinks, meta, small, x)
    n = y.shape[0]
    tpad = jnp.concatenate([jnp.zeros((N_META, D_MODEL), F32), target, jnp.zeros((n - N_META - seq, D_MODEL), F32)])
    dy, loss = _loss_call(y, tpad, seq, "loss")
    g_eps, g_cw, g_l1, g_meta, g_small, g_x = vjp(dy)
    return loss[0, 0], g_x, g_eps, g_l1, g_cw, g_meta, g_small


def _pack_rows(shapes, mult):
    total = sum(_size(s) for s in shapes)
    rows = -(-total // LANES)
    return -(-rows // mult) * mult


def _size(shape):
    n = 1
    for d in shape:
        n *= d
    return n


def _pack(arrs, rows, dtype):
    flat = [a.reshape(-1).astype(dtype) for a in arrs]
    used = sum(a.size for a in flat)
    flat.append(jnp.zeros((rows * LANES - used,), dtype))
    return jnp.concatenate(flat).reshape(rows, LANES)


def _unpack(p, shapes):
    flat = p.reshape(-1)
    out, off = [], 0
    for s in shapes:
        out.append(flat[off:off + _size(s)].reshape(s))
        off += _size(s)
    return out


MESH = pl.DeviceIdType.MESH
ANY = pl.BlockSpec(memory_space=pl.ANY)


def _me():
    return lax.axis_index("x"), lax.axis_index("y"), lax.axis_index("c")


def _remote(src, dst, send_sems, recv_sems, idx, dev):
    return pltpu.make_async_remote_copy(src_ref=src, dst_ref=dst, send_sem=send_sems.at[idx], recv_sem=recv_sems.at[idx],
                                        device_id=dev, device_id_type=MESH)


def _layer_gather(ins, outs, send_sems, recv_sems, layer, base):
    x, y, _ = _me()
    j = 2 * x + y
    sibling = (x, y, 1 - layer)
    chips = [(1 - x, y), (x, 1 - y), (1 - x, 1 - y)]

    def ici(p, r):
        cx, cy = chips[r]
        return _remote(ins[p].at[layer], outs[p].at[j], send_sems, recv_sems, base + 6 * p + r, (cx, cy, layer))

    def d2d(p, r):
        cx, cy = chips[r]
        blk = outs[p].at[2 * cx + cy]
        return _remote(blk, blk, send_sems, recv_sems, base + 6 * p + 3 + r, sibling)

    pairs = [(p, r) for p in range(len(ins)) for r in range(3)]

    def start():
        for p, r in pairs:
            ici(p, r).start()

    def forward():
        for p, r in pairs:
            ici(p, r).wait_recv()
            d2d(p, r).start()

    def drain():
        for p, r in pairs:
            ici(p, r).wait_send()
            d2d(p, r).wait_send()

    def receive():
        for p, r in pairs:
            d2d(p, r).wait_recv()

    return start, forward, drain, receive


def _fill_own(outs, own):
    j = 2 * lax.axis_index("x") + lax.axis_index("y")
    return [lax.dynamic_update_index_in_dim(o, a, j, 0) for o, a in zip(outs, own)]


def _gather_early(meta, arrs, name):
    npk = len(arrs)

    def body(*refs):
        m_in, ins = refs[0], refs[1:npk + 1]
        m_out, outs = refs[npk + 1], refs[npk + 2:2 * npk + 2]
        send_sems, recv_sems = refs[2 * npk + 2:]
        x, y, c = _me()
        j = 2 * x + y
        sibling = (x, y, 1 - c)
        chips = [(1 - x, y), (x, 1 - y), (1 - x, 1 - y)]
        start, forward, drain, receive = _layer_gather(ins, outs, send_sems, recv_sems, 0, 6)
        sends = []
        for r, (cx, cy) in enumerate(chips):
            cp = _remote(m_in.at[c], m_out.at[j, c], send_sems, recv_sems, r, (cx, cy, c))
            cp.start()
            sends.append(cp)
        pl.when(c == 0)(start)
        for r, (cx, cy) in enumerate(chips):
            blk = m_out.at[2 * cx + cy, c]
            _remote(blk, blk, send_sems, recv_sems, r, sibling).wait_recv()
            fw = _remote(blk, blk, send_sems, recv_sems, 3 + r, sibling)
            fw.start()
            sends.append(fw)
        for r, (cx, cy) in enumerate(chips):
            blk = m_out.at[2 * cx + cy, 1 - c]
            _remote(blk, blk, send_sems, recv_sems, 3 + r, sibling).wait_recv()
        for cp in sends:
            cp.wait_send()

        @pl.when(c == 0)
        def _():
            forward()
            drain()

        pl.when(c == 1)(receive)

    nsem = 6 + 6 * npk
    res = pl.pallas_call(
        body, name=name, in_specs=[ANY] * (npk + 1), out_specs=[ANY] * (npk + 1),
        out_shape=[SDS((N_CHIPS,) + meta.shape, meta.dtype)] + [SDS((N_CHIPS,) + a.shape[1:], a.dtype) for a in arrs],
        scratch_shapes=[pltpu.SemaphoreType.DMA((nsem,)), pltpu.SemaphoreType.DMA((nsem,))],
        compiler_params=pltpu.CompilerParams(has_side_effects=True))(meta, *arrs)
    return _fill_own(res[:1], [meta])[0], _fill_own(res[1:], [a[0] for a in arrs])


def _pair_exchange(gs, name):
    npk = len(gs)

    def body(*refs):
        ins, outs = refs[:npk], refs[npk:2 * npk]
        send_sems, recv_sems = refs[2 * npk:]
        x, y, c = _me()
        cps = [_remote(ins[p].at[:, 1 - c], outs[p], send_sems, recv_sems, p, (x, y, 1 - c)) for p in range(npk)]
        for cp in cps:
            cp.start()
        for cp in cps:
            cp.wait()

    return pl.pallas_call(
        body, name=name, in_specs=[ANY] * npk, out_specs=[ANY] * npk,
        out_shape=[SDS((N_CHIPS,) + g.shape[2:], g.dtype) for g in gs],
        scratch_shapes=[pltpu.SemaphoreType.DMA((npk,)), pltpu.SemaphoreType.DMA((npk,))],
        compiler_params=pltpu.CompilerParams(has_side_effects=True))(*gs)


def _chip_exchange(ss, small, name):
    npk = len(ss)

    def body(*refs):
        ins, sm_ref = refs[:npk], refs[npk]
        outs, sa_ref = refs[npk + 1:2 * npk + 1], refs[2 * npk + 1]
        send_sems, recv_sems, loc_sem = refs[2 * npk + 2:]
        x, y, c = _me()
        me = 4 * x + 2 * y + c
        lc = pltpu.make_async_copy(sm_ref, sa_ref.at[me], loc_sem.at[0])
        lc.start()
        cps = []
        for p in range(npk):
            for r, (cx, cy) in enumerate([(1 - x, y), (x, 1 - y), (1 - x, 1 - y)]):
                cp = _remote(ins[p].at[2 * cx + cy], outs[p].at[r], send_sems, recv_sems, 3 * p + r, (cx, cy, c))
                cp.start()
                cps.append(cp)
        base = 3 * npk - 1
        for mask in range(1, N_DEV):
            px, py, pc = x ^ (mask >> 2), y ^ ((mask >> 1) & 1), c ^ (mask & 1)
            cp = _remote(sm_ref, sa_ref.at[me], send_sems, recv_sems, base + mask, (px, py, pc))
            cp.start()
            cps.append(cp)
        for p in range(npk):
            for r in range(3):
                _remote(outs[p].at[r], outs[p].at[r], send_sems, recv_sems, 3 * p + r, (x, y, c)).wait_recv()
        for mask in range(1, N_DEV):
            src = 4 * (x ^ (mask >> 2)) + 2 * (y ^ ((mask >> 1) & 1)) + (c ^ (mask & 1))
            _remote(sa_ref.at[src], sa_ref.at[src], send_sems, recv_sems, base + mask, (x, y, c)).wait_recv()
        for cp in cps:
            cp.wait_send()
        lc.wait()

    nsem = 3 * npk + N_DEV - 1
    res = pl.pallas_call(
        body, name=name, in_specs=[ANY] * (npk + 1), out_specs=[ANY] * (npk + 1),
        out_shape=[SDS((3,) + s.shape[1:], s.dtype) for s in ss] + [SDS((N_DEV,) + small.shape, small.dtype)],
        scratch_shapes=[pltpu.SemaphoreType.DMA((nsem,)), pltpu.SemaphoreType.DMA((nsem,)),
                        pltpu.SemaphoreType.DMA((1,))],
        compiler_params=pltpu.CompilerParams(has_side_effects=True))(*ss, small)
    return res[:npk], res[npk]


def _half_exchange(ghs, name):
    npk = len(ghs)

    def body(*refs):
        ins, outs = refs[:npk], refs[npk:2 * npk]
        send_sems, recv_sems = refs[2 * npk:]
        x, y, c = _me()
        cps = [_remote(ins[p], outs[p], send_sems, recv_sems, p, (x, y, 1 - c)) for p in range(npk)]
        for cp in cps:
            cp.start()
        for cp in cps:
            cp.wait()

    return pl.pallas_call(
        body, name=name, in_specs=[ANY] * npk, out_specs=[ANY] * npk, out_shape=[SDS(g.shape, g.dtype) for g in ghs],
        scratch_shapes=[pltpu.SemaphoreType.DMA((npk,)), pltpu.SemaphoreType.DMA((npk,))],
        compiler_params=pltpu.CompilerParams(has_side_effects=True))(*ghs)


def _add_tile(rows, cols):
    return _div_tile(rows, max(16, (1 << 19) // max(cols, LANES)), 16)


def _pair_add(g, r1, c_idx, name):
    _, rows, cols = r1.shape
    tr = _add_tile(rows, cols)

    def body(c_ref, g_ref, r_ref, o_ref, ob_ref):
        s = g_ref[0] + r_ref[...]
        o_ref[...] = s
        ob_ref[...] = s.astype(BF16)

    own = BS((1, tr, cols), lambda k, i, c: (k, i, 0))
    return pl.pallas_call(
        body, name=name,
        grid_spec=pltpu.PrefetchScalarGridSpec(
            num_scalar_prefetch=1, grid=(N_CHIPS, rows // tr),
            in_specs=[BS((1, 1, tr, cols), lambda k, i, c: (k, c[0], i, 0)), own], out_specs=[own, own]),
        out_shape=[SDS(r1.shape, F32), SDS(r1.shape, BF16)],
        compiler_params=_cparams(("parallel", "parallel")))(c_idx, g, r1)


def _chip_add(s1, r2, j_idx, name):
    _, rows, cols = s1.shape
    tr = _add_tile(rows, cols)

    def body(j_ref, s_ref, r_ref, o_ref):
        o_ref[...] = ((s_ref[0] + r_ref[0].astype(F32)) + r_ref[1].astype(F32)) + r_ref[2].astype(F32)

    return pl.pallas_call(
        body, name=name,
        grid_spec=pltpu.PrefetchScalarGridSpec(
            num_scalar_prefetch=1, grid=(rows // tr,),
            in_specs=[BS((1, tr, cols), lambda i, j: (j[0], i, 0)), BS((3, tr, cols), lambda i, j: (0, i, 0))],
            out_specs=BS((tr, cols), lambda i, j: (i, 0))),
        out_shape=SDS((rows, cols), F32), compiler_params=_cparams(("parallel",)))(j_idx, s1, r2)


def _adamw_math(w, g, m, v):
    m = ADAM_B1 * m + (1.0 - ADAM_B1) * g
    v = ADAM_B2 * v + (1.0 - ADAM_B2) * (g * g)
    m_hat = m / (1.0 - ADAM_B1 ** ADAM_STEP)
    v_hat = v / (1.0 - ADAM_B2 ** ADAM_STEP)
    delta = -ADAM_LR * (m_hat / (jnp.sqrt(v_hat) + ADAM_EPS) + ADAM_WD * w)
    return delta, m, v


def _adamw(w, gh, go, m, v, c_idx, name):
    _, rows, cols = w.shape
    tr = _add_tile(rows, cols)

    def body(c_ref, w_ref, gh_ref, go_ref, m_ref, v_ref, g_out, d_out, m_out, v_out):
        g = jnp.where(pl.program_id(0) == c_ref[0], gh_ref[...], go_ref[...])
        g_out[0] = g
        d_out[0], m_out[0], v_out[0] = _adamw_math(w_ref[0], g, m_ref[0], v_ref[0])

    full = BS((1, tr, cols), lambda hf, i, c: (hf, i, 0))
    half = BS((tr, cols), lambda hf, i, c: (i, 0))
    return pl.pallas_call(
        body, name=name,
        grid_spec=pltpu.PrefetchScalarGridSpec(
            num_scalar_prefetch=1, grid=(2, rows // tr), in_specs=[full, half, half, full, full],
            out_specs=[full] * 4),
        out_shape=[SDS(w.shape, F32)] * 4, compiler_params=_cparams(("parallel", "parallel")))(c_idx, w, gh, go, m, v)


def _adamw_layers(w, gh0, go0, g1, m, v, c_idx, name):
    _, _, rows, cols = w.shape
    tr = _add_tile(rows, cols)

    def body(c_ref, w_ref, gh_ref, go_ref, g1_ref, m_ref, v_ref, g_out, d_out, m_out, v_out):
        g0 = jnp.where(pl.program_id(1) == c_ref[0], gh_ref[...], go_ref[...])
        g = jnp.where(pl.program_id(0) == 0, g0, g1_ref[0])
        g_out[0, 0] = g
        d_out[0, 0], m_out[0, 0], v_out[0, 0] = _adamw_math(w_ref[0, 0], g, m_ref[0, 0], v_ref[0, 0])

    full = BS((1, 1, tr, cols), lambda l, hf, i, c: (l, hf, i, 0))
    half = BS((tr, cols), lambda l, hf, i, c: (i, 0))
    return pl.pallas_call(
        body, name=name,
        grid_spec=pltpu.PrefetchScalarGridSpec(
            num_scalar_prefetch=1, grid=(2, 2, rows // tr),
            in_specs=[full, half, half, BS((1, tr, cols), lambda l, hf, i, c: (hf, i, 0)), full, full],
            out_specs=[full] * 4),
        out_shape=[SDS(w.shape, F32)] * 4,
        compiler_params=_cparams(("parallel", "parallel", "parallel")))(c_idx, w, gh0, go0, g1, m, v)


def _sum_devices(sa, name):
    def body(sa_ref, g_out):
        g = sa_ref[0]
        for d in range(1, N_DEV):
            g = g + sa_ref[d]
        g_out[...] = g

    return pl.pallas_call(body, name=name, out_shape=SDS(sa.shape[1:], F32),
                          compiler_params=pltpu.CompilerParams(vmem_limit_bytes=VMEM_LIMIT))(sa)


def _adamw_small(ws, gs, ms, vs, name):
    k = len(ws)

    def body(*refs):
        ins, outs = refs[:4 * k], refs[4 * k:]
        for i in range(k):
            d, m, v = _adamw_math(ins[i][...], ins[k + i][...], ins[2 * k + i][...], ins[3 * k + i][...])
            outs[i][...], outs[k + i][...], outs[2 * k + i][...] = d, m, v

    return pl.pallas_call(body, name=name, out_shape=[SDS(w.shape, F32) for w in ws] * 3,
                          compiler_params=pltpu.CompilerParams(vmem_limit_bytes=VMEM_LIMIT))(*ws, *gs, *ms, *vs)


HALVED = {"w_in": (2, 1024, 1450), "mla_w_q_up": (2, 256, 192), "mla_w_kv_up": (2, 128, 256),
          "w_branch": (2, 1536, 256), "w_o": (2, 256, 1024), "ffn_w_up": (2, 1024, 1408),
          "ffn_w_down": (2, 704, 1024), "ffn_conv_w": (2, 3, 1408), "meta_tokens": (2, 8, 256)}
SMALL_SHAPE = {"norm1_g": (2, 1024), "fox_forget_b": (2, 8), "fox_q_g": (2, 64), "fox_k_g": (2, 64),
               "mla_q_a_g": (2, 256), "mla_kv_a_g": (2, 128), "mla_q_g": (2, 96), "mla_k_g": (2, 96),
               "swa_q_g": (2, 64), "swa_k_g": (2, 64), "swa_sinks": (2, 8), "norm2_g": (2, 1024),
               "ffn_conv_b": (2, 5632)}
SMALL_ROWS = _pack_rows([SMALL_SHAPE[k] for k in SMALL] + [(1,)], SUBLANES)


def kernel(x, meta_tokens, norm1_g, w_in, fox_forget_b, fox_q_g, fox_k_g, mla_q_a_g, mla_w_q_up, mla_kv_a_g, mla_w_kv_up, mla_q_g, mla_k_g, swa_q_g, swa_k_g, swa_sinks, w_branch, w_o, norm2_g, ffn_w_up, ffn_conv_w, ffn_conv_b, ffn_w_down, loss_target, m_meta_tokens, m_norm1_g, m_w_in, m_fox_forget_b, m_fox_q_g, m_fox_k_g, m_mla_q_a_g, m_mla_w_q_up, m_mla_kv_a_g, m_mla_w_kv_up, m_mla_q_g, m_mla_k_g, m_swa_q_g, m_swa_k_g, m_swa_sinks, m_w_branch, m_w_o, m_norm2_g, m_ffn_w_up, m_ffn_conv_w, m_ffn_conv_b, m_ffn_w_down, v_meta_tokens, v_norm1_g, v_w_in, v_fox_forget_b, v_fox_q_g, v_fox_k_g, v_mla_q_a_g, v_mla_w_q_up, v_mla_kv_a_g, v_mla_w_kv_up, v_mla_q_g, v_mla_k_g, v_swa_q_g, v_swa_k_g, v_swa_sinks, v_w_branch, v_w_o, v_norm2_g, v_ffn_w_up, v_ffn_conv_w, v_ffn_conv_b, v_ffn_w_down):
    w = dict(meta_tokens=meta_tokens, norm1_g=norm1_g, w_in=w_in, fox_forget_b=fox_forget_b, fox_q_g=fox_q_g,
             fox_k_g=fox_k_g, mla_q_a_g=mla_q_a_g, mla_w_q_up=mla_w_q_up, mla_kv_a_g=mla_kv_a_g,
             mla_w_kv_up=mla_w_kv_up, mla_q_g=mla_q_g, mla_k_g=mla_k_g, swa_q_g=swa_q_g, swa_k_g=swa_k_g,
             swa_sinks=swa_sinks, w_branch=w_branch, w_o=w_o, norm2_g=norm2_g, ffn_w_up=ffn_w_up,
             ffn_conv_w=ffn_conv_w, ffn_conv_b=ffn_conv_b, ffn_w_down=ffn_w_down)
    m = dict(meta_tokens=m_meta_tokens, norm1_g=m_norm1_g, w_in=m_w_in, fox_forget_b=m_fox_forget_b,
             fox_q_g=m_fox_q_g, fox_k_g=m_fox_k_g, mla_q_a_g=m_mla_q_a_g, mla_w_q_up=m_mla_w_q_up,
             mla_kv_a_g=m_mla_kv_a_g, mla_w_kv_up=m_mla_w_kv_up, mla_q_g=m_mla_q_g, mla_k_g=m_mla_k_g,
             swa_q_g=m_swa_q_g, swa_k_g=m_swa_k_g, swa_sinks=m_swa_sinks, w_branch=m_w_branch, w_o=m_w_o,
             norm2_g=m_norm2_g, ffn_w_up=m_ffn_w_up, ffn_conv_w=m_ffn_conv_w, ffn_conv_b=m_ffn_conv_b,
             ffn_w_down=m_ffn_w_down)
    v = dict(meta_tokens=v_meta_tokens, norm1_g=v_norm1_g, w_in=v_w_in, fox_forget_b=v_fox_forget_b,
             fox_q_g=v_fox_q_g, fox_k_g=v_fox_k_g, mla_q_a_g=v_mla_q_a_g, mla_w_q_up=v_mla_w_q_up,
             mla_kv_a_g=v_mla_kv_a_g, mla_w_kv_up=v_mla_w_kv_up, mla_q_g=v_mla_q_g, mla_k_g=v_mla_k_g,
             swa_q_g=v_swa_q_g, swa_k_g=v_swa_k_g, swa_sinks=v_swa_sinks, w_branch=v_w_branch, w_o=v_w_o,
             norm2_g=v_norm2_g, ffn_w_up=v_ffn_w_up, ffn_conv_w=v_ffn_conv_w, ffn_conv_b=v_ffn_conv_b,
             ffn_w_down=v_ffn_w_down)
    xi, yi, ci = _me()
    c_idx = ci.astype(jnp.int32).reshape(1)
    j_idx = (2 * xi + yi).astype(jnp.int32).reshape(1)

    sh_names = BIG + FINE
    local = {k: (w[k].astype(BF16) if k in BIG else w[k]).reshape(HALVED[k]) for k in sh_names}
    late = [local[k] for k in LAYERED]
    meta_g, early = _gather_early(local["meta_tokens"], late, "gather_early")
    meta = jnp.concatenate([meta_g[i].reshape(SHARD_SHAPE["meta_tokens"]) for i in range(N_CHIPS)], axis=1)
    w0 = _assemble_layer(_layer_parts(early))
    small = {k: w[k] for k in SMALL}

    loss, g_x, g_eps, g_l1, g_cw, g_meta, g_small = _local_step(x[0], loss_target[0], w0, late, meta, small)

    quarter = {k: (2, HALVED[k][1] // 2, HALVED[k][2]) for k in BIG}
    gs = [g_eps[k].reshape((N_CHIPS,) + quarter[k]) for k in BIG]
    for k, g in (("meta_tokens", g_meta), ("ffn_conv_w", g_cw)):
        gs.append(jnp.stack(jnp.split(g, N_CHIPS, axis=SHARD_AXIS[k])).reshape((N_CHIPS,) + HALVED[k]))
    spack = _pack([g_small[k] for k in SMALL] + [loss.reshape(1)], SMALL_ROWS, F32)
    r1 = _pair_exchange(gs, "grads_pair_exchange")
    s1 = [_pair_add(g, r, c_idx, "grads_pair_add_" + k) for g, r, k in zip(gs, r1, sh_names)]
    r2, sa = _chip_exchange([s[1] for s in s1], spack, "grads_chip_exchange")
    gh = [_chip_add(s[0], r, j_idx, "grads_chip_add_" + k) for s, r, k in zip(s1, r2, sh_names)]
    go = _half_exchange(gh, "grads_half_exchange")

    grads, deltas, new_m, new_v = {}, {}, {}, {}
    for i, (k, a, b) in enumerate(zip(sh_names, gh, go)):
        if k in BIG:
            shp = (2,) + quarter[k]
            outs = _adamw_layers(w[k].reshape(shp), a, b, g_l1[i], m[k].reshape(shp), v[k].reshape(shp), c_idx,
                                 "adamw_" + k)
        else:
            outs = _adamw(w[k].reshape(HALVED[k]), a, b, m[k].reshape(HALVED[k]), v[k].reshape(HALVED[k]), c_idx,
                          "adamw_" + k)
        for dst, o in zip((grads, deltas, new_m, new_v), outs):
            dst[k] = o.reshape(SHARD_SHAPE[k])
    sm_shapes = [SMALL_SHAPE[k] for k in SMALL] + [(1,)]
    g_sum = _unpack(_sum_devices(sa, "sum_small"), sm_shapes)
    res = _adamw_small([w[k] for k in SMALL], g_sum[:-1], [m[k] for k in SMALL], [v[k] for k in SMALL], "adamw_small")
    ns = len(SMALL)
    grads.update(zip(SMALL, g_sum[:-1]))
    for dst, vals in zip((deltas, new_m, new_v), (res[:ns], res[ns:2 * ns], res[2 * ns:])):
        dst.update(zip(SMALL, vals))
    total_loss = g_sum[-1][0]
    return (total_loss, g_x[None], *[grads[k] for k in WEIGHTS], *[deltas[k] for k in WEIGHTS],
            *[new_m[k] for k in WEIGHTS], *[new_v[k] for k in WEIGHTS])
```

```python
import functools

import jax
import jax.numpy as jnp
from jax import lax
from jax.experimental import pallas as pl
from jax.experimental.pallas import tpu as pltpu

F32 = jnp.float32
BF16 = jnp.bfloat16
SDS = jax.ShapeDtypeStruct
BS = pl.BlockSpec

D_MODEL = 1024
DEPTH = 2
N_META = 16
EPS = 1e-6
HEADS = 8
HEAD_DIM = 64
MLA_Q_RANK = 256
MLA_KV_RANK = 128
MLA_NOPE = 64
MLA_ROPE = 32
MLA_QK = MLA_NOPE + MLA_ROPE
ROPE_THETA = 10000.0
SWA_KV_HEADS = 2
WINDOW = 128
N_BRANCH = 3
BRANCH_WIDTH = 512
D_FF = 2816
IN_WIDTH = 5800
IN_PAD = 6144
N_CHIPS = 4
N_DEV = 8

ADAM_LR = 0.001
ADAM_B1 = 0.9
ADAM_B2 = 0.999
ADAM_EPS = 1e-08
ADAM_WD = 0.01
ADAM_STEP = 10

LANES = 128
SUBLANES = 8
ROW_PAD = 128
CAUSAL_TILE = 384
NEG = -1e30
VMEM_LIMIT = 56 * 1024 * 1024

O_FQ, O_FK, O_FV, O_FF = 0, 512, 1024, 1536
O_CQ, O_CKV, O_KR = 1664, 1920, 2048
O_SQ, O_SK, O_SV, O_G = 2176, 2688, 2816, 2944

SHARDED = ("meta_tokens", "w_in", "mla_w_q_up", "mla_w_kv_up", "w_branch", "w_o", "ffn_w_up", "ffn_conv_w",
           "ffn_w_down")
SHARD_AXIS = {"meta_tokens": 1, "w_in": 2, "mla_w_q_up": 2, "mla_w_kv_up": 2, "w_branch": 3, "w_o": 1,
              "ffn_w_up": 2, "ffn_conv_w": 2, "ffn_w_down": 1}
SHARD_SHAPE = {"meta_tokens": (16, 256), "w_in": (2, 1024, 1450), "mla_w_q_up": (2, 256, 192),
               "mla_w_kv_up": (2, 128, 256), "w_branch": (2, 3, 512, 256), "w_o": (2, 256, 1024),
               "ffn_w_up": (2, 1024, 1408), "ffn_conv_w": (2, 3, 1408), "ffn_w_down": (2, 704, 1024)}
BIG = ("w_in", "mla_w_q_up", "mla_w_kv_up", "w_branch", "w_o", "ffn_w_up", "ffn_w_down")
FINE = ("meta_tokens", "ffn_conv_w")
SMALL = ("norm1_g", "fox_forget_b", "fox_q_g", "fox_k_g", "mla_q_a_g", "mla_kv_a_g", "mla_q_g", "mla_k_g",
         "swa_q_g", "swa_k_g", "swa_sinks", "norm2_g", "ffn_conv_b")
WEIGHTS = ("meta_tokens", "norm1_g", "w_in", "fox_forget_b", "fox_q_g", "fox_k_g", "mla_q_a_g", "mla_w_q_up",
           "mla_kv_a_g", "mla_w_kv_up", "mla_q_g", "mla_k_g", "swa_q_g", "swa_k_g", "swa_sinks", "w_branch", "w_o",
           "norm2_g", "ffn_w_up", "ffn_conv_w", "ffn_conv_b", "ffn_w_down")


def _cparams(sem):
    return pltpu.CompilerParams(dimension_semantics=sem, vmem_limit_bytes=VMEM_LIMIT)


def _div_tile(n, cap, mult=SUBLANES):
    best = None
    for t in range(mult, min(n, cap) + 1, mult):
        if n % t == 0:
            best = t
    return best if best is not None else n


def _rows_tile(n, width, budget=2 << 20):
    return _div_tile(n, max(SUBLANES, budget // (4 * max(width, LANES))))


def _op(fwd, bwd):
    @jax.custom_vjp
    def op(*args):
        return fwd(*args)[0]
    op.defvjp(fwd, bwd)
    return op


def _rotate(v, cos, sin):
    lane = lax.broadcasted_iota(jnp.int32, v.shape, 1)
    rot = jnp.where(lane < MLA_NOPE + MLA_ROPE // 2, -pltpu.roll(v, LANES - MLA_ROPE // 2, 1),
                    pltpu.roll(v, MLA_ROPE // 2, 1))
    return v * cos + rot * sin


def _rms_fwd_call(x, g, denom, name, rot=None):
    n, c = x.shape
    tr = _rows_tile(n if rot is None else rot[0].shape[0], c)
    nt = None if rot is None else rot[0].shape[0] // tr

    def body(x_ref, g_ref, *rest):
        y_ref = rest[-1]
        xv = x_ref[...]
        ms = jnp.sum(xv * xv, axis=-1, keepdims=True) * (1.0 / denom)
        y = xv * lax.rsqrt(ms + EPS) * g_ref[...]
        y_ref[...] = y if rot is None else _rotate(y, rest[0][...], rest[1][...])

    ins, args = [BS((tr, c), lambda i: (i, 0)), BS((1, c), lambda i: (0, 0))], [x, g]
    if rot is not None:
        ins += [BS((tr, c), lambda i: (i % nt, 0))] * 2
        args += list(rot)
    return pl.pallas_call(
        body, name=name, grid=(n // tr,), in_specs=ins,
        out_specs=BS((tr, c), lambda i: (i, 0)), out_shape=SDS((n, c), F32),
        compiler_params=_cparams(("parallel",)))(*args)


def _rms_bwd_call(x, g, dy, denom, name, rot=None):
    n, c = x.shape
    tr = _rows_tile(n if rot is None else rot[0].shape[0], c)
    nt = None if rot is None else rot[0].shape[0] // tr

    def body(x_ref, g_ref, dy_ref, *rest):
        dx_ref, dg_ref = rest[-2:]
        xv = x_ref[...]
        dy = dy_ref[...]
        if rot is not None:
            dy = _rotate(dy, rest[0][...], -rest[1][...])
        ms = jnp.sum(xv * xv, axis=-1, keepdims=True) * (1.0 / denom)
        r = lax.rsqrt(ms + EPS)
        xh = xv * r
        dxh = dy * g_ref[...]
        dx_ref[...] = r * (dxh - xh * (jnp.sum(dxh * xh, axis=-1, keepdims=True) * (1.0 / denom)))

        @pl.when(pl.program_id(0) == 0)
        def _():
            dg_ref[...] = jnp.zeros_like(dg_ref)

        dg_ref[...] += jnp.sum(dy * xh, axis=0, keepdims=True)

    ins = [BS((tr, c), lambda i: (i, 0)), BS((1, c), lambda i: (0, 0)), BS((tr, c), lambda i: (i, 0))]
    args = [x, g, dy]
    if rot is not None:
        ins += [BS((tr, c), lambda i: (i % nt, 0))] * 2
        args += list(rot)
    return pl.pallas_call(
        body, name=name, grid=(n // tr,), in_specs=ins,
        out_specs=[BS((tr, c), lambda i: (i, 0)), BS((1, c), lambda i: (0, 0))],
        out_shape=[SDS((n, c), F32), SDS((1, c), F32)],
        compiler_params=_cparams(("arbitrary",)))(*args)


def rms_norm(x, g, denom, name):
    def fwd(x, g):
        return _rms_fwd_call(x, g, denom, name + "_f"), (x, g)

    def bwd(res, dy):
        return tuple(_rms_bwd_call(res[0], res[1], dy, denom, name + "_b"))

    return _op(fwd, bwd)(x, g)


def rms_norm_rope(x, g, cos, sin, denom, name):
    def fwd(x, g, cos, sin):
        return _rms_fwd_call(x, g, denom, name + "_f", (cos, sin)), (x, g, cos, sin)

    def bwd(res, dy):
        x, g, cos, sin = res
        dx, dg = _rms_bwd_call(x, g, dy, denom, name + "_b", (cos, sin))
        return dx, dg, jnp.zeros_like(cos), jnp.zeros_like(sin)

    return _op(fwd, bwd)(x, g, cos, sin)


def _mm_call(a, b, mode, res, name):
    if mode == "nn":
        (m, kc), n = a.shape, b.shape[1]
    elif mode == "nt":
        (m, kc), n = a.shape, b.shape[0]
    else:
        (kc, m), n = a.shape, b.shape[1]
    if mode == "tn":
        tk = _div_tile(kc, 528)
        tm = _div_tile(m, 1408, LANES)
        tn = _div_tile(n, 2048, LANES)
    else:
        tk = kc if kc <= 2816 else _div_tile(kc, 1024, LANES)
        tm = _div_tile(m, max(LANES, (9 << 19) // (4 * tk)))
        tn = _div_tile(n, 1408 if mode == "nt" else 512, LANES)
    nk = kc // tk
    dims = {"nn": (((1,), (0,)), ((), ())), "nt": (((1,), (1,)), ((), ())), "tn": (((0,), (0,)), ((), ()))}[mode]

    def body(*refs):
        if res is None:
            a_ref, b_ref, o_ref, acc_ref = refs
            r_ref = None
        else:
            a_ref, b_ref, r_ref, o_ref, acc_ref = refs
        k = pl.program_id(2)

        @pl.when(k == 0)
        def _():
            acc_ref[...] = jnp.zeros_like(acc_ref)

        acc_ref[...] += lax.dot_general(a_ref[...].astype(BF16), b_ref[...].astype(BF16), dims,
                                        preferred_element_type=F32)

        @pl.when(k == nk - 1)
        def _():
            if r_ref is None:
                o_ref[...] = acc_ref[...]
            else:
                o_ref[...] = r_ref[...] + acc_ref[...]

    a_spec = BS((tk, tm), lambda i, j, k: (k, i)) if mode == "tn" else BS((tm, tk), lambda i, j, k: (i, k))
    b_spec = BS((tn, tk), lambda i, j, k: (j, k)) if mode == "nt" else BS((tk, tn), lambda i, j, k: (k, j))
    o_spec = BS((tm, tn), lambda i, j, k: (i, j))
    ins, args = [a_spec, b_spec], [a, b]
    if res is not None:
        ins.append(o_spec)
        args.append(res)
    return pl.pallas_call(
        body, name=name, grid=(m // tm, n // tn, nk), in_specs=ins, out_specs=o_spec,
        out_shape=SDS((m, n), F32), scratch_shapes=[pltpu.VMEM((tm, tn), F32)],
        compiler_params=_cparams(("parallel", "parallel", "arbitrary")))(*args)


def linear(a, w, eps, name, res=None):
    if res is None:
        def fwd(a, w, eps):
            return _mm_call(a, w, "nn", None, name + "_f"), (a, w)

        def bwd(r, dc):
            a, w = r
            return (_mm_call(dc, w, "nt", None, name + "_da"), jnp.zeros_like(w),
                    _mm_call(a, dc, "tn", None, name + "_dw"))

        return _op(fwd, bwd)(a, w, eps)

    def fwd_r(a, w, eps, res):
        return _mm_call(a, w, "nn", res, name + "_f"), (a, w)

    def bwd_r(r, dc):
        a, w = r
        return (_mm_call(dc, w, "nt", None, name + "_da"), jnp.zeros_like(w),
                _mm_call(a, dc, "tn", None, name + "_dw"), dc)

    return _op(fwd_r, bwd_r)(a, w, eps, res)


CT = 128


def _tri_dot(v, upper):
    r = lax.broadcasted_iota(jnp.int32, (CT, CT), 0)
    c = lax.broadcasted_iota(jnp.int32, (CT, CT), 1)
    tri = jnp.where((r <= c) if upper else (r >= c), 1.0, 0.0).astype(F32)
    return jnp.dot(v, tri, preferred_element_type=F32, precision=lax.Precision.HIGHEST)


def _gate_fwd_call(z, b, name):
    h, n = z.shape

    def body(z_ref, b_ref, c_ref, carry):
        @pl.when(pl.program_id(0) == 0)
        def _():
            carry[...] = jnp.zeros_like(carry)

        x = z_ref[...] + b_ref[...]
        ls = jnp.minimum(x, 0.0) - jnp.log(1.0 + jnp.exp(-jnp.abs(x)))
        c_ref[...] = _tri_dot(ls, True) + carry[...]
        carry[...] += jnp.sum(ls, axis=1, keepdims=True)

    return pl.pallas_call(
        body, name=name, grid=(n // CT,),
        in_specs=[BS((h, CT), lambda j: (0, j)), BS((h, 1), lambda j: (0, 0))],
        out_specs=BS((h, CT), lambda j: (0, j)), out_shape=SDS((h, n), F32),
        scratch_shapes=[pltpu.VMEM((h, 1), F32)],
        compiler_params=_cparams(("arbitrary",)))(z, b)


def _gate_bwd_call(z, b, dc, name):
    h, n = z.shape
    nt = n // CT

    def body(z_ref, b_ref, dc_ref, dz_ref, db_ref, carry):
        @pl.when(pl.program_id(0) == 0)
        def _():
            carry[...] = jnp.zeros_like(carry)
            db_ref[...] = jnp.zeros_like(db_ref)

        dcv = dc_ref[...]
        dls = _tri_dot(dcv, False) + carry[...]
        carry[...] += jnp.sum(dcv, axis=1, keepdims=True)
        x = z_ref[...] + b_ref[...]
        e = jnp.exp(-jnp.abs(x))
        dz = dls * jnp.where(x >= 0, e / (1.0 + e), 1.0 / (1.0 + e))
        dz_ref[...] = dz
        db_ref[...] += jnp.sum(dz, axis=1, keepdims=True)

    rev = lambda j: (0, nt - 1 - j)
    return pl.pallas_call(
        body, name=name, grid=(nt,),
        in_specs=[BS((h, CT), rev), BS((h, 1), lambda j: (0, 0)), BS((h, CT), rev)],
        out_specs=[BS((h, CT), rev), BS((h, 1), lambda j: (0, 0))],
        out_shape=[SDS((h, n), F32), SDS((h, 1), F32)],
        scratch_shapes=[pltpu.VMEM((h, 1), F32)],
        compiler_params=_cparams(("arbitrary",)))(z, b, dc)


def forget_cumsum(z, b, name):
    def fwd(z, b):
        return _gate_fwd_call(z, b, name + "_f"), (z, b)

    def bwd(res, dc):
        return tuple(_gate_bwd_call(res[0], res[1], dc, name + "_b"))

    return _op(fwd, bwd)(z, b)


NT_DIMS = (((1,), (1,)), ((), ()))
TN_DIMS = (((0,), (0,)), ((), ()))
HEADS_PER_STEP = 2


def _causal_tile(n):
    return CAUSAL_TILE if n % CAUSAL_TILE == 0 else ROW_PAD


def _causal_fwd_call(q, k, v, ck_r, fox, scale, name, late=None):
    h, n, dk = q.shape
    dv = v.shape[2]
    t = _causal_tile(n)
    nq = n // t
    hb = HEADS_PER_STEP
    nl = 0 if late is None else len(late)
    n_in = 3 + int(fox) + nl

    def body(*refs):
        q_ref, k_ref, v_ref = refs[:3]
        ck_ref = refs[3] if fox else None
        o_ref, lse_ref = refs[n_in:n_in + 2]
        m_scr, l_scr, acc_scr = refs[n_in + 2 + nl:n_in + 5 + nl]
        qi = pl.program_id(1)
        if nl:
            start, forward, drain, receive = _layer_gather(refs[n_in - nl:n_in], refs[n_in + 2:n_in + 2 + nl],
                                                           refs[-2], refs[-1], 1, 0)
            hp, core = pl.program_id(0), lax.axis_index("c")
            last = (hp == h // hb - 1) & (qi == nq - 1)
            pl.when((hp == 0) & (qi == 0) & (core == 1))(start)
            pl.when((hp == h // hb // 2) & (qi == 0) & (core == 1))(forward)
            pl.when(last & (core == 1))(drain)
            pl.when(last & (core == 0))(receive)
        qbs = [q_ref[e].astype(BF16) for e in range(hb)]
        m_scr[...] = jnp.full(m_scr.shape, NEG, F32)
        l_scr[...] = jnp.zeros_like(l_scr)
        acc_scr[...] = jnp.zeros_like(acc_scr)

        def process(j, masked):
            off = pl.multiple_of(j * t, t)
            if masked:
                rows = lax.broadcasted_iota(jnp.int32, (t, t), 0)
                cols = lax.broadcasted_iota(jnp.int32, (t, t), 1)
                valid = cols <= rows
            for e in range(hb):
                kb = k_ref[e, pl.ds(off, t), :].astype(BF16)
                vb = v_ref[e, pl.ds(off, t), :].astype(BF16)
                s = lax.dot_general(qbs[e], kb, NT_DIMS, preferred_element_type=F32) * scale
                if fox:
                    s = s - ck_ref[e, j]
                if masked:
                    s = jnp.where(valid, s, NEG)
                m_old = m_scr[e]
                m_new = jnp.maximum(m_old, jnp.max(s, axis=1, keepdims=True))
                alpha = jnp.exp(m_old - m_new)
                p = jnp.exp(s - jnp.tile(m_new, (1, t // LANES)))
                l_scr[e] = alpha * l_scr[e] + jnp.sum(p, axis=1, keepdims=True)
                acc_scr[e] = alpha[:, :dv] * acc_scr[e] + jnp.dot(p.astype(BF16), vb, preferred_element_type=F32)
                m_scr[e] = m_new

        def step(j, carry):
            process(j, False)
            return carry

        lax.fori_loop(0, qi, step, 0)
        process(qi, True)
        for e in range(hb):
            l = l_scr[e]
            o_ref[e] = acc_scr[e] / l[:, :dv]
            lse_ref[e, 0] = jnp.transpose(m_scr[e] + jnp.log(l))[0:1, :]

    ins = [BS((hb, t, dk), lambda a, b: (a, b, 0)), BS((hb, n, dk), lambda a, b: (a, 0, 0)),
           BS((hb, n, dv), lambda a, b: (a, 0, 0))]
    args = [q, k, v]
    if fox:
        ins.append(BS((hb, nq, 1, t), lambda a, b: (a, 0, 0, 0)))
        args.append(ck_r)
    outs = [BS((hb, t, dv), lambda a, b: (a, b, 0)), BS((hb, 1, 1, t), lambda a, b: (a, b, 0, 0))]
    oshape = [SDS((h, n, dv), F32), SDS((h, nq, 1, t), F32)]
    scratch = [pltpu.VMEM((hb, t, LANES), F32), pltpu.VMEM((hb, t, LANES), F32), pltpu.VMEM((hb, t, dv), F32)]
    if nl:
        ins += [ANY] * nl
        args += list(late)
        outs += [ANY] * nl
        oshape += [SDS((N_CHIPS,) + a.shape[1:], a.dtype) for a in late]
        scratch += [pltpu.SemaphoreType.DMA((6 * nl,)), pltpu.SemaphoreType.DMA((6 * nl,))]
    res = pl.pallas_call(
        body, name=name, grid=(h // hb, nq), in_specs=ins, out_specs=outs, out_shape=oshape, scratch_shapes=scratch,
        compiler_params=pltpu.CompilerParams(dimension_semantics=("arbitrary", "arbitrary"),
                                             vmem_limit_bytes=VMEM_LIMIT, has_side_effects=bool(nl)))(*args)
    return res[0], res[1], list(res[2:])


def _causal_bwd_call(q, k, v, do, o, lse_r, ck_r, fox, scale, name, side=None):
    h, n, dk = q.shape
    dv = v.shape[2]
    t = _causal_tile(n)
    nq = n // t
    hb = HEADS_PER_STEP
    ns = 0 if side is None else len(side)

    def body(*refs):
        it = iter(refs)
        q_ref, k_ref, v_ref, do_ref, o_ref, lse_ref = (next(it) for _ in range(6))
        ck_ref = next(it) if fox else None
        side_in = [next(it) for _ in range(ns)]
        dq_ref, dk_ref, dv_ref = next(it), next(it), next(it)
        dck_ref, dcq_ref = (next(it), next(it)) if fox else (None, None)
        side_out = [next(it) for _ in range(ns)]
        delta_scr, dk_scr, dv_scr = next(it), next(it), next(it)
        dck_scr = next(it) if fox else None
        kj = pl.program_id(1)
        if ns:
            send_sems, recv_sems = next(it), next(it)
            x, y, core = _me()
            chips = [(1 - x, y), (x, 1 - y), (1 - x, 1 - y)]
            copies = [_remote(side_in[p].at[2 * cx + cy], side_out[p].at[r], send_sems, recv_sems, 3 * p + r,
                              (cx, cy, core)) for p in range(ns) for r, (cx, cy) in enumerate(chips)]

            @pl.when((pl.program_id(0) == 0) & (kj == 0))
            def _():
                for cp in copies:
                    cp.start()

            @pl.when((pl.program_id(0) == h // hb - 1) & (kj == nq - 1))
            def _():
                for cp in copies:
                    cp.wait()

        @pl.when(kj == 0)
        def _():
            dq_ref[...] = jnp.zeros_like(dq_ref)
            if fox:
                dcq_ref[...] = jnp.zeros_like(dcq_ref)
            ones = jnp.ones((SUBLANES, dv), F32)

            def fill(qi, carry):
                off = pl.multiple_of(qi * t, t)
                for e in range(hb):
                    prod = do_ref[e, pl.ds(off, t), :] * o_ref[e, pl.ds(off, t), :]
                    delta_scr[e, qi] = lax.dot_general(ones, prod, NT_DIMS, preferred_element_type=F32,
                                                       precision=lax.Precision.HIGHEST)[0:1, :]
                return carry

            lax.fori_loop(0, nq, fill, 0)

        kbs = [k_ref[e].astype(BF16) for e in range(hb)]
        vbs = [v_ref[e].astype(BF16) for e in range(hb)]
        dk_scr[...] = jnp.zeros_like(dk_scr)
        dv_scr[...] = jnp.zeros_like(dv_scr)
        if fox:
            dck_scr[...] = jnp.zeros_like(dck_scr)
            ckcs = [jnp.tile(jnp.transpose(jnp.broadcast_to(ck_ref[e, 0], (LANES, t))), (1, t // LANES))
                    for e in range(hb)]

        def process(qi, masked):
            off = pl.multiple_of(qi * t, t)
            if masked:
                krows = lax.broadcasted_iota(jnp.int32, (t, t), 0)
                qcols = lax.broadcasted_iota(jnp.int32, (t, t), 1)
                valid = krows <= qcols
            for e in range(hb):
                qb = q_ref[e, pl.ds(off, t), :].astype(BF16)
                dob = do_ref[e, pl.ds(off, t), :].astype(BF16)
                st = lax.dot_general(kbs[e], qb, NT_DIMS, preferred_element_type=F32) * scale
                if fox:
                    st = st - ckcs[e]
                pt = jnp.exp(st - lse_ref[e, qi])
                if masked:
                    pt = jnp.where(valid, pt, 0.0)
                dv_scr[e] += jnp.dot(pt.astype(BF16), dob, preferred_element_type=F32)
                dpt = lax.dot_general(vbs[e], dob, NT_DIMS, preferred_element_type=F32)
                dst = pt * (dpt - delta_scr[e, qi])
                if fox:
                    dck_scr[e] -= jnp.sum(dst, axis=1, keepdims=True)
                    dcq_ref[e, qi] += jnp.sum(dst, axis=0, keepdims=True)
                dsb = (dst * scale).astype(BF16)
                dk_scr[e] += jnp.dot(dsb, qb, preferred_element_type=F32)
                dq_ref[e, pl.ds(off, t), :] += lax.dot_general(dsb, kbs[e], TN_DIMS, preferred_element_type=F32)

        def step(qi, carry):
            process(qi, False)
            return carry

        process(kj, True)
        lax.fori_loop(kj + 1, nq, step, 0)
        dk_ref[...] = dk_scr[...]
        dv_ref[...] = dv_scr[...]
        if fox:
            for e in range(hb):
                dck_ref[e, 0] = jnp.transpose(jnp.broadcast_to(dck_scr[e], (t, LANES)))[0:1, :]

    whole = lambda a, b: (a, 0, 0)
    tile = lambda a, b: (a, b, 0)
    rowv = lambda a, b: (a, 0, 0, 0)
    rowt = lambda a, b: (a, b, 0, 0)
    ins = [BS((hb, n, dk), whole), BS((hb, t, dk), tile), BS((hb, t, dv), tile), BS((hb, n, dv), whole),
           BS((hb, n, dv), whole), BS((hb, nq, 1, t), rowv)]
    args = [q, k, v, do, o, lse_r]
    outs = [BS((hb, n, dk), whole), BS((hb, t, dk), tile), BS((hb, t, dv), tile)]
    oshape = [SDS((h, n, dk), F32), SDS((h, n, dk), F32), SDS((h, n, dv), F32)]
    scratch = [pltpu.VMEM((hb, nq, 1, t), F32), pltpu.VMEM((hb, t, dk), F32), pltpu.VMEM((hb, t, dv), F32)]
    if fox:
        ins.append(BS((hb, 1, 1, t), rowt))
        args.append(ck_r)
        outs += [BS((hb, 1, 1, t), rowt), BS((hb, nq, 1, t), rowv)]
        oshape += [SDS((h, nq, 1, t), F32), SDS((h, nq, 1, t), F32)]
        scratch.append(pltpu.VMEM((hb, t, 1), F32))
    if ns:
        ins += [ANY] * ns
        args += list(side)
        outs += [ANY] * ns
        oshape += [SDS((3,) + s.shape[1:], s.dtype) for s in side]
        scratch += [pltpu.SemaphoreType.DMA((3 * ns,)), pltpu.SemaphoreType.DMA((3 * ns,))]
    return pl.pallas_call(
        body, name=name, grid=(h // hb, nq), in_specs=ins, out_specs=outs, out_shape=oshape, scratch_shapes=scratch,
        compiler_params=pltpu.CompilerParams(dimension_semantics=("arbitrary", "arbitrary"),
                                             vmem_limit_bytes=VMEM_LIMIT, has_side_effects=bool(ns)))(*args)


def causal_attention(q, k, v, c, scale, name, late=None, reduce=None, sinks=None):
    h, n, _ = q.shape
    t = _causal_tile(n)
    nq = n // t
    fox = c is not None

    def run_fwd(q, k, v, c, late, sinks):
        ck_r = c.reshape(h, nq, 1, t) if fox else None
        o, lse, got = _causal_fwd_call(q, k, v, ck_r, fox, scale, name + "_f", late)
        out = (o,)
        if late is not None:
            out += (got,)
        if reduce is not None:
            out += ([jnp.zeros((N_CHIPS,) + SHARD_SHAPE[w][1:], F32) for w in reduce],)
        return (out if len(out) > 1 else o), (q, k, v, c, late, o, lse)

    def run_bwd(res, ct):
        q, k, v, c, late, o, lse = res
        ck_r = c.reshape(h, nq, 1, t) if fox else None
        dlate = None if late is None else [jnp.zeros_like(a) for a in late]
        if reduce is None:
            do = ct if late is None else ct[0]
            outs = _causal_bwd_call(q, k, v, do, o, lse, ck_r, fox, scale, name + "_b")
            return outs[0], outs[1], outs[2], ((outs[3] + outs[4]).reshape(h, n) if fox else None), dlate, None
        do, g1 = ct[0], ct[-1]
        xi, yi, ci = _me()
        c_idx = ci.astype(jnp.int32).reshape(1)
        j_idx = (2 * xi + yi).astype(jnp.int32).reshape(1)
        gs = [g.reshape(N_CHIPS, 2, HALVED[w][1] // 2, HALVED[w][2]) for g, w in zip(g1, reduce)]
        r1 = _pair_exchange(gs, name + "_pair_exchange")
        s1 = [_pair_add(g, r, c_idx, name + "_pair_add_" + w) for g, r, w in zip(gs, r1, reduce)]
        outs = _causal_bwd_call(q, k, v, do, o, lse, ck_r, fox, scale, name + "_b", [s[1] for s in s1])
        gh = [_chip_add(s[0], r, j_idx, name + "_chip_add_" + w) for s, r, w in zip(s1, outs[-len(reduce):], reduce)]
        go = _half_exchange(gh, name + "_half_exchange")
        full = [jnp.where(ci == 0, jnp.stack([a, b]), jnp.stack([b, a])) for a, b in zip(gh, go)]
        return outs[0], outs[1], outs[2], ((outs[3] + outs[4]).reshape(h, n) if fox else None), dlate, full

    return _op(run_fwd, run_bwd)(q, k, v, c, late, sinks)


SWA_T = 128


def _swa_masks(qi):
    t = SWA_T
    r = lax.broadcasted_iota(jnp.int32, (t, 3 * t), 0)
    c = lax.broadcasted_iota(jnp.int32, (t, 3 * t), 1)
    seg0 = c < t
    seg1 = (c >= t) & (c < 2 * t)
    jp = jnp.maximum(qi - 1, 0)
    kpos = jnp.where(seg0, c, jnp.where(seg1, jp * t + c - t, qi * t + c - 2 * t))
    dist = qi * t + r - kpos
    band = (dist >= 0) & ((dist < WINDOW) | (kpos < N_META))
    valid = (seg0 & (kpos < N_META) & (qi >= 2)) | (jnp.logical_not(seg0) & band & (jnp.logical_not(seg1) | (qi >= 1)))
    return valid, dist.astype(F32)


def _swa_cat(ref, qi):
    t = SWA_T
    jp = jnp.maximum(qi - 1, 0)
    return jnp.concatenate([ref[0, 0:t, :], ref[0, pl.ds(pl.multiple_of(jp * t, t), t), :],
                            ref[0, pl.ds(pl.multiple_of(qi * t, t), t), :]], axis=0).astype(BF16)


def _swa_fwd_call(q, k, v, sinks, slopes, scale, name):
    hq, n, d = q.shape
    hkv = k.shape[0]
    g = hq // hkv
    t = SWA_T
    nq = n // t

    def body(q_ref, k_ref, v_ref, sink_ref, slope_ref, o_ref, lse_ref):
        grp = pl.program_id(0)
        qi = pl.program_id(1)
        valid, dist = _swa_masks(qi)
        kc = _swa_cat(k_ref, qi)
        vc = _swa_cat(v_ref, qi)
        qs = jnp.concatenate([q_ref[e] for e in range(g)], axis=0).astype(BF16)
        s_all = lax.dot_general(qs, kc, NT_DIMS, preferred_element_type=F32) * scale
        ps, ls, ms = [], [], []
        for e in range(g):
            hh = grp * g + e
            s = jnp.where(valid, s_all[e * t:(e + 1) * t] - slope_ref[hh] * dist, NEG)
            m = jnp.maximum(jnp.max(s, axis=1, keepdims=True), sink_ref[hh])
            p = jnp.exp(s - m)
            ls.append(jnp.sum(p, axis=1, keepdims=True) + jnp.exp(sink_ref[hh] - m))
            ms.append(m)
            ps.append(p.astype(BF16))
        acc = jnp.dot(jnp.concatenate(ps, axis=0), vc, preferred_element_type=F32)
        for e in range(g):
            o_ref[e] = acc[e * t:(e + 1) * t] / ls[e]
            lse_ref[e] = ms[e] + jnp.log(ls[e])

    return pl.pallas_call(
        body, name=name, grid=(hkv, nq),
        in_specs=[BS((g, t, d), lambda a, b: (a, b, 0)), BS((1, n, d), lambda a, b: (a, 0, 0)),
                  BS((1, n, d), lambda a, b: (a, 0, 0)), BS(memory_space=pltpu.SMEM), BS(memory_space=pltpu.SMEM)],
        out_specs=[BS((g, t, d), lambda a, b: (a, b, 0)), BS((g, t, 1), lambda a, b: (a, b, 0))],
        out_shape=[SDS((hq, n, d), F32), SDS((hq, n, 1), F32)],
        compiler_params=_cparams(("parallel", "parallel")))(q, k, v, sinks, slopes)


def _swa_bwd_call(q, k, v, o, lse, do, sinks, slopes, scale, name):
    hq, n, d = q.shape
    hkv = k.shape[0]
    g = hq // hkv
    t = SWA_T
    nq = n // t

    def body(q_ref, k_ref, v_ref, o_ref, lse_ref, do_ref, sink_ref, slope_ref, dq_ref, dk_ref, dv_ref, ds_ref):
        grp = pl.program_id(0)
        qi = pl.program_id(1)

        @pl.when(qi == 0)
        def _():
            dk_ref[...] = jnp.zeros_like(dk_ref)
            dv_ref[...] = jnp.zeros_like(dv_ref)
            ds_ref[...] = jnp.zeros_like(ds_ref)

        valid, dist = _swa_masks(qi)
        kc = _swa_cat(k_ref, qi)
        vc = _swa_cat(v_ref, qi)
        qs = jnp.concatenate([q_ref[e] for e in range(g)], axis=0).astype(BF16)
        dos = jnp.concatenate([do_ref[e] for e in range(g)], axis=0).astype(BF16)
        s_all = lax.dot_general(qs, kc, NT_DIMS, preferred_element_type=F32) * scale
        dp_all = lax.dot_general(dos, vc, NT_DIMS, preferred_element_type=F32)
        ps, dss = [], []
        for e in range(g):
            hh = grp * g + e
            lse_e = lse_ref[e]
            delta = jnp.sum(do_ref[e] * o_ref[e], axis=1, keepdims=True)
            s = s_all[e * t:(e + 1) * t] - slope_ref[hh] * dist
            p = jnp.where(valid, jnp.exp(s - lse_e), 0.0)
            ds = p * (dp_all[e * t:(e + 1) * t] - delta)
            ps.append(p.astype(BF16))
            dss.append((ds * scale).astype(BF16))
            ds_ref[e] += -jnp.sum(jnp.exp(sink_ref[hh] - lse_e) * delta)
        p_st = jnp.concatenate(ps, axis=0)
        ds_st = jnp.concatenate(dss, axis=0)
        dq = jnp.dot(ds_st, kc, preferred_element_type=F32)
        for e in range(g):
            dq_ref[e] = dq[e * t:(e + 1) * t]
        dkc = lax.dot_general(ds_st, qs, TN_DIMS, preferred_element_type=F32)
        dvc = lax.dot_general(p_st, dos, TN_DIMS, preferred_element_type=F32)
        jp = jnp.maximum(qi - 1, 0)
        for seg, off in enumerate((0, pl.multiple_of(jp * t, t), pl.multiple_of(qi * t, t))):
            dk_ref[0, pl.ds(off, t), :] += dkc[seg * t:(seg + 1) * t]
            dv_ref[0, pl.ds(off, t), :] += dvc[seg * t:(seg + 1) * t]

    tile = lambda a, b: (a, b, 0)
    whole = lambda a, b: (a, 0, 0)
    return pl.pallas_call(
        body, name=name, grid=(hkv, nq),
        in_specs=[BS((g, t, d), tile), BS((1, n, d), whole), BS((1, n, d), whole), BS((g, t, d), tile),
                  BS((g, t, 1), tile), BS((g, t, d), tile), BS(memory_space=pltpu.SMEM), BS(memory_space=pltpu.SMEM)],
        out_specs=[BS((g, t, d), tile), BS((1, n, d), whole), BS((1, n, d), whole), BS((g, 1, LANES), whole)],
        out_shape=[SDS((hq, n, d), F32), SDS((hkv, n, d), F32), SDS((hkv, n, d), F32), SDS((hq, 1, LANES), F32)],
        compiler_params=_cparams(("arbitrary", "arbitrary")))(q, k, v, o, lse, do, sinks, slopes)


def window_attention(q, k, v, sinks, slopes, scale, name):
    def run_fwd(q, k, v, sinks, slopes):
        o, lse = _swa_fwd_call(q, k, v, sinks, slopes, scale, name + "_f")
        return o, (q, k, v, sinks, slopes, o, lse)

    def run_bwd(res, do):
        q, k, v, sinks, slopes, o, lse = res
        dq, dk, dv, ds = _swa_bwd_call(q, k, v, o, lse, do, sinks, slopes, scale, name + "_b")
        return dq, dk, dv, ds[:, 0, 0], jnp.zeros_like(slopes)

    return _op(run_fwd, run_bwd)(q, k, v, sinks, slopes)


def _sigmoid(x):
    return 1.0 / (1.0 + jnp.exp(-x))


def _merge_fwd_call(gs, ys, name):
    n, c = ys[0].shape
    tr = _rows_tile(n, c, 1 << 20)

    def body(g0, g1, g2, y0, y1, y2, m_ref):
        m_ref[...] = (_sigmoid(g0[...]) * y0[...] + _sigmoid(g1[...]) * y1[...]) + _sigmoid(g2[...]) * y2[...]

    spec = BS((tr, c), lambda i: (i, 0))
    return pl.pallas_call(
        body, name=name, grid=(n // tr,), in_specs=[spec] * 6, out_specs=spec, out_shape=SDS((n, c), F32),
        compiler_params=_cparams(("parallel",)))(*gs, *ys)


def _merge_bwd_call(gs, ys, dm, name):
    n, c = ys[0].shape
    tr = _rows_tile(n, c, 1 << 20)

    def body(g0, g1, g2, y0, y1, y2, dm_ref, dg0, dg1, dg2, dy0, dy1, dy2):
        d = dm_ref[...]
        for g, y, dg, dy in ((g0, y0, dg0, dy0), (g1, y1, dg1, dy1), (g2, y2, dg2, dy2)):
            s = _sigmoid(g[...])
            dy[...] = d * s
            dg[...] = d * y[...] * (s * (1.0 - s))

    spec = BS((tr, c), lambda i: (i, 0))
    return pl.pallas_call(
        body, name=name, grid=(n // tr,), in_specs=[spec] * 7, out_specs=[spec] * 6,
        out_shape=[SDS((n, c), F32)] * 6, compiler_params=_cparams(("parallel",)))(*gs, *ys, dm)


def gated_merge(gs, ys, name):
    def fwd(gs, ys):
        return _merge_fwd_call(gs, ys, name + "_f"), (gs, ys)

    def bwd(res, dm):
        out = _merge_bwd_call(res[0], res[1], dm, name + "_b")
        return tuple(out[:3]), tuple(out[3:])

    return _op(fwd, bwd)(tuple(gs), tuple(ys))


CONV_TR = 264
CONV_TC = 1408


def _conv_tiles(n, f):
    tr = CONV_TR if n % CONV_TR == 0 else _div_tile(n, CONV_TR)
    tc = CONV_TC if f % CONV_TC == 0 else f
    return tr, tc


def _shift_down(cur, halo, first, tr):
    halo = jnp.where(first, 0.0, halo)
    row = lax.broadcasted_iota(jnp.int32, cur.shape, 0)
    h7, h6 = halo[7:8, :], halo[6:7, :]
    u1 = jnp.where(row == 0, h7, pltpu.roll(cur, 1, 0))
    u2 = jnp.where(row == 0, h6, jnp.where(row == 1, h7, pltpu.roll(cur, 2, 0)))
    return u1, u2


def _conv_lin(cur, u1, u2, w_ref, b_ref):
    return ((b_ref[...] + w_ref[0:1, :] * u2) + w_ref[1:2, :] * u1) + w_ref[2:3, :] * cur


def _conv_in_specs(tr, tc, nj):
    sub = tr // SUBLANES
    prev = lambda j, i: (jnp.maximum(i * sub - 1, 0), j)
    prev_v = lambda j, i: (jnp.maximum(i * sub - 1, 0), j + nj)
    return [BS((tr, tc), lambda j, i: (i, j)), BS((SUBLANES, tc), prev),
            BS((tr, tc), lambda j, i: (i, j + nj)), BS((SUBLANES, tc), prev_v),
            BS((3, tc), lambda j, i: (0, j)), BS((3, tc), lambda j, i: (0, j + nj)),
            BS((1, tc), lambda j, i: (0, j)), BS((1, tc), lambda j, i: (0, j + nj))]


def _conv_fwd_call(u, cw, cb, name):
    n, f2 = u.shape
    f = f2 // 2
    tr, tc = _conv_tiles(n, f)
    nj = f // tc

    def body(ug, ugh, uv, uvh, wg, wv, bg, bv, a_ref):
        first = pl.program_id(1) == 0
        g1, g2 = _shift_down(ug[...], ugh[...], first, tr)
        v1, v2 = _shift_down(uv[...], uvh[...], first, tr)
        cg = _conv_lin(ug[...], g1, g2, wg, bg)
        cv = _conv_lin(uv[...], v1, v2, wv, bv)
        a_ref[...] = cg * _sigmoid(cg) * cv

    return pl.pallas_call(
        body, name=name, grid=(nj, n // tr), in_specs=_conv_in_specs(tr, tc, nj),
        out_specs=BS((tr, tc), lambda j, i: (i, j)), out_shape=SDS((n, f), F32),
        compiler_params=_cparams(("parallel", "parallel")))(u, u, u, u, cw, cw, cb, cb)


def _conv_bwd_dc_call(u, cw, cb, da, name):
    n, f2 = u.shape
    f = f2 // 2
    tr, tc = _conv_tiles(n, f)
    nj = f // tc

    def body(ug, ugh, uv, uvh, wg, wv, bg, bv, da_ref, dc_ref, dw_ref, db_ref):
        first = pl.program_id(1) == 0
        g0, v0 = ug[...], uv[...]
        g1, g2 = _shift_down(g0, ugh[...], first, tr)
        v1, v2 = _shift_down(v0, uvh[...], first, tr)
        cg = _conv_lin(g0, g1, g2, wg, bg)
        cv = _conv_lin(v0, v1, v2, wv, bv)
        d = da_ref[...]
        s = _sigmoid(cg)
        dcg = d * cv * (s * (1.0 + cg * (1.0 - s)))
        dcv = d * (cg * s)
        dc_ref[0] = dcg
        dc_ref[1] = dcv

        @pl.when(first)
        def _():
            dw_ref[...] = jnp.zeros_like(dw_ref)
            db_ref[...] = jnp.zeros_like(db_ref)

        for p, dc, taps in ((0, dcg, (g2, g1, g0)), (1, dcv, (v2, v1, v0))):
            for t in range(3):
                dw_ref[p, t:t + 1, :] += jnp.sum(dc * taps[t], axis=0, keepdims=True)
            db_ref[p] += jnp.sum(dc, axis=0, keepdims=True)

    return pl.pallas_call(
        body, name=name, grid=(nj, n // tr),
        in_specs=_conv_in_specs(tr, tc, nj) + [BS((tr, tc), lambda j, i: (i, j))],
        out_specs=[BS((2, tr, tc), lambda j, i: (0, i, j)), BS((2, 3, tc), lambda j, i: (0, 0, j)),
                   BS((2, 1, tc), lambda j, i: (0, 0, j))],
        out_shape=[SDS((2, n, f), F32), SDS((2, 3, f), F32), SDS((2, 1, f), F32)],
        compiler_params=_cparams(("arbitrary", "arbitrary")))(u, u, u, u, cw, cw, cb, cb, da)


def _conv_bwd_du_call(dc, cw, name):
    _, n, f = dc.shape
    tr, tc = _conv_tiles(n, f)
    nj = f // tc
    ni = n // tr
    sub = tr // SUBLANES

    def body(c_ref, nx_ref, w_ref, du_ref):
        cur = c_ref[0]
        nxt = jnp.where(pl.program_id(2) == ni - 1, 0.0, nx_ref[0])
        row = lax.broadcasted_iota(jnp.int32, cur.shape, 0)
        n0, n1 = nxt[0:1, :], nxt[1:2, :]
        d1 = jnp.where(row == tr - 1, n0, pltpu.roll(cur, tr - 1, 0))
        d2 = jnp.where(row == tr - 1, n1, jnp.where(row == tr - 2, n0, pltpu.roll(cur, tr - 2, 0)))
        du_ref[...] = (w_ref[2:3, :] * cur + w_ref[1:2, :] * d1) + w_ref[0:1, :] * d2

    nxt_map = lambda p, j, i: (p, jnp.minimum((i + 1) * sub, n // SUBLANES - 1), j)
    return pl.pallas_call(
        body, name=name, grid=(2, nj, ni),
        in_specs=[BS((1, tr, tc), lambda p, j, i: (p, i, j)), BS((1, SUBLANES, tc), nxt_map),
                  BS((3, tc), lambda p, j, i: (0, p * nj + j))],
        out_specs=BS((tr, tc), lambda p, j, i: (i, p * nj + j)), out_shape=SDS((n, 2 * f), F32),
        compiler_params=_cparams(("parallel", "parallel", "parallel")))(dc, dc, cw)


def conv_glu(u, cw, cb, name):
    def fwd(u, cw, cb):
        return _conv_fwd_call(u, cw, cb, name + "_f"), (u, cw, cb)

    def bwd(res, da):
        u, cw, cb = res
        dc, dw, db = _conv_bwd_dc_call(u, cw, cb, da, name + "_bc")
        du = _conv_bwd_du_call(dc, cw, name + "_bu")
        return du, jnp.concatenate([dw[0], dw[1]], axis=-1), jnp.concatenate([db[0], db[1]], axis=-1)

    return _op(fwd, bwd)(u, cw, cb)


def _loss_call(y, t, n_real, name):
    n, c = y.shape
    tr = _rows_tile(n, c, 1 << 20)

    def body(y_ref, t_ref, dy_ref, l_ref):
        i = pl.program_id(0)
        row = i * tr + lax.broadcasted_iota(jnp.int32, (tr, c), 0)
        real = (row >= N_META) & (row < N_META + n_real)
        e = jnp.where(real, y_ref[...] - t_ref[...], 0.0)
        dy_ref[...] = e * (1.0 / c)

        @pl.when(i == 0)
        def _():
            l_ref[...] = jnp.zeros_like(l_ref)

        l_ref[...] += 0.5 * jnp.sum(jnp.sum(e * e, axis=-1, keepdims=True) * (1.0 / c), axis=0, keepdims=True)

    spec = BS((tr, c), lambda i: (i, 0))
    return pl.pallas_call(
        body, name=name, grid=(n // tr,), in_specs=[spec, spec],
        out_specs=[spec, BS((1, 1), lambda i: (0, 0))], out_shape=[SDS((n, c), F32), SDS((1, 1), F32)],
        compiler_params=_cparams(("arbitrary",)))(y, t)


def _to_heads(x, nh):
    n = x.shape[0]
    return x.reshape(n, nh, x.shape[1] // nh).transpose(1, 0, 2)


def _from_heads(x):
    h, n, d = x.shape
    return x.transpose(1, 0, 2).reshape(n, h * d)


def _head_norm(x, g, denom, name):
    h, n, d = x.shape
    return rms_norm(x.reshape(h * n, d), g, denom, name).reshape(h, n, d)


def _head_norm_rope(x, g, cos, sin, name):
    h, n, d = x.shape
    return rms_norm_rope(x.reshape(h * n, d), g, cos, sin, MLA_QK, name).reshape(h, n, d)


def _pad_in_cols(w):
    z = lambda k: jnp.zeros(w.shape[:-1] + (k,), w.dtype)
    return jnp.concatenate([w[..., :1544], z(120), w[..., 1544:1960], z(96), w[..., 1960:], z(128)], axis=-1)


def _pad_q_up(w):
    s = w.shape[:-1]
    w = w.reshape(s + (HEADS, MLA_QK))
    w = jnp.concatenate([w, jnp.zeros(s + (HEADS, LANES - MLA_QK), w.dtype)], axis=-1)
    return w.reshape(s + (HEADS * LANES,))


LAYERED = BIG + ("ffn_conv_w",)
BEFORE_MLA = ("w_in", "mla_w_q_up", "mla_w_kv_up")
AFTER_MLA = ("w_branch", "w_o", "ffn_w_up", "ffn_w_down")


def _assemble_layer(parts):
    out = {k: jnp.concatenate([v[i] for i in range(N_CHIPS)], axis=SHARD_AXIS[k] - 1) for k, v in parts.items()}
    if "w_in" in out:
        out["w_in"] = _pad_in_cols(out["w_in"])
    if "mla_w_q_up" in out:
        out["mla_w_q_up"] = _pad_q_up(out["mla_w_q_up"])
    return out


def _layer_parts(gathered):
    return {k: g.reshape((N_CHIPS,) + SHARD_SHAPE[k][1:]) for k, g in zip(LAYERED, gathered)}


def _rope_tables(n):
    half = MLA_ROPE // 2
    freqs = ROPE_THETA ** (-jnp.arange(half, dtype=F32) / half)
    ang = jnp.arange(n).astype(F32)[:, None] * freqs[None, :]
    cos, sin = jnp.cos(ang), jnp.sin(ang)
    one, zero = jnp.ones((n, MLA_NOPE), F32), jnp.zeros((n, MLA_NOPE), F32)
    tail1, tail0 = jnp.ones((n, LANES - MLA_QK), F32), jnp.zeros((n, LANES - MLA_QK), F32)
    return (jnp.concatenate([one, cos, cos, tail1], axis=1), jnp.concatenate([zero, sin, sin, tail0], axis=1))


def _pad_lanes(g, width):
    return jnp.concatenate([g, jnp.zeros((width - g.shape[0],), g.dtype)]).reshape(1, width)


PROJ_SEGMENTS = ((O_FQ, 512), (O_FK, 512), (O_FV, 512), (O_FF, HEADS), (O_CQ, MLA_Q_RANK), (O_CKV, MLA_KV_RANK),
                 (O_KR, MLA_ROPE), (O_SQ, 512), (O_SK, 128), (O_SV, 128), (O_G, D_MODEL), (O_G + D_MODEL, D_MODEL),
                 (O_G + 2 * D_MODEL, D_MODEL))


def _split_proj(proj):
    def fwd(x):
        return tuple(x[:, s:s + w] for s, w in PROJ_SEGMENTS), None

    def bwd(_, cts):
        rows, parts, pos = cts[0].shape[0], [], 0
        for (s, w), ct in zip(PROJ_SEGMENTS, cts):
            if s > pos:
                parts.append(jnp.zeros((rows, s - pos), F32))
            parts.append(ct)
            pos = s + w
        parts.append(jnp.zeros((rows, IN_PAD - pos), F32))
        return (jnp.concatenate(parts, axis=1),)

    return _op(fwd, bwd)(proj)


def _trunk(eps, eps_cw, sinks, meta, small, x, w0, late):
    assert DEPTH == 2
    seq = x.shape[0]
    n = -(-(N_META + seq) // ROW_PAD) * ROW_PAD
    ew = _assemble_layer(eps)
    cos, sin = _rope_tables(n)
    slopes = jnp.exp2(-8.0 * jnp.arange(1, HEADS + 1, dtype=F32) / HEADS)
    h = jnp.concatenate([meta, x, jnp.zeros((n - N_META - seq, D_MODEL), F32)], axis=0)
    wb = w0
    for l in range(DEPTH):
        p = f"l{l}_"
        row = lambda name: small[name][l].reshape(1, -1)
        xn = rms_norm(h, row("norm1_g"), D_MODEL, p + "norm1")
        proj = linear(xn, wb["w_in"], ew["w_in"], p + "win")
        p_fq, p_fk, p_fv, p_ff, p_cq, p_ckv, p_kr, p_sq, p_sk, p_sv, g0, g1, g2 = _split_proj(proj)
        fq = _head_norm(_to_heads(p_fq, HEADS), row("fox_q_g"), HEAD_DIM, p + "fqn")
        fk = _head_norm(_to_heads(p_fk, HEADS), row("fox_k_g"), HEAD_DIM, p + "fkn")
        fv = _to_heads(p_fv, HEADS)
        c = forget_cumsum(p_ff.T, small["fox_forget_b"][l].reshape(HEADS, 1), p + "fgate")
        if l == 0:
            out_a, got, carriers = causal_attention(fq, fk, fv, c, HEAD_DIM ** -0.5, p + "fox", late, BIG, sinks[1])
            w1 = _assemble_layer(_layer_parts(_fill_own(got, [a[1] for a in late])))
            ew1 = _assemble_layer(dict(zip(BIG, carriers)))
        else:
            out_a = causal_attention(fq, fk, fv, c, HEAD_DIM ** -0.5, p + "fox")
        cqn = rms_norm(p_cq, row("mla_q_a_g"), MLA_Q_RANK, p + "cqn")
        q = _to_heads(linear(cqn, wb["mla_w_q_up"], ew["mla_w_q_up"], p + "qup"), HEADS)
        q = _head_norm_rope(q, _pad_lanes(small["mla_q_g"][l], LANES), cos, sin, p + "mqn")
        ckvn = rms_norm(p_ckv, row("mla_kv_a_g"), MLA_KV_RANK, p + "ckvn")
        kv = _to_heads(linear(ckvn, wb["mla_w_kv_up"], ew["mla_w_kv_up"], p + "kvup"), HEADS)
        kr = jnp.broadcast_to(p_kr[None], (HEADS, n, MLA_ROPE))
        k = jnp.concatenate([kv[..., :MLA_NOPE], kr, jnp.zeros((HEADS, n, LANES - MLA_QK), F32)], axis=-1)
        k = _head_norm_rope(k, _pad_lanes(small["mla_k_g"][l], LANES), cos, sin, p + "mkn")
        if l == 0:
            out_b, carriers = causal_attention(q, k, kv[..., MLA_NOPE:], None, MLA_QK ** -0.5, p + "mla", None,
                                               AFTER_MLA, sinks[0])
            ew = dict(ew, **_assemble_layer(dict(zip(AFTER_MLA, carriers))))
        else:
            out_b = causal_attention(q, k, kv[..., MLA_NOPE:], None, MLA_QK ** -0.5, p + "mla")
        sq = _head_norm(_to_heads(p_sq, HEADS), row("swa_q_g"), HEAD_DIM, p + "sqn")
        sk = _head_norm(_to_heads(p_sk, SWA_KV_HEADS), row("swa_k_g"), HEAD_DIM, p + "skn")
        sv = _to_heads(p_sv, SWA_KV_HEADS)
        out_c = window_attention(sq, sk, sv, small["swa_sinks"][l], slopes, HEAD_DIM ** -0.5, p + "swa")
        ys = [linear(_from_heads(o), wb["w_branch"][i], ew["w_branch"][i], p + f"br{i}")
              for i, o in enumerate((out_a, out_b, out_c))]
        merged = gated_merge([g0, g1, g2], ys, p + "merge")
        h = linear(merged, wb["w_o"], ew["w_o"], p + "wo", res=h)
        xn2 = rms_norm(h, row("norm2_g"), D_MODEL, p + "norm2")
        u = linear(xn2, wb["ffn_w_up"], ew["ffn_w_up"], p + "wup")
        act = conv_glu(u, lax.stop_gradient(wb["ffn_conv_w"]) + eps_cw[l], row("ffn_conv_b"), p + "conv")
        h = linear(act, wb["ffn_w_down"], ew["ffn_w_down"], p + "wdown", res=h)
        wb, ew = w1, ew1
    return h


def _local_step(x, target, w0, late, meta, small):
    seq = x.shape[0]
    eps = {k: jnp.zeros((N_CHIPS,) + SHARD_SHAPE[k][1:], F32) for k in BEFORE_MLA}
    eps_cw = jnp.zeros((DEPTH, 3, 2 * D_FF), F32)
    sinks = tuple([jnp.zeros((2, HALVED[k][1] // 2, HALVED[k][2]), F32) for k in names] for names in (AFTER_MLA, BIG))
    y, vjp = jax.vjp(lambda e, ec, sk, mt, s, xx: _trunk(e, ec, sk, mt, s, xx, w0, late),
                     eps, eps_cw, sinks, meta, small, x)
    n = y.shape[0]
    tpad = jnp.concatenate([jnp.zeros((N_META, D_MODEL), F32), target, jnp.zeros((n - N_META - seq, D_MODEL), F32)])
    dy, loss = _loss_call(y, tpad, seq, "loss")
    g_eps, g_cw, g_l1, g_meta, g_small, g_x = vjp(dy)
    return loss[0, 0], g_x, g_eps, g_l1, g_cw, g_meta, g_small


def _pack_rows(shapes, mult):
    total = sum(_size(s) for s in shapes)
    rows = -(-total // LANES)
    return -(-rows // mult) * mult


def _size(shape):
    n = 1
    for d in shape:
        n *= d
    return n


def _pack(arrs, rows, dtype):
    flat = [a.reshape(-1).astype(dtype) for a in arrs]
    used = sum(a.size for a in flat)
    flat.append(jnp.zeros((rows * LANES - used,), dtype))
    return jnp.concatenate(flat).reshape(rows, LANES)


def _unpack(p, shapes):
    flat = p.reshape(-1)
    out, off = [], 0
    for s in shapes:
        out.append(flat[off:off + _size(s)].reshape(s))
        off += _size(s)
    return out


MESH = pl.DeviceIdType.MESH
ANY = pl.BlockSpec(memory_space=pl.ANY)


def _me():
    return lax.axis_index("x"), lax.axis_index("y"), lax.axis_index("c")


def _remote(src, dst, send_sems, recv_sems, idx, dev):
    return pltpu.make_async_remote_copy(src_ref=src, dst_ref=dst, send_sem=send_sems.at[idx], recv_sem=recv_sems.at[idx],
                                        device_id=dev, device_id_type=MESH)


def _layer_gather(ins, outs, send_sems, recv_sems, layer, base):
    x, y, _ = _me()
    j = 2 * x + y
    sibling = (x, y, 1 - layer)
    chips = [(1 - x, y), (x, 1 - y), (1 - x, 1 - y)]

    def ici(p, r):
        cx, cy = chips[r]
        return _remote(ins[p].at[layer], outs[p].at[j], send_sems, recv_sems, base + 6 * p + r, (cx, cy, layer))

    def d2d(p, r):
        cx, cy = chips[r]
        blk = outs[p].at[2 * cx + cy]
        return _remote(blk, blk, send_sems, recv_sems, base + 6 * p + 3 + r, sibling)

    pairs = [(p, r) for p in range(len(ins)) for r in range(3)]

    def start():
        for p, r in pairs:
            ici(p, r).start()

    def forward():
        for p, r in pairs:
            ici(p, r).wait_recv()
            d2d(p, r).start()

    def drain():
        for p, r in pairs:
            ici(p, r).wait_send()
            d2d(p, r).wait_send()

    def receive():
        for p, r in pairs:
            d2d(p, r).wait_recv()

    return start, forward, drain, receive


def _fill_own(outs, own):
    j = 2 * lax.axis_index("x") + lax.axis_index("y")
    return [lax.dynamic_update_index_in_dim(o, a, j, 0) for o, a in zip(outs, own)]


def _gather_early(meta, arrs, name):
    npk = len(arrs)

    def body(*refs):
        m_in, ins = refs[0], refs[1:npk + 1]
        m_out, outs = refs[npk + 1], refs[npk + 2:2 * npk + 2]
        send_sems, recv_sems = refs[2 * npk + 2:]
        x, y, c = _me()
        j = 2 * x + y
        sibling = (x, y, 1 - c)
        chips = [(1 - x, y), (x, 1 - y), (1 - x, 1 - y)]
        start, forward, drain, receive = _layer_gather(ins, outs, send_sems, recv_sems, 0, 6)
        sends = []
        for r, (cx, cy) in enumerate(chips):
            cp = _remote(m_in.at[c], m_out.at[j, c], send_sems, recv_sems, r, (cx, cy, c))
            cp.start()
            sends.append(cp)
        pl.when(c == 0)(start)
        for r, (cx, cy) in enumerate(chips):
            blk = m_out.at[2 * cx + cy, c]
            _remote(blk, blk, send_sems, recv_sems, r, sibling).wait_recv()
            fw = _remote(blk, blk, send_sems, recv_sems, 3 + r, sibling)
            fw.start()
            sends.append(fw)
        for r, (cx, cy) in enumerate(chips):
            blk = m_out.at[2 * cx + cy, 1 - c]
            _remote(blk, blk, send_sems, recv_sems, 3 + r, sibling).wait_recv()
        for cp in sends:
            cp.wait_send()

        @pl.when(c == 0)
        def _():
            forward()
            drain()

        pl.when(c == 1)(receive)

    nsem = 6 + 6 * npk
    res = pl.pallas_call(
        body, name=name, in_specs=[ANY] * (npk + 1), out_specs=[ANY] * (npk + 1),
        out_shape=[SDS((N_CHIPS,) + meta.shape, meta.dtype)] + [SDS((N_CHIPS,) + a.shape[1:], a.dtype) for a in arrs],
        scratch_shapes=[pltpu.SemaphoreType.DMA((nsem,)), pltpu.SemaphoreType.DMA((nsem,))],
        compiler_params=pltpu.CompilerParams(has_side_effects=True))(meta, *arrs)
    return _fill_own(res[:1], [meta])[0], _fill_own(res[1:], [a[0] for a in arrs])


def _pair_exchange(gs, name):
    npk = len(gs)

    def body(*refs):
        ins, outs = refs[:npk], refs[npk:2 * npk]
        send_sems, recv_sems = refs[2 * npk:]
        x, y, c = _me()
        cps = [_remote(ins[p].at[:, 1 - c], outs[p], send_sems, recv_sems, p, (x, y, 1 - c)) for p in range(npk)]
        for cp in cps:
            cp.start()
        for cp in cps:
            cp.wait()

    return pl.pallas_call(
        body, name=name, in_specs=[ANY] * npk, out_specs=[ANY] * npk,
        out_shape=[SDS((N_CHIPS,) + g.shape[2:], g.dtype) for g in gs],
        scratch_shapes=[pltpu.SemaphoreType.DMA((npk,)), pltpu.SemaphoreType.DMA((npk,))],
        compiler_params=pltpu.CompilerParams(has_side_effects=True))(*gs)


def _chip_exchange(ss, small, name):
    npk = len(ss)

    def body(*refs):
        ins, sm_ref = refs[:npk], refs[npk]
        outs, sa_ref = refs[npk + 1:2 * npk + 1], refs[2 * npk + 1]
        send_sems, recv_sems, loc_sem = refs[2 * npk + 2:]
        x, y, c = _me()
        me = 4 * x + 2 * y + c
        lc = pltpu.make_async_copy(sm_ref, sa_ref.at[me], loc_sem.at[0])
        lc.start()
        cps = []
        for p in range(npk):
            for r, (cx, cy) in enumerate([(1 - x, y), (x, 1 - y), (1 - x, 1 - y)]):
                cp = _remote(ins[p].at[2 * cx + cy], outs[p].at[r], send_sems, recv_sems, 3 * p + r, (cx, cy, c))
                cp.start()
                cps.append(cp)
        base = 3 * npk - 1
        for mask in range(1, N_DEV):
            px, py, pc = x ^ (mask >> 2), y ^ ((mask >> 1) & 1), c ^ (mask & 1)
            cp = _remote(sm_ref, sa_ref.at[me], send_sems, recv_sems, base + mask, (px, py, pc))
            cp.start()
            cps.append(cp)
        for p in range(npk):
            for r in range(3):
                _remote(outs[p].at[r], outs[p].at[r], send_sems, recv_sems, 3 * p + r, (x, y, c)).wait_recv()
        for mask in range(1, N_DEV):
            src = 4 * (x ^ (mask >> 2)) + 2 * (y ^ ((mask >> 1) & 1)) + (c ^ (mask & 1))
            _remote(sa_ref.at[src], sa_ref.at[src], send_sems, recv_sems, base + mask, (x, y, c)).wait_recv()
        for cp in cps:
            cp.wait_send()
        lc.wait()

    nsem = 3 * npk + N_DEV - 1
    res = pl.pallas_call(
        body, name=name, in_specs=[ANY] * (npk + 1), out_specs=[ANY] * (npk + 1),
        out_shape=[SDS((3,) + s.shape[1:], s.dtype) for s in ss] + [SDS((N_DEV,) + small.shape, small.dtype)],
        scratch_shapes=[pltpu.SemaphoreType.DMA((nsem,)), pltpu.SemaphoreType.DMA((nsem,)),
                        pltpu.SemaphoreType.DMA((1,))],
        compiler_params=pltpu.CompilerParams(has_side_effects=True))(*ss, small)
    return res[:npk], res[npk]


def _half_exchange(ghs, name):
    npk = len(ghs)

    def body(*refs):
        ins, outs = refs[:npk], refs[npk:2 * npk]
        send_sems, recv_sems = refs[2 * npk:]
        x, y, c = _me()
        cps = [_remote(ins[p], outs[p], send_sems, recv_sems, p, (x, y, 1 - c)) for p in range(npk)]
        for cp in cps:
            cp.start()
        for cp in cps:
            cp.wait()

    return pl.pallas_call(
        body, name=name, in_specs=[ANY] * npk, out_specs=[ANY] * npk, out_shape=[SDS(g.shape, g.dtype) for g in ghs],
        scratch_shapes=[pltpu.SemaphoreType.DMA((npk,)), pltpu.SemaphoreType.DMA((npk,))],
        compiler_params=pltpu.CompilerParams(has_side_effects=True))(*ghs)


def _add_tile(rows, cols):
    return _div_tile(rows, max(16, (1 << 19) // max(cols, LANES)), 16)


def _pair_add(g, r1, c_idx, name):
    _, rows, cols = r1.shape
    tr = _add_tile(rows, cols)

    def body(c_ref, g_ref, r_ref, o_ref, ob_ref):
        s = g_ref[0] + r_ref[...]
        o_ref[...] = s
        ob_ref[...] = s.astype(BF16)

    own = BS((1, tr, cols), lambda k, i, c: (k, i, 0))
    return pl.pallas_call(
        body, name=name,
        grid_spec=pltpu.PrefetchScalarGridSpec(
            num_scalar_prefetch=1, grid=(N_CHIPS, rows // tr),
            in_specs=[BS((1, 1, tr, cols), lambda k, i, c: (k, c[0], i, 0)), own], out_specs=[own, own]),
        out_shape=[SDS(r1.shape, F32), SDS(r1.shape, BF16)],
        compiler_params=_cparams(("parallel", "parallel")))(c_idx, g, r1)


def _chip_add(s1, r2, j_idx, name):
    _, rows, cols = s1.shape
    tr = _add_tile(rows, cols)

    def body(j_ref, s_ref, r_ref, o_ref):
        o_ref[...] = ((s_ref[0] + r_ref[0].astype(F32)) + r_ref[1].astype(F32)) + r_ref[2].astype(F32)

    return pl.pallas_call(
        body, name=name,
        grid_spec=pltpu.PrefetchScalarGridSpec(
            num_scalar_prefetch=1, grid=(rows // tr,),
            in_specs=[BS((1, tr, cols), lambda i, j: (j[0], i, 0)), BS((3, tr, cols), lambda i, j: (0, i, 0))],
            out_specs=BS((tr, cols), lambda i, j: (i, 0))),
        out_shape=SDS((rows, cols), F32), compiler_params=_cparams(("parallel",)))(j_idx, s1, r2)


def _adamw_math(w, g, m, v):
    m = ADAM_B1 * m + (1.0 - ADAM_B1) * g
    v = ADAM_B2 * v + (1.0 - ADAM_B2) * (g * g)
    m_hat = m / (1.0 - ADAM_B1 ** ADAM_STEP)
    v_hat = v / (1.0 - ADAM_B2 ** ADAM_STEP)
    delta = -ADAM_LR * (m_hat / (jnp.sqrt(v_hat) + ADAM_EPS) + ADAM_WD * w)
    return delta, m, v


def _adamw(w, gh, go, m, v, c_idx, name):
    _, rows, cols = w.shape
    tr = _add_tile(rows, cols)

    def body(c_ref, w_ref, gh_ref, go_ref, m_ref, v_ref, g_out, d_out, m_out, v_out):
        g = jnp.where(pl.program_id(0) == c_ref[0], gh_ref[...], go_ref[...])
        g_out[0] = g
        d_out[0], m_out[0], v_out[0] = _adamw_math(w_ref[0], g, m_ref[0], v_ref[0])

    full = BS((1, tr, cols), lambda hf, i, c: (hf, i, 0))
    half = BS((tr, cols), lambda hf, i, c: (i, 0))
    return pl.pallas_call(
        body, name=name,
        grid_spec=pltpu.PrefetchScalarGridSpec(
            num_scalar_prefetch=1, grid=(2, rows // tr), in_specs=[full, half, half, full, full],
            out_specs=[full] * 4),
        out_shape=[SDS(w.shape, F32)] * 4, compiler_params=_cparams(("parallel", "parallel")))(c_idx, w, gh, go, m, v)


def _adamw_layers(w, gh0, go0, g1, m, v, c_idx, name):
    _, _, rows, cols = w.shape
    tr = _add_tile(rows, cols)

    def body(c_ref, w_ref, gh_ref, go_ref, g1_ref, m_ref, v_ref, g_out, d_out, m_out, v_out):
        g0 = jnp.where(pl.program_id(1) == c_ref[0], gh_ref[...], go_ref[...])
        g = jnp.where(pl.program_id(0) == 0, g0, g1_ref[0])
        g_out[0, 0] = g
        d_out[0, 0], m_out[0, 0], v_out[0, 0] = _adamw_math(w_ref[0, 0], g, m_ref[0, 0], v_ref[0, 0])

    full = BS((1, 1, tr, cols), lambda l, hf, i, c: (l, hf, i, 0))
    half = BS((tr, cols), lambda l, hf, i, c: (i, 0))
    return pl.pallas_call(
        body, name=name,
        grid_spec=pltpu.PrefetchScalarGridSpec(
            num_scalar_prefetch=1, grid=(2, 2, rows // tr),
            in_specs=[full, half, half, BS((1, tr, cols), lambda l, hf, i, c: (hf, i, 0)), full, full],
            out_specs=[full] * 4),
        out_shape=[SDS(w.shape, F32)] * 4,
        compiler_params=_cparams(("parallel", "parallel", "parallel")))(c_idx, w, gh0, go0, g1, m, v)


def _adamw_two(w, g0, g1, m, v, name):
    _, _, rows, cols = w.shape
    tr = _add_tile(rows, cols)

    def body(w_ref, g0_ref, g1_ref, m_ref, v_ref, g_out, d_out, m_out, v_out):
        g = jnp.where(pl.program_id(0) == 0, g0_ref[0], g1_ref[0])
        g_out[0, 0] = g
        d_out[0, 0], m_out[0, 0], v_out[0, 0] = _adamw_math(w_ref[0, 0], g, m_ref[0, 0], v_ref[0, 0])

    full = BS((1, 1, tr, cols), lambda l, hf, i: (l, hf, i, 0))
    half = BS((1, tr, cols), lambda l, hf, i: (hf, i, 0))
    return pl.pallas_call(
        body, name=name, grid=(2, 2, rows // tr), in_specs=[full, half, half, full, full], out_specs=[full] * 4,
        out_shape=[SDS(w.shape, F32)] * 4,
        compiler_params=_cparams(("parallel", "parallel", "parallel")))(w, g0, g1, m, v)


def _sum_devices(sa, name):
    def body(sa_ref, g_out):
        g = sa_ref[0]
        for d in range(1, N_DEV):
            g = g + sa_ref[d]
        g_out[...] = g

    return pl.pallas_call(body, name=name, out_shape=SDS(sa.shape[1:], F32),
                          compiler_params=pltpu.CompilerParams(vmem_limit_bytes=VMEM_LIMIT))(sa)


def _adamw_small(ws, gs, ms, vs, name):
    k = len(ws)

    def body(*refs):
        ins, outs = refs[:4 * k], refs[4 * k:]
        for i in range(k):
            d, m, v = _adamw_math(ins[i][...], ins[k + i][...], ins[2 * k + i][...], ins[3 * k + i][...])
            outs[i][...], outs[k + i][...], outs[2 * k + i][...] = d, m, v

    return pl.pallas_call(body, name=name, out_shape=[SDS(w.shape, F32) for w in ws] * 3,
                          compiler_params=pltpu.CompilerParams(vmem_limit_bytes=VMEM_LIMIT))(*ws, *gs, *ms, *vs)


HALVED = {"w_in": (2, 1024, 1450), "mla_w_q_up": (2, 256, 192), "mla_w_kv_up": (2, 128, 256),
          "w_branch": (2, 1536, 256), "w_o": (2, 256, 1024), "ffn_w_up": (2, 1024, 1408),
          "ffn_w_down": (2, 704, 1024), "ffn_conv_w": (2, 3, 1408), "meta_tokens": (2, 8, 256)}
SMALL_SHAPE = {"norm1_g": (2, 1024), "fox_forget_b": (2, 8), "fox_q_g": (2, 64), "fox_k_g": (2, 64),
               "mla_q_a_g": (2, 256), "mla_kv_a_g": (2, 128), "mla_q_g": (2, 96), "mla_k_g": (2, 96),
               "swa_q_g": (2, 64), "swa_k_g": (2, 64), "swa_sinks": (2, 8), "norm2_g": (2, 1024),
               "ffn_conv_b": (2, 5632)}
SMALL_ROWS = _pack_rows([SMALL_SHAPE[k] for k in SMALL] + [(1,)], SUBLANES)


def kernel(x, meta_tokens, norm1_g, w_in, fox_forget_b, fox_q_g, fox_k_g, mla_q_a_g, mla_w_q_up, mla_kv_a_g, mla_w_kv_up, mla_q_g, mla_k_g, swa_q_g, swa_k_g, swa_sinks, w_branch, w_o, norm2_g, ffn_w_up, ffn_conv_w, ffn_conv_b, ffn_w_down, loss_target, m_meta_tokens, m_norm1_g, m_w_in, m_fox_forget_b, m_fox_q_g, m_fox_k_g, m_mla_q_a_g, m_mla_w_q_up, m_mla_kv_a_g, m_mla_w_kv_up, m_mla_q_g, m_mla_k_g, m_swa_q_g, m_swa_k_g, m_swa_sinks, m_w_branch, m_w_o, m_norm2_g, m_ffn_w_up, m_ffn_conv_w, m_ffn_conv_b, m_ffn_w_down, v_meta_tokens, v_norm1_g, v_w_in, v_fox_forget_b, v_fox_q_g, v_fox_k_g, v_mla_q_a_g, v_mla_w_q_up, v_mla_kv_a_g, v_mla_w_kv_up, v_mla_q_g, v_mla_k_g, v_swa_q_g, v_swa_k_g, v_swa_sinks, v_w_branch, v_w_o, v_norm2_g, v_ffn_w_up, v_ffn_conv_w, v_ffn_conv_b, v_ffn_w_down):
    w = dict(meta_tokens=meta_tokens, norm1_g=norm1_g, w_in=w_in, fox_forget_b=fox_forget_b, fox_q_g=fox_q_g,
             fox_k_g=fox_k_g, mla_q_a_g=mla_q_a_g, mla_w_q_up=mla_w_q_up, mla_kv_a_g=mla_kv_a_g,
             mla_w_kv_up=mla_w_kv_up, mla_q_g=mla_q_g, mla_k_g=mla_k_g, swa_q_g=swa_q_g, swa_k_g=swa_k_g,
             swa_sinks=swa_sinks, w_branch=w_branch, w_o=w_o, norm2_g=norm2_g, ffn_w_up=ffn_w_up,
             ffn_conv_w=ffn_conv_w, ffn_conv_b=ffn_conv_b, ffn_w_down=ffn_w_down)
    m = dict(meta_tokens=m_meta_tokens, norm1_g=m_norm1_g, w_in=m_w_in, fox_forget_b=m_fox_forget_b,
             fox_q_g=m_fox_q_g, fox_k_g=m_fox_k_g, mla_q_a_g=m_mla_q_a_g, mla_w_q_up=m_mla_w_q_up,
             mla_kv_a_g=m_mla_kv_a_g, mla_w_kv_up=m_mla_w_kv_up, mla_q_g=m_mla_q_g, mla_k_g=m_mla_k_g,
             swa_q_g=m_swa_q_g, swa_k_g=m_swa_k_g, swa_sinks=m_swa_sinks, w_branch=m_w_branch, w_o=m_w_o,
             norm2_g=m_norm2_g, ffn_w_up=m_ffn_w_up, ffn_conv_w=m_ffn_conv_w, ffn_conv_b=m_ffn_conv_b,
             ffn_w_down=m_ffn_w_down)
    v = dict(meta_tokens=v_meta_tokens, norm1_g=v_norm1_g, w_in=v_w_in, fox_forget_b=v_fox_forget_b,
             fox_q_g=v_fox_q_g, fox_k_g=v_fox_k_g, mla_q_a_g=v_mla_q_a_g, mla_w_q_up=v_mla_w_q_up,
             mla_kv_a_g=v_mla_kv_a_g, mla_w_kv_up=v_mla_w_kv_up, mla_q_g=v_mla_q_g, mla_k_g=v_mla_k_g,
             swa_q_g=v_swa_q_g, swa_k_g=v_swa_k_g, swa_sinks=v_swa_sinks, w_branch=v_w_branch, w_o=v_w_o,
             norm2_g=v_norm2_g, ffn_w_up=v_ffn_w_up, ffn_conv_w=v_ffn_conv_w, ffn_conv_b=v_ffn_conv_b,
             ffn_w_down=v_ffn_w_down)
    xi, yi, ci = _me()
    c_idx = ci.astype(jnp.int32).reshape(1)
    j_idx = (2 * xi + yi).astype(jnp.int32).reshape(1)

    sh_names = BIG + FINE
    local = {k: (w[k].astype(BF16) if k in BIG else w[k]).reshape(HALVED[k]) for k in sh_names}
    late = [local[k] for k in LAYERED]
    meta_g, early = _gather_early(local["meta_tokens"], late, "gather_early")
    meta = jnp.concatenate([meta_g[i].reshape(SHARD_SHAPE["meta_tokens"]) for i in range(N_CHIPS)], axis=1)
    w0 = _assemble_layer(_layer_parts(early))
    small = {k: w[k] for k in SMALL}

    loss, g_x, g_eps, g_red, g_cw, g_meta, g_small = _local_step(x[0], loss_target[0], w0, late, meta, small)
    g_after0, g_l1 = dict(zip(AFTER_MLA, g_red[0])), dict(zip(BIG, g_red[1]))

    quarter = {k: (2, HALVED[k][1] // 2, HALVED[k][2]) for k in BIG}
    last = BEFORE_MLA + FINE
    gs = [g_eps[k].reshape((N_CHIPS,) + quarter[k]) for k in BEFORE_MLA]
    for k, g in (("meta_tokens", g_meta), ("ffn_conv_w", g_cw)):
        gs.append(jnp.stack(jnp.split(g, N_CHIPS, axis=SHARD_AXIS[k])).reshape((N_CHIPS,) + HALVED[k]))
    spack = _pack([g_small[k] for k in SMALL] + [loss.reshape(1)], SMALL_ROWS, F32)
    r1 = _pair_exchange(gs, "grads_pair_exchange")
    s1 = [_pair_add(g, r, c_idx, "grads_pair_add_" + k) for g, r, k in zip(gs, r1, last)]
    r2, sa = _chip_exchange([s[1] for s in s1], spack, "grads_chip_exchange")
    gh = dict(zip(last, [_chip_add(s[0], r, j_idx, "grads_chip_add_" + k) for s, r, k in zip(s1, r2, last)]))
    go = dict(zip(last, _half_exchange([gh[k] for k in last], "grads_half_exchange")))

    grads, deltas, new_m, new_v = {}, {}, {}, {}
    for k in sh_names:
        if k in BIG:
            shp = (2,) + quarter[k]
            wk, mk, vk = w[k].reshape(shp), m[k].reshape(shp), v[k].reshape(shp)
            if k in BEFORE_MLA:
                outs = _adamw_layers(wk, gh[k], go[k], g_l1[k], mk, vk, c_idx, "adamw_" + k)
            else:
                outs = _adamw_two(wk, g_after0[k], g_l1[k], mk, vk, "adamw_" + k)
        else:
            outs = _adamw(w[k].reshape(HALVED[k]), gh[k], go[k], m[k].reshape(HALVED[k]), v[k].reshape(HALVED[k]),
                          c_idx, "adamw_" + k)
        for dst, o in zip((grads, deltas, new_m, new_v), outs):
            dst[k] = o.reshape(SHARD_SHAPE[k])
    sm_shapes = [SMALL_SHAPE[k] for k in SMALL] + [(1,)]
    g_sum = _unpack(_sum_devices(sa, "sum_small"), sm_shapes)
    res = _adamw_small([w[k] for k in SMALL], g_sum[:-1], [m[k] for k in SMALL], [v[k] for k in SMALL], "adamw_small")
    ns = len(SMALL)
    grads.update(zip(SMALL, g_sum[:-1]))
    for dst, vals in zip((deltas, new_m, new_v), (res[:ns], res[ns:2 * ns], res[2 * ns:])):
        dst.update(zip(SMALL, vals))
    total_loss = g_sum[-1][0]
    return (total_loss, g_x[None], *[grads[k] for k in WEIGHTS], *[deltas[k] for k in WEIGHTS],
            *[new_m[k] for k in WEIGHTS], *[new_v[k] for k in WEIGHTS])
```

```python
import functools

import jax
import jax.numpy as jnp
from jax import lax
from jax.experimental import pallas as pl
from jax.experimental.pallas import tpu as pltpu

F32 = jnp.float32
BF16 = jnp.bfloat16
SDS = jax.ShapeDtypeStruct
BS = pl.BlockSpec

D_MODEL = 1024
DEPTH = 2
N_META = 16
EPS = 1e-6
HEADS = 8
HEAD_DIM = 64
MLA_Q_RANK = 256
MLA_KV_RANK = 128
MLA_NOPE = 64
MLA_ROPE = 32
MLA_QK = MLA_NOPE + MLA_ROPE
ROPE_THETA = 10000.0
SWA_KV_HEADS = 2
WINDOW = 128
N_BRANCH = 3
BRANCH_WIDTH = 512
D_FF = 2816
IN_WIDTH = 5800
IN_PAD = 6144
N_CHIPS = 4
N_DEV = 8

ADAM_LR = 0.001
ADAM_B1 = 0.9
ADAM_B2 = 0.999
ADAM_EPS = 1e-08
ADAM_WD = 0.01
ADAM_STEP = 10

LANES = 128
SUBLANES = 8
ROW_PAD = 128
CAUSAL_TILE = 384
NEG = -1e30
VMEM_LIMIT = 56 * 1024 * 1024

O_FQ, O_FK, O_FV, O_FF = 0, 512, 1024, 1536
O_CQ, O_CKV, O_KR = 1664, 1920, 2048
O_SQ, O_SK, O_SV, O_G = 2176, 2688, 2816, 2944

SHARDED = ("meta_tokens", "w_in", "mla_w_q_up", "mla_w_kv_up", "w_branch", "w_o", "ffn_w_up", "ffn_conv_w",
           "ffn_w_down")
SHARD_AXIS = {"meta_tokens": 1, "w_in": 2, "mla_w_q_up": 2, "mla_w_kv_up": 2, "w_branch": 3, "w_o": 1,
              "ffn_w_up": 2, "ffn_conv_w": 2, "ffn_w_down": 1}
SHARD_SHAPE = {"meta_tokens": (16, 256), "w_in": (2, 1024, 1450), "mla_w_q_up": (2, 256, 192),
               "mla_w_kv_up": (2, 128, 256), "w_branch": (2, 3, 512, 256), "w_o": (2, 256, 1024),
               "ffn_w_up": (2, 1024, 1408), "ffn_conv_w": (2, 3, 1408), "ffn_w_down": (2, 704, 1024)}
BIG = ("w_in", "mla_w_q_up", "mla_w_kv_up", "w_branch", "w_o", "ffn_w_up", "ffn_w_down")
FINE = ("meta_tokens", "ffn_conv_w")
SMALL = ("norm1_g", "fox_forget_b", "fox_q_g", "fox_k_g", "mla_q_a_g", "mla_kv_a_g", "mla_q_g", "mla_k_g",
         "swa_q_g", "swa_k_g", "swa_sinks", "norm2_g", "ffn_conv_b")
WEIGHTS = ("meta_tokens", "norm1_g", "w_in", "fox_forget_b", "fox_q_g", "fox_k_g", "mla_q_a_g", "mla_w_q_up",
           "mla_kv_a_g", "mla_w_kv_up", "mla_q_g", "mla_k_g", "swa_q_g", "swa_k_g", "swa_sinks", "w_branch", "w_o",
           "norm2_g", "ffn_w_up", "ffn_conv_w", "ffn_conv_b", "ffn_w_down")


def _cparams(sem):
    return pltpu.CompilerParams(dimension_semantics=sem, vmem_limit_bytes=VMEM_LIMIT)


def _div_tile(n, cap, mult=SUBLANES):
    best = None
    for t in range(mult, min(n, cap) + 1, mult):
        if n % t == 0:
            best = t
    return best if best is not None else n


def _rows_tile(n, width, budget=2 << 20):
    return _div_tile(n, max(SUBLANES, budget // (4 * max(width, LANES))))


def _op(fwd, bwd):
    @jax.custom_vjp
    def op(*args):
        return fwd(*args)[0]
    op.defvjp(fwd, bwd)
    return op


def _rotate(v, cos, sin):
    lane = lax.broadcasted_iota(jnp.int32, v.shape, 1)
    rot = jnp.where(lane < MLA_NOPE + MLA_ROPE // 2, -pltpu.roll(v, LANES - MLA_ROPE // 2, 1),
                    pltpu.roll(v, MLA_ROPE // 2, 1))
    return v * cos + rot * sin


def _rms_fwd_call(x, g, denom, name, rot=None):
    n, c = x.shape
    tr = _rows_tile(n if rot is None else rot[0].shape[0], c)
    nt = None if rot is None else rot[0].shape[0] // tr

    def body(x_ref, g_ref, *rest):
        y_ref = rest[-1]
        xv = x_ref[...]
        ms = jnp.sum(xv * xv, axis=-1, keepdims=True) * (1.0 / denom)
        y = xv * lax.rsqrt(ms + EPS) * g_ref[...]
        y_ref[...] = y if rot is None else _rotate(y, rest[0][...], rest[1][...])

    ins, args = [BS((tr, c), lambda i: (i, 0)), BS((1, c), lambda i: (0, 0))], [x, g]
    if rot is not None:
        ins += [BS((tr, c), lambda i: (i % nt, 0))] * 2
        args += list(rot)
    return pl.pallas_call(
        body, name=name, grid=(n // tr,), in_specs=ins,
        out_specs=BS((tr, c), lambda i: (i, 0)), out_shape=SDS((n, c), F32),
        compiler_params=_cparams(("parallel",)))(*args)


def _rms_bwd_call(x, g, dy, denom, name, rot=None):
    n, c = x.shape
    tr = _rows_tile(n if rot is None else rot[0].shape[0], c)
    nt = None if rot is None else rot[0].shape[0] // tr

    def body(x_ref, g_ref, dy_ref, *rest):
        dx_ref, dg_ref = rest[-2:]
        xv = x_ref[...]
        dy = dy_ref[...]
        if rot is not None:
            dy = _rotate(dy, rest[0][...], -rest[1][...])
        ms = jnp.sum(xv * xv, axis=-1, keepdims=True) * (1.0 / denom)
        r = lax.rsqrt(ms + EPS)
        xh = xv * r
        dxh = dy * g_ref[...]
        dx_ref[...] = r * (dxh - xh * (jnp.sum(dxh * xh, axis=-1, keepdims=True) * (1.0 / denom)))

        @pl.when(pl.program_id(0) == 0)
        def _():
            dg_ref[...] = jnp.zeros_like(dg_ref)

        dg_ref[...] += jnp.sum(dy * xh, axis=0, keepdims=True)

    ins = [BS((tr, c), lambda i: (i, 0)), BS((1, c), lambda i: (0, 0)), BS((tr, c), lambda i: (i, 0))]
    args = [x, g, dy]
    if rot is not None:
        ins += [BS((tr, c), lambda i: (i % nt, 0))] * 2
        args += list(rot)
    return pl.pallas_call(
        body, name=name, grid=(n // tr,), in_specs=ins,
        out_specs=[BS((tr, c), lambda i: (i, 0)), BS((1, c), lambda i: (0, 0))],
        out_shape=[SDS((n, c), F32), SDS((1, c), F32)],
        compiler_params=_cparams(("arbitrary",)))(*args)


def rms_norm(x, g, denom, name):
    def fwd(x, g):
        return _rms_fwd_call(x, g, denom, name + "_f"), (x, g)

    def bwd(res, dy):
        return tuple(_rms_bwd_call(res[0], res[1], dy, denom, name + "_b"))

    return _op(fwd, bwd)(x, g)


def rms_norm_rope(x, g, cos, sin, denom, name):
    def fwd(x, g, cos, sin):
        return _rms_fwd_call(x, g, denom, name + "_f", (cos, sin)), (x, g, cos, sin)

    def bwd(res, dy):
        x, g, cos, sin = res
        dx, dg = _rms_bwd_call(x, g, dy, denom, name + "_b", (cos, sin))
        return dx, dg, jnp.zeros_like(cos), jnp.zeros_like(sin)

    return _op(fwd, bwd)(x, g, cos, sin)


def _mm_call(a, b, mode, res, name):
    if mode == "nn":
        (m, kc), n = a.shape, b.shape[1]
    elif mode == "nt":
        (m, kc), n = a.shape, b.shape[0]
    else:
        (kc, m), n = a.shape, b.shape[1]
    if mode == "tn":
        tk = _div_tile(kc, 528)
        tm = _div_tile(m, 1408, LANES)
        tn = _div_tile(n, 2048, LANES)
    else:
        tk = kc if kc <= 2816 else _div_tile(kc, 1024, LANES)
        tm = _div_tile(m, max(LANES, (9 << 19) // (4 * tk)))
        tn = _div_tile(n, 1408 if mode == "nt" else 512, LANES)
    nk = kc // tk
    dims = {"nn": (((1,), (0,)), ((), ())), "nt": (((1,), (1,)), ((), ())), "tn": (((0,), (0,)), ((), ()))}[mode]

    def body(*refs):
        if res is None:
            a_ref, b_ref, o_ref, acc_ref = refs
            r_ref = None
        else:
            a_ref, b_ref, r_ref, o_ref, acc_ref = refs
        k = pl.program_id(2)

        @pl.when(k == 0)
        def _():
            acc_ref[...] = jnp.zeros_like(acc_ref)

        acc_ref[...] += lax.dot_general(a_ref[...].astype(BF16), b_ref[...].astype(BF16), dims,
                                        preferred_element_type=F32)

        @pl.when(k == nk - 1)
        def _():
            if r_ref is None:
                o_ref[...] = acc_ref[...]
            else:
                o_ref[...] = r_ref[...] + acc_ref[...]

    a_spec = BS((tk, tm), lambda i, j, k: (k, i)) if mode == "tn" else BS((tm, tk), lambda i, j, k: (i, k))
    b_spec = BS((tn, tk), lambda i, j, k: (j, k)) if mode == "nt" else BS((tk, tn), lambda i, j, k: (k, j))
    o_spec = BS((tm, tn), lambda i, j, k: (i, j))
    ins, args = [a_spec, b_spec], [a, b]
    if res is not None:
        ins.append(o_spec)
        args.append(res)
    return pl.pallas_call(
        body, name=name, grid=(m // tm, n // tn, nk), in_specs=ins, out_specs=o_spec,
        out_shape=SDS((m, n), F32), scratch_shapes=[pltpu.VMEM((tm, tn), F32)],
        compiler_params=_cparams(("parallel", "parallel", "arbitrary")))(*args)


def linear(a, w, eps, name, res=None):
    if res is None:
        def fwd(a, w, eps):
            return _mm_call(a, w, "nn", None, name + "_f"), (a, w)

        def bwd(r, dc):
            a, w = r
            return (_mm_call(dc, w, "nt", None, name + "_da"), jnp.zeros_like(w),
                    _mm_call(a, dc, "tn", None, name + "_dw"))

        return _op(fwd, bwd)(a, w, eps)

    def fwd_r(a, w, eps, res):
        return _mm_call(a, w, "nn", res, name + "_f"), (a, w)

    def bwd_r(r, dc):
        a, w = r
        return (_mm_call(dc, w, "nt", None, name + "_da"), jnp.zeros_like(w),
                _mm_call(a, dc, "tn", None, name + "_dw"), dc)

    return _op(fwd_r, bwd_r)(a, w, eps, res)


CT = 128


def _tri_dot(v, upper):
    r = lax.broadcasted_iota(jnp.int32, (CT, CT), 0)
    c = lax.broadcasted_iota(jnp.int32, (CT, CT), 1)
    tri = jnp.where((r <= c) if upper else (r >= c), 1.0, 0.0).astype(F32)
    return jnp.dot(v, tri, preferred_element_type=F32, precision=lax.Precision.HIGHEST)


def _gate_fwd_call(z, b, name):
    h, n = z.shape

    def body(z_ref, b_ref, c_ref, carry):
        @pl.when(pl.program_id(0) == 0)
        def _():
            carry[...] = jnp.zeros_like(carry)

        x = z_ref[...] + b_ref[...]
        ls = jnp.minimum(x, 0.0) - jnp.log(1.0 + jnp.exp(-jnp.abs(x)))
        c_ref[...] = _tri_dot(ls, True) + carry[...]
        carry[...] += jnp.sum(ls, axis=1, keepdims=True)

    return pl.pallas_call(
        body, name=name, grid=(n // CT,),
        in_specs=[BS((h, CT), lambda j: (0, j)), BS((h, 1), lambda j: (0, 0))],
        out_specs=BS((h, CT), lambda j: (0, j)), out_shape=SDS((h, n), F32),
        scratch_shapes=[pltpu.VMEM((h, 1), F32)],
        compiler_params=_cparams(("arbitrary",)))(z, b)


def _gate_bwd_call(z, b, dc, name):
    h, n = z.shape
    nt = n // CT

    def body(z_ref, b_ref, dc_ref, dz_ref, db_ref, carry):
        @pl.when(pl.program_id(0) == 0)
        def _():
            carry[...] = jnp.zeros_like(carry)
            db_ref[...] = jnp.zeros_like(db_ref)

        dcv = dc_ref[...]
        dls = _tri_dot(dcv, False) + carry[...]
        carry[...] += jnp.sum(dcv, axis=1, keepdims=True)
        x = z_ref[...] + b_ref[...]
        e = jnp.exp(-jnp.abs(x))
        dz = dls * jnp.where(x >= 0, e / (1.0 + e), 1.0 / (1.0 + e))
        dz_ref[...] = dz
        db_ref[...] += jnp.sum(dz, axis=1, keepdims=True)

    rev = lambda j: (0, nt - 1 - j)
    return pl.pallas_call(
        body, name=name, grid=(nt,),
        in_specs=[BS((h, CT), rev), BS((h, 1), lambda j: (0, 0)), BS((h, CT), rev)],
        out_specs=[BS((h, CT), rev), BS((h, 1), lambda j: (0, 0))],
        out_shape=[SDS((h, n), F32), SDS((h, 1), F32)],
        scratch_shapes=[pltpu.VMEM((h, 1), F32)],
        compiler_params=_cparams(("arbitrary",)))(z, b, dc)


def forget_cumsum(z, b, name):
    def fwd(z, b):
        return _gate_fwd_call(z, b, name + "_f"), (z, b)

    def bwd(res, dc):
        return tuple(_gate_bwd_call(res[0], res[1], dc, name + "_b"))

    return _op(fwd, bwd)(z, b)


NT_DIMS = (((1,), (1,)), ((), ()))
TN_DIMS = (((0,), (0,)), ((), ()))
HEADS_PER_STEP = 2


def _causal_tile(n):
    return CAUSAL_TILE if n % CAUSAL_TILE == 0 else ROW_PAD


def _causal_fwd_call(q, k, v, ck_r, fox, scale, name, late=None, late_layer=1):
    h, n, dk = q.shape
    dv = v.shape[2]
    t = _causal_tile(n)
    nq = n // t
    hb = HEADS_PER_STEP
    nl = 0 if late is None else len(late)
    n_in = 3 + int(fox) + nl

    def body(*refs):
        q_ref, k_ref, v_ref = refs[:3]
        ck_ref = refs[3] if fox else None
        o_ref, lse_ref = refs[n_in:n_in + 2]
        m_scr, l_scr, acc_scr = refs[n_in + 2 + nl:n_in + 5 + nl]
        qi = pl.program_id(1)
        if nl:
            start, forward, drain, receive = _layer_gather(refs[n_in - nl:n_in], refs[n_in + 2:n_in + 2 + nl],
                                                           refs[-2], refs[-1], late_layer, 0)
            hp, core = pl.program_id(0), lax.axis_index("c")
            last = (hp == h // hb - 1) & (qi == nq - 1)
            pl.when((hp == 0) & (qi == 0) & (core == late_layer))(start)
            pl.when((hp == h // hb // 2) & (qi == 0) & (core == late_layer))(forward)
            pl.when(last & (core == late_layer))(drain)
            pl.when(last & (core == 1 - late_layer))(receive)
        qbs = [q_ref[e].astype(BF16) for e in range(hb)]
        m_scr[...] = jnp.full(m_scr.shape, NEG, F32)
        l_scr[...] = jnp.zeros_like(l_scr)
        acc_scr[...] = jnp.zeros_like(acc_scr)

        def process(j, masked):
            off = pl.multiple_of(j * t, t)
            if masked:
                rows = lax.broadcasted_iota(jnp.int32, (t, t), 0)
                cols = lax.broadcasted_iota(jnp.int32, (t, t), 1)
                valid = cols <= rows
            for e in range(hb):
                kb = k_ref[e, pl.ds(off, t), :].astype(BF16)
                vb = v_ref[e, pl.ds(off, t), :].astype(BF16)
                s = lax.dot_general(qbs[e], kb, NT_DIMS, preferred_element_type=F32) * scale
                if fox:
                    s = s - ck_ref[e, j]
                if masked:
                    s = jnp.where(valid, s, NEG)
                m_old = m_scr[e]
                m_new = jnp.maximum(m_old, jnp.max(s, axis=1, keepdims=True))
                alpha = jnp.exp(m_old - m_new)
                p = jnp.exp(s - jnp.tile(m_new, (1, t // LANES)))
                l_scr[e] = alpha * l_scr[e] + jnp.sum(p, axis=1, keepdims=True)
                acc_scr[e] = alpha[:, :dv] * acc_scr[e] + jnp.dot(p.astype(BF16), vb, preferred_element_type=F32)
                m_scr[e] = m_new

        def step(j, carry):
            process(j, False)
            return carry

        lax.fori_loop(0, qi, step, 0)
        process(qi, True)
        for e in range(hb):
            l = l_scr[e]
            o_ref[e] = acc_scr[e] / l[:, :dv]
            lse_ref[e, 0] = jnp.transpose(m_scr[e] + jnp.log(l))[0:1, :]

    ins = [BS((hb, t, dk), lambda a, b: (a, b, 0)), BS((hb, n, dk), lambda a, b: (a, 0, 0)),
           BS((hb, n, dv), lambda a, b: (a, 0, 0))]
    args = [q, k, v]
    if fox:
        ins.append(BS((hb, nq, 1, t), lambda a, b: (a, 0, 0, 0)))
        args.append(ck_r)
    outs = [BS((hb, t, dv), lambda a, b: (a, b, 0)), BS((hb, 1, 1, t), lambda a, b: (a, b, 0, 0))]
    oshape = [SDS((h, n, dv), F32), SDS((h, nq, 1, t), F32)]
    scratch = [pltpu.VMEM((hb, t, LANES), F32), pltpu.VMEM((hb, t, LANES), F32), pltpu.VMEM((hb, t, dv), F32)]
    if nl:
        ins += [ANY] * nl
        args += list(late)
        outs += [ANY] * nl
        oshape += [SDS((N_CHIPS,) + a.shape[1:], a.dtype) for a in late]
        scratch += [pltpu.SemaphoreType.DMA((6 * nl,)), pltpu.SemaphoreType.DMA((6 * nl,))]
    res = pl.pallas_call(
        body, name=name, grid=(h // hb, nq), in_specs=ins, out_specs=outs, out_shape=oshape, scratch_shapes=scratch,
        compiler_params=pltpu.CompilerParams(dimension_semantics=("arbitrary", "arbitrary"),
                                             vmem_limit_bytes=VMEM_LIMIT, has_side_effects=bool(nl)))(*args)
    return res[0], res[1], list(res[2:])


def _causal_bwd_call(q, k, v, do, o, lse_r, ck_r, fox, scale, name, side=None):
    h, n, dk = q.shape
    dv = v.shape[2]
    t = _causal_tile(n)
    nq = n // t
    hb = HEADS_PER_STEP
    ns = 0 if side is None else len(side)

    def body(*refs):
        it = iter(refs)
        q_ref, k_ref, v_ref, do_ref, o_ref, lse_ref = (next(it) for _ in range(6))
        ck_ref = next(it) if fox else None
        side_in = [next(it) for _ in range(ns)]
        dq_ref, dk_ref, dv_ref = next(it), next(it), next(it)
        dck_ref, dcq_ref = (next(it), next(it)) if fox else (None, None)
        side_out = [next(it) for _ in range(ns)]
        delta_scr, dk_scr, dv_scr = next(it), next(it), next(it)
        dck_scr = next(it) if fox else None
        kj = pl.program_id(1)
        if ns:
            send_sems, recv_sems = next(it), next(it)
            x, y, core = _me()
            chips = [(1 - x, y), (x, 1 - y), (1 - x, 1 - y)]
            copies = [_remote(side_in[p].at[2 * cx + cy], side_out[p].at[r], send_sems, recv_sems, 3 * p + r,
                              (cx, cy, core)) for p in range(ns) for r, (cx, cy) in enumerate(chips)]

            @pl.when((pl.program_id(0) == 0) & (kj == 0))
            def _():
                for cp in copies:
                    cp.start()

            @pl.when((pl.program_id(0) == h // hb - 1) & (kj == nq - 1))
            def _():
                for cp in copies:
                    cp.wait()

        @pl.when(kj == 0)
        def _():
            dq_ref[...] = jnp.zeros_like(dq_ref)
            if fox:
                dcq_ref[...] = jnp.zeros_like(dcq_ref)
            ones = jnp.ones((SUBLANES, dv), F32)

            def fill(qi, carry):
                off = pl.multiple_of(qi * t, t)
                for e in range(hb):
                    prod = do_ref[e, pl.ds(off, t), :] * o_ref[e, pl.ds(off, t), :]
                    delta_scr[e, qi] = lax.dot_general(ones, prod, NT_DIMS, preferred_element_type=F32,
                                                       precision=lax.Precision.HIGHEST)[0:1, :]
                return carry

            lax.fori_loop(0, nq, fill, 0)

        kbs = [k_ref[e].astype(BF16) for e in range(hb)]
        vbs = [v_ref[e].astype(BF16) for e in range(hb)]
        dk_scr[...] = jnp.zeros_like(dk_scr)
        dv_scr[...] = jnp.zeros_like(dv_scr)
        if fox:
            dck_scr[...] = jnp.zeros_like(dck_scr)
            ckcs = [jnp.tile(jnp.transpose(jnp.broadcast_to(ck_ref[e, 0], (LANES, t))), (1, t // LANES))
                    for e in range(hb)]

        def process(qi, masked):
            off = pl.multiple_of(qi * t, t)
            if masked:
                krows = lax.broadcasted_iota(jnp.int32, (t, t), 0)
                qcols = lax.broadcasted_iota(jnp.int32, (t, t), 1)
                valid = krows <= qcols
            for e in range(hb):
                qb = q_ref[e, pl.ds(off, t), :].astype(BF16)
                dob = do_ref[e, pl.ds(off, t), :].astype(BF16)
                st = lax.dot_general(kbs[e], qb, NT_DIMS, preferred_element_type=F32) * scale
                if fox:
                    st = st - ckcs[e]
                pt = jnp.exp(st - lse_ref[e, qi])
                if masked:
                    pt = jnp.where(valid, pt, 0.0)
                dv_scr[e] += jnp.dot(pt.astype(BF16), dob, preferred_element_type=F32)
                dpt = lax.dot_general(vbs[e], dob, NT_DIMS, preferred_element_type=F32)
                dst = pt * (dpt - delta_scr[e, qi])
                if fox:
                    dck_scr[e] -= jnp.sum(dst, axis=1, keepdims=True)
                    dcq_ref[e, qi] += jnp.sum(dst, axis=0, keepdims=True)
                dsb = (dst * scale).astype(BF16)
                dk_scr[e] += jnp.dot(dsb, qb, preferred_element_type=F32)
                dq_ref[e, pl.ds(off, t), :] += lax.dot_general(dsb, kbs[e], TN_DIMS, preferred_element_type=F32)

        def step(qi, carry):
            process(qi, False)
            return carry

        process(kj, True)
        lax.fori_loop(kj + 1, nq, step, 0)
        dk_ref[...] = dk_scr[...]
        dv_ref[...] = dv_scr[...]
        if fox:
            for e in range(hb):
                dck_ref[e, 0] = jnp.transpose(jnp.broadcast_to(dck_scr[e], (t, LANES)))[0:1, :]

    whole = lambda a, b: (a, 0, 0)
    tile = lambda a, b: (a, b, 0)
    rowv = lambda a, b: (a, 0, 0, 0)
    rowt = lambda a, b: (a, b, 0, 0)
    ins = [BS((hb, n, dk), whole), BS((hb, t, dk), tile), BS((hb, t, dv), tile), BS((hb, n, dv), whole),
           BS((hb, n, dv), whole), BS((hb, nq, 1, t), rowv)]
    args = [q, k, v, do, o, lse_r]
    outs = [BS((hb, n, dk), whole), BS((hb, t, dk), tile), BS((hb, t, dv), tile)]
    oshape = [SDS((h, n, dk), F32), SDS((h, n, dk), F32), SDS((h, n, dv), F32)]
    scratch = [pltpu.VMEM((hb, nq, 1, t), F32), pltpu.VMEM((hb, t, dk), F32), pltpu.VMEM((hb, t, dv), F32)]
    if fox:
        ins.append(BS((hb, 1, 1, t), rowt))
        args.append(ck_r)
        outs += [BS((hb, 1, 1, t), rowt), BS((hb, nq, 1, t), rowv)]
        oshape += [SDS((h, nq, 1, t), F32), SDS((h, nq, 1, t), F32)]
        scratch.append(pltpu.VMEM((hb, t, 1), F32))
    if ns:
        ins += [ANY] * ns
        args += list(side)
        outs += [ANY] * ns
        oshape += [SDS((3,) + s.shape[1:], s.dtype) for s in side]
        scratch += [pltpu.SemaphoreType.DMA((3 * ns,)), pltpu.SemaphoreType.DMA((3 * ns,))]
    return pl.pallas_call(
        body, name=name, grid=(h // hb, nq), in_specs=ins, out_specs=outs, out_shape=oshape, scratch_shapes=scratch,
        compiler_params=pltpu.CompilerParams(dimension_semantics=("arbitrary", "arbitrary"),
                                             vmem_limit_bytes=VMEM_LIMIT, has_side_effects=bool(ns)))(*args)


def causal_attention(q, k, v, c, scale, name, late=None, late_layer=1, reduce=None, sinks=None):
    h, n, _ = q.shape
    t = _causal_tile(n)
    nq = n // t
    fox = c is not None

    def run_fwd(q, k, v, c, late, sinks):
        ck_r = c.reshape(h, nq, 1, t) if fox else None
        o, lse, got = _causal_fwd_call(q, k, v, ck_r, fox, scale, name + "_f", late, late_layer)
        out = (o,)
        if late is not None:
            out += (got,)
        if reduce is not None:
            out += ([jnp.zeros((N_CHIPS,) + SHARD_SHAPE[w][1:], F32) for w in reduce],)
        return (out if len(out) > 1 else o), (q, k, v, c, late, o, lse)

    def run_bwd(res, ct):
        q, k, v, c, late, o, lse = res
        ck_r = c.reshape(h, nq, 1, t) if fox else None
        dlate = None if late is None else [jnp.zeros_like(a) for a in late]
        if reduce is None:
            do = ct if late is None else ct[0]
            outs = _causal_bwd_call(q, k, v, do, o, lse, ck_r, fox, scale, name + "_b")
            return outs[0], outs[1], outs[2], ((outs[3] + outs[4]).reshape(h, n) if fox else None), dlate, None
        do, g1 = ct[0], ct[-1]
        xi, yi, ci = _me()
        c_idx = ci.astype(jnp.int32).reshape(1)
        j_idx = (2 * xi + yi).astype(jnp.int32).reshape(1)
        gs = [g.reshape(N_CHIPS, 2, HALVED[w][1] // 2, HALVED[w][2]) for g, w in zip(g1, reduce)]
        r1 = _pair_exchange(gs, name + "_pair_exchange")
        s1 = [_pair_add(g, r, c_idx, name + "_pair_add_" + w) for g, r, w in zip(gs, r1, reduce)]
        outs = _causal_bwd_call(q, k, v, do, o, lse, ck_r, fox, scale, name + "_b", [s[1] for s in s1])
        gh = [_chip_add(s[0], r, j_idx, name + "_chip_add_" + w) for s, r, w in zip(s1, outs[-len(reduce):], reduce)]
        go = _half_exchange(gh, name + "_half_exchange")
        full = [jnp.where(ci == 0, jnp.stack([a, b]), jnp.stack([b, a])) for a, b in zip(gh, go)]
        return outs[0], outs[1], outs[2], ((outs[3] + outs[4]).reshape(h, n) if fox else None), dlate, full

    return _op(run_fwd, run_bwd)(q, k, v, c, late, sinks)


SWA_T = 128


def _swa_masks(qi):
    t = SWA_T
    r = lax.broadcasted_iota(jnp.int32, (t, 3 * t), 0)
    c = lax.broadcasted_iota(jnp.int32, (t, 3 * t), 1)
    seg0 = c < t
    seg1 = (c >= t) & (c < 2 * t)
    jp = jnp.maximum(qi - 1, 0)
    kpos = jnp.where(seg0, c, jnp.where(seg1, jp * t + c - t, qi * t + c - 2 * t))
    dist = qi * t + r - kpos
    band = (dist >= 0) & ((dist < WINDOW) | (kpos < N_META))
    valid = (seg0 & (kpos < N_META) & (qi >= 2)) | (jnp.logical_not(seg0) & band & (jnp.logical_not(seg1) | (qi >= 1)))
    return valid, dist.astype(F32)


def _swa_cat(ref, qi):
    t = SWA_T
    jp = jnp.maximum(qi - 1, 0)
    return jnp.concatenate([ref[0, 0:t, :], ref[0, pl.ds(pl.multiple_of(jp * t, t), t), :],
                            ref[0, pl.ds(pl.multiple_of(qi * t, t), t), :]], axis=0).astype(BF16)


def _swa_fwd_call(q, k, v, sinks, slopes, scale, name):
    hq, n, d = q.shape
    hkv = k.shape[0]
    g = hq // hkv
    t = SWA_T
    nq = n // t

    def body(q_ref, k_ref, v_ref, sink_ref, slope_ref, o_ref, lse_ref):
        grp = pl.program_id(0)
        qi = pl.program_id(1)
        valid, dist = _swa_masks(qi)
        kc = _swa_cat(k_ref, qi)
        vc = _swa_cat(v_ref, qi)
        qs = jnp.concatenate([q_ref[e] for e in range(g)], axis=0).astype(BF16)
        s_all = lax.dot_general(qs, kc, NT_DIMS, preferred_element_type=F32) * scale
        ps, ls, ms = [], [], []
        for e in range(g):
            hh = grp * g + e
            s = jnp.where(valid, s_all[e * t:(e + 1) * t] - slope_ref[hh] * dist, NEG)
            m = jnp.maximum(jnp.max(s, axis=1, keepdims=True), sink_ref[hh])
            p = jnp.exp(s - m)
            ls.append(jnp.sum(p, axis=1, keepdims=True) + jnp.exp(sink_ref[hh] - m))
            ms.append(m)
            ps.append(p.astype(BF16))
        acc = jnp.dot(jnp.concatenate(ps, axis=0), vc, preferred_element_type=F32)
        for e in range(g):
            o_ref[e] = acc[e * t:(e + 1) * t] / ls[e]
            lse_ref[e] = ms[e] + jnp.log(ls[e])

    return pl.pallas_call(
        body, name=name, grid=(hkv, nq),
        in_specs=[BS((g, t, d), lambda a, b: (a, b, 0)), BS((1, n, d), lambda a, b: (a, 0, 0)),
                  BS((1, n, d), lambda a, b: (a, 0, 0)), BS(memory_space=pltpu.SMEM), BS(memory_space=pltpu.SMEM)],
        out_specs=[BS((g, t, d), lambda a, b: (a, b, 0)), BS((g, t, 1), lambda a, b: (a, b, 0))],
        out_shape=[SDS((hq, n, d), F32), SDS((hq, n, 1), F32)],
        compiler_params=_cparams(("parallel", "parallel")))(q, k, v, sinks, slopes)


def _swa_bwd_call(q, k, v, o, lse, do, sinks, slopes, scale, name):
    hq, n, d = q.shape
    hkv = k.shape[0]
    g = hq // hkv
    t = SWA_T
    nq = n // t

    def body(q_ref, k_ref, v_ref, o_ref, lse_ref, do_ref, sink_ref, slope_ref, dq_ref, dk_ref, dv_ref, ds_ref):
        grp = pl.program_id(0)
        qi = pl.program_id(1)

        @pl.when(qi == 0)
        def _():
            dk_ref[...] = jnp.zeros_like(dk_ref)
            dv_ref[...] = jnp.zeros_like(dv_ref)
            ds_ref[...] = jnp.zeros_like(ds_ref)

        valid, dist = _swa_masks(qi)
        kc = _swa_cat(k_ref, qi)
        vc = _swa_cat(v_ref, qi)
        qs = jnp.concatenate([q_ref[e] for e in range(g)], axis=0).astype(BF16)
        dos = jnp.concatenate([do_ref[e] for e in range(g)], axis=0).astype(BF16)
        s_all = lax.dot_general(qs, kc, NT_DIMS, preferred_element_type=F32) * scale
        dp_all = lax.dot_general(dos, vc, NT_DIMS, preferred_element_type=F32)
        ps, dss = [], []
        for e in range(g):
            hh = grp * g + e
            lse_e = lse_ref[e]
            delta = jnp.sum(do_ref[e] * o_ref[e], axis=1, keepdims=True)
            s = s_all[e * t:(e + 1) * t] - slope_ref[hh] * dist
            p = jnp.where(valid, jnp.exp(s - lse_e), 0.0)
            ds = p * (dp_all[e * t:(e + 1) * t] - delta)
            ps.append(p.astype(BF16))
            dss.append((ds * scale).astype(BF16))
            ds_ref[e] += -jnp.sum(jnp.exp(sink_ref[hh] - lse_e) * delta)
        p_st = jnp.concatenate(ps, axis=0)
        ds_st = jnp.concatenate(dss, axis=0)
        dq = jnp.dot(ds_st, kc, preferred_element_type=F32)
        for e in range(g):
            dq_ref[e] = dq[e * t:(e + 1) * t]
        dkc = lax.dot_general(ds_st, qs, TN_DIMS, preferred_element_type=F32)
        dvc = lax.dot_general(p_st, dos, TN_DIMS, preferred_element_type=F32)
        jp = jnp.maximum(qi - 1, 0)
        for seg, off in enumerate((0, pl.multiple_of(jp * t, t), pl.multiple_of(qi * t, t))):
            dk_ref[0, pl.ds(off, t), :] += dkc[seg * t:(seg + 1) * t]
            dv_ref[0, pl.ds(off, t), :] += dvc[seg * t:(seg + 1) * t]

    tile = lambda a, b: (a, b, 0)
    whole = lambda a, b: (a, 0, 0)
    return pl.pallas_call(
        body, name=name, grid=(hkv, nq),
        in_specs=[BS((g, t, d), tile), BS((1, n, d), whole), BS((1, n, d), whole), BS((g, t, d), tile),
                  BS((g, t, 1), tile), BS((g, t, d), tile), BS(memory_space=pltpu.SMEM), BS(memory_space=pltpu.SMEM)],
        out_specs=[BS((g, t, d), tile), BS((1, n, d), whole), BS((1, n, d), whole), BS((g, 1, LANES), whole)],
        out_shape=[SDS((hq, n, d), F32), SDS((hkv, n, d), F32), SDS((hkv, n, d), F32), SDS((hq, 1, LANES), F32)],
        compiler_params=_cparams(("arbitrary", "arbitrary")))(q, k, v, o, lse, do, sinks, slopes)


def window_attention(q, k, v, sinks, slopes, scale, name):
    def run_fwd(q, k, v, sinks, slopes):
        o, lse = _swa_fwd_call(q, k, v, sinks, slopes, scale, name + "_f")
        return o, (q, k, v, sinks, slopes, o, lse)

    def run_bwd(res, do):
        q, k, v, sinks, slopes, o, lse = res
        dq, dk, dv, ds = _swa_bwd_call(q, k, v, o, lse, do, sinks, slopes, scale, name + "_b")
        return dq, dk, dv, ds[:, 0, 0], jnp.zeros_like(slopes)

    return _op(run_fwd, run_bwd)(q, k, v, sinks, slopes)


def _sigmoid(x):
    return 1.0 / (1.0 + jnp.exp(-x))


def _merge_fwd_call(gs, ys, name):
    n, c = ys[0].shape
    tr = _rows_tile(n, c, 1 << 20)

    def body(g0, g1, g2, y0, y1, y2, m_ref):
        m_ref[...] = (_sigmoid(g0[...]) * y0[...] + _sigmoid(g1[...]) * y1[...]) + _sigmoid(g2[...]) * y2[...]

    spec = BS((tr, c), lambda i: (i, 0))
    return pl.pallas_call(
        body, name=name, grid=(n // tr,), in_specs=[spec] * 6, out_specs=spec, out_shape=SDS((n, c), F32),
        compiler_params=_cparams(("parallel",)))(*gs, *ys)


def _merge_bwd_call(gs, ys, dm, name):
    n, c = ys[0].shape
    tr = _rows_tile(n, c, 1 << 20)

    def body(g0, g1, g2, y0, y1, y2, dm_ref, dg0, dg1, dg2, dy0, dy1, dy2):
        d = dm_ref[...]
        for g, y, dg, dy in ((g0, y0, dg0, dy0), (g1, y1, dg1, dy1), (g2, y2, dg2, dy2)):
            s = _sigmoid(g[...])
            dy[...] = d * s
            dg[...] = d * y[...] * (s * (1.0 - s))

    spec = BS((tr, c), lambda i: (i, 0))
    return pl.pallas_call(
        body, name=name, grid=(n // tr,), in_specs=[spec] * 7, out_specs=[spec] * 6,
        out_shape=[SDS((n, c), F32)] * 6, compiler_params=_cparams(("parallel",)))(*gs, *ys, dm)


def gated_merge(gs, ys, name):
    def fwd(gs, ys):
        return _merge_fwd_call(gs, ys, name + "_f"), (gs, ys)

    def bwd(res, dm):
        out = _merge_bwd_call(res[0], res[1], dm, name + "_b")
        return tuple(out[:3]), tuple(out[3:])

    return _op(fwd, bwd)(tuple(gs), tuple(ys))


CONV_TR = 264
CONV_TC = 1408


def _conv_tiles(n, f):
    tr = CONV_TR if n % CONV_TR == 0 else _div_tile(n, CONV_TR)
    tc = CONV_TC if f % CONV_TC == 0 else f
    return tr, tc


def _shift_down(cur, halo, first, tr):
    halo = jnp.where(first, 0.0, halo)
    row = lax.broadcasted_iota(jnp.int32, cur.shape, 0)
    h7, h6 = halo[7:8, :], halo[6:7, :]
    u1 = jnp.where(row == 0, h7, pltpu.roll(cur, 1, 0))
    u2 = jnp.where(row == 0, h6, jnp.where(row == 1, h7, pltpu.roll(cur, 2, 0)))
    return u1, u2


def _conv_lin(cur, u1, u2, w_ref, b_ref):
    return ((b_ref[...] + w_ref[0:1, :] * u2) + w_ref[1:2, :] * u1) + w_ref[2:3, :] * cur


def _conv_in_specs(tr, tc, nj):
    sub = tr // SUBLANES
    prev = lambda j, i: (jnp.maximum(i * sub - 1, 0), j)
    prev_v = lambda j, i: (jnp.maximum(i * sub - 1, 0), j + nj)
    return [BS((tr, tc), lambda j, i: (i, j)), BS((SUBLANES, tc), prev),
            BS((tr, tc), lambda j, i: (i, j + nj)), BS((SUBLANES, tc), prev_v),
            BS((3, tc), lambda j, i: (0, j)), BS((3, tc), lambda j, i: (0, j + nj)),
            BS((1, tc), lambda j, i: (0, j)), BS((1, tc), lambda j, i: (0, j + nj))]


def _conv_fwd_call(u, cw, cb, name):
    n, f2 = u.shape
    f = f2 // 2
    tr, tc = _conv_tiles(n, f)
    nj = f // tc

    def body(ug, ugh, uv, uvh, wg, wv, bg, bv, a_ref):
        first = pl.program_id(1) == 0
        g1, g2 = _shift_down(ug[...], ugh[...], first, tr)
        v1, v2 = _shift_down(uv[...], uvh[...], first, tr)
        cg = _conv_lin(ug[...], g1, g2, wg, bg)
        cv = _conv_lin(uv[...], v1, v2, wv, bv)
        a_ref[...] = cg * _sigmoid(cg) * cv

    return pl.pallas_call(
        body, name=name, grid=(nj, n // tr), in_specs=_conv_in_specs(tr, tc, nj),
        out_specs=BS((tr, tc), lambda j, i: (i, j)), out_shape=SDS((n, f), F32),
        compiler_params=_cparams(("parallel", "parallel")))(u, u, u, u, cw, cw, cb, cb)


def _conv_bwd_dc_call(u, cw, cb, da, name):
    n, f2 = u.shape
    f = f2 // 2
    tr, tc = _conv_tiles(n, f)
    nj = f // tc

    def body(ug, ugh, uv, uvh, wg, wv, bg, bv, da_ref, dc_ref, dw_ref, db_ref):
        first = pl.program_id(1) == 0
        g0, v0 = ug[...], uv[...]
        g1, g2 = _shift_down(g0, ugh[...], first, tr)
        v1, v2 = _shift_down(v0, uvh[...], first, tr)
        cg = _conv_lin(g0, g1, g2, wg, bg)
        cv = _conv_lin(v0, v1, v2, wv, bv)
        d = da_ref[...]
        s = _sigmoid(cg)
        dcg = d * cv * (s * (1.0 + cg * (1.0 - s)))
        dcv = d * (cg * s)
        dc_ref[0] = dcg
        dc_ref[1] = dcv

        @pl.when(first)
        def _():
            dw_ref[...] = jnp.zeros_like(dw_ref)
            db_ref[...] = jnp.zeros_like(db_ref)

        for p, dc, taps in ((0, dcg, (g2, g1, g0)), (1, dcv, (v2, v1, v0))):
            for t in range(3):
                dw_ref[p, t:t + 1, :] += jnp.sum(dc * taps[t], axis=0, keepdims=True)
            db_ref[p] += jnp.sum(dc, axis=0, keepdims=True)

    return pl.pallas_call(
        body, name=name, grid=(nj, n // tr),
        in_specs=_conv_in_specs(tr, tc, nj) + [BS((tr, tc), lambda j, i: (i, j))],
        out_specs=[BS((2, tr, tc), lambda j, i: (0, i, j)), BS((2, 3, tc), lambda j, i: (0, 0, j)),
                   BS((2, 1, tc), lambda j, i: (0, 0, j))],
        out_shape=[SDS((2, n, f), F32), SDS((2, 3, f), F32), SDS((2, 1, f), F32)],
        compiler_params=_cparams(("arbitrary", "arbitrary")))(u, u, u, u, cw, cw, cb, cb, da)


def _conv_bwd_du_call(dc, cw, name):
    _, n, f = dc.shape
    tr, tc = _conv_tiles(n, f)
    nj = f // tc
    ni = n // tr
    sub = tr // SUBLANES

    def body(c_ref, nx_ref, w_ref, du_ref):
        cur = c_ref[0]
        nxt = jnp.where(pl.program_id(2) == ni - 1, 0.0, nx_ref[0])
        row = lax.broadcasted_iota(jnp.int32, cur.shape, 0)
        n0, n1 = nxt[0:1, :], nxt[1:2, :]
        d1 = jnp.where(row == tr - 1, n0, pltpu.roll(cur, tr - 1, 0))
        d2 = jnp.where(row == tr - 1, n1, jnp.where(row == tr - 2, n0, pltpu.roll(cur, tr - 2, 0)))
        du_ref[...] = (w_ref[2:3, :] * cur + w_ref[1:2, :] * d1) + w_ref[0:1, :] * d2

    nxt_map = lambda p, j, i: (p, jnp.minimum((i + 1) * sub, n // SUBLANES - 1), j)
    return pl.pallas_call(
        body, name=name, grid=(2, nj, ni),
        in_specs=[BS((1, tr, tc), lambda p, j, i: (p, i, j)), BS((1, SUBLANES, tc), nxt_map),
                  BS((3, tc), lambda p, j, i: (0, p * nj + j))],
        out_specs=BS((tr, tc), lambda p, j, i: (i, p * nj + j)), out_shape=SDS((n, 2 * f), F32),
        compiler_params=_cparams(("parallel", "parallel", "parallel")))(dc, dc, cw)


def conv_glu(u, cw, cb, name):
    def fwd(u, cw, cb):
        return _conv_fwd_call(u, cw, cb, name + "_f"), (u, cw, cb)

    def bwd(res, da):
        u, cw, cb = res
        dc, dw, db = _conv_bwd_dc_call(u, cw, cb, da, name + "_bc")
        du = _conv_bwd_du_call(dc, cw, name + "_bu")
        return du, jnp.concatenate([dw[0], dw[1]], axis=-1), jnp.concatenate([db[0], db[1]], axis=-1)

    return _op(fwd, bwd)(u, cw, cb)


def _loss_call(y, t, n_real, name):
    n, c = y.shape
    tr = _rows_tile(n, c, 1 << 20)

    def body(y_ref, t_ref, dy_ref, l_ref):
        i = pl.program_id(0)
        row = i * tr + lax.broadcasted_iota(jnp.int32, (tr, c), 0)
        real = (row >= N_META) & (row < N_META + n_real)
        e = jnp.where(real, y_ref[...] - t_ref[...], 0.0)
        dy_ref[...] = e * (1.0 / c)

        @pl.when(i == 0)
        def _():
            l_ref[...] = jnp.zeros_like(l_ref)

        l_ref[...] += 0.5 * jnp.sum(jnp.sum(e * e, axis=-1, keepdims=True) * (1.0 / c), axis=0, keepdims=True)

    spec = BS((tr, c), lambda i: (i, 0))
    return pl.pallas_call(
        body, name=name, grid=(n // tr,), in_specs=[spec, spec],
        out_specs=[spec, BS((1, 1), lambda i: (0, 0))], out_shape=[SDS((n, c), F32), SDS((1, 1), F32)],
        compiler_params=_cparams(("arbitrary",)))(y, t)


def _to_heads(x, nh):
    n = x.shape[0]
    return x.reshape(n, nh, x.shape[1] // nh).transpose(1, 0, 2)


def _from_heads(x):
    h, n, d = x.shape
    return x.transpose(1, 0, 2).reshape(n, h * d)


def _head_norm(x, g, denom, name):
    h, n, d = x.shape
    return rms_norm(x.reshape(h * n, d), g, denom, name).reshape(h, n, d)


def _head_norm_rope(x, g, cos, sin, name):
    h, n, d = x.shape
    return rms_norm_rope(x.reshape(h * n, d), g, cos, sin, MLA_QK, name).reshape(h, n, d)


def _pad_in_cols(w):
    z = lambda k: jnp.zeros(w.shape[:-1] + (k,), w.dtype)
    return jnp.concatenate([w[..., :1544], z(120), w[..., 1544:1960], z(96), w[..., 1960:], z(128)], axis=-1)


def _pad_q_up(w):
    s = w.shape[:-1]
    w = w.reshape(s + (HEADS, MLA_QK))
    w = jnp.concatenate([w, jnp.zeros(s + (HEADS, LANES - MLA_QK), w.dtype)], axis=-1)
    return w.reshape(s + (HEADS * LANES,))


LAYERED = BIG + ("ffn_conv_w",)
BEFORE_MLA = ("w_in", "mla_w_q_up", "mla_w_kv_up")
AFTER_MLA = ("w_branch", "w_o", "ffn_w_up", "ffn_w_down")


def _assemble_layer(parts):
    out = {k: jnp.concatenate([v[i] for i in range(N_CHIPS)], axis=SHARD_AXIS[k] - 1) for k, v in parts.items()}
    if "w_in" in out:
        out["w_in"] = _pad_in_cols(out["w_in"])
    if "mla_w_q_up" in out:
        out["mla_w_q_up"] = _pad_q_up(out["mla_w_q_up"])
    return out


def _layer_parts(names, gathered):
    return {k: g.reshape((N_CHIPS,) + SHARD_SHAPE[k][1:]) for k, g in zip(names, gathered)}


def _rope_tables(n):
    half = MLA_ROPE // 2
    freqs = ROPE_THETA ** (-jnp.arange(half, dtype=F32) / half)
    ang = jnp.arange(n).astype(F32)[:, None] * freqs[None, :]
    cos, sin = jnp.cos(ang), jnp.sin(ang)
    one, zero = jnp.ones((n, MLA_NOPE), F32), jnp.zeros((n, MLA_NOPE), F32)
    tail1, tail0 = jnp.ones((n, LANES - MLA_QK), F32), jnp.zeros((n, LANES - MLA_QK), F32)
    return (jnp.concatenate([one, cos, cos, tail1], axis=1), jnp.concatenate([zero, sin, sin, tail0], axis=1))


def _pad_lanes(g, width):
    return jnp.concatenate([g, jnp.zeros((width - g.shape[0],), g.dtype)]).reshape(1, width)


PROJ_SEGMENTS = ((O_FQ, 512), (O_FK, 512), (O_FV, 512), (O_FF, HEADS), (O_CQ, MLA_Q_RANK), (O_CKV, MLA_KV_RANK),
                 (O_KR, MLA_ROPE), (O_SQ, 512), (O_SK, 128), (O_SV, 128), (O_G, D_MODEL), (O_G + D_MODEL, D_MODEL),
                 (O_G + 2 * D_MODEL, D_MODEL))


def _split_proj(proj):
    def fwd(x):
        return tuple(x[:, s:s + w] for s, w in PROJ_SEGMENTS), None

    def bwd(_, cts):
        rows, parts, pos = cts[0].shape[0], [], 0
        for (s, w), ct in zip(PROJ_SEGMENTS, cts):
            if s > pos:
                parts.append(jnp.zeros((rows, s - pos), F32))
            parts.append(ct)
            pos = s + w
        parts.append(jnp.zeros((rows, IN_PAD - pos), F32))
        return (jnp.concatenate(parts, axis=1),)

    return _op(fwd, bwd)(proj)


def _trunk(eps, eps_cw, sinks, meta, small, x, w0, late):
    assert DEPTH == 2
    seq = x.shape[0]
    n = -(-(N_META + seq) // ROW_PAD) * ROW_PAD
    ew = _assemble_layer(eps)
    cos, sin = _rope_tables(n)
    slopes = jnp.exp2(-8.0 * jnp.arange(1, HEADS + 1, dtype=F32) / HEADS)
    h = jnp.concatenate([meta, x, jnp.zeros((n - N_META - seq, D_MODEL), F32)], axis=0)
    wb = w0
    for l in range(DEPTH):
        p = f"l{l}_"
        row = lambda name: small[name][l].reshape(1, -1)
        xn = rms_norm(h, row("norm1_g"), D_MODEL, p + "norm1")
        proj = linear(xn, wb["w_in"], ew["w_in"], p + "win")
        p_fq, p_fk, p_fv, p_ff, p_cq, p_ckv, p_kr, p_sq, p_sk, p_sv, g0, g1, g2 = _split_proj(proj)
        fq = _head_norm(_to_heads(p_fq, HEADS), row("fox_q_g"), HEAD_DIM, p + "fqn")
        fk = _head_norm(_to_heads(p_fk, HEADS), row("fox_k_g"), HEAD_DIM, p + "fkn")
        fv = _to_heads(p_fv, HEADS)
        c = forget_cumsum(p_ff.T, small["fox_forget_b"][l].reshape(HEADS, 1), p + "fgate")
        if l == 0:
            rest = [a for k, a in zip(LAYERED, late) if k != "w_in"]
            out_a, got, carriers = causal_attention(fq, fk, fv, c, HEAD_DIM ** -0.5, p + "fox", rest, 0, BIG,
                                                    sinks[1])
            wb = dict(wb, **_assemble_layer(_layer_parts(LAYERED[1:], _fill_own(got, [a[0] for a in rest]))))
            ew1 = _assemble_layer(dict(zip(BIG, carriers)))
        else:
            out_a = causal_attention(fq, fk, fv, c, HEAD_DIM ** -0.5, p + "fox")
        cqn = rms_norm(p_cq, row("mla_q_a_g"), MLA_Q_RANK, p + "cqn")
        q = _to_heads(linear(cqn, wb["mla_w_q_up"], ew["mla_w_q_up"], p + "qup"), HEADS)
        q = _head_norm_rope(q, _pad_lanes(small["mla_q_g"][l], LANES), cos, sin, p + "mqn")
        ckvn = rms_norm(p_ckv, row("mla_kv_a_g"), MLA_KV_RANK, p + "ckvn")
        kv = _to_heads(linear(ckvn, wb["mla_w_kv_up"], ew["mla_w_kv_up"], p + "kvup"), HEADS)
        kr = jnp.broadcast_to(p_kr[None], (HEADS, n, MLA_ROPE))
        k = jnp.concatenate([kv[..., :MLA_NOPE], kr, jnp.zeros((HEADS, n, LANES - MLA_QK), F32)], axis=-1)
        k = _head_norm_rope(k, _pad_lanes(small["mla_k_g"][l], LANES), cos, sin, p + "mkn")
        if l == 0:
            out_b, got, carriers = causal_attention(q, k, kv[..., MLA_NOPE:], None, MLA_QK ** -0.5, p + "mla", late,
                                                    1, AFTER_MLA, sinks[0])
            w1 = _assemble_layer(_layer_parts(LAYERED, _fill_own(got, [a[1] for a in late])))
            ew = dict(ew, **_assemble_layer(dict(zip(AFTER_MLA, carriers))))
        else:
            out_b = causal_attention(q, k, kv[..., MLA_NOPE:], None, MLA_QK ** -0.5, p + "mla")
        sq = _head_norm(_to_heads(p_sq, HEADS), row("swa_q_g"), HEAD_DIM, p + "sqn")
        sk = _head_norm(_to_heads(p_sk, SWA_KV_HEADS), row("swa_k_g"), HEAD_DIM, p + "skn")
        sv = _to_heads(p_sv, SWA_KV_HEADS)
        out_c = window_attention(sq, sk, sv, small["swa_sinks"][l], slopes, HEAD_DIM ** -0.5, p + "swa")
        ys = [linear(_from_heads(o), wb["w_branch"][i], ew["w_branch"][i], p + f"br{i}")
              for i, o in enumerate((out_a, out_b, out_c))]
        merged = gated_merge([g0, g1, g2], ys, p + "merge")
        h = linear(merged, wb["w_o"], ew["w_o"], p + "wo", res=h)
        xn2 = rms_norm(h, row("norm2_g"), D_MODEL, p + "norm2")
        u = linear(xn2, wb["ffn_w_up"], ew["ffn_w_up"], p + "wup")
        act = conv_glu(u, lax.stop_gradient(wb["ffn_conv_w"]) + eps_cw[l], row("ffn_conv_b"), p + "conv")
        h = linear(act, wb["ffn_w_down"], ew["ffn_w_down"], p + "wdown", res=h)
        wb, ew = w1, ew1
    return h


def _local_step(x, target, w0, late, meta, small):
    seq = x.shape[0]
    eps = {k: jnp.zeros((N_CHIPS,) + SHARD_SHAPE[k][1:], F32) for k in BEFORE_MLA}
    eps_cw = jnp.zeros((DEPTH, 3, 2 * D_FF), F32)
    sinks = tuple([jnp.zeros((2, HALVED[k][1] // 2, HALVED[k][2]), F32) for k in names] for names in (AFTER_MLA, BIG))
    y, vjp = jax.vjp(lambda e, ec, sk, mt, s, xx: _trunk(e, ec, sk, mt, s, xx, w0, late),
                     eps, eps_cw, sinks, meta, small, x)
    n = y.shape[0]
    tpad = jnp.concatenate([jnp.zeros((N_META, D_MODEL), F32), target, jnp.zeros((n - N_META - seq, D_MODEL), F32)])
    dy, loss = _loss_call(y, tpad, seq, "loss")
    g_eps, g_cw, g_l1, g_meta, g_small, g_x = vjp(dy)
    return loss[0, 0], g_x, g_eps, g_l1, g_cw, g_meta, g_small


def _pack_rows(shapes, mult):
    total = sum(_size(s) for s in shapes)
    rows = -(-total // LANES)
    return -(-rows // mult) * mult


def _size(shape):
    n = 1
    for d in shape:
        n *= d
    return n


def _pack(arrs, rows, dtype):
    flat = [a.reshape(-1).astype(dtype) for a in arrs]
    used = sum(a.size for a in flat)
    flat.append(jnp.zeros((rows * LANES - used,), dtype))
    return jnp.concatenate(flat).reshape(rows, LANES)


def _unpack(p, shapes):
    flat = p.reshape(-1)
    out, off = [], 0
    for s in shapes:
        out.append(flat[off:off + _size(s)].reshape(s))
        off += _size(s)
    return out


MESH = pl.DeviceIdType.MESH
ANY = pl.BlockSpec(memory_space=pl.ANY)


def _me():
    return lax.axis_index("x"), lax.axis_index("y"), lax.axis_index("c")


def _remote(src, dst, send_sems, recv_sems, idx, dev):
    return pltpu.make_async_remote_copy(src_ref=src, dst_ref=dst, send_sem=send_sems.at[idx], recv_sem=recv_sems.at[idx],
                                        device_id=dev, device_id_type=MESH)


def _layer_gather(ins, outs, send_sems, recv_sems, layer, base):
    x, y, _ = _me()
    j = 2 * x + y
    sibling = (x, y, 1 - layer)
    chips = [(1 - x, y), (x, 1 - y), (1 - x, 1 - y)]

    def ici(p, r):
        cx, cy = chips[r]
        return _remote(ins[p].at[layer], outs[p].at[j], send_sems, recv_sems, base + 6 * p + r, (cx, cy, layer))

    def d2d(p, r):
        cx, cy = chips[r]
        blk = outs[p].at[2 * cx + cy]
        return _remote(blk, blk, send_sems, recv_sems, base + 6 * p + 3 + r, sibling)

    pairs = [(p, r) for p in range(len(ins)) for r in range(3)]

    def start():
        for p, r in pairs:
            ici(p, r).start()

    def forward():
        for p, r in pairs:
            ici(p, r).wait_recv()
            d2d(p, r).start()

    def drain():
        for p, r in pairs:
            ici(p, r).wait_send()
            d2d(p, r).wait_send()

    def receive():
        for p, r in pairs:
            d2d(p, r).wait_recv()

    return start, forward, drain, receive


def _fill_own(outs, own):
    j = 2 * lax.axis_index("x") + lax.axis_index("y")
    return [lax.dynamic_update_index_in_dim(o, a, j, 0) for o, a in zip(outs, own)]


def _gather_early(meta, arrs, name):
    npk = len(arrs)

    def body(*refs):
        m_in, ins = refs[0], refs[1:npk + 1]
        m_out, outs = refs[npk + 1], refs[npk + 2:2 * npk + 2]
        send_sems, recv_sems = refs[2 * npk + 2:]
        x, y, c = _me()
        j = 2 * x + y
        sibling = (x, y, 1 - c)
        chips = [(1 - x, y), (x, 1 - y), (1 - x, 1 - y)]
        start, forward, drain, receive = _layer_gather(ins, outs, send_sems, recv_sems, 0, 6)
        sends = []
        for r, (cx, cy) in enumerate(chips):
            cp = _remote(m_in.at[c], m_out.at[j, c], send_sems, recv_sems, r, (cx, cy, c))
            cp.start()
            sends.append(cp)
        pl.when(c == 0)(start)
        for r, (cx, cy) in enumerate(chips):
            blk = m_out.at[2 * cx + cy, c]
            _remote(blk, blk, send_sems, recv_sems, r, sibling).wait_recv()
            fw = _remote(blk, blk, send_sems, recv_sems, 3 + r, sibling)
            fw.start()
            sends.append(fw)
        for r, (cx, cy) in enumerate(chips):
            blk = m_out.at[2 * cx + cy, 1 - c]
            _remote(blk, blk, send_sems, recv_sems, 3 + r, sibling).wait_recv()
        for cp in sends:
            cp.wait_send()

        @pl.when(c == 0)
        def _():
            forward()
            drain()

        pl.when(c == 1)(receive)

    nsem = 6 + 6 * npk
    res = pl.pallas_call(
        body, name=name, in_specs=[ANY] * (npk + 1), out_specs=[ANY] * (npk + 1),
        out_shape=[SDS((N_CHIPS,) + meta.shape, meta.dtype)] + [SDS((N_CHIPS,) + a.shape[1:], a.dtype) for a in arrs],
        scratch_shapes=[pltpu.SemaphoreType.DMA((nsem,)), pltpu.SemaphoreType.DMA((nsem,))],
        compiler_params=pltpu.CompilerParams(has_side_effects=True))(meta, *arrs)
    return _fill_own(res[:1], [meta])[0], _fill_own(res[1:], [a[0] for a in arrs])


def _pair_exchange(gs, name):
    npk = len(gs)

    def body(*refs):
        ins, outs = refs[:npk], refs[npk:2 * npk]
        send_sems, recv_sems = refs[2 * npk:]
        x, y, c = _me()
        cps = [_remote(ins[p].at[:, 1 - c], outs[p], send_sems, recv_sems, p, (x, y, 1 - c)) for p in range(npk)]
        for cp in cps:
            cp.start()
        for cp in cps:
            cp.wait()

    return pl.pallas_call(
        body, name=name, in_specs=[ANY] * npk, out_specs=[ANY] * npk,
        out_shape=[SDS((N_CHIPS,) + g.shape[2:], g.dtype) for g in gs],
        scratch_shapes=[pltpu.SemaphoreType.DMA((npk,)), pltpu.SemaphoreType.DMA((npk,))],
        compiler_params=pltpu.CompilerParams(has_side_effects=True))(*gs)


def _chip_exchange(ss, small, name):
    npk = len(ss)

    def body(*refs):
        ins, sm_ref = refs[:npk], refs[npk]
        outs, sa_ref = refs[npk + 1:2 * npk + 1], refs[2 * npk + 1]
        send_sems, recv_sems, loc_sem = refs[2 * npk + 2:]
        x, y, c = _me()
        me = 4 * x + 2 * y + c
        lc = pltpu.make_async_copy(sm_ref, sa_ref.at[me], loc_sem.at[0])
        lc.start()
        cps = []
        for p in range(npk):
            for r, (cx, cy) in enumerate([(1 - x, y), (x, 1 - y), (1 - x, 1 - y)]):
                cp = _remote(ins[p].at[2 * cx + cy], outs[p].at[r], send_sems, recv_sems, 3 * p + r, (cx, cy, c))
                cp.start()
                cps.append(cp)
        base = 3 * npk - 1
        for mask in range(1, N_DEV):
            px, py, pc = x ^ (mask >> 2), y ^ ((mask >> 1) & 1), c ^ (mask & 1)
            cp = _remote(sm_ref, sa_ref.at[me], send_sems, recv_sems, base + mask, (px, py, pc))
            cp.start()
            cps.append(cp)
        for p in range(npk):
            for r in range(3):
                _remote(outs[p].at[r], outs[p].at[r], send_sems, recv_sems, 3 * p + r, (x, y, c)).wait_recv()
        for mask in range(1, N_DEV):
            src = 4 * (x ^ (mask >> 2)) + 2 * (y ^ ((mask >> 1) & 1)) + (c ^ (mask & 1))
            _remote(sa_ref.at[src], sa_ref.at[src], send_sems, recv_sems, base + mask, (x, y, c)).wait_recv()
        for cp in cps:
            cp.wait_send()
        lc.wait()

    nsem = 3 * npk + N_DEV - 1
    res = pl.pallas_call(
        body, name=name, in_specs=[ANY] * (npk + 1), out_specs=[ANY] * (npk + 1),
        out_shape=[SDS((3,) + s.shape[1:], s.dtype) for s in ss] + [SDS((N_DEV,) + small.shape, small.dtype)],
        scratch_shapes=[pltpu.SemaphoreType.DMA((nsem,)), pltpu.SemaphoreType.DMA((nsem,)),
                        pltpu.SemaphoreType.DMA((1,))],
        compiler_params=pltpu.CompilerParams(has_side_effects=True))(*ss, small)
    return res[:npk], res[npk]


def _half_exchange(ghs, name):
    npk = len(ghs)

    def body(*refs):
        ins, outs = refs[:npk], refs[npk:2 * npk]
        send_sems, recv_sems = refs[2 * npk:]
        x, y, c = _me()
        cps = [_remote(ins[p], outs[p], send_sems, recv_sems, p, (x, y, 1 - c)) for p in range(npk)]
        for cp in cps:
            cp.start()
        for cp in cps:
            cp.wait()

    return pl.pallas_call(
        body, name=name, in_specs=[ANY] * npk, out_specs=[ANY] * npk, out_shape=[SDS(g.shape, g.dtype) for g in ghs],
        scratch_shapes=[pltpu.SemaphoreType.DMA((npk,)), pltpu.SemaphoreType.DMA((npk,))],
        compiler_params=pltpu.CompilerParams(has_side_effects=True))(*ghs)


def _add_tile(rows, cols):
    return _div_tile(rows, max(16, (1 << 19) // max(cols, LANES)), 16)


def _pair_add(g, r1, c_idx, name):
    _, rows, cols = r1.shape
    tr = _add_tile(rows, cols)

    def body(c_ref, g_ref, r_ref, o_ref, ob_ref):
        s = g_ref[0] + r_ref[...]
        o_ref[...] = s
        ob_ref[...] = s.astype(BF16)

    own = BS((1, tr, cols), lambda k, i, c: (k, i, 0))
    return pl.pallas_call(
        body, name=name,
        grid_spec=pltpu.PrefetchScalarGridSpec(
            num_scalar_prefetch=1, grid=(N_CHIPS, rows // tr),
            in_specs=[BS((1, 1, tr, cols), lambda k, i, c: (k, c[0], i, 0)), own], out_specs=[own, own]),
        out_shape=[SDS(r1.shape, F32), SDS(r1.shape, BF16)],
        compiler_params=_cparams(("parallel", "parallel")))(c_idx, g, r1)


def _chip_add(s1, r2, j_idx, name):
    _, rows, cols = s1.shape
    tr = _add_tile(rows, cols)

    def body(j_ref, s_ref, r_ref, o_ref):
        o_ref[...] = ((s_ref[0] + r_ref[0].astype(F32)) + r_ref[1].astype(F32)) + r_ref[2].astype(F32)

    return pl.pallas_call(
        body, name=name,
        grid_spec=pltpu.PrefetchScalarGridSpec(
            num_scalar_prefetch=1, grid=(rows // tr,),
            in_specs=[BS((1, tr, cols), lambda i, j: (j[0], i, 0)), BS((3, tr, cols), lambda i, j: (0, i, 0))],
            out_specs=BS((tr, cols), lambda i, j: (i, 0))),
        out_shape=SDS((rows, cols), F32), compiler_params=_cparams(("parallel",)))(j_idx, s1, r2)


def _adamw_math(w, g, m, v):
    m = ADAM_B1 * m + (1.0 - ADAM_B1) * g
    v = ADAM_B2 * v + (1.0 - ADAM_B2) * (g * g)
    m_hat = m / (1.0 - ADAM_B1 ** ADAM_STEP)
    v_hat = v / (1.0 - ADAM_B2 ** ADAM_STEP)
    delta = -ADAM_LR * (m_hat / (jnp.sqrt(v_hat) + ADAM_EPS) + ADAM_WD * w)
    return delta, m, v


def _adamw(w, gh, go, m, v, c_idx, name):
    _, rows, cols = w.shape
    tr = _add_tile(rows, cols)

    def body(c_ref, w_ref, gh_ref, go_ref, m_ref, v_ref, g_out, d_out, m_out, v_out):
        g = jnp.where(pl.program_id(0) == c_ref[0], gh_ref[...], go_ref[...])
        g_out[0] = g
        d_out[0], m_out[0], v_out[0] = _adamw_math(w_ref[0], g, m_ref[0], v_ref[0])

    full = BS((1, tr, cols), lambda hf, i, c: (hf, i, 0))
    half = BS((tr, cols), lambda hf, i, c: (i, 0))
    return pl.pallas_call(
        body, name=name,
        grid_spec=pltpu.PrefetchScalarGridSpec(
            num_scalar_prefetch=1, grid=(2, rows // tr), in_specs=[full, half, half, full, full],
            out_specs=[full] * 4),
        out_shape=[SDS(w.shape, F32)] * 4, compiler_params=_cparams(("parallel", "parallel")))(c_idx, w, gh, go, m, v)


def _adamw_layers(w, gh0, go0, g1, m, v, c_idx, name):
    _, _, rows, cols = w.shape
    tr = _add_tile(rows, cols)

    def body(c_ref, w_ref, gh_ref, go_ref, g1_ref, m_ref, v_ref, g_out, d_out, m_out, v_out):
        g0 = jnp.where(pl.program_id(1) == c_ref[0], gh_ref[...], go_ref[...])
        g = jnp.where(pl.program_id(0) == 0, g0, g1_ref[0])
        g_out[0, 0] = g
        d_out[0, 0], m_out[0, 0], v_out[0, 0] = _adamw_math(w_ref[0, 0], g, m_ref[0, 0], v_ref[0, 0])

    full = BS((1, 1, tr, cols), lambda l, hf, i, c: (l, hf, i, 0))
    half = BS((tr, cols), lambda l, hf, i, c: (i, 0))
    return pl.pallas_call(
        body, name=name,
        grid_spec=pltpu.PrefetchScalarGridSpec(
            num_scalar_prefetch=1, grid=(2, 2, rows // tr),
            in_specs=[full, half, half, BS((1, tr, cols), lambda l, hf, i, c: (hf, i, 0)), full, full],
            out_specs=[full] * 4),
        out_shape=[SDS(w.shape, F32)] * 4,
        compiler_params=_cparams(("parallel", "parallel", "parallel")))(c_idx, w, gh0, go0, g1, m, v)


def _adamw_two(w, g0, g1, m, v, name):
    _, _, rows, cols = w.shape
    tr = _add_tile(rows, cols)

    def body(w_ref, g0_ref, g1_ref, m_ref, v_ref, g_out, d_out, m_out, v_out):
        g = jnp.where(pl.program_id(0) == 0, g0_ref[0], g1_ref[0])
        g_out[0, 0] = g
        d_out[0, 0], m_out[0, 0], v_out[0, 0] = _adamw_math(w_ref[0, 0], g, m_ref[0, 0], v_ref[0, 0])

    full = BS((1, 1, tr, cols), lambda l, hf, i: (l, hf, i, 0))
    half = BS((1, tr, cols), lambda l, hf, i: (hf, i, 0))
    return pl.pallas_call(
        body, name=name, grid=(2, 2, rows // tr), in_specs=[full, half, half, full, full], out_specs=[full] * 4,
        out_shape=[SDS(w.shape, F32)] * 4,
        compiler_params=_cparams(("parallel", "parallel", "parallel")))(w, g0, g1, m, v)


def _sum_devices(sa, name):
    def body(sa_ref, g_out):
        g = sa_ref[0]
        for d in range(1, N_DEV):
            g = g + sa_ref[d]
        g_out[...] = g

    return pl.pallas_call(body, name=name, out_shape=SDS(sa.shape[1:], F32),
                          compiler_params=pltpu.CompilerParams(vmem_limit_bytes=VMEM_LIMIT))(sa)


def _adamw_small(ws, gs, ms, vs, name):
    k = len(ws)

    def body(*refs):
        ins, outs = refs[:4 * k], refs[4 * k:]
        for i in range(k):
            d, m, v = _adamw_math(ins[i][...], ins[k + i][...], ins[2 * k + i][...], ins[3 * k + i][...])
            outs[i][...], outs[k + i][...], outs[2 * k + i][...] = d, m, v

    return pl.pallas_call(body, name=name, out_shape=[SDS(w.shape, F32) for w in ws] * 3,
                          compiler_params=pltpu.CompilerParams(vmem_limit_bytes=VMEM_LIMIT))(*ws, *gs, *ms, *vs)


HALVED = {"w_in": (2, 1024, 1450), "mla_w_q_up": (2, 256, 192), "mla_w_kv_up": (2, 128, 256),
          "w_branch": (2, 1536, 256), "w_o": (2, 256, 1024), "ffn_w_up": (2, 1024, 1408),
          "ffn_w_down": (2, 704, 1024), "ffn_conv_w": (2, 3, 1408), "meta_tokens": (2, 8, 256)}
SMALL_SHAPE = {"norm1_g": (2, 1024), "fox_forget_b": (2, 8), "fox_q_g": (2, 64), "fox_k_g": (2, 64),
               "mla_q_a_g": (2, 256), "mla_kv_a_g": (2, 128), "mla_q_g": (2, 96), "mla_k_g": (2, 96),
               "swa_q_g": (2, 64), "swa_k_g": (2, 64), "swa_sinks": (2, 8), "norm2_g": (2, 1024),
               "ffn_conv_b": (2, 5632)}
SMALL_ROWS = _pack_rows([SMALL_SHAPE[k] for k in SMALL] + [(1,)], SUBLANES)


def kernel(x, meta_tokens, norm1_g, w_in, fox_forget_b, fox_q_g, fox_k_g, mla_q_a_g, mla_w_q_up, mla_kv_a_g, mla_w_kv_up, mla_q_g, mla_k_g, swa_q_g, swa_k_g, swa_sinks, w_branch, w_o, norm2_g, ffn_w_up, ffn_conv_w, ffn_conv_b, ffn_w_down, loss_target, m_meta_tokens, m_norm1_g, m_w_in, m_fox_forget_b, m_fox_q_g, m_fox_k_g, m_mla_q_a_g, m_mla_w_q_up, m_mla_kv_a_g, m_mla_w_kv_up, m_mla_q_g, m_mla_k_g, m_swa_q_g, m_swa_k_g, m_swa_sinks, m_w_branch, m_w_o, m_norm2_g, m_ffn_w_up, m_ffn_conv_w, m_ffn_conv_b, m_ffn_w_down, v_meta_tokens, v_norm1_g, v_w_in, v_fox_forget_b, v_fox_q_g, v_fox_k_g, v_mla_q_a_g, v_mla_w_q_up, v_mla_kv_a_g, v_mla_w_kv_up, v_mla_q_g, v_mla_k_g, v_swa_q_g, v_swa_k_g, v_swa_sinks, v_w_branch, v_w_o, v_norm2_g, v_ffn_w_up, v_ffn_conv_w, v_ffn_conv_b, v_ffn_w_down):
    w = dict(meta_tokens=meta_tokens, norm1_g=norm1_g, w_in=w_in, fox_forget_b=fox_forget_b, fox_q_g=fox_q_g,
             fox_k_g=fox_k_g, mla_q_a_g=mla_q_a_g, mla_w_q_up=mla_w_q_up, mla_kv_a_g=mla_kv_a_g,
             mla_w_kv_up=mla_w_kv_up, mla_q_g=mla_q_g, mla_k_g=mla_k_g, swa_q_g=swa_q_g, swa_k_g=swa_k_g,
             swa_sinks=swa_sinks, w_branch=w_branch, w_o=w_o, norm2_g=norm2_g, ffn_w_up=ffn_w_up,
             ffn_conv_w=ffn_conv_w, ffn_conv_b=ffn_conv_b, ffn_w_down=ffn_w_down)
    m = dict(meta_tokens=m_meta_tokens, norm1_g=m_norm1_g, w_in=m_w_in, fox_forget_b=m_fox_forget_b,
             fox_q_g=m_fox_q_g, fox_k_g=m_fox_k_g, mla_q_a_g=m_mla_q_a_g, mla_w_q_up=m_mla_w_q_up,
             mla_kv_a_g=m_mla_kv_a_g, mla_w_kv_up=m_mla_w_kv_up, mla_q_g=m_mla_q_g, mla_k_g=m_mla_k_g,
             swa_q_g=m_swa_q_g, swa_k_g=m_swa_k_g, swa_sinks=m_swa_sinks, w_branch=m_w_branch, w_o=m_w_o,
             norm2_g=m_norm2_g, ffn_w_up=m_ffn_w_up, ffn_conv_w=m_ffn_conv_w, ffn_conv_b=m_ffn_conv_b,
             ffn_w_down=m_ffn_w_down)
    v = dict(meta_tokens=v_meta_tokens, norm1_g=v_norm1_g, w_in=v_w_in, fox_forget_b=v_fox_forget_b,
             fox_q_g=v_fox_q_g, fox_k_g=v_fox_k_g, mla_q_a_g=v_mla_q_a_g, mla_w_q_up=v_mla_w_q_up,
             mla_kv_a_g=v_mla_kv_a_g, mla_w_kv_up=v_mla_w_kv_up, mla_q_g=v_mla_q_g, mla_k_g=v_mla_k_g,
             swa_q_g=v_swa_q_g, swa_k_g=v_swa_k_g, swa_sinks=v_swa_sinks, w_branch=v_w_branch, w_o=v_w_o,
             norm2_g=v_norm2_g, ffn_w_up=v_ffn_w_up, ffn_conv_w=v_ffn_conv_w, ffn_conv_b=v_ffn_conv_b,
             ffn_w_down=v_ffn_w_down)
    xi, yi, ci = _me()
    c_idx = ci.astype(jnp.int32).reshape(1)
    j_idx = (2 * xi + yi).astype(jnp.int32).reshape(1)

    sh_names = BIG + FINE
    local = {k: (w[k].astype(BF16) if k in BIG else w[k]).reshape(HALVED[k]) for k in sh_names}
    late = [local[k] for k in LAYERED]
    meta_g, early = _gather_early(local["meta_tokens"], [local["w_in"]], "gather_early")
    meta = jnp.concatenate([meta_g[i].reshape(SHARD_SHAPE["meta_tokens"]) for i in range(N_CHIPS)], axis=1)
    w0 = _assemble_layer(_layer_parts(("w_in",), early))
    small = {k: w[k] for k in SMALL}

    loss, g_x, g_eps, g_red, g_cw, g_meta, g_small = _local_step(x[0], loss_target[0], w0, late, meta, small)
    g_after0, g_l1 = dict(zip(AFTER_MLA, g_red[0])), dict(zip(BIG, g_red[1]))

    quarter = {k: (2, HALVED[k][1] // 2, HALVED[k][2]) for k in BIG}
    last = BEFORE_MLA + FINE
    gs = [g_eps[k].reshape((N_CHIPS,) + quarter[k]) for k in BEFORE_MLA]
    for k, g in (("meta_tokens", g_meta), ("ffn_conv_w", g_cw)):
        gs.append(jnp.stack(jnp.split(g, N_CHIPS, axis=SHARD_AXIS[k])).reshape((N_CHIPS,) + HALVED[k]))
    spack = _pack([g_small[k] for k in SMALL] + [loss.reshape(1)], SMALL_ROWS, F32)
    r1 = _pair_exchange(gs, "grads_pair_exchange")
    s1 = [_pair_add(g, r, c_idx, "grads_pair_add_" + k) for g, r, k in zip(gs, r1, last)]
    r2, sa = _chip_exchange([s[1] for s in s1], spack, "grads_chip_exchange")
    gh = dict(zip(last, [_chip_add(s[0], r, j_idx, "grads_chip_add_" + k) for s, r, k in zip(s1, r2, last)]))
    go = dict(zip(last, _half_exchange([gh[k] for k in last], "grads_half_exchange")))

    grads, deltas, new_m, new_v = {}, {}, {}, {}
    for k in sh_names:
        if k in BIG:
            shp = (2,) + quarter[k]
            wk, mk, vk = w[k].reshape(shp), m[k].reshape(shp), v[k].reshape(shp)
            if k in BEFORE_MLA:
                outs = _adamw_layers(wk, gh[k], go[k], g_l1[k], mk, vk, c_idx, "adamw_" + k)
            else:
                outs = _adamw_two(wk, g_after0[k], g_l1[k], mk, vk, "adamw_" + k)
        else:
            outs = _adamw(w[k].reshape(HALVED[k]), gh[k], go[k], m[k].reshape(HALVED[k]), v[k].reshape(HALVED[k]),
                          c_idx, "adamw_" + k)
        for dst, o in zip((grads, deltas, new_m, new_v), outs):
            dst[k] = o.reshape(SHARD_SHAPE[k])
    sm_shapes = [SMALL_SHAPE[k] for k in SMALL] + [(1,)]
    g_sum = _unpack(_sum_devices(sa, "sum_small"), sm_shapes)
    res = _adamw_small([w[k] for k in SMALL], g_sum[:-1], [m[k] for k in SMALL], [v[k] for k in SMALL], "adamw_small")
    ns = len(SMALL)
    grads.update(zip(SMALL, g_sum[:-1]))
    for dst, vals in zip((deltas, new_m, new_v), (res[:ns], res[ns:2 * ns], res[2 * ns:])):
        dst.update(zip(SMALL, vals))
    total_loss = g_sum[-1][0]
    return (total_loss, g_x[None], *[grads[k] for k in WEIGHTS], *[deltas[k] for k in WEIGHTS],
            *[new_m[k] for k in WEIGHTS], *[new_v[k] for k in WEIGHTS])
```

```python
import jax
import jax.numpy as jnp
from jax import lax
from jax.experimental import pallas as pl
from jax.experimental.pallas import tpu as pltpu

F32 = jnp.float32
BF16 = jnp.bfloat16
SDS = jax.ShapeDtypeStruct
BS = pl.BlockSpec

D_MODEL = 1024
DEPTH = 2
N_META = 16
EPS = 1e-6
HEADS = 8
HEAD_DIM = 64
MLA_Q_RANK = 256
MLA_KV_RANK = 128
MLA_NOPE = 64
MLA_ROPE = 32
MLA_QK = MLA_NOPE + MLA_ROPE
ROPE_THETA = 10000.0
SWA_KV_HEADS = 2
WINDOW = 128
D_FF = 2816
IN_PAD = 6144
N_CHIPS = 4
N_DEV = 8

ADAM_LR = 0.001
ADAM_B1 = 0.9
ADAM_B2 = 0.999
ADAM_EPS = 1e-08
ADAM_WD = 0.01
ADAM_STEP = 10

LANES = 128
SUBLANES = 8
ROW_PAD = 128
CAUSAL_TILE = 384
NEG = -1e30
VMEM_LIMIT = 56 * 1024 * 1024

O_FQ, O_FK, O_FV, O_FF = 0, 512, 1024, 1536
O_CQ, O_CKV, O_KR = 1664, 1920, 2048
O_SQ, O_SK, O_SV, O_G = 2176, 2688, 2816, 2944

SHARD_AXIS = {"meta_tokens": 1, "w_in": 2, "mla_w_q_up": 2, "mla_w_kv_up": 2, "w_branch": 3, "w_o": 1,
              "ffn_w_up": 2, "ffn_conv_w": 2, "ffn_w_down": 1}
SHARD_SHAPE = {"meta_tokens": (16, 256), "w_in": (2, 1024, 1450), "mla_w_q_up": (2, 256, 192),
               "mla_w_kv_up": (2, 128, 256), "w_branch": (2, 3, 512, 256), "w_o": (2, 256, 1024),
               "ffn_w_up": (2, 1024, 1408), "ffn_conv_w": (2, 3, 1408), "ffn_w_down": (2, 704, 1024)}
BIG = ("w_in", "mla_w_q_up", "mla_w_kv_up", "w_branch", "w_o", "ffn_w_up", "ffn_w_down")
FINE = ("meta_tokens", "ffn_conv_w")
SMALL = ("norm1_g", "fox_forget_b", "fox_q_g", "fox_k_g", "mla_q_a_g", "mla_kv_a_g", "mla_q_g", "mla_k_g",
         "swa_q_g", "swa_k_g", "swa_sinks", "norm2_g", "ffn_conv_b")
WEIGHTS = ("meta_tokens", "norm1_g", "w_in", "fox_forget_b", "fox_q_g", "fox_k_g", "mla_q_a_g", "mla_w_q_up",
           "mla_kv_a_g", "mla_w_kv_up", "mla_q_g", "mla_k_g", "swa_q_g", "swa_k_g", "swa_sinks", "w_branch", "w_o",
           "norm2_g", "ffn_w_up", "ffn_conv_w", "ffn_conv_b", "ffn_w_down")


def _cparams(sem):
    return pltpu.CompilerParams(dimension_semantics=sem, vmem_limit_bytes=VMEM_LIMIT)


def _div_tile(n, cap, mult=SUBLANES):
    best = None
    for t in range(mult, min(n, cap) + 1, mult):
        if n % t == 0:
            best = t
    return best if best is not None else n


def _rows_tile(n, width, budget=2 << 20):
    return _div_tile(n, max(SUBLANES, budget // (4 * max(width, LANES))))


def _op(fwd, bwd):
    @jax.custom_vjp
    def op(*args):
        return fwd(*args)[0]
    op.defvjp(fwd, bwd)
    return op


def _rotate(v, cos, sin):
    lane = lax.broadcasted_iota(jnp.int32, v.shape, 1)
    rot = jnp.where(lane < MLA_NOPE + MLA_ROPE // 2, -pltpu.roll(v, LANES - MLA_ROPE // 2, 1),
                    pltpu.roll(v, MLA_ROPE // 2, 1))
    return v * cos + rot * sin


def _rms_fwd_call(x, g, denom, name, rot=None):
    n, c = x.shape
    tr = _rows_tile(n if rot is None else rot[0].shape[0], c)
    nt = None if rot is None else rot[0].shape[0] // tr

    def body(x_ref, g_ref, *rest):
        y_ref = rest[-1]
        xv = x_ref[...]
        ms = jnp.sum(xv * xv, axis=-1, keepdims=True) * (1.0 / denom)
        y = xv * lax.rsqrt(ms + EPS) * g_ref[...]
        y_ref[...] = y if rot is None else _rotate(y, rest[0][...], rest[1][...])

    ins, args = [BS((tr, c), lambda i: (i, 0)), BS((1, c), lambda i: (0, 0))], [x, g]
    if rot is not None:
        ins += [BS((tr, c), lambda i: (i % nt, 0))] * 2
        args += list(rot)
    return pl.pallas_call(
        body, name=name, grid=(n // tr,), in_specs=ins,
        out_specs=BS((tr, c), lambda i: (i, 0)), out_shape=SDS((n, c), F32),
        compiler_params=_cparams(("parallel",)))(*args)


def _rms_bwd_call(x, g, dy, denom, name, rot=None):
    n, c = x.shape
    tr = _rows_tile(n if rot is None else rot[0].shape[0], c)
    nt = None if rot is None else rot[0].shape[0] // tr

    def body(x_ref, g_ref, dy_ref, *rest):
        dx_ref, dg_ref = rest[-2:]
        xv = x_ref[...]
        dy = dy_ref[...]
        if rot is not None:
            dy = _rotate(dy, rest[0][...], -rest[1][...])
        ms = jnp.sum(xv * xv, axis=-1, keepdims=True) * (1.0 / denom)
        r = lax.rsqrt(ms + EPS)
        xh = xv * r
        dxh = dy * g_ref[...]
        dx_ref[...] = r * (dxh - xh * (jnp.sum(dxh * xh, axis=-1, keepdims=True) * (1.0 / denom)))

        @pl.when(pl.program_id(0) == 0)
        def _():
            dg_ref[...] = jnp.zeros_like(dg_ref)

        dg_ref[...] += jnp.sum(dy * xh, axis=0, keepdims=True)

    ins = [BS((tr, c), lambda i: (i, 0)), BS((1, c), lambda i: (0, 0)), BS((tr, c), lambda i: (i, 0))]
    args = [x, g, dy]
    if rot is not None:
        ins += [BS((tr, c), lambda i: (i % nt, 0))] * 2
        args += list(rot)
    return pl.pallas_call(
        body, name=name, grid=(n // tr,), in_specs=ins,
        out_specs=[BS((tr, c), lambda i: (i, 0)), BS((1, c), lambda i: (0, 0))],
        out_shape=[SDS((n, c), F32), SDS((1, c), F32)],
        compiler_params=_cparams(("arbitrary",)))(*args)


def rms_norm(x, g, denom, name):
    def fwd(x, g):
        return _rms_fwd_call(x, g, denom, name + "_f"), (x, g)

    def bwd(res, dy):
        return tuple(_rms_bwd_call(res[0], res[1], dy, denom, name + "_b"))

    return _op(fwd, bwd)(x, g)


def rms_norm_rope(x, g, cos, sin, denom, name):
    def fwd(x, g, cos, sin):
        return _rms_fwd_call(x, g, denom, name + "_f", (cos, sin)), (x, g, cos, sin)

    def bwd(res, dy):
        x, g, cos, sin = res
        dx, dg = _rms_bwd_call(x, g, dy, denom, name + "_b", (cos, sin))
        return dx, dg, jnp.zeros_like(cos), jnp.zeros_like(sin)

    return _op(fwd, bwd)(x, g, cos, sin)


def _mm_call(a, b, mode, res, name):
    if mode == "nn":
        (m, kc), n = a.shape, b.shape[1]
    elif mode == "nt":
        (m, kc), n = a.shape, b.shape[0]
    else:
        (kc, m), n = a.shape, b.shape[1]
    if mode == "tn":
        tk = _div_tile(kc, 528)
        tm = _div_tile(m, 1408, LANES)
        tn = _div_tile(n, 2048, LANES)
    else:
        tk = kc if kc <= 2816 else _div_tile(kc, 1024, LANES)
        tm = _div_tile(m, max(LANES, (9 << 19) // (4 * tk)))
        tn = _div_tile(n, 1408 if mode == "nt" else 512, LANES)
    nk = kc // tk
    dims = {"nn": (((1,), (0,)), ((), ())), "nt": (((1,), (1,)), ((), ())), "tn": (((0,), (0,)), ((), ()))}[mode]

    def body(*refs):
        if res is None:
            a_ref, b_ref, o_ref, acc_ref = refs
            r_ref = None
        else:
            a_ref, b_ref, r_ref, o_ref, acc_ref = refs
        k = pl.program_id(2)

        @pl.when(k == 0)
        def _():
            acc_ref[...] = jnp.zeros_like(acc_ref)

        acc_ref[...] += lax.dot_general(a_ref[...].astype(BF16), b_ref[...].astype(BF16), dims,
                                        preferred_element_type=F32)

        @pl.when(k == nk - 1)
        def _():
            if r_ref is None:
                o_ref[...] = acc_ref[...]
            else:
                o_ref[...] = r_ref[...] + acc_ref[...]

    a_spec = BS((tk, tm), lambda i, j, k: (k, i)) if mode == "tn" else BS((tm, tk), lambda i, j, k: (i, k))
    b_spec = BS((tn, tk), lambda i, j, k: (j, k)) if mode == "nt" else BS((tk, tn), lambda i, j, k: (k, j))
    o_spec = BS((tm, tn), lambda i, j, k: (i, j))
    ins, args = [a_spec, b_spec], [a, b]
    if res is not None:
        ins.append(o_spec)
        args.append(res)
    return pl.pallas_call(
        body, name=name, grid=(m // tm, n // tn, nk), in_specs=ins, out_specs=o_spec,
        out_shape=SDS((m, n), F32), scratch_shapes=[pltpu.VMEM((tm, tn), F32)],
        compiler_params=_cparams(("parallel", "parallel", "arbitrary")))(*args)


def linear(a, w, eps, name, res=None):
    if res is None:
        def fwd(a, w, eps):
            return _mm_call(a, w, "nn", None, name + "_f"), (a, w)

        def bwd(r, dc):
            a, w = r
            return (_mm_call(dc, w, "nt", None, name + "_da"), jnp.zeros_like(w),
                    _mm_call(a, dc, "tn", None, name + "_dw"))

        return _op(fwd, bwd)(a, w, eps)

    def fwd_r(a, w, eps, res):
        return _mm_call(a, w, "nn", res, name + "_f"), (a, w)

    def bwd_r(r, dc):
        a, w = r
        return (_mm_call(dc, w, "nt", None, name + "_da"), jnp.zeros_like(w),
                _mm_call(a, dc, "tn", None, name + "_dw"), dc)

    return _op(fwd_r, bwd_r)(a, w, eps, res)


def _gate_tile(n):
    return _div_tile(n, CAUSAL_TILE, LANES)


def _tri_dot(v, upper):
    ct = v.shape[1]
    r = lax.broadcasted_iota(jnp.int32, (ct, ct), 0)
    c = lax.broadcasted_iota(jnp.int32, (ct, ct), 1)
    tri = jnp.where((r <= c) if upper else (r >= c), 1.0, 0.0).astype(F32)
    return jnp.dot(v, tri, preferred_element_type=F32, precision=lax.Precision.HIGHEST)


def _gate_fwd_call(z, b, name):
    h, n = z.shape
    CT = _gate_tile(n)

    def body(z_ref, b_ref, c_ref, carry):
        @pl.when(pl.program_id(0) == 0)
        def _():
            carry[...] = jnp.zeros_like(carry)

        x = z_ref[...] + b_ref[...]
        ls = jnp.minimum(x, 0.0) - jnp.log(1.0 + jnp.exp(-jnp.abs(x)))
        c_ref[...] = _tri_dot(ls, True) + carry[...]
        carry[...] += jnp.sum(ls, axis=1, keepdims=True)

    return pl.pallas_call(
        body, name=name, grid=(n // CT,),
        in_specs=[BS((h, CT), lambda j: (0, j)), BS((h, 1), lambda j: (0, 0))],
        out_specs=BS((h, CT), lambda j: (0, j)), out_shape=SDS((h, n), F32),
        scratch_shapes=[pltpu.VMEM((h, 1), F32)],
        compiler_params=_cparams(("arbitrary",)))(z, b)


def _gate_bwd_call(z, b, dc, name):
    h, n = z.shape
    CT = _gate_tile(n)
    nt = n // CT

    def body(z_ref, b_ref, dc_ref, dz_ref, db_ref, carry):
        @pl.when(pl.program_id(0) == 0)
        def _():
            carry[...] = jnp.zeros_like(carry)
            db_ref[...] = jnp.zeros_like(db_ref)

        dcv = dc_ref[...]
        dls = _tri_dot(dcv, False) + carry[...]
        carry[...] += jnp.sum(dcv, axis=1, keepdims=True)
        x = z_ref[...] + b_ref[...]
        e = jnp.exp(-jnp.abs(x))
        dz = dls * jnp.where(x >= 0, e / (1.0 + e), 1.0 / (1.0 + e))
        dz_ref[...] = dz
        db_ref[...] += jnp.sum(dz, axis=1, keepdims=True)

    rev = lambda j: (0, nt - 1 - j)
    return pl.pallas_call(
        body, name=name, grid=(nt,),
        in_specs=[BS((h, CT), rev), BS((h, 1), lambda j: (0, 0)), BS((h, CT), rev)],
        out_specs=[BS((h, CT), rev), BS((h, 1), lambda j: (0, 0))],
        out_shape=[SDS((h, n), F32), SDS((h, 1), F32)],
        scratch_shapes=[pltpu.VMEM((h, 1), F32)],
        compiler_params=_cparams(("arbitrary",)))(z, b, dc)


def forget_cumsum(z, b, name):
    def fwd(z, b):
        return _gate_fwd_call(z, b, name + "_f"), (z, b)

    def bwd(res, dc):
        return tuple(_gate_bwd_call(res[0], res[1], dc, name + "_b"))

    return _op(fwd, bwd)(z, b)


NT_DIMS = (((1,), (1,)), ((), ()))
TN_DIMS = (((0,), (0,)), ((), ()))
HEADS_PER_STEP = 2


def _causal_tile(n):
    return CAUSAL_TILE if n % CAUSAL_TILE == 0 else ROW_PAD


def _causal_fwd_call(q, k, v, ck_r, fox, scale, name, late=None, late_layer=1):
    h, n, dk = q.shape
    dv = v.shape[2]
    t = _causal_tile(n)
    nq = n // t
    hb = HEADS_PER_STEP
    nl = 0 if late is None else len(late)
    n_in = 3 + int(fox) + nl

    def body(*refs):
        q_ref, k_ref, v_ref = refs[:3]
        ck_ref = refs[3] if fox else None
        o_ref, lse_ref = refs[n_in:n_in + 2]
        m_scr, l_scr, acc_scr = refs[n_in + 2 + nl:n_in + 5 + nl]
        qi = pl.program_id(1)
        if nl:
            start, forward, drain, receive = _layer_gather(refs[n_in - nl:n_in], refs[n_in + 2:n_in + 2 + nl],
                                                           refs[-2], refs[-1], late_layer, 0)
            hp, core = pl.program_id(0), lax.axis_index("c")
            last = (hp == h // hb - 1) & (qi == nq - 1)
            pl.when((hp == 0) & (qi == 0) & (core == late_layer))(start)
            pl.when((hp == h // hb // 2) & (qi == 0) & (core == late_layer))(forward)
            pl.when(last & (core == late_layer))(drain)
            pl.when(last & (core == 1 - late_layer))(receive)
        qbs = [q_ref[e].astype(BF16) for e in range(hb)]
        m_scr[...] = jnp.full(m_scr.shape, NEG, F32)
        l_scr[...] = jnp.zeros_like(l_scr)
        acc_scr[...] = jnp.zeros_like(acc_scr)

        def process(j, masked):
            off = pl.multiple_of(j * t, t)
            if masked:
                rows = lax.broadcasted_iota(jnp.int32, (t, t), 0)
                cols = lax.broadcasted_iota(jnp.int32, (t, t), 1)
                valid = cols <= rows
            for e in range(hb):
                kb = k_ref[e, pl.ds(off, t), :].astype(BF16)
                vb = v_ref[e, pl.ds(off, t), :].astype(BF16)
                s = lax.dot_general(qbs[e], kb, NT_DIMS, preferred_element_type=F32) * scale
                if fox:
                    s = s - ck_ref[e, j]
                if masked:
                    s = jnp.where(valid, s, NEG)
                m_old = m_scr[e]
                m_new = jnp.maximum(m_old, jnp.max(s, axis=1, keepdims=True))
                alpha = jnp.exp(m_old - m_new)
                p = jnp.exp(s - jnp.tile(m_new, (1, t // LANES)))
                l_scr[e] = alpha * l_scr[e] + jnp.sum(p, axis=1, keepdims=True)
                acc_scr[e] = alpha[:, :dv] * acc_scr[e] + jnp.dot(p.astype(BF16), vb, preferred_element_type=F32)
                m_scr[e] = m_new

        def step(j, carry):
            process(j, False)
            return carry

        lax.fori_loop(0, qi, step, 0)
        process(qi, True)
        for e in range(hb):
            l = l_scr[e]
            o_ref[e] = acc_scr[e] / l[:, :dv]
            lse_ref[e, 0] = jnp.transpose(m_scr[e] + jnp.log(l))[0:1, :]

    ins = [BS((hb, t, dk), lambda a, b: (a, b, 0)), BS((hb, n, dk), lambda a, b: (a, 0, 0)),
           BS((hb, n, dv), lambda a, b: (a, 0, 0))]
    args = [q, k, v]
    if fox:
        ins.append(BS((hb, nq, 1, t), lambda a, b: (a, 0, 0, 0)))
        args.append(ck_r)
    outs = [BS((hb, t, dv), lambda a, b: (a, b, 0)), BS((hb, 1, 1, t), lambda a, b: (a, b, 0, 0))]
    oshape = [SDS((h, n, dv), F32), SDS((h, nq, 1, t), F32)]
    scratch = [pltpu.VMEM((hb, t, LANES), F32), pltpu.VMEM((hb, t, LANES), F32), pltpu.VMEM((hb, t, dv), F32)]
    if nl:
        ins += [ANY] * nl
        args += list(late)
        outs += [ANY] * nl
        oshape += [SDS((N_CHIPS,) + a.shape[1:], a.dtype) for a in late]
        scratch += [pltpu.SemaphoreType.DMA((6 * nl,)), pltpu.SemaphoreType.DMA((6 * nl,))]
    res = pl.pallas_call(
        body, name=name, grid=(h // hb, nq), in_specs=ins, out_specs=outs, out_shape=oshape, scratch_shapes=scratch,
        compiler_params=pltpu.CompilerParams(dimension_semantics=("arbitrary", "arbitrary"),
                                             vmem_limit_bytes=VMEM_LIMIT, has_side_effects=bool(nl)))(*args)
    return res[0], res[1], list(res[2:])


def _causal_bwd_call(q, k, v, do, o, lse_r, ck_r, fox, scale, name, side=None):
    h, n, dk = q.shape
    dv = v.shape[2]
    t = _causal_tile(n)
    nq = n // t
    hb = HEADS_PER_STEP
    ns = 0 if side is None else len(side)

    def body(*refs):
        it = iter(refs)
        q_ref, k_ref, v_ref, do_ref, o_ref, lse_ref = (next(it) for _ in range(6))
        ck_ref = next(it) if fox else None
        side_in = [next(it) for _ in range(ns)]
        dq_ref, dk_ref, dv_ref = next(it), next(it), next(it)
        dck_ref, dcq_ref = (next(it), next(it)) if fox else (None, None)
        side_out = [next(it) for _ in range(ns)]
        delta_scr, dk_scr, dv_scr = next(it), next(it), next(it)
        dck_scr = next(it) if fox else None
        kj = pl.program_id(1)
        if ns:
            send_sems, recv_sems = next(it), next(it)
            x, y, core = _me()
            chips = [(1 - x, y), (x, 1 - y), (1 - x, 1 - y)]
            copies = [_remote(side_in[p].at[2 * cx + cy], side_out[p].at[r], send_sems, recv_sems, 3 * p + r,
                              (cx, cy, core)) for p in range(ns) for r, (cx, cy) in enumerate(chips)]

            @pl.when((pl.program_id(0) == 0) & (kj == 0))
            def _():
                for cp in copies:
                    cp.start()

            @pl.when((pl.program_id(0) == h // hb - 1) & (kj == nq - 1))
            def _():
                for cp in copies:
                    cp.wait()

        @pl.when(kj == 0)
        def _():
            dq_ref[...] = jnp.zeros_like(dq_ref)
            if fox:
                dcq_ref[...] = jnp.zeros_like(dcq_ref)
            ones = jnp.ones((SUBLANES, dv), F32)

            def fill(qi, carry):
                off = pl.multiple_of(qi * t, t)
                for e in range(hb):
                    prod = do_ref[e, pl.ds(off, t), :] * o_ref[e, pl.ds(off, t), :]
                    delta_scr[e, qi] = lax.dot_general(ones, prod, NT_DIMS, preferred_element_type=F32,
                                                       precision=lax.Precision.HIGHEST)[0:1, :]
                return carry

            lax.fori_loop(0, nq, fill, 0)

        kbs = [k_ref[e].astype(BF16) for e in range(hb)]
        vbs = [v_ref[e].astype(BF16) for e in range(hb)]
        dk_scr[...] = jnp.zeros_like(dk_scr)
        dv_scr[...] = jnp.zeros_like(dv_scr)
        if fox:
            dck_scr[...] = jnp.zeros_like(dck_scr)
            ckcs = [jnp.tile(jnp.transpose(jnp.broadcast_to(ck_ref[e, 0], (LANES, t))), (1, t // LANES))
                    for e in range(hb)]

        def process(qi, masked):
            off = pl.multiple_of(qi * t, t)
            if masked:
                krows = lax.broadcasted_iota(jnp.int32, (t, t), 0)
                qcols = lax.broadcasted_iota(jnp.int32, (t, t), 1)
                valid = krows <= qcols
            for e in range(hb):
                qb = q_ref[e, pl.ds(off, t), :].astype(BF16)
                dob = do_ref[e, pl.ds(off, t), :].astype(BF16)
                st = lax.dot_general(kbs[e], qb, NT_DIMS, preferred_element_type=F32) * scale
                if fox:
                    st = st - ckcs[e]
                pt = jnp.exp(st - lse_ref[e, qi])
                if masked:
                    pt = jnp.where(valid, pt, 0.0)
                dv_scr[e] += jnp.dot(pt.astype(BF16), dob, preferred_element_type=F32)
                dpt = lax.dot_general(vbs[e], dob, NT_DIMS, preferred_element_type=F32)
                dst = pt * (dpt - delta_scr[e, qi])
                if fox:
                    dck_scr[e] -= jnp.sum(dst, axis=1, keepdims=True)
                    dcq_ref[e, qi] += jnp.sum(dst, axis=0, keepdims=True)
                dsb = (dst * scale).astype(BF16)
                dk_scr[e] += jnp.dot(dsb, qb, preferred_element_type=F32)
                dq_ref[e, pl.ds(off, t), :] += lax.dot_general(dsb, kbs[e], TN_DIMS, preferred_element_type=F32)

        def step(qi, carry):
            process(qi, False)
            return carry

        process(kj, True)
        lax.fori_loop(kj + 1, nq, step, 0)
        dk_ref[...] = dk_scr[...]
        dv_ref[...] = dv_scr[...]
        if fox:
            for e in range(hb):
                dck_ref[e, 0] = jnp.transpose(jnp.broadcast_to(dck_scr[e], (t, LANES)))[0:1, :]

    whole = lambda a, b: (a, 0, 0)
    tile = lambda a, b: (a, b, 0)
    rowv = lambda a, b: (a, 0, 0, 0)
    rowt = lambda a, b: (a, b, 0, 0)
    ins = [BS((hb, n, dk), whole), BS((hb, t, dk), tile), BS((hb, t, dv), tile), BS((hb, n, dv), whole),
           BS((hb, n, dv), whole), BS((hb, nq, 1, t), rowv)]
    args = [q, k, v, do, o, lse_r]
    outs = [BS((hb, n, dk), whole), BS((hb, t, dk), tile), BS((hb, t, dv), tile)]
    oshape = [SDS((h, n, dk), F32), SDS((h, n, dk), F32), SDS((h, n, dv), F32)]
    scratch = [pltpu.VMEM((hb, nq, 1, t), F32), pltpu.VMEM((hb, t, dk), F32), pltpu.VMEM((hb, t, dv), F32)]
    if fox:
        ins.append(BS((hb, 1, 1, t), rowt))
        args.append(ck_r)
        outs += [BS((hb, 1, 1, t), rowt), BS((hb, nq, 1, t), rowv)]
        oshape += [SDS((h, nq, 1, t), F32), SDS((h, nq, 1, t), F32)]
        scratch.append(pltpu.VMEM((hb, t, 1), F32))
    if ns:
        ins += [ANY] * ns
        args += list(side)
        outs += [ANY] * ns
        oshape += [SDS((3,) + s.shape[1:], s.dtype) for s in side]
        scratch += [pltpu.SemaphoreType.DMA((3 * ns,)), pltpu.SemaphoreType.DMA((3 * ns,))]
    return pl.pallas_call(
        body, name=name, grid=(h // hb, nq), in_specs=ins, out_specs=outs, out_shape=oshape, scratch_shapes=scratch,
        compiler_params=pltpu.CompilerParams(dimension_semantics=("arbitrary", "arbitrary"),
                                             vmem_limit_bytes=VMEM_LIMIT, has_side_effects=bool(ns)))(*args)


def causal_attention(q, k, v, c, scale, name, late=None, late_layer=1, reduce=None, sinks=None):
    h, n, _ = q.shape
    t = _causal_tile(n)
    nq = n // t
    fox = c is not None

    def run_fwd(q, k, v, c, late, sinks):
        ck_r = c.reshape(h, nq, 1, t) if fox else None
        o, lse, got = _causal_fwd_call(q, k, v, ck_r, fox, scale, name + "_f", late, late_layer)
        out = (o,)
        if late is not None:
            out += (got,)
        if reduce is not None:
            out += ([jnp.zeros((N_CHIPS,) + SHARD_SHAPE[w][1:], F32) for w in reduce],)
        return (out if len(out) > 1 else o), (q, k, v, c, late, o, lse)

    def run_bwd(res, ct):
        q, k, v, c, late, o, lse = res
        ck_r = c.reshape(h, nq, 1, t) if fox else None
        dlate = None if late is None else [jnp.zeros_like(a) for a in late]
        if reduce is None:
            do = ct if late is None else ct[0]
            outs = _causal_bwd_call(q, k, v, do, o, lse, ck_r, fox, scale, name + "_b")
            return outs[0], outs[1], outs[2], ((outs[3] + outs[4]).reshape(h, n) if fox else None), dlate, None
        do, g1 = ct[0], ct[-1]
        xi, yi, ci = _me()
        c_idx = ci.astype(jnp.int32).reshape(1)
        j_idx = (2 * xi + yi).astype(jnp.int32).reshape(1)
        gs = [g.reshape(N_CHIPS, 2, HALVED[w][1] // 2, HALVED[w][2]) for g, w in zip(g1, reduce)]
        r1 = _pair_exchange(gs, name + "_pair_exchange")
        s1 = [_pair_add(g, r, c_idx, name + "_pair_add_" + w) for g, r, w in zip(gs, r1, reduce)]
        outs = _causal_bwd_call(q, k, v, do, o, lse, ck_r, fox, scale, name + "_b", [s[1] for s in s1])
        gh = [_chip_add(s[0], r, j_idx, name + "_chip_add_" + w) for s, r, w in zip(s1, outs[-len(reduce):], reduce)]
        go = _half_exchange(gh, name + "_half_exchange")
        full = [jnp.where(ci == 0, jnp.stack([a, b]), jnp.stack([b, a])) for a, b in zip(gh, go)]
        return outs[0], outs[1], outs[2], ((outs[3] + outs[4]).reshape(h, n) if fox else None), dlate, full

    return _op(run_fwd, run_bwd)(q, k, v, c, late, sinks)


SWA_T = 128


def _swa_masks(qi):
    t = SWA_T
    r = lax.broadcasted_iota(jnp.int32, (t, 3 * t), 0)
    c = lax.broadcasted_iota(jnp.int32, (t, 3 * t), 1)
    seg0 = c < t
    seg1 = (c >= t) & (c < 2 * t)
    jp = jnp.maximum(qi - 1, 0)
    kpos = jnp.where(seg0, c, jnp.where(seg1, jp * t + c - t, qi * t + c - 2 * t))
    dist = qi * t + r - kpos
    band = (dist >= 0) & ((dist < WINDOW) | (kpos < N_META))
    valid = (seg0 & (kpos < N_META) & (qi >= 2)) | (jnp.logical_not(seg0) & band & (jnp.logical_not(seg1) | (qi >= 1)))
    return valid, dist.astype(F32)


def _swa_cat(ref, qi):
    t = SWA_T
    jp = jnp.maximum(qi - 1, 0)
    return jnp.concatenate([ref[0, 0:t, :], ref[0, pl.ds(pl.multiple_of(jp * t, t), t), :],
                            ref[0, pl.ds(pl.multiple_of(qi * t, t), t), :]], axis=0).astype(BF16)


def _swa_fwd_call(q, k, v, sinks, slopes, scale, name):
    hq, n, d = q.shape
    hkv = k.shape[0]
    g = hq // hkv
    t = SWA_T
    nq = n // t

    def body(q_ref, k_ref, v_ref, sink_ref, slope_ref, o_ref, lse_ref):
        grp = pl.program_id(0)
        qi = pl.program_id(1)
        valid, dist = _swa_masks(qi)
        kc = _swa_cat(k_ref, qi)
        vc = _swa_cat(v_ref, qi)
        qs = jnp.concatenate([q_ref[e] for e in range(g)], axis=0).astype(BF16)
        s_all = lax.dot_general(qs, kc, NT_DIMS, preferred_element_type=F32) * scale
        ps, ls, ms = [], [], []
        for e in range(g):
            hh = grp * g + e
            s = jnp.where(valid, s_all[e * t:(e + 1) * t] - slope_ref[hh] * dist, NEG)
            m = jnp.maximum(jnp.max(s, axis=1, keepdims=True), sink_ref[hh])
            p = jnp.exp(s - m)
            ls.append(jnp.sum(p, axis=1, keepdims=True) + jnp.exp(sink_ref[hh] - m))
            ms.append(m)
            ps.append(p.astype(BF16))
        acc = jnp.dot(jnp.concatenate(ps, axis=0), vc, preferred_element_type=F32)
        for e in range(g):
            o_ref[e] = acc[e * t:(e + 1) * t] / ls[e]
            lse_ref[e] = ms[e] + jnp.log(ls[e])

    return pl.pallas_call(
        body, name=name, grid=(hkv, nq),
        in_specs=[BS((g, t, d), lambda a, b: (a, b, 0)), BS((1, n, d), lambda a, b: (a, 0, 0)),
                  BS((1, n, d), lambda a, b: (a, 0, 0)), BS(memory_space=pltpu.SMEM), BS(memory_space=pltpu.SMEM)],
        out_specs=[BS((g, t, d), lambda a, b: (a, b, 0)), BS((g, t, 1), lambda a, b: (a, b, 0))],
        out_shape=[SDS((hq, n, d), F32), SDS((hq, n, 1), F32)],
        compiler_params=_cparams(("parallel", "parallel")))(q, k, v, sinks, slopes)


def _swa_bwd_call(q, k, v, o, lse, do, sinks, slopes, scale, name):
    hq, n, d = q.shape
    hkv = k.shape[0]
    g = hq // hkv
    t = SWA_T
    nq = n // t

    def body(q_ref, k_ref, v_ref, o_ref, lse_ref, do_ref, sink_ref, slope_ref, dq_ref, dk_ref, dv_ref, ds_ref):
        grp = pl.program_id(0)
        qi = pl.program_id(1)

        @pl.when(qi == 0)
        def _():
            dk_ref[...] = jnp.zeros_like(dk_ref)
            dv_ref[...] = jnp.zeros_like(dv_ref)
            ds_ref[...] = jnp.zeros_like(ds_ref)

        valid, dist = _swa_masks(qi)
        kc = _swa_cat(k_ref, qi)
        vc = _swa_cat(v_ref, qi)
        qs = jnp.concatenate([q_ref[e] for e in range(g)], axis=0).astype(BF16)
        dos = jnp.concatenate([do_ref[e] for e in range(g)], axis=0).astype(BF16)
        s_all = lax.dot_general(qs, kc, NT_DIMS, preferred_element_type=F32) * scale
        dp_all = lax.dot_general(dos, vc, NT_DIMS, preferred_element_type=F32)
        ps, dss = [], []
        for e in range(g):
            hh = grp * g + e
            lse_e = lse_ref[e]
            delta = jnp.sum(do_ref[e] * o_ref[e], axis=1, keepdims=True)
            s = s_all[e * t:(e + 1) * t] - slope_ref[hh] * dist
            p = jnp.where(valid, jnp.exp(s - lse_e), 0.0)
            ds = p * (dp_all[e * t:(e + 1) * t] - delta)
            ps.append(p.astype(BF16))
            dss.append((ds * scale).astype(BF16))
            ds_ref[e] += -jnp.sum(jnp.exp(sink_ref[hh] - lse_e) * delta)
        p_st = jnp.concatenate(ps, axis=0)
        ds_st = jnp.concatenate(dss, axis=0)
        dq = jnp.dot(ds_st, kc, preferred_element_type=F32)
        for e in range(g):
            dq_ref[e] = dq[e * t:(e + 1) * t]
        dkc = lax.dot_general(ds_st, qs, TN_DIMS, preferred_element_type=F32)
        dvc = lax.dot_general(p_st, dos, TN_DIMS, preferred_element_type=F32)
        jp = jnp.maximum(qi - 1, 0)
        for seg, off in enumerate((0, pl.multiple_of(jp * t, t), pl.multiple_of(qi * t, t))):
            dk_ref[0, pl.ds(off, t), :] += dkc[seg * t:(seg + 1) * t]
            dv_ref[0, pl.ds(off, t), :] += dvc[seg * t:(seg + 1) * t]

    tile = lambda a, b: (a, b, 0)
    whole = lambda a, b: (a, 0, 0)
    return pl.pallas_call(
        body, name=name, grid=(hkv, nq),
        in_specs=[BS((g, t, d), tile), BS((1, n, d), whole), BS((1, n, d), whole), BS((g, t, d), tile),
                  BS((g, t, 1), tile), BS((g, t, d), tile), BS(memory_space=pltpu.SMEM), BS(memory_space=pltpu.SMEM)],
        out_specs=[BS((g, t, d), tile), BS((1, n, d), whole), BS((1, n, d), whole), BS((g, 1, LANES), whole)],
        out_shape=[SDS((hq, n, d), F32), SDS((hkv, n, d), F32), SDS((hkv, n, d), F32), SDS((hq, 1, LANES), F32)],
        compiler_params=_cparams(("arbitrary", "arbitrary")))(q, k, v, o, lse, do, sinks, slopes)


def window_attention(q, k, v, sinks, slopes, scale, name):
    def run_fwd(q, k, v, sinks, slopes):
        o, lse = _swa_fwd_call(q, k, v, sinks, slopes, scale, name + "_f")
        return o, (q, k, v, sinks, slopes, o, lse)

    def run_bwd(res, do):
        q, k, v, sinks, slopes, o, lse = res
        dq, dk, dv, ds = _swa_bwd_call(q, k, v, o, lse, do, sinks, slopes, scale, name + "_b")
        return dq, dk, dv, ds[:, 0, 0], jnp.zeros_like(slopes)

    return _op(run_fwd, run_bwd)(q, k, v, sinks, slopes)


def _sigmoid(x):
    return 1.0 / (1.0 + jnp.exp(-x))


def _merge_fwd_call(gs, ys, name):
    n, c = ys[0].shape
    tr = _rows_tile(n, c, 1 << 20)

    def body(g0, g1, g2, y0, y1, y2, m_ref):
        m_ref[...] = (_sigmoid(g0[...]) * y0[...] + _sigmoid(g1[...]) * y1[...]) + _sigmoid(g2[...]) * y2[...]

    spec = BS((tr, c), lambda i: (i, 0))
    return pl.pallas_call(
        body, name=name, grid=(n // tr,), in_specs=[spec] * 6, out_specs=spec, out_shape=SDS((n, c), F32),
        compiler_params=_cparams(("parallel",)))(*gs, *ys)


def _merge_bwd_call(gs, ys, dm, name):
    n, c = ys[0].shape
    tr = _rows_tile(n, c, 1 << 20)

    def body(g0, g1, g2, y0, y1, y2, dm_ref, dg0, dg1, dg2, dy0, dy1, dy2):
        d = dm_ref[...]
        for g, y, dg, dy in ((g0, y0, dg0, dy0), (g1, y1, dg1, dy1), (g2, y2, dg2, dy2)):
            s = _sigmoid(g[...])
            dy[...] = d * s
            dg[...] = d * y[...] * (s * (1.0 - s))

    spec = BS((tr, c), lambda i: (i, 0))
    return pl.pallas_call(
        body, name=name, grid=(n // tr,), in_specs=[spec] * 7, out_specs=[spec] * 6,
        out_shape=[SDS((n, c), F32)] * 6, compiler_params=_cparams(("parallel",)))(*gs, *ys, dm)


def gated_merge(gs, ys, name):
    def fwd(gs, ys):
        return _merge_fwd_call(gs, ys, name + "_f"), (gs, ys)

    def bwd(res, dm):
        out = _merge_bwd_call(res[0], res[1], dm, name + "_b")
        return tuple(out[:3]), tuple(out[3:])

    return _op(fwd, bwd)(tuple(gs), tuple(ys))


CONV_TR = 264
CONV_TC = 1408


def _conv_tiles(n, f):
    tr = CONV_TR if n % CONV_TR == 0 else _div_tile(n, CONV_TR)
    tc = CONV_TC if f % CONV_TC == 0 else f
    return tr, tc


def _shift_down(cur, halo, first, tr):
    halo = jnp.where(first, 0.0, halo)
    row = lax.broadcasted_iota(jnp.int32, cur.shape, 0)
    h7, h6 = halo[7:8, :], halo[6:7, :]
    u1 = jnp.where(row == 0, h7, pltpu.roll(cur, 1, 0))
    u2 = jnp.where(row == 0, h6, jnp.where(row == 1, h7, pltpu.roll(cur, 2, 0)))
    return u1, u2


def _conv_lin(cur, u1, u2, w_ref, b_ref):
    return ((b_ref[...] + w_ref[0:1, :] * u2) + w_ref[1:2, :] * u1) + w_ref[2:3, :] * cur


def _conv_in_specs(tr, tc, nj):
    sub = tr // SUBLANES
    prev = lambda j, i: (jnp.maximum(i * sub - 1, 0), j)
    prev_v = lambda j, i: (jnp.maximum(i * sub - 1, 0), j + nj)
    return [BS((tr, tc), lambda j, i: (i, j)), BS((SUBLANES, tc), prev),
            BS((tr, tc), lambda j, i: (i, j + nj)), BS((SUBLANES, tc), prev_v),
            BS((3, tc), lambda j, i: (0, j)), BS((3, tc), lambda j, i: (0, j + nj)),
            BS((1, tc), lambda j, i: (0, j)), BS((1, tc), lambda j, i: (0, j + nj))]


def _conv_fwd_call(u, cw, cb, name):
    n, f2 = u.shape
    f = f2 // 2
    tr, tc = _conv_tiles(n, f)
    nj = f // tc

    def body(ug, ugh, uv, uvh, wg, wv, bg, bv, a_ref):
        first = pl.program_id(1) == 0
        g1, g2 = _shift_down(ug[...], ugh[...], first, tr)
        v1, v2 = _shift_down(uv[...], uvh[...], first, tr)
        cg = _conv_lin(ug[...], g1, g2, wg, bg)
        cv = _conv_lin(uv[...], v1, v2, wv, bv)
        a_ref[...] = cg * _sigmoid(cg) * cv

    return pl.pallas_call(
        body, name=name, grid=(nj, n // tr), in_specs=_conv_in_specs(tr, tc, nj),
        out_specs=BS((tr, tc), lambda j, i: (i, j)), out_shape=SDS((n, f), F32),
        compiler_params=_cparams(("parallel", "parallel")))(u, u, u, u, cw, cw, cb, cb)


def _conv_bwd_dc_call(u, cw, cb, da, name):
    n, f2 = u.shape
    f = f2 // 2
    tr, tc = _conv_tiles(n, f)
    nj = f // tc

    def body(ug, ugh, uv, uvh, wg, wv, bg, bv, da_ref, dc_ref, dw_ref, db_ref):
        first = pl.program_id(1) == 0
        g0, v0 = ug[...], uv[...]
        g1, g2 = _shift_down(g0, ugh[...], first, tr)
        v1, v2 = _shift_down(v0, uvh[...], first, tr)
        cg = _conv_lin(g0, g1, g2, wg, bg)
        cv = _conv_lin(v0, v1, v2, wv, bv)
        d = da_ref[...]
        s = _sigmoid(cg)
        dcg = d * cv * (s * (1.0 + cg * (1.0 - s)))
        dcv = d * (cg * s)
        dc_ref[0] = dcg
        dc_ref[1] = dcv

        @pl.when(first)
        def _():
            dw_ref[...] = jnp.zeros_like(dw_ref)
            db_ref[...] = jnp.zeros_like(db_ref)

        for p, dc, taps in ((0, dcg, (g2, g1, g0)), (1, dcv, (v2, v1, v0))):
            for t in range(3):
                dw_ref[p, t:t + 1, :] += jnp.sum(dc * taps[t], axis=0, keepdims=True)
            db_ref[p] += jnp.sum(dc, axis=0, keepdims=True)

    return pl.pallas_call(
        body, name=name, grid=(nj, n // tr),
        in_specs=_conv_in_specs(tr, tc, nj) + [BS((tr, tc), lambda j, i: (i, j))],
        out_specs=[BS((2, tr, tc), lambda j, i: (0, i, j)), BS((2, 3, tc), lambda j, i: (0, 0, j)),
                   BS((2, 1, tc), lambda j, i: (0, 0, j))],
        out_shape=[SDS((2, n, f), F32), SDS((2, 3, f), F32), SDS((2, 1, f), F32)],
        compiler_params=_cparams(("arbitrary", "arbitrary")))(u, u, u, u, cw, cw, cb, cb, da)


def _conv_bwd_du_call(dc, cw, name):
    _, n, f = dc.shape
    tr, tc = _conv_tiles(n, f)
    nj = f // tc
    ni = n // tr
    sub = tr // SUBLANES

    def body(c_ref, nx_ref, w_ref, du_ref):
        cur = c_ref[0]
        nxt = jnp.where(pl.program_id(2) == ni - 1, 0.0, nx_ref[0])
        row = lax.broadcasted_iota(jnp.int32, cur.shape, 0)
        n0, n1 = nxt[0:1, :], nxt[1:2, :]
        d1 = jnp.where(row == tr - 1, n0, pltpu.roll(cur, tr - 1, 0))
        d2 = jnp.where(row == tr - 1, n1, jnp.where(row == tr - 2, n0, pltpu.roll(cur, tr - 2, 0)))
        du_ref[...] = (w_ref[2:3, :] * cur + w_ref[1:2, :] * d1) + w_ref[0:1, :] * d2

    nxt_map = lambda p, j, i: (p, jnp.minimum((i + 1) * sub, n // SUBLANES - 1), j)
    return pl.pallas_call(
        body, name=name, grid=(2, nj, ni),
        in_specs=[BS((1, tr, tc), lambda p, j, i: (p, i, j)), BS((1, SUBLANES, tc), nxt_map),
                  BS((3, tc), lambda p, j, i: (0, p * nj + j))],
        out_specs=BS((tr, tc), lambda p, j, i: (i, p * nj + j)), out_shape=SDS((n, 2 * f), F32),
        compiler_params=_cparams(("parallel", "parallel", "parallel")))(dc, dc, cw)


def conv_glu(u, cw, cb, name):
    def fwd(u, cw, cb):
        return _conv_fwd_call(u, cw, cb, name + "_f"), (u, cw, cb)

    def bwd(res, da):
        u, cw, cb = res
        dc, dw, db = _conv_bwd_dc_call(u, cw, cb, da, name + "_bc")
        du = _conv_bwd_du_call(dc, cw, name + "_bu")
        return du, jnp.concatenate([dw[0], dw[1]], axis=-1), jnp.concatenate([db[0], db[1]], axis=-1)

    return _op(fwd, bwd)(u, cw, cb)


def _loss_call(y, t, n_real, name):
    n, c = y.shape
    tr = _rows_tile(n, c, 1 << 20)

    def body(y_ref, t_ref, dy_ref, l_ref):
        i = pl.program_id(0)
        row = i * tr + lax.broadcasted_iota(jnp.int32, (tr, c), 0)
        real = (row >= N_META) & (row < N_META + n_real)
        e = jnp.where(real, y_ref[...] - t_ref[...], 0.0)
        dy_ref[...] = e * (1.0 / c)

        @pl.when(i == 0)
        def _():
            l_ref[...] = jnp.zeros_like(l_ref)

        l_ref[...] += 0.5 * jnp.sum(jnp.sum(e * e, axis=-1, keepdims=True) * (1.0 / c), axis=0, keepdims=True)

    spec = BS((tr, c), lambda i: (i, 0))
    return pl.pallas_call(
        body, name=name, grid=(n // tr,), in_specs=[spec, spec],
        out_specs=[spec, BS((1, 1), lambda i: (0, 0))], out_shape=[SDS((n, c), F32), SDS((1, 1), F32)],
        compiler_params=_cparams(("arbitrary",)))(y, t)


def _to_heads(x, nh):
    n = x.shape[0]
    return x.reshape(n, nh, x.shape[1] // nh).transpose(1, 0, 2)


def _from_heads(x):
    h, n, d = x.shape
    return x.transpose(1, 0, 2).reshape(n, h * d)


def _head_norm(x, g, denom, name):
    h, n, d = x.shape
    return rms_norm(x.reshape(h * n, d), g, denom, name).reshape(h, n, d)


def _head_norm_rope(x, g, cos, sin, name):
    h, n, d = x.shape
    return rms_norm_rope(x.reshape(h * n, d), g, cos, sin, MLA_QK, name).reshape(h, n, d)


def _pad_in_cols(w):
    z = lambda k: jnp.zeros(w.shape[:-1] + (k,), w.dtype)
    return jnp.concatenate([w[..., :1544], z(120), w[..., 1544:1960], z(96), w[..., 1960:], z(128)], axis=-1)


def _pad_q_up(w):
    s = w.shape[:-1]
    w = w.reshape(s + (HEADS, MLA_QK))
    w = jnp.concatenate([w, jnp.zeros(s + (HEADS, LANES - MLA_QK), w.dtype)], axis=-1)
    return w.reshape(s + (HEADS * LANES,))


LAYERED = BIG + ("ffn_conv_w",)
BEFORE_MLA = ("w_in", "mla_w_q_up", "mla_w_kv_up")
AFTER_MLA = ("w_branch", "w_o", "ffn_w_up", "ffn_w_down")


def _assemble_layer(parts):
    out = {k: jnp.concatenate([v[i] for i in range(N_CHIPS)], axis=SHARD_AXIS[k] - 1) for k, v in parts.items()}
    if "w_in" in out:
        out["w_in"] = _pad_in_cols(out["w_in"])
    if "mla_w_q_up" in out:
        out["mla_w_q_up"] = _pad_q_up(out["mla_w_q_up"])
    return out


def _layer_parts(names, gathered):
    return {k: g.reshape((N_CHIPS,) + SHARD_SHAPE[k][1:]) for k, g in zip(names, gathered)}


def _rope_tables(n):
    half = MLA_ROPE // 2
    freqs = ROPE_THETA ** (-jnp.arange(half, dtype=F32) / half)
    ang = jnp.arange(n).astype(F32)[:, None] * freqs[None, :]
    cos, sin = jnp.cos(ang), jnp.sin(ang)
    one, zero = jnp.ones((n, MLA_NOPE), F32), jnp.zeros((n, MLA_NOPE), F32)
    tail1, tail0 = jnp.ones((n, LANES - MLA_QK), F32), jnp.zeros((n, LANES - MLA_QK), F32)
    return (jnp.concatenate([one, cos, cos, tail1], axis=1), jnp.concatenate([zero, sin, sin, tail0], axis=1))


def _pad_lanes(g, width):
    return jnp.concatenate([g, jnp.zeros((width - g.shape[0],), g.dtype)]).reshape(1, width)


PROJ_SEGMENTS = ((O_FQ, 512), (O_FK, 512), (O_FV, 512), (O_FF, HEADS), (O_CQ, MLA_Q_RANK), (O_CKV, MLA_KV_RANK),
                 (O_KR, MLA_ROPE), (O_SQ, 512), (O_SK, 128), (O_SV, 128), (O_G, D_MODEL), (O_G + D_MODEL, D_MODEL),
                 (O_G + 2 * D_MODEL, D_MODEL))


def _split_proj(proj):
    def fwd(x):
        return tuple(x[:, s:s + w] for s, w in PROJ_SEGMENTS), None

    def bwd(_, cts):
        rows, parts, pos = cts[0].shape[0], [], 0
        for (s, w), ct in zip(PROJ_SEGMENTS, cts):
            if s > pos:
                parts.append(jnp.zeros((rows, s - pos), F32))
            parts.append(ct)
            pos = s + w
        parts.append(jnp.zeros((rows, IN_PAD - pos), F32))
        return (jnp.concatenate(parts, axis=1),)

    return _op(fwd, bwd)(proj)


def _trunk(eps, eps_cw, sinks, meta, small, x, w0, late):
    assert DEPTH == 2
    seq = x.shape[0]
    n = -(-(N_META + seq) // ROW_PAD) * ROW_PAD
    ew = _assemble_layer(eps)
    cos, sin = _rope_tables(n)
    slopes = jnp.exp2(-8.0 * jnp.arange(1, HEADS + 1, dtype=F32) / HEADS)
    h = jnp.concatenate([meta, x, jnp.zeros((n - N_META - seq, D_MODEL), F32)], axis=0)
    wb = w0
    for l in range(DEPTH):
        p = f"l{l}_"
        row = lambda name: small[name][l].reshape(1, -1)
        xn = rms_norm(h, row("norm1_g"), D_MODEL, p + "norm1")
        proj = linear(xn, wb["w_in"], ew["w_in"], p + "win")
        p_fq, p_fk, p_fv, p_ff, p_cq, p_ckv, p_kr, p_sq, p_sk, p_sv, g0, g1, g2 = _split_proj(proj)
        fq = _head_norm(_to_heads(p_fq, HEADS), row("fox_q_g"), HEAD_DIM, p + "fqn")
        fk = _head_norm(_to_heads(p_fk, HEADS), row("fox_k_g"), HEAD_DIM, p + "fkn")
        fv = _to_heads(p_fv, HEADS)
        c = forget_cumsum(p_ff.T, small["fox_forget_b"][l].reshape(HEADS, 1), p + "fgate")
        if l == 0:
            rest = [a for k, a in zip(LAYERED, late) if k != "w_in"]
            out_a, got, carriers = causal_attention(fq, fk, fv, c, HEAD_DIM ** -0.5, p + "fox", rest, 0, BIG,
                                                    sinks[1])
            wb = dict(wb, **_assemble_layer(_layer_parts(LAYERED[1:], _fill_own(got, [a[0] for a in rest]))))
            ew1 = _assemble_layer(dict(zip(BIG, carriers)))
        else:
            out_a = causal_attention(fq, fk, fv, c, HEAD_DIM ** -0.5, p + "fox")
        cqn = rms_norm(p_cq, row("mla_q_a_g"), MLA_Q_RANK, p + "cqn")
        q = _to_heads(linear(cqn, wb["mla_w_q_up"], ew["mla_w_q_up"], p + "qup"), HEADS)
        q = _head_norm_rope(q, _pad_lanes(small["mla_q_g"][l], LANES), cos, sin, p + "mqn")
        ckvn = rms_norm(p_ckv, row("mla_kv_a_g"), MLA_KV_RANK, p + "ckvn")
        kv = _to_heads(linear(ckvn, wb["mla_w_kv_up"], ew["mla_w_kv_up"], p + "kvup"), HEADS)
        kr = jnp.broadcast_to(p_kr[None], (HEADS, n, MLA_ROPE))
        k = jnp.concatenate([kv[..., :MLA_NOPE], kr, jnp.zeros((HEADS, n, LANES - MLA_QK), F32)], axis=-1)
        k = _head_norm_rope(k, _pad_lanes(small["mla_k_g"][l], LANES), cos, sin, p + "mkn")
        if l == 0:
            out_b, got, carriers = causal_attention(q, k, kv[..., MLA_NOPE:], None, MLA_QK ** -0.5, p + "mla", late,
                                                    1, AFTER_MLA, sinks[0])
            w1 = _assemble_layer(_layer_parts(LAYERED, _fill_own(got, [a[1] for a in late])))
            ew = dict(ew, **_assemble_layer(dict(zip(AFTER_MLA, carriers))))
        else:
            out_b = causal_attention(q, k, kv[..., MLA_NOPE:], None, MLA_QK ** -0.5, p + "mla")
        sq = _head_norm(_to_heads(p_sq, HEADS), row("swa_q_g"), HEAD_DIM, p + "sqn")
        sk = _head_norm(_to_heads(p_sk, SWA_KV_HEADS), row("swa_k_g"), HEAD_DIM, p + "skn")
        sv = _to_heads(p_sv, SWA_KV_HEADS)
        out_c = window_attention(sq, sk, sv, small["swa_sinks"][l], slopes, HEAD_DIM ** -0.5, p + "swa")
        ys = [linear(_from_heads(o), wb["w_branch"][i], ew["w_branch"][i], p + f"br{i}")
              for i, o in enumerate((out_a, out_b, out_c))]
        merged = gated_merge([g0, g1, g2], ys, p + "merge")
        h = linear(merged, wb["w_o"], ew["w_o"], p + "wo", res=h)
        xn2 = rms_norm(h, row("norm2_g"), D_MODEL, p + "norm2")
        u = linear(xn2, wb["ffn_w_up"], ew["ffn_w_up"], p + "wup")
        act = conv_glu(u, lax.stop_gradient(wb["ffn_conv_w"]) + eps_cw[l], row("ffn_conv_b"), p + "conv")
        h = linear(act, wb["ffn_w_down"], ew["ffn_w_down"], p + "wdown", res=h)
        wb, ew = w1, ew1
    return h


def _local_step(x, target, w0, late, meta, small):
    seq = x.shape[0]
    eps = {k: jnp.zeros((N_CHIPS,) + SHARD_SHAPE[k][1:], F32) for k in BEFORE_MLA}
    eps_cw = jnp.zeros((DEPTH, 3, 2 * D_FF), F32)
    sinks = tuple([jnp.zeros((2, HALVED[k][1] // 2, HALVED[k][2]), F32) for k in names] for names in (AFTER_MLA, BIG))
    y, vjp = jax.vjp(lambda e, ec, sk, mt, s, xx: _trunk(e, ec, sk, mt, s, xx, w0, late),
                     eps, eps_cw, sinks, meta, small, x)
    n = y.shape[0]
    tpad = jnp.concatenate([jnp.zeros((N_META, D_MODEL), F32), target, jnp.zeros((n - N_META - seq, D_MODEL), F32)])
    dy, loss = _loss_call(y, tpad, seq, "loss")
    g_eps, g_cw, g_l1, g_meta, g_small, g_x = vjp(dy)
    return loss[0, 0], g_x, g_eps, g_l1, g_cw, g_meta, g_small


def _pack_rows(shapes, mult):
    total = sum(_size(s) for s in shapes)
    rows = -(-total // LANES)
    return -(-rows // mult) * mult


def _size(shape):
    n = 1
    for d in shape:
        n *= d
    return n


def _pack(arrs, rows, dtype):
    flat = [a.reshape(-1).astype(dtype) for a in arrs]
    used = sum(a.size for a in flat)
    flat.append(jnp.zeros((rows * LANES - used,), dtype))
    return jnp.concatenate(flat).reshape(rows, LANES)


def _unpack(p, shapes):
    flat = p.reshape(-1)
    out, off = [], 0
    for s in shapes:
        out.append(flat[off:off + _size(s)].reshape(s))
        off += _size(s)
    return out


MESH = pl.DeviceIdType.MESH
ANY = pl.BlockSpec(memory_space=pl.ANY)


def _me():
    return lax.axis_index("x"), lax.axis_index("y"), lax.axis_index("c")


def _remote(src, dst, send_sems, recv_sems, idx, dev):
    return pltpu.make_async_remote_copy(src_ref=src, dst_ref=dst, send_sem=send_sems.at[idx], recv_sem=recv_sems.at[idx],
                                        device_id=dev, device_id_type=MESH)


def _layer_gather(ins, outs, send_sems, recv_sems, layer, base):
    x, y, _ = _me()
    j = 2 * x + y
    sibling = (x, y, 1 - layer)
    chips = [(1 - x, y), (x, 1 - y), (1 - x, 1 - y)]

    def ici(p, r):
        cx, cy = chips[r]
        return _remote(ins[p].at[layer], outs[p].at[j], send_sems, recv_sems, base + 6 * p + r, (cx, cy, layer))

    def d2d(p, r):
        cx, cy = chips[r]
        blk = outs[p].at[2 * cx + cy]
        return _remote(blk, blk, send_sems, recv_sems, base + 6 * p + 3 + r, sibling)

    pairs = [(p, r) for p in range(len(ins)) for r in range(3)]

    def start():
        for p, r in pairs:
            ici(p, r).start()

    def forward():
        for p, r in pairs:
            ici(p, r).wait_recv()
            d2d(p, r).start()

    def drain():
        for p, r in pairs:
            ici(p, r).wait_send()
            d2d(p, r).wait_send()

    def receive():
        for p, r in pairs:
            d2d(p, r).wait_recv()

    return start, forward, drain, receive


def _fill_own(outs, own):
    j = 2 * lax.axis_index("x") + lax.axis_index("y")
    return [lax.dynamic_update_index_in_dim(o, a, j, 0) for o, a in zip(outs, own)]


def _gather_early(meta, arrs, name):
    npk = len(arrs)

    def body(*refs):
        m_in, ins = refs[0], refs[1:npk + 1]
        m_out, outs = refs[npk + 1], refs[npk + 2:2 * npk + 2]
        send_sems, recv_sems = refs[2 * npk + 2:]
        x, y, c = _me()
        j = 2 * x + y
        sibling = (x, y, 1 - c)
        chips = [(1 - x, y), (x, 1 - y), (1 - x, 1 - y)]
        start, forward, drain, receive = _layer_gather(ins, outs, send_sems, recv_sems, 0, 6)
        sends = []
        for r, (cx, cy) in enumerate(chips):
            cp = _remote(m_in.at[c], m_out.at[j, c], send_sems, recv_sems, r, (cx, cy, c))
            cp.start()
            sends.append(cp)
        pl.when(c == 0)(start)
        for r, (cx, cy) in enumerate(chips):
            blk = m_out.at[2 * cx + cy, c]
            _remote(blk, blk, send_sems, recv_sems, r, sibling).wait_recv()
            fw = _remote(blk, blk, send_sems, recv_sems, 3 + r, sibling)
            fw.start()
            sends.append(fw)
        for r, (cx, cy) in enumerate(chips):
            blk = m_out.at[2 * cx + cy, 1 - c]
            _remote(blk, blk, send_sems, recv_sems, 3 + r, sibling).wait_recv()
        for cp in sends:
            cp.wait_send()

        @pl.when(c == 0)
        def _():
            forward()
            drain()

        pl.when(c == 1)(receive)

    nsem = 6 + 6 * npk
    res = pl.pallas_call(
        body, name=name, in_specs=[ANY] * (npk + 1), out_specs=[ANY] * (npk + 1),
        out_shape=[SDS((N_CHIPS,) + meta.shape, meta.dtype)] + [SDS((N_CHIPS,) + a.shape[1:], a.dtype) for a in arrs],
        scratch_shapes=[pltpu.SemaphoreType.DMA((nsem,)), pltpu.SemaphoreType.DMA((nsem,))],
        compiler_params=pltpu.CompilerParams(has_side_effects=True))(meta, *arrs)
    return _fill_own(res[:1], [meta])[0], _fill_own(res[1:], [a[0] for a in arrs])


def _pair_exchange(gs, name):
    npk = len(gs)

    def body(*refs):
        ins, outs = refs[:npk], refs[npk:2 * npk]
        send_sems, recv_sems = refs[2 * npk:]
        x, y, c = _me()
        cps = [_remote(ins[p].at[:, 1 - c], outs[p], send_sems, recv_sems, p, (x, y, 1 - c)) for p in range(npk)]
        for cp in cps:
            cp.start()
        for cp in cps:
            cp.wait()

    return pl.pallas_call(
        body, name=name, in_specs=[ANY] * npk, out_specs=[ANY] * npk,
        out_shape=[SDS((N_CHIPS,) + g.shape[2:], g.dtype) for g in gs],
        scratch_shapes=[pltpu.SemaphoreType.DMA((npk,)), pltpu.SemaphoreType.DMA((npk,))],
        compiler_params=pltpu.CompilerParams(has_side_effects=True))(*gs)


def _chip_exchange(ss, small, name):
    npk = len(ss)

    def body(*refs):
        ins, sm_ref = refs[:npk], refs[npk]
        outs, sa_ref = refs[npk + 1:2 * npk + 1], refs[2 * npk + 1]
        send_sems, recv_sems, loc_sem = refs[2 * npk + 2:]
        x, y, c = _me()
        me = 4 * x + 2 * y + c
        lc = pltpu.make_async_copy(sm_ref, sa_ref.at[me], loc_sem.at[0])
        lc.start()
        cps = []
        for p in range(npk):
            for r, (cx, cy) in enumerate([(1 - x, y), (x, 1 - y), (1 - x, 1 - y)]):
                cp = _remote(ins[p].at[2 * cx + cy], outs[p].at[r], send_sems, recv_sems, 3 * p + r, (cx, cy, c))
                cp.start()
                cps.append(cp)
        base = 3 * npk - 1
        for mask in range(1, N_DEV):
            px, py, pc = x ^ (mask >> 2), y ^ ((mask >> 1) & 1), c ^ (mask & 1)
            cp = _remote(sm_ref, sa_ref.at[me], send_sems, recv_sems, base + mask, (px, py, pc))
            cp.start()
            cps.append(cp)
        for p in range(npk):
            for r in range(3):
                _remote(outs[p].at[r], outs[p].at[r], send_sems, recv_sems, 3 * p + r, (x, y, c)).wait_recv()
        for mask in range(1, N_DEV):
            src = 4 * (x ^ (mask >> 2)) + 2 * (y ^ ((mask >> 1) & 1)) + (c ^ (mask & 1))
            _remote(sa_ref.at[src], sa_ref.at[src], send_sems, recv_sems, base + mask, (x, y, c)).wait_recv()
        for cp in cps:
            cp.wait_send()
        lc.wait()

    nsem = 3 * npk + N_DEV - 1
    res = pl.pallas_call(
        body, name=name, in_specs=[ANY] * (npk + 1), out_specs=[ANY] * (npk + 1),
        out_shape=[SDS((3,) + s.shape[1:], s.dtype) for s in ss] + [SDS((N_DEV,) + small.shape, small.dtype)],
        scratch_shapes=[pltpu.SemaphoreType.DMA((nsem,)), pltpu.SemaphoreType.DMA((nsem,)),
                        pltpu.SemaphoreType.DMA((1,))],
        compiler_params=pltpu.CompilerParams(has_side_effects=True))(*ss, small)
    return res[:npk], res[npk]


def _half_exchange(ghs, name):
    npk = len(ghs)

    def body(*refs):
        ins, outs = refs[:npk], refs[npk:2 * npk]
        send_sems, recv_sems = refs[2 * npk:]
        x, y, c = _me()
        cps = [_remote(ins[p], outs[p], send_sems, recv_sems, p, (x, y, 1 - c)) for p in range(npk)]
        for cp in cps:
            cp.start()
        for cp in cps:
            cp.wait()

    return pl.pallas_call(
        body, name=name, in_specs=[ANY] * npk, out_specs=[ANY] * npk, out_shape=[SDS(g.shape, g.dtype) for g in ghs],
        scratch_shapes=[pltpu.SemaphoreType.DMA((npk,)), pltpu.SemaphoreType.DMA((npk,))],
        compiler_params=pltpu.CompilerParams(has_side_effects=True))(*ghs)


def _add_tile(rows, cols):
    return _div_tile(rows, max(16, (1 << 19) // max(cols, LANES)), 16)


def _pair_add(g, r1, c_idx, name):
    _, rows, cols = r1.shape
    tr = _add_tile(rows, cols)

    def body(c_ref, g_ref, r_ref, o_ref, ob_ref):
        s = g_ref[0] + r_ref[...]
        o_ref[...] = s
        ob_ref[...] = s.astype(BF16)

    own = BS((1, tr, cols), lambda k, i, c: (k, i, 0))
    return pl.pallas_call(
        body, name=name,
        grid_spec=pltpu.PrefetchScalarGridSpec(
            num_scalar_prefetch=1, grid=(N_CHIPS, rows // tr),
            in_specs=[BS((1, 1, tr, cols), lambda k, i, c: (k, c[0], i, 0)), own], out_specs=[own, own]),
        out_shape=[SDS(r1.shape, F32), SDS(r1.shape, BF16)],
        compiler_params=_cparams(("parallel", "parallel")))(c_idx, g, r1)


def _chip_add(s1, r2, j_idx, name):
    _, rows, cols = s1.shape
    tr = _add_tile(rows, cols)

    def body(j_ref, s_ref, r_ref, o_ref):
        o_ref[...] = ((s_ref[0] + r_ref[0].astype(F32)) + r_ref[1].astype(F32)) + r_ref[2].astype(F32)

    return pl.pallas_call(
        body, name=name,
        grid_spec=pltpu.PrefetchScalarGridSpec(
            num_scalar_prefetch=1, grid=(rows // tr,),
            in_specs=[BS((1, tr, cols), lambda i, j: (j[0], i, 0)), BS((3, tr, cols), lambda i, j: (0, i, 0))],
            out_specs=BS((tr, cols), lambda i, j: (i, 0))),
        out_shape=SDS((rows, cols), F32), compiler_params=_cparams(("parallel",)))(j_idx, s1, r2)


def _adamw_math(w, g, m, v):
    m = ADAM_B1 * m + (1.0 - ADAM_B1) * g
    v = ADAM_B2 * v + (1.0 - ADAM_B2) * (g * g)
    m_hat = m / (1.0 - ADAM_B1 ** ADAM_STEP)
    v_hat = v / (1.0 - ADAM_B2 ** ADAM_STEP)
    delta = -ADAM_LR * (m_hat / (jnp.sqrt(v_hat) + ADAM_EPS) + ADAM_WD * w)
    return delta, m, v


def _adamw(w, gh, go, m, v, c_idx, name):
    _, rows, cols = w.shape
    tr = _add_tile(rows, cols)

    def body(c_ref, w_ref, gh_ref, go_ref, m_ref, v_ref, g_out, d_out, m_out, v_out):
        g = jnp.where(pl.program_id(0) == c_ref[0], gh_ref[...], go_ref[...])
        g_out[0] = g
        d_out[0], m_out[0], v_out[0] = _adamw_math(w_ref[0], g, m_ref[0], v_ref[0])

    full = BS((1, tr, cols), lambda hf, i, c: (hf, i, 0))
    half = BS((tr, cols), lambda hf, i, c: (i, 0))
    return pl.pallas_call(
        body, name=name,
        grid_spec=pltpu.PrefetchScalarGridSpec(
            num_scalar_prefetch=1, grid=(2, rows // tr), in_specs=[full, half, half, full, full],
            out_specs=[full] * 4),
        out_shape=[SDS(w.shape, F32)] * 4, compiler_params=_cparams(("parallel", "parallel")))(c_idx, w, gh, go, m, v)


def _adamw_layers(w, gh0, go0, g1, m, v, c_idx, name):
    _, _, rows, cols = w.shape
    tr = _add_tile(rows, cols)

    def body(c_ref, w_ref, gh_ref, go_ref, g1_ref, m_ref, v_ref, g_out, d_out, m_out, v_out):
        g0 = jnp.where(pl.program_id(1) == c_ref[0], gh_ref[...], go_ref[...])
        g = jnp.where(pl.program_id(0) == 0, g0, g1_ref[0])
        g_out[0, 0] = g
        d_out[0, 0], m_out[0, 0], v_out[0, 0] = _adamw_math(w_ref[0, 0], g, m_ref[0, 0], v_ref[0, 0])

    full = BS((1, 1, tr, cols), lambda l, hf, i, c: (l, hf, i, 0))
    half = BS((tr, cols), lambda l, hf, i, c: (i, 0))
    return pl.pallas_call(
        body, name=name,
        grid_spec=pltpu.PrefetchScalarGridSpec(
            num_scalar_prefetch=1, grid=(2, 2, rows // tr),
            in_specs=[full, half, half, BS((1, tr, cols), lambda l, hf, i, c: (hf, i, 0)), full, full],
            out_specs=[full] * 4),
        out_shape=[SDS(w.shape, F32)] * 4,
        compiler_params=_cparams(("parallel", "parallel", "parallel")))(c_idx, w, gh0, go0, g1, m, v)


def _adamw_two(w, g0, g1, m, v, name):
    _, _, rows, cols = w.shape
    tr = _add_tile(rows, cols)

    def body(w_ref, g0_ref, g1_ref, m_ref, v_ref, g_out, d_out, m_out, v_out):
        g = jnp.where(pl.program_id(0) == 0, g0_ref[0], g1_ref[0])
        g_out[0, 0] = g
        d_out[0, 0], m_out[0, 0], v_out[0, 0] = _adamw_math(w_ref[0, 0], g, m_ref[0, 0], v_ref[0, 0])

    full = BS((1, 1, tr, cols), lambda l, hf, i: (l, hf, i, 0))
    half = BS((1, tr, cols), lambda l, hf, i: (hf, i, 0))
    return pl.pallas_call(
        body, name=name, grid=(2, 2, rows // tr), in_specs=[full, half, half, full, full], out_specs=[full] * 4,
        out_shape=[SDS(w.shape, F32)] * 4,
        compiler_params=_cparams(("parallel", "parallel", "parallel")))(w, g0, g1, m, v)


def _sum_devices(sa, name):
    def body(sa_ref, g_out):
        g = sa_ref[0]
        for d in range(1, N_DEV):
            g = g + sa_ref[d]
        g_out[...] = g

    return pl.pallas_call(body, name=name, out_shape=SDS(sa.shape[1:], F32),
                          compiler_params=pltpu.CompilerParams(vmem_limit_bytes=VMEM_LIMIT))(sa)


def _adamw_small(ws, gs, ms, vs, name):
    k = len(ws)

    def body(*refs):
        ins, outs = refs[:4 * k], refs[4 * k:]
        for i in range(k):
            d, m, v = _adamw_math(ins[i][...], ins[k + i][...], ins[2 * k + i][...], ins[3 * k + i][...])
            outs[i][...], outs[k + i][...], outs[2 * k + i][...] = d, m, v

    return pl.pallas_call(body, name=name, out_shape=[SDS(w.shape, F32) for w in ws] * 3,
                          compiler_params=pltpu.CompilerParams(vmem_limit_bytes=VMEM_LIMIT))(*ws, *gs, *ms, *vs)


HALVED = {"w_in": (2, 1024, 1450), "mla_w_q_up": (2, 256, 192), "mla_w_kv_up": (2, 128, 256),
          "w_branch": (2, 1536, 256), "w_o": (2, 256, 1024), "ffn_w_up": (2, 1024, 1408),
          "ffn_w_down": (2, 704, 1024), "ffn_conv_w": (2, 3, 1408), "meta_tokens": (2, 8, 256)}
SMALL_SHAPE = {"norm1_g": (2, 1024), "fox_forget_b": (2, 8), "fox_q_g": (2, 64), "fox_k_g": (2, 64),
               "mla_q_a_g": (2, 256), "mla_kv_a_g": (2, 128), "mla_q_g": (2, 96), "mla_k_g": (2, 96),
               "swa_q_g": (2, 64), "swa_k_g": (2, 64), "swa_sinks": (2, 8), "norm2_g": (2, 1024),
               "ffn_conv_b": (2, 5632)}
SMALL_ROWS = _pack_rows([SMALL_SHAPE[k] for k in SMALL] + [(1,)], SUBLANES)


def kernel(x, meta_tokens, norm1_g, w_in, fox_forget_b, fox_q_g, fox_k_g, mla_q_a_g, mla_w_q_up, mla_kv_a_g, mla_w_kv_up, mla_q_g, mla_k_g, swa_q_g, swa_k_g, swa_sinks, w_branch, w_o, norm2_g, ffn_w_up, ffn_conv_w, ffn_conv_b, ffn_w_down, loss_target, m_meta_tokens, m_norm1_g, m_w_in, m_fox_forget_b, m_fox_q_g, m_fox_k_g, m_mla_q_a_g, m_mla_w_q_up, m_mla_kv_a_g, m_mla_w_kv_up, m_mla_q_g, m_mla_k_g, m_swa_q_g, m_swa_k_g, m_swa_sinks, m_w_branch, m_w_o, m_norm2_g, m_ffn_w_up, m_ffn_conv_w, m_ffn_conv_b, m_ffn_w_down, v_meta_tokens, v_norm1_g, v_w_in, v_fox_forget_b, v_fox_q_g, v_fox_k_g, v_mla_q_a_g, v_mla_w_q_up, v_mla_kv_a_g, v_mla_w_kv_up, v_mla_q_g, v_mla_k_g, v_swa_q_g, v_swa_k_g, v_swa_sinks, v_w_branch, v_w_o, v_norm2_g, v_ffn_w_up, v_ffn_conv_w, v_ffn_conv_b, v_ffn_w_down):
    w = dict(meta_tokens=meta_tokens, norm1_g=norm1_g, w_in=w_in, fox_forget_b=fox_forget_b, fox_q_g=fox_q_g,
             fox_k_g=fox_k_g, mla_q_a_g=mla_q_a_g, mla_w_q_up=mla_w_q_up, mla_kv_a_g=mla_kv_a_g,
             mla_w_kv_up=mla_w_kv_up, mla_q_g=mla_q_g, mla_k_g=mla_k_g, swa_q_g=swa_q_g, swa_k_g=swa_k_g,
             swa_sinks=swa_sinks, w_branch=w_branch, w_o=w_o, norm2_g=norm2_g, ffn_w_up=ffn_w_up,
             ffn_conv_w=ffn_conv_w, ffn_conv_b=ffn_conv_b, ffn_w_down=ffn_w_down)
    m = dict(meta_tokens=m_meta_tokens, norm1_g=m_norm1_g, w_in=m_w_in, fox_forget_b=m_fox_forget_b,
             fox_q_g=m_fox_q_g, fox_k_g=m_fox_k_g, mla_q_a_g=m_mla_q_a_g, mla_w_q_up=m_mla_w_q_up,
             mla_kv_a_g=m_mla_kv_a_g, mla_w_kv_up=m_mla_w_kv_up, mla_q_g=m_mla_q_g, mla_k_g=m_mla_k_g,
             swa_q_g=m_swa_q_g, swa_k_g=m_swa_k_g, swa_sinks=m_swa_sinks, w_branch=m_w_branch, w_o=m_w_o,
             norm2_g=m_norm2_g, ffn_w_up=m_ffn_w_up, ffn_conv_w=m_ffn_conv_w, ffn_conv_b=m_ffn_conv_b,
             ffn_w_down=m_ffn_w_down)
    v = dict(meta_tokens=v_meta_tokens, norm1_g=v_norm1_g, w_in=v_w_in, fox_forget_b=v_fox_forget_b,
             fox_q_g=v_fox_q_g, fox_k_g=v_fox_k_g, mla_q_a_g=v_mla_q_a_g, mla_w_q_up=v_mla_w_q_up,
             mla_kv_a_g=v_mla_kv_a_g, mla_w_kv_up=v_mla_w_kv_up, mla_q_g=v_mla_q_g, mla_k_g=v_mla_k_g,
             swa_q_g=v_swa_q_g, swa_k_g=v_swa_k_g, swa_sinks=v_swa_sinks, w_branch=v_w_branch, w_o=v_w_o,
             norm2_g=v_norm2_g, ffn_w_up=v_ffn_w_up, ffn_conv_w=v_ffn_conv_w, ffn_conv_b=v_ffn_conv_b,
             ffn_w_down=v_ffn_w_down)
    xi, yi, ci = _me()
    c_idx = ci.astype(jnp.int32).reshape(1)
    j_idx = (2 * xi + yi).astype(jnp.int32).reshape(1)

    sh_names = BIG + FINE
    local = {k: (w[k].astype(BF16) if k in BIG else w[k]).reshape(HALVED[k]) for k in sh_names}
    late = [local[k] for k in LAYERED]
    meta_g, early = _gather_early(local["meta_tokens"], [local["w_in"]], "gather_early")
    meta = jnp.concatenate([meta_g[i].reshape(SHARD_SHAPE["meta_tokens"]) for i in range(N_CHIPS)], axis=1)
    w0 = _assemble_layer(_layer_parts(("w_in",), early))
    small = {k: w[k] for k in SMALL}

    loss, g_x, g_eps, g_red, g_cw, g_meta, g_small = _local_step(x[0], loss_target[0], w0, late, meta, small)
    g_after0, g_l1 = dict(zip(AFTER_MLA, g_red[0])), dict(zip(BIG, g_red[1]))

    quarter = {k: (2, HALVED[k][1] // 2, HALVED[k][2]) for k in BIG}
    last = BEFORE_MLA + FINE
    gs = [g_eps[k].reshape((N_CHIPS,) + quarter[k]) for k in BEFORE_MLA]
    for k, g in (("meta_tokens", g_meta), ("ffn_conv_w", g_cw)):
        gs.append(jnp.stack(jnp.split(g, N_CHIPS, axis=SHARD_AXIS[k])).reshape((N_CHIPS,) + HALVED[k]))
    spack = _pack([g_small[k] for k in SMALL] + [loss.reshape(1)], SMALL_ROWS, F32)
    r1 = _pair_exchange(gs, "grads_pair_exchange")
    s1 = [_pair_add(g, r, c_idx, "grads_pair_add_" + k) for g, r, k in zip(gs, r1, last)]
    r2, sa = _chip_exchange([s[1] for s in s1], spack, "grads_chip_exchange")
    gh = dict(zip(last, [_chip_add(s[0], r, j_idx, "grads_chip_add_" + k) for s, r, k in zip(s1, r2, last)]))
    go = dict(zip(last, _half_exchange([gh[k] for k in last], "grads_half_exchange")))

    grads, deltas, new_m, new_v = {}, {}, {}, {}
    for k in sh_names:
        if k in BIG:
            shp = (2,) + quarter[k]
            wk, mk, vk = w[k].reshape(shp), m[k].reshape(shp), v[k].reshape(shp)
            if k in BEFORE_MLA:
                outs = _adamw_layers(wk, gh[k], go[k], g_l1[k], mk, vk, c_idx, "adamw_" + k)
            else:
                outs = _adamw_two(wk, g_after0[k], g_l1[k], mk, vk, "adamw_" + k)
        else:
            outs = _adamw(w[k].reshape(HALVED[k]), gh[k], go[k], m[k].reshape(HALVED[k]), v[k].reshape(HALVED[k]),
                          c_idx, "adamw_" + k)
        for dst, o in zip((grads, deltas, new_m, new_v), outs):
            dst[k] = o.reshape(SHARD_SHAPE[k])
    sm_shapes = [SMALL_SHAPE[k] for k in SMALL] + [(1,)]
    g_sum = _unpack(_sum_devices(sa, "sum_small"), sm_shapes)
    res = _adamw_small([w[k] for k in SMALL], g_sum[:-1], [m[k] for k in SMALL], [v[k] for k in SMALL], "adamw_small")
    ns = len(SMALL)
    grads.update(zip(SMALL, g_sum[:-1]))
    for dst, vals in zip((deltas, new_m, new_v), (res[:ns], res[ns:2 * ns], res[2 * ns:])):
        dst.update(zip(SMALL, vals))
    total_loss = g_sum[-1][0]
    return (total_loss, g_x[None], *[grads[k] for k in WEIGHTS], *[deltas[k] for k in WEIGHTS],
            *[new_m[k] for k in WEIGHTS], *[new_v[k] for k in WEIGHTS])
```

```python
import jax
import jax.numpy as jnp
from jax import lax
from jax.experimental import pallas as pl
from jax.experimental.pallas import tpu as pltpu

F32 = jnp.float32
BF16 = jnp.bfloat16
SDS = jax.ShapeDtypeStruct
BS = pl.BlockSpec

D_MODEL = 1024
DEPTH = 2
N_META = 16
EPS = 1e-6
HEADS = 8
HEAD_DIM = 64
MLA_Q_RANK = 256
MLA_KV_RANK = 128
MLA_NOPE = 64
MLA_ROPE = 32
MLA_QK = MLA_NOPE + MLA_ROPE
ROPE_THETA = 10000.0
SWA_KV_HEADS = 2
WINDOW = 128
D_FF = 2816
IN_PAD = 6144
N_CHIPS = 4
N_DEV = 8

ADAM_LR = 0.001
ADAM_B1 = 0.9
ADAM_B2 = 0.999
ADAM_EPS = 1e-08
ADAM_WD = 0.01
ADAM_STEP = 10

LANES = 128
SUBLANES = 8
ROW_PAD = 128
CAUSAL_TILE = 384
NEG = -1e30
VMEM_LIMIT = 56 * 1024 * 1024

O_FQ, O_FK, O_FV, O_FF = 0, 512, 1024, 1536
O_CQ, O_CKV, O_KR = 1664, 1920, 2048
O_SQ, O_SK, O_SV, O_G = 2176, 2688, 2816, 2944

SHARD_AXIS = {"meta_tokens": 1, "w_in": 2, "mla_w_q_up": 2, "mla_w_kv_up": 2, "w_branch": 3, "w_o": 1,
              "ffn_w_up": 2, "ffn_conv_w": 2, "ffn_w_down": 1}
SHARD_SHAPE = {"meta_tokens": (16, 256), "w_in": (2, 1024, 1450), "mla_w_q_up": (2, 256, 192),
               "mla_w_kv_up": (2, 128, 256), "w_branch": (2, 3, 512, 256), "w_o": (2, 256, 1024),
               "ffn_w_up": (2, 1024, 1408), "ffn_conv_w": (2, 3, 1408), "ffn_w_down": (2, 704, 1024)}
BIG = ("w_in", "mla_w_q_up", "mla_w_kv_up", "w_branch", "w_o", "ffn_w_up", "ffn_w_down")
FINE = ("meta_tokens", "ffn_conv_w")
SMALL = ("norm1_g", "fox_forget_b", "fox_q_g", "fox_k_g", "mla_q_a_g", "mla_kv_a_g", "mla_q_g", "mla_k_g",
         "swa_q_g", "swa_k_g", "swa_sinks", "norm2_g", "ffn_conv_b")
WEIGHTS = ("meta_tokens", "norm1_g", "w_in", "fox_forget_b", "fox_q_g", "fox_k_g", "mla_q_a_g", "mla_w_q_up",
           "mla_kv_a_g", "mla_w_kv_up", "mla_q_g", "mla_k_g", "swa_q_g", "swa_k_g", "swa_sinks", "w_branch", "w_o",
           "norm2_g", "ffn_w_up", "ffn_conv_w", "ffn_conv_b", "ffn_w_down")


def _cparams(sem):
    return pltpu.CompilerParams(dimension_semantics=sem, vmem_limit_bytes=VMEM_LIMIT)


def _div_tile(n, cap, mult=SUBLANES):
    best = None
    for t in range(mult, min(n, cap) + 1, mult):
        if n % t == 0:
            best = t
    return best if best is not None else n


def _rows_tile(n, width, budget=2 << 20):
    return _div_tile(n, max(SUBLANES, budget // (4 * max(width, LANES))))


def _op(fwd, bwd):
    @jax.custom_vjp
    def op(*args):
        return fwd(*args)[0]
    op.defvjp(fwd, bwd)
    return op


def _rotate(v, cos, sin):
    lane = lax.broadcasted_iota(jnp.int32, v.shape, 1)
    rot = jnp.where(lane < MLA_NOPE + MLA_ROPE // 2, -pltpu.roll(v, LANES - MLA_ROPE // 2, 1),
                    pltpu.roll(v, MLA_ROPE // 2, 1))
    return v * cos + rot * sin


def _rms_fwd_call(x, g, denom, name, rot=None):
    n, c = x.shape
    tr = _rows_tile(n if rot is None else rot[0].shape[0], c)
    nt = None if rot is None else rot[0].shape[0] // tr

    def body(x_ref, g_ref, *rest):
        y_ref = rest[-1]
        xv = x_ref[...]
        ms = jnp.sum(xv * xv, axis=-1, keepdims=True) * (1.0 / denom)
        y = xv * lax.rsqrt(ms + EPS) * g_ref[...]
        y_ref[...] = y if rot is None else _rotate(y, rest[0][...], rest[1][...])

    ins, args = [BS((tr, c), lambda i: (i, 0)), BS((1, c), lambda i: (0, 0))], [x, g]
    if rot is not None:
        ins += [BS((tr, c), lambda i: (i % nt, 0))] * 2
        args += list(rot)
    return pl.pallas_call(
        body, name=name, grid=(n // tr,), in_specs=ins,
        out_specs=BS((tr, c), lambda i: (i, 0)), out_shape=SDS((n, c), F32),
        compiler_params=_cparams(("parallel",)))(*args)


def _rms_bwd_call(x, g, dy, denom, name, rot=None):
    n, c = x.shape
    tr = _rows_tile(n if rot is None else rot[0].shape[0], c)
    nt = None if rot is None else rot[0].shape[0] // tr

    def body(x_ref, g_ref, dy_ref, *rest):
        dx_ref, dg_ref = rest[-2:]
        xv = x_ref[...]
        dy = dy_ref[...]
        if rot is not None:
            dy = _rotate(dy, rest[0][...], -rest[1][...])
        ms = jnp.sum(xv * xv, axis=-1, keepdims=True) * (1.0 / denom)
        r = lax.rsqrt(ms + EPS)
        xh = xv * r
        dxh = dy * g_ref[...]
        dx_ref[...] = r * (dxh - xh * (jnp.sum(dxh * xh, axis=-1, keepdims=True) * (1.0 / denom)))

        @pl.when(pl.program_id(0) == 0)
        def _():
            dg_ref[...] = jnp.zeros_like(dg_ref)

        dg_ref[...] += jnp.sum(dy * xh, axis=0, keepdims=True)

    ins = [BS((tr, c), lambda i: (i, 0)), BS((1, c), lambda i: (0, 0)), BS((tr, c), lambda i: (i, 0))]
    args = [x, g, dy]
    if rot is not None:
        ins += [BS((tr, c), lambda i: (i % nt, 0))] * 2
        args += list(rot)
    return pl.pallas_call(
        body, name=name, grid=(n // tr,), in_specs=ins,
        out_specs=[BS((tr, c), lambda i: (i, 0)), BS((1, c), lambda i: (0, 0))],
        out_shape=[SDS((n, c), F32), SDS((1, c), F32)],
        compiler_params=_cparams(("arbitrary",)))(*args)


def rms_norm(x, g, denom, name):
    def fwd(x, g):
        return _rms_fwd_call(x, g, denom, name + "_f"), (x, g)

    def bwd(res, dy):
        return tuple(_rms_bwd_call(res[0], res[1], dy, denom, name + "_b"))

    return _op(fwd, bwd)(x, g)


def rms_norm_rope(x, g, cos, sin, denom, name):
    def fwd(x, g, cos, sin):
        return _rms_fwd_call(x, g, denom, name + "_f", (cos, sin)), (x, g, cos, sin)

    def bwd(res, dy):
        x, g, cos, sin = res
        dx, dg = _rms_bwd_call(x, g, dy, denom, name + "_b", (cos, sin))
        return dx, dg, jnp.zeros_like(cos), jnp.zeros_like(sin)

    return _op(fwd, bwd)(x, g, cos, sin)


def _mm_call(a, b, mode, res, name):
    if mode == "nn":
        (m, kc), n = a.shape, b.shape[1]
    elif mode == "nt":
        (m, kc), n = a.shape, b.shape[0]
    else:
        (kc, m), n = a.shape, b.shape[1]
    if mode == "tn":
        tk = _div_tile(kc, 528)
        tm = _div_tile(m, 1408, LANES)
        tn = _div_tile(n, 2048, LANES)
    else:
        tk = kc if kc <= 2816 else _div_tile(kc, 1024, LANES)
        tm = _div_tile(m, max(LANES, (9 << 19) // (4 * tk)))
        tn = _div_tile(n, 1408 if mode == "nt" else 512, LANES)
    nk = kc // tk
    dims = {"nn": (((1,), (0,)), ((), ())), "nt": (((1,), (1,)), ((), ())), "tn": (((0,), (0,)), ((), ()))}[mode]

    def body(*refs):
        if res is None:
            a_ref, b_ref, o_ref, acc_ref = refs
            r_ref = None
        else:
            a_ref, b_ref, r_ref, o_ref, acc_ref = refs
        k = pl.program_id(2)

        @pl.when(k == 0)
        def _():
            acc_ref[...] = jnp.zeros_like(acc_ref)

        acc_ref[...] += lax.dot_general(a_ref[...].astype(BF16), b_ref[...].astype(BF16), dims,
                                        preferred_element_type=F32)

        @pl.when(k == nk - 1)
        def _():
            if r_ref is None:
                o_ref[...] = acc_ref[...]
            else:
                o_ref[...] = r_ref[...] + acc_ref[...]

    a_spec = BS((tk, tm), lambda i, j, k: (k, i)) if mode == "tn" else BS((tm, tk), lambda i, j, k: (i, k))
    b_spec = BS((tn, tk), lambda i, j, k: (j, k)) if mode == "nt" else BS((tk, tn), lambda i, j, k: (k, j))
    o_spec = BS((tm, tn), lambda i, j, k: (i, j))
    ins, args = [a_spec, b_spec], [a, b]
    if res is not None:
        ins.append(o_spec)
        args.append(res)
    return pl.pallas_call(
        body, name=name, grid=(m // tm, n // tn, nk), in_specs=ins, out_specs=o_spec,
        out_shape=SDS((m, n), F32), scratch_shapes=[pltpu.VMEM((tm, tn), F32)],
        compiler_params=_cparams(("parallel", "parallel", "arbitrary")))(*args)


def linear(a, w, eps, name, res=None):
    if res is None:
        def fwd(a, w, eps):
            return _mm_call(a, w, "nn", None, name + "_f"), (a, w)

        def bwd(r, dc):
            a, w = r
            return (_mm_call(dc, w, "nt", None, name + "_da"), jnp.zeros_like(w),
                    _mm_call(a, dc, "tn", None, name + "_dw"))

        return _op(fwd, bwd)(a, w, eps)

    def fwd_r(a, w, eps, res):
        return _mm_call(a, w, "nn", res, name + "_f"), (a, w)

    def bwd_r(r, dc):
        a, w = r
        return (_mm_call(dc, w, "nt", None, name + "_da"), jnp.zeros_like(w),
                _mm_call(a, dc, "tn", None, name + "_dw"), dc)

    return _op(fwd_r, bwd_r)(a, w, eps, res)


def _gate_tile(n):
    return _div_tile(n, CAUSAL_TILE, LANES)


def _tri_dot(v, upper):
    ct = v.shape[1]
    r = lax.broadcasted_iota(jnp.int32, (ct, ct), 0)
    c = lax.broadcasted_iota(jnp.int32, (ct, ct), 1)
    tri = jnp.where((r <= c) if upper else (r >= c), 1.0, 0.0).astype(F32)
    return jnp.dot(v, tri, preferred_element_type=F32, precision=lax.Precision.HIGHEST)


def _gate_fwd_call(z, b, name):
    h, n = z.shape
    CT = _gate_tile(n)

    def body(z_ref, b_ref, c_ref, carry):
        @pl.when(pl.program_id(0) == 0)
        def _():
            carry[...] = jnp.zeros_like(carry)

        x = z_ref[...] + b_ref[...]
        ls = jnp.minimum(x, 0.0) - jnp.log(1.0 + jnp.exp(-jnp.abs(x)))
        c_ref[...] = _tri_dot(ls, True) + carry[...]
        carry[...] += jnp.sum(ls, axis=1, keepdims=True)

    return pl.pallas_call(
        body, name=name, grid=(n // CT,),
        in_specs=[BS((h, CT), lambda j: (0, j)), BS((h, 1), lambda j: (0, 0))],
        out_specs=BS((h, CT), lambda j: (0, j)), out_shape=SDS((h, n), F32),
        scratch_shapes=[pltpu.VMEM((h, 1), F32)],
        compiler_params=_cparams(("arbitrary",)))(z, b)


def _gate_bwd_call(z, b, dc, name):
    h, n = z.shape
    CT = _gate_tile(n)
    nt = n // CT

    def body(z_ref, b_ref, dc_ref, dz_ref, db_ref, carry):
        @pl.when(pl.program_id(0) == 0)
        def _():
            carry[...] = jnp.zeros_like(carry)
            db_ref[...] = jnp.zeros_like(db_ref)

        dcv = dc_ref[...]
        dls = _tri_dot(dcv, False) + carry[...]
        carry[...] += jnp.sum(dcv, axis=1, keepdims=True)
        x = z_ref[...] + b_ref[...]
        e = jnp.exp(-jnp.abs(x))
        dz = dls * jnp.where(x >= 0, e / (1.0 + e), 1.0 / (1.0 + e))
        dz_ref[...] = dz
        db_ref[...] += jnp.sum(dz, axis=1, keepdims=True)

    rev = lambda j: (0, nt - 1 - j)
    return pl.pallas_call(
        body, name=name, grid=(nt,),
        in_specs=[BS((h, CT), rev), BS((h, 1), lambda j: (0, 0)), BS((h, CT), rev)],
        out_specs=[BS((h, CT), rev), BS((h, 1), lambda j: (0, 0))],
        out_shape=[SDS((h, n), F32), SDS((h, 1), F32)],
        scratch_shapes=[pltpu.VMEM((h, 1), F32)],
        compiler_params=_cparams(("arbitrary",)))(z, b, dc)


def forget_cumsum(z, b, name):
    def fwd(z, b):
        return _gate_fwd_call(z, b, name + "_f"), (z, b)

    def bwd(res, dc):
        return tuple(_gate_bwd_call(res[0], res[1], dc, name + "_b"))

    return _op(fwd, bwd)(z, b)


NT_DIMS = (((1,), (1,)), ((), ()))
TN_DIMS = (((0,), (0,)), ((), ()))
HEADS_PER_STEP = 2


def _causal_tile(n):
    return CAUSAL_TILE if n % CAUSAL_TILE == 0 else ROW_PAD


def _causal_fwd_call(q, k, v, ck_r, fox, scale, name, late=None, late_layer=1):
    h, n, dk = q.shape
    dv = v.shape[2]
    t = _causal_tile(n)
    nq = n // t
    hb = HEADS_PER_STEP
    nl = 0 if late is None else len(late)
    n_in = 3 + int(fox) + nl

    def body(*refs):
        q_ref, k_ref, v_ref = refs[:3]
        ck_ref = refs[3] if fox else None
        o_ref, lse_ref = refs[n_in:n_in + 2]
        m_scr, l_scr, acc_scr = refs[n_in + 2 + nl:n_in + 5 + nl]
        qi = pl.program_id(1)
        if nl:
            start, forward, drain, receive = _layer_gather(refs[n_in - nl:n_in], refs[n_in + 2:n_in + 2 + nl],
                                                           refs[-2], refs[-1], late_layer, 0)
            hp, core = pl.program_id(0), lax.axis_index("c")
            last = (hp == h // hb - 1) & (qi == nq - 1)
            pl.when((hp == 0) & (qi == 0) & (core == late_layer))(start)
            pl.when((hp == h // hb // 2) & (qi == 0) & (core == late_layer))(forward)
            pl.when(last & (core == late_layer))(drain)
            pl.when(last & (core == 1 - late_layer))(receive)
        qbs = [q_ref[e].astype(BF16) for e in range(hb)]
        m_scr[...] = jnp.full(m_scr.shape, NEG, F32)
        l_scr[...] = jnp.zeros_like(l_scr)
        acc_scr[...] = jnp.zeros_like(acc_scr)

        def process(j, masked):
            off = pl.multiple_of(j * t, t)
            if masked:
                rows = lax.broadcasted_iota(jnp.int32, (t, t), 0)
                cols = lax.broadcasted_iota(jnp.int32, (t, t), 1)
                valid = cols <= rows
            for e in range(hb):
                kb = k_ref[e, pl.ds(off, t), :].astype(BF16)
                vb = v_ref[e, pl.ds(off, t), :].astype(BF16)
                s = lax.dot_general(qbs[e], kb, NT_DIMS, preferred_element_type=F32) * scale
                if fox:
                    s = s - ck_ref[e, j]
                if masked:
                    s = jnp.where(valid, s, NEG)
                m_old = m_scr[e]
                m_new = jnp.maximum(m_old, jnp.max(s, axis=1, keepdims=True))
                alpha = jnp.exp(m_old - m_new)
                p = jnp.exp(s - jnp.tile(m_new, (1, t // LANES)))
                l_scr[e] = alpha * l_scr[e] + jnp.sum(p, axis=1, keepdims=True)
                acc_scr[e] = alpha[:, :dv] * acc_scr[e] + jnp.dot(p.astype(BF16), vb, preferred_element_type=F32)
                m_scr[e] = m_new

        def step(j, carry):
            process(j, False)
            return carry

        lax.fori_loop(0, qi, step, 0)
        process(qi, True)
        for e in range(hb):
            l = l_scr[e]
            o_ref[e] = acc_scr[e] / l[:, :dv]
            lse_ref[e, 0] = jnp.transpose(m_scr[e] + jnp.log(l))[0:1, :]

    ins = [BS((hb, t, dk), lambda a, b: (a, b, 0)), BS((hb, n, dk), lambda a, b: (a, 0, 0)),
           BS((hb, n, dv), lambda a, b: (a, 0, 0))]
    args = [q, k, v]
    if fox:
        ins.append(BS((hb, nq, 1, t), lambda a, b: (a, 0, 0, 0)))
        args.append(ck_r)
    outs = [BS((hb, t, dv), lambda a, b: (a, b, 0)), BS((hb, 1, 1, t), lambda a, b: (a, b, 0, 0))]
    oshape = [SDS((h, n, dv), F32), SDS((h, nq, 1, t), F32)]
    scratch = [pltpu.VMEM((hb, t, LANES), F32), pltpu.VMEM((hb, t, LANES), F32), pltpu.VMEM((hb, t, dv), F32)]
    if nl:
        ins += [ANY] * nl
        args += list(late)
        outs += [ANY] * nl
        oshape += [SDS((N_CHIPS,) + a.shape[1:], a.dtype) for a in late]
        scratch += [pltpu.SemaphoreType.DMA((6 * nl,)), pltpu.SemaphoreType.DMA((6 * nl,))]
    res = pl.pallas_call(
        body, name=name, grid=(h // hb, nq), in_specs=ins, out_specs=outs, out_shape=oshape, scratch_shapes=scratch,
        compiler_params=pltpu.CompilerParams(dimension_semantics=("arbitrary", "arbitrary"),
                                             vmem_limit_bytes=VMEM_LIMIT, has_side_effects=bool(nl)))(*args)
    return res[0], res[1], list(res[2:])


def _causal_bwd_call(q, k, v, do, o, lse_r, ck_r, fox, scale, name, side=None):
    h, n, dk = q.shape
    dv = v.shape[2]
    t = _causal_tile(n)
    nq = n // t
    hb = HEADS_PER_STEP
    ns = 0 if side is None else len(side)

    def body(*refs):
        it = iter(refs)
        q_ref, k_ref, v_ref, do_ref, o_ref, lse_ref = (next(it) for _ in range(6))
        ck_ref = next(it) if fox else None
        side_in = [next(it) for _ in range(ns)]
        dq_ref, dk_ref, dv_ref = next(it), next(it), next(it)
        dck_ref, dcq_ref = (next(it), next(it)) if fox else (None, None)
        side_out = [next(it) for _ in range(ns)]
        delta_scr, dk_scr, dv_scr = next(it), next(it), next(it)
        dck_scr = next(it) if fox else None
        kj = pl.program_id(1)
        if ns:
            send_sems, recv_sems = next(it), next(it)
            x, y, core = _me()
            chips = [(1 - x, y), (x, 1 - y), (1 - x, 1 - y)]
            copies = [_remote(side_in[p].at[2 * cx + cy], side_out[p].at[r], send_sems, recv_sems, 3 * p + r,
                              (cx, cy, core)) for p in range(ns) for r, (cx, cy) in enumerate(chips)]

            @pl.when((pl.program_id(0) == 0) & (kj == 0))
            def _():
                for cp in copies:
                    cp.start()

            @pl.when((pl.program_id(0) == h // hb - 1) & (kj == nq - 1))
            def _():
                for cp in copies:
                    cp.wait()

        @pl.when(kj == 0)
        def _():
            dq_ref[...] = jnp.zeros_like(dq_ref)
            if fox:
                dcq_ref[...] = jnp.zeros_like(dcq_ref)
            ones = jnp.ones((SUBLANES, dv), F32)

            def fill(qi, carry):
                off = pl.multiple_of(qi * t, t)
                for e in range(hb):
                    prod = do_ref[e, pl.ds(off, t), :] * o_ref[e, pl.ds(off, t), :]
                    delta_scr[e, qi] = lax.dot_general(ones, prod, NT_DIMS, preferred_element_type=F32,
                                                       precision=lax.Precision.HIGHEST)[0:1, :]
                return carry

            lax.fori_loop(0, nq, fill, 0)

        kbs = [k_ref[e].astype(BF16) for e in range(hb)]
        vbs = [v_ref[e].astype(BF16) for e in range(hb)]
        dk_scr[...] = jnp.zeros_like(dk_scr)
        dv_scr[...] = jnp.zeros_like(dv_scr)
        if fox:
            dck_scr[...] = jnp.zeros_like(dck_scr)
            ckcs = [jnp.tile(jnp.transpose(jnp.broadcast_to(ck_ref[e, 0], (LANES, t))), (1, t // LANES))
                    for e in range(hb)]

        def process(qi, masked):
            off = pl.multiple_of(qi * t, t)
            if masked:
                krows = lax.broadcasted_iota(jnp.int32, (t, t), 0)
                qcols = lax.broadcasted_iota(jnp.int32, (t, t), 1)
                valid = krows <= qcols
            for e in range(hb):
                qb = q_ref[e, pl.ds(off, t), :].astype(BF16)
                dob = do_ref[e, pl.ds(off, t), :].astype(BF16)
                st = lax.dot_general(kbs[e], qb, NT_DIMS, preferred_element_type=F32) * scale
                if fox:
                    st = st - ckcs[e]
                pt = jnp.exp(st - lse_ref[e, qi])
                if masked:
                    pt = jnp.where(valid, pt, 0.0)
                dv_scr[e] += jnp.dot(pt.astype(BF16), dob, preferred_element_type=F32)
                dpt = lax.dot_general(vbs[e], dob, NT_DIMS, preferred_element_type=F32)
                dst = pt * (dpt - delta_scr[e, qi])
                if fox:
                    dck_scr[e] -= jnp.sum(dst, axis=1, keepdims=True)
                    dcq_ref[e, qi] += jnp.sum(dst, axis=0, keepdims=True)
                dsb = (dst * scale).astype(BF16)
                dk_scr[e] += jnp.dot(dsb, qb, preferred_element_type=F32)
                dq_ref[e, pl.ds(off, t), :] += lax.dot_general(dsb, kbs[e], TN_DIMS, preferred_element_type=F32)

        def step(qi, carry):
            process(qi, False)
            return carry

        process(kj, True)
        lax.fori_loop(kj + 1, nq, step, 0)
        dk_ref[...] = dk_scr[...]
        dv_ref[...] = dv_scr[...]
        if fox:
            for e in range(hb):
                dck_ref[e, 0] = jnp.transpose(jnp.broadcast_to(dck_scr[e], (t, LANES)))[0:1, :]

    whole = lambda a, b: (a, 0, 0)
    tile = lambda a, b: (a, b, 0)
    rowv = lambda a, b: (a, 0, 0, 0)
    rowt = lambda a, b: (a, b, 0, 0)
    ins = [BS((hb, n, dk), whole), BS((hb, t, dk), tile), BS((hb, t, dv), tile), BS((hb, n, dv), whole),
           BS((hb, n, dv), whole), BS((hb, nq, 1, t), rowv)]
    args = [q, k, v, do, o, lse_r]
    outs = [BS((hb, n, dk), whole), BS((hb, t, dk), tile), BS((hb, t, dv), tile)]
    oshape = [SDS((h, n, dk), F32), SDS((h, n, dk), F32), SDS((h, n, dv), F32)]
    scratch = [pltpu.VMEM((hb, nq, 1, t), F32), pltpu.VMEM((hb, t, dk), F32), pltpu.VMEM((hb, t, dv), F32)]
    if fox:
        ins.append(BS((hb, 1, 1, t), rowt))
        args.append(ck_r)
        outs += [BS((hb, 1, 1, t), rowt), BS((hb, nq, 1, t), rowv)]
        oshape += [SDS((h, nq, 1, t), F32), SDS((h, nq, 1, t), F32)]
        scratch.append(pltpu.VMEM((hb, t, 1), F32))
    if ns:
        ins += [ANY] * ns
        args += list(side)
        outs += [ANY] * ns
        oshape += [SDS((3,) + s.shape[1:], s.dtype) for s in side]
        scratch += [pltpu.SemaphoreType.DMA((3 * ns,)), pltpu.SemaphoreType.DMA((3 * ns,))]
    return pl.pallas_call(
        body, name=name, grid=(h // hb, nq), in_specs=ins, out_specs=outs, out_shape=oshape, scratch_shapes=scratch,
        compiler_params=pltpu.CompilerParams(dimension_semantics=("arbitrary", "arbitrary"),
                                             vmem_limit_bytes=VMEM_LIMIT, has_side_effects=bool(ns)))(*args)


def causal_attention(q, k, v, c, scale, name, late=None, late_layer=1, reduce=None, sinks=None):
    h, n, _ = q.shape
    t = _causal_tile(n)
    nq = n // t
    fox = c is not None

    def run_fwd(q, k, v, c, late, sinks):
        ck_r = c.reshape(h, nq, 1, t) if fox else None
        o, lse, got = _causal_fwd_call(q, k, v, ck_r, fox, scale, name + "_f", late, late_layer)
        out = (o,)
        if late is not None:
            out += (got,)
        if reduce is not None:
            out += ([jnp.zeros((N_CHIPS,) + SHARD_SHAPE[w][1:], F32) for w in reduce],)
        return (out if len(out) > 1 else o), (q, k, v, c, late, o, lse)

    def run_bwd(res, ct):
        q, k, v, c, late, o, lse = res
        ck_r = c.reshape(h, nq, 1, t) if fox else None
        dlate = None if late is None else [jnp.zeros_like(a) for a in late]
        if reduce is None:
            do = ct if late is None else ct[0]
            outs = _causal_bwd_call(q, k, v, do, o, lse, ck_r, fox, scale, name + "_b")
            return outs[0], outs[1], outs[2], ((outs[3] + outs[4]).reshape(h, n) if fox else None), dlate, None
        do, g1 = ct[0], ct[-1]
        xi, yi, ci = _me()
        c_idx = ci.astype(jnp.int32).reshape(1)
        j_idx = (2 * xi + yi).astype(jnp.int32).reshape(1)
        gs = [g.reshape(N_CHIPS, 2, HALVED[w][1] // 2, HALVED[w][2]) for g, w in zip(g1, reduce)]
        r1 = _pair_exchange(gs, name + "_pair_exchange")
        s1 = [_pair_add(g, r, c_idx, name + "_pair_add_" + w) for g, r, w in zip(gs, r1, reduce)]
        outs = _causal_bwd_call(q, k, v, do, o, lse, ck_r, fox, scale, name + "_b", [s[1] for s in s1])
        gh = [_chip_add(s[0], r, j_idx, name + "_chip_add_" + w) for s, r, w in zip(s1, outs[-len(reduce):], reduce)]
        go = _half_exchange(gh, name + "_half_exchange")
        full = [jnp.where(ci == 0, jnp.stack([a, b]), jnp.stack([b, a])) for a, b in zip(gh, go)]
        return outs[0], outs[1], outs[2], ((outs[3] + outs[4]).reshape(h, n) if fox else None), dlate, full

    return _op(run_fwd, run_bwd)(q, k, v, c, late, sinks)


SWA_T = 128


def _swa_masks(qi):
    t = SWA_T
    r = lax.broadcasted_iota(jnp.int32, (t, 3 * t), 0)
    c = lax.broadcasted_iota(jnp.int32, (t, 3 * t), 1)
    seg0 = c < t
    seg1 = (c >= t) & (c < 2 * t)
    jp = jnp.maximum(qi - 1, 0)
    kpos = jnp.where(seg0, c, jnp.where(seg1, jp * t + c - t, qi * t + c - 2 * t))
    dist = qi * t + r - kpos
    band = (dist >= 0) & ((dist < WINDOW) | (kpos < N_META))
    valid = (seg0 & (kpos < N_META) & (qi >= 2)) | (jnp.logical_not(seg0) & band & (jnp.logical_not(seg1) | (qi >= 1)))
    return valid, dist.astype(F32)


def _swa_cat(ref, qi):
    t = SWA_T
    jp = jnp.maximum(qi - 1, 0)
    return jnp.concatenate([ref[0, 0:t, :], ref[0, pl.ds(pl.multiple_of(jp * t, t), t), :],
                            ref[0, pl.ds(pl.multiple_of(qi * t, t), t), :]], axis=0).astype(BF16)


def _swa_fwd_call(q, k, v, sinks, slopes, scale, name):
    hq, n, d = q.shape
    hkv = k.shape[0]
    g = hq // hkv
    t = SWA_T
    nq = n // t

    def body(q_ref, k_ref, v_ref, sink_ref, slope_ref, o_ref, lse_ref):
        grp = pl.program_id(0)
        qi = pl.program_id(1)
        valid, dist = _swa_masks(qi)
        kc = _swa_cat(k_ref, qi)
        vc = _swa_cat(v_ref, qi)
        qs = jnp.concatenate([q_ref[e] for e in range(g)], axis=0).astype(BF16)
        s_all = lax.dot_general(qs, kc, NT_DIMS, preferred_element_type=F32) * scale
        ps, ls, ms = [], [], []
        for e in range(g):
            hh = grp * g + e
            s = jnp.where(valid, s_all[e * t:(e + 1) * t] - slope_ref[hh] * dist, NEG)
            m = jnp.maximum(jnp.max(s, axis=1, keepdims=True), sink_ref[hh])
            p = jnp.exp(s - m)
            ls.append(jnp.sum(p, axis=1, keepdims=True) + jnp.exp(sink_ref[hh] - m))
            ms.append(m)
            ps.append(p.astype(BF16))
        acc = jnp.dot(jnp.concatenate(ps, axis=0), vc, preferred_element_type=F32)
        for e in range(g):
            o_ref[e] = acc[e * t:(e + 1) * t] / ls[e]
            lse_ref[e] = ms[e] + jnp.log(ls[e])

    return pl.pallas_call(
        body, name=name, grid=(hkv, nq),
        in_specs=[BS((g, t, d), lambda a, b: (a, b, 0)), BS((1, n, d), lambda a, b: (a, 0, 0)),
                  BS((1, n, d), lambda a, b: (a, 0, 0)), BS(memory_space=pltpu.SMEM), BS(memory_space=pltpu.SMEM)],
        out_specs=[BS((g, t, d), lambda a, b: (a, b, 0)), BS((g, t, 1), lambda a, b: (a, b, 0))],
        out_shape=[SDS((hq, n, d), F32), SDS((hq, n, 1), F32)],
        compiler_params=_cparams(("parallel", "parallel")))(q, k, v, sinks, slopes)


def _swa_bwd_call(q, k, v, o, lse, do, sinks, slopes, scale, name):
    hq, n, d = q.shape
    hkv = k.shape[0]
    g = hq // hkv
    t = SWA_T
    nq = n // t

    def body(q_ref, k_ref, v_ref, o_ref, lse_ref, do_ref, sink_ref, slope_ref, dq_ref, dk_ref, dv_ref, ds_ref):
        grp = pl.program_id(0)
        qi = pl.program_id(1)

        @pl.when(qi == 0)
        def _():
            dk_ref[...] = jnp.zeros_like(dk_ref)
            dv_ref[...] = jnp.zeros_like(dv_ref)
            ds_ref[...] = jnp.zeros_like(ds_ref)

        valid, dist = _swa_masks(qi)
        kc = _swa_cat(k_ref, qi)
        vc = _swa_cat(v_ref, qi)
        qs = jnp.concatenate([q_ref[e] for e in range(g)], axis=0).astype(BF16)
        dos = jnp.concatenate([do_ref[e] for e in range(g)], axis=0).astype(BF16)
        s_all = lax.dot_general(qs, kc, NT_DIMS, preferred_element_type=F32) * scale
        dp_all = lax.dot_general(dos, vc, NT_DIMS, preferred_element_type=F32)
        ps, dss = [], []
        for e in range(g):
            hh = grp * g + e
            lse_e = lse_ref[e]
            delta = jnp.sum(do_ref[e] * o_ref[e], axis=1, keepdims=True)
            s = s_all[e * t:(e + 1) * t] - slope_ref[hh] * dist
            p = jnp.where(valid, jnp.exp(s - lse_e), 0.0)
            ds = p * (dp_all[e * t:(e + 1) * t] - delta)
            ps.append(p.astype(BF16))
            dss.append((ds * scale).astype(BF16))
            ds_ref[e] += -jnp.sum(jnp.exp(sink_ref[hh] - lse_e) * delta)
        p_st = jnp.concatenate(ps, axis=0)
        ds_st = jnp.concatenate(dss, axis=0)
        dq = jnp.dot(ds_st, kc, preferred_element_type=F32)
        for e in range(g):
            dq_ref[e] = dq[e * t:(e + 1) * t]
        dkc = lax.dot_general(ds_st, qs, TN_DIMS, preferred_element_type=F32)
        dvc = lax.dot_general(p_st, dos, TN_DIMS, preferred_element_type=F32)
        jp = jnp.maximum(qi - 1, 0)
        for seg, off in enumerate((0, pl.multiple_of(jp * t, t), pl.multiple_of(qi * t, t))):
            dk_ref[0, pl.ds(off, t), :] += dkc[seg * t:(seg + 1) * t]
            dv_ref[0, pl.ds(off, t), :] += dvc[seg * t:(seg + 1) * t]

    tile = lambda a, b: (a, b, 0)
    whole = lambda a, b: (a, 0, 0)
    return pl.pallas_call(
        body, name=name, grid=(hkv, nq),
        in_specs=[BS((g, t, d), tile), BS((1, n, d), whole), BS((1, n, d), whole), BS((g, t, d), tile),
                  BS((g, t, 1), tile), BS((g, t, d), tile), BS(memory_space=pltpu.SMEM), BS(memory_space=pltpu.SMEM)],
        out_specs=[BS((g, t, d), tile), BS((1, n, d), whole), BS((1, n, d), whole), BS((g, 1, LANES), whole)],
        out_shape=[SDS((hq, n, d), F32), SDS((hkv, n, d), F32), SDS((hkv, n, d), F32), SDS((hq, 1, LANES), F32)],
        compiler_params=_cparams(("arbitrary", "arbitrary")))(q, k, v, o, lse, do, sinks, slopes)


def window_attention(q, k, v, sinks, slopes, scale, name):
    def run_fwd(q, k, v, sinks, slopes):
        o, lse = _swa_fwd_call(q, k, v, sinks, slopes, scale, name + "_f")
        return o, (q, k, v, sinks, slopes, o, lse)

    def run_bwd(res, do):
        q, k, v, sinks, slopes, o, lse = res
        dq, dk, dv, ds = _swa_bwd_call(q, k, v, o, lse, do, sinks, slopes, scale, name + "_b")
        return dq, dk, dv, ds[:, 0, 0], jnp.zeros_like(slopes)

    return _op(run_fwd, run_bwd)(q, k, v, sinks, slopes)


def _sigmoid(x):
    return 1.0 / (1.0 + jnp.exp(-x))


def _merge_fwd_call(gs, ys, name):
    n, c = ys[0].shape
    tr = _rows_tile(n, c, 1 << 20)

    def body(g0, g1, g2, y0, y1, y2, m_ref):
        m_ref[...] = (_sigmoid(g0[...]) * y0[...] + _sigmoid(g1[...]) * y1[...]) + _sigmoid(g2[...]) * y2[...]

    spec = BS((tr, c), lambda i: (i, 0))
    return pl.pallas_call(
        body, name=name, grid=(n // tr,), in_specs=[spec] * 6, out_specs=spec, out_shape=SDS((n, c), F32),
        compiler_params=_cparams(("parallel",)))(*gs, *ys)


def _merge_bwd_call(gs, ys, dm, name):
    n, c = ys[0].shape
    tr = _rows_tile(n, c, 1 << 20)

    def body(g0, g1, g2, y0, y1, y2, dm_ref, dg0, dg1, dg2, dy0, dy1, dy2):
        d = dm_ref[...]
        for g, y, dg, dy in ((g0, y0, dg0, dy0), (g1, y1, dg1, dy1), (g2, y2, dg2, dy2)):
            s = _sigmoid(g[...])
            dy[...] = d * s
            dg[...] = d * y[...] * (s * (1.0 - s))

    spec = BS((tr, c), lambda i: (i, 0))
    return pl.pallas_call(
        body, name=name, grid=(n // tr,), in_specs=[spec] * 7, out_specs=[spec] * 6,
        out_shape=[SDS((n, c), F32)] * 6, compiler_params=_cparams(("parallel",)))(*gs, *ys, dm)


def gated_merge(gs, ys, name):
    def fwd(gs, ys):
        return _merge_fwd_call(gs, ys, name + "_f"), (gs, ys)

    def bwd(res, dm):
        out = _merge_bwd_call(res[0], res[1], dm, name + "_b")
        return tuple(out[:3]), tuple(out[3:])

    return _op(fwd, bwd)(tuple(gs), tuple(ys))


CONV_TR = 264
CONV_TC = 1408


def _conv_tiles(n, f):
    tr = CONV_TR if n % CONV_TR == 0 else _div_tile(n, CONV_TR)
    tc = CONV_TC if f % CONV_TC == 0 else f
    return tr, tc


def _shift_down(cur, halo, first, tr):
    halo = jnp.where(first, 0.0, halo)
    row = lax.broadcasted_iota(jnp.int32, cur.shape, 0)
    h7, h6 = halo[7:8, :], halo[6:7, :]
    u1 = jnp.where(row == 0, h7, pltpu.roll(cur, 1, 0))
    u2 = jnp.where(row == 0, h6, jnp.where(row == 1, h7, pltpu.roll(cur, 2, 0)))
    return u1, u2


def _conv_lin(cur, u1, u2, w_ref, b_ref):
    return ((b_ref[...] + w_ref[0:1, :] * u2) + w_ref[1:2, :] * u1) + w_ref[2:3, :] * cur


def _conv_in_specs(tr, tc, nj):
    sub = tr // SUBLANES
    prev = lambda j, i: (jnp.maximum(i * sub - 1, 0), j)
    prev_v = lambda j, i: (jnp.maximum(i * sub - 1, 0), j + nj)
    return [BS((tr, tc), lambda j, i: (i, j)), BS((SUBLANES, tc), prev),
            BS((tr, tc), lambda j, i: (i, j + nj)), BS((SUBLANES, tc), prev_v),
            BS((3, tc), lambda j, i: (0, j)), BS((3, tc), lambda j, i: (0, j + nj)),
            BS((1, tc), lambda j, i: (0, j)), BS((1, tc), lambda j, i: (0, j + nj))]


def _conv_fwd_call(u, cw, cb, name):
    n, f2 = u.shape
    f = f2 // 2
    tr, tc = _conv_tiles(n, f)
    nj = f // tc

    def body(ug, ugh, uv, uvh, wg, wv, bg, bv, a_ref):
        first = pl.program_id(1) == 0
        g1, g2 = _shift_down(ug[...], ugh[...], first, tr)
        v1, v2 = _shift_down(uv[...], uvh[...], first, tr)
        cg = _conv_lin(ug[...], g1, g2, wg, bg)
        cv = _conv_lin(uv[...], v1, v2, wv, bv)
        a_ref[...] = cg * _sigmoid(cg) * cv

    return pl.pallas_call(
        body, name=name, grid=(nj, n // tr), in_specs=_conv_in_specs(tr, tc, nj),
        out_specs=BS((tr, tc), lambda j, i: (i, j)), out_shape=SDS((n, f), F32),
        compiler_params=_cparams(("parallel", "parallel")))(u, u, u, u, cw, cw, cb, cb)


def _conv_bwd_dc_call(u, cw, cb, da, name):
    n, f2 = u.shape
    f = f2 // 2
    tr, tc = _conv_tiles(n, f)
    nj = f // tc

    def body(ug, ugh, uv, uvh, wg, wv, bg, bv, da_ref, dc_ref, dw_ref, db_ref):
        first = pl.program_id(1) == 0
        g0, v0 = ug[...], uv[...]
        g1, g2 = _shift_down(g0, ugh[...], first, tr)
        v1, v2 = _shift_down(v0, uvh[...], first, tr)
        cg = _conv_lin(g0, g1, g2, wg, bg)
        cv = _conv_lin(v0, v1, v2, wv, bv)
        d = da_ref[...]
        s = _sigmoid(cg)
        dcg = d * cv * (s * (1.0 + cg * (1.0 - s)))
        dcv = d * (cg * s)
        dc_ref[0] = dcg
        dc_ref[1] = dcv

        @pl.when(first)
        def _():
            dw_ref[...] = jnp.zeros_like(dw_ref)
            db_ref[...] = jnp.zeros_like(db_ref)

        for p, dc, taps in ((0, dcg, (g2, g1, g0)), (1, dcv, (v2, v1, v0))):
            for t in range(3):
                dw_ref[p, t:t + 1, :] += jnp.sum(dc * taps[t], axis=0, keepdims=True)
            db_ref[p] += jnp.sum(dc, axis=0, keepdims=True)

    return pl.pallas_call(
        body, name=name, grid=(nj, n // tr),
        in_specs=_conv_in_specs(tr, tc, nj) + [BS((tr, tc), lambda j, i: (i, j))],
        out_specs=[BS((2, tr, tc), lambda j, i: (0, i, j)), BS((2, 3, tc), lambda j, i: (0, 0, j)),
                   BS((2, 1, tc), lambda j, i: (0, 0, j))],
        out_shape=[SDS((2, n, f), F32), SDS((2, 3, f), F32), SDS((2, 1, f), F32)],
        compiler_params=_cparams(("arbitrary", "arbitrary")))(u, u, u, u, cw, cw, cb, cb, da)


def _conv_bwd_du_call(dc, cw, name):
    _, n, f = dc.shape
    tr, tc = _conv_tiles(n, f)
    nj = f // tc
    ni = n // tr
    sub = tr // SUBLANES

    def body(c_ref, nx_ref, w_ref, du_ref):
        cur = c_ref[0]
        nxt = jnp.where(pl.program_id(2) == ni - 1, 0.0, nx_ref[0])
        row = lax.broadcasted_iota(jnp.int32, cur.shape, 0)
        n0, n1 = nxt[0:1, :], nxt[1:2, :]
        d1 = jnp.where(row == tr - 1, n0, pltpu.roll(cur, tr - 1, 0))
        d2 = jnp.where(row == tr - 1, n1, jnp.where(row == tr - 2, n0, pltpu.roll(cur, tr - 2, 0)))
        du_ref[...] = (w_ref[2:3, :] * cur + w_ref[1:2, :] * d1) + w_ref[0:1, :] * d2

    nxt_map = lambda p, j, i: (p, jnp.minimum((i + 1) * sub, n // SUBLANES - 1), j)
    return pl.pallas_call(
        body, name=name, grid=(2, nj, ni),
        in_specs=[BS((1, tr, tc), lambda p, j, i: (p, i, j)), BS((1, SUBLANES, tc), nxt_map),
                  BS((3, tc), lambda p, j, i: (0, p * nj + j))],
        out_specs=BS((tr, tc), lambda p, j, i: (i, p * nj + j)), out_shape=SDS((n, 2 * f), F32),
        compiler_params=_cparams(("parallel", "parallel", "parallel")))(dc, dc, cw)


def conv_glu(u, cw, cb, name):
    def fwd(u, cw, cb):
        return _conv_fwd_call(u, cw, cb, name + "_f"), (u, cw, cb)

    def bwd(res, da):
        u, cw, cb = res
        dc, dw, db = _conv_bwd_dc_call(u, cw, cb, da, name + "_bc")
        du = _conv_bwd_du_call(dc, cw, name + "_bu")
        return du, jnp.concatenate([dw[0], dw[1]], axis=-1), jnp.concatenate([db[0], db[1]], axis=-1)

    return _op(fwd, bwd)(u, cw, cb)


def _loss_call(y, t, n_real, name):
    n, c = y.shape
    tr = _rows_tile(n, c, 1 << 20)

    def body(y_ref, t_ref, dy_ref, l_ref):
        i = pl.program_id(0)
        row = i * tr + lax.broadcasted_iota(jnp.int32, (tr, c), 0)
        real = (row >= N_META) & (row < N_META + n_real)
        e = jnp.where(real, y_ref[...] - t_ref[...], 0.0)
        dy_ref[...] = e * (1.0 / c)

        @pl.when(i == 0)
        def _():
            l_ref[...] = jnp.zeros_like(l_ref)

        l_ref[...] += 0.5 * jnp.sum(jnp.sum(e * e, axis=-1, keepdims=True) * (1.0 / c), axis=0, keepdims=True)

    spec = BS((tr, c), lambda i: (i, 0))
    return pl.pallas_call(
        body, name=name, grid=(n // tr,), in_specs=[spec, spec],
        out_specs=[spec, BS((1, 1), lambda i: (0, 0))], out_shape=[SDS((n, c), F32), SDS((1, 1), F32)],
        compiler_params=_cparams(("arbitrary",)))(y, t)


def _to_heads(x, nh):
    n = x.shape[0]
    return x.reshape(n, nh, x.shape[1] // nh).transpose(1, 0, 2)


def _from_heads(x):
    h, n, d = x.shape
    return x.transpose(1, 0, 2).reshape(n, h * d)


def _head_norm(x, g, denom, name):
    h, n, d = x.shape
    return rms_norm(x.reshape(h * n, d), g, denom, name).reshape(h, n, d)


def _head_norm_rope(x, g, cos, sin, name):
    h, n, d = x.shape
    return rms_norm_rope(x.reshape(h * n, d), g, cos, sin, MLA_QK, name).reshape(h, n, d)


def _pad_in_cols(w):
    z = lambda k: jnp.zeros(w.shape[:-1] + (k,), w.dtype)
    return jnp.concatenate([w[..., :1544], z(120), w[..., 1544:1960], z(96), w[..., 1960:], z(128)], axis=-1)


def _pad_q_up(w):
    s = w.shape[:-1]
    w = w.reshape(s + (HEADS, MLA_QK))
    w = jnp.concatenate([w, jnp.zeros(s + (HEADS, LANES - MLA_QK), w.dtype)], axis=-1)
    return w.reshape(s + (HEADS * LANES,))


LAYERED = BIG + ("ffn_conv_w",)
BEFORE_MLA = ("w_in", "mla_w_q_up", "mla_w_kv_up")
AFTER_MLA = ("w_branch", "w_o", "ffn_w_up", "ffn_w_down")


def _assemble_layer(parts):
    out = {k: jnp.concatenate([v[i] for i in range(N_CHIPS)], axis=SHARD_AXIS[k] - 1) for k, v in parts.items()}
    if "w_in" in out:
        out["w_in"] = _pad_in_cols(out["w_in"])
    if "mla_w_q_up" in out:
        out["mla_w_q_up"] = _pad_q_up(out["mla_w_q_up"])
    return out


def _layer_parts(names, gathered):
    return {k: g.reshape((N_CHIPS,) + SHARD_SHAPE[k][1:]) for k, g in zip(names, gathered)}


def _rope_tables(n):
    half = MLA_ROPE // 2
    freqs = ROPE_THETA ** (-jnp.arange(half, dtype=F32) / half)
    ang = jnp.arange(n).astype(F32)[:, None] * freqs[None, :]
    cos, sin = jnp.cos(ang), jnp.sin(ang)
    one, zero = jnp.ones((n, MLA_NOPE), F32), jnp.zeros((n, MLA_NOPE), F32)
    tail1, tail0 = jnp.ones((n, LANES - MLA_QK), F32), jnp.zeros((n, LANES - MLA_QK), F32)
    return (jnp.concatenate([one, cos, cos, tail1], axis=1), jnp.concatenate([zero, sin, sin, tail0], axis=1))


def _pad_lanes(g, width):
    return jnp.concatenate([g, jnp.zeros((width - g.shape[0],), g.dtype)]).reshape(1, width)


PROJ_SEGMENTS = ((O_FQ, 512), (O_FK, 512), (O_FV, 512), (O_FF, HEADS), (O_CQ, MLA_Q_RANK), (O_CKV, MLA_KV_RANK),
                 (O_KR, MLA_ROPE), (O_SQ, 512), (O_SK, 128), (O_SV, 128), (O_G, D_MODEL), (O_G + D_MODEL, D_MODEL),
                 (O_G + 2 * D_MODEL, D_MODEL))


def _join_proj_call(parts, name):
    n = parts[0].shape[0]
    tr = _rows_tile(n, IN_PAD, 4 << 20)

    def body(*refs):
        o_ref = refs[-1]
        o_ref[...] = jnp.zeros_like(o_ref)
        for (s, w), r in zip(PROJ_SEGMENTS, refs[:-1]):
            o_ref[:, s:s + w] = r[...]

    return pl.pallas_call(
        body, name=name, grid=(n // tr,), in_specs=[BS((tr, w), lambda i: (i, 0)) for _, w in PROJ_SEGMENTS],
        out_specs=BS((tr, IN_PAD), lambda i: (i, 0)), out_shape=SDS((n, IN_PAD), F32),
        compiler_params=_cparams(("parallel",)))(*parts)


def _split_proj(proj, name):
    def fwd(x):
        return tuple(x[:, s:s + w] for s, w in PROJ_SEGMENTS), None

    def bwd(_, cts):
        return (_join_proj_call(cts, name + "_b"),)

    return _op(fwd, bwd)(proj)


def _trunk(eps, eps_cw, sinks, meta, small, x, w0, late):
    assert DEPTH == 2
    seq = x.shape[0]
    n = -(-(N_META + seq) // ROW_PAD) * ROW_PAD
    ew = _assemble_layer(eps)
    cos, sin = _rope_tables(n)
    slopes = jnp.exp2(-8.0 * jnp.arange(1, HEADS + 1, dtype=F32) / HEADS)
    h = jnp.concatenate([meta, x, jnp.zeros((n - N_META - seq, D_MODEL), F32)], axis=0)
    wb = w0
    for l in range(DEPTH):
        p = f"l{l}_"
        row = lambda name: small[name][l].reshape(1, -1)
        xn = rms_norm(h, row("norm1_g"), D_MODEL, p + "norm1")
        proj = linear(xn, wb["w_in"], ew["w_in"], p + "win")
        p_fq, p_fk, p_fv, p_ff, p_cq, p_ckv, p_kr, p_sq, p_sk, p_sv, g0, g1, g2 = _split_proj(proj, p + "split")
        fq = _head_norm(_to_heads(p_fq, HEADS), row("fox_q_g"), HEAD_DIM, p + "fqn")
        fk = _head_norm(_to_heads(p_fk, HEADS), row("fox_k_g"), HEAD_DIM, p + "fkn")
        fv = _to_heads(p_fv, HEADS)
        c = forget_cumsum(p_ff.T, small["fox_forget_b"][l].reshape(HEADS, 1), p + "fgate")
        if l == 0:
            rest = [a for k, a in zip(LAYERED, late) if k != "w_in"]
            out_a, got, carriers = causal_attention(fq, fk, fv, c, HEAD_DIM ** -0.5, p + "fox", rest, 0, BIG,
                                                    sinks[1])
            wb = dict(wb, **_assemble_layer(_layer_parts(LAYERED[1:], _fill_own(got, [a[0] for a in rest]))))
            ew1 = _assemble_layer(dict(zip(BIG, carriers)))
        else:
            out_a = causal_attention(fq, fk, fv, c, HEAD_DIM ** -0.5, p + "fox")
        cqn = rms_norm(p_cq, row("mla_q_a_g"), MLA_Q_RANK, p + "cqn")
        q = _to_heads(linear(cqn, wb["mla_w_q_up"], ew["mla_w_q_up"], p + "qup"), HEADS)
        q = _head_norm_rope(q, _pad_lanes(small["mla_q_g"][l], LANES), cos, sin, p + "mqn")
        ckvn = rms_norm(p_ckv, row("mla_kv_a_g"), MLA_KV_RANK, p + "ckvn")
        kv = _to_heads(linear(ckvn, wb["mla_w_kv_up"], ew["mla_w_kv_up"], p + "kvup"), HEADS)
        kr = jnp.broadcast_to(p_kr[None], (HEADS, n, MLA_ROPE))
        k = jnp.concatenate([kv[..., :MLA_NOPE], kr, jnp.zeros((HEADS, n, LANES - MLA_QK), F32)], axis=-1)
        k = _head_norm_rope(k, _pad_lanes(small["mla_k_g"][l], LANES), cos, sin, p + "mkn")
        if l == 0:
            out_b, got, carriers = causal_attention(q, k, kv[..., MLA_NOPE:], None, MLA_QK ** -0.5, p + "mla", late,
                                                    1, AFTER_MLA, sinks[0])
            w1 = _assemble_layer(_layer_parts(LAYERED, _fill_own(got, [a[1] for a in late])))
            ew = dict(ew, **_assemble_layer(dict(zip(AFTER_MLA, carriers))))
        else:
            out_b = causal_attention(q, k, kv[..., MLA_NOPE:], None, MLA_QK ** -0.5, p + "mla")
        sq = _head_norm(_to_heads(p_sq, HEADS), row("swa_q_g"), HEAD_DIM, p + "sqn")
        sk = _head_norm(_to_heads(p_sk, SWA_KV_HEADS), row("swa_k_g"), HEAD_DIM, p + "skn")
        sv = _to_heads(p_sv, SWA_KV_HEADS)
        out_c = window_attention(sq, sk, sv, small["swa_sinks"][l], slopes, HEAD_DIM ** -0.5, p + "swa")
        ys = [linear(_from_heads(o), wb["w_branch"][i], ew["w_branch"][i], p + f"br{i}")
              for i, o in enumerate((out_a, out_b, out_c))]
        merged = gated_merge([g0, g1, g2], ys, p + "merge")
        h = linear(merged, wb["w_o"], ew["w_o"], p + "wo", res=h)
        xn2 = rms_norm(h, row("norm2_g"), D_MODEL, p + "norm2")
        u = linear(xn2, wb["ffn_w_up"], ew["ffn_w_up"], p + "wup")
        act = conv_glu(u, lax.stop_gradient(wb["ffn_conv_w"]) + eps_cw[l], row("ffn_conv_b"), p + "conv")
        h = linear(act, wb["ffn_w_down"], ew["ffn_w_down"], p + "wdown", res=h)
        wb, ew = w1, ew1
    return h


def _local_step(x, target, w0, late, meta, small):
    seq = x.shape[0]
    eps = {k: jnp.zeros((N_CHIPS,) + SHARD_SHAPE[k][1:], F32) for k in BEFORE_MLA}
    eps_cw = jnp.zeros((DEPTH, 3, 2 * D_FF), F32)
    sinks = tuple([jnp.zeros((2, HALVED[k][1] // 2, HALVED[k][2]), F32) for k in names] for names in (AFTER_MLA, BIG))
    y, vjp = jax.vjp(lambda e, ec, sk, mt, s, xx: _trunk(e, ec, sk, mt, s, xx, w0, late),
                     eps, eps_cw, sinks, meta, small, x)
    n = y.shape[0]
    tpad = jnp.concatenate([jnp.zeros((N_META, D_MODEL), F32), target, jnp.zeros((n - N_META - seq, D_MODEL), F32)])
    dy, loss = _loss_call(y, tpad, seq, "loss")
    g_eps, g_cw, g_l1, g_meta, g_small, g_x = vjp(dy)
    return loss[0, 0], g_x, g_eps, g_l1, g_cw, g_meta, g_small


def _pack_rows(shapes, mult):
    total = sum(_size(s) for s in shapes)
    rows = -(-total // LANES)
    return -(-rows // mult) * mult


def _size(shape):
    n = 1
    for d in shape:
        n *= d
    return n


def _pack(arrs, rows, dtype):
    flat = [a.reshape(-1).astype(dtype) for a in arrs]
    used = sum(a.size for a in flat)
    flat.append(jnp.zeros((rows * LANES - used,), dtype))
    return jnp.concatenate(flat).reshape(rows, LANES)


def _unpack(p, shapes):
    flat = p.reshape(-1)
    out, off = [], 0
    for s in shapes:
        out.append(flat[off:off + _size(s)].reshape(s))
        off += _size(s)
    return out


MESH = pl.DeviceIdType.MESH
ANY = pl.BlockSpec(memory_space=pl.ANY)


def _me():
    return lax.axis_index("x"), lax.axis_index("y"), lax.axis_index("c")


def _remote(src, dst, send_sems, recv_sems, idx, dev):
    return pltpu.make_async_remote_copy(src_ref=src, dst_ref=dst, send_sem=send_sems.at[idx], recv_sem=recv_sems.at[idx],
                                        device_id=dev, device_id_type=MESH)


def _layer_gather(ins, outs, send_sems, recv_sems, layer, base):
    x, y, _ = _me()
    j = 2 * x + y
    sibling = (x, y, 1 - layer)
    chips = [(1 - x, y), (x, 1 - y), (1 - x, 1 - y)]

    def ici(p, r):
        cx, cy = chips[r]
        return _remote(ins[p].at[layer], outs[p].at[j], send_sems, recv_sems, base + 6 * p + r, (cx, cy, layer))

    def d2d(p, r):
        cx, cy = chips[r]
        blk = outs[p].at[2 * cx + cy]
        return _remote(blk, blk, send_sems, recv_sems, base + 6 * p + 3 + r, sibling)

    pairs = [(p, r) for p in range(len(ins)) for r in range(3)]

    def start():
        for p, r in pairs:
            ici(p, r).start()

    def forward():
        for p, r in pairs:
            ici(p, r).wait_recv()
            d2d(p, r).start()

    def drain():
        for p, r in pairs:
            ici(p, r).wait_send()
            d2d(p, r).wait_send()

    def receive():
        for p, r in pairs:
            d2d(p, r).wait_recv()

    return start, forward, drain, receive


def _fill_own(outs, own):
    j = 2 * lax.axis_index("x") + lax.axis_index("y")
    return [lax.dynamic_update_index_in_dim(o, a, j, 0) for o, a in zip(outs, own)]


def _gather_early(meta, arrs, name):
    npk = len(arrs)

    def body(*refs):
        m_in, ins = refs[0], refs[1:npk + 1]
        m_out, outs = refs[npk + 1], refs[npk + 2:2 * npk + 2]
        send_sems, recv_sems = refs[2 * npk + 2:]
        x, y, c = _me()
        j = 2 * x + y
        sibling = (x, y, 1 - c)
        chips = [(1 - x, y), (x, 1 - y), (1 - x, 1 - y)]
        start, forward, drain, receive = _layer_gather(ins, outs, send_sems, recv_sems, 0, 6)
        sends = []
        for r, (cx, cy) in enumerate(chips):
            cp = _remote(m_in.at[c], m_out.at[j, c], send_sems, recv_sems, r, (cx, cy, c))
            cp.start()
            sends.append(cp)
        pl.when(c == 0)(start)
        for r, (cx, cy) in enumerate(chips):
            blk = m_out.at[2 * cx + cy, c]
            _remote(blk, blk, send_sems, recv_sems, r, sibling).wait_recv()
            fw = _remote(blk, blk, send_sems, recv_sems, 3 + r, sibling)
            fw.start()
            sends.append(fw)
        for r, (cx, cy) in enumerate(chips):
            blk = m_out.at[2 * cx + cy, 1 - c]
            _remote(blk, blk, send_sems, recv_sems, 3 + r, sibling).wait_recv()
        for cp in sends:
            cp.wait_send()

        @pl.when(c == 0)
        def _():
            forward()
            drain()

        pl.when(c == 1)(receive)

    nsem = 6 + 6 * npk
    res = pl.pallas_call(
        body, name=name, in_specs=[ANY] * (npk + 1), out_specs=[ANY] * (npk + 1),
        out_shape=[SDS((N_CHIPS,) + meta.shape, meta.dtype)] + [SDS((N_CHIPS,) + a.shape[1:], a.dtype) for a in arrs],
        scratch_shapes=[pltpu.SemaphoreType.DMA((nsem,)), pltpu.SemaphoreType.DMA((nsem,))],
        compiler_params=pltpu.CompilerParams(has_side_effects=True))(meta, *arrs)
    return _fill_own(res[:1], [meta])[0], _fill_own(res[1:], [a[0] for a in arrs])


def _pair_exchange(gs, name):
    npk = len(gs)

    def body(*refs):
        ins, outs = refs[:npk], refs[npk:2 * npk]
        send_sems, recv_sems = refs[2 * npk:]
        x, y, c = _me()
        cps = [_remote(ins[p].at[:, 1 - c], outs[p], send_sems, recv_sems, p, (x, y, 1 - c)) for p in range(npk)]
        for cp in cps:
            cp.start()
        for cp in cps:
            cp.wait()

    return pl.pallas_call(
        body, name=name, in_specs=[ANY] * npk, out_specs=[ANY] * npk,
        out_shape=[SDS((N_CHIPS,) + g.shape[2:], g.dtype) for g in gs],
        scratch_shapes=[pltpu.SemaphoreType.DMA((npk,)), pltpu.SemaphoreType.DMA((npk,))],
        compiler_params=pltpu.CompilerParams(has_side_effects=True))(*gs)


def _chip_exchange(ss, small, name):
    npk = len(ss)

    def body(*refs):
        ins, sm_ref = refs[:npk], refs[npk]
        outs, sa_ref = refs[npk + 1:2 * npk + 1], refs[2 * npk + 1]
        send_sems, recv_sems, loc_sem = refs[2 * npk + 2:]
        x, y, c = _me()
        me = 4 * x + 2 * y + c
        lc = pltpu.make_async_copy(sm_ref, sa_ref.at[me], loc_sem.at[0])
        lc.start()
        cps = []
        for p in range(npk):
            for r, (cx, cy) in enumerate([(1 - x, y), (x, 1 - y), (1 - x, 1 - y)]):
                cp = _remote(ins[p].at[2 * cx + cy], outs[p].at[r], send_sems, recv_sems, 3 * p + r, (cx, cy, c))
                cp.start()
                cps.append(cp)
        base = 3 * npk - 1
        for mask in range(1, N_DEV):
            px, py, pc = x ^ (mask >> 2), y ^ ((mask >> 1) & 1), c ^ (mask & 1)
            cp = _remote(sm_ref, sa_ref.at[me], send_sems, recv_sems, base + mask, (px, py, pc))
            cp.start()
            cps.append(cp)
        for p in range(npk):
            for r in range(3):
                _remote(outs[p].at[r], outs[p].at[r], send_sems, recv_sems, 3 * p + r, (x, y, c)).wait_recv()
        for mask in range(1, N_DEV):
            src = 4 * (x ^ (mask >> 2)) + 2 * (y ^ ((mask >> 1) & 1)) + (c ^ (mask & 1))
            _remote(sa_ref.at[src], sa_ref.at[src], send_sems, recv_sems, base + mask, (x, y, c)).wait_recv()
        for cp in cps:
            cp.wait_send()
        lc.wait()

    nsem = 3 * npk + N_DEV - 1
    res = pl.pallas_call(
        body, name=name, in_specs=[ANY] * (npk + 1), out_specs=[ANY] * (npk + 1),
        out_shape=[SDS((3,) + s.shape[1:], s.dtype) for s in ss] + [SDS((N_DEV,) + small.shape, small.dtype)],
        scratch_shapes=[pltpu.SemaphoreType.DMA((nsem,)), pltpu.SemaphoreType.DMA((nsem,)),
                        pltpu.SemaphoreType.DMA((1,))],
        compiler_params=pltpu.CompilerParams(has_side_effects=True))(*ss, small)
    return res[:npk], res[npk]


def _half_exchange(ghs, name):
    npk = len(ghs)

    def body(*refs):
        ins, outs = refs[:npk], refs[npk:2 * npk]
        send_sems, recv_sems = refs[2 * npk:]
        x, y, c = _me()
        cps = [_remote(ins[p], outs[p], send_sems, recv_sems, p, (x, y, 1 - c)) for p in range(npk)]
        for cp in cps:
            cp.start()
        for cp in cps:
            cp.wait()

    return pl.pallas_call(
        body, name=name, in_specs=[ANY] * npk, out_specs=[ANY] * npk, out_shape=[SDS(g.shape, g.dtype) for g in ghs],
        scratch_shapes=[pltpu.SemaphoreType.DMA((npk,)), pltpu.SemaphoreType.DMA((npk,))],
        compiler_params=pltpu.CompilerParams(has_side_effects=True))(*ghs)


def _add_tile(rows, cols):
    return _div_tile(rows, max(16, (1 << 19) // max(cols, LANES)), 16)


def _pair_add(g, r1, c_idx, name):
    _, rows, cols = r1.shape
    tr = _add_tile(rows, cols)

    def body(c_ref, g_ref, r_ref, o_ref, ob_ref):
        s = g_ref[0] + r_ref[...]
        o_ref[...] = s
        ob_ref[...] = s.astype(BF16)

    own = BS((1, tr, cols), lambda k, i, c: (k, i, 0))
    return pl.pallas_call(
        body, name=name,
        grid_spec=pltpu.PrefetchScalarGridSpec(
            num_scalar_prefetch=1, grid=(N_CHIPS, rows // tr),
            in_specs=[BS((1, 1, tr, cols), lambda k, i, c: (k, c[0], i, 0)), own], out_specs=[own, own]),
        out_shape=[SDS(r1.shape, F32), SDS(r1.shape, BF16)],
        compiler_params=_cparams(("parallel", "parallel")))(c_idx, g, r1)


def _chip_add(s1, r2, j_idx, name):
    _, rows, cols = s1.shape
    tr = _add_tile(rows, cols)

    def body(j_ref, s_ref, r_ref, o_ref):
        o_ref[...] = ((s_ref[0] + r_ref[0].astype(F32)) + r_ref[1].astype(F32)) + r_ref[2].astype(F32)

    return pl.pallas_call(
        body, name=name,
        grid_spec=pltpu.PrefetchScalarGridSpec(
            num_scalar_prefetch=1, grid=(rows // tr,),
            in_specs=[BS((1, tr, cols), lambda i, j: (j[0], i, 0)), BS((3, tr, cols), lambda i, j: (0, i, 0))],
            out_specs=BS((tr, cols), lambda i, j: (i, 0))),
        out_shape=SDS((rows, cols), F32), compiler_params=_cparams(("parallel",)))(j_idx, s1, r2)


def _adamw_math(w, g, m, v):
    m = ADAM_B1 * m + (1.0 - ADAM_B1) * g
    v = ADAM_B2 * v + (1.0 - ADAM_B2) * (g * g)
    m_hat = m / (1.0 - ADAM_B1 ** ADAM_STEP)
    v_hat = v / (1.0 - ADAM_B2 ** ADAM_STEP)
    delta = -ADAM_LR * (m_hat / (jnp.sqrt(v_hat) + ADAM_EPS) + ADAM_WD * w)
    return delta, m, v


def _adamw(w, gh, go, m, v, c_idx, name):
    _, rows, cols = w.shape
    tr = _add_tile(rows, cols)

    def body(c_ref, w_ref, gh_ref, go_ref, m_ref, v_ref, g_out, d_out, m_out, v_out):
        g = jnp.where(pl.program_id(0) == c_ref[0], gh_ref[...], go_ref[...])
        g_out[0] = g
        d_out[0], m_out[0], v_out[0] = _adamw_math(w_ref[0], g, m_ref[0], v_ref[0])

    full = BS((1, tr, cols), lambda hf, i, c: (hf, i, 0))
    half = BS((tr, cols), lambda hf, i, c: (i, 0))
    return pl.pallas_call(
        body, name=name,
        grid_spec=pltpu.PrefetchScalarGridSpec(
            num_scalar_prefetch=1, grid=(2, rows // tr), in_specs=[full, half, half, full, full],
            out_specs=[full] * 4),
        out_shape=[SDS(w.shape, F32)] * 4, compiler_params=_cparams(("parallel", "parallel")))(c_idx, w, gh, go, m, v)


def _adamw_layers(w, gh0, go0, g1, m, v, c_idx, name):
    _, _, rows, cols = w.shape
    tr = _add_tile(rows, cols)

    def body(c_ref, w_ref, gh_ref, go_ref, g1_ref, m_ref, v_ref, g_out, d_out, m_out, v_out):
        g0 = jnp.where(pl.program_id(1) == c_ref[0], gh_ref[...], go_ref[...])
        g = jnp.where(pl.program_id(0) == 0, g0, g1_ref[0])
        g_out[0, 0] = g
        d_out[0, 0], m_out[0, 0], v_out[0, 0] = _adamw_math(w_ref[0, 0], g, m_ref[0, 0], v_ref[0, 0])

    full = BS((1, 1, tr, cols), lambda l, hf, i, c: (l, hf, i, 0))
    half = BS((tr, cols), lambda l, hf, i, c: (i, 0))
    return pl.pallas_call(
        body, name=name,
        grid_spec=pltpu.PrefetchScalarGridSpec(
            num_scalar_prefetch=1, grid=(2, 2, rows // tr),
            in_specs=[full, half, half, BS((1, tr, cols), lambda l, hf, i, c: (hf, i, 0)), full, full],
            out_specs=[full] * 4),
        out_shape=[SDS(w.shape, F32)] * 4,
        compiler_params=_cparams(("parallel", "parallel", "parallel")))(c_idx, w, gh0, go0, g1, m, v)


def _adamw_two(w, g0, g1, m, v, name):
    _, _, rows, cols = w.shape
    tr = _add_tile(rows, cols)

    def body(w_ref, g0_ref, g1_ref, m_ref, v_ref, g_out, d_out, m_out, v_out):
        g = jnp.where(pl.program_id(0) == 0, g0_ref[0], g1_ref[0])
        g_out[0, 0] = g
        d_out[0, 0], m_out[0, 0], v_out[0, 0] = _adamw_math(w_ref[0, 0], g, m_ref[0, 0], v_ref[0, 0])

    full = BS((1, 1, tr, cols), lambda l, hf, i: (l, hf, i, 0))
    half = BS((1, tr, cols), lambda l, hf, i: (hf, i, 0))
    return pl.pallas_call(
        body, name=name, grid=(2, 2, rows // tr), in_specs=[full, half, half, full, full], out_specs=[full] * 4,
        out_shape=[SDS(w.shape, F32)] * 4,
        compiler_params=_cparams(("parallel", "parallel", "parallel")))(w, g0, g1, m, v)


def _sum_devices(sa, name):
    def body(sa_ref, g_out):
        g = sa_ref[0]
        for d in range(1, N_DEV):
            g = g + sa_ref[d]
        g_out[...] = g

    return pl.pallas_call(body, name=name, out_shape=SDS(sa.shape[1:], F32),
                          compiler_params=pltpu.CompilerParams(vmem_limit_bytes=VMEM_LIMIT))(sa)


def _adamw_small(ws, gs, ms, vs, name):
    k = len(ws)

    def body(*refs):
        ins, outs = refs[:4 * k], refs[4 * k:]
        for i in range(k):
            d, m, v = _adamw_math(ins[i][...], ins[k + i][...], ins[2 * k + i][...], ins[3 * k + i][...])
            outs[i][...], outs[k + i][...], outs[2 * k + i][...] = d, m, v

    return pl.pallas_call(body, name=name, out_shape=[SDS(w.shape, F32) for w in ws] * 3,
                          compiler_params=pltpu.CompilerParams(vmem_limit_bytes=VMEM_LIMIT))(*ws, *gs, *ms, *vs)


HALVED = {"w_in": (2, 1024, 1450), "mla_w_q_up": (2, 256, 192), "mla_w_kv_up": (2, 128, 256),
          "w_branch": (2, 1536, 256), "w_o": (2, 256, 1024), "ffn_w_up": (2, 1024, 1408),
          "ffn_w_down": (2, 704, 1024), "ffn_conv_w": (2, 3, 1408), "meta_tokens": (2, 8, 256)}
SMALL_SHAPE = {"norm1_g": (2, 1024), "fox_forget_b": (2, 8), "fox_q_g": (2, 64), "fox_k_g": (2, 64),
               "mla_q_a_g": (2, 256), "mla_kv_a_g": (2, 128), "mla_q_g": (2, 96), "mla_k_g": (2, 96),
               "swa_q_g": (2, 64), "swa_k_g": (2, 64), "swa_sinks": (2, 8), "norm2_g": (2, 1024),
               "ffn_conv_b": (2, 5632)}
SMALL_ROWS = _pack_rows([SMALL_SHAPE[k] for k in SMALL] + [(1,)], SUBLANES)


def kernel(x, meta_tokens, norm1_g, w_in, fox_forget_b, fox_q_g, fox_k_g, mla_q_a_g, mla_w_q_up, mla_kv_a_g, mla_w_kv_up, mla_q_g, mla_k_g, swa_q_g, swa_k_g, swa_sinks, w_branch, w_o, norm2_g, ffn_w_up, ffn_conv_w, ffn_conv_b, ffn_w_down, loss_target, m_meta_tokens, m_norm1_g, m_w_in, m_fox_forget_b, m_fox_q_g, m_fox_k_g, m_mla_q_a_g, m_mla_w_q_up, m_mla_kv_a_g, m_mla_w_kv_up, m_mla_q_g, m_mla_k_g, m_swa_q_g, m_swa_k_g, m_swa_sinks, m_w_branch, m_w_o, m_norm2_g, m_ffn_w_up, m_ffn_conv_w, m_ffn_conv_b, m_ffn_w_down, v_meta_tokens, v_norm1_g, v_w_in, v_fox_forget_b, v_fox_q_g, v_fox_k_g, v_mla_q_a_g, v_mla_w_q_up, v_mla_kv_a_g, v_mla_w_kv_up, v_mla_q_g, v_mla_k_g, v_swa_q_g, v_swa_k_g, v_swa_sinks, v_w_branch, v_w_o, v_norm2_g, v_ffn_w_up, v_ffn_conv_w, v_ffn_conv_b, v_ffn_w_down):
    w = dict(meta_tokens=meta_tokens, norm1_g=norm1_g, w_in=w_in, fox_forget_b=fox_forget_b, fox_q_g=fox_q_g,
             fox_k_g=fox_k_g, mla_q_a_g=mla_q_a_g, mla_w_q_up=mla_w_q_up, mla_kv_a_g=mla_kv_a_g,
             mla_w_kv_up=mla_w_kv_up, mla_q_g=mla_q_g, mla_k_g=mla_k_g, swa_q_g=swa_q_g, swa_k_g=swa_k_g,
             swa_sinks=swa_sinks, w_branch=w_branch, w_o=w_o, norm2_g=norm2_g, ffn_w_up=ffn_w_up,
             ffn_conv_w=ffn_conv_w, ffn_conv_b=ffn_conv_b, ffn_w_down=ffn_w_down)
    m = dict(meta_tokens=m_meta_tokens, norm1_g=m_norm1_g, w_in=m_w_in, fox_forget_b=m_fox_forget_b,
             fox_q_g=m_fox_q_g, fox_k_g=m_fox_k_g, mla_q_a_g=m_mla_q_a_g, mla_w_q_up=m_mla_w_q_up,
             mla_kv_a_g=m_mla_kv_a_g, mla_w_kv_up=m_mla_w_kv_up, mla_q_g=m_mla_q_g, mla_k_g=m_mla_k_g,
             swa_q_g=m_swa_q_g, swa_k_g=m_swa_k_g, swa_sinks=m_swa_sinks, w_branch=m_w_branch, w_o=m_w_o,
             norm2_g=m_norm2_g, ffn_w_up=m_ffn_w_up, ffn_conv_w=m_ffn_conv_w, ffn_conv_b=m_ffn_conv_b,
             ffn_w_down=m_ffn_w_down)
    v = dict(meta_tokens=v_meta_tokens, norm1_g=v_norm1_g, w_in=v_w_in, fox_forget_b=v_fox_forget_b,
             fox_q_g=v_fox_q_g, fox_k_g=v_fox_k_g, mla_q_a_g=v_mla_q_a_g, mla_w_q_up=v_mla_w_q_up,
             mla_kv_a_g=v_mla_kv_a_g, mla_w_kv_up=v_mla_w_kv_up, mla_q_g=v_mla_q_g, mla_k_g=v_mla_k_g,
             swa_q_g=v_swa_q_g, swa_k_g=v_swa_k_g, swa_sinks=v_swa_sinks, w_branch=v_w_branch, w_o=v_w_o,
             norm2_g=v_norm2_g, ffn_w_up=v_ffn_w_up, ffn_conv_w=v_ffn_conv_w, ffn_conv_b=v_ffn_conv_b,
             ffn_w_down=v_ffn_w_down)
    xi, yi, ci = _me()
    c_idx = ci.astype(jnp.int32).reshape(1)
    j_idx = (2 * xi + yi).astype(jnp.int32).reshape(1)

    sh_names = BIG + FINE
    local = {k: (w[k].astype(BF16) if k in BIG else w[k]).reshape(HALVED[k]) for k in sh_names}
    late = [local[k] for k in LAYERED]
    meta_g, early = _gather_early(local["meta_tokens"], [local["w_in"]], "gather_early")
    meta = jnp.concatenate([meta_g[i].reshape(SHARD_SHAPE["meta_tokens"]) for i in range(N_CHIPS)], axis=1)
    w0 = _assemble_layer(_layer_parts(("w_in",), early))
    small = {k: w[k] for k in SMALL}

    loss, g_x, g_eps, g_red, g_cw, g_meta, g_small = _local_step(x[0], loss_target[0], w0, late, meta, small)
    g_after0, g_l1 = dict(zip(AFTER_MLA, g_red[0])), dict(zip(BIG, g_red[1]))

    quarter = {k: (2, HALVED[k][1] // 2, HALVED[k][2]) for k in BIG}
    last = BEFORE_MLA + FINE
    gs = [g_eps[k].reshape((N_CHIPS,) + quarter[k]) for k in BEFORE_MLA]
    for k, g in (("meta_tokens", g_meta), ("ffn_conv_w", g_cw)):
        gs.append(jnp.stack(jnp.split(g, N_CHIPS, axis=SHARD_AXIS[k])).reshape((N_CHIPS,) + HALVED[k]))
    spack = _pack([g_small[k] for k in SMALL] + [loss.reshape(1)], SMALL_ROWS, F32)
    r1 = _pair_exchange(gs, "grads_pair_exchange")
    s1 = [_pair_add(g, r, c_idx, "grads_pair_add_" + k) for g, r, k in zip(gs, r1, last)]
    r2, sa = _chip_exchange([s[1] for s in s1], spack, "grads_chip_exchange")
    gh = dict(zip(last, [_chip_add(s[0], r, j_idx, "grads_chip_add_" + k) for s, r, k in zip(s1, r2, last)]))
    go = dict(zip(last, _half_exchange([gh[k] for k in last], "grads_half_exchange")))

    grads, deltas, new_m, new_v = {}, {}, {}, {}
    for k in sh_names:
        if k in BIG:
            shp = (2,) + quarter[k]
            wk, mk, vk = w[k].reshape(shp), m[k].reshape(shp), v[k].reshape(shp)
            if k in BEFORE_MLA:
                outs = _adamw_layers(wk, gh[k], go[k], g_l1[k], mk, vk, c_idx, "adamw_" + k)
            else:
                outs = _adamw_two(wk, g_after0[k], g_l1[k], mk, vk, "adamw_" + k)
        else:
            outs = _adamw(w[k].reshape(HALVED[k]), gh[k], go[k], m[k].reshape(HALVED[k]), v[k].reshape(HALVED[k]),
                          c_idx, "adamw_" + k)
        for dst, o in zip((grads, deltas, new_m, new_v), outs):
            dst[k] = o.reshape(SHARD_SHAPE[k])
    sm_shapes = [SMALL_SHAPE[k] for k in SMALL] + [(1,)]
    g_sum = _unpack(_sum_devices(sa, "sum_small"), sm_shapes)
    res = _adamw_small([w[k] for k in SMALL], g_sum[:-1], [m[k] for k in SMALL], [v[k] for k in SMALL], "adamw_small")
    ns = len(SMALL)
    grads.update(zip(SMALL, g_sum[:-1]))
    for dst, vals in zip((deltas, new_m, new_v), (res[:ns], res[ns:2 * ns], res[2 * ns:])):
        dst.update(zip(SMALL, vals))
    total_loss = g_sum[-1][0]
    return (total_loss, g_x[None], *[grads[k] for k in WEIGHTS], *[deltas[k] for k in WEIGHTS],
            *[new_m[k] for k in WEIGHTS], *[new_v[k] for k in WEIGHTS])
```

```python
import jax
import jax.numpy as jnp
from jax import lax
from jax.experimental import pallas as pl
from jax.experimental.pallas import tpu as pltpu

F32 = jnp.float32
BF16 = jnp.bfloat16
SDS = jax.ShapeDtypeStruct
BS = pl.BlockSpec

D_MODEL = 1024
DEPTH = 2
N_META = 16
EPS = 1e-6
HEADS = 8
HEAD_DIM = 64
MLA_Q_RANK = 256
MLA_KV_RANK = 128
MLA_NOPE = 64
MLA_ROPE = 32
MLA_QK = MLA_NOPE + MLA_ROPE
ROPE_THETA = 10000.0
SWA_KV_HEADS = 2
WINDOW = 128
D_FF = 2816
IN_PAD = 6144
N_CHIPS = 4
N_DEV = 8

ADAM_LR = 0.001
ADAM_B1 = 0.9
ADAM_B2 = 0.999
ADAM_EPS = 1e-08
ADAM_WD = 0.01
ADAM_STEP = 10

LANES = 128
SUBLANES = 8
ROW_PAD = 128
CAUSAL_TILE = 384
NEG = -1e30
VMEM_LIMIT = 56 * 1024 * 1024

O_FQ, O_FK, O_FV, O_FF = 0, 512, 1024, 1536
O_CQ, O_CKV, O_KR = 1664, 1920, 2048
O_SQ, O_SK, O_SV, O_G = 2176, 2688, 2816, 2944

SHARD_AXIS = {"meta_tokens": 1, "w_in": 2, "mla_w_q_up": 2, "mla_w_kv_up": 2, "w_branch": 3, "w_o": 1,
              "ffn_w_up": 2, "ffn_conv_w": 2, "ffn_w_down": 1}
SHARD_SHAPE = {"meta_tokens": (16, 256), "w_in": (2, 1024, 1450), "mla_w_q_up": (2, 256, 192),
               "mla_w_kv_up": (2, 128, 256), "w_branch": (2, 3, 512, 256), "w_o": (2, 256, 1024),
               "ffn_w_up": (2, 1024, 1408), "ffn_conv_w": (2, 3, 1408), "ffn_w_down": (2, 704, 1024)}
BIG = ("w_in", "mla_w_q_up", "mla_w_kv_up", "w_branch", "w_o", "ffn_w_up", "ffn_w_down")
FINE = ("meta_tokens", "ffn_conv_w")
SMALL = ("norm1_g", "fox_forget_b", "fox_q_g", "fox_k_g", "mla_q_a_g", "mla_kv_a_g", "mla_q_g", "mla_k_g",
         "swa_q_g", "swa_k_g", "swa_sinks", "norm2_g", "ffn_conv_b")
WEIGHTS = ("meta_tokens", "norm1_g", "w_in", "fox_forget_b", "fox_q_g", "fox_k_g", "mla_q_a_g", "mla_w_q_up",
           "mla_kv_a_g", "mla_w_kv_up", "mla_q_g", "mla_k_g", "swa_q_g", "swa_k_g", "swa_sinks", "w_branch", "w_o",
           "norm2_g", "ffn_w_up", "ffn_conv_w", "ffn_conv_b", "ffn_w_down")


def _cparams(sem):
    return pltpu.CompilerParams(dimension_semantics=sem, vmem_limit_bytes=VMEM_LIMIT)


def _div_tile(n, cap, mult=SUBLANES):
    best = None
    for t in range(mult, min(n, cap) + 1, mult):
        if n % t == 0:
            best = t
    return best if best is not None else n


def _rows_tile(n, width, budget=2 << 20):
    return _div_tile(n, max(SUBLANES, budget // (4 * max(width, LANES))))


def _op(fwd, bwd):
    @jax.custom_vjp
    def op(*args):
        return fwd(*args)[0]
    op.defvjp(fwd, bwd)
    return op


def _rotate(v, cos, sin):
    lane = lax.broadcasted_iota(jnp.int32, v.shape, 1)
    rot = jnp.where(lane < MLA_NOPE + MLA_ROPE // 2, -pltpu.roll(v, LANES - MLA_ROPE // 2, 1),
                    pltpu.roll(v, MLA_ROPE // 2, 1))
    return v * cos + rot * sin


def _rms_fwd_call(x, g, denom, name, rot=None):
    n, c = x.shape
    tr = _rows_tile(n if rot is None else rot[0].shape[0], c)
    nt = None if rot is None else rot[0].shape[0] // tr

    def body(x_ref, g_ref, *rest):
        y_ref = rest[-1]
        xv = x_ref[...]
        ms = jnp.sum(xv * xv, axis=-1, keepdims=True) * (1.0 / denom)
        y = xv * lax.rsqrt(ms + EPS) * g_ref[...]
        y_ref[...] = y if rot is None else _rotate(y, rest[0][...], rest[1][...])

    ins, args = [BS((tr, c), lambda i: (i, 0)), BS((1, c), lambda i: (0, 0))], [x, g]
    if rot is not None:
        ins += [BS((tr, c), lambda i: (i % nt, 0))] * 2
        args += list(rot)
    return pl.pallas_call(
        body, name=name, grid=(n // tr,), in_specs=ins,
        out_specs=BS((tr, c), lambda i: (i, 0)), out_shape=SDS((n, c), F32),
        compiler_params=_cparams(("parallel",)))(*args)


def _rms_bwd_call(x, g, dy, denom, name, rot=None):
    n, c = x.shape
    tr = _rows_tile(n if rot is None else rot[0].shape[0], c)
    nt = None if rot is None else rot[0].shape[0] // tr

    def body(x_ref, g_ref, dy_ref, *rest):
        dx_ref, dg_ref = rest[-2:]
        xv = x_ref[...]
        dy = dy_ref[...]
        if rot is not None:
            dy = _rotate(dy, rest[0][...], -rest[1][...])
        ms = jnp.sum(xv * xv, axis=-1, keepdims=True) * (1.0 / denom)
        r = lax.rsqrt(ms + EPS)
        xh = xv * r
        dxh = dy * g_ref[...]
        dx_ref[...] = r * (dxh - xh * (jnp.sum(dxh * xh, axis=-1, keepdims=True) * (1.0 / denom)))

        @pl.when(pl.program_id(0) == 0)
        def _():
            dg_ref[...] = jnp.zeros_like(dg_ref)

        dg_ref[...] += jnp.sum(dy * xh, axis=0, keepdims=True)

    ins = [BS((tr, c), lambda i: (i, 0)), BS((1, c), lambda i: (0, 0)), BS((tr, c), lambda i: (i, 0))]
    args = [x, g, dy]
    if rot is not None:
        ins += [BS((tr, c), lambda i: (i % nt, 0))] * 2
        args += list(rot)
    return pl.pallas_call(
        body, name=name, grid=(n // tr,), in_specs=ins,
        out_specs=[BS((tr, c), lambda i: (i, 0)), BS((1, c), lambda i: (0, 0))],
        out_shape=[SDS((n, c), F32), SDS((1, c), F32)],
        compiler_params=_cparams(("arbitrary",)))(*args)


def rms_norm(x, g, denom, name):
    def fwd(x, g):
        return _rms_fwd_call(x, g, denom, name + "_f"), (x, g)

    def bwd(res, dy):
        return tuple(_rms_bwd_call(res[0], res[1], dy, denom, name + "_b"))

    return _op(fwd, bwd)(x, g)


def rms_norm_rope(x, g, cos, sin, denom, name):
    def fwd(x, g, cos, sin):
        return _rms_fwd_call(x, g, denom, name + "_f", (cos, sin)), (x, g, cos, sin)

    def bwd(res, dy):
        x, g, cos, sin = res
        dx, dg = _rms_bwd_call(x, g, dy, denom, name + "_b", (cos, sin))
        return dx, dg, jnp.zeros_like(cos), jnp.zeros_like(sin)

    return _op(fwd, bwd)(x, g, cos, sin)


def _mm_call(a, b, mode, res, name):
    if mode == "nn":
        (m, kc), n = a.shape, b.shape[1]
    elif mode == "nt":
        (m, kc), n = a.shape, b.shape[0]
    else:
        (kc, m), n = a.shape, b.shape[1]
    if mode == "tn":
        tk = _div_tile(kc, 528)
        tm = _div_tile(m, 1408, LANES)
        tn = _div_tile(n, 2048, LANES)
    else:
        tk = kc if kc <= 2816 else _div_tile(kc, 1024, LANES)
        tm = _div_tile(m, max(LANES, (9 << 19) // (4 * tk)))
        tn = _div_tile(n, 1408 if mode == "nt" else 512, LANES)
    nk = kc // tk
    dims = {"nn": (((1,), (0,)), ((), ())), "nt": (((1,), (1,)), ((), ())), "tn": (((0,), (0,)), ((), ()))}[mode]

    def body(*refs):
        if res is None:
            a_ref, b_ref, o_ref, acc_ref = refs
            r_ref = None
        else:
            a_ref, b_ref, r_ref, o_ref, acc_ref = refs
        k = pl.program_id(2)

        @pl.when(k == 0)
        def _():
            acc_ref[...] = jnp.zeros_like(acc_ref)

        acc_ref[...] += lax.dot_general(a_ref[...].astype(BF16), b_ref[...].astype(BF16), dims,
                                        preferred_element_type=F32)

        @pl.when(k == nk - 1)
        def _():
            if r_ref is None:
                o_ref[...] = acc_ref[...]
            else:
                o_ref[...] = r_ref[...] + acc_ref[...]

    a_spec = BS((tk, tm), lambda i, j, k: (k, i)) if mode == "tn" else BS((tm, tk), lambda i, j, k: (i, k))
    b_spec = BS((tn, tk), lambda i, j, k: (j, k)) if mode == "nt" else BS((tk, tn), lambda i, j, k: (k, j))
    o_spec = BS((tm, tn), lambda i, j, k: (i, j))
    ins, args = [a_spec, b_spec], [a, b]
    if res is not None:
        ins.append(o_spec)
        args.append(res)
    return pl.pallas_call(
        body, name=name, grid=(m // tm, n // tn, nk), in_specs=ins, out_specs=o_spec,
        out_shape=SDS((m, n), F32), scratch_shapes=[pltpu.VMEM((tm, tn), F32)],
        compiler_params=_cparams(("parallel", "parallel", "arbitrary")))(*args)


def linear(a, w, eps, name, res=None):
    if res is None:
        def fwd(a, w, eps):
            return _mm_call(a, w, "nn", None, name + "_f"), (a, w)

        def bwd(r, dc):
            a, w = r
            return (_mm_call(dc, w, "nt", None, name + "_da"), jnp.zeros_like(w),
                    _mm_call(a, dc, "tn", None, name + "_dw"))

        return _op(fwd, bwd)(a, w, eps)

    def fwd_r(a, w, eps, res):
        return _mm_call(a, w, "nn", res, name + "_f"), (a, w)

    def bwd_r(r, dc):
        a, w = r
        return (_mm_call(dc, w, "nt", None, name + "_da"), jnp.zeros_like(w),
                _mm_call(a, dc, "tn", None, name + "_dw"), dc)

    return _op(fwd_r, bwd_r)(a, w, eps, res)


def _gate_tile(n):
    return _div_tile(n, CAUSAL_TILE, LANES)


def _tri_dot(v, upper):
    ct = v.shape[1]
    r = lax.broadcasted_iota(jnp.int32, (ct, ct), 0)
    c = lax.broadcasted_iota(jnp.int32, (ct, ct), 1)
    tri = jnp.where((r <= c) if upper else (r >= c), 1.0, 0.0).astype(F32)
    return jnp.dot(v, tri, preferred_element_type=F32, precision=lax.Precision.HIGHEST)


def _gate_fwd_call(z, b, name):
    h, n = z.shape
    CT = _gate_tile(n)

    def body(z_ref, b_ref, c_ref, carry):
        @pl.when(pl.program_id(0) == 0)
        def _():
            carry[...] = jnp.zeros_like(carry)

        x = z_ref[...] + b_ref[...]
        ls = jnp.minimum(x, 0.0) - jnp.log(1.0 + jnp.exp(-jnp.abs(x)))
        c_ref[...] = _tri_dot(ls, True) + carry[...]
        carry[...] += jnp.sum(ls, axis=1, keepdims=True)

    return pl.pallas_call(
        body, name=name, grid=(n // CT,),
        in_specs=[BS((h, CT), lambda j: (0, j)), BS((h, 1), lambda j: (0, 0))],
        out_specs=BS((h, CT), lambda j: (0, j)), out_shape=SDS((h, n), F32),
        scratch_shapes=[pltpu.VMEM((h, 1), F32)],
        compiler_params=_cparams(("arbitrary",)))(z, b)


def _gate_bwd_call(z, b, dc, name):
    h, n = z.shape
    CT = _gate_tile(n)
    nt = n // CT

    def body(z_ref, b_ref, dc_ref, dz_ref, db_ref, carry):
        @pl.when(pl.program_id(0) == 0)
        def _():
            carry[...] = jnp.zeros_like(carry)
            db_ref[...] = jnp.zeros_like(db_ref)

        dcv = dc_ref[...]
        dls = _tri_dot(dcv, False) + carry[...]
        carry[...] += jnp.sum(dcv, axis=1, keepdims=True)
        x = z_ref[...] + b_ref[...]
        e = jnp.exp(-jnp.abs(x))
        dz = dls * jnp.where(x >= 0, e / (1.0 + e), 1.0 / (1.0 + e))
        dz_ref[...] = dz
        db_ref[...] += jnp.sum(dz, axis=1, keepdims=True)

    rev = lambda j: (0, nt - 1 - j)
    return pl.pallas_call(
        body, name=name, grid=(nt,),
        in_specs=[BS((h, CT), rev), BS((h, 1), lambda j: (0, 0)), BS((h, CT), rev)],
        out_specs=[BS((h, CT), rev), BS((h, 1), lambda j: (0, 0))],
        out_shape=[SDS((h, n), F32), SDS((h, 1), F32)],
        scratch_shapes=[pltpu.VMEM((h, 1), F32)],
        compiler_params=_cparams(("arbitrary",)))(z, b, dc)


def forget_cumsum(z, b, name):
    def fwd(z, b):
        return _gate_fwd_call(z, b, name + "_f"), (z, b)

    def bwd(res, dc):
        return tuple(_gate_bwd_call(res[0], res[1], dc, name + "_b"))

    return _op(fwd, bwd)(z, b)


NT_DIMS = (((1,), (1,)), ((), ()))
TN_DIMS = (((0,), (0,)), ((), ()))
HEADS_PER_STEP = 2


def _causal_tile(n):
    return CAUSAL_TILE if n % CAUSAL_TILE == 0 else ROW_PAD


def _causal_fwd_call(q, k, v, ck_r, fox, scale, name, late=None, late_layer=1):
    h, n, dk = q.shape
    dv = v.shape[2]
    t = _causal_tile(n)
    nq = n // t
    hb = HEADS_PER_STEP
    nl = 0 if late is None else len(late)
    n_in = 3 + int(fox) + nl

    def body(*refs):
        q_ref, k_ref, v_ref = refs[:3]
        ck_ref = refs[3] if fox else None
        o_ref, lse_ref = refs[n_in:n_in + 2]
        m_scr, l_scr, acc_scr = refs[n_in + 2 + nl:n_in + 5 + nl]
        qi = pl.program_id(1)
        if nl:
            start, forward, drain, receive = _layer_gather(refs[n_in - nl:n_in], refs[n_in + 2:n_in + 2 + nl],
                                                           refs[-2], refs[-1], late_layer, 0)
            hp, core = pl.program_id(0), lax.axis_index("c")
            last = (hp == h // hb - 1) & (qi == nq - 1)
            pl.when((hp == 0) & (qi == 0) & (core == late_layer))(start)
            pl.when((hp == h // hb // 2) & (qi == 0) & (core == late_layer))(forward)
            pl.when(last & (core == late_layer))(drain)
            pl.when(last & (core == 1 - late_layer))(receive)
        qbs = [q_ref[e].astype(BF16) for e in range(hb)]
        m_scr[...] = jnp.full(m_scr.shape, NEG, F32)
        l_scr[...] = jnp.zeros_like(l_scr)
        acc_scr[...] = jnp.zeros_like(acc_scr)

        def process(j, masked):
            off = pl.multiple_of(j * t, t)
            if masked:
                rows = lax.broadcasted_iota(jnp.int32, (t, t), 0)
                cols = lax.broadcasted_iota(jnp.int32, (t, t), 1)
                valid = cols <= rows
            for e in range(hb):
                kb = k_ref[e, pl.ds(off, t), :].astype(BF16)
                vb = v_ref[e, pl.ds(off, t), :].astype(BF16)
                s = lax.dot_general(qbs[e], kb, NT_DIMS, preferred_element_type=F32) * scale
                if fox:
                    s = s - ck_ref[e, j]
                if masked:
                    s = jnp.where(valid, s, NEG)
                m_old = m_scr[e]
                m_new = jnp.maximum(m_old, jnp.max(s, axis=1, keepdims=True))
                alpha = jnp.exp(m_old - m_new)
                p = jnp.exp(s - jnp.tile(m_new, (1, t // LANES)))
                l_scr[e] = alpha * l_scr[e] + jnp.sum(p, axis=1, keepdims=True)
                acc_scr[e] = alpha[:, :dv] * acc_scr[e] + jnp.dot(p.astype(BF16), vb, preferred_element_type=F32)
                m_scr[e] = m_new

        def step(j, carry):
            process(j, False)
            return carry

        lax.fori_loop(0, qi, step, 0)
        process(qi, True)
        for e in range(hb):
            l = l_scr[e]
            o_ref[e] = acc_scr[e] / l[:, :dv]
            lse_ref[e, 0] = jnp.transpose(m_scr[e] + jnp.log(l))[0:1, :]

    ins = [BS((hb, t, dk), lambda a, b: (a, b, 0)), BS((hb, n, dk), lambda a, b: (a, 0, 0)),
           BS((hb, n, dv), lambda a, b: (a, 0, 0))]
    args = [q, k, v]
    if fox:
        ins.append(BS((hb, nq, 1, t), lambda a, b: (a, 0, 0, 0)))
        args.append(ck_r)
    outs = [BS((hb, t, dv), lambda a, b: (a, b, 0)), BS((hb, 1, 1, t), lambda a, b: (a, b, 0, 0))]
    oshape = [SDS((h, n, dv), F32), SDS((h, nq, 1, t), F32)]
    scratch = [pltpu.VMEM((hb, t, LANES), F32), pltpu.VMEM((hb, t, LANES), F32), pltpu.VMEM((hb, t, dv), F32)]
    if nl:
        ins += [ANY] * nl
        args += list(late)
        outs += [ANY] * nl
        oshape += [SDS((N_CHIPS,) + a.shape[1:], a.dtype) for a in late]
        scratch += [pltpu.SemaphoreType.DMA((6 * nl,)), pltpu.SemaphoreType.DMA((6 * nl,))]
    res = pl.pallas_call(
        body, name=name, grid=(h // hb, nq), in_specs=ins, out_specs=outs, out_shape=oshape, scratch_shapes=scratch,
        compiler_params=pltpu.CompilerParams(dimension_semantics=("arbitrary", "arbitrary"),
                                             vmem_limit_bytes=VMEM_LIMIT, has_side_effects=bool(nl)))(*args)
    return res[0], res[1], list(res[2:])


def _causal_bwd_call(q, k, v, do, o, lse_r, ck_r, fox, scale, name, side=None):
    h, n, dk = q.shape
    dv = v.shape[2]
    t = _causal_tile(n)
    nq = n // t
    hb = HEADS_PER_STEP
    ns = 0 if side is None else len(side)

    def body(*refs):
        it = iter(refs)
        q_ref, k_ref, v_ref, do_ref, o_ref, lse_ref = (next(it) for _ in range(6))
        ck_ref = next(it) if fox else None
        side_in = [next(it) for _ in range(ns)]
        dq_ref, dk_ref, dv_ref = next(it), next(it), next(it)
        dck_ref, dcq_ref = (next(it), next(it)) if fox else (None, None)
        side_out = [next(it) for _ in range(ns)]
        delta_scr, dk_scr, dv_scr = next(it), next(it), next(it)
        dck_scr = next(it) if fox else None
        kj = pl.program_id(1)
        if ns:
            send_sems, recv_sems = next(it), next(it)
            x, y, core = _me()
            chips = [(1 - x, y), (x, 1 - y), (1 - x, 1 - y)]
            copies = [_remote(side_in[p].at[2 * cx + cy], side_out[p].at[r], send_sems, recv_sems, 3 * p + r,
                              (cx, cy, core)) for p in range(ns) for r, (cx, cy) in enumerate(chips)]

            @pl.when((pl.program_id(0) == 0) & (kj == 0))
            def _():
                for cp in copies:
                    cp.start()

            @pl.when((pl.program_id(0) == h // hb - 1) & (kj == nq - 1))
            def _():
                for cp in copies:
                    cp.wait()

        @pl.when(kj == 0)
        def _():
            dq_ref[...] = jnp.zeros_like(dq_ref)
            if fox:
                dcq_ref[...] = jnp.zeros_like(dcq_ref)
            ones = jnp.ones((SUBLANES, dv), F32)

            def fill(qi, carry):
                off = pl.multiple_of(qi * t, t)
                for e in range(hb):
                    prod = do_ref[e, pl.ds(off, t), :] * o_ref[e, pl.ds(off, t), :]
                    delta_scr[e, qi] = lax.dot_general(ones, prod, NT_DIMS, preferred_element_type=F32,
                                                       precision=lax.Precision.HIGHEST)[0:1, :]
                return carry

            lax.fori_loop(0, nq, fill, 0)

        kbs = [k_ref[e].astype(BF16) for e in range(hb)]
        vbs = [v_ref[e].astype(BF16) for e in range(hb)]
        dk_scr[...] = jnp.zeros_like(dk_scr)
        dv_scr[...] = jnp.zeros_like(dv_scr)
        if fox:
            dck_scr[...] = jnp.zeros_like(dck_scr)
            ckcs = [jnp.tile(jnp.transpose(jnp.broadcast_to(ck_ref[e, 0], (LANES, t))), (1, t // LANES))
                    for e in range(hb)]

        def process(qi, masked):
            off = pl.multiple_of(qi * t, t)
            if masked:
                krows = lax.broadcasted_iota(jnp.int32, (t, t), 0)
                qcols = lax.broadcasted_iota(jnp.int32, (t, t), 1)
                valid = krows <= qcols
            for e in range(hb):
                qb = q_ref[e, pl.ds(off, t), :].astype(BF16)
                dob = do_ref[e, pl.ds(off, t), :].astype(BF16)
                st = lax.dot_general(kbs[e], qb, NT_DIMS, preferred_element_type=F32) * scale
                if fox:
                    st = st - ckcs[e]
                pt = jnp.exp(st - lse_ref[e, qi])
                if masked:
                    pt = jnp.where(valid, pt, 0.0)
                dv_scr[e] += jnp.dot(pt.astype(BF16), dob, preferred_element_type=F32)
                dpt = lax.dot_general(vbs[e], dob, NT_DIMS, preferred_element_type=F32)
                dst = pt * (dpt - delta_scr[e, qi])
                if fox:
                    dck_scr[e] -= jnp.sum(dst, axis=1, keepdims=True)
                    dcq_ref[e, qi] += jnp.sum(dst, axis=0, keepdims=True)
                dsb = (dst * scale).astype(BF16)
                dk_scr[e] += jnp.dot(dsb, qb, preferred_element_type=F32)
                dq_ref[e, pl.ds(off, t), :] += lax.dot_general(dsb, kbs[e], TN_DIMS, preferred_element_type=F32)

        def step(qi, carry):
            process(qi, False)
            return carry

        process(kj, True)
        lax.fori_loop(kj + 1, nq, step, 0)
        dk_ref[...] = dk_scr[...]
        dv_ref[...] = dv_scr[...]
        if fox:
            for e in range(hb):
                dck_ref[e, 0] = jnp.transpose(jnp.broadcast_to(dck_scr[e], (t, LANES)))[0:1, :]

    whole = lambda a, b: (a, 0, 0)
    tile = lambda a, b: (a, b, 0)
    rowv = lambda a, b: (a, 0, 0, 0)
    rowt = lambda a, b: (a, b, 0, 0)
    ins = [BS((hb, n, dk), whole), BS((hb, t, dk), tile), BS((hb, t, dv), tile), BS((hb, n, dv), whole),
           BS((hb, n, dv), whole), BS((hb, nq, 1, t), rowv)]
    args = [q, k, v, do, o, lse_r]
    outs = [BS((hb, n, dk), whole), BS((hb, t, dk), tile), BS((hb, t, dv), tile)]
    oshape = [SDS((h, n, dk), F32), SDS((h, n, dk), F32), SDS((h, n, dv), F32)]
    scratch = [pltpu.VMEM((hb, nq, 1, t), F32), pltpu.VMEM((hb, t, dk), F32), pltpu.VMEM((hb, t, dv), F32)]
    if fox:
        ins.append(BS((hb, 1, 1, t), rowt))
        args.append(ck_r)
        outs += [BS((hb, 1, 1, t), rowt), BS((hb, nq, 1, t), rowv)]
        oshape += [SDS((h, nq, 1, t), F32), SDS((h, nq, 1, t), F32)]
        scratch.append(pltpu.VMEM((hb, t, 1), F32))
    if ns:
        ins += [ANY] * ns
        args += list(side)
        outs += [ANY] * ns
        oshape += [SDS((3,) + s.shape[1:], s.dtype) for s in side]
        scratch += [pltpu.SemaphoreType.DMA((3 * ns,)), pltpu.SemaphoreType.DMA((3 * ns,))]
    return pl.pallas_call(
        body, name=name, grid=(h // hb, nq), in_specs=ins, out_specs=outs, out_shape=oshape, scratch_shapes=scratch,
        compiler_params=pltpu.CompilerParams(dimension_semantics=("arbitrary", "arbitrary"),
                                             vmem_limit_bytes=VMEM_LIMIT, has_side_effects=bool(ns)))(*args)


def causal_attention(q, k, v, c, scale, name, late=None, late_layer=1, reduce=None, sinks=None):
    h, n, _ = q.shape
    t = _causal_tile(n)
    nq = n // t
    fox = c is not None

    def run_fwd(q, k, v, c, late, sinks):
        ck_r = c.reshape(h, nq, 1, t) if fox else None
        o, lse, got = _causal_fwd_call(q, k, v, ck_r, fox, scale, name + "_f", late, late_layer)
        out = (o,)
        if late is not None:
            out += (got,)
        if reduce is not None:
            out += ([jnp.zeros((N_CHIPS,) + SHARD_SHAPE[w][1:], F32) for w in reduce],)
        return (out if len(out) > 1 else o), (q, k, v, c, late, o, lse)

    def run_bwd(res, ct):
        q, k, v, c, late, o, lse = res
        ck_r = c.reshape(h, nq, 1, t) if fox else None
        dlate = None if late is None else [jnp.zeros_like(a) for a in late]
        if reduce is None:
            do = ct if late is None else ct[0]
            outs = _causal_bwd_call(q, k, v, do, o, lse, ck_r, fox, scale, name + "_b")
            return outs[0], outs[1], outs[2], ((outs[3] + outs[4]).reshape(h, n) if fox else None), dlate, None
        do, g1 = ct[0], ct[-1]
        xi, yi, ci = _me()
        c_idx = ci.astype(jnp.int32).reshape(1)
        j_idx = (2 * xi + yi).astype(jnp.int32).reshape(1)
        gs = [g.reshape(N_CHIPS, 2, HALVED[w][1] // 2, HALVED[w][2]) for g, w in zip(g1, reduce)]
        r1 = _pair_exchange(gs, name + "_pair_exchange")
        s1 = [_pair_add(g, r, c_idx, name + "_pair_add_" + w) for g, r, w in zip(gs, r1, reduce)]
        outs = _causal_bwd_call(q, k, v, do, o, lse, ck_r, fox, scale, name + "_b", [s[1] for s in s1])
        gh = [_chip_add(s[0], r, j_idx, name + "_chip_add_" + w) for s, r, w in zip(s1, outs[-len(reduce):], reduce)]
        go = _half_exchange(gh, name + "_half_exchange")
        return (outs[0], outs[1], outs[2], ((outs[3] + outs[4]).reshape(h, n) if fox else None), dlate,
                list(zip(gh, go)))

    return _op(run_fwd, run_bwd)(q, k, v, c, late, sinks)


SWA_T = 128


def _swa_masks(qi):
    t = SWA_T
    r = lax.broadcasted_iota(jnp.int32, (t, 3 * t), 0)
    c = lax.broadcasted_iota(jnp.int32, (t, 3 * t), 1)
    seg0 = c < t
    seg1 = (c >= t) & (c < 2 * t)
    jp = jnp.maximum(qi - 1, 0)
    kpos = jnp.where(seg0, c, jnp.where(seg1, jp * t + c - t, qi * t + c - 2 * t))
    dist = qi * t + r - kpos
    band = (dist >= 0) & ((dist < WINDOW) | (kpos < N_META))
    valid = (seg0 & (kpos < N_META) & (qi >= 2)) | (jnp.logical_not(seg0) & band & (jnp.logical_not(seg1) | (qi >= 1)))
    return valid, dist.astype(F32)


def _swa_cat(ref, qi):
    t = SWA_T
    jp = jnp.maximum(qi - 1, 0)
    return jnp.concatenate([ref[0, 0:t, :], ref[0, pl.ds(pl.multiple_of(jp * t, t), t), :],
                            ref[0, pl.ds(pl.multiple_of(qi * t, t), t), :]], axis=0).astype(BF16)


def _swa_fwd_call(q, k, v, sinks, slopes, scale, name):
    hq, n, d = q.shape
    hkv = k.shape[0]
    g = hq // hkv
    t = SWA_T
    nq = n // t

    def body(q_ref, k_ref, v_ref, sink_ref, slope_ref, o_ref, lse_ref):
        grp = pl.program_id(0)
        qi = pl.program_id(1)
        valid, dist = _swa_masks(qi)
        kc = _swa_cat(k_ref, qi)
        vc = _swa_cat(v_ref, qi)
        qs = jnp.concatenate([q_ref[e] for e in range(g)], axis=0).astype(BF16)
        s_all = lax.dot_general(qs, kc, NT_DIMS, preferred_element_type=F32) * scale
        ps, ls, ms = [], [], []
        for e in range(g):
            hh = grp * g + e
            s = jnp.where(valid, s_all[e * t:(e + 1) * t] - slope_ref[hh] * dist, NEG)
            m = jnp.maximum(jnp.max(s, axis=1, keepdims=True), sink_ref[hh])
            p = jnp.exp(s - m)
            ls.append(jnp.sum(p, axis=1, keepdims=True) + jnp.exp(sink_ref[hh] - m))
            ms.append(m)
            ps.append(p.astype(BF16))
        acc = jnp.dot(jnp.concatenate(ps, axis=0), vc, preferred_element_type=F32)
        for e in range(g):
            o_ref[e] = acc[e * t:(e + 1) * t] / ls[e]
            lse_ref[e] = ms[e] + jnp.log(ls[e])

    return pl.pallas_call(
        body, name=name, grid=(hkv, nq),
        in_specs=[BS((g, t, d), lambda a, b: (a, b, 0)), BS((1, n, d), lambda a, b: (a, 0, 0)),
                  BS((1, n, d), lambda a, b: (a, 0, 0)), BS(memory_space=pltpu.SMEM), BS(memory_space=pltpu.SMEM)],
        out_specs=[BS((g, t, d), lambda a, b: (a, b, 0)), BS((g, t, 1), lambda a, b: (a, b, 0))],
        out_shape=[SDS((hq, n, d), F32), SDS((hq, n, 1), F32)],
        compiler_params=_cparams(("parallel", "parallel")))(q, k, v, sinks, slopes)


def _swa_bwd_call(q, k, v, o, lse, do, sinks, slopes, scale, name):
    hq, n, d = q.shape
    hkv = k.shape[0]
    g = hq // hkv
    t = SWA_T
    nq = n // t

    def body(q_ref, k_ref, v_ref, o_ref, lse_ref, do_ref, sink_ref, slope_ref, dq_ref, dk_ref, dv_ref, ds_ref):
        grp = pl.program_id(0)
        qi = pl.program_id(1)

        @pl.when(qi == 0)
        def _():
            dk_ref[...] = jnp.zeros_like(dk_ref)
            dv_ref[...] = jnp.zeros_like(dv_ref)
            ds_ref[...] = jnp.zeros_like(ds_ref)

        valid, dist = _swa_masks(qi)
        kc = _swa_cat(k_ref, qi)
        vc = _swa_cat(v_ref, qi)
        qs = jnp.concatenate([q_ref[e] for e in range(g)], axis=0).astype(BF16)
        dos = jnp.concatenate([do_ref[e] for e in range(g)], axis=0).astype(BF16)
        s_all = lax.dot_general(qs, kc, NT_DIMS, preferred_element_type=F32) * scale
        dp_all = lax.dot_general(dos, vc, NT_DIMS, preferred_element_type=F32)
        ps, dss = [], []
        for e in range(g):
            hh = grp * g + e
            lse_e = lse_ref[e]
            delta = jnp.sum(do_ref[e] * o_ref[e], axis=1, keepdims=True)
            s = s_all[e * t:(e + 1) * t] - slope_ref[hh] * dist
            p = jnp.where(valid, jnp.exp(s - lse_e), 0.0)
            ds = p * (dp_all[e * t:(e + 1) * t] - delta)
            ps.append(p.astype(BF16))
            dss.append((ds * scale).astype(BF16))
            ds_ref[e] += -jnp.sum(jnp.exp(sink_ref[hh] - lse_e) * delta)
        p_st = jnp.concatenate(ps, axis=0)
        ds_st = jnp.concatenate(dss, axis=0)
        dq = jnp.dot(ds_st, kc, preferred_element_type=F32)
        for e in range(g):
            dq_ref[e] = dq[e * t:(e + 1) * t]
        dkc = lax.dot_general(ds_st, qs, TN_DIMS, preferred_element_type=F32)
        dvc = lax.dot_general(p_st, dos, TN_DIMS, preferred_element_type=F32)
        jp = jnp.maximum(qi - 1, 0)
        for seg, off in enumerate((0, pl.multiple_of(jp * t, t), pl.multiple_of(qi * t, t))):
            dk_ref[0, pl.ds(off, t), :] += dkc[seg * t:(seg + 1) * t]
            dv_ref[0, pl.ds(off, t), :] += dvc[seg * t:(seg + 1) * t]

    tile = lambda a, b: (a, b, 0)
    whole = lambda a, b: (a, 0, 0)
    return pl.pallas_call(
        body, name=name, grid=(hkv, nq),
        in_specs=[BS((g, t, d), tile), BS((1, n, d), whole), BS((1, n, d), whole), BS((g, t, d), tile),
                  BS((g, t, 1), tile), BS((g, t, d), tile), BS(memory_space=pltpu.SMEM), BS(memory_space=pltpu.SMEM)],
        out_specs=[BS((g, t, d), tile), BS((1, n, d), whole), BS((1, n, d), whole), BS((g, 1, LANES), whole)],
        out_shape=[SDS((hq, n, d), F32), SDS((hkv, n, d), F32), SDS((hkv, n, d), F32), SDS((hq, 1, LANES), F32)],
        compiler_params=_cparams(("arbitrary", "arbitrary")))(q, k, v, o, lse, do, sinks, slopes)


def window_attention(q, k, v, sinks, slopes, scale, name):
    def run_fwd(q, k, v, sinks, slopes):
        o, lse = _swa_fwd_call(q, k, v, sinks, slopes, scale, name + "_f")
        return o, (q, k, v, sinks, slopes, o, lse)

    def run_bwd(res, do):
        q, k, v, sinks, slopes, o, lse = res
        dq, dk, dv, ds = _swa_bwd_call(q, k, v, o, lse, do, sinks, slopes, scale, name + "_b")
        return dq, dk, dv, ds[:, 0, 0], jnp.zeros_like(slopes)

    return _op(run_fwd, run_bwd)(q, k, v, sinks, slopes)


def _sigmoid(x):
    return 1.0 / (1.0 + jnp.exp(-x))


def _merge_fwd_call(gs, ys, name):
    n, c = ys[0].shape
    tr = _rows_tile(n, c, 1 << 20)

    def body(g0, g1, g2, y0, y1, y2, m_ref):
        m_ref[...] = (_sigmoid(g0[...]) * y0[...] + _sigmoid(g1[...]) * y1[...]) + _sigmoid(g2[...]) * y2[...]

    spec = BS((tr, c), lambda i: (i, 0))
    return pl.pallas_call(
        body, name=name, grid=(n // tr,), in_specs=[spec] * 6, out_specs=spec, out_shape=SDS((n, c), F32),
        compiler_params=_cparams(("parallel",)))(*gs, *ys)


def _merge_bwd_call(gs, ys, dm, name):
    n, c = ys[0].shape
    tr = _rows_tile(n, c, 1 << 20)

    def body(g0, g1, g2, y0, y1, y2, dm_ref, dg0, dg1, dg2, dy0, dy1, dy2):
        d = dm_ref[...]
        for g, y, dg, dy in ((g0, y0, dg0, dy0), (g1, y1, dg1, dy1), (g2, y2, dg2, dy2)):
            s = _sigmoid(g[...])
            dy[...] = d * s
            dg[...] = d * y[...] * (s * (1.0 - s))

    spec = BS((tr, c), lambda i: (i, 0))
    return pl.pallas_call(
        body, name=name, grid=(n // tr,), in_specs=[spec] * 7, out_specs=[spec] * 6,
        out_shape=[SDS((n, c), F32)] * 6, compiler_params=_cparams(("parallel",)))(*gs, *ys, dm)


def gated_merge(gs, ys, name):
    def fwd(gs, ys):
        return _merge_fwd_call(gs, ys, name + "_f"), (gs, ys)

    def bwd(res, dm):
        out = _merge_bwd_call(res[0], res[1], dm, name + "_b")
        return tuple(out[:3]), tuple(out[3:])

    return _op(fwd, bwd)(tuple(gs), tuple(ys))


CONV_TR = 264
CONV_TC = 1408


def _conv_tiles(n, f):
    tr = CONV_TR if n % CONV_TR == 0 else _div_tile(n, CONV_TR)
    tc = CONV_TC if f % CONV_TC == 0 else f
    return tr, tc


def _shift_down(cur, halo, first, tr):
    halo = jnp.where(first, 0.0, halo)
    row = lax.broadcasted_iota(jnp.int32, cur.shape, 0)
    h7, h6 = halo[7:8, :], halo[6:7, :]
    u1 = jnp.where(row == 0, h7, pltpu.roll(cur, 1, 0))
    u2 = jnp.where(row == 0, h6, jnp.where(row == 1, h7, pltpu.roll(cur, 2, 0)))
    return u1, u2


def _conv_lin(cur, u1, u2, w_ref, b_ref):
    return ((b_ref[...] + w_ref[0:1, :] * u2) + w_ref[1:2, :] * u1) + w_ref[2:3, :] * cur


def _conv_in_specs(tr, tc, nj):
    sub = tr // SUBLANES
    prev = lambda j, i: (jnp.maximum(i * sub - 1, 0), j)
    prev_v = lambda j, i: (jnp.maximum(i * sub - 1, 0), j + nj)
    return [BS((tr, tc), lambda j, i: (i, j)), BS((SUBLANES, tc), prev),
            BS((tr, tc), lambda j, i: (i, j + nj)), BS((SUBLANES, tc), prev_v),
            BS((3, tc), lambda j, i: (0, j)), BS((3, tc), lambda j, i: (0, j + nj)),
            BS((1, tc), lambda j, i: (0, j)), BS((1, tc), lambda j, i: (0, j + nj))]


def _conv_fwd_call(u, cw, cb, name):
    n, f2 = u.shape
    f = f2 // 2
    tr, tc = _conv_tiles(n, f)
    nj = f // tc

    def body(ug, ugh, uv, uvh, wg, wv, bg, bv, a_ref):
        first = pl.program_id(1) == 0
        g1, g2 = _shift_down(ug[...], ugh[...], first, tr)
        v1, v2 = _shift_down(uv[...], uvh[...], first, tr)
        cg = _conv_lin(ug[...], g1, g2, wg, bg)
        cv = _conv_lin(uv[...], v1, v2, wv, bv)
        a_ref[...] = cg * _sigmoid(cg) * cv

    return pl.pallas_call(
        body, name=name, grid=(nj, n // tr), in_specs=_conv_in_specs(tr, tc, nj),
        out_specs=BS((tr, tc), lambda j, i: (i, j)), out_shape=SDS((n, f), F32),
        compiler_params=_cparams(("parallel", "parallel")))(u, u, u, u, cw, cw, cb, cb)


def _conv_bwd_dc_call(u, cw, cb, da, name):
    n, f2 = u.shape
    f = f2 // 2
    tr, tc = _conv_tiles(n, f)
    nj = f // tc

    def body(ug, ugh, uv, uvh, wg, wv, bg, bv, da_ref, dc_ref, dw_ref, db_ref):
        first = pl.program_id(1) == 0
        g0, v0 = ug[...], uv[...]
        g1, g2 = _shift_down(g0, ugh[...], first, tr)
        v1, v2 = _shift_down(v0, uvh[...], first, tr)
        cg = _conv_lin(g0, g1, g2, wg, bg)
        cv = _conv_lin(v0, v1, v2, wv, bv)
        d = da_ref[...]
        s = _sigmoid(cg)
        dcg = d * cv * (s * (1.0 + cg * (1.0 - s)))
        dcv = d * (cg * s)
        dc_ref[0] = dcg
        dc_ref[1] = dcv

        @pl.when(first)
        def _():
            dw_ref[...] = jnp.zeros_like(dw_ref)
            db_ref[...] = jnp.zeros_like(db_ref)

        for p, dc, taps in ((0, dcg, (g2, g1, g0)), (1, dcv, (v2, v1, v0))):
            for t in range(3):
                dw_ref[p, t:t + 1, :] += jnp.sum(dc * taps[t], axis=0, keepdims=True)
            db_ref[p] += jnp.sum(dc, axis=0, keepdims=True)

    return pl.pallas_call(
        body, name=name, grid=(nj, n // tr),
        in_specs=_conv_in_specs(tr, tc, nj) + [BS((tr, tc), lambda j, i: (i, j))],
        out_specs=[BS((2, tr, tc), lambda j, i: (0, i, j)), BS((2, 3, tc), lambda j, i: (0, 0, j)),
                   BS((2, 1, tc), lambda j, i: (0, 0, j))],
        out_shape=[SDS((2, n, f), F32), SDS((2, 3, f), F32), SDS((2, 1, f), F32)],
        compiler_params=_cparams(("arbitrary", "arbitrary")))(u, u, u, u, cw, cw, cb, cb, da)


def _conv_bwd_du_call(dc, cw, name):
    _, n, f = dc.shape
    tr, tc = _conv_tiles(n, f)
    nj = f // tc
    ni = n // tr
    sub = tr // SUBLANES

    def body(c_ref, nx_ref, w_ref, du_ref):
        cur = c_ref[0]
        nxt = jnp.where(pl.program_id(2) == ni - 1, 0.0, nx_ref[0])
        row = lax.broadcasted_iota(jnp.int32, cur.shape, 0)
        n0, n1 = nxt[0:1, :], nxt[1:2, :]
        d1 = jnp.where(row == tr - 1, n0, pltpu.roll(cur, tr - 1, 0))
        d2 = jnp.where(row == tr - 1, n1, jnp.where(row == tr - 2, n0, pltpu.roll(cur, tr - 2, 0)))
        du_ref[...] = (w_ref[2:3, :] * cur + w_ref[1:2, :] * d1) + w_ref[0:1, :] * d2

    nxt_map = lambda p, j, i: (p, jnp.minimum((i + 1) * sub, n // SUBLANES - 1), j)
    return pl.pallas_call(
        body, name=name, grid=(2, nj, ni),
        in_specs=[BS((1, tr, tc), lambda p, j, i: (p, i, j)), BS((1, SUBLANES, tc), nxt_map),
                  BS((3, tc), lambda p, j, i: (0, p * nj + j))],
        out_specs=BS((tr, tc), lambda p, j, i: (i, p * nj + j)), out_shape=SDS((n, 2 * f), F32),
        compiler_params=_cparams(("parallel", "parallel", "parallel")))(dc, dc, cw)


def conv_glu(u, cw, cb, name):
    def fwd(u, cw, cb):
        return _conv_fwd_call(u, cw, cb, name + "_f"), (u, cw, cb)

    def bwd(res, da):
        u, cw, cb = res
        dc, dw, db = _conv_bwd_dc_call(u, cw, cb, da, name + "_bc")
        du = _conv_bwd_du_call(dc, cw, name + "_bu")
        return du, jnp.concatenate([dw[0], dw[1]], axis=-1), jnp.concatenate([db[0], db[1]], axis=-1)

    return _op(fwd, bwd)(u, cw, cb)


def _loss_call(y, t, n_real, name):
    n, c = y.shape
    tr = _rows_tile(n, c, 1 << 20)

    def body(y_ref, t_ref, dy_ref, l_ref):
        i = pl.program_id(0)
        row = i * tr + lax.broadcasted_iota(jnp.int32, (tr, c), 0)
        real = (row >= N_META) & (row < N_META + n_real)
        e = jnp.where(real, y_ref[...] - t_ref[...], 0.0)
        dy_ref[...] = e * (1.0 / c)

        @pl.when(i == 0)
        def _():
            l_ref[...] = jnp.zeros_like(l_ref)

        l_ref[...] += 0.5 * jnp.sum(jnp.sum(e * e, axis=-1, keepdims=True) * (1.0 / c), axis=0, keepdims=True)

    spec = BS((tr, c), lambda i: (i, 0))
    return pl.pallas_call(
        body, name=name, grid=(n // tr,), in_specs=[spec, spec],
        out_specs=[spec, BS((1, 1), lambda i: (0, 0))], out_shape=[SDS((n, c), F32), SDS((1, 1), F32)],
        compiler_params=_cparams(("arbitrary",)))(y, t)


def _to_heads(x, nh):
    n = x.shape[0]
    return x.reshape(n, nh, x.shape[1] // nh).transpose(1, 0, 2)


def _from_heads(x):
    h, n, d = x.shape
    return x.transpose(1, 0, 2).reshape(n, h * d)


def _head_norm(x, g, denom, name):
    h, n, d = x.shape
    return rms_norm(x.reshape(h * n, d), g, denom, name).reshape(h, n, d)


def _head_norm_rope(x, g, cos, sin, name):
    h, n, d = x.shape
    return rms_norm_rope(x.reshape(h * n, d), g, cos, sin, MLA_QK, name).reshape(h, n, d)


def _pad_in_cols(w):
    z = lambda k: jnp.zeros(w.shape[:-1] + (k,), w.dtype)
    return jnp.concatenate([w[..., :1544], z(120), w[..., 1544:1960], z(96), w[..., 1960:], z(128)], axis=-1)


def _pad_q_up(w):
    s = w.shape[:-1]
    w = w.reshape(s + (HEADS, MLA_QK))
    w = jnp.concatenate([w, jnp.zeros(s + (HEADS, LANES - MLA_QK), w.dtype)], axis=-1)
    return w.reshape(s + (HEADS * LANES,))


LAYERED = BIG + ("ffn_conv_w",)
BEFORE_MLA = ("w_in", "mla_w_q_up", "mla_w_kv_up")
AFTER_MLA = ("w_branch", "w_o", "ffn_w_up", "ffn_w_down")


def _assemble_layer(parts):
    out = {k: jnp.concatenate([v[i] for i in range(N_CHIPS)], axis=SHARD_AXIS[k] - 1) for k, v in parts.items()}
    if "w_in" in out:
        out["w_in"] = _pad_in_cols(out["w_in"])
    if "mla_w_q_up" in out:
        out["mla_w_q_up"] = _pad_q_up(out["mla_w_q_up"])
    return out


def _layer_parts(names, gathered):
    return {k: g.reshape((N_CHIPS,) + SHARD_SHAPE[k][1:]) for k, g in zip(names, gathered)}


def _rope_tables(n):
    half = MLA_ROPE // 2
    freqs = ROPE_THETA ** (-jnp.arange(half, dtype=F32) / half)
    ang = jnp.arange(n).astype(F32)[:, None] * freqs[None, :]
    cos, sin = jnp.cos(ang), jnp.sin(ang)
    one, zero = jnp.ones((n, MLA_NOPE), F32), jnp.zeros((n, MLA_NOPE), F32)
    tail1, tail0 = jnp.ones((n, LANES - MLA_QK), F32), jnp.zeros((n, LANES - MLA_QK), F32)
    return (jnp.concatenate([one, cos, cos, tail1], axis=1), jnp.concatenate([zero, sin, sin, tail0], axis=1))


def _pad_lanes(g, width):
    return jnp.concatenate([g, jnp.zeros((width - g.shape[0],), g.dtype)]).reshape(1, width)


PROJ_SEGMENTS = ((O_FQ, 512), (O_FK, 512), (O_FV, 512), (O_FF, HEADS), (O_CQ, MLA_Q_RANK), (O_CKV, MLA_KV_RANK),
                 (O_KR, MLA_ROPE), (O_SQ, 512), (O_SK, 128), (O_SV, 128), (O_G, D_MODEL), (O_G + D_MODEL, D_MODEL),
                 (O_G + 2 * D_MODEL, D_MODEL))


def _join_proj_call(parts, name):
    n = parts[0].shape[0]
    tr = _rows_tile(n, IN_PAD, 4 << 20)

    def body(*refs):
        o_ref = refs[-1]
        o_ref[...] = jnp.zeros_like(o_ref)
        for (s, w), r in zip(PROJ_SEGMENTS, refs[:-1]):
            o_ref[:, s:s + w] = r[...]

    return pl.pallas_call(
        body, name=name, grid=(n // tr,), in_specs=[BS((tr, w), lambda i: (i, 0)) for _, w in PROJ_SEGMENTS],
        out_specs=BS((tr, IN_PAD), lambda i: (i, 0)), out_shape=SDS((n, IN_PAD), F32),
        compiler_params=_cparams(("parallel",)))(*parts)


def _split_proj(proj, name):
    def fwd(x):
        return tuple(x[:, s:s + w] for s, w in PROJ_SEGMENTS), None

    def bwd(_, cts):
        return (_join_proj_call(cts, name + "_b"),)

    return _op(fwd, bwd)(proj)


def _trunk(eps, eps_cw, sinks, meta, small, x, w0, late):
    assert DEPTH == 2
    seq = x.shape[0]
    n = -(-(N_META + seq) // ROW_PAD) * ROW_PAD
    ew = _assemble_layer(eps)
    cos, sin = _rope_tables(n)
    slopes = jnp.exp2(-8.0 * jnp.arange(1, HEADS + 1, dtype=F32) / HEADS)
    h = jnp.concatenate([meta, x, jnp.zeros((n - N_META - seq, D_MODEL), F32)], axis=0)
    wb = w0
    for l in range(DEPTH):
        p = f"l{l}_"
        row = lambda name: small[name][l].reshape(1, -1)
        xn = rms_norm(h, row("norm1_g"), D_MODEL, p + "norm1")
        proj = linear(xn, wb["w_in"], ew["w_in"], p + "win")
        p_fq, p_fk, p_fv, p_ff, p_cq, p_ckv, p_kr, p_sq, p_sk, p_sv, g0, g1, g2 = _split_proj(proj, p + "split")
        fq = _head_norm(_to_heads(p_fq, HEADS), row("fox_q_g"), HEAD_DIM, p + "fqn")
        fk = _head_norm(_to_heads(p_fk, HEADS), row("fox_k_g"), HEAD_DIM, p + "fkn")
        fv = _to_heads(p_fv, HEADS)
        c = forget_cumsum(p_ff.T, small["fox_forget_b"][l].reshape(HEADS, 1), p + "fgate")
        if l == 0:
            rest = [a for k, a in zip(LAYERED, late) if k != "w_in"]
            out_a, got, carriers = causal_attention(fq, fk, fv, c, HEAD_DIM ** -0.5, p + "fox", rest, 0, BIG,
                                                    sinks[1])
            wb = dict(wb, **_assemble_layer(_layer_parts(LAYERED[1:], _fill_own(got, [a[0] for a in rest]))))
            ew1 = _assemble_layer(dict(zip(BIG, carriers)))
        else:
            out_a = causal_attention(fq, fk, fv, c, HEAD_DIM ** -0.5, p + "fox")
        cqn = rms_norm(p_cq, row("mla_q_a_g"), MLA_Q_RANK, p + "cqn")
        q = _to_heads(linear(cqn, wb["mla_w_q_up"], ew["mla_w_q_up"], p + "qup"), HEADS)
        q = _head_norm_rope(q, _pad_lanes(small["mla_q_g"][l], LANES), cos, sin, p + "mqn")
        ckvn = rms_norm(p_ckv, row("mla_kv_a_g"), MLA_KV_RANK, p + "ckvn")
        kv = _to_heads(linear(ckvn, wb["mla_w_kv_up"], ew["mla_w_kv_up"], p + "kvup"), HEADS)
        kr = jnp.broadcast_to(p_kr[None], (HEADS, n, MLA_ROPE))
        k = jnp.concatenate([kv[..., :MLA_NOPE], kr, jnp.zeros((HEADS, n, LANES - MLA_QK), F32)], axis=-1)
        k = _head_norm_rope(k, _pad_lanes(small["mla_k_g"][l], LANES), cos, sin, p + "mkn")
        if l == 0:
            out_b, got, carriers = causal_attention(q, k, kv[..., MLA_NOPE:], None, MLA_QK ** -0.5, p + "mla", late,
                                                    1, AFTER_MLA, sinks[0])
            w1 = _assemble_layer(_layer_parts(LAYERED, _fill_own(got, [a[1] for a in late])))
            ew = dict(ew, **_assemble_layer(dict(zip(AFTER_MLA, carriers))))
        else:
            out_b = causal_attention(q, k, kv[..., MLA_NOPE:], None, MLA_QK ** -0.5, p + "mla")
        sq = _head_norm(_to_heads(p_sq, HEADS), row("swa_q_g"), HEAD_DIM, p + "sqn")
        sk = _head_norm(_to_heads(p_sk, SWA_KV_HEADS), row("swa_k_g"), HEAD_DIM, p + "skn")
        sv = _to_heads(p_sv, SWA_KV_HEADS)
        out_c = window_attention(sq, sk, sv, small["swa_sinks"][l], slopes, HEAD_DIM ** -0.5, p + "swa")
        ys = [linear(_from_heads(o), wb["w_branch"][i], ew["w_branch"][i], p + f"br{i}")
              for i, o in enumerate((out_a, out_b, out_c))]
        merged = gated_merge([g0, g1, g2], ys, p + "merge")
        h = linear(merged, wb["w_o"], ew["w_o"], p + "wo", res=h)
        xn2 = rms_norm(h, row("norm2_g"), D_MODEL, p + "norm2")
        u = linear(xn2, wb["ffn_w_up"], ew["ffn_w_up"], p + "wup")
        act = conv_glu(u, lax.stop_gradient(wb["ffn_conv_w"]) + eps_cw[l], row("ffn_conv_b"), p + "conv")
        h = linear(act, wb["ffn_w_down"], ew["ffn_w_down"], p + "wdown", res=h)
        wb, ew = w1, ew1
    return h


def _local_step(x, target, w0, late, meta, small):
    seq = x.shape[0]
    eps = {k: jnp.zeros((N_CHIPS,) + SHARD_SHAPE[k][1:], F32) for k in BEFORE_MLA}
    eps_cw = jnp.zeros((DEPTH, 3, 2 * D_FF), F32)
    half = lambda k: jnp.zeros((HALVED[k][1] // 2, HALVED[k][2]), F32)
    sinks = tuple([(half(k), half(k)) for k in names] for names in (AFTER_MLA, BIG))
    y, vjp = jax.vjp(lambda e, ec, sk, mt, s, xx: _trunk(e, ec, sk, mt, s, xx, w0, late),
                     eps, eps_cw, sinks, meta, small, x)
    n = y.shape[0]
    tpad = jnp.concatenate([jnp.zeros((N_META, D_MODEL), F32), target, jnp.zeros((n - N_META - seq, D_MODEL), F32)])
    dy, loss = _loss_call(y, tpad, seq, "loss")
    g_eps, g_cw, g_l1, g_meta, g_small, g_x = vjp(dy)
    return loss[0, 0], g_x, g_eps, g_l1, g_cw, g_meta, g_small


def _pack_rows(shapes, mult):
    total = sum(_size(s) for s in shapes)
    rows = -(-total // LANES)
    return -(-rows // mult) * mult


def _size(shape):
    n = 1
    for d in shape:
        n *= d
    return n


def _pack(arrs, rows, dtype):
    flat = [a.reshape(-1).astype(dtype) for a in arrs]
    used = sum(a.size for a in flat)
    flat.append(jnp.zeros((rows * LANES - used,), dtype))
    return jnp.concatenate(flat).reshape(rows, LANES)


def _unpack(p, shapes):
    flat = p.reshape(-1)
    out, off = [], 0
    for s in shapes:
        out.append(flat[off:off + _size(s)].reshape(s))
        off += _size(s)
    return out


MESH = pl.DeviceIdType.MESH
ANY = pl.BlockSpec(memory_space=pl.ANY)


def _me():
    return lax.axis_index("x"), lax.axis_index("y"), lax.axis_index("c")


def _remote(src, dst, send_sems, recv_sems, idx, dev):
    return pltpu.make_async_remote_copy(src_ref=src, dst_ref=dst, send_sem=send_sems.at[idx], recv_sem=recv_sems.at[idx],
                                        device_id=dev, device_id_type=MESH)


def _layer_gather(ins, outs, send_sems, recv_sems, layer, base):
    x, y, _ = _me()
    j = 2 * x + y
    sibling = (x, y, 1 - layer)
    chips = [(1 - x, y), (x, 1 - y), (1 - x, 1 - y)]

    def ici(p, r):
        cx, cy = chips[r]
        return _remote(ins[p].at[layer], outs[p].at[j], send_sems, recv_sems, base + 6 * p + r, (cx, cy, layer))

    def d2d(p, r):
        cx, cy = chips[r]
        blk = outs[p].at[2 * cx + cy]
        return _remote(blk, blk, send_sems, recv_sems, base + 6 * p + 3 + r, sibling)

    pairs = [(p, r) for p in range(len(ins)) for r in range(3)]

    def start():
        for p, r in pairs:
            ici(p, r).start()

    def forward():
        for p, r in pairs:
            ici(p, r).wait_recv()
            d2d(p, r).start()

    def drain():
        for p, r in pairs:
            ici(p, r).wait_send()
            d2d(p, r).wait_send()

    def receive():
        for p, r in pairs:
            d2d(p, r).wait_recv()

    return start, forward, drain, receive


def _fill_own(outs, own):
    j = 2 * lax.axis_index("x") + lax.axis_index("y")
    return [lax.dynamic_update_index_in_dim(o, a, j, 0) for o, a in zip(outs, own)]


def _gather_early(meta, arrs, name):
    npk = len(arrs)

    def body(*refs):
        m_in, ins = refs[0], refs[1:npk + 1]
        m_out, outs = refs[npk + 1], refs[npk + 2:2 * npk + 2]
        send_sems, recv_sems = refs[2 * npk + 2:]
        x, y, c = _me()
        j = 2 * x + y
        sibling = (x, y, 1 - c)
        chips = [(1 - x, y), (x, 1 - y), (1 - x, 1 - y)]
        start, forward, drain, receive = _layer_gather(ins, outs, send_sems, recv_sems, 0, 6)
        sends = []
        for r, (cx, cy) in enumerate(chips):
            cp = _remote(m_in.at[c], m_out.at[j, c], send_sems, recv_sems, r, (cx, cy, c))
            cp.start()
            sends.append(cp)
        pl.when(c == 0)(start)
        for r, (cx, cy) in enumerate(chips):
            blk = m_out.at[2 * cx + cy, c]
            _remote(blk, blk, send_sems, recv_sems, r, sibling).wait_recv()
            fw = _remote(blk, blk, send_sems, recv_sems, 3 + r, sibling)
            fw.start()
            sends.append(fw)
        for r, (cx, cy) in enumerate(chips):
            blk = m_out.at[2 * cx + cy, 1 - c]
            _remote(blk, blk, send_sems, recv_sems, 3 + r, sibling).wait_recv()
        for cp in sends:
            cp.wait_send()

        @pl.when(c == 0)
        def _():
            forward()
            drain()

        pl.when(c == 1)(receive)

    nsem = 6 + 6 * npk
    res = pl.pallas_call(
        body, name=name, in_specs=[ANY] * (npk + 1), out_specs=[ANY] * (npk + 1),
        out_shape=[SDS((N_CHIPS,) + meta.shape, meta.dtype)] + [SDS((N_CHIPS,) + a.shape[1:], a.dtype) for a in arrs],
        scratch_shapes=[pltpu.SemaphoreType.DMA((nsem,)), pltpu.SemaphoreType.DMA((nsem,))],
        compiler_params=pltpu.CompilerParams(has_side_effects=True))(meta, *arrs)
    return _fill_own(res[:1], [meta])[0], _fill_own(res[1:], [a[0] for a in arrs])


def _pair_exchange(gs, name):
    npk = len(gs)

    def body(*refs):
        ins, outs = refs[:npk], refs[npk:2 * npk]
        send_sems, recv_sems = refs[2 * npk:]
        x, y, c = _me()
        cps = [_remote(ins[p].at[:, 1 - c], outs[p], send_sems, recv_sems, p, (x, y, 1 - c)) for p in range(npk)]
        for cp in cps:
            cp.start()
        for cp in cps:
            cp.wait()

    return pl.pallas_call(
        body, name=name, in_specs=[ANY] * npk, out_specs=[ANY] * npk,
        out_shape=[SDS((N_CHIPS,) + g.shape[2:], g.dtype) for g in gs],
        scratch_shapes=[pltpu.SemaphoreType.DMA((npk,)), pltpu.SemaphoreType.DMA((npk,))],
        compiler_params=pltpu.CompilerParams(has_side_effects=True))(*gs)


def _chip_exchange(ss, small, name):
    npk = len(ss)

    def body(*refs):
        ins, sm_ref = refs[:npk], refs[npk]
        outs, sa_ref = refs[npk + 1:2 * npk + 1], refs[2 * npk + 1]
        send_sems, recv_sems, loc_sem = refs[2 * npk + 2:]
        x, y, c = _me()
        me = 4 * x + 2 * y + c
        lc = pltpu.make_async_copy(sm_ref, sa_ref.at[me], loc_sem.at[0])
        lc.start()
        cps = []
        for p in range(npk):
            for r, (cx, cy) in enumerate([(1 - x, y), (x, 1 - y), (1 - x, 1 - y)]):
                cp = _remote(ins[p].at[2 * cx + cy], outs[p].at[r], send_sems, recv_sems, 3 * p + r, (cx, cy, c))
                cp.start()
                cps.append(cp)
        base = 3 * npk - 1
        for mask in range(1, N_DEV):
            px, py, pc = x ^ (mask >> 2), y ^ ((mask >> 1) & 1), c ^ (mask & 1)
            cp = _remote(sm_ref, sa_ref.at[me], send_sems, recv_sems, base + mask, (px, py, pc))
            cp.start()
            cps.append(cp)
        for p in range(npk):
            for r in range(3):
                _remote(outs[p].at[r], outs[p].at[r], send_sems, recv_sems, 3 * p + r, (x, y, c)).wait_recv()
        for mask in range(1, N_DEV):
            src = 4 * (x ^ (mask >> 2)) + 2 * (y ^ ((mask >> 1) & 1)) + (c ^ (mask & 1))
            _remote(sa_ref.at[src], sa_ref.at[src], send_sems, recv_sems, base + mask, (x, y, c)).wait_recv()
        for cp in cps:
            cp.wait_send()
        lc.wait()

    nsem = 3 * npk + N_DEV - 1
    res = pl.pallas_call(
        body, name=name, in_specs=[ANY] * (npk + 1), out_specs=[ANY] * (npk + 1),
        out_shape=[SDS((3,) + s.shape[1:], s.dtype) for s in ss] + [SDS((N_DEV,) + small.shape, small.dtype)],
        scratch_shapes=[pltpu.SemaphoreType.DMA((nsem,)), pltpu.SemaphoreType.DMA((nsem,)),
                        pltpu.SemaphoreType.DMA((1,))],
        compiler_params=pltpu.CompilerParams(has_side_effects=True))(*ss, small)
    return res[:npk], res[npk]


def _half_exchange(ghs, name):
    npk = len(ghs)

    def body(*refs):
        ins, outs = refs[:npk], refs[npk:2 * npk]
        send_sems, recv_sems = refs[2 * npk:]
        x, y, c = _me()
        cps = [_remote(ins[p], outs[p], send_sems, recv_sems, p, (x, y, 1 - c)) for p in range(npk)]
        for cp in cps:
            cp.start()
        for cp in cps:
            cp.wait()

    return pl.pallas_call(
        body, name=name, in_specs=[ANY] * npk, out_specs=[ANY] * npk, out_shape=[SDS(g.shape, g.dtype) for g in ghs],
        scratch_shapes=[pltpu.SemaphoreType.DMA((npk,)), pltpu.SemaphoreType.DMA((npk,))],
        compiler_params=pltpu.CompilerParams(has_side_effects=True))(*ghs)


def _add_tile(rows, cols):
    return _div_tile(rows, max(16, (1 << 19) // max(cols, LANES)), 16)


def _pair_add(g, r1, c_idx, name):
    _, rows, cols = r1.shape
    tr = _add_tile(rows, cols)

    def body(c_ref, g_ref, r_ref, o_ref, ob_ref):
        s = g_ref[0] + r_ref[...]
        o_ref[...] = s
        ob_ref[...] = s.astype(BF16)

    own = BS((1, tr, cols), lambda k, i, c: (k, i, 0))
    return pl.pallas_call(
        body, name=name,
        grid_spec=pltpu.PrefetchScalarGridSpec(
            num_scalar_prefetch=1, grid=(N_CHIPS, rows // tr),
            in_specs=[BS((1, 1, tr, cols), lambda k, i, c: (k, c[0], i, 0)), own], out_specs=[own, own]),
        out_shape=[SDS(r1.shape, F32), SDS(r1.shape, BF16)],
        compiler_params=_cparams(("parallel", "parallel")))(c_idx, g, r1)


def _chip_add(s1, r2, j_idx, name):
    _, rows, cols = s1.shape
    tr = _add_tile(rows, cols)

    def body(j_ref, s_ref, r_ref, o_ref):
        o_ref[...] = ((s_ref[0] + r_ref[0].astype(F32)) + r_ref[1].astype(F32)) + r_ref[2].astype(F32)

    return pl.pallas_call(
        body, name=name,
        grid_spec=pltpu.PrefetchScalarGridSpec(
            num_scalar_prefetch=1, grid=(rows // tr,),
            in_specs=[BS((1, tr, cols), lambda i, j: (j[0], i, 0)), BS((3, tr, cols), lambda i, j: (0, i, 0))],
            out_specs=BS((tr, cols), lambda i, j: (i, 0))),
        out_shape=SDS((rows, cols), F32), compiler_params=_cparams(("parallel",)))(j_idx, s1, r2)


def _adamw_math(w, g, m, v):
    m = ADAM_B1 * m + (1.0 - ADAM_B1) * g
    v = ADAM_B2 * v + (1.0 - ADAM_B2) * (g * g)
    m_hat = m / (1.0 - ADAM_B1 ** ADAM_STEP)
    v_hat = v / (1.0 - ADAM_B2 ** ADAM_STEP)
    delta = -ADAM_LR * (m_hat / (jnp.sqrt(v_hat) + ADAM_EPS) + ADAM_WD * w)
    return delta, m, v


def _adamw(w, gh, go, m, v, c_idx, name):
    _, rows, cols = w.shape
    tr = _add_tile(rows, cols)

    def body(c_ref, w_ref, gh_ref, go_ref, m_ref, v_ref, g_out, d_out, m_out, v_out):
        g = jnp.where(pl.program_id(0) == c_ref[0], gh_ref[...], go_ref[...])
        g_out[0] = g
        d_out[0], m_out[0], v_out[0] = _adamw_math(w_ref[0], g, m_ref[0], v_ref[0])

    full = BS((1, tr, cols), lambda hf, i, c: (hf, i, 0))
    half = BS((tr, cols), lambda hf, i, c: (i, 0))
    return pl.pallas_call(
        body, name=name,
        grid_spec=pltpu.PrefetchScalarGridSpec(
            num_scalar_prefetch=1, grid=(2, rows // tr), in_specs=[full, half, half, full, full],
            out_specs=[full] * 4),
        out_shape=[SDS(w.shape, F32)] * 4, compiler_params=_cparams(("parallel", "parallel")))(c_idx, w, gh, go, m, v)


def _adamw_layers(w, g0, g1, m, v, c_idx, name):
    _, _, rows, cols = w.shape
    tr = _add_tile(rows, cols)

    def body(c_ref, w_ref, h0_ref, o0_ref, h1_ref, o1_ref, m_ref, v_ref, g_out, d_out, m_out, v_out):
        mine = pl.program_id(1) == c_ref[0]
        g = jnp.where(pl.program_id(0) == 0, jnp.where(mine, h0_ref[...], o0_ref[...]),
                      jnp.where(mine, h1_ref[...], o1_ref[...]))
        g_out[0, 0] = g
        d_out[0, 0], m_out[0, 0], v_out[0, 0] = _adamw_math(w_ref[0, 0], g, m_ref[0, 0], v_ref[0, 0])

    full = BS((1, 1, tr, cols), lambda l, hf, i, c: (l, hf, i, 0))
    half = BS((tr, cols), lambda l, hf, i, c: (i, 0))
    return pl.pallas_call(
        body, name=name,
        grid_spec=pltpu.PrefetchScalarGridSpec(
            num_scalar_prefetch=1, grid=(2, 2, rows // tr), in_specs=[full, half, half, half, half, full, full],
            out_specs=[full] * 4),
        out_shape=[SDS(w.shape, F32)] * 4,
        compiler_params=_cparams(("parallel", "parallel", "parallel")))(c_idx, w, *g0, *g1, m, v)


def _sum_devices(sa, name):
    def body(sa_ref, g_out):
        g = sa_ref[0]
        for d in range(1, N_DEV):
            g = g + sa_ref[d]
        g_out[...] = g

    return pl.pallas_call(body, name=name, out_shape=SDS(sa.shape[1:], F32),
                          compiler_params=pltpu.CompilerParams(vmem_limit_bytes=VMEM_LIMIT))(sa)


def _adamw_small(ws, gs, ms, vs, name):
    k = len(ws)

    def body(*refs):
        ins, outs = refs[:4 * k], refs[4 * k:]
        for i in range(k):
            d, m, v = _adamw_math(ins[i][...], ins[k + i][...], ins[2 * k + i][...], ins[3 * k + i][...])
            outs[i][...], outs[k + i][...], outs[2 * k + i][...] = d, m, v

    return pl.pallas_call(body, name=name, out_shape=[SDS(w.shape, F32) for w in ws] * 3,
                          compiler_params=pltpu.CompilerParams(vmem_limit_bytes=VMEM_LIMIT))(*ws, *gs, *ms, *vs)


HALVED = {"w_in": (2, 1024, 1450), "mla_w_q_up": (2, 256, 192), "mla_w_kv_up": (2, 128, 256),
          "w_branch": (2, 1536, 256), "w_o": (2, 256, 1024), "ffn_w_up": (2, 1024, 1408),
          "ffn_w_down": (2, 704, 1024), "ffn_conv_w": (2, 3, 1408), "meta_tokens": (2, 8, 256)}
SMALL_SHAPE = {"norm1_g": (2, 1024), "fox_forget_b": (2, 8), "fox_q_g": (2, 64), "fox_k_g": (2, 64),
               "mla_q_a_g": (2, 256), "mla_kv_a_g": (2, 128), "mla_q_g": (2, 96), "mla_k_g": (2, 96),
               "swa_q_g": (2, 64), "swa_k_g": (2, 64), "swa_sinks": (2, 8), "norm2_g": (2, 1024),
               "ffn_conv_b": (2, 5632)}
SMALL_ROWS = _pack_rows([SMALL_SHAPE[k] for k in SMALL] + [(1,)], SUBLANES)


def kernel(x, meta_tokens, norm1_g, w_in, fox_forget_b, fox_q_g, fox_k_g, mla_q_a_g, mla_w_q_up, mla_kv_a_g, mla_w_kv_up, mla_q_g, mla_k_g, swa_q_g, swa_k_g, swa_sinks, w_branch, w_o, norm2_g, ffn_w_up, ffn_conv_w, ffn_conv_b, ffn_w_down, loss_target, m_meta_tokens, m_norm1_g, m_w_in, m_fox_forget_b, m_fox_q_g, m_fox_k_g, m_mla_q_a_g, m_mla_w_q_up, m_mla_kv_a_g, m_mla_w_kv_up, m_mla_q_g, m_mla_k_g, m_swa_q_g, m_swa_k_g, m_swa_sinks, m_w_branch, m_w_o, m_norm2_g, m_ffn_w_up, m_ffn_conv_w, m_ffn_conv_b, m_ffn_w_down, v_meta_tokens, v_norm1_g, v_w_in, v_fox_forget_b, v_fox_q_g, v_fox_k_g, v_mla_q_a_g, v_mla_w_q_up, v_mla_kv_a_g, v_mla_w_kv_up, v_mla_q_g, v_mla_k_g, v_swa_q_g, v_swa_k_g, v_swa_sinks, v_w_branch, v_w_o, v_norm2_g, v_ffn_w_up, v_ffn_conv_w, v_ffn_conv_b, v_ffn_w_down):
    w = dict(meta_tokens=meta_tokens, norm1_g=norm1_g, w_in=w_in, fox_forget_b=fox_forget_b, fox_q_g=fox_q_g,
             fox_k_g=fox_k_g, mla_q_a_g=mla_q_a_g, mla_w_q_up=mla_w_q_up, mla_kv_a_g=mla_kv_a_g,
             mla_w_kv_up=mla_w_kv_up, mla_q_g=mla_q_g, mla_k_g=mla_k_g, swa_q_g=swa_q_g, swa_k_g=swa_k_g,
             swa_sinks=swa_sinks, w_branch=w_branch, w_o=w_o, norm2_g=norm2_g, ffn_w_up=ffn_w_up,
             ffn_conv_w=ffn_conv_w, ffn_conv_b=ffn_conv_b, ffn_w_down=ffn_w_down)
    m = dict(meta_tokens=m_meta_tokens, norm1_g=m_norm1_g, w_in=m_w_in, fox_forget_b=m_fox_forget_b,
             fox_q_g=m_fox_q_g, fox_k_g=m_fox_k_g, mla_q_a_g=m_mla_q_a_g, mla_w_q_up=m_mla_w_q_up,
             mla_kv_a_g=m_mla_kv_a_g, mla_w_kv_up=m_mla_w_kv_up, mla_q_g=m_mla_q_g, mla_k_g=m_mla_k_g,
             swa_q_g=m_swa_q_g, swa_k_g=m_swa_k_g, swa_sinks=m_swa_sinks, w_branch=m_w_branch, w_o=m_w_o,
             norm2_g=m_norm2_g, ffn_w_up=m_ffn_w_up, ffn_conv_w=m_ffn_conv_w, ffn_conv_b=m_ffn_conv_b,
             ffn_w_down=m_ffn_w_down)
    v = dict(meta_tokens=v_meta_tokens, norm1_g=v_norm1_g, w_in=v_w_in, fox_forget_b=v_fox_forget_b,
             fox_q_g=v_fox_q_g, fox_k_g=v_fox_k_g, mla_q_a_g=v_mla_q_a_g, mla_w_q_up=v_mla_w_q_up,
             mla_kv_a_g=v_mla_kv_a_g, mla_w_kv_up=v_mla_w_kv_up, mla_q_g=v_mla_q_g, mla_k_g=v_mla_k_g,
             swa_q_g=v_swa_q_g, swa_k_g=v_swa_k_g, swa_sinks=v_swa_sinks, w_branch=v_w_branch, w_o=v_w_o,
             norm2_g=v_norm2_g, ffn_w_up=v_ffn_w_up, ffn_conv_w=v_ffn_conv_w, ffn_conv_b=v_ffn_conv_b,
             ffn_w_down=v_ffn_w_down)
    xi, yi, ci = _me()
    c_idx = ci.astype(jnp.int32).reshape(1)
    j_idx = (2 * xi + yi).astype(jnp.int32).reshape(1)

    sh_names = BIG + FINE
    local = {k: (w[k].astype(BF16) if k in BIG else w[k]).reshape(HALVED[k]) for k in sh_names}
    late = [local[k] for k in LAYERED]
    meta_g, early = _gather_early(local["meta_tokens"], [local["w_in"]], "gather_early")
    meta = jnp.concatenate([meta_g[i].reshape(SHARD_SHAPE["meta_tokens"]) for i in range(N_CHIPS)], axis=1)
    w0 = _assemble_layer(_layer_parts(("w_in",), early))
    small = {k: w[k] for k in SMALL}

    loss, g_x, g_eps, g_red, g_cw, g_meta, g_small = _local_step(x[0], loss_target[0], w0, late, meta, small)
    g_after0, g_l1 = dict(zip(AFTER_MLA, g_red[0])), dict(zip(BIG, g_red[1]))

    quarter = {k: (2, HALVED[k][1] // 2, HALVED[k][2]) for k in BIG}
    last = BEFORE_MLA + FINE
    gs = [g_eps[k].reshape((N_CHIPS,) + quarter[k]) for k in BEFORE_MLA]
    for k, g in (("meta_tokens", g_meta), ("ffn_conv_w", g_cw)):
        gs.append(jnp.stack(jnp.split(g, N_CHIPS, axis=SHARD_AXIS[k])).reshape((N_CHIPS,) + HALVED[k]))
    spack = _pack([g_small[k] for k in SMALL] + [loss.reshape(1)], SMALL_ROWS, F32)
    r1 = _pair_exchange(gs, "grads_pair_exchange")
    s1 = [_pair_add(g, r, c_idx, "grads_pair_add_" + k) for g, r, k in zip(gs, r1, last)]
    r2, sa = _chip_exchange([s[1] for s in s1], spack, "grads_chip_exchange")
    gh = dict(zip(last, [_chip_add(s[0], r, j_idx, "grads_chip_add_" + k) for s, r, k in zip(s1, r2, last)]))
    go = dict(zip(last, _half_exchange([gh[k] for k in last], "grads_half_exchange")))

    grads, deltas, new_m, new_v = {}, {}, {}, {}
    for k in sh_names:
        if k in BIG:
            shp = (2,) + quarter[k]
            wk, mk, vk = w[k].reshape(shp), m[k].reshape(shp), v[k].reshape(shp)
            g0 = (gh[k], go[k]) if k in BEFORE_MLA else g_after0[k]
            outs = _adamw_layers(wk, g0, g_l1[k], mk, vk, c_idx, "adamw_" + k)
        else:
            outs = _adamw(w[k].reshape(HALVED[k]), gh[k], go[k], m[k].reshape(HALVED[k]), v[k].reshape(HALVED[k]),
                          c_idx, "adamw_" + k)
        for dst, o in zip((grads, deltas, new_m, new_v), outs):
            dst[k] = o.reshape(SHARD_SHAPE[k])
    sm_shapes = [SMALL_SHAPE[k] for k in SMALL] + [(1,)]
    g_sum = _unpack(_sum_devices(sa, "sum_small"), sm_shapes)
    res = _adamw_small([w[k] for k in SMALL], g_sum[:-1], [m[k] for k in SMALL], [v[k] for k in SMALL], "adamw_small")
    ns = len(SMALL)
    grads.update(zip(SMALL, g_sum[:-1]))
    for dst, vals in zip((deltas, new_m, new_v), (res[:ns], res[ns:2 * ns], res[2 * ns:])):
        dst.update(zip(SMALL, vals))
    total_loss = g_sum[-1][0]
    return (total_loss, g_x[None], *[grads[k] for k in WEIGHTS], *[deltas[k] for k in WEIGHTS],
            *[new_m[k] for k in WEIGHTS], *[new_v[k] for k in WEIGHTS])
```

```python
import jax
import jax.numpy as jnp
from jax import lax
from jax.experimental import pallas as pl
from jax.experimental.pallas import tpu as pltpu

F32 = jnp.float32
BF16 = jnp.bfloat16
SDS = jax.ShapeDtypeStruct
BS = pl.BlockSpec

D_MODEL = 1024
DEPTH = 2
N_META = 16
EPS = 1e-6
HEADS = 8
HEAD_DIM = 64
MLA_Q_RANK = 256
MLA_KV_RANK = 128
MLA_NOPE = 64
MLA_ROPE = 32
MLA_QK = MLA_NOPE + MLA_ROPE
ROPE_THETA = 10000.0
SWA_KV_HEADS = 2
WINDOW = 128
D_FF = 2816
IN_PAD = 6144
N_CHIPS = 4
N_DEV = 8

ADAM_LR = 0.001
ADAM_B1 = 0.9
ADAM_B2 = 0.999
ADAM_EPS = 1e-08
ADAM_WD = 0.01
ADAM_STEP = 10

LANES = 128
SUBLANES = 8
ROW_PAD = 128
CAUSAL_TILE = 384
NEG = -1e30
VMEM_LIMIT = 56 * 1024 * 1024

O_FQ, O_FK, O_FV, O_FF = 0, 512, 1024, 1536
O_CQ, O_CKV, O_KR = 1664, 1920, 2048
O_SQ, O_SK, O_SV, O_G = 2176, 2688, 2816, 2944

SHARD_AXIS = {"meta_tokens": 1, "w_in": 2, "mla_w_q_up": 2, "mla_w_kv_up": 2, "w_branch": 3, "w_o": 1,
              "ffn_w_up": 2, "ffn_conv_w": 2, "ffn_w_down": 1}
SHARD_SHAPE = {"meta_tokens": (16, 256), "w_in": (2, 1024, 1450), "mla_w_q_up": (2, 256, 192),
               "mla_w_kv_up": (2, 128, 256), "w_branch": (2, 3, 512, 256), "w_o": (2, 256, 1024),
               "ffn_w_up": (2, 1024, 1408), "ffn_conv_w": (2, 3, 1408), "ffn_w_down": (2, 704, 1024)}
BIG = ("w_in", "mla_w_q_up", "mla_w_kv_up", "w_branch", "w_o", "ffn_w_up", "ffn_w_down")
FINE = ("meta_tokens", "ffn_conv_w")
SMALL = ("norm1_g", "fox_forget_b", "fox_q_g", "fox_k_g", "mla_q_a_g", "mla_kv_a_g", "mla_q_g", "mla_k_g",
         "swa_q_g", "swa_k_g", "swa_sinks", "norm2_g", "ffn_conv_b")
WEIGHTS = ("meta_tokens", "norm1_g", "w_in", "fox_forget_b", "fox_q_g", "fox_k_g", "mla_q_a_g", "mla_w_q_up",
           "mla_kv_a_g", "mla_w_kv_up", "mla_q_g", "mla_k_g", "swa_q_g", "swa_k_g", "swa_sinks", "w_branch", "w_o",
           "norm2_g", "ffn_w_up", "ffn_conv_w", "ffn_conv_b", "ffn_w_down")


def _cparams(sem):
    return pltpu.CompilerParams(dimension_semantics=sem, vmem_limit_bytes=VMEM_LIMIT)


def _div_tile(n, cap, mult=SUBLANES):
    best = None
    for t in range(mult, min(n, cap) + 1, mult):
        if n % t == 0:
            best = t
    return best if best is not None else n


def _rows_tile(n, width, budget=2 << 20):
    return _div_tile(n, max(SUBLANES, budget // (4 * max(width, LANES))))


def _op(fwd, bwd):
    @jax.custom_vjp
    def op(*args):
        return fwd(*args)[0]
    op.defvjp(fwd, bwd)
    return op


def _rotate(v, cos, sin):
    lane = lax.broadcasted_iota(jnp.int32, v.shape, 1)
    rot = jnp.where(lane < MLA_NOPE + MLA_ROPE // 2, -pltpu.roll(v, LANES - MLA_ROPE // 2, 1),
                    pltpu.roll(v, MLA_ROPE // 2, 1))
    return v * cos + rot * sin


def _rms_fwd_call(x, g, denom, name, rot=None):
    n, c = x.shape
    tr = _rows_tile(n if rot is None else rot[0].shape[0], c)
    nt = None if rot is None else rot[0].shape[0] // tr

    def body(x_ref, g_ref, *rest):
        y_ref = rest[-1]
        xv = x_ref[...]
        ms = jnp.sum(xv * xv, axis=-1, keepdims=True) * (1.0 / denom)
        y = xv * lax.rsqrt(ms + EPS) * g_ref[...]
        y_ref[...] = y if rot is None else _rotate(y, rest[0][...], rest[1][...])

    ins, args = [BS((tr, c), lambda i: (i, 0)), BS((1, c), lambda i: (0, 0))], [x, g]
    if rot is not None:
        ins += [BS((tr, c), lambda i: (i % nt, 0))] * 2
        args += list(rot)
    return pl.pallas_call(
        body, name=name, grid=(n // tr,), in_specs=ins,
        out_specs=BS((tr, c), lambda i: (i, 0)), out_shape=SDS((n, c), F32),
        compiler_params=_cparams(("parallel",)))(*args)


def _rms_bwd_call(x, g, dy, denom, name, rot=None):
    n, c = x.shape
    tr = _rows_tile(n if rot is None else rot[0].shape[0], c)
    nt = None if rot is None else rot[0].shape[0] // tr

    def body(x_ref, g_ref, dy_ref, *rest):
        dx_ref, dg_ref = rest[-2:]
        xv = x_ref[...]
        dy = dy_ref[...]
        if rot is not None:
            dy = _rotate(dy, rest[0][...], -rest[1][...])
        ms = jnp.sum(xv * xv, axis=-1, keepdims=True) * (1.0 / denom)
        r = lax.rsqrt(ms + EPS)
        xh = xv * r
        dxh = dy * g_ref[...]
        dx_ref[...] = r * (dxh - xh * (jnp.sum(dxh * xh, axis=-1, keepdims=True) * (1.0 / denom)))

        @pl.when(pl.program_id(0) == 0)
        def _():
            dg_ref[...] = jnp.zeros_like(dg_ref)

        dg_ref[...] += jnp.sum(dy * xh, axis=0, keepdims=True)

    ins = [BS((tr, c), lambda i: (i, 0)), BS((1, c), lambda i: (0, 0)), BS((tr, c), lambda i: (i, 0))]
    args = [x, g, dy]
    if rot is not None:
        ins += [BS((tr, c), lambda i: (i % nt, 0))] * 2
        args += list(rot)
    return pl.pallas_call(
        body, name=name, grid=(n // tr,), in_specs=ins,
        out_specs=[BS((tr, c), lambda i: (i, 0)), BS((1, c), lambda i: (0, 0))],
        out_shape=[SDS((n, c), F32), SDS((1, c), F32)],
        compiler_params=_cparams(("arbitrary",)))(*args)


def rms_norm(x, g, denom, name):
    def fwd(x, g):
        return _rms_fwd_call(x, g, denom, name + "_f"), (x, g)

    def bwd(res, dy):
        return tuple(_rms_bwd_call(res[0], res[1], dy, denom, name + "_b"))

    return _op(fwd, bwd)(x, g)


def rms_norm_rope(x, g, cos, sin, denom, name):
    def fwd(x, g, cos, sin):
        return _rms_fwd_call(x, g, denom, name + "_f", (cos, sin)), (x, g, cos, sin)

    def bwd(res, dy):
        x, g, cos, sin = res
        dx, dg = _rms_bwd_call(x, g, dy, denom, name + "_b", (cos, sin))
        return dx, dg, jnp.zeros_like(cos), jnp.zeros_like(sin)

    return _op(fwd, bwd)(x, g, cos, sin)


def _mm_call(a, b, mode, res, name):
    if mode == "nn":
        (m, kc), n = a.shape, b.shape[1]
    elif mode == "nt":
        (m, kc), n = a.shape, b.shape[0]
    else:
        (kc, m), n = a.shape, b.shape[1]
    if mode == "tn":
        tk = _div_tile(kc, 528)
        tm = _div_tile(m, 1408, LANES)
        tn = _div_tile(n, 2048, LANES)
    else:
        tk = kc if kc <= 2816 else _div_tile(kc, 1024, LANES)
        tm = _div_tile(m, max(LANES, (9 << 19) // (4 * tk)))
        tn = _div_tile(n, 1408 if mode == "nt" else 512, LANES)
    nk = kc // tk
    dims = {"nn": (((1,), (0,)), ((), ())), "nt": (((1,), (1,)), ((), ())), "tn": (((0,), (0,)), ((), ()))}[mode]

    def body(*refs):
        if res is None:
            a_ref, b_ref, o_ref, acc_ref = refs
            r_ref = None
        else:
            a_ref, b_ref, r_ref, o_ref, acc_ref = refs
        k = pl.program_id(2)

        @pl.when(k == 0)
        def _():
            acc_ref[...] = jnp.zeros_like(acc_ref)

        acc_ref[...] += lax.dot_general(a_ref[...].astype(BF16), b_ref[...].astype(BF16), dims,
                                        preferred_element_type=F32)

        @pl.when(k == nk - 1)
        def _():
            if r_ref is None:
                o_ref[...] = acc_ref[...]
            else:
                o_ref[...] = r_ref[...] + acc_ref[...]

    a_spec = BS((tk, tm), lambda i, j, k: (k, i)) if mode == "tn" else BS((tm, tk), lambda i, j, k: (i, k))
    b_spec = BS((tn, tk), lambda i, j, k: (j, k)) if mode == "nt" else BS((tk, tn), lambda i, j, k: (k, j))
    o_spec = BS((tm, tn), lambda i, j, k: (i, j))
    ins, args = [a_spec, b_spec], [a, b]
    if res is not None:
        ins.append(o_spec)
        args.append(res)
    return pl.pallas_call(
        body, name=name, grid=(m // tm, n // tn, nk), in_specs=ins, out_specs=o_spec,
        out_shape=SDS((m, n), F32), scratch_shapes=[pltpu.VMEM((tm, tn), F32)],
        compiler_params=_cparams(("parallel", "parallel", "arbitrary")))(*args)


def linear(a, w, eps, name, res=None):
    if res is None:
        def fwd(a, w, eps):
            return _mm_call(a, w, "nn", None, name + "_f"), (a, w)

        def bwd(r, dc):
            a, w = r
            return (_mm_call(dc, w, "nt", None, name + "_da"), jnp.zeros_like(w),
                    _mm_call(a, dc, "tn", None, name + "_dw"))

        return _op(fwd, bwd)(a, w, eps)

    def fwd_r(a, w, eps, res):
        return _mm_call(a, w, "nn", res, name + "_f"), (a, w)

    def bwd_r(r, dc):
        a, w = r
        return (_mm_call(dc, w, "nt", None, name + "_da"), jnp.zeros_like(w),
                _mm_call(a, dc, "tn", None, name + "_dw"), dc)

    return _op(fwd_r, bwd_r)(a, w, eps, res)


def _gate_tile(n):
    return _div_tile(n, CAUSAL_TILE, LANES)


def _tri_dot(v, upper):
    ct = v.shape[1]
    r = lax.broadcasted_iota(jnp.int32, (ct, ct), 0)
    c = lax.broadcasted_iota(jnp.int32, (ct, ct), 1)
    tri = jnp.where((r <= c) if upper else (r >= c), 1.0, 0.0).astype(F32)
    return jnp.dot(v, tri, preferred_element_type=F32, precision=lax.Precision.HIGHEST)


def _gate_fwd_call(z, b, name):
    h, n = z.shape
    CT = _gate_tile(n)

    def body(z_ref, b_ref, c_ref, carry):
        @pl.when(pl.program_id(0) == 0)
        def _():
            carry[...] = jnp.zeros_like(carry)

        x = z_ref[...] + b_ref[...]
        ls = jnp.minimum(x, 0.0) - jnp.log(1.0 + jnp.exp(-jnp.abs(x)))
        c_ref[...] = _tri_dot(ls, True) + carry[...]
        carry[...] += jnp.sum(ls, axis=1, keepdims=True)

    return pl.pallas_call(
        body, name=name, grid=(n // CT,),
        in_specs=[BS((h, CT), lambda j: (0, j)), BS((h, 1), lambda j: (0, 0))],
        out_specs=BS((h, CT), lambda j: (0, j)), out_shape=SDS((h, n), F32),
        scratch_shapes=[pltpu.VMEM((h, 1), F32)],
        compiler_params=_cparams(("arbitrary",)))(z, b)


def _gate_bwd_call(z, b, dc, name):
    h, n = z.shape
    CT = _gate_tile(n)
    nt = n // CT

    def body(z_ref, b_ref, dc_ref, dz_ref, db_ref, carry):
        @pl.when(pl.program_id(0) == 0)
        def _():
            carry[...] = jnp.zeros_like(carry)
            db_ref[...] = jnp.zeros_like(db_ref)

        dcv = dc_ref[...]
        dls = _tri_dot(dcv, False) + carry[...]
        carry[...] += jnp.sum(dcv, axis=1, keepdims=True)
        x = z_ref[...] + b_ref[...]
        e = jnp.exp(-jnp.abs(x))
        dz = dls * jnp.where(x >= 0, e / (1.0 + e), 1.0 / (1.0 + e))
        dz_ref[...] = dz
        db_ref[...] += jnp.sum(dz, axis=1, keepdims=True)

    rev = lambda j: (0, nt - 1 - j)
    return pl.pallas_call(
        body, name=name, grid=(nt,),
        in_specs=[BS((h, CT), rev), BS((h, 1), lambda j: (0, 0)), BS((h, CT), rev)],
        out_specs=[BS((h, CT), rev), BS((h, 1), lambda j: (0, 0))],
        out_shape=[SDS((h, n), F32), SDS((h, 1), F32)],
        scratch_shapes=[pltpu.VMEM((h, 1), F32)],
        compiler_params=_cparams(("arbitrary",)))(z, b, dc)


def forget_cumsum(z, b, name):
    def fwd(z, b):
        return _gate_fwd_call(z, b, name + "_f"), (z, b)

    def bwd(res, dc):
        return tuple(_gate_bwd_call(res[0], res[1], dc, name + "_b"))

    return _op(fwd, bwd)(z, b)


NT_DIMS = (((1,), (1,)), ((), ()))
TN_DIMS = (((0,), (0,)), ((), ()))
HEADS_PER_STEP = 2


def _causal_tile(n):
    return CAUSAL_TILE if n % CAUSAL_TILE == 0 else ROW_PAD


def _causal_fwd_call(q, k, v, ck_r, fox, scale, name, late=None, late_layer=1):
    h, n, dk = q.shape
    dv = v.shape[2]
    t = _causal_tile(n)
    nq = n // t
    hb = HEADS_PER_STEP
    nl = 0 if late is None else len(late)
    n_in = 3 + int(fox) + nl

    def body(*refs):
        q_ref, k_ref, v_ref = refs[:3]
        ck_ref = refs[3] if fox else None
        o_ref, lse_ref = refs[n_in:n_in + 2]
        m_scr, l_scr, acc_scr = refs[n_in + 2 + nl:n_in + 5 + nl]
        qi = pl.program_id(1)
        if nl:
            start, forward, drain, receive = _layer_gather(refs[n_in - nl:n_in], refs[n_in + 2:n_in + 2 + nl],
                                                           refs[-2], refs[-1], late_layer, 0)
            hp, core = pl.program_id(0), lax.axis_index("c")
            last = (hp == h // hb - 1) & (qi == nq - 1)
            pl.when((hp == 0) & (qi == 0) & (core == late_layer))(start)
            pl.when((hp == h // hb // 2) & (qi == 0) & (core == late_layer))(forward)
            pl.when(last & (core == late_layer))(drain)
            pl.when(last & (core == 1 - late_layer))(receive)
        qbs = [q_ref[e].astype(BF16) for e in range(hb)]
        m_scr[...] = jnp.full(m_scr.shape, NEG, F32)
        l_scr[...] = jnp.zeros_like(l_scr)
        acc_scr[...] = jnp.zeros_like(acc_scr)

        def process(j, masked):
            off = pl.multiple_of(j * t, t)
            if masked:
                rows = lax.broadcasted_iota(jnp.int32, (t, t), 0)
                cols = lax.broadcasted_iota(jnp.int32, (t, t), 1)
                valid = cols <= rows
            for e in range(hb):
                kb = k_ref[e, pl.ds(off, t), :].astype(BF16)
                vb = v_ref[e, pl.ds(off, t), :].astype(BF16)
                s = lax.dot_general(qbs[e], kb, NT_DIMS, preferred_element_type=F32) * scale
                if fox:
                    s = s - ck_ref[e, j]
                if masked:
                    s = jnp.where(valid, s, NEG)
                m_old = m_scr[e]
                m_new = jnp.maximum(m_old, jnp.max(s, axis=1, keepdims=True))
                alpha = jnp.exp(m_old - m_new)
                p = jnp.exp(s - jnp.tile(m_new, (1, t // LANES)))
                l_scr[e] = alpha * l_scr[e] + jnp.sum(p, axis=1, keepdims=True)
                acc_scr[e] = alpha[:, :dv] * acc_scr[e] + jnp.dot(p.astype(BF16), vb, preferred_element_type=F32)
                m_scr[e] = m_new

        def step(j, carry):
            process(j, False)
            return carry

        lax.fori_loop(0, qi, step, 0)
        process(qi, True)
        for e in range(hb):
            l = l_scr[e]
            o_ref[e] = acc_scr[e] / l[:, :dv]
            lse_ref[e, 0] = jnp.transpose(m_scr[e] + jnp.log(l))[0:1, :]

    ins = [BS((hb, t, dk), lambda a, b: (a, b, 0)), BS((hb, n, dk), lambda a, b: (a, 0, 0)),
           BS((hb, n, dv), lambda a, b: (a, 0, 0))]
    args = [q, k, v]
    if fox:
        ins.append(BS((hb, nq, 1, t), lambda a, b: (a, 0, 0, 0)))
        args.append(ck_r)
    outs = [BS((hb, t, dv), lambda a, b: (a, b, 0)), BS((hb, 1, 1, t), lambda a, b: (a, b, 0, 0))]
    oshape = [SDS((h, n, dv), F32), SDS((h, nq, 1, t), F32)]
    scratch = [pltpu.VMEM((hb, t, LANES), F32), pltpu.VMEM((hb, t, LANES), F32), pltpu.VMEM((hb, t, dv), F32)]
    if nl:
        ins += [ANY] * nl
        args += list(late)
        outs += [ANY] * nl
        oshape += [SDS((N_CHIPS,) + a.shape[1:], a.dtype) for a in late]
        scratch += [pltpu.SemaphoreType.DMA((6 * nl,)), pltpu.SemaphoreType.DMA((6 * nl,))]
    res = pl.pallas_call(
        body, name=name, grid=(h // hb, nq), in_specs=ins, out_specs=outs, out_shape=oshape, scratch_shapes=scratch,
        compiler_params=pltpu.CompilerParams(dimension_semantics=("arbitrary", "arbitrary"),
                                             vmem_limit_bytes=VMEM_LIMIT, has_side_effects=bool(nl)))(*args)
    return res[0], res[1], list(res[2:])


def _causal_bwd_call(q, k, v, do, o, lse_r, ck_r, fox, scale, name, side=None):
    h, n, dk = q.shape
    dv = v.shape[2]
    t = _causal_tile(n)
    nq = n // t
    hb = HEADS_PER_STEP
    ns = 0 if side is None else len(side)

    def body(*refs):
        it = iter(refs)
        q_ref, k_ref, v_ref, do_ref, o_ref, lse_ref = (next(it) for _ in range(6))
        ck_ref = next(it) if fox else None
        side_in = [next(it) for _ in range(ns)]
        dq_ref, dk_ref, dv_ref = next(it), next(it), next(it)
        dck_ref, dcq_ref = (next(it), next(it)) if fox else (None, None)
        side_out = [next(it) for _ in range(ns)]
        delta_scr, dk_scr, dv_scr = next(it), next(it), next(it)
        dck_scr = next(it) if fox else None
        kj = pl.program_id(1)
        if ns:
            send_sems, recv_sems = next(it), next(it)
            x, y, core = _me()
            chips = [(1 - x, y), (x, 1 - y), (1 - x, 1 - y)]
            copies = [_remote(side_in[p].at[2 * cx + cy], side_out[p].at[r], send_sems, recv_sems, 3 * p + r,
                              (cx, cy, core)) for p in range(ns) for r, (cx, cy) in enumerate(chips)]

            @pl.when((pl.program_id(0) == 0) & (kj == 0))
            def _():
                for cp in copies:
                    cp.start()

            @pl.when((pl.program_id(0) == h // hb - 1) & (kj == nq - 1))
            def _():
                for cp in copies:
                    cp.wait()

        @pl.when(kj == 0)
        def _():
            dq_ref[...] = jnp.zeros_like(dq_ref)
            if fox:
                dcq_ref[...] = jnp.zeros_like(dcq_ref)
            ones = jnp.ones((SUBLANES, dv), F32)

            def fill(qi, carry):
                off = pl.multiple_of(qi * t, t)
                for e in range(hb):
                    prod = do_ref[e, pl.ds(off, t), :] * o_ref[e, pl.ds(off, t), :]
                    delta_scr[e, qi] = lax.dot_general(ones, prod, NT_DIMS, preferred_element_type=F32,
                                                       precision=lax.Precision.HIGHEST)[0:1, :]
                return carry

            lax.fori_loop(0, nq, fill, 0)

        kbs = [k_ref[e].astype(BF16) for e in range(hb)]
        vbs = [v_ref[e].astype(BF16) for e in range(hb)]
        dk_scr[...] = jnp.zeros_like(dk_scr)
        dv_scr[...] = jnp.zeros_like(dv_scr)
        if fox:
            dck_scr[...] = jnp.zeros_like(dck_scr)
            ckcs = [jnp.tile(jnp.transpose(jnp.broadcast_to(ck_ref[e, 0], (LANES, t))), (1, t // LANES))
                    for e in range(hb)]

        def process(qi, masked):
            off = pl.multiple_of(qi * t, t)
            if masked:
                krows = lax.broadcasted_iota(jnp.int32, (t, t), 0)
                qcols = lax.broadcasted_iota(jnp.int32, (t, t), 1)
                valid = krows <= qcols
            for e in range(hb):
                qb = q_ref[e, pl.ds(off, t), :].astype(BF16)
                dob = do_ref[e, pl.ds(off, t), :].astype(BF16)
                st = lax.dot_general(kbs[e], qb, NT_DIMS, preferred_element_type=F32) * scale
                if fox:
                    st = st - ckcs[e]
                pt = jnp.exp(st - lse_ref[e, qi])
                if masked:
                    pt = jnp.where(valid, pt, 0.0)
                dv_scr[e] += jnp.dot(pt.astype(BF16), dob, preferred_element_type=F32)
                dpt = lax.dot_general(vbs[e], dob, NT_DIMS, preferred_element_type=F32)
                dst = pt * (dpt - delta_scr[e, qi])
                if fox:
                    dck_scr[e] -= jnp.sum(dst, axis=1, keepdims=True)
                    dcq_ref[e, qi] += jnp.sum(dst, axis=0, keepdims=True)
                dsb = (dst * scale).astype(BF16)
                dk_scr[e] += jnp.dot(dsb, qb, preferred_element_type=F32)
                dq_ref[e, pl.ds(off, t), :] += lax.dot_general(dsb, kbs[e], TN_DIMS, preferred_element_type=F32)

        def step(qi, carry):
            process(qi, False)
            return carry

        process(kj, True)
        lax.fori_loop(kj + 1, nq, step, 0)
        dk_ref[...] = dk_scr[...]
        dv_ref[...] = dv_scr[...]
        if fox:
            for e in range(hb):
                dck_ref[e, 0] = jnp.transpose(jnp.broadcast_to(dck_scr[e], (t, LANES)))[0:1, :]

    whole = lambda a, b: (a, 0, 0)
    tile = lambda a, b: (a, b, 0)
    rowv = lambda a, b: (a, 0, 0, 0)
    rowt = lambda a, b: (a, b, 0, 0)
    ins = [BS((hb, n, dk), whole), BS((hb, t, dk), tile), BS((hb, t, dv), tile), BS((hb, n, dv), whole),
           BS((hb, n, dv), whole), BS((hb, nq, 1, t), rowv)]
    args = [q, k, v, do, o, lse_r]
    outs = [BS((hb, n, dk), whole), BS((hb, t, dk), tile), BS((hb, t, dv), tile)]
    oshape = [SDS((h, n, dk), F32), SDS((h, n, dk), F32), SDS((h, n, dv), F32)]
    scratch = [pltpu.VMEM((hb, nq, 1, t), F32), pltpu.VMEM((hb, t, dk), F32), pltpu.VMEM((hb, t, dv), F32)]
    if fox:
        ins.append(BS((hb, 1, 1, t), rowt))
        args.append(ck_r)
        outs += [BS((hb, 1, 1, t), rowt), BS((hb, nq, 1, t), rowv)]
        oshape += [SDS((h, nq, 1, t), F32), SDS((h, nq, 1, t), F32)]
        scratch.append(pltpu.VMEM((hb, t, 1), F32))
    if ns:
        ins += [ANY] * ns
        args += list(side)
        outs += [ANY] * ns
        oshape += [SDS((3,) + s.shape[1:], s.dtype) for s in side]
        scratch += [pltpu.SemaphoreType.DMA((3 * ns,)), pltpu.SemaphoreType.DMA((3 * ns,))]
    return pl.pallas_call(
        body, name=name, grid=(h // hb, nq), in_specs=ins, out_specs=outs, out_shape=oshape, scratch_shapes=scratch,
        compiler_params=pltpu.CompilerParams(dimension_semantics=("arbitrary", "arbitrary"),
                                             vmem_limit_bytes=VMEM_LIMIT, has_side_effects=bool(ns)))(*args)


def causal_attention(q, k, v, c, scale, name, late=None, late_layer=1, reduce=None, sinks=None):
    h, n, _ = q.shape
    t = _causal_tile(n)
    nq = n // t
    fox = c is not None

    def run_fwd(q, k, v, c, late, sinks):
        ck_r = c.reshape(h, nq, 1, t) if fox else None
        o, lse, got = _causal_fwd_call(q, k, v, ck_r, fox, scale, name + "_f", late, late_layer)
        out = (o,)
        if late is not None:
            out += (got,)
        if reduce is not None:
            out += ([jnp.zeros((N_CHIPS,) + SHARD_SHAPE[w][1:], F32) for w in reduce],)
        return (out if len(out) > 1 else o), (q, k, v, c, late, o, lse)

    def run_bwd(res, ct):
        q, k, v, c, late, o, lse = res
        ck_r = c.reshape(h, nq, 1, t) if fox else None
        dlate = None if late is None else [jnp.zeros_like(a) for a in late]
        if reduce is None:
            do = ct if late is None else ct[0]
            outs = _causal_bwd_call(q, k, v, do, o, lse, ck_r, fox, scale, name + "_b")
            return outs[0], outs[1], outs[2], ((outs[3] + outs[4]).reshape(h, n) if fox else None), dlate, None
        do, g1 = ct[0], ct[-1]
        xi, yi, ci = _me()
        c_idx = ci.astype(jnp.int32).reshape(1)
        j_idx = (2 * xi + yi).astype(jnp.int32).reshape(1)
        gs = [g.reshape((N_CHIPS,) + HALVED[w][1:]) for g, w in zip(g1, reduce)]
        outs = _causal_bwd_call(q, k, v, do, o, lse, ck_r, fox, scale, name + "_b", [g.astype(BF16) for g in gs])
        sc = [_chip_add(g, r, j_idx, name + "_chip_add_" + w).reshape(1, 2, HALVED[w][1] // 2, HALVED[w][2])
              for g, r, w in zip(gs, outs[-len(reduce):], reduce)]
        r1 = _pair_exchange(sc, name + "_pair_exchange")
        gh = [_pair_add(s, r, c_idx, name + "_pair_add_" + w)[0][0] for s, r, w in zip(sc, r1, reduce)]
        go = _half_exchange(gh, name + "_half_exchange")
        return (outs[0], outs[1], outs[2], ((outs[3] + outs[4]).reshape(h, n) if fox else None), dlate,
                list(zip(gh, go)))

    return _op(run_fwd, run_bwd)(q, k, v, c, late, sinks)


SWA_T = 128


def _swa_masks(qi):
    t = SWA_T
    r = lax.broadcasted_iota(jnp.int32, (t, 3 * t), 0)
    c = lax.broadcasted_iota(jnp.int32, (t, 3 * t), 1)
    seg0 = c < t
    seg1 = (c >= t) & (c < 2 * t)
    jp = jnp.maximum(qi - 1, 0)
    kpos = jnp.where(seg0, c, jnp.where(seg1, jp * t + c - t, qi * t + c - 2 * t))
    dist = qi * t + r - kpos
    band = (dist >= 0) & ((dist < WINDOW) | (kpos < N_META))
    valid = (seg0 & (kpos < N_META) & (qi >= 2)) | (jnp.logical_not(seg0) & band & (jnp.logical_not(seg1) | (qi >= 1)))
    return valid, dist.astype(F32)


def _swa_cat(ref, qi):
    t = SWA_T
    jp = jnp.maximum(qi - 1, 0)
    return jnp.concatenate([ref[0, 0:t, :], ref[0, pl.ds(pl.multiple_of(jp * t, t), t), :],
                            ref[0, pl.ds(pl.multiple_of(qi * t, t), t), :]], axis=0).astype(BF16)


def _swa_fwd_call(q, k, v, sinks, slopes, scale, name):
    hq, n, d = q.shape
    hkv = k.shape[0]
    g = hq // hkv
    t = SWA_T
    nq = n // t

    def body(q_ref, k_ref, v_ref, sink_ref, slope_ref, o_ref, lse_ref):
        grp = pl.program_id(0)
        qi = pl.program_id(1)
        valid, dist = _swa_masks(qi)
        kc = _swa_cat(k_ref, qi)
        vc = _swa_cat(v_ref, qi)
        qs = jnp.concatenate([q_ref[e] for e in range(g)], axis=0).astype(BF16)
        s_all = lax.dot_general(qs, kc, NT_DIMS, preferred_element_type=F32) * scale
        ps, ls, ms = [], [], []
        for e in range(g):
            hh = grp * g + e
            s = jnp.where(valid, s_all[e * t:(e + 1) * t] - slope_ref[hh] * dist, NEG)
            m = jnp.maximum(jnp.max(s, axis=1, keepdims=True), sink_ref[hh])
            p = jnp.exp(s - m)
            ls.append(jnp.sum(p, axis=1, keepdims=True) + jnp.exp(sink_ref[hh] - m))
            ms.append(m)
            ps.append(p.astype(BF16))
        acc = jnp.dot(jnp.concatenate(ps, axis=0), vc, preferred_element_type=F32)
        for e in range(g):
            o_ref[e] = acc[e * t:(e + 1) * t] / ls[e]
            lse_ref[e] = ms[e] + jnp.log(ls[e])

    return pl.pallas_call(
        body, name=name, grid=(hkv, nq),
        in_specs=[BS((g, t, d), lambda a, b: (a, b, 0)), BS((1, n, d), lambda a, b: (a, 0, 0)),
                  BS((1, n, d), lambda a, b: (a, 0, 0)), BS(memory_space=pltpu.SMEM), BS(memory_space=pltpu.SMEM)],
        out_specs=[BS((g, t, d), lambda a, b: (a, b, 0)), BS((g, t, 1), lambda a, b: (a, b, 0))],
        out_shape=[SDS((hq, n, d), F32), SDS((hq, n, 1), F32)],
        compiler_params=_cparams(("parallel", "parallel")))(q, k, v, sinks, slopes)


def _swa_bwd_call(q, k, v, o, lse, do, sinks, slopes, scale, name):
    hq, n, d = q.shape
    hkv = k.shape[0]
    g = hq // hkv
    t = SWA_T
    nq = n // t

    def body(q_ref, k_ref, v_ref, o_ref, lse_ref, do_ref, sink_ref, slope_ref, dq_ref, dk_ref, dv_ref, ds_ref):
        grp = pl.program_id(0)
        qi = pl.program_id(1)

        @pl.when(qi == 0)
        def _():
            dk_ref[...] = jnp.zeros_like(dk_ref)
            dv_ref[...] = jnp.zeros_like(dv_ref)
            ds_ref[...] = jnp.zeros_like(ds_ref)

        valid, dist = _swa_masks(qi)
        kc = _swa_cat(k_ref, qi)
        vc = _swa_cat(v_ref, qi)
        qs = jnp.concatenate([q_ref[e] for e in range(g)], axis=0).astype(BF16)
        dos = jnp.concatenate([do_ref[e] for e in range(g)], axis=0).astype(BF16)
        s_all = lax.dot_general(qs, kc, NT_DIMS, preferred_element_type=F32) * scale
        dp_all = lax.dot_general(dos, vc, NT_DIMS, preferred_element_type=F32)
        ps, dss = [], []
        for e in range(g):
            hh = grp * g + e
            lse_e = lse_ref[e]
            delta = jnp.sum(do_ref[e] * o_ref[e], axis=1, keepdims=True)
            s = s_all[e * t:(e + 1) * t] - slope_ref[hh] * dist
            p = jnp.where(valid, jnp.exp(s - lse_e), 0.0)
            ds = p * (dp_all[e * t:(e + 1) * t] - delta)
            ps.append(p.astype(BF16))
            dss.append((ds * scale).astype(BF16))
            ds_ref[e] += -jnp.sum(jnp.exp(sink_ref[hh] - lse_e) * delta)
        p_st = jnp.concatenate(ps, axis=0)
        ds_st = jnp.concatenate(dss, axis=0)
        dq = jnp.dot(ds_st, kc, preferred_element_type=F32)
        for e in range(g):
            dq_ref[e] = dq[e * t:(e + 1) * t]
        dkc = lax.dot_general(ds_st, qs, TN_DIMS, preferred_element_type=F32)
        dvc = lax.dot_general(p_st, dos, TN_DIMS, preferred_element_type=F32)
        jp = jnp.maximum(qi - 1, 0)
        for seg, off in enumerate((0, pl.multiple_of(jp * t, t), pl.multiple_of(qi * t, t))):
            dk_ref[0, pl.ds(off, t), :] += dkc[seg * t:(seg + 1) * t]
            dv_ref[0, pl.ds(off, t), :] += dvc[seg * t:(seg + 1) * t]

    tile = lambda a, b: (a, b, 0)
    whole = lambda a, b: (a, 0, 0)
    return pl.pallas_call(
        body, name=name, grid=(hkv, nq),
        in_specs=[BS((g, t, d), tile), BS((1, n, d), whole), BS((1, n, d), whole), BS((g, t, d), tile),
                  BS((g, t, 1), tile), BS((g, t, d), tile), BS(memory_space=pltpu.SMEM), BS(memory_space=pltpu.SMEM)],
        out_specs=[BS((g, t, d), tile), BS((1, n, d), whole), BS((1, n, d), whole), BS((g, 1, LANES), whole)],
        out_shape=[SDS((hq, n, d), F32), SDS((hkv, n, d), F32), SDS((hkv, n, d), F32), SDS((hq, 1, LANES), F32)],
        compiler_params=_cparams(("arbitrary", "arbitrary")))(q, k, v, o, lse, do, sinks, slopes)


def window_attention(q, k, v, sinks, slopes, scale, name):
    def run_fwd(q, k, v, sinks, slopes):
        o, lse = _swa_fwd_call(q, k, v, sinks, slopes, scale, name + "_f")
        return o, (q, k, v, sinks, slopes, o, lse)

    def run_bwd(res, do):
        q, k, v, sinks, slopes, o, lse = res
        dq, dk, dv, ds = _swa_bwd_call(q, k, v, o, lse, do, sinks, slopes, scale, name + "_b")
        return dq, dk, dv, ds[:, 0, 0], jnp.zeros_like(slopes)

    return _op(run_fwd, run_bwd)(q, k, v, sinks, slopes)


def _sigmoid(x):
    return 1.0 / (1.0 + jnp.exp(-x))


def _merge_fwd_call(gs, ys, name):
    n, c = ys[0].shape
    tr = _rows_tile(n, c, 1 << 20)

    def body(g0, g1, g2, y0, y1, y2, m_ref):
        m_ref[...] = (_sigmoid(g0[...]) * y0[...] + _sigmoid(g1[...]) * y1[...]) + _sigmoid(g2[...]) * y2[...]

    spec = BS((tr, c), lambda i: (i, 0))
    return pl.pallas_call(
        body, name=name, grid=(n // tr,), in_specs=[spec] * 6, out_specs=spec, out_shape=SDS((n, c), F32),
        compiler_params=_cparams(("parallel",)))(*gs, *ys)


def _merge_bwd_call(gs, ys, dm, name):
    n, c = ys[0].shape
    tr = _rows_tile(n, c, 1 << 20)

    def body(g0, g1, g2, y0, y1, y2, dm_ref, dg0, dg1, dg2, dy0, dy1, dy2):
        d = dm_ref[...]
        for g, y, dg, dy in ((g0, y0, dg0, dy0), (g1, y1, dg1, dy1), (g2, y2, dg2, dy2)):
            s = _sigmoid(g[...])
            dy[...] = d * s
            dg[...] = d * y[...] * (s * (1.0 - s))

    spec = BS((tr, c), lambda i: (i, 0))
    return pl.pallas_call(
        body, name=name, grid=(n // tr,), in_specs=[spec] * 7, out_specs=[spec] * 6,
        out_shape=[SDS((n, c), F32)] * 6, compiler_params=_cparams(("parallel",)))(*gs, *ys, dm)


def gated_merge(gs, ys, name):
    def fwd(gs, ys):
        return _merge_fwd_call(gs, ys, name + "_f"), (gs, ys)

    def bwd(res, dm):
        out = _merge_bwd_call(res[0], res[1], dm, name + "_b")
        return tuple(out[:3]), tuple(out[3:])

    return _op(fwd, bwd)(tuple(gs), tuple(ys))


CONV_TR = 264
CONV_TC = 1408


def _conv_tiles(n, f):
    tr = CONV_TR if n % CONV_TR == 0 else _div_tile(n, CONV_TR)
    tc = CONV_TC if f % CONV_TC == 0 else f
    return tr, tc


def _shift_down(cur, halo, first, tr):
    halo = jnp.where(first, 0.0, halo)
    row = lax.broadcasted_iota(jnp.int32, cur.shape, 0)
    h7, h6 = halo[7:8, :], halo[6:7, :]
    u1 = jnp.where(row == 0, h7, pltpu.roll(cur, 1, 0))
    u2 = jnp.where(row == 0, h6, jnp.where(row == 1, h7, pltpu.roll(cur, 2, 0)))
    return u1, u2


def _conv_lin(cur, u1, u2, w_ref, b_ref):
    return ((b_ref[...] + w_ref[0:1, :] * u2) + w_ref[1:2, :] * u1) + w_ref[2:3, :] * cur


def _conv_in_specs(tr, tc, nj):
    sub = tr // SUBLANES
    prev = lambda j, i: (jnp.maximum(i * sub - 1, 0), j)
    prev_v = lambda j, i: (jnp.maximum(i * sub - 1, 0), j + nj)
    return [BS((tr, tc), lambda j, i: (i, j)), BS((SUBLANES, tc), prev),
            BS((tr, tc), lambda j, i: (i, j + nj)), BS((SUBLANES, tc), prev_v),
            BS((3, tc), lambda j, i: (0, j)), BS((3, tc), lambda j, i: (0, j + nj)),
            BS((1, tc), lambda j, i: (0, j)), BS((1, tc), lambda j, i: (0, j + nj))]


def _conv_fwd_call(u, cw, cb, name):
    n, f2 = u.shape
    f = f2 // 2
    tr, tc = _conv_tiles(n, f)
    nj = f // tc

    def body(ug, ugh, uv, uvh, wg, wv, bg, bv, a_ref):
        first = pl.program_id(1) == 0
        g1, g2 = _shift_down(ug[...], ugh[...], first, tr)
        v1, v2 = _shift_down(uv[...], uvh[...], first, tr)
        cg = _conv_lin(ug[...], g1, g2, wg, bg)
        cv = _conv_lin(uv[...], v1, v2, wv, bv)
        a_ref[...] = cg * _sigmoid(cg) * cv

    return pl.pallas_call(
        body, name=name, grid=(nj, n // tr), in_specs=_conv_in_specs(tr, tc, nj),
        out_specs=BS((tr, tc), lambda j, i: (i, j)), out_shape=SDS((n, f), F32),
        compiler_params=_cparams(("parallel", "parallel")))(u, u, u, u, cw, cw, cb, cb)


def _conv_bwd_dc_call(u, cw, cb, da, name):
    n, f2 = u.shape
    f = f2 // 2
    tr, tc = _conv_tiles(n, f)
    nj = f // tc

    def body(ug, ugh, uv, uvh, wg, wv, bg, bv, da_ref, dc_ref, dw_ref, db_ref):
        first = pl.program_id(1) == 0
        g0, v0 = ug[...], uv[...]
        g1, g2 = _shift_down(g0, ugh[...], first, tr)
        v1, v2 = _shift_down(v0, uvh[...], first, tr)
        cg = _conv_lin(g0, g1, g2, wg, bg)
        cv = _conv_lin(v0, v1, v2, wv, bv)
        d = da_ref[...]
        s = _sigmoid(cg)
        dcg = d * cv * (s * (1.0 + cg * (1.0 - s)))
        dcv = d * (cg * s)
        dc_ref[0] = dcg
        dc_ref[1] = dcv

        @pl.when(first)
        def _():
            dw_ref[...] = jnp.zeros_like(dw_ref)
            db_ref[...] = jnp.zeros_like(db_ref)

        for p, dc, taps in ((0, dcg, (g2, g1, g0)), (1, dcv, (v2, v1, v0))):
            for t in range(3):
                dw_ref[p, t:t + 1, :] += jnp.sum(dc * taps[t], axis=0, keepdims=True)
            db_ref[p] += jnp.sum(dc, axis=0, keepdims=True)

    return pl.pallas_call(
        body, name=name, grid=(nj, n // tr),
        in_specs=_conv_in_specs(tr, tc, nj) + [BS((tr, tc), lambda j, i: (i, j))],
        out_specs=[BS((2, tr, tc), lambda j, i: (0, i, j)), BS((2, 3, tc), lambda j, i: (0, 0, j)),
                   BS((2, 1, tc), lambda j, i: (0, 0, j))],
        out_shape=[SDS((2, n, f), F32), SDS((2, 3, f), F32), SDS((2, 1, f), F32)],
        compiler_params=_cparams(("arbitrary", "arbitrary")))(u, u, u, u, cw, cw, cb, cb, da)


def _conv_bwd_du_call(dc, cw, name):
    _, n, f = dc.shape
    tr, tc = _conv_tiles(n, f)
    nj = f // tc
    ni = n // tr
    sub = tr // SUBLANES

    def body(c_ref, nx_ref, w_ref, du_ref):
        cur = c_ref[0]
        nxt = jnp.where(pl.program_id(2) == ni - 1, 0.0, nx_ref[0])
        row = lax.broadcasted_iota(jnp.int32, cur.shape, 0)
        n0, n1 = nxt[0:1, :], nxt[1:2, :]
        d1 = jnp.where(row == tr - 1, n0, pltpu.roll(cur, tr - 1, 0))
        d2 = jnp.where(row == tr - 1, n1, jnp.where(row == tr - 2, n0, pltpu.roll(cur, tr - 2, 0)))
        du_ref[...] = (w_ref[2:3, :] * cur + w_ref[1:2, :] * d1) + w_ref[0:1, :] * d2

    nxt_map = lambda p, j, i: (p, jnp.minimum((i + 1) * sub, n // SUBLANES - 1), j)
    return pl.pallas_call(
        body, name=name, grid=(2, nj, ni),
        in_specs=[BS((1, tr, tc), lambda p, j, i: (p, i, j)), BS((1, SUBLANES, tc), nxt_map),
                  BS((3, tc), lambda p, j, i: (0, p * nj + j))],
        out_specs=BS((tr, tc), lambda p, j, i: (i, p * nj + j)), out_shape=SDS((n, 2 * f), F32),
        compiler_params=_cparams(("parallel", "parallel", "parallel")))(dc, dc, cw)


def conv_glu(u, cw, cb, name):
    def fwd(u, cw, cb):
        return _conv_fwd_call(u, cw, cb, name + "_f"), (u, cw, cb)

    def bwd(res, da):
        u, cw, cb = res
        dc, dw, db = _conv_bwd_dc_call(u, cw, cb, da, name + "_bc")
        du = _conv_bwd_du_call(dc, cw, name + "_bu")
        return du, jnp.concatenate([dw[0], dw[1]], axis=-1), jnp.concatenate([db[0], db[1]], axis=-1)

    return _op(fwd, bwd)(u, cw, cb)


def _loss_call(y, t, n_real, name):
    n, c = y.shape
    tr = _rows_tile(n, c, 1 << 20)

    def body(y_ref, t_ref, dy_ref, l_ref):
        i = pl.program_id(0)
        row = i * tr + lax.broadcasted_iota(jnp.int32, (tr, c), 0)
        real = (row >= N_META) & (row < N_META + n_real)
        e = jnp.where(real, y_ref[...] - t_ref[...], 0.0)
        dy_ref[...] = e * (1.0 / c)

        @pl.when(i == 0)
        def _():
            l_ref[...] = jnp.zeros_like(l_ref)

        l_ref[...] += 0.5 * jnp.sum(jnp.sum(e * e, axis=-1, keepdims=True) * (1.0 / c), axis=0, keepdims=True)

    spec = BS((tr, c), lambda i: (i, 0))
    return pl.pallas_call(
        body, name=name, grid=(n // tr,), in_specs=[spec, spec],
        out_specs=[spec, BS((1, 1), lambda i: (0, 0))], out_shape=[SDS((n, c), F32), SDS((1, 1), F32)],
        compiler_params=_cparams(("arbitrary",)))(y, t)


def _to_heads(x, nh):
    n = x.shape[0]
    return x.reshape(n, nh, x.shape[1] // nh).transpose(1, 0, 2)


def _from_heads(x):
    h, n, d = x.shape
    return x.transpose(1, 0, 2).reshape(n, h * d)


def _head_norm(x, g, denom, name):
    h, n, d = x.shape
    return rms_norm(x.reshape(h * n, d), g, denom, name).reshape(h, n, d)


def _head_norm_rope(x, g, cos, sin, name):
    h, n, d = x.shape
    return rms_norm_rope(x.reshape(h * n, d), g, cos, sin, MLA_QK, name).reshape(h, n, d)


def _pad_in_cols(w):
    z = lambda k: jnp.zeros(w.shape[:-1] + (k,), w.dtype)
    return jnp.concatenate([w[..., :1544], z(120), w[..., 1544:1960], z(96), w[..., 1960:], z(128)], axis=-1)


def _pad_q_up(w):
    s = w.shape[:-1]
    w = w.reshape(s + (HEADS, MLA_QK))
    w = jnp.concatenate([w, jnp.zeros(s + (HEADS, LANES - MLA_QK), w.dtype)], axis=-1)
    return w.reshape(s + (HEADS * LANES,))


LAYERED = BIG + ("ffn_conv_w",)
BEFORE_MLA = ("w_in", "mla_w_q_up", "mla_w_kv_up")
AFTER_MLA = ("w_branch", "w_o", "ffn_w_up", "ffn_w_down")


def _assemble_layer(parts):
    out = {k: jnp.concatenate([v[i] for i in range(N_CHIPS)], axis=SHARD_AXIS[k] - 1) for k, v in parts.items()}
    if "w_in" in out:
        out["w_in"] = _pad_in_cols(out["w_in"])
    if "mla_w_q_up" in out:
        out["mla_w_q_up"] = _pad_q_up(out["mla_w_q_up"])
    return out


def _layer_parts(names, gathered):
    return {k: g.reshape((N_CHIPS,) + SHARD_SHAPE[k][1:]) for k, g in zip(names, gathered)}


def _rope_tables(n):
    half = MLA_ROPE // 2
    freqs = ROPE_THETA ** (-jnp.arange(half, dtype=F32) / half)
    ang = jnp.arange(n).astype(F32)[:, None] * freqs[None, :]
    cos, sin = jnp.cos(ang), jnp.sin(ang)
    one, zero = jnp.ones((n, MLA_NOPE), F32), jnp.zeros((n, MLA_NOPE), F32)
    tail1, tail0 = jnp.ones((n, LANES - MLA_QK), F32), jnp.zeros((n, LANES - MLA_QK), F32)
    return (jnp.concatenate([one, cos, cos, tail1], axis=1), jnp.concatenate([zero, sin, sin, tail0], axis=1))


def _pad_lanes(g, width):
    return jnp.concatenate([g, jnp.zeros((width - g.shape[0],), g.dtype)]).reshape(1, width)


PROJ_SEGMENTS = ((O_FQ, 512), (O_FK, 512), (O_FV, 512), (O_FF, HEADS), (O_CQ, MLA_Q_RANK), (O_CKV, MLA_KV_RANK),
                 (O_KR, MLA_ROPE), (O_SQ, 512), (O_SK, 128), (O_SV, 128), (O_G, D_MODEL), (O_G + D_MODEL, D_MODEL),
                 (O_G + 2 * D_MODEL, D_MODEL))


def _join_proj_call(parts, name):
    n = parts[0].shape[0]
    tr = _rows_tile(n, IN_PAD, 4 << 20)

    def body(*refs):
        o_ref = refs[-1]
        o_ref[...] = jnp.zeros_like(o_ref)
        for (s, w), r in zip(PROJ_SEGMENTS, refs[:-1]):
            o_ref[:, s:s + w] = r[...]

    return pl.pallas_call(
        body, name=name, grid=(n // tr,), in_specs=[BS((tr, w), lambda i: (i, 0)) for _, w in PROJ_SEGMENTS],
        out_specs=BS((tr, IN_PAD), lambda i: (i, 0)), out_shape=SDS((n, IN_PAD), F32),
        compiler_params=_cparams(("parallel",)))(*parts)


def _split_proj(proj, name):
    def fwd(x):
        return tuple(x[:, s:s + w] for s, w in PROJ_SEGMENTS), None

    def bwd(_, cts):
        return (_join_proj_call(cts, name + "_b"),)

    return _op(fwd, bwd)(proj)


def _trunk(eps, eps_cw, sinks, meta, small, x, w0, late):
    assert DEPTH == 2
    seq = x.shape[0]
    n = -(-(N_META + seq) // ROW_PAD) * ROW_PAD
    ew = _assemble_layer(eps)
    cos, sin = _rope_tables(n)
    slopes = jnp.exp2(-8.0 * jnp.arange(1, HEADS + 1, dtype=F32) / HEADS)
    h = jnp.concatenate([meta, x, jnp.zeros((n - N_META - seq, D_MODEL), F32)], axis=0)
    wb = w0
    for l in range(DEPTH):
        p = f"l{l}_"
        row = lambda name: small[name][l].reshape(1, -1)
        xn = rms_norm(h, row("norm1_g"), D_MODEL, p + "norm1")
        proj = linear(xn, wb["w_in"], ew["w_in"], p + "win")
        p_fq, p_fk, p_fv, p_ff, p_cq, p_ckv, p_kr, p_sq, p_sk, p_sv, g0, g1, g2 = _split_proj(proj, p + "split")
        fq = _head_norm(_to_heads(p_fq, HEADS), row("fox_q_g"), HEAD_DIM, p + "fqn")
        fk = _head_norm(_to_heads(p_fk, HEADS), row("fox_k_g"), HEAD_DIM, p + "fkn")
        fv = _to_heads(p_fv, HEADS)
        c = forget_cumsum(p_ff.T, small["fox_forget_b"][l].reshape(HEADS, 1), p + "fgate")
        if l == 0:
            rest = [a for k, a in zip(LAYERED, late) if k != "w_in"]
            out_a, got, carriers = causal_attention(fq, fk, fv, c, HEAD_DIM ** -0.5, p + "fox", rest, 0, BIG,
                                                    sinks[1])
            wb = dict(wb, **_assemble_layer(_layer_parts(LAYERED[1:], _fill_own(got, [a[0] for a in rest]))))
            ew1 = _assemble_layer(dict(zip(BIG, carriers)))
        else:
            out_a = causal_attention(fq, fk, fv, c, HEAD_DIM ** -0.5, p + "fox")
        cqn = rms_norm(p_cq, row("mla_q_a_g"), MLA_Q_RANK, p + "cqn")
        q = _to_heads(linear(cqn, wb["mla_w_q_up"], ew["mla_w_q_up"], p + "qup"), HEADS)
        q = _head_norm_rope(q, _pad_lanes(small["mla_q_g"][l], LANES), cos, sin, p + "mqn")
        ckvn = rms_norm(p_ckv, row("mla_kv_a_g"), MLA_KV_RANK, p + "ckvn")
        kv = _to_heads(linear(ckvn, wb["mla_w_kv_up"], ew["mla_w_kv_up"], p + "kvup"), HEADS)
        kr = jnp.broadcast_to(p_kr[None], (HEADS, n, MLA_ROPE))
        k = jnp.concatenate([kv[..., :MLA_NOPE], kr, jnp.zeros((HEADS, n, LANES - MLA_QK), F32)], axis=-1)
        k = _head_norm_rope(k, _pad_lanes(small["mla_k_g"][l], LANES), cos, sin, p + "mkn")
        if l == 0:
            out_b, got, carriers = causal_attention(q, k, kv[..., MLA_NOPE:], None, MLA_QK ** -0.5, p + "mla", late,
                                                    1, AFTER_MLA, sinks[0])
            w1 = _assemble_layer(_layer_parts(LAYERED, _fill_own(got, [a[1] for a in late])))
            ew = dict(ew, **_assemble_layer(dict(zip(AFTER_MLA, carriers))))
        else:
            out_b = causal_attention(q, k, kv[..., MLA_NOPE:], None, MLA_QK ** -0.5, p + "mla")
        sq = _head_norm(_to_heads(p_sq, HEADS), row("swa_q_g"), HEAD_DIM, p + "sqn")
        sk = _head_norm(_to_heads(p_sk, SWA_KV_HEADS), row("swa_k_g"), HEAD_DIM, p + "skn")
        sv = _to_heads(p_sv, SWA_KV_HEADS)
        out_c = window_attention(sq, sk, sv, small["swa_sinks"][l], slopes, HEAD_DIM ** -0.5, p + "swa")
        ys = [linear(_from_heads(o), wb["w_branch"][i], ew["w_branch"][i], p + f"br{i}")
              for i, o in enumerate((out_a, out_b, out_c))]
        merged = gated_merge([g0, g1, g2], ys, p + "merge")
        h = linear(merged, wb["w_o"], ew["w_o"], p + "wo", res=h)
        xn2 = rms_norm(h, row("norm2_g"), D_MODEL, p + "norm2")
        u = linear(xn2, wb["ffn_w_up"], ew["ffn_w_up"], p + "wup")
        act = conv_glu(u, lax.stop_gradient(wb["ffn_conv_w"]) + eps_cw[l], row("ffn_conv_b"), p + "conv")
        h = linear(act, wb["ffn_w_down"], ew["ffn_w_down"], p + "wdown", res=h)
        wb, ew = w1, ew1
    return h


def _local_step(x, target, w0, late, meta, small):
    seq = x.shape[0]
    eps = {k: jnp.zeros((N_CHIPS,) + SHARD_SHAPE[k][1:], F32) for k in BEFORE_MLA}
    eps_cw = jnp.zeros((DEPTH, 3, 2 * D_FF), F32)
    half = lambda k: jnp.zeros((HALVED[k][1] // 2, HALVED[k][2]), F32)
    sinks = tuple([(half(k), half(k)) for k in names] for names in (AFTER_MLA, BIG))
    y, vjp = jax.vjp(lambda e, ec, sk, mt, s, xx: _trunk(e, ec, sk, mt, s, xx, w0, late),
                     eps, eps_cw, sinks, meta, small, x)
    n = y.shape[0]
    tpad = jnp.concatenate([jnp.zeros((N_META, D_MODEL), F32), target, jnp.zeros((n - N_META - seq, D_MODEL), F32)])
    dy, loss = _loss_call(y, tpad, seq, "loss")
    g_eps, g_cw, g_l1, g_meta, g_small, g_x = vjp(dy)
    return loss[0, 0], g_x, g_eps, g_l1, g_cw, g_meta, g_small


def _pack_rows(shapes, mult):
    total = sum(_size(s) for s in shapes)
    rows = -(-total // LANES)
    return -(-rows // mult) * mult


def _size(shape):
    n = 1
    for d in shape:
        n *= d
    return n


def _pack(arrs, rows, dtype):
    flat = [a.reshape(-1).astype(dtype) for a in arrs]
    used = sum(a.size for a in flat)
    flat.append(jnp.zeros((rows * LANES - used,), dtype))
    return jnp.concatenate(flat).reshape(rows, LANES)


def _unpack(p, shapes):
    flat = p.reshape(-1)
    out, off = [], 0
    for s in shapes:
        out.append(flat[off:off + _size(s)].reshape(s))
        off += _size(s)
    return out


MESH = pl.DeviceIdType.MESH
ANY = pl.BlockSpec(memory_space=pl.ANY)


def _me():
    return lax.axis_index("x"), lax.axis_index("y"), lax.axis_index("c")


def _remote(src, dst, send_sems, recv_sems, idx, dev):
    return pltpu.make_async_remote_copy(src_ref=src, dst_ref=dst, send_sem=send_sems.at[idx], recv_sem=recv_sems.at[idx],
                                        device_id=dev, device_id_type=MESH)


def _layer_gather(ins, outs, send_sems, recv_sems, layer, base):
    x, y, _ = _me()
    j = 2 * x + y
    sibling = (x, y, 1 - layer)
    chips = [(1 - x, y), (x, 1 - y), (1 - x, 1 - y)]

    def ici(p, r):
        cx, cy = chips[r]
        return _remote(ins[p].at[layer], outs[p].at[j], send_sems, recv_sems, base + 6 * p + r, (cx, cy, layer))

    def d2d(p, r):
        cx, cy = chips[r]
        blk = outs[p].at[2 * cx + cy]
        return _remote(blk, blk, send_sems, recv_sems, base + 6 * p + 3 + r, sibling)

    pairs = [(p, r) for p in range(len(ins)) for r in range(3)]

    def start():
        for p, r in pairs:
            ici(p, r).start()

    def forward():
        for p, r in pairs:
            ici(p, r).wait_recv()
            d2d(p, r).start()

    def drain():
        for p, r in pairs:
            ici(p, r).wait_send()
            d2d(p, r).wait_send()

    def receive():
        for p, r in pairs:
            d2d(p, r).wait_recv()

    return start, forward, drain, receive


def _fill_own(outs, own):
    j = 2 * lax.axis_index("x") + lax.axis_index("y")
    return [lax.dynamic_update_index_in_dim(o, a, j, 0) for o, a in zip(outs, own)]


def _gather_early(meta, arrs, name):
    npk = len(arrs)

    def body(*refs):
        m_in, ins = refs[0], refs[1:npk + 1]
        m_out, outs = refs[npk + 1], refs[npk + 2:2 * npk + 2]
        send_sems, recv_sems = refs[2 * npk + 2:]
        x, y, c = _me()
        j = 2 * x + y
        sibling = (x, y, 1 - c)
        chips = [(1 - x, y), (x, 1 - y), (1 - x, 1 - y)]
        start, forward, drain, receive = _layer_gather(ins, outs, send_sems, recv_sems, 0, 6)
        sends = []
        for r, (cx, cy) in enumerate(chips):
            cp = _remote(m_in.at[c], m_out.at[j, c], send_sems, recv_sems, r, (cx, cy, c))
            cp.start()
            sends.append(cp)
        pl.when(c == 0)(start)
        for r, (cx, cy) in enumerate(chips):
            blk = m_out.at[2 * cx + cy, c]
            _remote(blk, blk, send_sems, recv_sems, r, sibling).wait_recv()
            fw = _remote(blk, blk, send_sems, recv_sems, 3 + r, sibling)
            fw.start()
            sends.append(fw)
        for r, (cx, cy) in enumerate(chips):
            blk = m_out.at[2 * cx + cy, 1 - c]
            _remote(blk, blk, send_sems, recv_sems, 3 + r, sibling).wait_recv()
        for cp in sends:
            cp.wait_send()

        @pl.when(c == 0)
        def _():
            forward()
            drain()

        pl.when(c == 1)(receive)

    nsem = 6 + 6 * npk
    res = pl.pallas_call(
        body, name=name, in_specs=[ANY] * (npk + 1), out_specs=[ANY] * (npk + 1),
        out_shape=[SDS((N_CHIPS,) + meta.shape, meta.dtype)] + [SDS((N_CHIPS,) + a.shape[1:], a.dtype) for a in arrs],
        scratch_shapes=[pltpu.SemaphoreType.DMA((nsem,)), pltpu.SemaphoreType.DMA((nsem,))],
        compiler_params=pltpu.CompilerParams(has_side_effects=True))(meta, *arrs)
    return _fill_own(res[:1], [meta])[0], _fill_own(res[1:], [a[0] for a in arrs])


def _pair_exchange(gs, name):
    npk = len(gs)

    def body(*refs):
        ins, outs = refs[:npk], refs[npk:2 * npk]
        send_sems, recv_sems = refs[2 * npk:]
        x, y, c = _me()
        cps = [_remote(ins[p].at[:, 1 - c], outs[p], send_sems, recv_sems, p, (x, y, 1 - c)) for p in range(npk)]
        for cp in cps:
            cp.start()
        for cp in cps:
            cp.wait()

    return pl.pallas_call(
        body, name=name, in_specs=[ANY] * npk, out_specs=[ANY] * npk,
        out_shape=[SDS(g.shape[:1] + g.shape[2:], g.dtype) for g in gs],
        scratch_shapes=[pltpu.SemaphoreType.DMA((npk,)), pltpu.SemaphoreType.DMA((npk,))],
        compiler_params=pltpu.CompilerParams(has_side_effects=True))(*gs)


def _chip_exchange(ss, small, name):
    npk = len(ss)

    def body(*refs):
        ins, sm_ref = refs[:npk], refs[npk]
        outs, sa_ref = refs[npk + 1:2 * npk + 1], refs[2 * npk + 1]
        send_sems, recv_sems, loc_sem = refs[2 * npk + 2:]
        x, y, c = _me()
        me = 4 * x + 2 * y + c
        lc = pltpu.make_async_copy(sm_ref, sa_ref.at[me], loc_sem.at[0])
        lc.start()
        cps = []
        for p in range(npk):
            for r, (cx, cy) in enumerate([(1 - x, y), (x, 1 - y), (1 - x, 1 - y)]):
                cp = _remote(ins[p].at[2 * cx + cy], outs[p].at[r], send_sems, recv_sems, 3 * p + r, (cx, cy, c))
                cp.start()
                cps.append(cp)
        base = 3 * npk - 1
        for mask in range(1, N_DEV):
            px, py, pc = x ^ (mask >> 2), y ^ ((mask >> 1) & 1), c ^ (mask & 1)
            cp = _remote(sm_ref, sa_ref.at[me], send_sems, recv_sems, base + mask, (px, py, pc))
            cp.start()
            cps.append(cp)
        for p in range(npk):
            for r in range(3):
                _remote(outs[p].at[r], outs[p].at[r], send_sems, recv_sems, 3 * p + r, (x, y, c)).wait_recv()
        for mask in range(1, N_DEV):
            src = 4 * (x ^ (mask >> 2)) + 2 * (y ^ ((mask >> 1) & 1)) + (c ^ (mask & 1))
            _remote(sa_ref.at[src], sa_ref.at[src], send_sems, recv_sems, base + mask, (x, y, c)).wait_recv()
        for cp in cps:
            cp.wait_send()
        lc.wait()

    nsem = 3 * npk + N_DEV - 1
    res = pl.pallas_call(
        body, name=name, in_specs=[ANY] * (npk + 1), out_specs=[ANY] * (npk + 1),
        out_shape=[SDS((3,) + s.shape[1:], s.dtype) for s in ss] + [SDS((N_DEV,) + small.shape, small.dtype)],
        scratch_shapes=[pltpu.SemaphoreType.DMA((nsem,)), pltpu.SemaphoreType.DMA((nsem,)),
                        pltpu.SemaphoreType.DMA((1,))],
        compiler_params=pltpu.CompilerParams(has_side_effects=True))(*ss, small)
    return res[:npk], res[npk]


def _half_exchange(ghs, name):
    npk = len(ghs)

    def body(*refs):
        ins, outs = refs[:npk], refs[npk:2 * npk]
        send_sems, recv_sems = refs[2 * npk:]
        x, y, c = _me()
        cps = [_remote(ins[p], outs[p], send_sems, recv_sems, p, (x, y, 1 - c)) for p in range(npk)]
        for cp in cps:
            cp.start()
        for cp in cps:
            cp.wait()

    return pl.pallas_call(
        body, name=name, in_specs=[ANY] * npk, out_specs=[ANY] * npk, out_shape=[SDS(g.shape, g.dtype) for g in ghs],
        scratch_shapes=[pltpu.SemaphoreType.DMA((npk,)), pltpu.SemaphoreType.DMA((npk,))],
        compiler_params=pltpu.CompilerParams(has_side_effects=True))(*ghs)


def _add_tile(rows, cols):
    return _div_tile(rows, max(16, (1 << 19) // max(cols, LANES)), 16)


def _pair_add(g, r1, c_idx, name):
    _, rows, cols = r1.shape
    tr = _add_tile(rows, cols)

    def body(c_ref, g_ref, r_ref, o_ref, ob_ref):
        s = g_ref[0] + r_ref[...]
        o_ref[...] = s
        ob_ref[...] = s.astype(BF16)

    own = BS((1, tr, cols), lambda k, i, c: (k, i, 0))
    return pl.pallas_call(
        body, name=name,
        grid_spec=pltpu.PrefetchScalarGridSpec(
            num_scalar_prefetch=1, grid=(r1.shape[0], rows // tr),
            in_specs=[BS((1, 1, tr, cols), lambda k, i, c: (k, c[0], i, 0)), own], out_specs=[own, own]),
        out_shape=[SDS(r1.shape, F32), SDS(r1.shape, BF16)],
        compiler_params=_cparams(("parallel", "parallel")))(c_idx, g, r1)


def _chip_add(s1, r2, j_idx, name):
    _, rows, cols = s1.shape
    tr = _add_tile(rows, cols)

    def body(j_ref, s_ref, r_ref, o_ref):
        o_ref[...] = ((s_ref[0] + r_ref[0].astype(F32)) + r_ref[1].astype(F32)) + r_ref[2].astype(F32)

    return pl.pallas_call(
        body, name=name,
        grid_spec=pltpu.PrefetchScalarGridSpec(
            num_scalar_prefetch=1, grid=(rows // tr,),
            in_specs=[BS((1, tr, cols), lambda i, j: (j[0], i, 0)), BS((3, tr, cols), lambda i, j: (0, i, 0))],
            out_specs=BS((tr, cols), lambda i, j: (i, 0))),
        out_shape=SDS((rows, cols), F32), compiler_params=_cparams(("parallel",)))(j_idx, s1, r2)


def _adamw_math(w, g, m, v):
    m = ADAM_B1 * m + (1.0 - ADAM_B1) * g
    v = ADAM_B2 * v + (1.0 - ADAM_B2) * (g * g)
    m_hat = m / (1.0 - ADAM_B1 ** ADAM_STEP)
    v_hat = v / (1.0 - ADAM_B2 ** ADAM_STEP)
    delta = -ADAM_LR * (m_hat / (jnp.sqrt(v_hat) + ADAM_EPS) + ADAM_WD * w)
    return delta, m, v


def _adamw(w, gh, go, m, v, c_idx, name):
    _, rows, cols = w.shape
    tr = _add_tile(rows, cols)

    def body(c_ref, w_ref, gh_ref, go_ref, m_ref, v_ref, g_out, d_out, m_out, v_out):
        g = jnp.where(pl.program_id(0) == c_ref[0], gh_ref[...], go_ref[...])
        g_out[0] = g
        d_out[0], m_out[0], v_out[0] = _adamw_math(w_ref[0], g, m_ref[0], v_ref[0])

    full = BS((1, tr, cols), lambda hf, i, c: (hf, i, 0))
    half = BS((tr, cols), lambda hf, i, c: (i, 0))
    return pl.pallas_call(
        body, name=name,
        grid_spec=pltpu.PrefetchScalarGridSpec(
            num_scalar_prefetch=1, grid=(2, rows // tr), in_specs=[full, half, half, full, full],
            out_specs=[full] * 4),
        out_shape=[SDS(w.shape, F32)] * 4, compiler_params=_cparams(("parallel", "parallel")))(c_idx, w, gh, go, m, v)


def _adamw_layers(w, g0, g1, m, v, c_idx, name):
    _, _, rows, cols = w.shape
    tr = _add_tile(rows, cols)

    def body(c_ref, w_ref, h0_ref, o0_ref, h1_ref, o1_ref, m_ref, v_ref, g_out, d_out, m_out, v_out):
        mine = pl.program_id(1) == c_ref[0]
        g = jnp.where(pl.program_id(0) == 0, jnp.where(mine, h0_ref[...], o0_ref[...]),
                      jnp.where(mine, h1_ref[...], o1_ref[...]))
        g_out[0, 0] = g
        d_out[0, 0], m_out[0, 0], v_out[0, 0] = _adamw_math(w_ref[0, 0], g, m_ref[0, 0], v_ref[0, 0])

    full = BS((1, 1, tr, cols), lambda l, hf, i, c: (l, hf, i, 0))
    half = BS((tr, cols), lambda l, hf, i, c: (i, 0))
    return pl.pallas_call(
        body, name=name,
        grid_spec=pltpu.PrefetchScalarGridSpec(
            num_scalar_prefetch=1, grid=(2, 2, rows // tr), in_specs=[full, half, half, half, half, full, full],
            out_specs=[full] * 4),
        out_shape=[SDS(w.shape, F32)] * 4,
        compiler_params=_cparams(("parallel", "parallel", "parallel")))(c_idx, w, *g0, *g1, m, v)


def _sum_devices(sa, name):
    def body(sa_ref, g_out):
        g = sa_ref[0]
        for d in range(1, N_DEV):
            g = g + sa_ref[d]
        g_out[...] = g

    return pl.pallas_call(body, name=name, out_shape=SDS(sa.shape[1:], F32),
                          compiler_params=pltpu.CompilerParams(vmem_limit_bytes=VMEM_LIMIT))(sa)


def _adamw_small(ws, gs, ms, vs, name):
    k = len(ws)

    def body(*refs):
        ins, outs = refs[:4 * k], refs[4 * k:]
        for i in range(k):
            d, m, v = _adamw_math(ins[i][...], ins[k + i][...], ins[2 * k + i][...], ins[3 * k + i][...])
            outs[i][...], outs[k + i][...], outs[2 * k + i][...] = d, m, v

    return pl.pallas_call(body, name=name, out_shape=[SDS(w.shape, F32) for w in ws] * 3,
                          compiler_params=pltpu.CompilerParams(vmem_limit_bytes=VMEM_LIMIT))(*ws, *gs, *ms, *vs)


HALVED = {"w_in": (2, 1024, 1450), "mla_w_q_up": (2, 256, 192), "mla_w_kv_up": (2, 128, 256),
          "w_branch": (2, 1536, 256), "w_o": (2, 256, 1024), "ffn_w_up": (2, 1024, 1408),
          "ffn_w_down": (2, 704, 1024), "ffn_conv_w": (2, 3, 1408), "meta_tokens": (2, 8, 256)}
SMALL_SHAPE = {"norm1_g": (2, 1024), "fox_forget_b": (2, 8), "fox_q_g": (2, 64), "fox_k_g": (2, 64),
               "mla_q_a_g": (2, 256), "mla_kv_a_g": (2, 128), "mla_q_g": (2, 96), "mla_k_g": (2, 96),
               "swa_q_g": (2, 64), "swa_k_g": (2, 64), "swa_sinks": (2, 8), "norm2_g": (2, 1024),
               "ffn_conv_b": (2, 5632)}
SMALL_ROWS = _pack_rows([SMALL_SHAPE[k] for k in SMALL] + [(1,)], SUBLANES)


def kernel(x, meta_tokens, norm1_g, w_in, fox_forget_b, fox_q_g, fox_k_g, mla_q_a_g, mla_w_q_up, mla_kv_a_g, mla_w_kv_up, mla_q_g, mla_k_g, swa_q_g, swa_k_g, swa_sinks, w_branch, w_o, norm2_g, ffn_w_up, ffn_conv_w, ffn_conv_b, ffn_w_down, loss_target, m_meta_tokens, m_norm1_g, m_w_in, m_fox_forget_b, m_fox_q_g, m_fox_k_g, m_mla_q_a_g, m_mla_w_q_up, m_mla_kv_a_g, m_mla_w_kv_up, m_mla_q_g, m_mla_k_g, m_swa_q_g, m_swa_k_g, m_swa_sinks, m_w_branch, m_w_o, m_norm2_g, m_ffn_w_up, m_ffn_conv_w, m_ffn_conv_b, m_ffn_w_down, v_meta_tokens, v_norm1_g, v_w_in, v_fox_forget_b, v_fox_q_g, v_fox_k_g, v_mla_q_a_g, v_mla_w_q_up, v_mla_kv_a_g, v_mla_w_kv_up, v_mla_q_g, v_mla_k_g, v_swa_q_g, v_swa_k_g, v_swa_sinks, v_w_branch, v_w_o, v_norm2_g, v_ffn_w_up, v_ffn_conv_w, v_ffn_conv_b, v_ffn_w_down):
    w = dict(meta_tokens=meta_tokens, norm1_g=norm1_g, w_in=w_in, fox_forget_b=fox_forget_b, fox_q_g=fox_q_g,
             fox_k_g=fox_k_g, mla_q_a_g=mla_q_a_g, mla_w_q_up=mla_w_q_up, mla_kv_a_g=mla_kv_a_g,
             mla_w_kv_up=mla_w_kv_up, mla_q_g=mla_q_g, mla_k_g=mla_k_g, swa_q_g=swa_q_g, swa_k_g=swa_k_g,
             swa_sinks=swa_sinks, w_branch=w_branch, w_o=w_o, norm2_g=norm2_g, ffn_w_up=ffn_w_up,
             ffn_conv_w=ffn_conv_w, ffn_conv_b=ffn_conv_b, ffn_w_down=ffn_w_down)
    m = dict(meta_tokens=m_meta_tokens, norm1_g=m_norm1_g, w_in=m_w_in, fox_forget_b=m_fox_forget_b,
             fox_q_g=m_fox_q_g, fox_k_g=m_fox_k_g, mla_q_a_g=m_mla_q_a_g, mla_w_q_up=m_mla_w_q_up,
             mla_kv_a_g=m_mla_kv_a_g, mla_w_kv_up=m_mla_w_kv_up, mla_q_g=m_mla_q_g, mla_k_g=m_mla_k_g,
             swa_q_g=m_swa_q_g, swa_k_g=m_swa_k_g, swa_sinks=m_swa_sinks, w_branch=m_w_branch, w_o=m_w_o,
             norm2_g=m_norm2_g, ffn_w_up=m_ffn_w_up, ffn_conv_w=m_ffn_conv_w, ffn_conv_b=m_ffn_conv_b,
             ffn_w_down=m_ffn_w_down)
    v = dict(meta_tokens=v_meta_tokens, norm1_g=v_norm1_g, w_in=v_w_in, fox_forget_b=v_fox_forget_b,
             fox_q_g=v_fox_q_g, fox_k_g=v_fox_k_g, mla_q_a_g=v_mla_q_a_g, mla_w_q_up=v_mla_w_q_up,
             mla_kv_a_g=v_mla_kv_a_g, mla_w_kv_up=v_mla_w_kv_up, mla_q_g=v_mla_q_g, mla_k_g=v_mla_k_g,
             swa_q_g=v_swa_q_g, swa_k_g=v_swa_k_g, swa_sinks=v_swa_sinks, w_branch=v_w_branch, w_o=v_w_o,
             norm2_g=v_norm2_g, ffn_w_up=v_ffn_w_up, ffn_conv_w=v_ffn_conv_w, ffn_conv_b=v_ffn_conv_b,
             ffn_w_down=v_ffn_w_down)
    xi, yi, ci = _me()
    c_idx = ci.astype(jnp.int32).reshape(1)
    j_idx = (2 * xi + yi).astype(jnp.int32).reshape(1)

    sh_names = BIG + FINE
    local = {k: (w[k].astype(BF16) if k in BIG else w[k]).reshape(HALVED[k]) for k in sh_names}
    late = [local[k] for k in LAYERED]
    meta_g, early = _gather_early(local["meta_tokens"], [local["w_in"]], "gather_early")
    meta = jnp.concatenate([meta_g[i].reshape(SHARD_SHAPE["meta_tokens"]) for i in range(N_CHIPS)], axis=1)
    w0 = _assemble_layer(_layer_parts(("w_in",), early))
    small = {k: w[k] for k in SMALL}

    loss, g_x, g_eps, g_red, g_cw, g_meta, g_small = _local_step(x[0], loss_target[0], w0, late, meta, small)
    g_after0, g_l1 = dict(zip(AFTER_MLA, g_red[0])), dict(zip(BIG, g_red[1]))

    quarter = {k: (2, HALVED[k][1] // 2, HALVED[k][2]) for k in BIG}
    last = BEFORE_MLA + FINE
    gs = [g_eps[k].reshape((N_CHIPS,) + quarter[k]) for k in BEFORE_MLA]
    for k, g in (("meta_tokens", g_meta), ("ffn_conv_w", g_cw)):
        gs.append(jnp.stack(jnp.split(g, N_CHIPS, axis=SHARD_AXIS[k])).reshape((N_CHIPS,) + HALVED[k]))
    spack = _pack([g_small[k] for k in SMALL] + [loss.reshape(1)], SMALL_ROWS, F32)
    r1 = _pair_exchange(gs, "grads_pair_exchange")
    s1 = [_pair_add(g, r, c_idx, "grads_pair_add_" + k) for g, r, k in zip(gs, r1, last)]
    r2, sa = _chip_exchange([s[1] for s in s1], spack, "grads_chip_exchange")
    gh = dict(zip(last, [_chip_add(s[0], r, j_idx, "grads_chip_add_" + k) for s, r, k in zip(s1, r2, last)]))
    go = dict(zip(last, _half_exchange([gh[k] for k in last], "grads_half_exchange")))

    grads, deltas, new_m, new_v = {}, {}, {}, {}
    for k in sh_names:
        if k in BIG:
            shp = (2,) + quarter[k]
            wk, mk, vk = w[k].reshape(shp), m[k].reshape(shp), v[k].reshape(shp)
            g0 = (gh[k], go[k]) if k in BEFORE_MLA else g_after0[k]
            outs = _adamw_layers(wk, g0, g_l1[k], mk, vk, c_idx, "adamw_" + k)
        else:
            outs = _adamw(w[k].reshape(HALVED[k]), gh[k], go[k], m[k].reshape(HALVED[k]), v[k].reshape(HALVED[k]),
                          c_idx, "adamw_" + k)
        for dst, o in zip((grads, deltas, new_m, new_v), outs):
            dst[k] = o.reshape(SHARD_SHAPE[k])
    sm_shapes = [SMALL_SHAPE[k] for k in SMALL] + [(1,)]
    g_sum = _unpack(_sum_devices(sa, "sum_small"), sm_shapes)
    res = _adamw_small([w[k] for k in SMALL], g_sum[:-1], [m[k] for k in SMALL], [v[k] for k in SMALL], "adamw_small")
    ns = len(SMALL)
    grads.update(zip(SMALL, g_sum[:-1]))
    for dst, vals in zip((deltas, new_m, new_v), (res[:ns], res[ns:2 * ns], res[2 * ns:])):
        dst.update(zip(SMALL, vals))
    total_loss = g_sum[-1][0]
    return (total_loss, g_x[None], *[grads[k] for k in WEIGHTS], *[deltas[k] for k in WEIGHTS],
            *[new_m[k] for k in WEIGHTS], *[new_v[k] for k in WEIGHTS])
```

```python
import jax
import jax.numpy as jnp
from jax import lax
from jax.experimental import pallas as pl
from jax.experimental.pallas import tpu as pltpu

F32 = jnp.float32
BF16 = jnp.bfloat16
SDS = jax.ShapeDtypeStruct
BS = pl.BlockSpec

D_MODEL = 1024
DEPTH = 2
N_META = 16
EPS = 1e-6
HEADS = 8
HEAD_DIM = 64
MLA_Q_RANK = 256
MLA_KV_RANK = 128
MLA_NOPE = 64
MLA_ROPE = 32
MLA_QK = MLA_NOPE + MLA_ROPE
ROPE_THETA = 10000.0
SWA_KV_HEADS = 2
WINDOW = 128
D_FF = 2816
IN_PAD = 6144
N_CHIPS = 4
N_DEV = 8

ADAM_LR = 0.001
ADAM_B1 = 0.9
ADAM_B2 = 0.999
ADAM_EPS = 1e-08
ADAM_WD = 0.01
ADAM_STEP = 10

LANES = 128
SUBLANES = 8
ROW_PAD = 128
CAUSAL_TILE = 384
NEG = -1e30
VMEM_LIMIT = 56 * 1024 * 1024

O_FQ, O_FK, O_FV, O_FF = 0, 512, 1024, 1536
O_CQ, O_CKV, O_KR = 1664, 1920, 2048
O_SQ, O_SK, O_SV, O_G = 2176, 2688, 2816, 2944

SHARD_AXIS = {"meta_tokens": 1, "w_in": 2, "mla_w_q_up": 2, "mla_w_kv_up": 2, "w_branch": 3, "w_o": 1,
              "ffn_w_up": 2, "ffn_conv_w": 2, "ffn_w_down": 1}
SHARD_SHAPE = {"meta_tokens": (16, 256), "w_in": (2, 1024, 1450), "mla_w_q_up": (2, 256, 192),
               "mla_w_kv_up": (2, 128, 256), "w_branch": (2, 3, 512, 256), "w_o": (2, 256, 1024),
               "ffn_w_up": (2, 1024, 1408), "ffn_conv_w": (2, 3, 1408), "ffn_w_down": (2, 704, 1024)}
BIG = ("w_in", "mla_w_q_up", "mla_w_kv_up", "w_branch", "w_o", "ffn_w_up", "ffn_w_down")
FINE = ("meta_tokens", "ffn_conv_w")
SMALL = ("norm1_g", "fox_forget_b", "fox_q_g", "fox_k_g", "mla_q_a_g", "mla_kv_a_g", "mla_q_g", "mla_k_g",
         "swa_q_g", "swa_k_g", "swa_sinks", "norm2_g", "ffn_conv_b")
WEIGHTS = ("meta_tokens", "norm1_g", "w_in", "fox_forget_b", "fox_q_g", "fox_k_g", "mla_q_a_g", "mla_w_q_up",
           "mla_kv_a_g", "mla_w_kv_up", "mla_q_g", "mla_k_g", "swa_q_g", "swa_k_g", "swa_sinks", "w_branch", "w_o",
           "norm2_g", "ffn_w_up", "ffn_conv_w", "ffn_conv_b", "ffn_w_down")


def _cparams(sem):
    return pltpu.CompilerParams(dimension_semantics=sem, vmem_limit_bytes=VMEM_LIMIT)


def _div_tile(n, cap, mult=SUBLANES):
    best = None
    for t in range(mult, min(n, cap) + 1, mult):
        if n % t == 0:
            best = t
    return best if best is not None else n


def _rows_tile(n, width, budget=2 << 20):
    return _div_tile(n, max(SUBLANES, budget // (4 * max(width, LANES))))


def _op(fwd, bwd):
    @jax.custom_vjp
    def op(*args):
        return fwd(*args)[0]
    op.defvjp(fwd, bwd)
    return op


def _rotate(v, cos, sin):
    lane = lax.broadcasted_iota(jnp.int32, v.shape, 1)
    rot = jnp.where(lane < MLA_NOPE + MLA_ROPE // 2, -pltpu.roll(v, LANES - MLA_ROPE // 2, 1),
                    pltpu.roll(v, MLA_ROPE // 2, 1))
    return v * cos + rot * sin


def _rms_fwd_call(x, g, denom, name, rot=None):
    n, c = x.shape
    tr = _rows_tile(n if rot is None else rot[0].shape[0], c)
    nt = None if rot is None else rot[0].shape[0] // tr

    def body(x_ref, g_ref, *rest):
        y_ref = rest[-1]
        xv = x_ref[...]
        ms = jnp.sum(xv * xv, axis=-1, keepdims=True) * (1.0 / denom)
        y = xv * lax.rsqrt(ms + EPS) * g_ref[...]
        y_ref[...] = y if rot is None else _rotate(y, rest[0][...], rest[1][...])

    ins, args = [BS((tr, c), lambda i: (i, 0)), BS((1, c), lambda i: (0, 0))], [x, g]
    if rot is not None:
        ins += [BS((tr, c), lambda i: (i % nt, 0))] * 2
        args += list(rot)
    return pl.pallas_call(
        body, name=name, grid=(n // tr,), in_specs=ins,
        out_specs=BS((tr, c), lambda i: (i, 0)), out_shape=SDS((n, c), F32),
        compiler_params=_cparams(("parallel",)))(*args)


def _rms_bwd_call(x, g, dy, denom, name, rot=None):
    n, c = x.shape
    tr = _rows_tile(n if rot is None else rot[0].shape[0], c)
    nt = None if rot is None else rot[0].shape[0] // tr

    def body(x_ref, g_ref, dy_ref, *rest):
        dx_ref, dg_ref = rest[-2:]
        xv = x_ref[...]
        dy = dy_ref[...]
        if rot is not None:
            dy = _rotate(dy, rest[0][...], -rest[1][...])
        ms = jnp.sum(xv * xv, axis=-1, keepdims=True) * (1.0 / denom)
        r = lax.rsqrt(ms + EPS)
        xh = xv * r
        dxh = dy * g_ref[...]
        dx_ref[...] = r * (dxh - xh * (jnp.sum(dxh * xh, axis=-1, keepdims=True) * (1.0 / denom)))

        @pl.when(pl.program_id(0) == 0)
        def _():
            dg_ref[...] = jnp.zeros_like(dg_ref)

        dg_ref[...] += jnp.sum(dy * xh, axis=0, keepdims=True)

    ins = [BS((tr, c), lambda i: (i, 0)), BS((1, c), lambda i: (0, 0)), BS((tr, c), lambda i: (i, 0))]
    args = [x, g, dy]
    if rot is not None:
        ins += [BS((tr, c), lambda i: (i % nt, 0))] * 2
        args += list(rot)
    return pl.pallas_call(
        body, name=name, grid=(n // tr,), in_specs=ins,
        out_specs=[BS((tr, c), lambda i: (i, 0)), BS((1, c), lambda i: (0, 0))],
        out_shape=[SDS((n, c), F32), SDS((1, c), F32)],
        compiler_params=_cparams(("arbitrary",)))(*args)


def rms_norm(x, g, denom, name):
    def fwd(x, g):
        return _rms_fwd_call(x, g, denom, name + "_f"), (x, g)

    def bwd(res, dy):
        return tuple(_rms_bwd_call(res[0], res[1], dy, denom, name + "_b"))

    return _op(fwd, bwd)(x, g)


def rms_norm_rope(x, g, cos, sin, denom, name):
    def fwd(x, g, cos, sin):
        return _rms_fwd_call(x, g, denom, name + "_f", (cos, sin)), (x, g, cos, sin)

    def bwd(res, dy):
        x, g, cos, sin = res
        dx, dg = _rms_bwd_call(x, g, dy, denom, name + "_b", (cos, sin))
        return dx, dg, jnp.zeros_like(cos), jnp.zeros_like(sin)

    return _op(fwd, bwd)(x, g, cos, sin)


def _mm_call(a, b, mode, res, name):
    if mode == "nn":
        (m, kc), n = a.shape, b.shape[1]
    elif mode == "nt":
        (m, kc), n = a.shape, b.shape[0]
    else:
        (kc, m), n = a.shape, b.shape[1]
    if mode == "tn":
        tk = _div_tile(kc, 528)
        tm = _div_tile(m, 1408, LANES)
        tn = _div_tile(n, 2048, LANES)
    else:
        tk = kc if kc <= 2816 else _div_tile(kc, 1024, LANES)
        tm = _div_tile(m, max(LANES, (9 << 19) // (4 * tk)))
        tn = _div_tile(n, 1408 if mode == "nt" else 512, LANES)
    nk = kc // tk
    dims = {"nn": (((1,), (0,)), ((), ())), "nt": (((1,), (1,)), ((), ())), "tn": (((0,), (0,)), ((), ()))}[mode]

    def body(*refs):
        if res is None:
            a_ref, b_ref, o_ref, acc_ref = refs
            r_ref = None
        else:
            a_ref, b_ref, r_ref, o_ref, acc_ref = refs
        k = pl.program_id(2)

        @pl.when(k == 0)
        def _():
            acc_ref[...] = jnp.zeros_like(acc_ref)

        acc_ref[...] += lax.dot_general(a_ref[...].astype(BF16), b_ref[...].astype(BF16), dims,
                                        preferred_element_type=F32)

        @pl.when(k == nk - 1)
        def _():
            if r_ref is None:
                o_ref[...] = acc_ref[...]
            else:
                o_ref[...] = r_ref[...] + acc_ref[...]

    a_spec = BS((tk, tm), lambda i, j, k: (k, i)) if mode == "tn" else BS((tm, tk), lambda i, j, k: (i, k))
    b_spec = BS((tn, tk), lambda i, j, k: (j, k)) if mode == "nt" else BS((tk, tn), lambda i, j, k: (k, j))
    o_spec = BS((tm, tn), lambda i, j, k: (i, j))
    ins, args = [a_spec, b_spec], [a, b]
    if res is not None:
        ins.append(o_spec)
        args.append(res)
    return pl.pallas_call(
        body, name=name, grid=(m // tm, n // tn, nk), in_specs=ins, out_specs=o_spec,
        out_shape=SDS((m, n), F32), scratch_shapes=[pltpu.VMEM((tm, tn), F32)],
        compiler_params=_cparams(("parallel", "parallel", "arbitrary")))(*args)


def linear(a, w, eps, name, res=None):
    if res is None:
        def fwd(a, w, eps):
            return _mm_call(a, w, "nn", None, name + "_f"), (a, w)

        def bwd(r, dc):
            a, w = r
            return (_mm_call(dc, w, "nt", None, name + "_da"), jnp.zeros_like(w),
                    _mm_call(a, dc, "tn", None, name + "_dw"))

        return _op(fwd, bwd)(a, w, eps)

    def fwd_r(a, w, eps, res):
        return _mm_call(a, w, "nn", res, name + "_f"), (a, w)

    def bwd_r(r, dc):
        a, w = r
        return (_mm_call(dc, w, "nt", None, name + "_da"), jnp.zeros_like(w),
                _mm_call(a, dc, "tn", None, name + "_dw"), dc)

    return _op(fwd_r, bwd_r)(a, w, eps, res)


def _gate_tile(n):
    return _div_tile(n, CAUSAL_TILE, LANES)


def _tri_dot(v, upper):
    ct = v.shape[1]
    r = lax.broadcasted_iota(jnp.int32, (ct, ct), 0)
    c = lax.broadcasted_iota(jnp.int32, (ct, ct), 1)
    tri = jnp.where((r <= c) if upper else (r >= c), 1.0, 0.0).astype(F32)
    return jnp.dot(v, tri, preferred_element_type=F32, precision=lax.Precision.HIGHEST)


def _gate_fwd_call(z, b, name):
    h, n = z.shape
    CT = _gate_tile(n)

    def body(z_ref, b_ref, c_ref, carry):
        @pl.when(pl.program_id(0) == 0)
        def _():
            carry[...] = jnp.zeros_like(carry)

        x = z_ref[...] + b_ref[...]
        ls = jnp.minimum(x, 0.0) - jnp.log(1.0 + jnp.exp(-jnp.abs(x)))
        c_ref[...] = _tri_dot(ls, True) + carry[...]
        carry[...] += jnp.sum(ls, axis=1, keepdims=True)

    return pl.pallas_call(
        body, name=name, grid=(n // CT,),
        in_specs=[BS((h, CT), lambda j: (0, j)), BS((h, 1), lambda j: (0, 0))],
        out_specs=BS((h, CT), lambda j: (0, j)), out_shape=SDS((h, n), F32),
        scratch_shapes=[pltpu.VMEM((h, 1), F32)],
        compiler_params=_cparams(("arbitrary",)))(z, b)


def _gate_bwd_call(z, b, dc, name):
    h, n = z.shape
    CT = _gate_tile(n)
    nt = n // CT

    def body(z_ref, b_ref, dc_ref, dz_ref, db_ref, carry):
        @pl.when(pl.program_id(0) == 0)
        def _():
            carry[...] = jnp.zeros_like(carry)
            db_ref[...] = jnp.zeros_like(db_ref)

        dcv = dc_ref[...]
        dls = _tri_dot(dcv, False) + carry[...]
        carry[...] += jnp.sum(dcv, axis=1, keepdims=True)
        x = z_ref[...] + b_ref[...]
        e = jnp.exp(-jnp.abs(x))
        dz = dls * jnp.where(x >= 0, e / (1.0 + e), 1.0 / (1.0 + e))
        dz_ref[...] = dz
        db_ref[...] += jnp.sum(dz, axis=1, keepdims=True)

    rev = lambda j: (0, nt - 1 - j)
    return pl.pallas_call(
        body, name=name, grid=(nt,),
        in_specs=[BS((h, CT), rev), BS((h, 1), lambda j: (0, 0)), BS((h, CT), rev)],
        out_specs=[BS((h, CT), rev), BS((h, 1), lambda j: (0, 0))],
        out_shape=[SDS((h, n), F32), SDS((h, 1), F32)],
        scratch_shapes=[pltpu.VMEM((h, 1), F32)],
        compiler_params=_cparams(("arbitrary",)))(z, b, dc)


def forget_cumsum(z, b, name):
    def fwd(z, b):
        return _gate_fwd_call(z, b, name + "_f"), (z, b)

    def bwd(res, dc):
        return tuple(_gate_bwd_call(res[0], res[1], dc, name + "_b"))

    return _op(fwd, bwd)(z, b)


NT_DIMS = (((1,), (1,)), ((), ()))
TN_DIMS = (((0,), (0,)), ((), ()))
HEADS_PER_STEP = 2


def _causal_tile(n):
    return CAUSAL_TILE if n % CAUSAL_TILE == 0 else ROW_PAD


def _causal_fwd_call(q, k, v, ck_r, fox, scale, name, late=None, late_layer=1):
    h, n, dk = q.shape
    dv = v.shape[2]
    t = _causal_tile(n)
    nq = n // t
    hb = HEADS_PER_STEP
    nl = 0 if late is None else len(late)
    n_in = 3 + int(fox) + nl

    def body(*refs):
        q_ref, k_ref, v_ref = refs[:3]
        ck_ref = refs[3] if fox else None
        o_ref, lse_ref = refs[n_in:n_in + 2]
        m_scr, l_scr, acc_scr = refs[n_in + 2 + nl:n_in + 5 + nl]
        qi = pl.program_id(1)
        if nl:
            start, forward, drain, receive = _layer_gather(refs[n_in - nl:n_in], refs[n_in + 2:n_in + 2 + nl],
                                                           refs[-2], refs[-1], late_layer, 0)
            hp, core = pl.program_id(0), lax.axis_index("c")
            last = (hp == h // hb - 1) & (qi == nq - 1)
            pl.when((hp == 0) & (qi == 0) & (core == late_layer))(start)
            pl.when((hp == h // hb // 2) & (qi == 0) & (core == late_layer))(forward)
            pl.when(last & (core == late_layer))(drain)
            pl.when(last & (core == 1 - late_layer))(receive)
        qbs = [q_ref[e].astype(BF16) for e in range(hb)]
        m_scr[...] = jnp.full(m_scr.shape, NEG, F32)
        l_scr[...] = jnp.zeros_like(l_scr)
        acc_scr[...] = jnp.zeros_like(acc_scr)

        def process(j, masked):
            off = pl.multiple_of(j * t, t)
            if masked:
                rows = lax.broadcasted_iota(jnp.int32, (t, t), 0)
                cols = lax.broadcasted_iota(jnp.int32, (t, t), 1)
                valid = cols <= rows
            for e in range(hb):
                kb = k_ref[e, pl.ds(off, t), :].astype(BF16)
                vb = v_ref[e, pl.ds(off, t), :].astype(BF16)
                s = lax.dot_general(qbs[e], kb, NT_DIMS, preferred_element_type=F32) * scale
                if fox:
                    s = s - ck_ref[e, j]
                if masked:
                    s = jnp.where(valid, s, NEG)
                m_old = m_scr[e]
                m_new = jnp.maximum(m_old, jnp.max(s, axis=1, keepdims=True))
                alpha = jnp.exp(m_old - m_new)
                p = jnp.exp(s - jnp.tile(m_new, (1, t // LANES)))
                l_scr[e] = alpha * l_scr[e] + jnp.sum(p, axis=1, keepdims=True)
                acc_scr[e] = alpha[:, :dv] * acc_scr[e] + jnp.dot(p.astype(BF16), vb, preferred_element_type=F32)
                m_scr[e] = m_new

        def step(j, carry):
            process(j, False)
            return carry

        lax.fori_loop(0, qi, step, 0)
        process(qi, True)
        for e in range(hb):
            l = l_scr[e]
            o_ref[e] = acc_scr[e] / l[:, :dv]
            lse_ref[e, 0] = jnp.transpose(m_scr[e] + jnp.log(l))[0:1, :]

    ins = [BS((hb, t, dk), lambda a, b: (a, b, 0)), BS((hb, n, dk), lambda a, b: (a, 0, 0)),
           BS((hb, n, dv), lambda a, b: (a, 0, 0))]
    args = [q, k, v]
    if fox:
        ins.append(BS((hb, nq, 1, t), lambda a, b: (a, 0, 0, 0)))
        args.append(ck_r)
    outs = [BS((hb, t, dv), lambda a, b: (a, b, 0)), BS((hb, 1, 1, t), lambda a, b: (a, b, 0, 0))]
    oshape = [SDS((h, n, dv), F32), SDS((h, nq, 1, t), F32)]
    scratch = [pltpu.VMEM((hb, t, LANES), F32), pltpu.VMEM((hb, t, LANES), F32), pltpu.VMEM((hb, t, dv), F32)]
    if nl:
        ins += [ANY] * nl
        args += list(late)
        outs += [ANY] * nl
        oshape += [SDS((N_CHIPS,) + a.shape[1:], a.dtype) for a in late]
        scratch += [pltpu.SemaphoreType.DMA((6 * nl,)), pltpu.SemaphoreType.DMA((6 * nl,))]
    res = pl.pallas_call(
        body, name=name, grid=(h // hb, nq), in_specs=ins, out_specs=outs, out_shape=oshape, scratch_shapes=scratch,
        compiler_params=pltpu.CompilerParams(dimension_semantics=("arbitrary", "arbitrary"),
                                             vmem_limit_bytes=VMEM_LIMIT, has_side_effects=bool(nl)))(*args)
    return res[0], res[1], list(res[2:])


def _causal_bwd_call(q, k, v, do, o, lse_r, ck_r, fox, scale, name, side=None):
    h, n, dk = q.shape
    dv = v.shape[2]
    t = _causal_tile(n)
    nq = n // t
    hb = HEADS_PER_STEP
    ns = 0 if side is None else len(side)

    def body(*refs):
        it = iter(refs)
        q_ref, k_ref, v_ref, do_ref, o_ref, lse_ref = (next(it) for _ in range(6))
        ck_ref = next(it) if fox else None
        side_in = [next(it) for _ in range(ns)]
        dq_ref, dk_ref, dv_ref = next(it), next(it), next(it)
        dck_ref, dcq_ref = (next(it), next(it)) if fox else (None, None)
        side_out = [next(it) for _ in range(ns)]
        delta_scr, dk_scr, dv_scr = next(it), next(it), next(it)
        dck_scr = next(it) if fox else None
        kj = pl.program_id(1)
        if ns:
            send_sems, recv_sems = next(it), next(it)
            x, y, core = _me()
            chips = [(1 - x, y), (x, 1 - y), (1 - x, 1 - y)]
            copies = [_remote(side_in[p].at[2 * cx + cy], side_out[p].at[r], send_sems, recv_sems, 3 * p + r,
                              (cx, cy, core)) for p in range(ns) for r, (cx, cy) in enumerate(chips)]

            @pl.when((pl.program_id(0) == 0) & (kj == 0))
            def _():
                for cp in copies:
                    cp.start()

            @pl.when((pl.program_id(0) == h // hb - 1) & (kj == nq - 1))
            def _():
                for cp in copies:
                    cp.wait()

        @pl.when(kj == 0)
        def _():
            dq_ref[...] = jnp.zeros_like(dq_ref)
            if fox:
                dcq_ref[...] = jnp.zeros_like(dcq_ref)
            ones = jnp.ones((SUBLANES, dv), F32)

            def fill(qi, carry):
                off = pl.multiple_of(qi * t, t)
                for e in range(hb):
                    prod = do_ref[e, pl.ds(off, t), :] * o_ref[e, pl.ds(off, t), :]
                    delta_scr[e, qi] = lax.dot_general(ones, prod, NT_DIMS, preferred_element_type=F32,
                                                       precision=lax.Precision.HIGHEST)[0:1, :]
                return carry

            lax.fori_loop(0, nq, fill, 0)

        kbs = [k_ref[e].astype(BF16) for e in range(hb)]
        vbs = [v_ref[e].astype(BF16) for e in range(hb)]
        dk_scr[...] = jnp.zeros_like(dk_scr)
        dv_scr[...] = jnp.zeros_like(dv_scr)
        if fox:
            dck_scr[...] = jnp.zeros_like(dck_scr)
            ckcs = [jnp.tile(jnp.transpose(jnp.broadcast_to(ck_ref[e, 0], (LANES, t))), (1, t // LANES))
                    for e in range(hb)]

        def process(qi, masked):
            off = pl.multiple_of(qi * t, t)
            if masked:
                krows = lax.broadcasted_iota(jnp.int32, (t, t), 0)
                qcols = lax.broadcasted_iota(jnp.int32, (t, t), 1)
                valid = krows <= qcols
            for e in range(hb):
                qb = q_ref[e, pl.ds(off, t), :].astype(BF16)
                dob = do_ref[e, pl.ds(off, t), :].astype(BF16)
                st = lax.dot_general(kbs[e], qb, NT_DIMS, preferred_element_type=F32) * scale
                if fox:
                    st = st - ckcs[e]
                pt = jnp.exp(st - lse_ref[e, qi])
                if masked:
                    pt = jnp.where(valid, pt, 0.0)
                dv_scr[e] += jnp.dot(pt.astype(BF16), dob, preferred_element_type=F32)
                dpt = lax.dot_general(vbs[e], dob, NT_DIMS, preferred_element_type=F32)
                dst = pt * (dpt - delta_scr[e, qi])
                if fox:
                    dck_scr[e] -= jnp.sum(dst, axis=1, keepdims=True)
                    dcq_ref[e, qi] += jnp.sum(dst, axis=0, keepdims=True)
                dsb = (dst * scale).astype(BF16)
                dk_scr[e] += jnp.dot(dsb, qb, preferred_element_type=F32)
                dq_ref[e, pl.ds(off, t), :] += lax.dot_general(dsb, kbs[e], TN_DIMS, preferred_element_type=F32)

        def step(qi, carry):
            process(qi, False)
            return carry

        process(kj, True)
        lax.fori_loop(kj + 1, nq, step, 0)
        dk_ref[...] = dk_scr[...]
        dv_ref[...] = dv_scr[...]
        if fox:
            for e in range(hb):
                dck_ref[e, 0] = jnp.transpose(jnp.broadcast_to(dck_scr[e], (t, LANES)))[0:1, :]

    whole = lambda a, b: (a, 0, 0)
    tile = lambda a, b: (a, b, 0)
    rowv = lambda a, b: (a, 0, 0, 0)
    rowt = lambda a, b: (a, b, 0, 0)
    ins = [BS((hb, n, dk), whole), BS((hb, t, dk), tile), BS((hb, t, dv), tile), BS((hb, n, dv), whole),
           BS((hb, n, dv), whole), BS((hb, nq, 1, t), rowv)]
    args = [q, k, v, do, o, lse_r]
    outs = [BS((hb, n, dk), whole), BS((hb, t, dk), tile), BS((hb, t, dv), tile)]
    oshape = [SDS((h, n, dk), F32), SDS((h, n, dk), F32), SDS((h, n, dv), F32)]
    scratch = [pltpu.VMEM((hb, nq, 1, t), F32), pltpu.VMEM((hb, t, dk), F32), pltpu.VMEM((hb, t, dv), F32)]
    if fox:
        ins.append(BS((hb, 1, 1, t), rowt))
        args.append(ck_r)
        outs += [BS((hb, 1, 1, t), rowt), BS((hb, nq, 1, t), rowv)]
        oshape += [SDS((h, nq, 1, t), F32), SDS((h, nq, 1, t), F32)]
        scratch.append(pltpu.VMEM((hb, t, 1), F32))
    if ns:
        ins += [ANY] * ns
        args += list(side)
        outs += [ANY] * ns
        oshape += [SDS((3,) + s.shape[1:], s.dtype) for s in side]
        scratch += [pltpu.SemaphoreType.DMA((3 * ns,)), pltpu.SemaphoreType.DMA((3 * ns,))]
    return pl.pallas_call(
        body, name=name, grid=(h // hb, nq), in_specs=ins, out_specs=outs, out_shape=oshape, scratch_shapes=scratch,
        compiler_params=pltpu.CompilerParams(dimension_semantics=("arbitrary", "arbitrary"),
                                             vmem_limit_bytes=VMEM_LIMIT, has_side_effects=bool(ns)))(*args)


def causal_attention(q, k, v, c, scale, name, late=None, late_layer=1, reduce=None, sinks=None):
    h, n, _ = q.shape
    t = _causal_tile(n)
    nq = n // t
    fox = c is not None

    def run_fwd(q, k, v, c, late, sinks):
        ck_r = c.reshape(h, nq, 1, t) if fox else None
        o, lse, got = _causal_fwd_call(q, k, v, ck_r, fox, scale, name + "_f", late, late_layer)
        out = (o,)
        if late is not None:
            out += (got,)
        if reduce is not None:
            out += ([jnp.zeros((N_CHIPS,) + SHARD_SHAPE[w][1:], F32) for w in reduce],)
        return (out if len(out) > 1 else o), (q, k, v, c, late, o, lse)

    def run_bwd(res, ct):
        q, k, v, c, late, o, lse = res
        ck_r = c.reshape(h, nq, 1, t) if fox else None
        dlate = None if late is None else [jnp.zeros_like(a) for a in late]
        if reduce is None:
            do = ct if late is None else ct[0]
            outs = _causal_bwd_call(q, k, v, do, o, lse, ck_r, fox, scale, name + "_b")
            return outs[0], outs[1], outs[2], ((outs[3] + outs[4]).reshape(h, n) if fox else None), dlate, None
        do, g1 = ct[0], ct[-1]
        xi, yi, ci = _me()
        c_idx = ci.astype(jnp.int32).reshape(1)
        j_idx = (2 * xi + yi).astype(jnp.int32).reshape(1)
        gs = [g.reshape((N_CHIPS,) + HALVED[w][1:]) for g, w in zip(g1, reduce)]
        outs = _causal_bwd_call(q, k, v, do, o, lse, ck_r, fox, scale, name + "_b", [g.astype(BF16) for g in gs])
        sc = [_chip_add(g, r, j_idx, name + "_chip_add_" + w).reshape(1, 2, HALVED[w][1] // 2, HALVED[w][2])
              for g, r, w in zip(gs, outs[-len(reduce):], reduce)]
        r1 = _pair_exchange(sc, name + "_pair_exchange")
        gh = [_pair_add(s, r, c_idx, name + "_pair_add_" + w)[0][0] for s, r, w in zip(sc, r1, reduce)]
        go = _half_exchange(gh, name + "_half_exchange")
        return (outs[0], outs[1], outs[2], ((outs[3] + outs[4]).reshape(h, n) if fox else None), dlate,
                list(zip(gh, go)))

    return _op(run_fwd, run_bwd)(q, k, v, c, late, sinks)


SWA_T = 128


def _swa_masks(qi):
    t = SWA_T
    r = lax.broadcasted_iota(jnp.int32, (t, 3 * t), 0)
    c = lax.broadcasted_iota(jnp.int32, (t, 3 * t), 1)
    seg0 = c < t
    seg1 = (c >= t) & (c < 2 * t)
    jp = jnp.maximum(qi - 1, 0)
    kpos = jnp.where(seg0, c, jnp.where(seg1, jp * t + c - t, qi * t + c - 2 * t))
    dist = qi * t + r - kpos
    band = (dist >= 0) & ((dist < WINDOW) | (kpos < N_META))
    valid = (seg0 & (kpos < N_META) & (qi >= 2)) | (jnp.logical_not(seg0) & band & (jnp.logical_not(seg1) | (qi >= 1)))
    return valid, dist.astype(F32)


def _swa_cat(ref, qi):
    t = SWA_T
    jp = jnp.maximum(qi - 1, 0)
    return jnp.concatenate([ref[0, 0:t, :], ref[0, pl.ds(pl.multiple_of(jp * t, t), t), :],
                            ref[0, pl.ds(pl.multiple_of(qi * t, t), t), :]], axis=0).astype(BF16)


def _swa_fwd_call(q, k, v, sinks, slopes, scale, name):
    hq, n, d = q.shape
    hkv = k.shape[0]
    g = hq // hkv
    t = SWA_T
    nq = n // t

    def body(q_ref, k_ref, v_ref, sink_ref, slope_ref, o_ref, lse_ref):
        grp = pl.program_id(0)
        qi = pl.program_id(1)
        valid, dist = _swa_masks(qi)
        kc = _swa_cat(k_ref, qi)
        vc = _swa_cat(v_ref, qi)
        qs = jnp.concatenate([q_ref[e] for e in range(g)], axis=0).astype(BF16)
        s_all = lax.dot_general(qs, kc, NT_DIMS, preferred_element_type=F32) * scale
        ps, ls, ms = [], [], []
        for e in range(g):
            hh = grp * g + e
            s = jnp.where(valid, s_all[e * t:(e + 1) * t] - slope_ref[hh] * dist, NEG)
            m = jnp.maximum(jnp.max(s, axis=1, keepdims=True), sink_ref[hh])
            p = jnp.exp(s - m)
            ls.append(jnp.sum(p, axis=1, keepdims=True) + jnp.exp(sink_ref[hh] - m))
            ms.append(m)
            ps.append(p.astype(BF16))
        acc = jnp.dot(jnp.concatenate(ps, axis=0), vc, preferred_element_type=F32)
        for e in range(g):
            o_ref[e] = acc[e * t:(e + 1) * t] / ls[e]
            lse_ref[e] = ms[e] + jnp.log(ls[e])

    return pl.pallas_call(
        body, name=name, grid=(hkv, nq),
        in_specs=[BS((g, t, d), lambda a, b: (a, b, 0)), BS((1, n, d), lambda a, b: (a, 0, 0)),
                  BS((1, n, d), lambda a, b: (a, 0, 0)), BS(memory_space=pltpu.SMEM), BS(memory_space=pltpu.SMEM)],
        out_specs=[BS((g, t, d), lambda a, b: (a, b, 0)), BS((g, t, 1), lambda a, b: (a, b, 0))],
        out_shape=[SDS((hq, n, d), F32), SDS((hq, n, 1), F32)],
        compiler_params=_cparams(("parallel", "parallel")))(q, k, v, sinks, slopes)


def _swa_bwd_call(q, k, v, o, lse, do, sinks, slopes, scale, name):
    hq, n, d = q.shape
    hkv = k.shape[0]
    g = hq // hkv
    t = SWA_T
    nq = n // t

    def body(q_ref, k_ref, v_ref, o_ref, lse_ref, do_ref, sink_ref, slope_ref, dq_ref, dk_ref, dv_ref, ds_ref):
        grp = pl.program_id(0)
        qi = pl.program_id(1)

        @pl.when(qi == 0)
        def _():
            dk_ref[...] = jnp.zeros_like(dk_ref)
            dv_ref[...] = jnp.zeros_like(dv_ref)
            ds_ref[...] = jnp.zeros_like(ds_ref)

        valid, dist = _swa_masks(qi)
        kc = _swa_cat(k_ref, qi)
        vc = _swa_cat(v_ref, qi)
        qs = jnp.concatenate([q_ref[e] for e in range(g)], axis=0).astype(BF16)
        dos = jnp.concatenate([do_ref[e] for e in range(g)], axis=0).astype(BF16)
        s_all = lax.dot_general(qs, kc, NT_DIMS, preferred_element_type=F32) * scale
        dp_all = lax.dot_general(dos, vc, NT_DIMS, preferred_element_type=F32)
        ps, dss = [], []
        for e in range(g):
            hh = grp * g + e
            lse_e = lse_ref[e]
            delta = jnp.sum(do_ref[e] * o_ref[e], axis=1, keepdims=True)
            s = s_all[e * t:(e + 1) * t] - slope_ref[hh] * dist
            p = jnp.where(valid, jnp.exp(s - lse_e), 0.0)
            ds = p * (dp_all[e * t:(e + 1) * t] - delta)
            ps.append(p.astype(BF16))
            dss.append((ds * scale).astype(BF16))
            ds_ref[e] += -jnp.sum(jnp.exp(sink_ref[hh] - lse_e) * delta)
        p_st = jnp.concatenate(ps, axis=0)
        ds_st = jnp.concatenate(dss, axis=0)
        dq = jnp.dot(ds_st, kc, preferred_element_type=F32)
        for e in range(g):
            dq_ref[e] = dq[e * t:(e + 1) * t]
        dkc = lax.dot_general(ds_st, qs, TN_DIMS, preferred_element_type=F32)
        dvc = lax.dot_general(p_st, dos, TN_DIMS, preferred_element_type=F32)
        jp = jnp.maximum(qi - 1, 0)
        for seg, off in enumerate((0, pl.multiple_of(jp * t, t), pl.multiple_of(qi * t, t))):
            dk_ref[0, pl.ds(off, t), :] += dkc[seg * t:(seg + 1) * t]
            dv_ref[0, pl.ds(off, t), :] += dvc[seg * t:(seg + 1) * t]

    tile = lambda a, b: (a, b, 0)
    whole = lambda a, b: (a, 0, 0)
    return pl.pallas_call(
        body, name=name, grid=(hkv, nq),
        in_specs=[BS((g, t, d), tile), BS((1, n, d), whole), BS((1, n, d), whole), BS((g, t, d), tile),
                  BS((g, t, 1), tile), BS((g, t, d), tile), BS(memory_space=pltpu.SMEM), BS(memory_space=pltpu.SMEM)],
        out_specs=[BS((g, t, d), tile), BS((1, n, d), whole), BS((1, n, d), whole), BS((g, 1, LANES), whole)],
        out_shape=[SDS((hq, n, d), F32), SDS((hkv, n, d), F32), SDS((hkv, n, d), F32), SDS((hq, 1, LANES), F32)],
        compiler_params=_cparams(("arbitrary", "arbitrary")))(q, k, v, o, lse, do, sinks, slopes)


def window_attention(q, k, v, sinks, slopes, scale, name):
    def run_fwd(q, k, v, sinks, slopes):
        o, lse = _swa_fwd_call(q, k, v, sinks, slopes, scale, name + "_f")
        return o, (q, k, v, sinks, slopes, o, lse)

    def run_bwd(res, do):
        q, k, v, sinks, slopes, o, lse = res
        dq, dk, dv, ds = _swa_bwd_call(q, k, v, o, lse, do, sinks, slopes, scale, name + "_b")
        return dq, dk, dv, ds[:, 0, 0], jnp.zeros_like(slopes)

    return _op(run_fwd, run_bwd)(q, k, v, sinks, slopes)


def _sigmoid(x):
    return 1.0 / (1.0 + jnp.exp(-x))


def _merge_fwd_call(gs, ys, name):
    n, c = ys[0].shape
    tr = _rows_tile(n, c, 1 << 20)

    def body(g0, g1, g2, y0, y1, y2, m_ref):
        m_ref[...] = (_sigmoid(g0[...]) * y0[...] + _sigmoid(g1[...]) * y1[...]) + _sigmoid(g2[...]) * y2[...]

    spec = BS((tr, c), lambda i: (i, 0))
    return pl.pallas_call(
        body, name=name, grid=(n // tr,), in_specs=[spec] * 6, out_specs=spec, out_shape=SDS((n, c), F32),
        compiler_params=_cparams(("parallel",)))(*gs, *ys)


def _merge_bwd_call(gs, ys, dm, name):
    n, c = ys[0].shape
    tr = _rows_tile(n, c, 1 << 20)

    def body(g0, g1, g2, y0, y1, y2, dm_ref, dg0, dg1, dg2, dy0, dy1, dy2):
        d = dm_ref[...]
        for g, y, dg, dy in ((g0, y0, dg0, dy0), (g1, y1, dg1, dy1), (g2, y2, dg2, dy2)):
            s = _sigmoid(g[...])
            dy[...] = d * s
            dg[...] = d * y[...] * (s * (1.0 - s))

    spec = BS((tr, c), lambda i: (i, 0))
    return pl.pallas_call(
        body, name=name, grid=(n // tr,), in_specs=[spec] * 7, out_specs=[spec] * 6,
        out_shape=[SDS((n, c), F32)] * 6, compiler_params=_cparams(("parallel",)))(*gs, *ys, dm)


def gated_merge(gs, ys, name):
    def fwd(gs, ys):
        return _merge_fwd_call(gs, ys, name + "_f"), (gs, ys)

    def bwd(res, dm):
        out = _merge_bwd_call(res[0], res[1], dm, name + "_b")
        return tuple(out[:3]), tuple(out[3:])

    return _op(fwd, bwd)(tuple(gs), tuple(ys))


CONV_TR = 264
CONV_TC = 1408


def _conv_tiles(n, f):
    tr = CONV_TR if n % CONV_TR == 0 else _div_tile(n, CONV_TR)
    tc = CONV_TC if f % CONV_TC == 0 else f
    return tr, tc


def _shift_down(cur, halo, first, tr):
    halo = jnp.where(first, 0.0, halo)
    row = lax.broadcasted_iota(jnp.int32, cur.shape, 0)
    h7, h6 = halo[7:8, :], halo[6:7, :]
    u1 = jnp.where(row == 0, h7, pltpu.roll(cur, 1, 0))
    u2 = jnp.where(row == 0, h6, jnp.where(row == 1, h7, pltpu.roll(cur, 2, 0)))
    return u1, u2


def _conv_lin(cur, u1, u2, w_ref, b_ref):
    return ((b_ref[...] + w_ref[0:1, :] * u2) + w_ref[1:2, :] * u1) + w_ref[2:3, :] * cur


def _conv_in_specs(tr, tc, nj):
    sub = tr // SUBLANES
    prev = lambda j, i: (jnp.maximum(i * sub - 1, 0), j)
    prev_v = lambda j, i: (jnp.maximum(i * sub - 1, 0), j + nj)
    return [BS((tr, tc), lambda j, i: (i, j)), BS((SUBLANES, tc), prev),
            BS((tr, tc), lambda j, i: (i, j + nj)), BS((SUBLANES, tc), prev_v),
            BS((3, tc), lambda j, i: (0, j)), BS((3, tc), lambda j, i: (0, j + nj)),
            BS((1, tc), lambda j, i: (0, j)), BS((1, tc), lambda j, i: (0, j + nj))]


def _conv_fwd_call(u, cw, cb, name):
    n, f2 = u.shape
    f = f2 // 2
    tr, tc = _conv_tiles(n, f)
    nj = f // tc

    def body(ug, ugh, uv, uvh, wg, wv, bg, bv, a_ref):
        first = pl.program_id(1) == 0
        g1, g2 = _shift_down(ug[...], ugh[...], first, tr)
        v1, v2 = _shift_down(uv[...], uvh[...], first, tr)
        cg = _conv_lin(ug[...], g1, g2, wg, bg)
        cv = _conv_lin(uv[...], v1, v2, wv, bv)
        a_ref[...] = cg * _sigmoid(cg) * cv

    return pl.pallas_call(
        body, name=name, grid=(nj, n // tr), in_specs=_conv_in_specs(tr, tc, nj),
        out_specs=BS((tr, tc), lambda j, i: (i, j)), out_shape=SDS((n, f), F32),
        compiler_params=_cparams(("parallel", "parallel")))(u, u, u, u, cw, cw, cb, cb)


def _conv_bwd_dc_call(u, cw, cb, da, name):
    n, f2 = u.shape
    f = f2 // 2
    tr, tc = _conv_tiles(n, f)
    nj = f // tc

    def body(ug, ugh, uv, uvh, wg, wv, bg, bv, da_ref, dc_ref, dw_ref, db_ref):
        first = pl.program_id(1) == 0
        g0, v0 = ug[...], uv[...]
        g1, g2 = _shift_down(g0, ugh[...], first, tr)
        v1, v2 = _shift_down(v0, uvh[...], first, tr)
        cg = _conv_lin(g0, g1, g2, wg, bg)
        cv = _conv_lin(v0, v1, v2, wv, bv)
        d = da_ref[...]
        s = _sigmoid(cg)
        dcg = d * cv * (s * (1.0 + cg * (1.0 - s)))
        dcv = d * (cg * s)
        dc_ref[0] = dcg
        dc_ref[1] = dcv

        @pl.when(first)
        def _():
            dw_ref[...] = jnp.zeros_like(dw_ref)
            db_ref[...] = jnp.zeros_like(db_ref)

        for p, dc, taps in ((0, dcg, (g2, g1, g0)), (1, dcv, (v2, v1, v0))):
            for t in range(3):
                dw_ref[p, t:t + 1, :] += jnp.sum(dc * taps[t], axis=0, keepdims=True)
            db_ref[p] += jnp.sum(dc, axis=0, keepdims=True)

    return pl.pallas_call(
        body, name=name, grid=(nj, n // tr),
        in_specs=_conv_in_specs(tr, tc, nj) + [BS((tr, tc), lambda j, i: (i, j))],
        out_specs=[BS((2, tr, tc), lambda j, i: (0, i, j)), BS((2, 3, tc), lambda j, i: (0, 0, j)),
                   BS((2, 1, tc), lambda j, i: (0, 0, j))],
        out_shape=[SDS((2, n, f), F32), SDS((2, 3, f), F32), SDS((2, 1, f), F32)],
        compiler_params=_cparams(("arbitrary", "arbitrary")))(u, u, u, u, cw, cw, cb, cb, da)


def _conv_bwd_du_call(dc, cw, name):
    _, n, f = dc.shape
    tr, tc = _conv_tiles(n, f)
    nj = f // tc
    ni = n // tr
    sub = tr // SUBLANES

    def body(c_ref, nx_ref, w_ref, du_ref):
        cur = c_ref[0]
        nxt = jnp.where(pl.program_id(2) == ni - 1, 0.0, nx_ref[0])
        row = lax.broadcasted_iota(jnp.int32, cur.shape, 0)
        n0, n1 = nxt[0:1, :], nxt[1:2, :]
        d1 = jnp.where(row == tr - 1, n0, pltpu.roll(cur, tr - 1, 0))
        d2 = jnp.where(row == tr - 1, n1, jnp.where(row == tr - 2, n0, pltpu.roll(cur, tr - 2, 0)))
        du_ref[...] = (w_ref[2:3, :] * cur + w_ref[1:2, :] * d1) + w_ref[0:1, :] * d2

    nxt_map = lambda p, j, i: (p, jnp.minimum((i + 1) * sub, n // SUBLANES - 1), j)
    return pl.pallas_call(
        body, name=name, grid=(2, nj, ni),
        in_specs=[BS((1, tr, tc), lambda p, j, i: (p, i, j)), BS((1, SUBLANES, tc), nxt_map),
                  BS((3, tc), lambda p, j, i: (0, p * nj + j))],
        out_specs=BS((tr, tc), lambda p, j, i: (i, p * nj + j)), out_shape=SDS((n, 2 * f), F32),
        compiler_params=_cparams(("parallel", "parallel", "parallel")))(dc, dc, cw)


def conv_glu(u, cw, cb, name):
    def fwd(u, cw, cb):
        return _conv_fwd_call(u, cw, cb, name + "_f"), (u, cw, cb)

    def bwd(res, da):
        u, cw, cb = res
        dc, dw, db = _conv_bwd_dc_call(u, cw, cb, da, name + "_bc")
        du = _conv_bwd_du_call(dc, cw, name + "_bu")
        return du, jnp.concatenate([dw[0], dw[1]], axis=-1), jnp.concatenate([db[0], db[1]], axis=-1)

    return _op(fwd, bwd)(u, cw, cb)


def _loss_call(y, t, n_real, name):
    n, c = y.shape
    tr = _rows_tile(n, c, 1 << 20)

    def body(y_ref, t_ref, dy_ref, l_ref):
        i = pl.program_id(0)
        row = i * tr + lax.broadcasted_iota(jnp.int32, (tr, c), 0)
        real = (row >= N_META) & (row < N_META + n_real)
        e = jnp.where(real, y_ref[...] - t_ref[...], 0.0)
        dy_ref[...] = e * (1.0 / c)

        @pl.when(i == 0)
        def _():
            l_ref[...] = jnp.zeros_like(l_ref)

        l_ref[...] += 0.5 * jnp.sum(jnp.sum(e * e, axis=-1, keepdims=True) * (1.0 / c), axis=0, keepdims=True)

    spec = BS((tr, c), lambda i: (i, 0))
    return pl.pallas_call(
        body, name=name, grid=(n // tr,), in_specs=[spec, spec],
        out_specs=[spec, BS((1, 1), lambda i: (0, 0))], out_shape=[SDS((n, c), F32), SDS((1, 1), F32)],
        compiler_params=_cparams(("arbitrary",)))(y, t)


def _to_heads(x, nh):
    n = x.shape[0]
    return x.reshape(n, nh, x.shape[1] // nh).transpose(1, 0, 2)


def _from_heads(x):
    h, n, d = x.shape
    return x.transpose(1, 0, 2).reshape(n, h * d)


def _head_norm(x, g, denom, name):
    h, n, d = x.shape
    return rms_norm(x.reshape(h * n, d), g, denom, name).reshape(h, n, d)


def _head_norm_rope(x, g, cos, sin, name):
    h, n, d = x.shape
    return rms_norm_rope(x.reshape(h * n, d), g, cos, sin, MLA_QK, name).reshape(h, n, d)


def _pad_in_cols(w):
    z = lambda k: jnp.zeros(w.shape[:-1] + (k,), w.dtype)
    return jnp.concatenate([w[..., :1544], z(120), w[..., 1544:1960], z(96), w[..., 1960:], z(128)], axis=-1)


def _pad_q_up(w):
    s = w.shape[:-1]
    w = w.reshape(s + (HEADS, MLA_QK))
    w = jnp.concatenate([w, jnp.zeros(s + (HEADS, LANES - MLA_QK), w.dtype)], axis=-1)
    return w.reshape(s + (HEADS * LANES,))


LAYERED = BIG + ("ffn_conv_w",)
BEFORE_MLA = ("w_in", "mla_w_q_up", "mla_w_kv_up")
AFTER_MLA = ("w_branch", "w_o", "ffn_w_up", "ffn_w_down")


def _assemble_layer(parts):
    out = {k: jnp.concatenate([v[i] for i in range(N_CHIPS)], axis=SHARD_AXIS[k] - 1) for k, v in parts.items()}
    if "w_in" in out:
        out["w_in"] = _pad_in_cols(out["w_in"])
    if "mla_w_q_up" in out:
        out["mla_w_q_up"] = _pad_q_up(out["mla_w_q_up"])
    return out


def _layer_parts(names, gathered):
    return {k: g.reshape((N_CHIPS,) + SHARD_SHAPE[k][1:]) for k, g in zip(names, gathered)}


def _rope_tables(n):
    half = MLA_ROPE // 2
    freqs = ROPE_THETA ** (-jnp.arange(half, dtype=F32) / half)
    ang = jnp.arange(n).astype(F32)[:, None] * freqs[None, :]
    cos, sin = jnp.cos(ang), jnp.sin(ang)
    one, zero = jnp.ones((n, MLA_NOPE), F32), jnp.zeros((n, MLA_NOPE), F32)
    tail1, tail0 = jnp.ones((n, LANES - MLA_QK), F32), jnp.zeros((n, LANES - MLA_QK), F32)
    return (jnp.concatenate([one, cos, cos, tail1], axis=1), jnp.concatenate([zero, sin, sin, tail0], axis=1))


def _pad_lanes(g, width):
    return jnp.concatenate([g, jnp.zeros((width - g.shape[0],), g.dtype)]).reshape(1, width)


PROJ_SEGMENTS = ((O_FQ, 512), (O_FK, 512), (O_FV, 512), (O_FF, HEADS), (O_CQ, MLA_Q_RANK), (O_CKV, MLA_KV_RANK),
                 (O_KR, MLA_ROPE), (O_SQ, 512), (O_SK, 128), (O_SV, 128), (O_G, D_MODEL), (O_G + D_MODEL, D_MODEL),
                 (O_G + 2 * D_MODEL, D_MODEL))


def _join_proj_call(parts, name):
    n = parts[0].shape[0]
    tr = _rows_tile(n, IN_PAD, 4 << 20)

    def body(*refs):
        o_ref = refs[-1]
        o_ref[...] = jnp.zeros_like(o_ref)
        for (s, w), r in zip(PROJ_SEGMENTS, refs[:-1]):
            o_ref[:, s:s + w] = r[...]

    return pl.pallas_call(
        body, name=name, grid=(n // tr,), in_specs=[BS((tr, w), lambda i: (i, 0)) for _, w in PROJ_SEGMENTS],
        out_specs=BS((tr, IN_PAD), lambda i: (i, 0)), out_shape=SDS((n, IN_PAD), F32),
        compiler_params=_cparams(("parallel",)))(*parts)


def _split_proj(proj, name):
    def fwd(x):
        return tuple(x[:, s:s + w] for s, w in PROJ_SEGMENTS), None

    def bwd(_, cts):
        return (_join_proj_call(cts, name + "_b"),)

    return _op(fwd, bwd)(proj)


def _trunk(eps, eps_cw, sinks, meta, small, x, w0, late):
    assert DEPTH == 2
    seq = x.shape[0]
    n = -(-(N_META + seq) // ROW_PAD) * ROW_PAD
    ew = _assemble_layer(eps)
    cos, sin = _rope_tables(n)
    slopes = jnp.exp2(-8.0 * jnp.arange(1, HEADS + 1, dtype=F32) / HEADS)
    h = jnp.concatenate([meta, x, jnp.zeros((n - N_META - seq, D_MODEL), F32)], axis=0)
    wb = w0
    for l in range(DEPTH):
        p = f"l{l}_"
        row = lambda name: small[name][l].reshape(1, -1)
        xn = rms_norm(h, row("norm1_g"), D_MODEL, p + "norm1")
        proj = linear(xn, wb["w_in"], ew["w_in"], p + "win")
        p_fq, p_fk, p_fv, p_ff, p_cq, p_ckv, p_kr, p_sq, p_sk, p_sv, g0, g1, g2 = _split_proj(proj, p + "split")
        fq = _head_norm(_to_heads(p_fq, HEADS), row("fox_q_g"), HEAD_DIM, p + "fqn")
        fk = _head_norm(_to_heads(p_fk, HEADS), row("fox_k_g"), HEAD_DIM, p + "fkn")
        fv = _to_heads(p_fv, HEADS)
        c = forget_cumsum(p_ff.T, small["fox_forget_b"][l].reshape(HEADS, 1), p + "fgate")
        if l == 0:
            rest = [a for k, a in zip(LAYERED, late) if k != "w_in"]
            out_a, got, carriers = causal_attention(fq, fk, fv, c, HEAD_DIM ** -0.5, p + "fox", rest, 0, BEFORE_MLA,
                                                    sinks[1])
            wb = dict(wb, **_assemble_layer(_layer_parts(LAYERED[1:], _fill_own(got, [a[0] for a in rest]))))
            ew1 = _assemble_layer(dict(zip(BEFORE_MLA, carriers)))
        else:
            out_a = causal_attention(fq, fk, fv, c, HEAD_DIM ** -0.5, p + "fox")
        cqn = rms_norm(p_cq, row("mla_q_a_g"), MLA_Q_RANK, p + "cqn")
        q = _to_heads(linear(cqn, wb["mla_w_q_up"], ew["mla_w_q_up"], p + "qup"), HEADS)
        q = _head_norm_rope(q, _pad_lanes(small["mla_q_g"][l], LANES), cos, sin, p + "mqn")
        ckvn = rms_norm(p_ckv, row("mla_kv_a_g"), MLA_KV_RANK, p + "ckvn")
        kv = _to_heads(linear(ckvn, wb["mla_w_kv_up"], ew["mla_w_kv_up"], p + "kvup"), HEADS)
        kr = jnp.broadcast_to(p_kr[None], (HEADS, n, MLA_ROPE))
        k = jnp.concatenate([kv[..., :MLA_NOPE], kr, jnp.zeros((HEADS, n, LANES - MLA_QK), F32)], axis=-1)
        k = _head_norm_rope(k, _pad_lanes(small["mla_k_g"][l], LANES), cos, sin, p + "mkn")
        if l == 0:
            out_b, got, carriers = causal_attention(q, k, kv[..., MLA_NOPE:], None, MLA_QK ** -0.5, p + "mla", late,
                                                    1, AFTER_MLA, sinks[0])
            w1 = _assemble_layer(_layer_parts(LAYERED, _fill_own(got, [a[1] for a in late])))
        else:
            out_b, carriers = causal_attention(q, k, kv[..., MLA_NOPE:], None, MLA_QK ** -0.5, p + "mla", None, 1,
                                               AFTER_MLA, sinks[2])
        ew = dict(ew, **_assemble_layer(dict(zip(AFTER_MLA, carriers))))
        sq = _head_norm(_to_heads(p_sq, HEADS), row("swa_q_g"), HEAD_DIM, p + "sqn")
        sk = _head_norm(_to_heads(p_sk, SWA_KV_HEADS), row("swa_k_g"), HEAD_DIM, p + "skn")
        sv = _to_heads(p_sv, SWA_KV_HEADS)
        out_c = window_attention(sq, sk, sv, small["swa_sinks"][l], slopes, HEAD_DIM ** -0.5, p + "swa")
        ys = [linear(_from_heads(o), wb["w_branch"][i], ew["w_branch"][i], p + f"br{i}")
              for i, o in enumerate((out_a, out_b, out_c))]
        merged = gated_merge([g0, g1, g2], ys, p + "merge")
        h = linear(merged, wb["w_o"], ew["w_o"], p + "wo", res=h)
        xn2 = rms_norm(h, row("norm2_g"), D_MODEL, p + "norm2")
        u = linear(xn2, wb["ffn_w_up"], ew["ffn_w_up"], p + "wup")
        act = conv_glu(u, lax.stop_gradient(wb["ffn_conv_w"]) + eps_cw[l], row("ffn_conv_b"), p + "conv")
        h = linear(act, wb["ffn_w_down"], ew["ffn_w_down"], p + "wdown", res=h)
        wb, ew = w1, ew1
    return h


def _local_step(x, target, w0, late, meta, small):
    seq = x.shape[0]
    eps = {k: jnp.zeros((N_CHIPS,) + SHARD_SHAPE[k][1:], F32) for k in BEFORE_MLA}
    eps_cw = jnp.zeros((DEPTH, 3, 2 * D_FF), F32)
    half = lambda k: jnp.zeros((HALVED[k][1] // 2, HALVED[k][2]), F32)
    sinks = tuple([(half(k), half(k)) for k in names] for names in (AFTER_MLA, BEFORE_MLA, AFTER_MLA))
    y, vjp = jax.vjp(lambda e, ec, sk, mt, s, xx: _trunk(e, ec, sk, mt, s, xx, w0, late),
                     eps, eps_cw, sinks, meta, small, x)
    n = y.shape[0]
    tpad = jnp.concatenate([jnp.zeros((N_META, D_MODEL), F32), target, jnp.zeros((n - N_META - seq, D_MODEL), F32)])
    dy, loss = _loss_call(y, tpad, seq, "loss")
    g_eps, g_cw, g_l1, g_meta, g_small, g_x = vjp(dy)
    return loss[0, 0], g_x, g_eps, g_l1, g_cw, g_meta, g_small


def _pack_rows(shapes, mult):
    total = sum(_size(s) for s in shapes)
    rows = -(-total // LANES)
    return -(-rows // mult) * mult


def _size(shape):
    n = 1
    for d in shape:
        n *= d
    return n


def _pack(arrs, rows, dtype):
    flat = [a.reshape(-1).astype(dtype) for a in arrs]
    used = sum(a.size for a in flat)
    flat.append(jnp.zeros((rows * LANES - used,), dtype))
    return jnp.concatenate(flat).reshape(rows, LANES)


def _unpack(p, shapes):
    flat = p.reshape(-1)
    out, off = [], 0
    for s in shapes:
        out.append(flat[off:off + _size(s)].reshape(s))
        off += _size(s)
    return out


MESH = pl.DeviceIdType.MESH
ANY = pl.BlockSpec(memory_space=pl.ANY)


def _me():
    return lax.axis_index("x"), lax.axis_index("y"), lax.axis_index("c")


def _remote(src, dst, send_sems, recv_sems, idx, dev):
    return pltpu.make_async_remote_copy(src_ref=src, dst_ref=dst, send_sem=send_sems.at[idx], recv_sem=recv_sems.at[idx],
                                        device_id=dev, device_id_type=MESH)


def _layer_gather(ins, outs, send_sems, recv_sems, layer, base):
    x, y, _ = _me()
    j = 2 * x + y
    sibling = (x, y, 1 - layer)
    chips = [(1 - x, y), (x, 1 - y), (1 - x, 1 - y)]

    def ici(p, r):
        cx, cy = chips[r]
        return _remote(ins[p].at[layer], outs[p].at[j], send_sems, recv_sems, base + 6 * p + r, (cx, cy, layer))

    def d2d(p, r):
        cx, cy = chips[r]
        blk = outs[p].at[2 * cx + cy]
        return _remote(blk, blk, send_sems, recv_sems, base + 6 * p + 3 + r, sibling)

    pairs = [(p, r) for p in range(len(ins)) for r in range(3)]

    def start():
        for p, r in pairs:
            ici(p, r).start()

    def forward():
        for p, r in pairs:
            ici(p, r).wait_recv()
            d2d(p, r).start()

    def drain():
        for p, r in pairs:
            ici(p, r).wait_send()
            d2d(p, r).wait_send()

    def receive():
        for p, r in pairs:
            d2d(p, r).wait_recv()

    return start, forward, drain, receive


def _fill_own(outs, own):
    j = 2 * lax.axis_index("x") + lax.axis_index("y")
    return [lax.dynamic_update_index_in_dim(o, a, j, 0) for o, a in zip(outs, own)]


def _gather_early(meta, arrs, name):
    npk = len(arrs)

    def body(*refs):
        m_in, ins = refs[0], refs[1:npk + 1]
        m_out, outs = refs[npk + 1], refs[npk + 2:2 * npk + 2]
        send_sems, recv_sems = refs[2 * npk + 2:]
        x, y, c = _me()
        j = 2 * x + y
        sibling = (x, y, 1 - c)
        chips = [(1 - x, y), (x, 1 - y), (1 - x, 1 - y)]
        start, forward, drain, receive = _layer_gather(ins, outs, send_sems, recv_sems, 0, 6)
        sends = []
        for r, (cx, cy) in enumerate(chips):
            cp = _remote(m_in.at[c], m_out.at[j, c], send_sems, recv_sems, r, (cx, cy, c))
            cp.start()
            sends.append(cp)
        pl.when(c == 0)(start)
        for r, (cx, cy) in enumerate(chips):
            blk = m_out.at[2 * cx + cy, c]
            _remote(blk, blk, send_sems, recv_sems, r, sibling).wait_recv()
            fw = _remote(blk, blk, send_sems, recv_sems, 3 + r, sibling)
            fw.start()
            sends.append(fw)
        for r, (cx, cy) in enumerate(chips):
            blk = m_out.at[2 * cx + cy, 1 - c]
            _remote(blk, blk, send_sems, recv_sems, 3 + r, sibling).wait_recv()
        for cp in sends:
            cp.wait_send()

        @pl.when(c == 0)
        def _():
            forward()
            drain()

        pl.when(c == 1)(receive)

    nsem = 6 + 6 * npk
    res = pl.pallas_call(
        body, name=name, in_specs=[ANY] * (npk + 1), out_specs=[ANY] * (npk + 1),
        out_shape=[SDS((N_CHIPS,) + meta.shape, meta.dtype)] + [SDS((N_CHIPS,) + a.shape[1:], a.dtype) for a in arrs],
        scratch_shapes=[pltpu.SemaphoreType.DMA((nsem,)), pltpu.SemaphoreType.DMA((nsem,))],
        compiler_params=pltpu.CompilerParams(has_side_effects=True))(meta, *arrs)
    return _fill_own(res[:1], [meta])[0], _fill_own(res[1:], [a[0] for a in arrs])


def _pair_exchange(gs, name):
    npk = len(gs)

    def body(*refs):
        ins, outs = refs[:npk], refs[npk:2 * npk]
        send_sems, recv_sems = refs[2 * npk:]
        x, y, c = _me()
        cps = [_remote(ins[p].at[:, 1 - c], outs[p], send_sems, recv_sems, p, (x, y, 1 - c)) for p in range(npk)]
        for cp in cps:
            cp.start()
        for cp in cps:
            cp.wait()

    return pl.pallas_call(
        body, name=name, in_specs=[ANY] * npk, out_specs=[ANY] * npk,
        out_shape=[SDS(g.shape[:1] + g.shape[2:], g.dtype) for g in gs],
        scratch_shapes=[pltpu.SemaphoreType.DMA((npk,)), pltpu.SemaphoreType.DMA((npk,))],
        compiler_params=pltpu.CompilerParams(has_side_effects=True))(*gs)


def _chip_exchange(ss, small, name):
    npk = len(ss)

    def body(*refs):
        ins, sm_ref = refs[:npk], refs[npk]
        outs, sa_ref = refs[npk + 1:2 * npk + 1], refs[2 * npk + 1]
        send_sems, recv_sems, loc_sem = refs[2 * npk + 2:]
        x, y, c = _me()
        me = 4 * x + 2 * y + c
        lc = pltpu.make_async_copy(sm_ref, sa_ref.at[me], loc_sem.at[0])
        lc.start()
        cps = []
        for p in range(npk):
            for r, (cx, cy) in enumerate([(1 - x, y), (x, 1 - y), (1 - x, 1 - y)]):
                cp = _remote(ins[p].at[2 * cx + cy], outs[p].at[r], send_sems, recv_sems, 3 * p + r, (cx, cy, c))
                cp.start()
                cps.append(cp)
        base = 3 * npk - 1
        for mask in range(1, N_DEV):
            px, py, pc = x ^ (mask >> 2), y ^ ((mask >> 1) & 1), c ^ (mask & 1)
            cp = _remote(sm_ref, sa_ref.at[me], send_sems, recv_sems, base + mask, (px, py, pc))
            cp.start()
            cps.append(cp)
        for p in range(npk):
            for r in range(3):
                _remote(outs[p].at[r], outs[p].at[r], send_sems, recv_sems, 3 * p + r, (x, y, c)).wait_recv()
        for mask in range(1, N_DEV):
            src = 4 * (x ^ (mask >> 2)) + 2 * (y ^ ((mask >> 1) & 1)) + (c ^ (mask & 1))
            _remote(sa_ref.at[src], sa_ref.at[src], send_sems, recv_sems, base + mask, (x, y, c)).wait_recv()
        for cp in cps:
            cp.wait_send()
        lc.wait()

    nsem = 3 * npk + N_DEV - 1
    res = pl.pallas_call(
        body, name=name, in_specs=[ANY] * (npk + 1), out_specs=[ANY] * (npk + 1),
        out_shape=[SDS((3,) + s.shape[1:], s.dtype) for s in ss] + [SDS((N_DEV,) + small.shape, small.dtype)],
        scratch_shapes=[pltpu.SemaphoreType.DMA((nsem,)), pltpu.SemaphoreType.DMA((nsem,)),
                        pltpu.SemaphoreType.DMA((1,))],
        compiler_params=pltpu.CompilerParams(has_side_effects=True))(*ss, small)
    return res[:npk], res[npk]


def _half_exchange(ghs, name):
    npk = len(ghs)

    def body(*refs):
        ins, outs = refs[:npk], refs[npk:2 * npk]
        send_sems, recv_sems = refs[2 * npk:]
        x, y, c = _me()
        cps = [_remote(ins[p], outs[p], send_sems, recv_sems, p, (x, y, 1 - c)) for p in range(npk)]
        for cp in cps:
            cp.start()
        for cp in cps:
            cp.wait()

    return pl.pallas_call(
        body, name=name, in_specs=[ANY] * npk, out_specs=[ANY] * npk, out_shape=[SDS(g.shape, g.dtype) for g in ghs],
        scratch_shapes=[pltpu.SemaphoreType.DMA((npk,)), pltpu.SemaphoreType.DMA((npk,))],
        compiler_params=pltpu.CompilerParams(has_side_effects=True))(*ghs)


def _add_tile(rows, cols):
    return _div_tile(rows, max(16, (1 << 19) // max(cols, LANES)), 16)


def _pair_add(g, r1, c_idx, name):
    _, rows, cols = r1.shape
    tr = _add_tile(rows, cols)

    def body(c_ref, g_ref, r_ref, o_ref, ob_ref):
        s = g_ref[0] + r_ref[...]
        o_ref[...] = s
        ob_ref[...] = s.astype(BF16)

    own = BS((1, tr, cols), lambda k, i, c: (k, i, 0))
    return pl.pallas_call(
        body, name=name,
        grid_spec=pltpu.PrefetchScalarGridSpec(
            num_scalar_prefetch=1, grid=(r1.shape[0], rows // tr),
            in_specs=[BS((1, 1, tr, cols), lambda k, i, c: (k, c[0], i, 0)), own], out_specs=[own, own]),
        out_shape=[SDS(r1.shape, F32), SDS(r1.shape, BF16)],
        compiler_params=_cparams(("parallel", "parallel")))(c_idx, g, r1)


def _chip_add(s1, r2, j_idx, name):
    _, rows, cols = s1.shape
    tr = _add_tile(rows, cols)

    def body(j_ref, s_ref, r_ref, o_ref):
        o_ref[...] = ((s_ref[0] + r_ref[0].astype(F32)) + r_ref[1].astype(F32)) + r_ref[2].astype(F32)

    return pl.pallas_call(
        body, name=name,
        grid_spec=pltpu.PrefetchScalarGridSpec(
            num_scalar_prefetch=1, grid=(rows // tr,),
            in_specs=[BS((1, tr, cols), lambda i, j: (j[0], i, 0)), BS((3, tr, cols), lambda i, j: (0, i, 0))],
            out_specs=BS((tr, cols), lambda i, j: (i, 0))),
        out_shape=SDS((rows, cols), F32), compiler_params=_cparams(("parallel",)))(j_idx, s1, r2)


def _adamw_math(w, g, m, v):
    m = ADAM_B1 * m + (1.0 - ADAM_B1) * g
    v = ADAM_B2 * v + (1.0 - ADAM_B2) * (g * g)
    m_hat = m / (1.0 - ADAM_B1 ** ADAM_STEP)
    v_hat = v / (1.0 - ADAM_B2 ** ADAM_STEP)
    delta = -ADAM_LR * (m_hat / (jnp.sqrt(v_hat) + ADAM_EPS) + ADAM_WD * w)
    return delta, m, v


def _adamw(w, gh, go, m, v, c_idx, name):
    _, rows, cols = w.shape
    tr = _add_tile(rows, cols)

    def body(c_ref, w_ref, gh_ref, go_ref, m_ref, v_ref, g_out, d_out, m_out, v_out):
        g = jnp.where(pl.program_id(0) == c_ref[0], gh_ref[...], go_ref[...])
        g_out[0] = g
        d_out[0], m_out[0], v_out[0] = _adamw_math(w_ref[0], g, m_ref[0], v_ref[0])

    full = BS((1, tr, cols), lambda hf, i, c: (hf, i, 0))
    half = BS((tr, cols), lambda hf, i, c: (i, 0))
    return pl.pallas_call(
        body, name=name,
        grid_spec=pltpu.PrefetchScalarGridSpec(
            num_scalar_prefetch=1, grid=(2, rows // tr), in_specs=[full, half, half, full, full],
            out_specs=[full] * 4),
        out_shape=[SDS(w.shape, F32)] * 4, compiler_params=_cparams(("parallel", "parallel")))(c_idx, w, gh, go, m, v)


def _adamw_layers(w, g0, g1, m, v, c_idx, name):
    _, _, rows, cols = w.shape
    tr = _add_tile(rows, cols)

    def body(c_ref, w_ref, h0_ref, o0_ref, h1_ref, o1_ref, m_ref, v_ref, g_out, d_out, m_out, v_out):
        mine = pl.program_id(1) == c_ref[0]
        g = jnp.where(pl.program_id(0) == 0, jnp.where(mine, h0_ref[...], o0_ref[...]),
                      jnp.where(mine, h1_ref[...], o1_ref[...]))
        g_out[0, 0] = g
        d_out[0, 0], m_out[0, 0], v_out[0, 0] = _adamw_math(w_ref[0, 0], g, m_ref[0, 0], v_ref[0, 0])

    full = BS((1, 1, tr, cols), lambda l, hf, i, c: (l, hf, i, 0))
    half = BS((tr, cols), lambda l, hf, i, c: (i, 0))
    return pl.pallas_call(
        body, name=name,
        grid_spec=pltpu.PrefetchScalarGridSpec(
            num_scalar_prefetch=1, grid=(2, 2, rows // tr), in_specs=[full, half, half, half, half, full, full],
            out_specs=[full] * 4),
        out_shape=[SDS(w.shape, F32)] * 4,
        compiler_params=_cparams(("parallel", "parallel", "parallel")))(c_idx, w, *g0, *g1, m, v)


def _sum_devices(sa, name):
    def body(sa_ref, g_out):
        g = sa_ref[0]
        for d in range(1, N_DEV):
            g = g + sa_ref[d]
        g_out[...] = g

    return pl.pallas_call(body, name=name, out_shape=SDS(sa.shape[1:], F32),
                          compiler_params=pltpu.CompilerParams(vmem_limit_bytes=VMEM_LIMIT))(sa)


def _adamw_small(ws, gs, ms, vs, name):
    k = len(ws)

    def body(*refs):
        ins, outs = refs[:4 * k], refs[4 * k:]
        for i in range(k):
            d, m, v = _adamw_math(ins[i][...], ins[k + i][...], ins[2 * k + i][...], ins[3 * k + i][...])
            outs[i][...], outs[k + i][...], outs[2 * k + i][...] = d, m, v

    return pl.pallas_call(body, name=name, out_shape=[SDS(w.shape, F32) for w in ws] * 3,
                          compiler_params=pltpu.CompilerParams(vmem_limit_bytes=VMEM_LIMIT))(*ws, *gs, *ms, *vs)


HALVED = {"w_in": (2, 1024, 1450), "mla_w_q_up": (2, 256, 192), "mla_w_kv_up": (2, 128, 256),
          "w_branch": (2, 1536, 256), "w_o": (2, 256, 1024), "ffn_w_up": (2, 1024, 1408),
          "ffn_w_down": (2, 704, 1024), "ffn_conv_w": (2, 3, 1408), "meta_tokens": (2, 8, 256)}
SMALL_SHAPE = {"norm1_g": (2, 1024), "fox_forget_b": (2, 8), "fox_q_g": (2, 64), "fox_k_g": (2, 64),
               "mla_q_a_g": (2, 256), "mla_kv_a_g": (2, 128), "mla_q_g": (2, 96), "mla_k_g": (2, 96),
               "swa_q_g": (2, 64), "swa_k_g": (2, 64), "swa_sinks": (2, 8), "norm2_g": (2, 1024),
               "ffn_conv_b": (2, 5632)}
SMALL_ROWS = _pack_rows([SMALL_SHAPE[k] for k in SMALL] + [(1,)], SUBLANES)


def kernel(x, meta_tokens, norm1_g, w_in, fox_forget_b, fox_q_g, fox_k_g, mla_q_a_g, mla_w_q_up, mla_kv_a_g, mla_w_kv_up, mla_q_g, mla_k_g, swa_q_g, swa_k_g, swa_sinks, w_branch, w_o, norm2_g, ffn_w_up, ffn_conv_w, ffn_conv_b, ffn_w_down, loss_target, m_meta_tokens, m_norm1_g, m_w_in, m_fox_forget_b, m_fox_q_g, m_fox_k_g, m_mla_q_a_g, m_mla_w_q_up, m_mla_kv_a_g, m_mla_w_kv_up, m_mla_q_g, m_mla_k_g, m_swa_q_g, m_swa_k_g, m_swa_sinks, m_w_branch, m_w_o, m_norm2_g, m_ffn_w_up, m_ffn_conv_w, m_ffn_conv_b, m_ffn_w_down, v_meta_tokens, v_norm1_g, v_w_in, v_fox_forget_b, v_fox_q_g, v_fox_k_g, v_mla_q_a_g, v_mla_w_q_up, v_mla_kv_a_g, v_mla_w_kv_up, v_mla_q_g, v_mla_k_g, v_swa_q_g, v_swa_k_g, v_swa_sinks, v_w_branch, v_w_o, v_norm2_g, v_ffn_w_up, v_ffn_conv_w, v_ffn_conv_b, v_ffn_w_down):
    w = dict(meta_tokens=meta_tokens, norm1_g=norm1_g, w_in=w_in, fox_forget_b=fox_forget_b, fox_q_g=fox_q_g,
             fox_k_g=fox_k_g, mla_q_a_g=mla_q_a_g, mla_w_q_up=mla_w_q_up, mla_kv_a_g=mla_kv_a_g,
             mla_w_kv_up=mla_w_kv_up, mla_q_g=mla_q_g, mla_k_g=mla_k_g, swa_q_g=swa_q_g, swa_k_g=swa_k_g,
             swa_sinks=swa_sinks, w_branch=w_branch, w_o=w_o, norm2_g=norm2_g, ffn_w_up=ffn_w_up,
             ffn_conv_w=ffn_conv_w, ffn_conv_b=ffn_conv_b, ffn_w_down=ffn_w_down)
    m = dict(meta_tokens=m_meta_tokens, norm1_g=m_norm1_g, w_in=m_w_in, fox_forget_b=m_fox_forget_b,
             fox_q_g=m_fox_q_g, fox_k_g=m_fox_k_g, mla_q_a_g=m_mla_q_a_g, mla_w_q_up=m_mla_w_q_up,
             mla_kv_a_g=m_mla_kv_a_g, mla_w_kv_up=m_mla_w_kv_up, mla_q_g=m_mla_q_g, mla_k_g=m_mla_k_g,
             swa_q_g=m_swa_q_g, swa_k_g=m_swa_k_g, swa_sinks=m_swa_sinks, w_branch=m_w_branch, w_o=m_w_o,
             norm2_g=m_norm2_g, ffn_w_up=m_ffn_w_up, ffn_conv_w=m_ffn_conv_w, ffn_conv_b=m_ffn_conv_b,
             ffn_w_down=m_ffn_w_down)
    v = dict(meta_tokens=v_meta_tokens, norm1_g=v_norm1_g, w_in=v_w_in, fox_forget_b=v_fox_forget_b,
             fox_q_g=v_fox_q_g, fox_k_g=v_fox_k_g, mla_q_a_g=v_mla_q_a_g, mla_w_q_up=v_mla_w_q_up,
             mla_kv_a_g=v_mla_kv_a_g, mla_w_kv_up=v_mla_w_kv_up, mla_q_g=v_mla_q_g, mla_k_g=v_mla_k_g,
             swa_q_g=v_swa_q_g, swa_k_g=v_swa_k_g, swa_sinks=v_swa_sinks, w_branch=v_w_branch, w_o=v_w_o,
             norm2_g=v_norm2_g, ffn_w_up=v_ffn_w_up, ffn_conv_w=v_ffn_conv_w, ffn_conv_b=v_ffn_conv_b,
             ffn_w_down=v_ffn_w_down)
    xi, yi, ci = _me()
    c_idx = ci.astype(jnp.int32).reshape(1)
    j_idx = (2 * xi + yi).astype(jnp.int32).reshape(1)

    sh_names = BIG + FINE
    local = {k: (w[k].astype(BF16) if k in BIG else w[k]).reshape(HALVED[k]) for k in sh_names}
    late = [local[k] for k in LAYERED]
    meta_g, early = _gather_early(local["meta_tokens"], [local["w_in"]], "gather_early")
    meta = jnp.concatenate([meta_g[i].reshape(SHARD_SHAPE["meta_tokens"]) for i in range(N_CHIPS)], axis=1)
    w0 = _assemble_layer(_layer_parts(("w_in",), early))
    small = {k: w[k] for k in SMALL}

    loss, g_x, g_eps, g_red, g_cw, g_meta, g_small = _local_step(x[0], loss_target[0], w0, late, meta, small)
    g_after0 = dict(zip(AFTER_MLA, g_red[0]))
    g_l1 = dict(zip(BEFORE_MLA + AFTER_MLA, g_red[1] + g_red[2]))

    quarter = {k: (2, HALVED[k][1] // 2, HALVED[k][2]) for k in BIG}
    last = BEFORE_MLA + FINE
    gs = [g_eps[k].reshape((N_CHIPS,) + quarter[k]) for k in BEFORE_MLA]
    for k, g in (("meta_tokens", g_meta), ("ffn_conv_w", g_cw)):
        gs.append(jnp.stack(jnp.split(g, N_CHIPS, axis=SHARD_AXIS[k])).reshape((N_CHIPS,) + HALVED[k]))
    spack = _pack([g_small[k] for k in SMALL] + [loss.reshape(1)], SMALL_ROWS, F32)
    r1 = _pair_exchange(gs, "grads_pair_exchange")
    s1 = [_pair_add(g, r, c_idx, "grads_pair_add_" + k) for g, r, k in zip(gs, r1, last)]
    r2, sa = _chip_exchange([s[1] for s in s1], spack, "grads_chip_exchange")
    gh = dict(zip(last, [_chip_add(s[0], r, j_idx, "grads_chip_add_" + k) for s, r, k in zip(s1, r2, last)]))
    go = dict(zip(last, _half_exchange([gh[k] for k in last], "grads_half_exchange")))

    grads, deltas, new_m, new_v = {}, {}, {}, {}
    for k in sh_names:
        if k in BIG:
            shp = (2,) + quarter[k]
            wk, mk, vk = w[k].reshape(shp), m[k].reshape(shp), v[k].reshape(shp)
            g0 = (gh[k], go[k]) if k in BEFORE_MLA else g_after0[k]
            outs = _adamw_layers(wk, g0, g_l1[k], mk, vk, c_idx, "adamw_" + k)
        else:
            outs = _adamw(w[k].reshape(HALVED[k]), gh[k], go[k], m[k].reshape(HALVED[k]), v[k].reshape(HALVED[k]),
                          c_idx, "adamw_" + k)
        for dst, o in zip((grads, deltas, new_m, new_v), outs):
            dst[k] = o.reshape(SHARD_SHAPE[k])
    sm_shapes = [SMALL_SHAPE[k] for k in SMALL] + [(1,)]
    g_sum = _unpack(_sum_devices(sa, "sum_small"), sm_shapes)
    res = _adamw_small([w[k] for k in SMALL], g_sum[:-1], [m[k] for k in SMALL], [v[k] for k in SMALL], "adamw_small")
    ns = len(SMALL)
    grads.update(zip(SMALL, g_sum[:-1]))
    for dst, vals in zip((deltas, new_m, new_v), (res[:ns], res[ns:2 * ns], res[2 * ns:])):
        dst.update(zip(SMALL, vals))
    total_loss = g_sum[-1][0]
    return (total_loss, g_x[None], *[grads[k] for k in WEIGHTS], *[deltas[k] for k in WEIGHTS],
            *[new_m[k] for k in WEIGHTS], *[new_v[k] for k in WEIGHTS])
```

```python
import jax
import jax.numpy as jnp
from jax import lax
from jax.experimental import pallas as pl
from jax.experimental.pallas import tpu as pltpu

F32 = jnp.float32
BF16 = jnp.bfloat16
SDS = jax.ShapeDtypeStruct
BS = pl.BlockSpec

D_MODEL = 1024
DEPTH = 2
N_META = 16
EPS = 1e-6
HEADS = 8
HEAD_DIM = 64
MLA_Q_RANK = 256
MLA_KV_RANK = 128
MLA_NOPE = 64
MLA_ROPE = 32
MLA_QK = MLA_NOPE + MLA_ROPE
ROPE_THETA = 10000.0
SWA_KV_HEADS = 2
WINDOW = 128
D_FF = 2816
IN_PAD = 6144
N_CHIPS = 4
N_DEV = 8

ADAM_LR = 0.001
ADAM_B1 = 0.9
ADAM_B2 = 0.999
ADAM_EPS = 1e-08
ADAM_WD = 0.01
ADAM_STEP = 10

LANES = 128
SUBLANES = 8
ROW_PAD = 128
CAUSAL_TILE = 384
NEG = -1e30
VMEM_LIMIT = 56 * 1024 * 1024

O_FQ, O_FK, O_FV, O_FF = 0, 512, 1024, 1536
O_CQ, O_CKV, O_KR = 1664, 1920, 2048
O_SQ, O_SK, O_SV, O_G = 2176, 2688, 2816, 2944

SHARD_AXIS = {"meta_tokens": 1, "w_in": 2, "mla_w_q_up": 2, "mla_w_kv_up": 2, "w_branch": 3, "w_o": 1,
              "ffn_w_up": 2, "ffn_conv_w": 2, "ffn_w_down": 1}
SHARD_SHAPE = {"meta_tokens": (16, 256), "w_in": (2, 1024, 1450), "mla_w_q_up": (2, 256, 192),
               "mla_w_kv_up": (2, 128, 256), "w_branch": (2, 3, 512, 256), "w_o": (2, 256, 1024),
               "ffn_w_up": (2, 1024, 1408), "ffn_conv_w": (2, 3, 1408), "ffn_w_down": (2, 704, 1024)}
BIG = ("w_in", "mla_w_q_up", "mla_w_kv_up", "w_branch", "w_o", "ffn_w_up", "ffn_w_down")
FINE = ("meta_tokens", "ffn_conv_w")
SMALL = ("norm1_g", "fox_forget_b", "fox_q_g", "fox_k_g", "mla_q_a_g", "mla_kv_a_g", "mla_q_g", "mla_k_g",
         "swa_q_g", "swa_k_g", "swa_sinks", "norm2_g", "ffn_conv_b")
WEIGHTS = ("meta_tokens", "norm1_g", "w_in", "fox_forget_b", "fox_q_g", "fox_k_g", "mla_q_a_g", "mla_w_q_up",
           "mla_kv_a_g", "mla_w_kv_up", "mla_q_g", "mla_k_g", "swa_q_g", "swa_k_g", "swa_sinks", "w_branch", "w_o",
           "norm2_g", "ffn_w_up", "ffn_conv_w", "ffn_conv_b", "ffn_w_down")


def _cparams(sem):
    return pltpu.CompilerParams(dimension_semantics=sem, vmem_limit_bytes=VMEM_LIMIT)


def _div_tile(n, cap, mult=SUBLANES):
    best = None
    for t in range(mult, min(n, cap) + 1, mult):
        if n % t == 0:
            best = t
    return best if best is not None else n


def _rows_tile(n, width, budget=2 << 20):
    return _div_tile(n, max(SUBLANES, budget // (4 * max(width, LANES))))


def _op(fwd, bwd):
    @jax.custom_vjp
    def op(*args):
        return fwd(*args)[0]
    op.defvjp(fwd, bwd)
    return op


def _rotate(v, cos, sin):
    lane = lax.broadcasted_iota(jnp.int32, v.shape, 1)
    rot = jnp.where(lane < MLA_NOPE + MLA_ROPE // 2, -pltpu.roll(v, LANES - MLA_ROPE // 2, 1),
                    pltpu.roll(v, MLA_ROPE // 2, 1))
    return v * cos + rot * sin


def _rms_fwd_call(x, g, denom, name, rot=None):
    n, c = x.shape
    tr = _rows_tile(n if rot is None else rot[0].shape[0], c)
    nt = None if rot is None else rot[0].shape[0] // tr

    def body(x_ref, g_ref, *rest):
        y_ref = rest[-1]
        xv = x_ref[...]
        ms = jnp.sum(xv * xv, axis=-1, keepdims=True) * (1.0 / denom)
        y = xv * lax.rsqrt(ms + EPS) * g_ref[...]
        y_ref[...] = y if rot is None else _rotate(y, rest[0][...], rest[1][...])

    ins, args = [BS((tr, c), lambda i: (i, 0)), BS((1, c), lambda i: (0, 0))], [x, g]
    if rot is not None:
        ins += [BS((tr, c), lambda i: (i % nt, 0))] * 2
        args += list(rot)
    return pl.pallas_call(
        body, name=name, grid=(n // tr,), in_specs=ins,
        out_specs=BS((tr, c), lambda i: (i, 0)), out_shape=SDS((n, c), F32),
        compiler_params=_cparams(("parallel",)))(*args)


def _rms_bwd_call(x, g, dy, denom, name, rot=None):
    n, c = x.shape
    tr = _rows_tile(n if rot is None else rot[0].shape[0], c)
    nt = None if rot is None else rot[0].shape[0] // tr

    def body(x_ref, g_ref, dy_ref, *rest):
        dx_ref, dg_ref = rest[-2:]
        xv = x_ref[...]
        dy = dy_ref[...]
        if rot is not None:
            dy = _rotate(dy, rest[0][...], -rest[1][...])
        ms = jnp.sum(xv * xv, axis=-1, keepdims=True) * (1.0 / denom)
        r = lax.rsqrt(ms + EPS)
        xh = xv * r
        dxh = dy * g_ref[...]
        dx_ref[...] = r * (dxh - xh * (jnp.sum(dxh * xh, axis=-1, keepdims=True) * (1.0 / denom)))

        @pl.when(pl.program_id(0) == 0)
        def _():
            dg_ref[...] = jnp.zeros_like(dg_ref)

        dg_ref[...] += jnp.sum(dy * xh, axis=0, keepdims=True)

    ins = [BS((tr, c), lambda i: (i, 0)), BS((1, c), lambda i: (0, 0)), BS((tr, c), lambda i: (i, 0))]
    args = [x, g, dy]
    if rot is not None:
        ins += [BS((tr, c), lambda i: (i % nt, 0))] * 2
        args += list(rot)
    return pl.pallas_call(
        body, name=name, grid=(n // tr,), in_specs=ins,
        out_specs=[BS((tr, c), lambda i: (i, 0)), BS((1, c), lambda i: (0, 0))],
        out_shape=[SDS((n, c), F32), SDS((1, c), F32)],
        compiler_params=_cparams(("arbitrary",)))(*args)


def rms_norm(x, g, denom, name):
    def fwd(x, g):
        return _rms_fwd_call(x, g, denom, name + "_f"), (x, g)

    def bwd(res, dy):
        return tuple(_rms_bwd_call(res[0], res[1], dy, denom, name + "_b"))

    return _op(fwd, bwd)(x, g)


def rms_norm_rope(x, g, cos, sin, denom, name):
    def fwd(x, g, cos, sin):
        return _rms_fwd_call(x, g, denom, name + "_f", (cos, sin)), (x, g, cos, sin)

    def bwd(res, dy):
        x, g, cos, sin = res
        dx, dg = _rms_bwd_call(x, g, dy, denom, name + "_b", (cos, sin))
        return dx, dg, jnp.zeros_like(cos), jnp.zeros_like(sin)

    return _op(fwd, bwd)(x, g, cos, sin)


def _mm_call(a, b, mode, res, name):
    if mode == "nn":
        (m, kc), n = a.shape, b.shape[1]
    elif mode == "nt":
        (m, kc), n = a.shape, b.shape[0]
    else:
        (kc, m), n = a.shape, b.shape[1]
    if mode == "tn":
        tk = _div_tile(kc, 528)
        tm = _div_tile(m, 1408, LANES)
        tn = _div_tile(n, 2048, LANES)
    else:
        tk = kc if kc <= 2816 else _div_tile(kc, 1024, LANES)
        tm = _div_tile(m, max(LANES, (9 << 19) // (4 * tk)))
        tn = _div_tile(n, 1408 if mode == "nt" else 512, LANES)
    nk = kc // tk
    dims = {"nn": (((1,), (0,)), ((), ())), "nt": (((1,), (1,)), ((), ())), "tn": (((0,), (0,)), ((), ()))}[mode]

    def body(*refs):
        if res is None:
            a_ref, b_ref, o_ref, acc_ref = refs
            r_ref = None
        else:
            a_ref, b_ref, r_ref, o_ref, acc_ref = refs
        k = pl.program_id(2)

        @pl.when(k == 0)
        def _():
            acc_ref[...] = jnp.zeros_like(acc_ref)

        acc_ref[...] += lax.dot_general(a_ref[...].astype(BF16), b_ref[...].astype(BF16), dims,
                                        preferred_element_type=F32)

        @pl.when(k == nk - 1)
        def _():
            if r_ref is None:
                o_ref[...] = acc_ref[...]
            else:
                o_ref[...] = r_ref[...] + acc_ref[...]

    a_spec = BS((tk, tm), lambda i, j, k: (k, i)) if mode == "tn" else BS((tm, tk), lambda i, j, k: (i, k))
    b_spec = BS((tn, tk), lambda i, j, k: (j, k)) if mode == "nt" else BS((tk, tn), lambda i, j, k: (k, j))
    o_spec = BS((tm, tn), lambda i, j, k: (i, j))
    ins, args = [a_spec, b_spec], [a, b]
    if res is not None:
        ins.append(o_spec)
        args.append(res)
    return pl.pallas_call(
        body, name=name, grid=(m // tm, n // tn, nk), in_specs=ins, out_specs=o_spec,
        out_shape=SDS((m, n), F32), scratch_shapes=[pltpu.VMEM((tm, tn), F32)],
        compiler_params=_cparams(("parallel", "parallel", "arbitrary")))(*args)


def linear(a, w, eps, name, res=None):
    if res is None:
        def fwd(a, w, eps):
            return _mm_call(a, w, "nn", None, name + "_f"), (a, w)

        def bwd(r, dc):
            a, w = r
            return (_mm_call(dc, w, "nt", None, name + "_da"), jnp.zeros_like(w),
                    _mm_call(a, dc, "tn", None, name + "_dw"))

        return _op(fwd, bwd)(a, w, eps)

    def fwd_r(a, w, eps, res):
        return _mm_call(a, w, "nn", res, name + "_f"), (a, w)

    def bwd_r(r, dc):
        a, w = r
        return (_mm_call(dc, w, "nt", None, name + "_da"), jnp.zeros_like(w),
                _mm_call(a, dc, "tn", None, name + "_dw"), dc)

    return _op(fwd_r, bwd_r)(a, w, eps, res)


def _gate_tile(n):
    return _div_tile(n, CAUSAL_TILE, LANES)


def _tri_dot(v, upper):
    ct = v.shape[1]
    r = lax.broadcasted_iota(jnp.int32, (ct, ct), 0)
    c = lax.broadcasted_iota(jnp.int32, (ct, ct), 1)
    tri = jnp.where((r <= c) if upper else (r >= c), 1.0, 0.0).astype(F32)
    return jnp.dot(v, tri, preferred_element_type=F32, precision=lax.Precision.HIGHEST)


def _gate_fwd_call(z, b, name):
    h, n = z.shape
    CT = _gate_tile(n)

    def body(z_ref, b_ref, c_ref, carry):
        @pl.when(pl.program_id(0) == 0)
        def _():
            carry[...] = jnp.zeros_like(carry)

        x = z_ref[...] + b_ref[...]
        ls = jnp.minimum(x, 0.0) - jnp.log(1.0 + jnp.exp(-jnp.abs(x)))
        c_ref[...] = _tri_dot(ls, True) + carry[...]
        carry[...] += jnp.sum(ls, axis=1, keepdims=True)

    return pl.pallas_call(
        body, name=name, grid=(n // CT,),
        in_specs=[BS((h, CT), lambda j: (0, j)), BS((h, 1), lambda j: (0, 0))],
        out_specs=BS((h, CT), lambda j: (0, j)), out_shape=SDS((h, n), F32),
        scratch_shapes=[pltpu.VMEM((h, 1), F32)],
        compiler_params=_cparams(("arbitrary",)))(z, b)


def _gate_bwd_call(z, b, dc, name):
    h, n = z.shape
    CT = _gate_tile(n)
    nt = n // CT

    def body(z_ref, b_ref, dc_ref, dz_ref, db_ref, carry):
        @pl.when(pl.program_id(0) == 0)
        def _():
            carry[...] = jnp.zeros_like(carry)
            db_ref[...] = jnp.zeros_like(db_ref)

        dcv = dc_ref[...]
        dls = _tri_dot(dcv, False) + carry[...]
        carry[...] += jnp.sum(dcv, axis=1, keepdims=True)
        x = z_ref[...] + b_ref[...]
        e = jnp.exp(-jnp.abs(x))
        dz = dls * jnp.where(x >= 0, e / (1.0 + e), 1.0 / (1.0 + e))
        dz_ref[...] = dz
        db_ref[...] += jnp.sum(dz, axis=1, keepdims=True)

    rev = lambda j: (0, nt - 1 - j)
    return pl.pallas_call(
        body, name=name, grid=(nt,),
        in_specs=[BS((h, CT), rev), BS((h, 1), lambda j: (0, 0)), BS((h, CT), rev)],
        out_specs=[BS((h, CT), rev), BS((h, 1), lambda j: (0, 0))],
        out_shape=[SDS((h, n), F32), SDS((h, 1), F32)],
        scratch_shapes=[pltpu.VMEM((h, 1), F32)],
        compiler_params=_cparams(("arbitrary",)))(z, b, dc)


def forget_cumsum(z, b, name):
    def fwd(z, b):
        return _gate_fwd_call(z, b, name + "_f"), (z, b)

    def bwd(res, dc):
        return tuple(_gate_bwd_call(res[0], res[1], dc, name + "_b"))

    return _op(fwd, bwd)(z, b)


NT_DIMS = (((1,), (1,)), ((), ()))
TN_DIMS = (((0,), (0,)), ((), ()))
HEADS_PER_STEP = 2


def _causal_tile(n):
    return CAUSAL_TILE if n % CAUSAL_TILE == 0 else ROW_PAD


def _causal_fwd_call(q, k, v, ck_r, fox, scale, name, late=None, late_layer=1):
    h, n, dk = q.shape
    dv = v.shape[2]
    t = _causal_tile(n)
    nq = n // t
    hb = HEADS_PER_STEP
    nl = 0 if late is None else len(late)
    n_in = 3 + int(fox) + nl

    def body(*refs):
        q_ref, k_ref, v_ref = refs[:3]
        ck_ref = refs[3] if fox else None
        o_ref, lse_ref = refs[n_in:n_in + 2]
        m_scr, l_scr, acc_scr = refs[n_in + 2 + nl:n_in + 5 + nl]
        qi = pl.program_id(1)
        if nl:
            start, forward, drain, receive = _layer_gather(refs[n_in - nl:n_in], refs[n_in + 2:n_in + 2 + nl],
                                                           refs[-2], refs[-1], late_layer, 0)
            hp, core = pl.program_id(0), lax.axis_index("c")
            last = (hp == h // hb - 1) & (qi == nq - 1)
            pl.when((hp == 0) & (qi == 0) & (core == late_layer))(start)
            pl.when((hp == h // hb // 2) & (qi == 0) & (core == late_layer))(forward)
            pl.when(last & (core == late_layer))(drain)
            pl.when(last & (core == 1 - late_layer))(receive)
        qbs = [q_ref[e].astype(BF16) for e in range(hb)]
        m_scr[...] = jnp.full(m_scr.shape, NEG, F32)
        l_scr[...] = jnp.zeros_like(l_scr)
        acc_scr[...] = jnp.zeros_like(acc_scr)

        def process(j, masked):
            off = pl.multiple_of(j * t, t)
            if masked:
                rows = lax.broadcasted_iota(jnp.int32, (t, t), 0)
                cols = lax.broadcasted_iota(jnp.int32, (t, t), 1)
                valid = cols <= rows
            for e in range(hb):
                kb = k_ref[e, pl.ds(off, t), :].astype(BF16)
                vb = v_ref[e, pl.ds(off, t), :].astype(BF16)
                s = lax.dot_general(qbs[e], kb, NT_DIMS, preferred_element_type=F32) * scale
                if fox:
                    s = s - ck_ref[e, j]
                if masked:
                    s = jnp.where(valid, s, NEG)
                m_old = m_scr[e]
                m_new = jnp.maximum(m_old, jnp.max(s, axis=1, keepdims=True))
                alpha = jnp.exp(m_old - m_new)
                p = jnp.exp(s - jnp.tile(m_new, (1, t // LANES)))
                l_scr[e] = alpha * l_scr[e] + jnp.sum(p, axis=1, keepdims=True)
                acc_scr[e] = alpha[:, :dv] * acc_scr[e] + jnp.dot(p.astype(BF16), vb, preferred_element_type=F32)
                m_scr[e] = m_new

        def step(j, carry):
            process(j, False)
            return carry

        lax.fori_loop(0, qi, step, 0)
        process(qi, True)
        for e in range(hb):
            l = l_scr[e]
            o_ref[e] = acc_scr[e] / l[:, :dv]
            lse_ref[e, 0] = jnp.transpose(m_scr[e] + jnp.log(l))[0:1, :]

    ins = [BS((hb, t, dk), lambda a, b: (a, b, 0)), BS((hb, n, dk), lambda a, b: (a, 0, 0)),
           BS((hb, n, dv), lambda a, b: (a, 0, 0))]
    args = [q, k, v]
    if fox:
        ins.append(BS((hb, nq, 1, t), lambda a, b: (a, 0, 0, 0)))
        args.append(ck_r)
    outs = [BS((hb, t, dv), lambda a, b: (a, b, 0)), BS((hb, 1, 1, t), lambda a, b: (a, b, 0, 0))]
    oshape = [SDS((h, n, dv), F32), SDS((h, nq, 1, t), F32)]
    scratch = [pltpu.VMEM((hb, t, LANES), F32), pltpu.VMEM((hb, t, LANES), F32), pltpu.VMEM((hb, t, dv), F32)]
    if nl:
        ins += [ANY] * nl
        args += list(late)
        outs += [ANY] * nl
        oshape += [SDS((N_CHIPS,) + a.shape[1:], a.dtype) for a in late]
        scratch += [pltpu.SemaphoreType.DMA((6 * nl,)), pltpu.SemaphoreType.DMA((6 * nl,))]
    res = pl.pallas_call(
        body, name=name, grid=(h // hb, nq), in_specs=ins, out_specs=outs, out_shape=oshape, scratch_shapes=scratch,
        compiler_params=pltpu.CompilerParams(dimension_semantics=("arbitrary", "arbitrary"),
                                             vmem_limit_bytes=VMEM_LIMIT, has_side_effects=bool(nl)))(*args)
    return res[0], res[1], list(res[2:])


def _causal_bwd_call(q, k, v, do, o, lse_r, ck_r, fox, scale, name, side=None):
    h, n, dk = q.shape
    dv = v.shape[2]
    t = _causal_tile(n)
    nq = n // t
    hb = HEADS_PER_STEP
    ns = 0 if side is None else len(side)

    def body(*refs):
        it = iter(refs)
        q_ref, k_ref, v_ref, do_ref, o_ref, lse_ref = (next(it) for _ in range(6))
        ck_ref = next(it) if fox else None
        side_in = [next(it) for _ in range(ns)]
        dq_ref, dk_ref, dv_ref = next(it), next(it), next(it)
        dck_ref, dcq_ref = (next(it), next(it)) if fox else (None, None)
        side_out = [next(it) for _ in range(ns)]
        delta_scr, dk_scr, dv_scr = next(it), next(it), next(it)
        dck_scr = next(it) if fox else None
        kj = pl.program_id(1)
        if ns:
            send_sems, recv_sems = next(it), next(it)
            x, y, core = _me()
            chips = [(1 - x, y), (x, 1 - y), (1 - x, 1 - y)]
            copies = [_remote(side_in[p].at[2 * cx + cy], side_out[p].at[r], send_sems, recv_sems, 3 * p + r,
                              (cx, cy, core)) for p in range(ns) for r, (cx, cy) in enumerate(chips)]

            @pl.when((pl.program_id(0) == 0) & (kj == 0))
            def _():
                for cp in copies:
                    cp.start()

            @pl.when((pl.program_id(0) == h // hb - 1) & (kj == nq - 1))
            def _():
                for cp in copies:
                    cp.wait()

        @pl.when(kj == 0)
        def _():
            dq_ref[...] = jnp.zeros_like(dq_ref)
            if fox:
                dcq_ref[...] = jnp.zeros_like(dcq_ref)
            ones = jnp.ones((SUBLANES, dv), F32)

            def fill(qi, carry):
                off = pl.multiple_of(qi * t, t)
                for e in range(hb):
                    prod = do_ref[e, pl.ds(off, t), :] * o_ref[e, pl.ds(off, t), :]
                    delta_scr[e, qi] = lax.dot_general(ones, prod, NT_DIMS, preferred_element_type=F32,
                                                       precision=lax.Precision.HIGHEST)[0:1, :]
                return carry

            lax.fori_loop(0, nq, fill, 0)

        kbs = [k_ref[e].astype(BF16) for e in range(hb)]
        vbs = [v_ref[e].astype(BF16) for e in range(hb)]
        dk_scr[...] = jnp.zeros_like(dk_scr)
        dv_scr[...] = jnp.zeros_like(dv_scr)
        if fox:
            dck_scr[...] = jnp.zeros_like(dck_scr)
            ckcs = [jnp.tile(jnp.transpose(jnp.broadcast_to(ck_ref[e, 0], (LANES, t))), (1, t // LANES))
                    for e in range(hb)]

        def process(qi, masked):
            off = pl.multiple_of(qi * t, t)
            if masked:
                krows = lax.broadcasted_iota(jnp.int32, (t, t), 0)
                qcols = lax.broadcasted_iota(jnp.int32, (t, t), 1)
                valid = krows <= qcols
            for e in range(hb):
                qb = q_ref[e, pl.ds(off, t), :].astype(BF16)
                dob = do_ref[e, pl.ds(off, t), :].astype(BF16)
                st = lax.dot_general(kbs[e], qb, NT_DIMS, preferred_element_type=F32) * scale
                if fox:
                    st = st - ckcs[e]
                pt = jnp.exp(st - lse_ref[e, qi])
                if masked:
                    pt = jnp.where(valid, pt, 0.0)
                dv_scr[e] += jnp.dot(pt.astype(BF16), dob, preferred_element_type=F32)
                dpt = lax.dot_general(vbs[e], dob, NT_DIMS, preferred_element_type=F32)
                dst = pt * (dpt - delta_scr[e, qi])
                if fox:
                    dck_scr[e] -= jnp.sum(dst, axis=1, keepdims=True)
                    dcq_ref[e, qi] += jnp.sum(dst, axis=0, keepdims=True)
                dsb = (dst * scale).astype(BF16)
                dk_scr[e] += jnp.dot(dsb, qb, preferred_element_type=F32)
                dq_ref[e, pl.ds(off, t), :] += lax.dot_general(dsb, kbs[e], TN_DIMS, preferred_element_type=F32)

        def step(qi, carry):
            process(qi, False)
            return carry

        process(kj, True)
        lax.fori_loop(kj + 1, nq, step, 0)
        dk_ref[...] = dk_scr[...]
        dv_ref[...] = dv_scr[...]
        if fox:
            for e in range(hb):
                dck_ref[e, 0] = jnp.transpose(jnp.broadcast_to(dck_scr[e], (t, LANES)))[0:1, :]

    whole = lambda a, b: (a, 0, 0)
    tile = lambda a, b: (a, b, 0)
    rowv = lambda a, b: (a, 0, 0, 0)
    rowt = lambda a, b: (a, b, 0, 0)
    ins = [BS((hb, n, dk), whole), BS((hb, t, dk), tile), BS((hb, t, dv), tile), BS((hb, n, dv), whole),
           BS((hb, n, dv), whole), BS((hb, nq, 1, t), rowv)]
    args = [q, k, v, do, o, lse_r]
    outs = [BS((hb, n, dk), whole), BS((hb, t, dk), tile), BS((hb, t, dv), tile)]
    oshape = [SDS((h, n, dk), F32), SDS((h, n, dk), F32), SDS((h, n, dv), F32)]
    scratch = [pltpu.VMEM((hb, nq, 1, t), F32), pltpu.VMEM((hb, t, dk), F32), pltpu.VMEM((hb, t, dv), F32)]
    if fox:
        ins.append(BS((hb, 1, 1, t), rowt))
        args.append(ck_r)
        outs += [BS((hb, 1, 1, t), rowt), BS((hb, nq, 1, t), rowv)]
        oshape += [SDS((h, nq, 1, t), F32), SDS((h, nq, 1, t), F32)]
        scratch.append(pltpu.VMEM((hb, t, 1), F32))
    if ns:
        ins += [ANY] * ns
        args += list(side)
        outs += [ANY] * ns
        oshape += [SDS((3,) + s.shape[1:], s.dtype) for s in side]
        scratch += [pltpu.SemaphoreType.DMA((3 * ns,)), pltpu.SemaphoreType.DMA((3 * ns,))]
    return pl.pallas_call(
        body, name=name, grid=(h // hb, nq), in_specs=ins, out_specs=outs, out_shape=oshape, scratch_shapes=scratch,
        compiler_params=pltpu.CompilerParams(dimension_semantics=("arbitrary", "arbitrary"),
                                             vmem_limit_bytes=VMEM_LIMIT, has_side_effects=bool(ns)))(*args)


def causal_attention(q, k, v, c, scale, name, late=None, late_layer=1, reduce=None, sinks=None):
    h, n, _ = q.shape
    t = _causal_tile(n)
    nq = n // t
    fox = c is not None

    def run_fwd(q, k, v, c, late, sinks):
        ck_r = c.reshape(h, nq, 1, t) if fox else None
        o, lse, got = _causal_fwd_call(q, k, v, ck_r, fox, scale, name + "_f", late, late_layer)
        out = (o,)
        if late is not None:
            out += (got,)
        if reduce is not None:
            out += ([jnp.zeros((N_CHIPS,) + SHARD_SHAPE[w][1:], F32) for w in reduce],)
        return (out if len(out) > 1 else o), (q, k, v, c, late, o, lse)

    def run_bwd(res, ct):
        q, k, v, c, late, o, lse = res
        ck_r = c.reshape(h, nq, 1, t) if fox else None
        dlate = None if late is None else [jnp.zeros_like(a) for a in late]
        if reduce is None:
            do = ct if late is None else ct[0]
            outs = _causal_bwd_call(q, k, v, do, o, lse, ck_r, fox, scale, name + "_b")
            return outs[0], outs[1], outs[2], ((outs[3] + outs[4]).reshape(h, n) if fox else None), dlate, None
        do, g1 = ct[0], ct[-1]
        xi, yi, ci = _me()
        c_idx = ci.astype(jnp.int32).reshape(1)
        j_idx = (2 * xi + yi).astype(jnp.int32).reshape(1)
        gs = [g.reshape((N_CHIPS,) + HALVED[w][1:]) for g, w in zip(g1, reduce)]
        outs = _causal_bwd_call(q, k, v, do, o, lse, ck_r, fox, scale, name + "_b", [g.astype(BF16) for g in gs])
        sc = [_chip_add(g, r, j_idx, name + "_chip_add_" + w).reshape(1, 2, HALVED[w][1] // 2, HALVED[w][2])
              for g, r, w in zip(gs, outs[-len(reduce):], reduce)]
        r1 = _pair_exchange(sc, name + "_pair_exchange")
        gh = [_pair_add(s, r, c_idx, name + "_pair_add_" + w)[0][0] for s, r, w in zip(sc, r1, reduce)]
        go = _half_exchange(gh, name + "_half_exchange")
        return (outs[0], outs[1], outs[2], ((outs[3] + outs[4]).reshape(h, n) if fox else None), dlate,
                list(zip(gh, go)))

    return _op(run_fwd, run_bwd)(q, k, v, c, late, sinks)


SWA_T = 128


def _swa_masks(qi):
    t = SWA_T
    r = lax.broadcasted_iota(jnp.int32, (t, 3 * t), 0)
    c = lax.broadcasted_iota(jnp.int32, (t, 3 * t), 1)
    seg0 = c < t
    seg1 = (c >= t) & (c < 2 * t)
    jp = jnp.maximum(qi - 1, 0)
    kpos = jnp.where(seg0, c, jnp.where(seg1, jp * t + c - t, qi * t + c - 2 * t))
    dist = qi * t + r - kpos
    band = (dist >= 0) & ((dist < WINDOW) | (kpos < N_META))
    valid = (seg0 & (kpos < N_META) & (qi >= 2)) | (jnp.logical_not(seg0) & band & (jnp.logical_not(seg1) | (qi >= 1)))
    return valid, dist.astype(F32)


def _swa_cat(ref, qi):
    t = SWA_T
    jp = jnp.maximum(qi - 1, 0)
    return jnp.concatenate([ref[0, 0:t, :], ref[0, pl.ds(pl.multiple_of(jp * t, t), t), :],
                            ref[0, pl.ds(pl.multiple_of(qi * t, t), t), :]], axis=0).astype(BF16)


def _swa_fwd_call(q, k, v, sinks, slopes, scale, name):
    hq, n, d = q.shape
    hkv = k.shape[0]
    g = hq // hkv
    t = SWA_T
    nq = n // t

    def body(q_ref, k_ref, v_ref, sink_ref, slope_ref, o_ref, lse_ref):
        grp = pl.program_id(0)
        qi = pl.program_id(1)
        valid, dist = _swa_masks(qi)
        kc = _swa_cat(k_ref, qi)
        vc = _swa_cat(v_ref, qi)
        qs = jnp.concatenate([q_ref[e] for e in range(g)], axis=0).astype(BF16)
        s_all = lax.dot_general(qs, kc, NT_DIMS, preferred_element_type=F32) * scale
        ps, ls, ms = [], [], []
        for e in range(g):
            hh = grp * g + e
            s = jnp.where(valid, s_all[e * t:(e + 1) * t] - slope_ref[hh] * dist, NEG)
            m = jnp.maximum(jnp.max(s, axis=1, keepdims=True), sink_ref[hh])
            p = jnp.exp(s - m)
            ls.append(jnp.sum(p, axis=1, keepdims=True) + jnp.exp(sink_ref[hh] - m))
            ms.append(m)
            ps.append(p.astype(BF16))
        acc = jnp.dot(jnp.concatenate(ps, axis=0), vc, preferred_element_type=F32)
        for e in range(g):
            o_ref[e] = acc[e * t:(e + 1) * t] / ls[e]
            lse_ref[e] = ms[e] + jnp.log(ls[e])

    return pl.pallas_call(
        body, name=name, grid=(hkv, nq),
        in_specs=[BS((g, t, d), lambda a, b: (a, b, 0)), BS((1, n, d), lambda a, b: (a, 0, 0)),
                  BS((1, n, d), lambda a, b: (a, 0, 0)), BS(memory_space=pltpu.SMEM), BS(memory_space=pltpu.SMEM)],
        out_specs=[BS((g, t, d), lambda a, b: (a, b, 0)), BS((g, t, 1), lambda a, b: (a, b, 0))],
        out_shape=[SDS((hq, n, d), F32), SDS((hq, n, 1), F32)],
        compiler_params=_cparams(("parallel", "parallel")))(q, k, v, sinks, slopes)


def _swa_bwd_call(q, k, v, o, lse, do, sinks, slopes, scale, name):
    hq, n, d = q.shape
    hkv = k.shape[0]
    g = hq // hkv
    t = SWA_T
    nq = n // t

    def body(q_ref, k_ref, v_ref, o_ref, lse_ref, do_ref, sink_ref, slope_ref, dq_ref, dk_ref, dv_ref, ds_ref):
        grp = pl.program_id(0)
        qi = pl.program_id(1)

        @pl.when(qi == 0)
        def _():
            dk_ref[...] = jnp.zeros_like(dk_ref)
            dv_ref[...] = jnp.zeros_like(dv_ref)
            ds_ref[...] = jnp.zeros_like(ds_ref)

        valid, dist = _swa_masks(qi)
        kc = _swa_cat(k_ref, qi)
        vc = _swa_cat(v_ref, qi)
        qs = jnp.concatenate([q_ref[e] for e in range(g)], axis=0).astype(BF16)
        dos = jnp.concatenate([do_ref[e] for e in range(g)], axis=0).astype(BF16)
        s_all = lax.dot_general(qs, kc, NT_DIMS, preferred_element_type=F32) * scale
        dp_all = lax.dot_general(dos, vc, NT_DIMS, preferred_element_type=F32)
        ps, dss = [], []
        for e in range(g):
            hh = grp * g + e
            lse_e = lse_ref[e]
            delta = jnp.sum(do_ref[e] * o_ref[e], axis=1, keepdims=True)
            s = s_all[e * t:(e + 1) * t] - slope_ref[hh] * dist
            p = jnp.where(valid, jnp.exp(s - lse_e), 0.0)
            ds = p * (dp_all[e * t:(e + 1) * t] - delta)
            ps.append(p.astype(BF16))
            dss.append((ds * scale).astype(BF16))
            ds_ref[e] += -jnp.sum(jnp.exp(sink_ref[hh] - lse_e) * delta)
        p_st = jnp.concatenate(ps, axis=0)
        ds_st = jnp.concatenate(dss, axis=0)
        dq = jnp.dot(ds_st, kc, preferred_element_type=F32)
        for e in range(g):
            dq_ref[e] = dq[e * t:(e + 1) * t]
        dkc = lax.dot_general(ds_st, qs, TN_DIMS, preferred_element_type=F32)
        dvc = lax.dot_general(p_st, dos, TN_DIMS, preferred_element_type=F32)
        jp = jnp.maximum(qi - 1, 0)
        for seg, off in enumerate((0, pl.multiple_of(jp * t, t), pl.multiple_of(qi * t, t))):
            dk_ref[0, pl.ds(off, t), :] += dkc[seg * t:(seg + 1) * t]
            dv_ref[0, pl.ds(off, t), :] += dvc[seg * t:(seg + 1) * t]

    tile = lambda a, b: (a, b, 0)
    whole = lambda a, b: (a, 0, 0)
    return pl.pallas_call(
        body, name=name, grid=(hkv, nq),
        in_specs=[BS((g, t, d), tile), BS((1, n, d), whole), BS((1, n, d), whole), BS((g, t, d), tile),
                  BS((g, t, 1), tile), BS((g, t, d), tile), BS(memory_space=pltpu.SMEM), BS(memory_space=pltpu.SMEM)],
        out_specs=[BS((g, t, d), tile), BS((1, n, d), whole), BS((1, n, d), whole), BS((g, 1, LANES), whole)],
        out_shape=[SDS((hq, n, d), F32), SDS((hkv, n, d), F32), SDS((hkv, n, d), F32), SDS((hq, 1, LANES), F32)],
        compiler_params=_cparams(("arbitrary", "arbitrary")))(q, k, v, o, lse, do, sinks, slopes)


def window_attention(q, k, v, sinks, slopes, scale, name):
    def run_fwd(q, k, v, sinks, slopes):
        o, lse = _swa_fwd_call(q, k, v, sinks, slopes, scale, name + "_f")
        return o, (q, k, v, sinks, slopes, o, lse)

    def run_bwd(res, do):
        q, k, v, sinks, slopes, o, lse = res
        dq, dk, dv, ds = _swa_bwd_call(q, k, v, o, lse, do, sinks, slopes, scale, name + "_b")
        return dq, dk, dv, ds[:, 0, 0], jnp.zeros_like(slopes)

    return _op(run_fwd, run_bwd)(q, k, v, sinks, slopes)


def _sigmoid(x):
    return 1.0 / (1.0 + jnp.exp(-x))


def _merge_fwd_call(gs, ys, name):
    n, c = ys[0].shape
    tr = _rows_tile(n, c, 1 << 20)

    def body(g0, g1, g2, y0, y1, y2, m_ref):
        m_ref[...] = (_sigmoid(g0[...]) * y0[...] + _sigmoid(g1[...]) * y1[...]) + _sigmoid(g2[...]) * y2[...]

    spec = BS((tr, c), lambda i: (i, 0))
    return pl.pallas_call(
        body, name=name, grid=(n // tr,), in_specs=[spec] * 6, out_specs=spec, out_shape=SDS((n, c), F32),
        compiler_params=_cparams(("parallel",)))(*gs, *ys)


def _merge_bwd_call(gs, ys, dm, name):
    n, c = ys[0].shape
    tr = _rows_tile(n, c, 1 << 20)

    def body(g0, g1, g2, y0, y1, y2, dm_ref, dg0, dg1, dg2, dy0, dy1, dy2):
        d = dm_ref[...]
        for g, y, dg, dy in ((g0, y0, dg0, dy0), (g1, y1, dg1, dy1), (g2, y2, dg2, dy2)):
            s = _sigmoid(g[...])
            dy[...] = d * s
            dg[...] = d * y[...] * (s * (1.0 - s))

    spec = BS((tr, c), lambda i: (i, 0))
    return pl.pallas_call(
        body, name=name, grid=(n // tr,), in_specs=[spec] * 7, out_specs=[spec] * 6,
        out_shape=[SDS((n, c), F32)] * 6, compiler_params=_cparams(("parallel",)))(*gs, *ys, dm)


def gated_merge(gs, ys, name):
    def fwd(gs, ys):
        return _merge_fwd_call(gs, ys, name + "_f"), (gs, ys)

    def bwd(res, dm):
        out = _merge_bwd_call(res[0], res[1], dm, name + "_b")
        return tuple(out[:3]), tuple(out[3:])

    return _op(fwd, bwd)(tuple(gs), tuple(ys))


CONV_TR = 264
CONV_TC = 1408


def _conv_tiles(n, f):
    tr = CONV_TR if n % CONV_TR == 0 else _div_tile(n, CONV_TR)
    tc = CONV_TC if f % CONV_TC == 0 else f
    return tr, tc


def _shift_down(cur, halo, first, tr):
    halo = jnp.where(first, 0.0, halo)
    row = lax.broadcasted_iota(jnp.int32, cur.shape, 0)
    h7, h6 = halo[7:8, :], halo[6:7, :]
    u1 = jnp.where(row == 0, h7, pltpu.roll(cur, 1, 0))
    u2 = jnp.where(row == 0, h6, jnp.where(row == 1, h7, pltpu.roll(cur, 2, 0)))
    return u1, u2


def _conv_lin(cur, u1, u2, w_ref, b_ref):
    return ((b_ref[...] + w_ref[0:1, :] * u2) + w_ref[1:2, :] * u1) + w_ref[2:3, :] * cur


def _conv_in_specs(tr, tc, nj):
    sub = tr // SUBLANES
    prev = lambda j, i: (jnp.maximum(i * sub - 1, 0), j)
    prev_v = lambda j, i: (jnp.maximum(i * sub - 1, 0), j + nj)
    return [BS((tr, tc), lambda j, i: (i, j)), BS((SUBLANES, tc), prev),
            BS((tr, tc), lambda j, i: (i, j + nj)), BS((SUBLANES, tc), prev_v),
            BS((3, tc), lambda j, i: (0, j)), BS((3, tc), lambda j, i: (0, j + nj)),
            BS((1, tc), lambda j, i: (0, j)), BS((1, tc), lambda j, i: (0, j + nj))]


def _conv_fwd_call(u, cw, cb, name):
    n, f2 = u.shape
    f = f2 // 2
    tr, tc = _conv_tiles(n, f)
    nj = f // tc

    def body(ug, ugh, uv, uvh, wg, wv, bg, bv, a_ref):
        first = pl.program_id(1) == 0
        g1, g2 = _shift_down(ug[...], ugh[...], first, tr)
        v1, v2 = _shift_down(uv[...], uvh[...], first, tr)
        cg = _conv_lin(ug[...], g1, g2, wg, bg)
        cv = _conv_lin(uv[...], v1, v2, wv, bv)
        a_ref[...] = cg * _sigmoid(cg) * cv

    return pl.pallas_call(
        body, name=name, grid=(nj, n // tr), in_specs=_conv_in_specs(tr, tc, nj),
        out_specs=BS((tr, tc), lambda j, i: (i, j)), out_shape=SDS((n, f), F32),
        compiler_params=_cparams(("parallel", "parallel")))(u, u, u, u, cw, cw, cb, cb)


def _conv_bwd_dc_call(u, cw, cb, da, name):
    n, f2 = u.shape
    f = f2 // 2
    tr, tc = _conv_tiles(n, f)
    nj = f // tc

    def body(ug, ugh, uv, uvh, wg, wv, bg, bv, da_ref, dc_ref, dw_ref, db_ref):
        first = pl.program_id(1) == 0
        g0, v0 = ug[...], uv[...]
        g1, g2 = _shift_down(g0, ugh[...], first, tr)
        v1, v2 = _shift_down(v0, uvh[...], first, tr)
        cg = _conv_lin(g0, g1, g2, wg, bg)
        cv = _conv_lin(v0, v1, v2, wv, bv)
        d = da_ref[...]
        s = _sigmoid(cg)
        dcg = d * cv * (s * (1.0 + cg * (1.0 - s)))
        dcv = d * (cg * s)
        dc_ref[0] = dcg
        dc_ref[1] = dcv

        @pl.when(first)
        def _():
            dw_ref[...] = jnp.zeros_like(dw_ref)
            db_ref[...] = jnp.zeros_like(db_ref)

        for p, dc, taps in ((0, dcg, (g2, g1, g0)), (1, dcv, (v2, v1, v0))):
            for t in range(3):
                dw_ref[p, t:t + 1, :] += jnp.sum(dc * taps[t], axis=0, keepdims=True)
            db_ref[p] += jnp.sum(dc, axis=0, keepdims=True)

    return pl.pallas_call(
        body, name=name, grid=(nj, n // tr),
        in_specs=_conv_in_specs(tr, tc, nj) + [BS((tr, tc), lambda j, i: (i, j))],
        out_specs=[BS((2, tr, tc), lambda j, i: (0, i, j)), BS((2, 3, tc), lambda j, i: (0, 0, j)),
                   BS((2, 1, tc), lambda j, i: (0, 0, j))],
        out_shape=[SDS((2, n, f), F32), SDS((2, 3, f), F32), SDS((2, 1, f), F32)],
        compiler_params=_cparams(("arbitrary", "arbitrary")))(u, u, u, u, cw, cw, cb, cb, da)


def _conv_bwd_du_call(dc, cw, name):
    _, n, f = dc.shape
    tr, tc = _conv_tiles(n, f)
    nj = f // tc
    ni = n // tr
    sub = tr // SUBLANES

    def body(c_ref, nx_ref, w_ref, du_ref):
        cur = c_ref[0]
        nxt = jnp.where(pl.program_id(2) == ni - 1, 0.0, nx_ref[0])
        row = lax.broadcasted_iota(jnp.int32, cur.shape, 0)
        n0, n1 = nxt[0:1, :], nxt[1:2, :]
        d1 = jnp.where(row == tr - 1, n0, pltpu.roll(cur, tr - 1, 0))
        d2 = jnp.where(row == tr - 1, n1, jnp.where(row == tr - 2, n0, pltpu.roll(cur, tr - 2, 0)))
        du_ref[...] = (w_ref[2:3, :] * cur + w_ref[1:2, :] * d1) + w_ref[0:1, :] * d2

    nxt_map = lambda p, j, i: (p, jnp.minimum((i + 1) * sub, n // SUBLANES - 1), j)
    return pl.pallas_call(
        body, name=name, grid=(2, nj, ni),
        in_specs=[BS((1, tr, tc), lambda p, j, i: (p, i, j)), BS((1, SUBLANES, tc), nxt_map),
                  BS((3, tc), lambda p, j, i: (0, p * nj + j))],
        out_specs=BS((tr, tc), lambda p, j, i: (i, p * nj + j)), out_shape=SDS((n, 2 * f), F32),
        compiler_params=_cparams(("parallel", "parallel", "parallel")))(dc, dc, cw)


def conv_glu(u, cw, cb, name):
    def fwd(u, cw, cb):
        return _conv_fwd_call(u, cw, cb, name + "_f"), (u, cw, cb)

    def bwd(res, da):
        u, cw, cb = res
        dc, dw, db = _conv_bwd_dc_call(u, cw, cb, da, name + "_bc")
        du = _conv_bwd_du_call(dc, cw, name + "_bu")
        return du, jnp.concatenate([dw[0], dw[1]], axis=-1), jnp.concatenate([db[0], db[1]], axis=-1)

    return _op(fwd, bwd)(u, cw, cb)


def _loss_call(y, t, n_real, name):
    n, c = y.shape
    tr = _rows_tile(n, c, 1 << 20)

    def body(y_ref, t_ref, dy_ref, l_ref):
        i = pl.program_id(0)
        row = i * tr + lax.broadcasted_iota(jnp.int32, (tr, c), 0)
        real = (row >= N_META) & (row < N_META + n_real)
        e = jnp.where(real, y_ref[...] - t_ref[...], 0.0)
        dy_ref[...] = e * (1.0 / c)

        @pl.when(i == 0)
        def _():
            l_ref[...] = jnp.zeros_like(l_ref)

        l_ref[...] += 0.5 * jnp.sum(jnp.sum(e * e, axis=-1, keepdims=True) * (1.0 / c), axis=0, keepdims=True)

    spec = BS((tr, c), lambda i: (i, 0))
    return pl.pallas_call(
        body, name=name, grid=(n // tr,), in_specs=[spec, spec],
        out_specs=[spec, BS((1, 1), lambda i: (0, 0))], out_shape=[SDS((n, c), F32), SDS((1, 1), F32)],
        compiler_params=_cparams(("arbitrary",)))(y, t)


def _to_heads(x, nh):
    n = x.shape[0]
    return x.reshape(n, nh, x.shape[1] // nh).transpose(1, 0, 2)


def _from_heads(x):
    h, n, d = x.shape
    return x.transpose(1, 0, 2).reshape(n, h * d)


def _head_norm(x, g, denom, name):
    h, n, d = x.shape
    return rms_norm(x.reshape(h * n, d), g, denom, name).reshape(h, n, d)


def _head_norm_rope(x, g, cos, sin, name):
    h, n, d = x.shape
    return rms_norm_rope(x.reshape(h * n, d), g, cos, sin, MLA_QK, name).reshape(h, n, d)


def _pad_in_cols(w):
    z = lambda k: jnp.zeros(w.shape[:-1] + (k,), w.dtype)
    return jnp.concatenate([w[..., :1544], z(120), w[..., 1544:1960], z(96), w[..., 1960:], z(128)], axis=-1)


def _pad_q_up(w):
    s = w.shape[:-1]
    w = w.reshape(s + (HEADS, MLA_QK))
    w = jnp.concatenate([w, jnp.zeros(s + (HEADS, LANES - MLA_QK), w.dtype)], axis=-1)
    return w.reshape(s + (HEADS * LANES,))


LAYERED = BIG + ("ffn_conv_w",)
BEFORE_MLA = ("w_in", "mla_w_q_up", "mla_w_kv_up")
AFTER_FOX = ("w_branch", "w_o")
FFN_PAIR = ("ffn_w_up", "ffn_w_down")
AFTER_MLA = AFTER_FOX + FFN_PAIR


def _assemble_layer(parts):
    out = {k: jnp.concatenate([v[i] for i in range(N_CHIPS)], axis=SHARD_AXIS[k] - 1) for k, v in parts.items()}
    if "w_in" in out:
        out["w_in"] = _pad_in_cols(out["w_in"])
    if "mla_w_q_up" in out:
        out["mla_w_q_up"] = _pad_q_up(out["mla_w_q_up"])
    return out


def _layer_parts(names, gathered):
    return {k: g.reshape((N_CHIPS,) + SHARD_SHAPE[k][1:]) for k, g in zip(names, gathered)}


def _rope_tables(n):
    half = MLA_ROPE // 2
    freqs = ROPE_THETA ** (-jnp.arange(half, dtype=F32) / half)
    ang = jnp.arange(n).astype(F32)[:, None] * freqs[None, :]
    cos, sin = jnp.cos(ang), jnp.sin(ang)
    one, zero = jnp.ones((n, MLA_NOPE), F32), jnp.zeros((n, MLA_NOPE), F32)
    tail1, tail0 = jnp.ones((n, LANES - MLA_QK), F32), jnp.zeros((n, LANES - MLA_QK), F32)
    return (jnp.concatenate([one, cos, cos, tail1], axis=1), jnp.concatenate([zero, sin, sin, tail0], axis=1))


def _pad_lanes(g, width):
    return jnp.concatenate([g, jnp.zeros((width - g.shape[0],), g.dtype)]).reshape(1, width)


PROJ_SEGMENTS = ((O_FQ, 512), (O_FK, 512), (O_FV, 512), (O_FF, HEADS), (O_CQ, MLA_Q_RANK), (O_CKV, MLA_KV_RANK),
                 (O_KR, MLA_ROPE), (O_SQ, 512), (O_SK, 128), (O_SV, 128), (O_G, D_MODEL), (O_G + D_MODEL, D_MODEL),
                 (O_G + 2 * D_MODEL, D_MODEL))


def _join_proj_call(parts, name):
    n = parts[0].shape[0]
    tr = _rows_tile(n, IN_PAD, 4 << 20)

    def body(*refs):
        o_ref = refs[-1]
        o_ref[...] = jnp.zeros_like(o_ref)
        for (s, w), r in zip(PROJ_SEGMENTS, refs[:-1]):
            o_ref[:, s:s + w] = r[...]

    return pl.pallas_call(
        body, name=name, grid=(n // tr,), in_specs=[BS((tr, w), lambda i: (i, 0)) for _, w in PROJ_SEGMENTS],
        out_specs=BS((tr, IN_PAD), lambda i: (i, 0)), out_shape=SDS((n, IN_PAD), F32),
        compiler_params=_cparams(("parallel",)))(*parts)


def _split_proj(proj, name):
    def fwd(x):
        return tuple(x[:, s:s + w] for s, w in PROJ_SEGMENTS), None

    def bwd(_, cts):
        return (_join_proj_call(cts, name + "_b"),)

    return _op(fwd, bwd)(proj)


def _trunk(eps, eps_cw, sinks, meta, small, x, w0, late):
    assert DEPTH == 2
    seq = x.shape[0]
    n = -(-(N_META + seq) // ROW_PAD) * ROW_PAD
    ew = _assemble_layer(eps)
    cos, sin = _rope_tables(n)
    slopes = jnp.exp2(-8.0 * jnp.arange(1, HEADS + 1, dtype=F32) / HEADS)
    h = jnp.concatenate([meta, x, jnp.zeros((n - N_META - seq, D_MODEL), F32)], axis=0)
    wb = w0
    for l in range(DEPTH):
        p = f"l{l}_"
        row = lambda name: small[name][l].reshape(1, -1)
        xn = rms_norm(h, row("norm1_g"), D_MODEL, p + "norm1")
        proj = linear(xn, wb["w_in"], ew["w_in"], p + "win")
        p_fq, p_fk, p_fv, p_ff, p_cq, p_ckv, p_kr, p_sq, p_sk, p_sv, g0, g1, g2 = _split_proj(proj, p + "split")
        fq = _head_norm(_to_heads(p_fq, HEADS), row("fox_q_g"), HEAD_DIM, p + "fqn")
        fk = _head_norm(_to_heads(p_fk, HEADS), row("fox_k_g"), HEAD_DIM, p + "fkn")
        fv = _to_heads(p_fv, HEADS)
        c = forget_cumsum(p_ff.T, small["fox_forget_b"][l].reshape(HEADS, 1), p + "fgate")
        if l == 0:
            rest = [a for k, a in zip(LAYERED, late) if k != "w_in"]
            out_a, got, carriers = causal_attention(fq, fk, fv, c, HEAD_DIM ** -0.5, p + "fox", rest, 0,
                                                    BEFORE_MLA + AFTER_FOX, sinks[1])
            wb = dict(wb, **_assemble_layer(_layer_parts(LAYERED[1:], _fill_own(got, [a[0] for a in rest]))))
            ew1 = _assemble_layer(dict(zip(BEFORE_MLA, carriers[:len(BEFORE_MLA)])))
            carriers = carriers[len(BEFORE_MLA):]
        else:
            out_a, carriers = causal_attention(fq, fk, fv, c, HEAD_DIM ** -0.5, p + "fox", None, 1, AFTER_FOX,
                                               sinks[3])
        ew = dict(ew, **_assemble_layer(dict(zip(AFTER_FOX, carriers))))
        cqn = rms_norm(p_cq, row("mla_q_a_g"), MLA_Q_RANK, p + "cqn")
        q = _to_heads(linear(cqn, wb["mla_w_q_up"], ew["mla_w_q_up"], p + "qup"), HEADS)
        q = _head_norm_rope(q, _pad_lanes(small["mla_q_g"][l], LANES), cos, sin, p + "mqn")
        ckvn = rms_norm(p_ckv, row("mla_kv_a_g"), MLA_KV_RANK, p + "ckvn")
        kv = _to_heads(linear(ckvn, wb["mla_w_kv_up"], ew["mla_w_kv_up"], p + "kvup"), HEADS)
        kr = jnp.broadcast_to(p_kr[None], (HEADS, n, MLA_ROPE))
        k = jnp.concatenate([kv[..., :MLA_NOPE], kr, jnp.zeros((HEADS, n, LANES - MLA_QK), F32)], axis=-1)
        k = _head_norm_rope(k, _pad_lanes(small["mla_k_g"][l], LANES), cos, sin, p + "mkn")
        if l == 0:
            out_b, got, carriers = causal_attention(q, k, kv[..., MLA_NOPE:], None, MLA_QK ** -0.5, p + "mla", late,
                                                    1, FFN_PAIR, sinks[0])
            w1 = _assemble_layer(_layer_parts(LAYERED, _fill_own(got, [a[1] for a in late])))
        else:
            out_b, carriers = causal_attention(q, k, kv[..., MLA_NOPE:], None, MLA_QK ** -0.5, p + "mla", None, 1,
                                               FFN_PAIR, sinks[2])
        ew = dict(ew, **_assemble_layer(dict(zip(FFN_PAIR, carriers))))
        sq = _head_norm(_to_heads(p_sq, HEADS), row("swa_q_g"), HEAD_DIM, p + "sqn")
        sk = _head_norm(_to_heads(p_sk, SWA_KV_HEADS), row("swa_k_g"), HEAD_DIM, p + "skn")
        sv = _to_heads(p_sv, SWA_KV_HEADS)
        out_c = window_attention(sq, sk, sv, small["swa_sinks"][l], slopes, HEAD_DIM ** -0.5, p + "swa")
        ys = [linear(_from_heads(o), wb["w_branch"][i], ew["w_branch"][i], p + f"br{i}")
              for i, o in enumerate((out_a, out_b, out_c))]
        merged = gated_merge([g0, g1, g2], ys, p + "merge")
        h = linear(merged, wb["w_o"], ew["w_o"], p + "wo", res=h)
        xn2 = rms_norm(h, row("norm2_g"), D_MODEL, p + "norm2")
        u = linear(xn2, wb["ffn_w_up"], ew["ffn_w_up"], p + "wup")
        act = conv_glu(u, lax.stop_gradient(wb["ffn_conv_w"]) + eps_cw[l], row("ffn_conv_b"), p + "conv")
        h = linear(act, wb["ffn_w_down"], ew["ffn_w_down"], p + "wdown", res=h)
        wb, ew = w1, ew1
    return h


def _local_step(x, target, w0, late, meta, small):
    seq = x.shape[0]
    eps = {k: jnp.zeros((N_CHIPS,) + SHARD_SHAPE[k][1:], F32) for k in BEFORE_MLA}
    eps_cw = jnp.zeros((DEPTH, 3, 2 * D_FF), F32)
    half = lambda k: jnp.zeros((HALVED[k][1] // 2, HALVED[k][2]), F32)
    sinks = tuple([(half(k), half(k)) for k in names]
                  for names in (FFN_PAIR, BEFORE_MLA + AFTER_FOX, FFN_PAIR, AFTER_FOX))
    y, vjp = jax.vjp(lambda e, ec, sk, mt, s, xx: _trunk(e, ec, sk, mt, s, xx, w0, late),
                     eps, eps_cw, sinks, meta, small, x)
    n = y.shape[0]
    tpad = jnp.concatenate([jnp.zeros((N_META, D_MODEL), F32), target, jnp.zeros((n - N_META - seq, D_MODEL), F32)])
    dy, loss = _loss_call(y, tpad, seq, "loss")
    g_eps, g_cw, g_l1, g_meta, g_small, g_x = vjp(dy)
    return loss[0, 0], g_x, g_eps, g_l1, g_cw, g_meta, g_small


def _pack_rows(shapes, mult):
    total = sum(_size(s) for s in shapes)
    rows = -(-total // LANES)
    return -(-rows // mult) * mult


def _size(shape):
    n = 1
    for d in shape:
        n *= d
    return n


def _pack(arrs, rows, dtype):
    flat = [a.reshape(-1).astype(dtype) for a in arrs]
    used = sum(a.size for a in flat)
    flat.append(jnp.zeros((rows * LANES - used,), dtype))
    return jnp.concatenate(flat).reshape(rows, LANES)


def _unpack(p, shapes):
    flat = p.reshape(-1)
    out, off = [], 0
    for s in shapes:
        out.append(flat[off:off + _size(s)].reshape(s))
        off += _size(s)
    return out


MESH = pl.DeviceIdType.MESH
ANY = pl.BlockSpec(memory_space=pl.ANY)


def _me():
    return lax.axis_index("x"), lax.axis_index("y"), lax.axis_index("c")


def _remote(src, dst, send_sems, recv_sems, idx, dev):
    return pltpu.make_async_remote_copy(src_ref=src, dst_ref=dst, send_sem=send_sems.at[idx], recv_sem=recv_sems.at[idx],
                                        device_id=dev, device_id_type=MESH)


def _layer_gather(ins, outs, send_sems, recv_sems, layer, base):
    x, y, _ = _me()
    j = 2 * x + y
    sibling = (x, y, 1 - layer)
    chips = [(1 - x, y), (x, 1 - y), (1 - x, 1 - y)]

    def ici(p, r):
        cx, cy = chips[r]
        return _remote(ins[p].at[layer], outs[p].at[j], send_sems, recv_sems, base + 6 * p + r, (cx, cy, layer))

    def d2d(p, r):
        cx, cy = chips[r]
        blk = outs[p].at[2 * cx + cy]
        return _remote(blk, blk, send_sems, recv_sems, base + 6 * p + 3 + r, sibling)

    pairs = [(p, r) for p in range(len(ins)) for r in range(3)]

    def start():
        for p, r in pairs:
            ici(p, r).start()

    def forward():
        for p, r in pairs:
            ici(p, r).wait_recv()
            d2d(p, r).start()

    def drain():
        for p, r in pairs:
            ici(p, r).wait_send()
            d2d(p, r).wait_send()

    def receive():
        for p, r in pairs:
            d2d(p, r).wait_recv()

    return start, forward, drain, receive


def _fill_own(outs, own):
    j = 2 * lax.axis_index("x") + lax.axis_index("y")
    return [lax.dynamic_update_index_in_dim(o, a, j, 0) for o, a in zip(outs, own)]


def _gather_early(meta, arrs, name):
    npk = len(arrs)

    def body(*refs):
        m_in, ins = refs[0], refs[1:npk + 1]
        m_out, outs = refs[npk + 1], refs[npk + 2:2 * npk + 2]
        send_sems, recv_sems = refs[2 * npk + 2:]
        x, y, c = _me()
        j = 2 * x + y
        sibling = (x, y, 1 - c)
        chips = [(1 - x, y), (x, 1 - y), (1 - x, 1 - y)]
        start, forward, drain, receive = _layer_gather(ins, outs, send_sems, recv_sems, 0, 6)
        sends = []
        for r, (cx, cy) in enumerate(chips):
            cp = _remote(m_in.at[c], m_out.at[j, c], send_sems, recv_sems, r, (cx, cy, c))
            cp.start()
            sends.append(cp)
        pl.when(c == 0)(start)
        for r, (cx, cy) in enumerate(chips):
            blk = m_out.at[2 * cx + cy, c]
            _remote(blk, blk, send_sems, recv_sems, r, sibling).wait_recv()
            fw = _remote(blk, blk, send_sems, recv_sems, 3 + r, sibling)
            fw.start()
            sends.append(fw)
        for r, (cx, cy) in enumerate(chips):
            blk = m_out.at[2 * cx + cy, 1 - c]
            _remote(blk, blk, send_sems, recv_sems, 3 + r, sibling).wait_recv()
        for cp in sends:
            cp.wait_send()

        @pl.when(c == 0)
        def _():
            forward()
            drain()

        pl.when(c == 1)(receive)

    nsem = 6 + 6 * npk
    res = pl.pallas_call(
        body, name=name, in_specs=[ANY] * (npk + 1), out_specs=[ANY] * (npk + 1),
        out_shape=[SDS((N_CHIPS,) + meta.shape, meta.dtype)] + [SDS((N_CHIPS,) + a.shape[1:], a.dtype) for a in arrs],
        scratch_shapes=[pltpu.SemaphoreType.DMA((nsem,)), pltpu.SemaphoreType.DMA((nsem,))],
        compiler_params=pltpu.CompilerParams(has_side_effects=True))(meta, *arrs)
    return _fill_own(res[:1], [meta])[0], _fill_own(res[1:], [a[0] for a in arrs])


def _pair_exchange(gs, name):
    npk = len(gs)

    def body(*refs):
        ins, outs = refs[:npk], refs[npk:2 * npk]
        send_sems, recv_sems = refs[2 * npk:]
        x, y, c = _me()
        cps = [_remote(ins[p].at[:, 1 - c], outs[p], send_sems, recv_sems, p, (x, y, 1 - c)) for p in range(npk)]
        for cp in cps:
            cp.start()
        for cp in cps:
            cp.wait()

    return pl.pallas_call(
        body, name=name, in_specs=[ANY] * npk, out_specs=[ANY] * npk,
        out_shape=[SDS(g.shape[:1] + g.shape[2:], g.dtype) for g in gs],
        scratch_shapes=[pltpu.SemaphoreType.DMA((npk,)), pltpu.SemaphoreType.DMA((npk,))],
        compiler_params=pltpu.CompilerParams(has_side_effects=True))(*gs)


def _chip_exchange(ss, small, name):
    npk = len(ss)

    def body(*refs):
        ins, sm_ref = refs[:npk], refs[npk]
        outs, sa_ref = refs[npk + 1:2 * npk + 1], refs[2 * npk + 1]
        send_sems, recv_sems, loc_sem = refs[2 * npk + 2:]
        x, y, c = _me()
        me = 4 * x + 2 * y + c
        lc = pltpu.make_async_copy(sm_ref, sa_ref.at[me], loc_sem.at[0])
        lc.start()
        cps = []
        for p in range(npk):
            for r, (cx, cy) in enumerate([(1 - x, y), (x, 1 - y), (1 - x, 1 - y)]):
                cp = _remote(ins[p].at[2 * cx + cy], outs[p].at[r], send_sems, recv_sems, 3 * p + r, (cx, cy, c))
                cp.start()
                cps.append(cp)
        base = 3 * npk - 1
        for mask in range(1, N_DEV):
            px, py, pc = x ^ (mask >> 2), y ^ ((mask >> 1) & 1), c ^ (mask & 1)
            cp = _remote(sm_ref, sa_ref.at[me], send_sems, recv_sems, base + mask, (px, py, pc))
            cp.start()
            cps.append(cp)
        for p in range(npk):
            for r in range(3):
                _remote(outs[p].at[r], outs[p].at[r], send_sems, recv_sems, 3 * p + r, (x, y, c)).wait_recv()
        for mask in range(1, N_DEV):
            src = 4 * (x ^ (mask >> 2)) + 2 * (y ^ ((mask >> 1) & 1)) + (c ^ (mask & 1))
            _remote(sa_ref.at[src], sa_ref.at[src], send_sems, recv_sems, base + mask, (x, y, c)).wait_recv()
        for cp in cps:
            cp.wait_send()
        lc.wait()

    nsem = 3 * npk + N_DEV - 1
    res = pl.pallas_call(
        body, name=name, in_specs=[ANY] * (npk + 1), out_specs=[ANY] * (npk + 1),
        out_shape=[SDS((3,) + s.shape[1:], s.dtype) for s in ss] + [SDS((N_DEV,) + small.shape, small.dtype)],
        scratch_shapes=[pltpu.SemaphoreType.DMA((nsem,)), pltpu.SemaphoreType.DMA((nsem,)),
                        pltpu.SemaphoreType.DMA((1,))],
        compiler_params=pltpu.CompilerParams(has_side_effects=True))(*ss, small)
    return res[:npk], res[npk]


def _half_exchange(ghs, name):
    npk = len(ghs)

    def body(*refs):
        ins, outs = refs[:npk], refs[npk:2 * npk]
        send_sems, recv_sems = refs[2 * npk:]
        x, y, c = _me()
        cps = [_remote(ins[p], outs[p], send_sems, recv_sems, p, (x, y, 1 - c)) for p in range(npk)]
        for cp in cps:
            cp.start()
        for cp in cps:
            cp.wait()

    return pl.pallas_call(
        body, name=name, in_specs=[ANY] * npk, out_specs=[ANY] * npk, out_shape=[SDS(g.shape, g.dtype) for g in ghs],
        scratch_shapes=[pltpu.SemaphoreType.DMA((npk,)), pltpu.SemaphoreType.DMA((npk,))],
        compiler_params=pltpu.CompilerParams(has_side_effects=True))(*ghs)


def _add_tile(rows, cols):
    return _div_tile(rows, max(16, (1 << 19) // max(cols, LANES)), 16)


def _pair_add(g, r1, c_idx, name):
    _, rows, cols = r1.shape
    tr = _add_tile(rows, cols)

    def body(c_ref, g_ref, r_ref, o_ref, ob_ref):
        s = g_ref[0] + r_ref[...]
        o_ref[...] = s
        ob_ref[...] = s.astype(BF16)

    own = BS((1, tr, cols), lambda k, i, c: (k, i, 0))
    return pl.pallas_call(
        body, name=name,
        grid_spec=pltpu.PrefetchScalarGridSpec(
            num_scalar_prefetch=1, grid=(r1.shape[0], rows // tr),
            in_specs=[BS((1, 1, tr, cols), lambda k, i, c: (k, c[0], i, 0)), own], out_specs=[own, own]),
        out_shape=[SDS(r1.shape, F32), SDS(r1.shape, BF16)],
        compiler_params=_cparams(("parallel", "parallel")))(c_idx, g, r1)


def _chip_add(s1, r2, j_idx, name):
    _, rows, cols = s1.shape
    tr = _add_tile(rows, cols)

    def body(j_ref, s_ref, r_ref, o_ref):
        o_ref[...] = ((s_ref[0] + r_ref[0].astype(F32)) + r_ref[1].astype(F32)) + r_ref[2].astype(F32)

    return pl.pallas_call(
        body, name=name,
        grid_spec=pltpu.PrefetchScalarGridSpec(
            num_scalar_prefetch=1, grid=(rows // tr,),
            in_specs=[BS((1, tr, cols), lambda i, j: (j[0], i, 0)), BS((3, tr, cols), lambda i, j: (0, i, 0))],
            out_specs=BS((tr, cols), lambda i, j: (i, 0))),
        out_shape=SDS((rows, cols), F32), compiler_params=_cparams(("parallel",)))(j_idx, s1, r2)


def _adamw_math(w, g, m, v):
    m = ADAM_B1 * m + (1.0 - ADAM_B1) * g
    v = ADAM_B2 * v + (1.0 - ADAM_B2) * (g * g)
    m_hat = m / (1.0 - ADAM_B1 ** ADAM_STEP)
    v_hat = v / (1.0 - ADAM_B2 ** ADAM_STEP)
    delta = -ADAM_LR * (m_hat / (jnp.sqrt(v_hat) + ADAM_EPS) + ADAM_WD * w)
    return delta, m, v


def _adamw(w, gh, go, m, v, c_idx, name):
    _, rows, cols = w.shape
    tr = _add_tile(rows, cols)

    def body(c_ref, w_ref, gh_ref, go_ref, m_ref, v_ref, g_out, d_out, m_out, v_out):
        g = jnp.where(pl.program_id(0) == c_ref[0], gh_ref[...], go_ref[...])
        g_out[0] = g
        d_out[0], m_out[0], v_out[0] = _adamw_math(w_ref[0], g, m_ref[0], v_ref[0])

    full = BS((1, tr, cols), lambda hf, i, c: (hf, i, 0))
    half = BS((tr, cols), lambda hf, i, c: (i, 0))
    return pl.pallas_call(
        body, name=name,
        grid_spec=pltpu.PrefetchScalarGridSpec(
            num_scalar_prefetch=1, grid=(2, rows // tr), in_specs=[full, half, half, full, full],
            out_specs=[full] * 4),
        out_shape=[SDS(w.shape, F32)] * 4, compiler_params=_cparams(("parallel", "parallel")))(c_idx, w, gh, go, m, v)


def _adamw_layers(w, g0, g1, m, v, c_idx, name):
    _, _, rows, cols = w.shape
    tr = _add_tile(rows, cols)

    def body(c_ref, w_ref, h0_ref, o0_ref, h1_ref, o1_ref, m_ref, v_ref, g_out, d_out, m_out, v_out):
        mine = pl.program_id(1) == c_ref[0]
        g = jnp.where(pl.program_id(0) == 0, jnp.where(mine, h0_ref[...], o0_ref[...]),
                      jnp.where(mine, h1_ref[...], o1_ref[...]))
        g_out[0, 0] = g
        d_out[0, 0], m_out[0, 0], v_out[0, 0] = _adamw_math(w_ref[0, 0], g, m_ref[0, 0], v_ref[0, 0])

    full = BS((1, 1, tr, cols), lambda l, hf, i, c: (l, hf, i, 0))
    half = BS((tr, cols), lambda l, hf, i, c: (i, 0))
    return pl.pallas_call(
        body, name=name,
        grid_spec=pltpu.PrefetchScalarGridSpec(
            num_scalar_prefetch=1, grid=(2, 2, rows // tr), in_specs=[full, half, half, half, half, full, full],
            out_specs=[full] * 4),
        out_shape=[SDS(w.shape, F32)] * 4,
        compiler_params=_cparams(("parallel", "parallel", "parallel")))(c_idx, w, *g0, *g1, m, v)


def _sum_devices(sa, name):
    def body(sa_ref, g_out):
        g = sa_ref[0]
        for d in range(1, N_DEV):
            g = g + sa_ref[d]
        g_out[...] = g

    return pl.pallas_call(body, name=name, out_shape=SDS(sa.shape[1:], F32),
                          compiler_params=pltpu.CompilerParams(vmem_limit_bytes=VMEM_LIMIT))(sa)


def _adamw_small(ws, gs, ms, vs, name):
    k = len(ws)

    def body(*refs):
        ins, outs = refs[:4 * k], refs[4 * k:]
        for i in range(k):
            d, m, v = _adamw_math(ins[i][...], ins[k + i][...], ins[2 * k + i][...], ins[3 * k + i][...])
            outs[i][...], outs[k + i][...], outs[2 * k + i][...] = d, m, v

    return pl.pallas_call(body, name=name, out_shape=[SDS(w.shape, F32) for w in ws] * 3,
                          compiler_params=pltpu.CompilerParams(vmem_limit_bytes=VMEM_LIMIT))(*ws, *gs, *ms, *vs)


HALVED = {"w_in": (2, 1024, 1450), "mla_w_q_up": (2, 256, 192), "mla_w_kv_up": (2, 128, 256),
          "w_branch": (2, 1536, 256), "w_o": (2, 256, 1024), "ffn_w_up": (2, 1024, 1408),
          "ffn_w_down": (2, 704, 1024), "ffn_conv_w": (2, 3, 1408), "meta_tokens": (2, 8, 256)}
SMALL_SHAPE = {"norm1_g": (2, 1024), "fox_forget_b": (2, 8), "fox_q_g": (2, 64), "fox_k_g": (2, 64),
               "mla_q_a_g": (2, 256), "mla_kv_a_g": (2, 128), "mla_q_g": (2, 96), "mla_k_g": (2, 96),
               "swa_q_g": (2, 64), "swa_k_g": (2, 64), "swa_sinks": (2, 8), "norm2_g": (2, 1024),
               "ffn_conv_b": (2, 5632)}
SMALL_ROWS = _pack_rows([SMALL_SHAPE[k] for k in SMALL] + [(1,)], SUBLANES)


def kernel(x, meta_tokens, norm1_g, w_in, fox_forget_b, fox_q_g, fox_k_g, mla_q_a_g, mla_w_q_up, mla_kv_a_g, mla_w_kv_up, mla_q_g, mla_k_g, swa_q_g, swa_k_g, swa_sinks, w_branch, w_o, norm2_g, ffn_w_up, ffn_conv_w, ffn_conv_b, ffn_w_down, loss_target, m_meta_tokens, m_norm1_g, m_w_in, m_fox_forget_b, m_fox_q_g, m_fox_k_g, m_mla_q_a_g, m_mla_w_q_up, m_mla_kv_a_g, m_mla_w_kv_up, m_mla_q_g, m_mla_k_g, m_swa_q_g, m_swa_k_g, m_swa_sinks, m_w_branch, m_w_o, m_norm2_g, m_ffn_w_up, m_ffn_conv_w, m_ffn_conv_b, m_ffn_w_down, v_meta_tokens, v_norm1_g, v_w_in, v_fox_forget_b, v_fox_q_g, v_fox_k_g, v_mla_q_a_g, v_mla_w_q_up, v_mla_kv_a_g, v_mla_w_kv_up, v_mla_q_g, v_mla_k_g, v_swa_q_g, v_swa_k_g, v_swa_sinks, v_w_branch, v_w_o, v_norm2_g, v_ffn_w_up, v_ffn_conv_w, v_ffn_conv_b, v_ffn_w_down):
    w = dict(meta_tokens=meta_tokens, norm1_g=norm1_g, w_in=w_in, fox_forget_b=fox_forget_b, fox_q_g=fox_q_g,
             fox_k_g=fox_k_g, mla_q_a_g=mla_q_a_g, mla_w_q_up=mla_w_q_up, mla_kv_a_g=mla_kv_a_g,
             mla_w_kv_up=mla_w_kv_up, mla_q_g=mla_q_g, mla_k_g=mla_k_g, swa_q_g=swa_q_g, swa_k_g=swa_k_g,
             swa_sinks=swa_sinks, w_branch=w_branch, w_o=w_o, norm2_g=norm2_g, ffn_w_up=ffn_w_up,
             ffn_conv_w=ffn_conv_w, ffn_conv_b=ffn_conv_b, ffn_w_down=ffn_w_down)
    m = dict(meta_tokens=m_meta_tokens, norm1_g=m_norm1_g, w_in=m_w_in, fox_forget_b=m_fox_forget_b,
             fox_q_g=m_fox_q_g, fox_k_g=m_fox_k_g, mla_q_a_g=m_mla_q_a_g, mla_w_q_up=m_mla_w_q_up,
             mla_kv_a_g=m_mla_kv_a_g, mla_w_kv_up=m_mla_w_kv_up, mla_q_g=m_mla_q_g, mla_k_g=m_mla_k_g,
             swa_q_g=m_swa_q_g, swa_k_g=m_swa_k_g, swa_sinks=m_swa_sinks, w_branch=m_w_branch, w_o=m_w_o,
             norm2_g=m_norm2_g, ffn_w_up=m_ffn_w_up, ffn_conv_w=m_ffn_conv_w, ffn_conv_b=m_ffn_conv_b,
             ffn_w_down=m_ffn_w_down)
    v = dict(meta_tokens=v_meta_tokens, norm1_g=v_norm1_g, w_in=v_w_in, fox_forget_b=v_fox_forget_b,
             fox_q_g=v_fox_q_g, fox_k_g=v_fox_k_g, mla_q_a_g=v_mla_q_a_g, mla_w_q_up=v_mla_w_q_up,
             mla_kv_a_g=v_mla_kv_a_g, mla_w_kv_up=v_mla_w_kv_up, mla_q_g=v_mla_q_g, mla_k_g=v_mla_k_g,
             swa_q_g=v_swa_q_g, swa_k_g=v_swa_k_g, swa_sinks=v_swa_sinks, w_branch=v_w_branch, w_o=v_w_o,
             norm2_g=v_norm2_g, ffn_w_up=v_ffn_w_up, ffn_conv_w=v_ffn_conv_w, ffn_conv_b=v_ffn_conv_b,
             ffn_w_down=v_ffn_w_down)
    xi, yi, ci = _me()
    c_idx = ci.astype(jnp.int32).reshape(1)
    j_idx = (2 * xi + yi).astype(jnp.int32).reshape(1)

    sh_names = BIG + FINE
    local = {k: (w[k].astype(BF16) if k in BIG else w[k]).reshape(HALVED[k]) for k in sh_names}
    late = [local[k] for k in LAYERED]
    meta_g, early = _gather_early(local["meta_tokens"], [local["w_in"]], "gather_early")
    meta = jnp.concatenate([meta_g[i].reshape(SHARD_SHAPE["meta_tokens"]) for i in range(N_CHIPS)], axis=1)
    w0 = _assemble_layer(_layer_parts(("w_in",), early))
    small = {k: w[k] for k in SMALL}

    loss, g_x, g_eps, g_red, g_cw, g_meta, g_small = _local_step(x[0], loss_target[0], w0, late, meta, small)
    nb = len(BEFORE_MLA)
    g_after0 = dict(zip(FFN_PAIR + AFTER_FOX, g_red[0] + g_red[1][nb:]))
    g_l1 = dict(zip(BEFORE_MLA + FFN_PAIR + AFTER_FOX, g_red[1][:nb] + g_red[2] + g_red[3]))

    quarter = {k: (2, HALVED[k][1] // 2, HALVED[k][2]) for k in BIG}
    last = BEFORE_MLA + FINE
    gs = [g_eps[k].reshape((N_CHIPS,) + quarter[k]) for k in BEFORE_MLA]
    for k, g in (("meta_tokens", g_meta), ("ffn_conv_w", g_cw)):
        gs.append(jnp.stack(jnp.split(g, N_CHIPS, axis=SHARD_AXIS[k])).reshape((N_CHIPS,) + HALVED[k]))
    spack = _pack([g_small[k] for k in SMALL] + [loss.reshape(1)], SMALL_ROWS, F32)
    r1 = _pair_exchange(gs, "grads_pair_exchange")
    s1 = [_pair_add(g, r, c_idx, "grads_pair_add_" + k) for g, r, k in zip(gs, r1, last)]
    r2, sa = _chip_exchange([s[1] for s in s1], spack, "grads_chip_exchange")
    gh = dict(zip(last, [_chip_add(s[0], r, j_idx, "grads_chip_add_" + k) for s, r, k in zip(s1, r2, last)]))
    go = dict(zip(last, _half_exchange([gh[k] for k in last], "grads_half_exchange")))

    grads, deltas, new_m, new_v = {}, {}, {}, {}
    for k in sh_names:
        if k in BIG:
            shp = (2,) + quarter[k]
            wk, mk, vk = w[k].reshape(shp), m[k].reshape(shp), v[k].reshape(shp)
            g0 = (gh[k], go[k]) if k in BEFORE_MLA else g_after0[k]
            outs = _adamw_layers(wk, g0, g_l1[k], mk, vk, c_idx, "adamw_" + k)
        else:
            outs = _adamw(w[k].reshape(HALVED[k]), gh[k], go[k], m[k].reshape(HALVED[k]), v[k].reshape(HALVED[k]),
                          c_idx, "adamw_" + k)
        for dst, o in zip((grads, deltas, new_m, new_v), outs):
            dst[k] = o.reshape(SHARD_SHAPE[k])
    sm_shapes = [SMALL_SHAPE[k] for k in SMALL] + [(1,)]
    g_sum = _unpack(_sum_devices(sa, "sum_small"), sm_shapes)
    res = _adamw_small([w[k] for k in SMALL], g_sum[:-1], [m[k] for k in SMALL], [v[k] for k in SMALL], "adamw_small")
    ns = len(SMALL)
    grads.update(zip(SMALL, g_sum[:-1]))
    for dst, vals in zip((deltas, new_m, new_v), (res[:ns], res[ns:2 * ns], res[2 * ns:])):
        dst.update(zip(SMALL, vals))
    total_loss = g_sum[-1][0]
    return (total_loss, g_x[None], *[grads[k] for k in WEIGHTS], *[deltas[k] for k in WEIGHTS],
            *[new_m[k] for k in WEIGHTS], *[new_v[k] for k in WEIGHTS])
```

```python
import jax
import jax.numpy as jnp
from jax import lax
from jax.experimental import pallas as pl
from jax.experimental.pallas import tpu as pltpu

F32 = jnp.float32
BF16 = jnp.bfloat16
SDS = jax.ShapeDtypeStruct
BS = pl.BlockSpec

D_MODEL = 1024
DEPTH = 2
N_META = 16
EPS = 1e-6
HEADS = 8
HEAD_DIM = 64
MLA_Q_RANK = 256
MLA_KV_RANK = 128
MLA_NOPE = 64
MLA_ROPE = 32
MLA_QK = MLA_NOPE + MLA_ROPE
ROPE_THETA = 10000.0
SWA_KV_HEADS = 2
WINDOW = 128
D_FF = 2816
IN_PAD = 6144
N_CHIPS = 4
N_DEV = 8

ADAM_LR = 0.001
ADAM_B1 = 0.9
ADAM_B2 = 0.999
ADAM_EPS = 1e-08
ADAM_WD = 0.01
ADAM_STEP = 10

LANES = 128
SUBLANES = 8
ROW_PAD = 128
CAUSAL_TILE = 384
NEG = -1e30
VMEM_LIMIT = 56 * 1024 * 1024

O_FQ, O_FK, O_FV, O_FF = 0, 512, 1024, 1536
O_CQ, O_CKV, O_KR = 1664, 1920, 2048
O_SQ, O_SK, O_SV, O_G = 2176, 2688, 2816, 2944

SHARD_AXIS = {"meta_tokens": 1, "w_in": 2, "mla_w_q_up": 2, "mla_w_kv_up": 2, "w_branch": 3, "w_o": 1,
              "ffn_w_up": 2, "ffn_conv_w": 2, "ffn_w_down": 1}
SHARD_SHAPE = {"meta_tokens": (16, 256), "w_in": (2, 1024, 1450), "mla_w_q_up": (2, 256, 192),
               "mla_w_kv_up": (2, 128, 256), "w_branch": (2, 3, 512, 256), "w_o": (2, 256, 1024),
               "ffn_w_up": (2, 1024, 1408), "ffn_conv_w": (2, 3, 1408), "ffn_w_down": (2, 704, 1024)}
BIG = ("w_in", "mla_w_q_up", "mla_w_kv_up", "w_branch", "w_o", "ffn_w_up", "ffn_w_down")
FINE = ("meta_tokens", "ffn_conv_w")
SMALL = ("norm1_g", "fox_forget_b", "fox_q_g", "fox_k_g", "mla_q_a_g", "mla_kv_a_g", "mla_q_g", "mla_k_g",
         "swa_q_g", "swa_k_g", "swa_sinks", "norm2_g", "ffn_conv_b")
WEIGHTS = ("meta_tokens", "norm1_g", "w_in", "fox_forget_b", "fox_q_g", "fox_k_g", "mla_q_a_g", "mla_w_q_up",
           "mla_kv_a_g", "mla_w_kv_up", "mla_q_g", "mla_k_g", "swa_q_g", "swa_k_g", "swa_sinks", "w_branch", "w_o",
           "norm2_g", "ffn_w_up", "ffn_conv_w", "ffn_conv_b", "ffn_w_down")


def _cparams(sem):
    return pltpu.CompilerParams(dimension_semantics=sem, vmem_limit_bytes=VMEM_LIMIT)


def _div_tile(n, cap, mult=SUBLANES):
    best = None
    for t in range(mult, min(n, cap) + 1, mult):
        if n % t == 0:
            best = t
    return best if best is not None else n


def _rows_tile(n, width, budget=2 << 20):
    return _div_tile(n, max(SUBLANES, budget // (4 * max(width, LANES))))


def _op(fwd, bwd):
    @jax.custom_vjp
    def op(*args):
        return fwd(*args)[0]
    op.defvjp(fwd, bwd)
    return op


def _rotate(v, cos, sin):
    lane = lax.broadcasted_iota(jnp.int32, v.shape, 1)
    rot = jnp.where(lane < MLA_NOPE + MLA_ROPE // 2, -pltpu.roll(v, LANES - MLA_ROPE // 2, 1),
                    pltpu.roll(v, MLA_ROPE // 2, 1))
    return v * cos + rot * sin


def _rms_fwd_call(x, g, denom, name, rot=None):
    n, c = x.shape
    tr = _rows_tile(n if rot is None else rot[0].shape[0], c)
    nt = None if rot is None else rot[0].shape[0] // tr

    def body(x_ref, g_ref, *rest):
        y_ref = rest[-1]
        xv = x_ref[...]
        ms = jnp.sum(xv * xv, axis=-1, keepdims=True) * (1.0 / denom)
        y = xv * lax.rsqrt(ms + EPS) * g_ref[...]
        y_ref[...] = y if rot is None else _rotate(y, rest[0][...], rest[1][...])

    ins, args = [BS((tr, c), lambda i: (i, 0)), BS((1, c), lambda i: (0, 0))], [x, g]
    if rot is not None:
        ins += [BS((tr, c), lambda i: (i % nt, 0))] * 2
        args += list(rot)
    return pl.pallas_call(
        body, name=name, grid=(n // tr,), in_specs=ins,
        out_specs=BS((tr, c), lambda i: (i, 0)), out_shape=SDS((n, c), F32),
        compiler_params=_cparams(("parallel",)))(*args)


def _rms_bwd_call(x, g, dy, denom, name, rot=None):
    n, c = x.shape
    tr = _rows_tile(n if rot is None else rot[0].shape[0], c)
    nt = None if rot is None else rot[0].shape[0] // tr

    def body(x_ref, g_ref, dy_ref, *rest):
        dx_ref, dg_ref = rest[-2:]
        xv = x_ref[...]
        dy = dy_ref[...]
        if rot is not None:
            dy = _rotate(dy, rest[0][...], -rest[1][...])
        ms = jnp.sum(xv * xv, axis=-1, keepdims=True) * (1.0 / denom)
        r = lax.rsqrt(ms + EPS)
        xh = xv * r
        dxh = dy * g_ref[...]
        dx_ref[...] = r * (dxh - xh * (jnp.sum(dxh * xh, axis=-1, keepdims=True) * (1.0 / denom)))

        @pl.when(pl.program_id(0) == 0)
        def _():
            dg_ref[...] = jnp.zeros_like(dg_ref)

        dg_ref[...] += jnp.sum(dy * xh, axis=0, keepdims=True)

    ins = [BS((tr, c), lambda i: (i, 0)), BS((1, c), lambda i: (0, 0)), BS((tr, c), lambda i: (i, 0))]
    args = [x, g, dy]
    if rot is not None:
        ins += [BS((tr, c), lambda i: (i % nt, 0))] * 2
        args += list(rot)
    return pl.pallas_call(
        body, name=name, grid=(n // tr,), in_specs=ins,
        out_specs=[BS((tr, c), lambda i: (i, 0)), BS((1, c), lambda i: (0, 0))],
        out_shape=[SDS((n, c), F32), SDS((1, c), F32)],
        compiler_params=_cparams(("arbitrary",)))(*args)


def rms_norm(x, g, denom, name):
    def fwd(x, g):
        return _rms_fwd_call(x, g, denom, name + "_f"), (x, g)

    def bwd(res, dy):
        return tuple(_rms_bwd_call(res[0], res[1], dy, denom, name + "_b"))

    return _op(fwd, bwd)(x, g)


def rms_norm_rope(x, g, cos, sin, denom, name):
    def fwd(x, g, cos, sin):
        return _rms_fwd_call(x, g, denom, name + "_f", (cos, sin)), (x, g, cos, sin)

    def bwd(res, dy):
        x, g, cos, sin = res
        dx, dg = _rms_bwd_call(x, g, dy, denom, name + "_b", (cos, sin))
        return dx, dg, jnp.zeros_like(cos), jnp.zeros_like(sin)

    return _op(fwd, bwd)(x, g, cos, sin)


def _mm_call(a, b, mode, res, name):
    if mode == "nn":
        (m, kc), n = a.shape, b.shape[1]
    elif mode == "nt":
        (m, kc), n = a.shape, b.shape[0]
    else:
        (kc, m), n = a.shape, b.shape[1]
    if mode == "tn":
        tk = _div_tile(kc, 528)
        tm = _div_tile(m, 1408, LANES)
        tn = _div_tile(n, 2048, LANES)
    else:
        tk = kc if kc <= 2816 else _div_tile(kc, 1024, LANES)
        tm = _div_tile(m, max(LANES, (9 << 19) // (4 * tk)))
        tn = _div_tile(n, 1408 if mode == "nt" else 512, LANES)
    nk = kc // tk
    dims = {"nn": (((1,), (0,)), ((), ())), "nt": (((1,), (1,)), ((), ())), "tn": (((0,), (0,)), ((), ()))}[mode]

    def body(*refs):
        if res is None:
            a_ref, b_ref, o_ref, acc_ref = refs
            r_ref = None
        else:
            a_ref, b_ref, r_ref, o_ref, acc_ref = refs
        k = pl.program_id(2)

        @pl.when(k == 0)
        def _():
            acc_ref[...] = jnp.zeros_like(acc_ref)

        acc_ref[...] += lax.dot_general(a_ref[...].astype(BF16), b_ref[...].astype(BF16), dims,
                                        preferred_element_type=F32)

        @pl.when(k == nk - 1)
        def _():
            if r_ref is None:
                o_ref[...] = acc_ref[...]
            else:
                o_ref[...] = r_ref[...] + acc_ref[...]

    a_spec = BS((tk, tm), lambda i, j, k: (k, i)) if mode == "tn" else BS((tm, tk), lambda i, j, k: (i, k))
    b_spec = BS((tn, tk), lambda i, j, k: (j, k)) if mode == "nt" else BS((tk, tn), lambda i, j, k: (k, j))
    o_spec = BS((tm, tn), lambda i, j, k: (i, j))
    ins, args = [a_spec, b_spec], [a, b]
    if res is not None:
        ins.append(o_spec)
        args.append(res)
    return pl.pallas_call(
        body, name=name, grid=(m // tm, n // tn, nk), in_specs=ins, out_specs=o_spec,
        out_shape=SDS((m, n), F32), scratch_shapes=[pltpu.VMEM((tm, tn), F32)],
        compiler_params=_cparams(("parallel", "parallel", "arbitrary")))(*args)


def linear(a, w, eps, name, res=None):
    if res is None:
        def fwd(a, w, eps):
            return _mm_call(a, w, "nn", None, name + "_f"), (a, w)

        def bwd(r, dc):
            a, w = r
            return (_mm_call(dc, w, "nt", None, name + "_da"), jnp.zeros_like(w),
                    _mm_call(a, dc, "tn", None, name + "_dw"))

        return _op(fwd, bwd)(a, w, eps)

    def fwd_r(a, w, eps, res):
        return _mm_call(a, w, "nn", res, name + "_f"), (a, w)

    def bwd_r(r, dc):
        a, w = r
        return (_mm_call(dc, w, "nt", None, name + "_da"), jnp.zeros_like(w),
                _mm_call(a, dc, "tn", None, name + "_dw"), dc)

    return _op(fwd_r, bwd_r)(a, w, eps, res)


def _gate_tile(n):
    return _div_tile(n, CAUSAL_TILE, LANES)


def _tri_dot(v, upper):
    ct = v.shape[1]
    r = lax.broadcasted_iota(jnp.int32, (ct, ct), 0)
    c = lax.broadcasted_iota(jnp.int32, (ct, ct), 1)
    tri = jnp.where((r <= c) if upper else (r >= c), 1.0, 0.0).astype(F32)
    return jnp.dot(v, tri, preferred_element_type=F32, precision=lax.Precision.HIGHEST)


def _gate_fwd_call(z, b, name):
    h, n = z.shape
    CT = _gate_tile(n)

    def body(z_ref, b_ref, c_ref, carry):
        @pl.when(pl.program_id(0) == 0)
        def _():
            carry[...] = jnp.zeros_like(carry)

        x = z_ref[...] + b_ref[...]
        ls = jnp.minimum(x, 0.0) - jnp.log(1.0 + jnp.exp(-jnp.abs(x)))
        c_ref[...] = _tri_dot(ls, True) + carry[...]
        carry[...] += jnp.sum(ls, axis=1, keepdims=True)

    return pl.pallas_call(
        body, name=name, grid=(n // CT,),
        in_specs=[BS((h, CT), lambda j: (0, j)), BS((h, 1), lambda j: (0, 0))],
        out_specs=BS((h, CT), lambda j: (0, j)), out_shape=SDS((h, n), F32),
        scratch_shapes=[pltpu.VMEM((h, 1), F32)],
        compiler_params=_cparams(("arbitrary",)))(z, b)


def _gate_bwd_call(z, b, dc, name):
    h, n = z.shape
    CT = _gate_tile(n)
    nt = n // CT

    def body(z_ref, b_ref, dc_ref, dz_ref, db_ref, carry):
        @pl.when(pl.program_id(0) == 0)
        def _():
            carry[...] = jnp.zeros_like(carry)
            db_ref[...] = jnp.zeros_like(db_ref)

        dcv = dc_ref[...]
        dls = _tri_dot(dcv, False) + carry[...]
        carry[...] += jnp.sum(dcv, axis=1, keepdims=True)
        x = z_ref[...] + b_ref[...]
        e = jnp.exp(-jnp.abs(x))
        dz = dls * jnp.where(x >= 0, e / (1.0 + e), 1.0 / (1.0 + e))
        dz_ref[...] = dz
        db_ref[...] += jnp.sum(dz, axis=1, keepdims=True)

    rev = lambda j: (0, nt - 1 - j)
    return pl.pallas_call(
        body, name=name, grid=(nt,),
        in_specs=[BS((h, CT), rev), BS((h, 1), lambda j: (0, 0)), BS((h, CT), rev)],
        out_specs=[BS((h, CT), rev), BS((h, 1), lambda j: (0, 0))],
        out_shape=[SDS((h, n), F32), SDS((h, 1), F32)],
        scratch_shapes=[pltpu.VMEM((h, 1), F32)],
        compiler_params=_cparams(("arbitrary",)))(z, b, dc)


def forget_cumsum(z, b, name):
    def fwd(z, b):
        return _gate_fwd_call(z, b, name + "_f"), (z, b)

    def bwd(res, dc):
        return tuple(_gate_bwd_call(res[0], res[1], dc, name + "_b"))

    return _op(fwd, bwd)(z, b)


NT_DIMS = (((1,), (1,)), ((), ()))
TN_DIMS = (((0,), (0,)), ((), ()))
HEADS_PER_STEP = 2
FWD_HEADS_PER_STEP = 4


def _causal_tile(n):
    return CAUSAL_TILE if n % CAUSAL_TILE == 0 else ROW_PAD


def _causal_fwd_call(q, k, v, ck_r, fox, scale, name, late=None, late_layer=1):
    h, n, dk = q.shape
    dv = v.shape[2]
    t = _causal_tile(n)
    nq = n // t
    hb = FWD_HEADS_PER_STEP
    nl = 0 if late is None else len(late)
    n_in = 3 + int(fox) + nl

    def body(*refs):
        q_ref, k_ref, v_ref = refs[:3]
        ck_ref = refs[3] if fox else None
        o_ref, lse_ref = refs[n_in:n_in + 2]
        m_scr, l_scr, acc_scr = refs[n_in + 2 + nl:n_in + 5 + nl]
        qi = pl.program_id(1)
        if nl:
            start, forward, drain, receive = _layer_gather(refs[n_in - nl:n_in], refs[n_in + 2:n_in + 2 + nl],
                                                           refs[-2], refs[-1], late_layer, 0)
            hp, core = pl.program_id(0), lax.axis_index("c")
            last = (hp == h // hb - 1) & (qi == nq - 1)
            pl.when((hp == 0) & (qi == 0) & (core == late_layer))(start)
            pl.when((hp == h // hb // 2) & (qi == 0) & (core == late_layer))(forward)
            pl.when(last & (core == late_layer))(drain)
            pl.when(last & (core == 1 - late_layer))(receive)
        qbs = [q_ref[e].astype(BF16) for e in range(hb)]
        m_scr[...] = jnp.full(m_scr.shape, NEG, F32)
        l_scr[...] = jnp.zeros_like(l_scr)
        acc_scr[...] = jnp.zeros_like(acc_scr)

        def process(j, masked):
            off = pl.multiple_of(j * t, t)
            if masked:
                rows = lax.broadcasted_iota(jnp.int32, (t, t), 0)
                cols = lax.broadcasted_iota(jnp.int32, (t, t), 1)
                valid = cols <= rows
            for e in range(hb):
                kb = k_ref[e, pl.ds(off, t), :].astype(BF16)
                vb = v_ref[e, pl.ds(off, t), :].astype(BF16)
                s = lax.dot_general(qbs[e], kb, NT_DIMS, preferred_element_type=F32) * scale
                if fox:
                    s = s - ck_ref[e, j]
                if masked:
                    s = jnp.where(valid, s, NEG)
                m_old = m_scr[e]
                m_new = jnp.maximum(m_old, jnp.max(s, axis=1, keepdims=True))
                alpha = jnp.exp(m_old - m_new)
                p = jnp.exp(s - jnp.tile(m_new, (1, t // LANES)))
                l_scr[e] = alpha * l_scr[e] + jnp.sum(p, axis=1, keepdims=True)
                acc_scr[e] = alpha[:, :dv] * acc_scr[e] + jnp.dot(p.astype(BF16), vb, preferred_element_type=F32)
                m_scr[e] = m_new

        def step(j, carry):
            process(j, False)
            return carry

        lax.fori_loop(0, qi, step, 0)
        process(qi, True)
        for e in range(hb):
            l = l_scr[e]
            o_ref[e] = acc_scr[e] / l[:, :dv]
            lse_ref[e, 0] = jnp.transpose(m_scr[e] + jnp.log(l))[0:1, :]

    ins = [BS((hb, t, dk), lambda a, b: (a, b, 0)), BS((hb, n, dk), lambda a, b: (a, 0, 0)),
           BS((hb, n, dv), lambda a, b: (a, 0, 0))]
    args = [q, k, v]
    if fox:
        ins.append(BS((hb, nq, 1, t), lambda a, b: (a, 0, 0, 0)))
        args.append(ck_r)
    outs = [BS((hb, t, dv), lambda a, b: (a, b, 0)), BS((hb, 1, 1, t), lambda a, b: (a, b, 0, 0))]
    oshape = [SDS((h, n, dv), F32), SDS((h, nq, 1, t), F32)]
    scratch = [pltpu.VMEM((hb, t, LANES), F32), pltpu.VMEM((hb, t, LANES), F32), pltpu.VMEM((hb, t, dv), F32)]
    if nl:
        ins += [ANY] * nl
        args += list(late)
        outs += [ANY] * nl
        oshape += [SDS((N_CHIPS,) + a.shape[1:], a.dtype) for a in late]
        scratch += [pltpu.SemaphoreType.DMA((6 * nl,)), pltpu.SemaphoreType.DMA((6 * nl,))]
    res = pl.pallas_call(
        body, name=name, grid=(h // hb, nq), in_specs=ins, out_specs=outs, out_shape=oshape, scratch_shapes=scratch,
        compiler_params=pltpu.CompilerParams(dimension_semantics=("arbitrary", "arbitrary"),
                                             vmem_limit_bytes=VMEM_LIMIT, has_side_effects=bool(nl)))(*args)
    return res[0], res[1], list(res[2:])


def _causal_bwd_call(q, k, v, do, o, lse_r, ck_r, fox, scale, name, side=None):
    h, n, dk = q.shape
    dv = v.shape[2]
    t = _causal_tile(n)
    nq = n // t
    hb = HEADS_PER_STEP
    ns = 0 if side is None else len(side)

    def body(*refs):
        it = iter(refs)
        q_ref, k_ref, v_ref, do_ref, o_ref, lse_ref = (next(it) for _ in range(6))
        ck_ref = next(it) if fox else None
        side_in = [next(it) for _ in range(ns)]
        dq_ref, dk_ref, dv_ref = next(it), next(it), next(it)
        dck_ref, dcq_ref = (next(it), next(it)) if fox else (None, None)
        side_out = [next(it) for _ in range(ns)]
        delta_scr, dk_scr, dv_scr = next(it), next(it), next(it)
        dck_scr = next(it) if fox else None
        kj = pl.program_id(1)
        if ns:
            send_sems, recv_sems = next(it), next(it)
            x, y, core = _me()
            chips = [(1 - x, y), (x, 1 - y), (1 - x, 1 - y)]
            copies = [_remote(side_in[p].at[2 * cx + cy], side_out[p].at[r], send_sems, recv_sems, 3 * p + r,
                              (cx, cy, core)) for p in range(ns) for r, (cx, cy) in enumerate(chips)]

            @pl.when((pl.program_id(0) == 0) & (kj == 0))
            def _():
                for cp in copies:
                    cp.start()

            @pl.when((pl.program_id(0) == h // hb - 1) & (kj == nq - 1))
            def _():
                for cp in copies:
                    cp.wait()

        @pl.when(kj == 0)
        def _():
            dq_ref[...] = jnp.zeros_like(dq_ref)
            if fox:
                dcq_ref[...] = jnp.zeros_like(dcq_ref)
            ones = jnp.ones((SUBLANES, dv), F32)

            def fill(qi, carry):
                off = pl.multiple_of(qi * t, t)
                for e in range(hb):
                    prod = do_ref[e, pl.ds(off, t), :] * o_ref[e, pl.ds(off, t), :]
                    delta_scr[e, qi] = lax.dot_general(ones, prod, NT_DIMS, preferred_element_type=F32,
                                                       precision=lax.Precision.HIGHEST)[0:1, :]
                return carry

            lax.fori_loop(0, nq, fill, 0)

        kbs = [k_ref[e].astype(BF16) for e in range(hb)]
        vbs = [v_ref[e].astype(BF16) for e in range(hb)]
        dk_scr[...] = jnp.zeros_like(dk_scr)
        dv_scr[...] = jnp.zeros_like(dv_scr)
        if fox:
            dck_scr[...] = jnp.zeros_like(dck_scr)
            ckcs = [jnp.tile(jnp.transpose(jnp.broadcast_to(ck_ref[e, 0], (LANES, t))), (1, t // LANES))
                    for e in range(hb)]

        def process(qi, masked):
            off = pl.multiple_of(qi * t, t)
            if masked:
                krows = lax.broadcasted_iota(jnp.int32, (t, t), 0)
                qcols = lax.broadcasted_iota(jnp.int32, (t, t), 1)
                valid = krows <= qcols
            for e in range(hb):
                qb = q_ref[e, pl.ds(off, t), :].astype(BF16)
                dob = do_ref[e, pl.ds(off, t), :].astype(BF16)
                st = lax.dot_general(kbs[e], qb, NT_DIMS, preferred_element_type=F32) * scale
                if fox:
                    st = st - ckcs[e]
                pt = jnp.exp(st - lse_ref[e, qi])
                if masked:
                    pt = jnp.where(valid, pt, 0.0)
                dv_scr[e] += jnp.dot(pt.astype(BF16), dob, preferred_element_type=F32)
                dpt = lax.dot_general(vbs[e], dob, NT_DIMS, preferred_element_type=F32)
                dst = pt * (dpt - delta_scr[e, qi])
                if fox:
                    dck_scr[e] -= jnp.sum(dst, axis=1, keepdims=True)
                    dcq_ref[e, qi] += jnp.sum(dst, axis=0, keepdims=True)
                dsb = (dst * scale).astype(BF16)
                dk_scr[e] += jnp.dot(dsb, qb, preferred_element_type=F32)
                dq_ref[e, pl.ds(off, t), :] += lax.dot_general(dsb, kbs[e], TN_DIMS, preferred_element_type=F32)

        def step(qi, carry):
            process(qi, False)
            return carry

        process(kj, True)
        lax.fori_loop(kj + 1, nq, step, 0)
        dk_ref[...] = dk_scr[...]
        dv_ref[...] = dv_scr[...]
        if fox:
            for e in range(hb):
                dck_ref[e, 0] = jnp.transpose(jnp.broadcast_to(dck_scr[e], (t, LANES)))[0:1, :]

    whole = lambda a, b: (a, 0, 0)
    tile = lambda a, b: (a, b, 0)
    rowv = lambda a, b: (a, 0, 0, 0)
    rowt = lambda a, b: (a, b, 0, 0)
    ins = [BS((hb, n, dk), whole), BS((hb, t, dk), tile), BS((hb, t, dv), tile), BS((hb, n, dv), whole),
           BS((hb, n, dv), whole), BS((hb, nq, 1, t), rowv)]
    args = [q, k, v, do, o, lse_r]
    outs = [BS((hb, n, dk), whole), BS((hb, t, dk), tile), BS((hb, t, dv), tile)]
    oshape = [SDS((h, n, dk), F32), SDS((h, n, dk), F32), SDS((h, n, dv), F32)]
    scratch = [pltpu.VMEM((hb, nq, 1, t), F32), pltpu.VMEM((hb, t, dk), F32), pltpu.VMEM((hb, t, dv), F32)]
    if fox:
        ins.append(BS((hb, 1, 1, t), rowt))
        args.append(ck_r)
        outs += [BS((hb, 1, 1, t), rowt), BS((hb, nq, 1, t), rowv)]
        oshape += [SDS((h, nq, 1, t), F32), SDS((h, nq, 1, t), F32)]
        scratch.append(pltpu.VMEM((hb, t, 1), F32))
    if ns:
        ins += [ANY] * ns
        args += list(side)
        outs += [ANY] * ns
        oshape += [SDS((3,) + s.shape[1:], s.dtype) for s in side]
        scratch += [pltpu.SemaphoreType.DMA((3 * ns,)), pltpu.SemaphoreType.DMA((3 * ns,))]
    return pl.pallas_call(
        body, name=name, grid=(h // hb, nq), in_specs=ins, out_specs=outs, out_shape=oshape, scratch_shapes=scratch,
        compiler_params=pltpu.CompilerParams(dimension_semantics=("arbitrary", "arbitrary"),
                                             vmem_limit_bytes=VMEM_LIMIT, has_side_effects=bool(ns)))(*args)


def causal_attention(q, k, v, c, scale, name, late=None, late_layer=1, reduce=None, sinks=None):
    h, n, _ = q.shape
    t = _causal_tile(n)
    nq = n // t
    fox = c is not None

    def run_fwd(q, k, v, c, late, sinks):
        ck_r = c.reshape(h, nq, 1, t) if fox else None
        o, lse, got = _causal_fwd_call(q, k, v, ck_r, fox, scale, name + "_f", late, late_layer)
        out = (o,)
        if late is not None:
            out += (got,)
        if reduce is not None:
            out += ([jnp.zeros((N_CHIPS,) + SHARD_SHAPE[w][1:], F32) for w in reduce],)
        return (out if len(out) > 1 else o), (q, k, v, c, late, o, lse)

    def run_bwd(res, ct):
        q, k, v, c, late, o, lse = res
        ck_r = c.reshape(h, nq, 1, t) if fox else None
        dlate = None if late is None else [jnp.zeros_like(a) for a in late]
        if reduce is None:
            do = ct if late is None else ct[0]
            outs = _causal_bwd_call(q, k, v, do, o, lse, ck_r, fox, scale, name + "_b")
            return outs[0], outs[1], outs[2], ((outs[3] + outs[4]).reshape(h, n) if fox else None), dlate, None
        do, g1 = ct[0], ct[-1]
        xi, yi, ci = _me()
        c_idx = ci.astype(jnp.int32).reshape(1)
        j_idx = (2 * xi + yi).astype(jnp.int32).reshape(1)
        gs = [g.reshape((N_CHIPS,) + HALVED[w][1:]) for g, w in zip(g1, reduce)]
        outs = _causal_bwd_call(q, k, v, do, o, lse, ck_r, fox, scale, name + "_b", [g.astype(BF16) for g in gs])
        sc = [_chip_add(g, r, j_idx, name + "_chip_add_" + w).reshape(1, 2, HALVED[w][1] // 2, HALVED[w][2])
              for g, r, w in zip(gs, outs[-len(reduce):], reduce)]
        r1 = _pair_exchange(sc, name + "_pair_exchange")
        gh = [_pair_add(s, r, c_idx, name + "_pair_add_" + w)[0][0] for s, r, w in zip(sc, r1, reduce)]
        go = _half_exchange(gh, name + "_half_exchange")
        return (outs[0], outs[1], outs[2], ((outs[3] + outs[4]).reshape(h, n) if fox else None), dlate,
                list(zip(gh, go)))

    return _op(run_fwd, run_bwd)(q, k, v, c, late, sinks)


SWA_T = 128


def _swa_masks(qi):
    t = SWA_T
    r = lax.broadcasted_iota(jnp.int32, (t, 3 * t), 0)
    c = lax.broadcasted_iota(jnp.int32, (t, 3 * t), 1)
    seg0 = c < t
    seg1 = (c >= t) & (c < 2 * t)
    jp = jnp.maximum(qi - 1, 0)
    kpos = jnp.where(seg0, c, jnp.where(seg1, jp * t + c - t, qi * t + c - 2 * t))
    dist = qi * t + r - kpos
    band = (dist >= 0) & ((dist < WINDOW) | (kpos < N_META))
    valid = (seg0 & (kpos < N_META) & (qi >= 2)) | (jnp.logical_not(seg0) & band & (jnp.logical_not(seg1) | (qi >= 1)))
    return valid, dist.astype(F32)


def _swa_cat(ref, qi):
    t = SWA_T
    jp = jnp.maximum(qi - 1, 0)
    return jnp.concatenate([ref[0, 0:t, :], ref[0, pl.ds(pl.multiple_of(jp * t, t), t), :],
                            ref[0, pl.ds(pl.multiple_of(qi * t, t), t), :]], axis=0).astype(BF16)


def _swa_fwd_call(q, k, v, sinks, slopes, scale, name):
    hq, n, d = q.shape
    hkv = k.shape[0]
    g = hq // hkv
    t = SWA_T
    nq = n // t

    def body(q_ref, k_ref, v_ref, sink_ref, slope_ref, o_ref, lse_ref):
        grp = pl.program_id(0)
        qi = pl.program_id(1)
        valid, dist = _swa_masks(qi)
        kc = _swa_cat(k_ref, qi)
        vc = _swa_cat(v_ref, qi)
        qs = jnp.concatenate([q_ref[e] for e in range(g)], axis=0).astype(BF16)
        s_all = lax.dot_general(qs, kc, NT_DIMS, preferred_element_type=F32) * scale
        ps, ls, ms = [], [], []
        for e in range(g):
            hh = grp * g + e
            s = jnp.where(valid, s_all[e * t:(e + 1) * t] - slope_ref[hh] * dist, NEG)
            m = jnp.maximum(jnp.max(s, axis=1, keepdims=True), sink_ref[hh])
            p = jnp.exp(s - m)
            ls.append(jnp.sum(p, axis=1, keepdims=True) + jnp.exp(sink_ref[hh] - m))
            ms.append(m)
            ps.append(p.astype(BF16))
        acc = jnp.dot(jnp.concatenate(ps, axis=0), vc, preferred_element_type=F32)
        for e in range(g):
            o_ref[e] = acc[e * t:(e + 1) * t] / ls[e]
            lse_ref[e] = ms[e] + jnp.log(ls[e])

    return pl.pallas_call(
        body, name=name, grid=(hkv, nq),
        in_specs=[BS((g, t, d), lambda a, b: (a, b, 0)), BS((1, n, d), lambda a, b: (a, 0, 0)),
                  BS((1, n, d), lambda a, b: (a, 0, 0)), BS(memory_space=pltpu.SMEM), BS(memory_space=pltpu.SMEM)],
        out_specs=[BS((g, t, d), lambda a, b: (a, b, 0)), BS((g, t, 1), lambda a, b: (a, b, 0))],
        out_shape=[SDS((hq, n, d), F32), SDS((hq, n, 1), F32)],
        compiler_params=_cparams(("parallel", "parallel")))(q, k, v, sinks, slopes)


def _swa_bwd_call(q, k, v, o, lse, do, sinks, slopes, scale, name):
    hq, n, d = q.shape
    hkv = k.shape[0]
    g = hq // hkv
    t = SWA_T
    nq = n // t

    def body(q_ref, k_ref, v_ref, o_ref, lse_ref, do_ref, sink_ref, slope_ref, dq_ref, dk_ref, dv_ref, ds_ref):
        grp = pl.program_id(0)
        qi = pl.program_id(1)

        @pl.when(qi == 0)
        def _():
            dk_ref[...] = jnp.zeros_like(dk_ref)
            dv_ref[...] = jnp.zeros_like(dv_ref)
            ds_ref[...] = jnp.zeros_like(ds_ref)

        valid, dist = _swa_masks(qi)
        kc = _swa_cat(k_ref, qi)
        vc = _swa_cat(v_ref, qi)
        qs = jnp.concatenate([q_ref[e] for e in range(g)], axis=0).astype(BF16)
        dos = jnp.concatenate([do_ref[e] for e in range(g)], axis=0).astype(BF16)
        s_all = lax.dot_general(qs, kc, NT_DIMS, preferred_element_type=F32) * scale
        dp_all = lax.dot_general(dos, vc, NT_DIMS, preferred_element_type=F32)
        ps, dss = [], []
        for e in range(g):
            hh = grp * g + e
            lse_e = lse_ref[e]
            delta = jnp.sum(do_ref[e] * o_ref[e], axis=1, keepdims=True)
            s = s_all[e * t:(e + 1) * t] - slope_ref[hh] * dist
            p = jnp.where(valid, jnp.exp(s - lse_e), 0.0)
            ds = p * (dp_all[e * t:(e + 1) * t] - delta)
            ps.append(p.astype(BF16))
            dss.append((ds * scale).astype(BF16))
            ds_ref[e] += -jnp.sum(jnp.exp(sink_ref[hh] - lse_e) * delta)
        p_st = jnp.concatenate(ps, axis=0)
        ds_st = jnp.concatenate(dss, axis=0)
        dq = jnp.dot(ds_st, kc, preferred_element_type=F32)
        for e in range(g):
            dq_ref[e] = dq[e * t:(e + 1) * t]
        dkc = lax.dot_general(ds_st, qs, TN_DIMS, preferred_element_type=F32)
        dvc = lax.dot_general(p_st, dos, TN_DIMS, preferred_element_type=F32)
        jp = jnp.maximum(qi - 1, 0)
        for seg, off in enumerate((0, pl.multiple_of(jp * t, t), pl.multiple_of(qi * t, t))):
            dk_ref[0, pl.ds(off, t), :] += dkc[seg * t:(seg + 1) * t]
            dv_ref[0, pl.ds(off, t), :] += dvc[seg * t:(seg + 1) * t]

    tile = lambda a, b: (a, b, 0)
    whole = lambda a, b: (a, 0, 0)
    return pl.pallas_call(
        body, name=name, grid=(hkv, nq),
        in_specs=[BS((g, t, d), tile), BS((1, n, d), whole), BS((1, n, d), whole), BS((g, t, d), tile),
                  BS((g, t, 1), tile), BS((g, t, d), tile), BS(memory_space=pltpu.SMEM), BS(memory_space=pltpu.SMEM)],
        out_specs=[BS((g, t, d), tile), BS((1, n, d), whole), BS((1, n, d), whole), BS((g, 1, LANES), whole)],
        out_shape=[SDS((hq, n, d), F32), SDS((hkv, n, d), F32), SDS((hkv, n, d), F32), SDS((hq, 1, LANES), F32)],
        compiler_params=_cparams(("arbitrary", "arbitrary")))(q, k, v, o, lse, do, sinks, slopes)


def window_attention(q, k, v, sinks, slopes, scale, name):
    def run_fwd(q, k, v, sinks, slopes):
        o, lse = _swa_fwd_call(q, k, v, sinks, slopes, scale, name + "_f")
        return o, (q, k, v, sinks, slopes, o, lse)

    def run_bwd(res, do):
        q, k, v, sinks, slopes, o, lse = res
        dq, dk, dv, ds = _swa_bwd_call(q, k, v, o, lse, do, sinks, slopes, scale, name + "_b")
        return dq, dk, dv, ds[:, 0, 0], jnp.zeros_like(slopes)

    return _op(run_fwd, run_bwd)(q, k, v, sinks, slopes)


def _sigmoid(x):
    return 1.0 / (1.0 + jnp.exp(-x))


def _merge_fwd_call(gs, ys, name):
    n, c = ys[0].shape
    tr = _rows_tile(n, c, 1 << 20)

    def body(g0, g1, g2, y0, y1, y2, m_ref):
        m_ref[...] = (_sigmoid(g0[...]) * y0[...] + _sigmoid(g1[...]) * y1[...]) + _sigmoid(g2[...]) * y2[...]

    spec = BS((tr, c), lambda i: (i, 0))
    return pl.pallas_call(
        body, name=name, grid=(n // tr,), in_specs=[spec] * 6, out_specs=spec, out_shape=SDS((n, c), F32),
        compiler_params=_cparams(("parallel",)))(*gs, *ys)


def _merge_bwd_call(gs, ys, dm, name):
    n, c = ys[0].shape
    tr = _rows_tile(n, c, 1 << 20)

    def body(g0, g1, g2, y0, y1, y2, dm_ref, dg0, dg1, dg2, dy0, dy1, dy2):
        d = dm_ref[...]
        for g, y, dg, dy in ((g0, y0, dg0, dy0), (g1, y1, dg1, dy1), (g2, y2, dg2, dy2)):
            s = _sigmoid(g[...])
            dy[...] = d * s
            dg[...] = d * y[...] * (s * (1.0 - s))

    spec = BS((tr, c), lambda i: (i, 0))
    return pl.pallas_call(
        body, name=name, grid=(n // tr,), in_specs=[spec] * 7, out_specs=[spec] * 6,
        out_shape=[SDS((n, c), F32)] * 6, compiler_params=_cparams(("parallel",)))(*gs, *ys, dm)


def gated_merge(gs, ys, name):
    def fwd(gs, ys):
        return _merge_fwd_call(gs, ys, name + "_f"), (gs, ys)

    def bwd(res, dm):
        out = _merge_bwd_call(res[0], res[1], dm, name + "_b")
        return tuple(out[:3]), tuple(out[3:])

    return _op(fwd, bwd)(tuple(gs), tuple(ys))


CONV_TR = 264
CONV_TC = 1408


def _conv_tiles(n, f):
    tr = CONV_TR if n % CONV_TR == 0 else _div_tile(n, CONV_TR)
    tc = CONV_TC if f % CONV_TC == 0 else f
    return tr, tc


def _shift_down(cur, halo, first, tr):
    halo = jnp.where(first, 0.0, halo)
    row = lax.broadcasted_iota(jnp.int32, cur.shape, 0)
    h7, h6 = halo[7:8, :], halo[6:7, :]
    u1 = jnp.where(row == 0, h7, pltpu.roll(cur, 1, 0))
    u2 = jnp.where(row == 0, h6, jnp.where(row == 1, h7, pltpu.roll(cur, 2, 0)))
    return u1, u2


def _conv_lin(cur, u1, u2, w_ref, b_ref):
    return ((b_ref[...] + w_ref[0:1, :] * u2) + w_ref[1:2, :] * u1) + w_ref[2:3, :] * cur


def _conv_in_specs(tr, tc, nj):
    sub = tr // SUBLANES
    prev = lambda j, i: (jnp.maximum(i * sub - 1, 0), j)
    prev_v = lambda j, i: (jnp.maximum(i * sub - 1, 0), j + nj)
    return [BS((tr, tc), lambda j, i: (i, j)), BS((SUBLANES, tc), prev),
            BS((tr, tc), lambda j, i: (i, j + nj)), BS((SUBLANES, tc), prev_v),
            BS((3, tc), lambda j, i: (0, j)), BS((3, tc), lambda j, i: (0, j + nj)),
            BS((1, tc), lambda j, i: (0, j)), BS((1, tc), lambda j, i: (0, j + nj))]


def _conv_fwd_call(u, cw, cb, name):
    n, f2 = u.shape
    f = f2 // 2
    tr, tc = _conv_tiles(n, f)
    nj = f // tc

    def body(ug, ugh, uv, uvh, wg, wv, bg, bv, a_ref):
        first = pl.program_id(1) == 0
        g1, g2 = _shift_down(ug[...], ugh[...], first, tr)
        v1, v2 = _shift_down(uv[...], uvh[...], first, tr)
        cg = _conv_lin(ug[...], g1, g2, wg, bg)
        cv = _conv_lin(uv[...], v1, v2, wv, bv)
        a_ref[...] = cg * _sigmoid(cg) * cv

    return pl.pallas_call(
        body, name=name, grid=(nj, n // tr), in_specs=_conv_in_specs(tr, tc, nj),
        out_specs=BS((tr, tc), lambda j, i: (i, j)), out_shape=SDS((n, f), F32),
        compiler_params=_cparams(("parallel", "parallel")))(u, u, u, u, cw, cw, cb, cb)


def _conv_bwd_dc_call(u, cw, cb, da, name):
    n, f2 = u.shape
    f = f2 // 2
    tr, tc = _conv_tiles(n, f)
    nj = f // tc

    def body(ug, ugh, uv, uvh, wg, wv, bg, bv, da_ref, dc_ref, dw_ref, db_ref):
        first = pl.program_id(1) == 0
        g0, v0 = ug[...], uv[...]
        g1, g2 = _shift_down(g0, ugh[...], first, tr)
        v1, v2 = _shift_down(v0, uvh[...], first, tr)
        cg = _conv_lin(g0, g1, g2, wg, bg)
        cv = _conv_lin(v0, v1, v2, wv, bv)
        d = da_ref[...]
        s = _sigmoid(cg)
        dcg = d * cv * (s * (1.0 + cg * (1.0 - s)))
        dcv = d * (cg * s)
        dc_ref[0] = dcg
        dc_ref[1] = dcv

        @pl.when(first)
        def _():
            dw_ref[...] = jnp.zeros_like(dw_ref)
            db_ref[...] = jnp.zeros_like(db_ref)

        for p, dc, taps in ((0, dcg, (g2, g1, g0)), (1, dcv, (v2, v1, v0))):
            for t in range(3):
                dw_ref[p, t:t + 1, :] += jnp.sum(dc * taps[t], axis=0, keepdims=True)
            db_ref[p] += jnp.sum(dc, axis=0, keepdims=True)

    return pl.pallas_call(
        body, name=name, grid=(nj, n // tr),
        in_specs=_conv_in_specs(tr, tc, nj) + [BS((tr, tc), lambda j, i: (i, j))],
        out_specs=[BS((2, tr, tc), lambda j, i: (0, i, j)), BS((2, 3, tc), lambda j, i: (0, 0, j)),
                   BS((2, 1, tc), lambda j, i: (0, 0, j))],
        out_shape=[SDS((2, n, f), F32), SDS((2, 3, f), F32), SDS((2, 1, f), F32)],
        compiler_params=_cparams(("arbitrary", "arbitrary")))(u, u, u, u, cw, cw, cb, cb, da)


def _conv_bwd_du_call(dc, cw, name):
    _, n, f = dc.shape
    tr, tc = _conv_tiles(n, f)
    nj = f // tc
    ni = n // tr
    sub = tr // SUBLANES

    def body(c_ref, nx_ref, w_ref, du_ref):
        cur = c_ref[0]
        nxt = jnp.where(pl.program_id(2) == ni - 1, 0.0, nx_ref[0])
        row = lax.broadcasted_iota(jnp.int32, cur.shape, 0)
        n0, n1 = nxt[0:1, :], nxt[1:2, :]
        d1 = jnp.where(row == tr - 1, n0, pltpu.roll(cur, tr - 1, 0))
        d2 = jnp.where(row == tr - 1, n1, jnp.where(row == tr - 2, n0, pltpu.roll(cur, tr - 2, 0)))
        du_ref[...] = (w_ref[2:3, :] * cur + w_ref[1:2, :] * d1) + w_ref[0:1, :] * d2

    nxt_map = lambda p, j, i: (p, jnp.minimum((i + 1) * sub, n // SUBLANES - 1), j)
    return pl.pallas_call(
        body, name=name, grid=(2, nj, ni),
        in_specs=[BS((1, tr, tc), lambda p, j, i: (p, i, j)), BS((1, SUBLANES, tc), nxt_map),
                  BS((3, tc), lambda p, j, i: (0, p * nj + j))],
        out_specs=BS((tr, tc), lambda p, j, i: (i, p * nj + j)), out_shape=SDS((n, 2 * f), F32),
        compiler_params=_cparams(("parallel", "parallel", "parallel")))(dc, dc, cw)


def conv_glu(u, cw, cb, name):
    def fwd(u, cw, cb):
        return _conv_fwd_call(u, cw, cb, name + "_f"), (u, cw, cb)

    def bwd(res, da):
        u, cw, cb = res
        dc, dw, db = _conv_bwd_dc_call(u, cw, cb, da, name + "_bc")
        du = _conv_bwd_du_call(dc, cw, name + "_bu")
        return du, jnp.concatenate([dw[0], dw[1]], axis=-1), jnp.concatenate([db[0], db[1]], axis=-1)

    return _op(fwd, bwd)(u, cw, cb)


def _loss_call(y, t, n_real, name):
    n, c = y.shape
    tr = _rows_tile(n, c, 1 << 20)

    def body(y_ref, t_ref, dy_ref, l_ref):
        i = pl.program_id(0)
        row = i * tr + lax.broadcasted_iota(jnp.int32, (tr, c), 0)
        real = (row >= N_META) & (row < N_META + n_real)
        e = jnp.where(real, y_ref[...] - t_ref[...], 0.0)
        dy_ref[...] = e * (1.0 / c)

        @pl.when(i == 0)
        def _():
            l_ref[...] = jnp.zeros_like(l_ref)

        l_ref[...] += 0.5 * jnp.sum(jnp.sum(e * e, axis=-1, keepdims=True) * (1.0 / c), axis=0, keepdims=True)

    spec = BS((tr, c), lambda i: (i, 0))
    return pl.pallas_call(
        body, name=name, grid=(n // tr,), in_specs=[spec, spec],
        out_specs=[spec, BS((1, 1), lambda i: (0, 0))], out_shape=[SDS((n, c), F32), SDS((1, 1), F32)],
        compiler_params=_cparams(("arbitrary",)))(y, t)


def _to_heads(x, nh):
    n = x.shape[0]
    return x.reshape(n, nh, x.shape[1] // nh).transpose(1, 0, 2)


def _from_heads(x):
    h, n, d = x.shape
    return x.transpose(1, 0, 2).reshape(n, h * d)


def _head_norm(x, g, denom, name):
    h, n, d = x.shape
    return rms_norm(x.reshape(h * n, d), g, denom, name).reshape(h, n, d)


def _head_norm_rope(x, g, cos, sin, name):
    h, n, d = x.shape
    return rms_norm_rope(x.reshape(h * n, d), g, cos, sin, MLA_QK, name).reshape(h, n, d)


def _pad_in_cols(w):
    z = lambda k: jnp.zeros(w.shape[:-1] + (k,), w.dtype)
    return jnp.concatenate([w[..., :1544], z(120), w[..., 1544:1960], z(96), w[..., 1960:], z(128)], axis=-1)


def _pad_q_up(w):
    s = w.shape[:-1]
    w = w.reshape(s + (HEADS, MLA_QK))
    w = jnp.concatenate([w, jnp.zeros(s + (HEADS, LANES - MLA_QK), w.dtype)], axis=-1)
    return w.reshape(s + (HEADS * LANES,))


LAYERED = BIG + ("ffn_conv_w",)
BEFORE_MLA = ("w_in", "mla_w_q_up", "mla_w_kv_up")
AFTER_MLA = ("w_branch", "w_o", "ffn_w_up", "ffn_w_down")


def _assemble_layer(parts):
    out = {k: jnp.concatenate([v[i] for i in range(N_CHIPS)], axis=SHARD_AXIS[k] - 1) for k, v in parts.items()}
    if "w_in" in out:
        out["w_in"] = _pad_in_cols(out["w_in"])
    if "mla_w_q_up" in out:
        out["mla_w_q_up"] = _pad_q_up(out["mla_w_q_up"])
    return out


def _layer_parts(names, gathered):
    return {k: g.reshape((N_CHIPS,) + SHARD_SHAPE[k][1:]) for k, g in zip(names, gathered)}


def _rope_tables(n):
    half = MLA_ROPE // 2
    freqs = ROPE_THETA ** (-jnp.arange(half, dtype=F32) / half)
    ang = jnp.arange(n).astype(F32)[:, None] * freqs[None, :]
    cos, sin = jnp.cos(ang), jnp.sin(ang)
    one, zero = jnp.ones((n, MLA_NOPE), F32), jnp.zeros((n, MLA_NOPE), F32)
    tail1, tail0 = jnp.ones((n, LANES - MLA_QK), F32), jnp.zeros((n, LANES - MLA_QK), F32)
    return (jnp.concatenate([one, cos, cos, tail1], axis=1), jnp.concatenate([zero, sin, sin, tail0], axis=1))


def _pad_lanes(g, width):
    return jnp.concatenate([g, jnp.zeros((width - g.shape[0],), g.dtype)]).reshape(1, width)


PROJ_SEGMENTS = ((O_FQ, 512), (O_FK, 512), (O_FV, 512), (O_FF, HEADS), (O_CQ, MLA_Q_RANK), (O_CKV, MLA_KV_RANK),
                 (O_KR, MLA_ROPE), (O_SQ, 512), (O_SK, 128), (O_SV, 128), (O_G, D_MODEL), (O_G + D_MODEL, D_MODEL),
                 (O_G + 2 * D_MODEL, D_MODEL))


def _join_proj_call(parts, name):
    n = parts[0].shape[0]
    tr = _rows_tile(n, IN_PAD, 4 << 20)

    def body(*refs):
        o_ref = refs[-1]
        o_ref[...] = jnp.zeros_like(o_ref)
        for (s, w), r in zip(PROJ_SEGMENTS, refs[:-1]):
            o_ref[:, s:s + w] = r[...]

    return pl.pallas_call(
        body, name=name, grid=(n // tr,), in_specs=[BS((tr, w), lambda i: (i, 0)) for _, w in PROJ_SEGMENTS],
        out_specs=BS((tr, IN_PAD), lambda i: (i, 0)), out_shape=SDS((n, IN_PAD), F32),
        compiler_params=_cparams(("parallel",)))(*parts)


def _split_proj(proj, name):
    def fwd(x):
        return tuple(x[:, s:s + w] for s, w in PROJ_SEGMENTS), None

    def bwd(_, cts):
        return (_join_proj_call(cts, name + "_b"),)

    return _op(fwd, bwd)(proj)


def _trunk(eps, eps_cw, sinks, meta, small, x, w0, late):
    assert DEPTH == 2
    seq = x.shape[0]
    n = -(-(N_META + seq) // ROW_PAD) * ROW_PAD
    ew = _assemble_layer(eps)
    cos, sin = _rope_tables(n)
    slopes = jnp.exp2(-8.0 * jnp.arange(1, HEADS + 1, dtype=F32) / HEADS)
    h = jnp.concatenate([meta, x, jnp.zeros((n - N_META - seq, D_MODEL), F32)], axis=0)
    wb = w0
    for l in range(DEPTH):
        p = f"l{l}_"
        row = lambda name: small[name][l].reshape(1, -1)
        xn = rms_norm(h, row("norm1_g"), D_MODEL, p + "norm1")
        proj = linear(xn, wb["w_in"], ew["w_in"], p + "win")
        p_fq, p_fk, p_fv, p_ff, p_cq, p_ckv, p_kr, p_sq, p_sk, p_sv, g0, g1, g2 = _split_proj(proj, p + "split")
        fq = _head_norm(_to_heads(p_fq, HEADS), row("fox_q_g"), HEAD_DIM, p + "fqn")
        fk = _head_norm(_to_heads(p_fk, HEADS), row("fox_k_g"), HEAD_DIM, p + "fkn")
        fv = _to_heads(p_fv, HEADS)
        c = forget_cumsum(p_ff.T, small["fox_forget_b"][l].reshape(HEADS, 1), p + "fgate")
        if l == 0:
            rest = [a for k, a in zip(LAYERED, late) if k != "w_in"]
            out_a, got, carriers = causal_attention(fq, fk, fv, c, HEAD_DIM ** -0.5, p + "fox", rest, 0, BEFORE_MLA,
                                                    sinks[1])
            wb = dict(wb, **_assemble_layer(_layer_parts(LAYERED[1:], _fill_own(got, [a[0] for a in rest]))))
            ew1 = _assemble_layer(dict(zip(BEFORE_MLA, carriers)))
        else:
            out_a = causal_attention(fq, fk, fv, c, HEAD_DIM ** -0.5, p + "fox")
        cqn = rms_norm(p_cq, row("mla_q_a_g"), MLA_Q_RANK, p + "cqn")
        q = _to_heads(linear(cqn, wb["mla_w_q_up"], ew["mla_w_q_up"], p + "qup"), HEADS)
        q = _head_norm_rope(q, _pad_lanes(small["mla_q_g"][l], LANES), cos, sin, p + "mqn")
        ckvn = rms_norm(p_ckv, row("mla_kv_a_g"), MLA_KV_RANK, p + "ckvn")
        kv = _to_heads(linear(ckvn, wb["mla_w_kv_up"], ew["mla_w_kv_up"], p + "kvup"), HEADS)
        kr = jnp.broadcast_to(p_kr[None], (HEADS, n, MLA_ROPE))
        k = jnp.concatenate([kv[..., :MLA_NOPE], kr, jnp.zeros((HEADS, n, LANES - MLA_QK), F32)], axis=-1)
        k = _head_norm_rope(k, _pad_lanes(small["mla_k_g"][l], LANES), cos, sin, p + "mkn")
        if l == 0:
            out_b, got, carriers = causal_attention(q, k, kv[..., MLA_NOPE:], None, MLA_QK ** -0.5, p + "mla", late,
                                                    1, AFTER_MLA, sinks[0])
            w1 = _assemble_layer(_layer_parts(LAYERED, _fill_own(got, [a[1] for a in late])))
        else:
            out_b, carriers = causal_attention(q, k, kv[..., MLA_NOPE:], None, MLA_QK ** -0.5, p + "mla", None, 1,
                                               AFTER_MLA, sinks[2])
        ew = dict(ew, **_assemble_layer(dict(zip(AFTER_MLA, carriers))))
        sq = _head_norm(_to_heads(p_sq, HEADS), row("swa_q_g"), HEAD_DIM, p + "sqn")
        sk = _head_norm(_to_heads(p_sk, SWA_KV_HEADS), row("swa_k_g"), HEAD_DIM, p + "skn")
        sv = _to_heads(p_sv, SWA_KV_HEADS)
        out_c = window_attention(sq, sk, sv, small["swa_sinks"][l], slopes, HEAD_DIM ** -0.5, p + "swa")
        ys = [linear(_from_heads(o), wb["w_branch"][i], ew["w_branch"][i], p + f"br{i}")
              for i, o in enumerate((out_a, out_b, out_c))]
        merged = gated_merge([g0, g1, g2], ys, p + "merge")
        h = linear(merged, wb["w_o"], ew["w_o"], p + "wo", res=h)
        xn2 = rms_norm(h, row("norm2_g"), D_MODEL, p + "norm2")
        u = linear(xn2, wb["ffn_w_up"], ew["ffn_w_up"], p + "wup")
        act = conv_glu(u, lax.stop_gradient(wb["ffn_conv_w"]) + eps_cw[l], row("ffn_conv_b"), p + "conv")
        h = linear(act, wb["ffn_w_down"], ew["ffn_w_down"], p + "wdown", res=h)
        wb, ew = w1, ew1
    return h


def _local_step(x, target, w0, late, meta, small):
    seq = x.shape[0]
    eps = {k: jnp.zeros((N_CHIPS,) + SHARD_SHAPE[k][1:], F32) for k in BEFORE_MLA}
    eps_cw = jnp.zeros((DEPTH, 3, 2 * D_FF), F32)
    half = lambda k: jnp.zeros((HALVED[k][1] // 2, HALVED[k][2]), F32)
    sinks = tuple([(half(k), half(k)) for k in names] for names in (AFTER_MLA, BEFORE_MLA, AFTER_MLA))
    y, vjp = jax.vjp(lambda e, ec, sk, mt, s, xx: _trunk(e, ec, sk, mt, s, xx, w0, late),
                     eps, eps_cw, sinks, meta, small, x)
    n = y.shape[0]
    tpad = jnp.concatenate([jnp.zeros((N_META, D_MODEL), F32), target, jnp.zeros((n - N_META - seq, D_MODEL), F32)])
    dy, loss = _loss_call(y, tpad, seq, "loss")
    g_eps, g_cw, g_l1, g_meta, g_small, g_x = vjp(dy)
    return loss[0, 0], g_x, g_eps, g_l1, g_cw, g_meta, g_small


def _pack_rows(shapes, mult):
    total = sum(_size(s) for s in shapes)
    rows = -(-total // LANES)
    return -(-rows // mult) * mult


def _size(shape):
    n = 1
    for d in shape:
        n *= d
    return n


def _pack(arrs, rows, dtype):
    flat = [a.reshape(-1).astype(dtype) for a in arrs]
    used = sum(a.size for a in flat)
    flat.append(jnp.zeros((rows * LANES - used,), dtype))
    return jnp.concatenate(flat).reshape(rows, LANES)


def _unpack(p, shapes):
    flat = p.reshape(-1)
    out, off = [], 0
    for s in shapes:
        out.append(flat[off:off + _size(s)].reshape(s))
        off += _size(s)
    return out


MESH = pl.DeviceIdType.MESH
ANY = pl.BlockSpec(memory_space=pl.ANY)


def _me():
    return lax.axis_index("x"), lax.axis_index("y"), lax.axis_index("c")


def _remote(src, dst, send_sems, recv_sems, idx, dev):
    return pltpu.make_async_remote_copy(src_ref=src, dst_ref=dst, send_sem=send_sems.at[idx], recv_sem=recv_sems.at[idx],
                                        device_id=dev, device_id_type=MESH)


def _layer_gather(ins, outs, send_sems, recv_sems, layer, base):
    x, y, _ = _me()
    j = 2 * x + y
    sibling = (x, y, 1 - layer)
    chips = [(1 - x, y), (x, 1 - y), (1 - x, 1 - y)]

    def ici(p, r):
        cx, cy = chips[r]
        return _remote(ins[p].at[layer], outs[p].at[j], send_sems, recv_sems, base + 6 * p + r, (cx, cy, layer))

    def d2d(p, r):
        cx, cy = chips[r]
        blk = outs[p].at[2 * cx + cy]
        return _remote(blk, blk, send_sems, recv_sems, base + 6 * p + 3 + r, sibling)

    pairs = [(p, r) for p in range(len(ins)) for r in range(3)]

    def start():
        for p, r in pairs:
            ici(p, r).start()

    def forward():
        for p, r in pairs:
            ici(p, r).wait_recv()
            d2d(p, r).start()

    def drain():
        for p, r in pairs:
            ici(p, r).wait_send()
            d2d(p, r).wait_send()

    def receive():
        for p, r in pairs:
            d2d(p, r).wait_recv()

    return start, forward, drain, receive


def _fill_own(outs, own):
    j = 2 * lax.axis_index("x") + lax.axis_index("y")
    return [lax.dynamic_update_index_in_dim(o, a, j, 0) for o, a in zip(outs, own)]


def _gather_early(meta, arrs, name):
    npk = len(arrs)

    def body(*refs):
        m_in, ins = refs[0], refs[1:npk + 1]
        m_out, outs = refs[npk + 1], refs[npk + 2:2 * npk + 2]
        send_sems, recv_sems = refs[2 * npk + 2:]
        x, y, c = _me()
        j = 2 * x + y
        sibling = (x, y, 1 - c)
        chips = [(1 - x, y), (x, 1 - y), (1 - x, 1 - y)]
        start, forward, drain, receive = _layer_gather(ins, outs, send_sems, recv_sems, 0, 6)
        sends = []
        for r, (cx, cy) in enumerate(chips):
            cp = _remote(m_in.at[c], m_out.at[j, c], send_sems, recv_sems, r, (cx, cy, c))
            cp.start()
            sends.append(cp)
        pl.when(c == 0)(start)
        for r, (cx, cy) in enumerate(chips):
            blk = m_out.at[2 * cx + cy, c]
            _remote(blk, blk, send_sems, recv_sems, r, sibling).wait_recv()
            fw = _remote(blk, blk, send_sems, recv_sems, 3 + r, sibling)
            fw.start()
            sends.append(fw)
        for r, (cx, cy) in enumerate(chips):
            blk = m_out.at[2 * cx + cy, 1 - c]
            _remote(blk, blk, send_sems, recv_sems, 3 + r, sibling).wait_recv()
        for cp in sends:
            cp.wait_send()

        @pl.when(c == 0)
        def _():
            forward()
            drain()

        pl.when(c == 1)(receive)

    nsem = 6 + 6 * npk
    res = pl.pallas_call(
        body, name=name, in_specs=[ANY] * (npk + 1), out_specs=[ANY] * (npk + 1),
        out_shape=[SDS((N_CHIPS,) + meta.shape, meta.dtype)] + [SDS((N_CHIPS,) + a.shape[1:], a.dtype) for a in arrs],
        scratch_shapes=[pltpu.SemaphoreType.DMA((nsem,)), pltpu.SemaphoreType.DMA((nsem,))],
        compiler_params=pltpu.CompilerParams(has_side_effects=True))(meta, *arrs)
    return _fill_own(res[:1], [meta])[0], _fill_own(res[1:], [a[0] for a in arrs])


def _pair_exchange(gs, name):
    npk = len(gs)

    def body(*refs):
        ins, outs = refs[:npk], refs[npk:2 * npk]
        send_sems, recv_sems = refs[2 * npk:]
        x, y, c = _me()
        cps = [_remote(ins[p].at[:, 1 - c], outs[p], send_sems, recv_sems, p, (x, y, 1 - c)) for p in range(npk)]
        for cp in cps:
            cp.start()
        for cp in cps:
            cp.wait()

    return pl.pallas_call(
        body, name=name, in_specs=[ANY] * npk, out_specs=[ANY] * npk,
        out_shape=[SDS(g.shape[:1] + g.shape[2:], g.dtype) for g in gs],
        scratch_shapes=[pltpu.SemaphoreType.DMA((npk,)), pltpu.SemaphoreType.DMA((npk,))],
        compiler_params=pltpu.CompilerParams(has_side_effects=True))(*gs)


def _chip_exchange(ss, small, name):
    npk = len(ss)

    def body(*refs):
        ins, sm_ref = refs[:npk], refs[npk]
        outs, sa_ref = refs[npk + 1:2 * npk + 1], refs[2 * npk + 1]
        send_sems, recv_sems, loc_sem = refs[2 * npk + 2:]
        x, y, c = _me()
        me = 4 * x + 2 * y + c
        lc = pltpu.make_async_copy(sm_ref, sa_ref.at[me], loc_sem.at[0])
        lc.start()
        cps = []
        for p in range(npk):
            for r, (cx, cy) in enumerate([(1 - x, y), (x, 1 - y), (1 - x, 1 - y)]):
                cp = _remote(ins[p].at[2 * cx + cy], outs[p].at[r], send_sems, recv_sems, 3 * p + r, (cx, cy, c))
                cp.start()
                cps.append(cp)
        base = 3 * npk - 1
        for mask in range(1, N_DEV):
            px, py, pc = x ^ (mask >> 2), y ^ ((mask >> 1) & 1), c ^ (mask & 1)
            cp = _remote(sm_ref, sa_ref.at[me], send_sems, recv_sems, base + mask, (px, py, pc))
            cp.start()
            cps.append(cp)
        for p in range(npk):
            for r in range(3):
                _remote(outs[p].at[r], outs[p].at[r], send_sems, recv_sems, 3 * p + r, (x, y, c)).wait_recv()
        for mask in range(1, N_DEV):
            src = 4 * (x ^ (mask >> 2)) + 2 * (y ^ ((mask >> 1) & 1)) + (c ^ (mask & 1))
            _remote(sa_ref.at[src], sa_ref.at[src], send_sems, recv_sems, base + mask, (x, y, c)).wait_recv()
        for cp in cps:
            cp.wait_send()
        lc.wait()

    nsem = 3 * npk + N_DEV - 1
    res = pl.pallas_call(
        body, name=name, in_specs=[ANY] * (npk + 1), out_specs=[ANY] * (npk + 1),
        out_shape=[SDS((3,) + s.shape[1:], s.dtype) for s in ss] + [SDS((N_DEV,) + small.shape, small.dtype)],
        scratch_shapes=[pltpu.SemaphoreType.DMA((nsem,)), pltpu.SemaphoreType.DMA((nsem,)),
                        pltpu.SemaphoreType.DMA((1,))],
        compiler_params=pltpu.CompilerParams(has_side_effects=True))(*ss, small)
    return res[:npk], res[npk]


def _half_exchange(ghs, name):
    npk = len(ghs)

    def body(*refs):
        ins, outs = refs[:npk], refs[npk:2 * npk]
        send_sems, recv_sems = refs[2 * npk:]
        x, y, c = _me()
        cps = [_remote(ins[p], outs[p], send_sems, recv_sems, p, (x, y, 1 - c)) for p in range(npk)]
        for cp in cps:
            cp.start()
        for cp in cps:
            cp.wait()

    return pl.pallas_call(
        body, name=name, in_specs=[ANY] * npk, out_specs=[ANY] * npk, out_shape=[SDS(g.shape, g.dtype) for g in ghs],
        scratch_shapes=[pltpu.SemaphoreType.DMA((npk,)), pltpu.SemaphoreType.DMA((npk,))],
        compiler_params=pltpu.CompilerParams(has_side_effects=True))(*ghs)


def _add_tile(rows, cols):
    return _div_tile(rows, max(16, (1 << 19) // max(cols, LANES)), 16)


def _pair_add(g, r1, c_idx, name):
    _, rows, cols = r1.shape
    tr = _add_tile(rows, cols)

    def body(c_ref, g_ref, r_ref, o_ref, ob_ref):
        s = g_ref[0] + r_ref[...]
        o_ref[...] = s
        ob_ref[...] = s.astype(BF16)

    own = BS((1, tr, cols), lambda k, i, c: (k, i, 0))
    return pl.pallas_call(
        body, name=name,
        grid_spec=pltpu.PrefetchScalarGridSpec(
            num_scalar_prefetch=1, grid=(r1.shape[0], rows // tr),
            in_specs=[BS((1, 1, tr, cols), lambda k, i, c: (k, c[0], i, 0)), own], out_specs=[own, own]),
        out_shape=[SDS(r1.shape, F32), SDS(r1.shape, BF16)],
        compiler_params=_cparams(("parallel", "parallel")))(c_idx, g, r1)


def _chip_add(s1, r2, j_idx, name):
    _, rows, cols = s1.shape
    tr = _add_tile(rows, cols)

    def body(j_ref, s_ref, r_ref, o_ref):
        o_ref[...] = ((s_ref[0] + r_ref[0].astype(F32)) + r_ref[1].astype(F32)) + r_ref[2].astype(F32)

    return pl.pallas_call(
        body, name=name,
        grid_spec=pltpu.PrefetchScalarGridSpec(
            num_scalar_prefetch=1, grid=(rows // tr,),
            in_specs=[BS((1, tr, cols), lambda i, j: (j[0], i, 0)), BS((3, tr, cols), lambda i, j: (0, i, 0))],
            out_specs=BS((tr, cols), lambda i, j: (i, 0))),
        out_shape=SDS((rows, cols), F32), compiler_params=_cparams(("parallel",)))(j_idx, s1, r2)


def _adamw_math(w, g, m, v):
    m = ADAM_B1 * m + (1.0 - ADAM_B1) * g
    v = ADAM_B2 * v + (1.0 - ADAM_B2) * (g * g)
    m_hat = m / (1.0 - ADAM_B1 ** ADAM_STEP)
    v_hat = v / (1.0 - ADAM_B2 ** ADAM_STEP)
    delta = -ADAM_LR * (m_hat / (jnp.sqrt(v_hat) + ADAM_EPS) + ADAM_WD * w)
    return delta, m, v


def _adamw(w, gh, go, m, v, c_idx, name):
    _, rows, cols = w.shape
    tr = _add_tile(rows, cols)

    def body(c_ref, w_ref, gh_ref, go_ref, m_ref, v_ref, g_out, d_out, m_out, v_out):
        g = jnp.where(pl.program_id(0) == c_ref[0], gh_ref[...], go_ref[...])
        g_out[0] = g
        d_out[0], m_out[0], v_out[0] = _adamw_math(w_ref[0], g, m_ref[0], v_ref[0])

    full = BS((1, tr, cols), lambda hf, i, c: (hf, i, 0))
    half = BS((tr, cols), lambda hf, i, c: (i, 0))
    return pl.pallas_call(
        body, name=name,
        grid_spec=pltpu.PrefetchScalarGridSpec(
            num_scalar_prefetch=1, grid=(2, rows // tr), in_specs=[full, half, half, full, full],
            out_specs=[full] * 4),
        out_shape=[SDS(w.shape, F32)] * 4, compiler_params=_cparams(("parallel", "parallel")))(c_idx, w, gh, go, m, v)


def _adamw_layers(w, g0, g1, m, v, c_idx, name):
    _, _, rows, cols = w.shape
    tr = _add_tile(rows, cols)

    def body(c_ref, w_ref, h0_ref, o0_ref, h1_ref, o1_ref, m_ref, v_ref, g_out, d_out, m_out, v_out):
        mine = pl.program_id(1) == c_ref[0]
        g = jnp.where(pl.program_id(0) == 0, jnp.where(mine, h0_ref[...], o0_ref[...]),
                      jnp.where(mine, h1_ref[...], o1_ref[...]))
        g_out[0, 0] = g
        d_out[0, 0], m_out[0, 0], v_out[0, 0] = _adamw_math(w_ref[0, 0], g, m_ref[0, 0], v_ref[0, 0])

    full = BS((1, 1, tr, cols), lambda l, hf, i, c: (l, hf, i, 0))
    half = BS((tr, cols), lambda l, hf, i, c: (i, 0))
    return pl.pallas_call(
        body, name=name,
        grid_spec=pltpu.PrefetchScalarGridSpec(
            num_scalar_prefetch=1, grid=(2, 2, rows // tr), in_specs=[full, half, half, half, half, full, full],
            out_specs=[full] * 4),
        out_shape=[SDS(w.shape, F32)] * 4,
        compiler_params=_cparams(("parallel", "parallel", "parallel")))(c_idx, w, *g0, *g1, m, v)


def _sum_devices(sa, name):
    def body(sa_ref, g_out):
        g = sa_ref[0]
        for d in range(1, N_DEV):
            g = g + sa_ref[d]
        g_out[...] = g

    return pl.pallas_call(body, name=name, out_shape=SDS(sa.shape[1:], F32),
                          compiler_params=pltpu.CompilerParams(vmem_limit_bytes=VMEM_LIMIT))(sa)


def _adamw_small(ws, gs, ms, vs, name):
    k = len(ws)

    def body(*refs):
        ins, outs = refs[:4 * k], refs[4 * k:]
        for i in range(k):
            d, m, v = _adamw_math(ins[i][...], ins[k + i][...], ins[2 * k + i][...], ins[3 * k + i][...])
            outs[i][...], outs[k + i][...], outs[2 * k + i][...] = d, m, v

    return pl.pallas_call(body, name=name, out_shape=[SDS(w.shape, F32) for w in ws] * 3,
                          compiler_params=pltpu.CompilerParams(vmem_limit_bytes=VMEM_LIMIT))(*ws, *gs, *ms, *vs)


HALVED = {"w_in": (2, 1024, 1450), "mla_w_q_up": (2, 256, 192), "mla_w_kv_up": (2, 128, 256),
          "w_branch": (2, 1536, 256), "w_o": (2, 256, 1024), "ffn_w_up": (2, 1024, 1408),
          "ffn_w_down": (2, 704, 1024), "ffn_conv_w": (2, 3, 1408), "meta_tokens": (2, 8, 256)}
SMALL_SHAPE = {"norm1_g": (2, 1024), "fox_forget_b": (2, 8), "fox_q_g": (2, 64), "fox_k_g": (2, 64),
               "mla_q_a_g": (2, 256), "mla_kv_a_g": (2, 128), "mla_q_g": (2, 96), "mla_k_g": (2, 96),
               "swa_q_g": (2, 64), "swa_k_g": (2, 64), "swa_sinks": (2, 8), "norm2_g": (2, 1024),
               "ffn_conv_b": (2, 5632)}
SMALL_ROWS = _pack_rows([SMALL_SHAPE[k] for k in SMALL] + [(1,)], SUBLANES)


def kernel(x, meta_tokens, norm1_g, w_in, fox_forget_b, fox_q_g, fox_k_g, mla_q_a_g, mla_w_q_up, mla_kv_a_g, mla_w_kv_up, mla_q_g, mla_k_g, swa_q_g, swa_k_g, swa_sinks, w_branch, w_o, norm2_g, ffn_w_up, ffn_conv_w, ffn_conv_b, ffn_w_down, loss_target, m_meta_tokens, m_norm1_g, m_w_in, m_fox_forget_b, m_fox_q_g, m_fox_k_g, m_mla_q_a_g, m_mla_w_q_up, m_mla_kv_a_g, m_mla_w_kv_up, m_mla_q_g, m_mla_k_g, m_swa_q_g, m_swa_k_g, m_swa_sinks, m_w_branch, m_w_o, m_norm2_g, m_ffn_w_up, m_ffn_conv_w, m_ffn_conv_b, m_ffn_w_down, v_meta_tokens, v_norm1_g, v_w_in, v_fox_forget_b, v_fox_q_g, v_fox_k_g, v_mla_q_a_g, v_mla_w_q_up, v_mla_kv_a_g, v_mla_w_kv_up, v_mla_q_g, v_mla_k_g, v_swa_q_g, v_swa_k_g, v_swa_sinks, v_w_branch, v_w_o, v_norm2_g, v_ffn_w_up, v_ffn_conv_w, v_ffn_conv_b, v_ffn_w_down):
    w = dict(meta_tokens=meta_tokens, norm1_g=norm1_g, w_in=w_in, fox_forget_b=fox_forget_b, fox_q_g=fox_q_g,
             fox_k_g=fox_k_g, mla_q_a_g=mla_q_a_g, mla_w_q_up=mla_w_q_up, mla_kv_a_g=mla_kv_a_g,
             mla_w_kv_up=mla_w_kv_up, mla_q_g=mla_q_g, mla_k_g=mla_k_g, swa_q_g=swa_q_g, swa_k_g=swa_k_g,
             swa_sinks=swa_sinks, w_branch=w_branch, w_o=w_o, norm2_g=norm2_g, ffn_w_up=ffn_w_up,
             ffn_conv_w=ffn_conv_w, ffn_conv_b=ffn_conv_b, ffn_w_down=ffn_w_down)
    m = dict(meta_tokens=m_meta_tokens, norm1_g=m_norm1_g, w_in=m_w_in, fox_forget_b=m_fox_forget_b,
             fox_q_g=m_fox_q_g, fox_k_g=m_fox_k_g, mla_q_a_g=m_mla_q_a_g, mla_w_q_up=m_mla_w_q_up,
             mla_kv_a_g=m_mla_kv_a_g, mla_w_kv_up=m_mla_w_kv_up, mla_q_g=m_mla_q_g, mla_k_g=m_mla_k_g,
             swa_q_g=m_swa_q_g, swa_k_g=m_swa_k_g, swa_sinks=m_swa_sinks, w_branch=m_w_branch, w_o=m_w_o,
             norm2_g=m_norm2_g, ffn_w_up=m_ffn_w_up, ffn_conv_w=m_ffn_conv_w, ffn_conv_b=m_ffn_conv_b,
             ffn_w_down=m_ffn_w_down)
    v = dict(meta_tokens=v_meta_tokens, norm1_g=v_norm1_g, w_in=v_w_in, fox_forget_b=v_fox_forget_b,
             fox_q_g=v_fox_q_g, fox_k_g=v_fox_k_g, mla_q_a_g=v_mla_q_a_g, mla_w_q_up=v_mla_w_q_up,
             mla_kv_a_g=v_mla_kv_a_g, mla_w_kv_up=v_mla_w_kv_up, mla_q_g=v_mla_q_g, mla_k_g=v_mla_k_g,
             swa_q_g=v_swa_q_g, swa_k_g=v_swa_k_g, swa_sinks=v_swa_sinks, w_branch=v_w_branch, w_o=v_w_o,
             norm2_g=v_norm2_g, ffn_w_up=v_ffn_w_up, ffn_conv_w=v_ffn_conv_w, ffn_conv_b=v_ffn_conv_b,
             ffn_w_down=v_ffn_w_down)
    xi, yi, ci = _me()
    c_idx = ci.astype(jnp.int32).reshape(1)
    j_idx = (2 * xi + yi).astype(jnp.int32).reshape(1)

    sh_names = BIG + FINE
    local = {k: (w[k].astype(BF16) if k in BIG else w[k]).reshape(HALVED[k]) for k in sh_names}
    late = [local[k] for k in LAYERED]
    meta_g, early = _gather_early(local["meta_tokens"], [local["w_in"]], "gather_early")
    meta = jnp.concatenate([meta_g[i].reshape(SHARD_SHAPE["meta_tokens"]) for i in range(N_CHIPS)], axis=1)
    w0 = _assemble_layer(_layer_parts(("w_in",), early))
    small = {k: w[k] for k in SMALL}

    loss, g_x, g_eps, g_red, g_cw, g_meta, g_small = _local_step(x[0], loss_target[0], w0, late, meta, small)
    g_after0 = dict(zip(AFTER_MLA, g_red[0]))
    g_l1 = dict(zip(BEFORE_MLA + AFTER_MLA, g_red[1] + g_red[2]))

    quarter = {k: (2, HALVED[k][1] // 2, HALVED[k][2]) for k in BIG}
    last = BEFORE_MLA + FINE
    gs = [g_eps[k].reshape((N_CHIPS,) + quarter[k]) for k in BEFORE_MLA]
    for k, g in (("meta_tokens", g_meta), ("ffn_conv_w", g_cw)):
        gs.append(jnp.stack(jnp.split(g, N_CHIPS, axis=SHARD_AXIS[k])).reshape((N_CHIPS,) + HALVED[k]))
    spack = _pack([g_small[k] for k in SMALL] + [loss.reshape(1)], SMALL_ROWS, F32)
    r1 = _pair_exchange(gs, "grads_pair_exchange")
    s1 = [_pair_add(g, r, c_idx, "grads_pair_add_" + k) for g, r, k in zip(gs, r1, last)]
    r2, sa = _chip_exchange([s[1] for s in s1], spack, "grads_chip_exchange")
    gh = dict(zip(last, [_chip_add(s[0], r, j_idx, "grads_chip_add_" + k) for s, r, k in zip(s1, r2, last)]))
    go = dict(zip(last, _half_exchange([gh[k] for k in last], "grads_half_exchange")))

    grads, deltas, new_m, new_v = {}, {}, {}, {}
    for k in sh_names:
        if k in BIG:
            shp = (2,) + quarter[k]
            wk, mk, vk = w[k].reshape(shp), m[k].reshape(shp), v[k].reshape(shp)
            g0 = (gh[k], go[k]) if k in BEFORE_MLA else g_after0[k]
            outs = _adamw_layers(wk, g0, g_l1[k], mk, vk, c_idx, "adamw_" + k)
        else:
            outs = _adamw(w[k].reshape(HALVED[k]), gh[k], go[k], m[k].reshape(HALVED[k]), v[k].reshape(HALVED[k]),
                          c_idx, "adamw_" + k)
        for dst, o in zip((grads, deltas, new_m, new_v), outs):
            dst[k] = o.reshape(SHARD_SHAPE[k])
    sm_shapes = [SMALL_SHAPE[k] for k in SMALL] + [(1,)]
    g_sum = _unpack(_sum_devices(sa, "sum_small"), sm_shapes)
    res = _adamw_small([w[k] for k in SMALL], g_sum[:-1], [m[k] for k in SMALL], [v[k] for k in SMALL], "adamw_small")
    ns = len(SMALL)
    grads.update(zip(SMALL, g_sum[:-1]))
    for dst, vals in zip((deltas, new_m, new_v), (res[:ns], res[ns:2 * ns], res[2 * ns:])):
        dst.update(zip(SMALL, vals))
    total_loss = g_sum[-1][0]
    return (total_loss, g_x[None], *[grads[k] for k in WEIGHTS], *[deltas[k] for k in WEIGHTS],
            *[new_m[k] for k in WEIGHTS], *[new_v[k] for k in WEIGHTS])
```

```python
import math

import jax
import jax.numpy as jnp
from jax import lax
from jax.experimental import pallas as pl
from jax.experimental.pallas import tpu as pltpu

F32 = jnp.float32
BF16 = jnp.bfloat16
SDS = jax.ShapeDtypeStruct
BS = pl.BlockSpec

D_MODEL = 1024
DEPTH = 2
N_META = 16
EPS = 1e-6
HEADS = 8
HEAD_DIM = 64
MLA_Q_RANK = 256
MLA_KV_RANK = 128
MLA_NOPE = 64
MLA_ROPE = 32
MLA_QK = MLA_NOPE + MLA_ROPE
ROPE_THETA = 10000.0
SWA_KV_HEADS = 2
WINDOW = 128
D_FF = 2816
IN_PAD = 6144
N_CHIPS = 4
N_DEV = 8

ADAM_LR = 0.001
ADAM_B1 = 0.9
ADAM_B2 = 0.999
ADAM_EPS = 1e-08
ADAM_WD = 0.01
ADAM_STEP = 10

LANES = 128
SUBLANES = 8
ROW_PAD = 128
CAUSAL_TILE = 384
NEG = -1e30
VMEM_LIMIT = 56 * 1024 * 1024

O_FQ, O_FK, O_FV, O_FF = 0, 512, 1024, 1536
O_CQ, O_CKV, O_KR = 1664, 1920, 2048
O_SQ, O_SK, O_SV, O_G = 2176, 2688, 2816, 2944

SHARD_AXIS = {"meta_tokens": 1, "w_in": 2, "mla_w_q_up": 2, "mla_w_kv_up": 2, "w_branch": 3, "w_o": 1,
              "ffn_w_up": 2, "ffn_conv_w": 2, "ffn_w_down": 1}
SHARD_SHAPE = {"meta_tokens": (16, 256), "w_in": (2, 1024, 1450), "mla_w_q_up": (2, 256, 192),
               "mla_w_kv_up": (2, 128, 256), "w_branch": (2, 3, 512, 256), "w_o": (2, 256, 1024),
               "ffn_w_up": (2, 1024, 1408), "ffn_conv_w": (2, 3, 1408), "ffn_w_down": (2, 704, 1024)}
BIG = ("w_in", "mla_w_q_up", "mla_w_kv_up", "w_branch", "w_o", "ffn_w_up", "ffn_w_down")
FINE = ("meta_tokens", "ffn_conv_w")
SMALL = ("norm1_g", "fox_forget_b", "fox_q_g", "fox_k_g", "mla_q_a_g", "mla_kv_a_g", "mla_q_g", "mla_k_g",
         "swa_q_g", "swa_k_g", "swa_sinks", "norm2_g", "ffn_conv_b")
WEIGHTS = ("meta_tokens", "norm1_g", "w_in", "fox_forget_b", "fox_q_g", "fox_k_g", "mla_q_a_g", "mla_w_q_up",
           "mla_kv_a_g", "mla_w_kv_up", "mla_q_g", "mla_k_g", "swa_q_g", "swa_k_g", "swa_sinks", "w_branch", "w_o",
           "norm2_g", "ffn_w_up", "ffn_conv_w", "ffn_conv_b", "ffn_w_down")


def _cparams(sem):
    return pltpu.CompilerParams(dimension_semantics=sem, vmem_limit_bytes=VMEM_LIMIT)


def _div_tile(n, cap, mult=SUBLANES):
    best = None
    for t in range(mult, min(n, cap) + 1, mult):
        if n % t == 0:
            best = t
    return best if best is not None else n


def _rows_tile(n, width, budget=2 << 20):
    return _div_tile(n, max(SUBLANES, budget // (4 * max(width, LANES))))


def _op(fwd, bwd):
    @jax.custom_vjp
    def op(*args):
        return fwd(*args)[0]
    op.defvjp(fwd, bwd)
    return op


def _rotate(v, cos, sin):
    lane = lax.broadcasted_iota(jnp.int32, v.shape, 1)
    rot = jnp.where(lane < MLA_NOPE + MLA_ROPE // 2, -pltpu.roll(v, LANES - MLA_ROPE // 2, 1),
                    pltpu.roll(v, MLA_ROPE // 2, 1))
    return v * cos + rot * sin


def _rms_fwd_call(x, g, denom, name, rot=None):
    n, c = x.shape
    tr = _rows_tile(n if rot is None else rot[0].shape[0], c)
    nt = None if rot is None else rot[0].shape[0] // tr

    def body(x_ref, g_ref, *rest):
        y_ref = rest[-1]
        xv = x_ref[...]
        ms = jnp.sum(xv * xv, axis=-1, keepdims=True) * (1.0 / denom)
        y = xv * lax.rsqrt(ms + EPS) * g_ref[...]
        y_ref[...] = y if rot is None else _rotate(y, rest[0][...], rest[1][...])

    ins, args = [BS((tr, c), lambda i: (i, 0)), BS((1, c), lambda i: (0, 0))], [x, g]
    if rot is not None:
        ins += [BS((tr, c), lambda i: (i % nt, 0))] * 2
        args += list(rot)
    return pl.pallas_call(
        body, name=name, grid=(n // tr,), in_specs=ins,
        out_specs=BS((tr, c), lambda i: (i, 0)), out_shape=SDS((n, c), F32),
        compiler_params=_cparams(("parallel",)))(*args)


def _rms_bwd_call(x, g, dy, denom, name, rot=None):
    n, c = x.shape
    tr = _rows_tile(n if rot is None else rot[0].shape[0], c)
    nt = None if rot is None else rot[0].shape[0] // tr

    def body(x_ref, g_ref, dy_ref, *rest):
        dx_ref, dg_ref = rest[-2:]
        xv = x_ref[...]
        dy = dy_ref[...]
        if rot is not None:
            dy = _rotate(dy, rest[0][...], -rest[1][...])
        ms = jnp.sum(xv * xv, axis=-1, keepdims=True) * (1.0 / denom)
        r = lax.rsqrt(ms + EPS)
        xh = xv * r
        dxh = dy * g_ref[...]
        dx_ref[...] = r * (dxh - xh * (jnp.sum(dxh * xh, axis=-1, keepdims=True) * (1.0 / denom)))

        @pl.when(pl.program_id(0) == 0)
        def _():
            dg_ref[...] = jnp.zeros_like(dg_ref)

        dg_ref[...] += jnp.sum(dy * xh, axis=0, keepdims=True)

    ins = [BS((tr, c), lambda i: (i, 0)), BS((1, c), lambda i: (0, 0)), BS((tr, c), lambda i: (i, 0))]
    args = [x, g, dy]
    if rot is not None:
        ins += [BS((tr, c), lambda i: (i % nt, 0))] * 2
        args += list(rot)
    return pl.pallas_call(
        body, name=name, grid=(n // tr,), in_specs=ins,
        out_specs=[BS((tr, c), lambda i: (i, 0)), BS((1, c), lambda i: (0, 0))],
        out_shape=[SDS((n, c), F32), SDS((1, c), F32)],
        compiler_params=_cparams(("arbitrary",)))(*args)


def rms_norm(x, g, denom, name):
    def fwd(x, g):
        return _rms_fwd_call(x, g, denom, name + "_f"), (x, g)

    def bwd(res, dy):
        return tuple(_rms_bwd_call(res[0], res[1], dy, denom, name + "_b"))

    return _op(fwd, bwd)(x, g)


def rms_norm_rope(x, g, cos, sin, denom, name):
    def fwd(x, g, cos, sin):
        return _rms_fwd_call(x, g, denom, name + "_f", (cos, sin)), (x, g, cos, sin)

    def bwd(res, dy):
        x, g, cos, sin = res
        dx, dg = _rms_bwd_call(x, g, dy, denom, name + "_b", (cos, sin))
        return dx, dg, jnp.zeros_like(cos), jnp.zeros_like(sin)

    return _op(fwd, bwd)(x, g, cos, sin)


def _mm_call(a, b, mode, res, name):
    if mode == "nn":
        (m, kc), n = a.shape, b.shape[1]
    elif mode == "nt":
        (m, kc), n = a.shape, b.shape[0]
    else:
        (kc, m), n = a.shape, b.shape[1]
    if mode == "tn":
        tk = _div_tile(kc, 528)
        tm = _div_tile(m, 1408, LANES)
        tn = _div_tile(n, 2048, LANES)
    else:
        tk = kc if kc <= 2816 else _div_tile(kc, 1024, LANES)
        tm = _div_tile(m, max(LANES, (9 << 19) // (4 * tk)))
        tn = _div_tile(n, 1408 if mode == "nt" else 512, LANES)
    nk = kc // tk
    dims = {"nn": (((1,), (0,)), ((), ())), "nt": (((1,), (1,)), ((), ())), "tn": (((0,), (0,)), ((), ()))}[mode]

    def body(*refs):
        if res is None:
            a_ref, b_ref, o_ref, acc_ref = refs
            r_ref = None
        else:
            a_ref, b_ref, r_ref, o_ref, acc_ref = refs
        k = pl.program_id(2)

        @pl.when(k == 0)
        def _():
            acc_ref[...] = jnp.zeros_like(acc_ref)

        acc_ref[...] += lax.dot_general(a_ref[...].astype(BF16), b_ref[...].astype(BF16), dims,
                                        preferred_element_type=F32)

        @pl.when(k == nk - 1)
        def _():
            if r_ref is None:
                o_ref[...] = acc_ref[...]
            else:
                o_ref[...] = r_ref[...] + acc_ref[...]

    a_spec = BS((tk, tm), lambda i, j, k: (k, i)) if mode == "tn" else BS((tm, tk), lambda i, j, k: (i, k))
    b_spec = BS((tn, tk), lambda i, j, k: (j, k)) if mode == "nt" else BS((tk, tn), lambda i, j, k: (k, j))
    o_spec = BS((tm, tn), lambda i, j, k: (i, j))
    ins, args = [a_spec, b_spec], [a, b]
    if res is not None:
        ins.append(o_spec)
        args.append(res)
    return pl.pallas_call(
        body, name=name, grid=(m // tm, n // tn, nk), in_specs=ins, out_specs=o_spec,
        out_shape=SDS((m, n), F32), scratch_shapes=[pltpu.VMEM((tm, tn), F32)],
        compiler_params=_cparams(("parallel", "parallel", "arbitrary")))(*args)


def linear(a, w, eps, name, res=None):
    if res is None:
        def fwd(a, w, eps):
            return _mm_call(a, w, "nn", None, name + "_f"), (a, w)

        def bwd(r, dc):
            a, w = r
            return (_mm_call(dc, w, "nt", None, name + "_da"), jnp.zeros_like(w),
                    _mm_call(a, dc, "tn", None, name + "_dw"))

        return _op(fwd, bwd)(a, w, eps)

    def fwd_r(a, w, eps, res):
        return _mm_call(a, w, "nn", res, name + "_f"), (a, w)

    def bwd_r(r, dc):
        a, w = r
        return (_mm_call(dc, w, "nt", None, name + "_da"), jnp.zeros_like(w),
                _mm_call(a, dc, "tn", None, name + "_dw"), dc)

    return _op(fwd_r, bwd_r)(a, w, eps, res)


def _gate_tile(n):
    return _div_tile(n, CAUSAL_TILE, LANES)


def _tri_dot(v, upper):
    ct = v.shape[1]
    r = lax.broadcasted_iota(jnp.int32, (ct, ct), 0)
    c = lax.broadcasted_iota(jnp.int32, (ct, ct), 1)
    tri = jnp.where((r <= c) if upper else (r >= c), 1.0, 0.0).astype(F32)
    return jnp.dot(v, tri, preferred_element_type=F32, precision=lax.Precision.HIGHEST)


def _gate_fwd_call(z, b, name):
    h, n = z.shape
    CT = _gate_tile(n)

    def body(z_ref, b_ref, c_ref, carry):
        @pl.when(pl.program_id(0) == 0)
        def _():
            carry[...] = jnp.zeros_like(carry)

        x = z_ref[...] + b_ref[...]
        ls = jnp.minimum(x, 0.0) - jnp.log(1.0 + jnp.exp(-jnp.abs(x)))
        c_ref[...] = _tri_dot(ls, True) + carry[...]
        carry[...] += jnp.sum(ls, axis=1, keepdims=True)

    return pl.pallas_call(
        body, name=name, grid=(n // CT,),
        in_specs=[BS((h, CT), lambda j: (0, j)), BS((h, 1), lambda j: (0, 0))],
        out_specs=BS((h, CT), lambda j: (0, j)), out_shape=SDS((h, n), F32),
        scratch_shapes=[pltpu.VMEM((h, 1), F32)],
        compiler_params=_cparams(("arbitrary",)))(z, b)


def _gate_bwd_call(z, b, dc, name):
    h, n = z.shape
    CT = _gate_tile(n)
    nt = n // CT

    def body(z_ref, b_ref, dc_ref, dz_ref, db_ref, carry):
        @pl.when(pl.program_id(0) == 0)
        def _():
            carry[...] = jnp.zeros_like(carry)
            db_ref[...] = jnp.zeros_like(db_ref)

        dcv = dc_ref[...]
        dls = _tri_dot(dcv, False) + carry[...]
        carry[...] += jnp.sum(dcv, axis=1, keepdims=True)
        x = z_ref[...] + b_ref[...]
        e = jnp.exp(-jnp.abs(x))
        dz = dls * jnp.where(x >= 0, e / (1.0 + e), 1.0 / (1.0 + e))
        dz_ref[...] = dz
        db_ref[...] += jnp.sum(dz, axis=1, keepdims=True)

    rev = lambda j: (0, nt - 1 - j)
    return pl.pallas_call(
        body, name=name, grid=(nt,),
        in_specs=[BS((h, CT), rev), BS((h, 1), lambda j: (0, 0)), BS((h, CT), rev)],
        out_specs=[BS((h, CT), rev), BS((h, 1), lambda j: (0, 0))],
        out_shape=[SDS((h, n), F32), SDS((h, 1), F32)],
        scratch_shapes=[pltpu.VMEM((h, 1), F32)],
        compiler_params=_cparams(("arbitrary",)))(z, b, dc)


def forget_cumsum(z, b, name):
    def fwd(z, b):
        return _gate_fwd_call(z, b, name + "_f"), (z, b)

    def bwd(res, dc):
        return tuple(_gate_bwd_call(res[0], res[1], dc, name + "_b"))

    return _op(fwd, bwd)(z, b)


NT_DIMS = (((1,), (1,)), ((), ()))
TN_DIMS = (((0,), (0,)), ((), ()))
HEADS_PER_STEP = 2
FWD_HEADS_PER_STEP = 4


def _split_scale(scale):
    mant, _ = math.frexp(scale)
    return (scale, 1.0) if mant == 0.5 else (1.0, scale)


def _causal_tile(n):
    return CAUSAL_TILE if n % CAUSAL_TILE == 0 else ROW_PAD


def _causal_fwd_call(q, k, v, ck_r, fox, scale, name, late=None, late_layer=1):
    h, n, dk = q.shape
    dv = v.shape[2]
    t = _causal_tile(n)
    nq = n // t
    hb = FWD_HEADS_PER_STEP
    pre, post = _split_scale(scale)
    nl = 0 if late is None else len(late)
    n_in = 3 + int(fox) + nl

    def body(*refs):
        q_ref, k_ref, v_ref = refs[:3]
        ck_ref = refs[3] if fox else None
        o_ref, lse_ref = refs[n_in:n_in + 2]
        m_scr, l_scr, acc_scr = refs[n_in + 2 + nl:n_in + 5 + nl]
        qi = pl.program_id(1)
        if nl:
            start, forward, drain, receive = _layer_gather(refs[n_in - nl:n_in], refs[n_in + 2:n_in + 2 + nl],
                                                           refs[-2], refs[-1], late_layer, 0)
            hp, core = pl.program_id(0), lax.axis_index("c")
            last = (hp == h // hb - 1) & (qi == nq - 1)
            pl.when((hp == 0) & (qi == 0) & (core == late_layer))(start)
            pl.when((hp == h // hb // 2) & (qi == 0) & (core == late_layer))(forward)
            pl.when(last & (core == late_layer))(drain)
            pl.when(last & (core == 1 - late_layer))(receive)
        qbs = [(q_ref[e] * pre).astype(BF16) for e in range(hb)]
        m_scr[...] = jnp.full(m_scr.shape, NEG, F32)
        l_scr[...] = jnp.zeros_like(l_scr)
        acc_scr[...] = jnp.zeros_like(acc_scr)

        def process(j, masked):
            off = pl.multiple_of(j * t, t)
            if masked:
                rows = lax.broadcasted_iota(jnp.int32, (t, t), 0)
                cols = lax.broadcasted_iota(jnp.int32, (t, t), 1)
                valid = cols <= rows
            for e in range(hb):
                kb = k_ref[e, pl.ds(off, t), :].astype(BF16)
                vb = v_ref[e, pl.ds(off, t), :].astype(BF16)
                s = lax.dot_general(qbs[e], kb, NT_DIMS, preferred_element_type=F32)
                if post != 1.0:
                    s = s * post
                if fox:
                    s = s - ck_ref[e, j]
                if masked:
                    s = jnp.where(valid, s, NEG)
                m_old = m_scr[e]
                m_new = jnp.maximum(m_old, jnp.max(s, axis=1, keepdims=True))
                alpha = jnp.exp(m_old - m_new)
                p = jnp.exp(s - jnp.tile(m_new, (1, t // LANES)))
                l_scr[e] = alpha * l_scr[e] + jnp.sum(p, axis=1, keepdims=True)
                acc_scr[e] = alpha[:, :dv] * acc_scr[e] + jnp.dot(p.astype(BF16), vb, preferred_element_type=F32)
                m_scr[e] = m_new

        def step(j, carry):
            process(j, False)
            return carry

        lax.fori_loop(0, qi, step, 0)
        process(qi, True)
        for e in range(hb):
            l = l_scr[e]
            o_ref[e] = acc_scr[e] / l[:, :dv]
            lse_ref[e, 0] = jnp.transpose(m_scr[e] + jnp.log(l))[0:1, :]

    ins = [BS((hb, t, dk), lambda a, b: (a, b, 0)), BS((hb, n, dk), lambda a, b: (a, 0, 0)),
           BS((hb, n, dv), lambda a, b: (a, 0, 0))]
    args = [q, k, v]
    if fox:
        ins.append(BS((hb, nq, 1, t), lambda a, b: (a, 0, 0, 0)))
        args.append(ck_r)
    outs = [BS((hb, t, dv), lambda a, b: (a, b, 0)), BS((hb, 1, 1, t), lambda a, b: (a, b, 0, 0))]
    oshape = [SDS((h, n, dv), F32), SDS((h, nq, 1, t), F32)]
    scratch = [pltpu.VMEM((hb, t, LANES), F32), pltpu.VMEM((hb, t, LANES), F32), pltpu.VMEM((hb, t, dv), F32)]
    if nl:
        ins += [ANY] * nl
        args += list(late)
        outs += [ANY] * nl
        oshape += [SDS((N_CHIPS,) + a.shape[1:], a.dtype) for a in late]
        scratch += [pltpu.SemaphoreType.DMA((6 * nl,)), pltpu.SemaphoreType.DMA((6 * nl,))]
    res = pl.pallas_call(
        body, name=name, grid=(h // hb, nq), in_specs=ins, out_specs=outs, out_shape=oshape, scratch_shapes=scratch,
        compiler_params=pltpu.CompilerParams(dimension_semantics=("arbitrary", "arbitrary"),
                                             vmem_limit_bytes=VMEM_LIMIT, has_side_effects=bool(nl)))(*args)
    return res[0], res[1], list(res[2:])


def _causal_bwd_call(q, k, v, do, o, lse_r, ck_r, fox, scale, name, side=None):
    h, n, dk = q.shape
    dv = v.shape[2]
    t = _causal_tile(n)
    nq = n // t
    hb = HEADS_PER_STEP
    pre, post = _split_scale(scale)
    ns = 0 if side is None else len(side)

    def body(*refs):
        it = iter(refs)
        q_ref, k_ref, v_ref, do_ref, o_ref, lse_ref = (next(it) for _ in range(6))
        ck_ref = next(it) if fox else None
        side_in = [next(it) for _ in range(ns)]
        dq_ref, dk_ref, dv_ref = next(it), next(it), next(it)
        dck_ref, dcq_ref = (next(it), next(it)) if fox else (None, None)
        side_out = [next(it) for _ in range(ns)]
        delta_scr, dk_scr, dv_scr = next(it), next(it), next(it)
        dck_scr = next(it) if fox else None
        kj = pl.program_id(1)
        if ns:
            send_sems, recv_sems = next(it), next(it)
            x, y, core = _me()
            chips = [(1 - x, y), (x, 1 - y), (1 - x, 1 - y)]
            copies = [_remote(side_in[p].at[2 * cx + cy], side_out[p].at[r], send_sems, recv_sems, 3 * p + r,
                              (cx, cy, core)) for p in range(ns) for r, (cx, cy) in enumerate(chips)]

            @pl.when((pl.program_id(0) == 0) & (kj == 0))
            def _():
                for cp in copies:
                    cp.start()

            @pl.when((pl.program_id(0) == h // hb - 1) & (kj == nq - 1))
            def _():
                for cp in copies:
                    cp.wait()

        @pl.when(kj == 0)
        def _():
            dq_ref[...] = jnp.zeros_like(dq_ref)
            if fox:
                dcq_ref[...] = jnp.zeros_like(dcq_ref)
            ones = jnp.ones((SUBLANES, dv), F32)

            def fill(qi, carry):
                off = pl.multiple_of(qi * t, t)
                for e in range(hb):
                    prod = do_ref[e, pl.ds(off, t), :] * o_ref[e, pl.ds(off, t), :]
                    delta_scr[e, qi] = lax.dot_general(ones, prod, NT_DIMS, preferred_element_type=F32,
                                                       precision=lax.Precision.HIGHEST)[0:1, :]
                return carry

            lax.fori_loop(0, nq, fill, 0)

        kbs = [k_ref[e].astype(BF16) for e in range(hb)]
        vbs = [v_ref[e].astype(BF16) for e in range(hb)]
        dk_scr[...] = jnp.zeros_like(dk_scr)
        dv_scr[...] = jnp.zeros_like(dv_scr)
        if fox:
            dck_scr[...] = jnp.zeros_like(dck_scr)
            ckcs = [jnp.tile(jnp.transpose(jnp.broadcast_to(ck_ref[e, 0], (LANES, t))), (1, t // LANES))
                    for e in range(hb)]

        def process(qi, masked):
            off = pl.multiple_of(qi * t, t)
            if masked:
                krows = lax.broadcasted_iota(jnp.int32, (t, t), 0)
                qcols = lax.broadcasted_iota(jnp.int32, (t, t), 1)
                valid = krows <= qcols
            for e in range(hb):
                qb = (q_ref[e, pl.ds(off, t), :] * pre).astype(BF16)
                dob = do_ref[e, pl.ds(off, t), :].astype(BF16)
                st = lax.dot_general(kbs[e], qb, NT_DIMS, preferred_element_type=F32)
                if post != 1.0:
                    st = st * post
                if fox:
                    st = st - ckcs[e]
                pt = jnp.exp(st - lse_ref[e, qi])
                if masked:
                    pt = jnp.where(valid, pt, 0.0)
                dv_scr[e] += jnp.dot(pt.astype(BF16), dob, preferred_element_type=F32)
                dpt = lax.dot_general(vbs[e], dob, NT_DIMS, preferred_element_type=F32)
                dst = pt * (dpt - delta_scr[e, qi])
                if fox:
                    dck_scr[e] -= jnp.sum(dst, axis=1, keepdims=True)
                    dcq_ref[e, qi] += jnp.sum(dst, axis=0, keepdims=True)
                dsb = (dst if post == 1.0 else dst * post).astype(BF16)
                dk_scr[e] += jnp.dot(dsb, qb, preferred_element_type=F32)
                dq_ref[e, pl.ds(off, t), :] += pre * lax.dot_general(dsb, kbs[e], TN_DIMS, preferred_element_type=F32)

        def step(qi, carry):
            process(qi, False)
            return carry

        process(kj, True)
        lax.fori_loop(kj + 1, nq, step, 0)
        dk_ref[...] = dk_scr[...]
        dv_ref[...] = dv_scr[...]
        if fox:
            for e in range(hb):
                dck_ref[e, 0] = jnp.transpose(jnp.broadcast_to(dck_scr[e], (t, LANES)))[0:1, :]

    whole = lambda a, b: (a, 0, 0)
    tile = lambda a, b: (a, b, 0)
    rowv = lambda a, b: (a, 0, 0, 0)
    rowt = lambda a, b: (a, b, 0, 0)
    ins = [BS((hb, n, dk), whole), BS((hb, t, dk), tile), BS((hb, t, dv), tile), BS((hb, n, dv), whole),
           BS((hb, n, dv), whole), BS((hb, nq, 1, t), rowv)]
    args = [q, k, v, do, o, lse_r]
    outs = [BS((hb, n, dk), whole), BS((hb, t, dk), tile), BS((hb, t, dv), tile)]
    oshape = [SDS((h, n, dk), F32), SDS((h, n, dk), F32), SDS((h, n, dv), F32)]
    scratch = [pltpu.VMEM((hb, nq, 1, t), F32), pltpu.VMEM((hb, t, dk), F32), pltpu.VMEM((hb, t, dv), F32)]
    if fox:
        ins.append(BS((hb, 1, 1, t), rowt))
        args.append(ck_r)
        outs += [BS((hb, 1, 1, t), rowt), BS((hb, nq, 1, t), rowv)]
        oshape += [SDS((h, nq, 1, t), F32), SDS((h, nq, 1, t), F32)]
        scratch.append(pltpu.VMEM((hb, t, 1), F32))
    if ns:
        ins += [ANY] * ns
        args += list(side)
        outs += [ANY] * ns
        oshape += [SDS((3,) + s.shape[1:], s.dtype) for s in side]
        scratch += [pltpu.SemaphoreType.DMA((3 * ns,)), pltpu.SemaphoreType.DMA((3 * ns,))]
    return pl.pallas_call(
        body, name=name, grid=(h // hb, nq), in_specs=ins, out_specs=outs, out_shape=oshape, scratch_shapes=scratch,
        compiler_params=pltpu.CompilerParams(dimension_semantics=("arbitrary", "arbitrary"),
                                             vmem_limit_bytes=VMEM_LIMIT, has_side_effects=bool(ns)))(*args)


def causal_attention(q, k, v, c, scale, name, late=None, late_layer=1, reduce=None, sinks=None):
    h, n, _ = q.shape
    t = _causal_tile(n)
    nq = n // t
    fox = c is not None

    def run_fwd(q, k, v, c, late, sinks):
        ck_r = c.reshape(h, nq, 1, t) if fox else None
        o, lse, got = _causal_fwd_call(q, k, v, ck_r, fox, scale, name + "_f", late, late_layer)
        out = (o,)
        if late is not None:
            out += (got,)
        if reduce is not None:
            out += ([jnp.zeros((N_CHIPS,) + SHARD_SHAPE[w][1:], F32) for w in reduce],)
        return (out if len(out) > 1 else o), (q, k, v, c, late, o, lse)

    def run_bwd(res, ct):
        q, k, v, c, late, o, lse = res
        ck_r = c.reshape(h, nq, 1, t) if fox else None
        dlate = None if late is None else [jnp.zeros_like(a) for a in late]
        if reduce is None:
            do = ct if late is None else ct[0]
            outs = _causal_bwd_call(q, k, v, do, o, lse, ck_r, fox, scale, name + "_b")
            return outs[0], outs[1], outs[2], ((outs[3] + outs[4]).reshape(h, n) if fox else None), dlate, None
        do, g1 = ct[0], ct[-1]
        xi, yi, ci = _me()
        c_idx = ci.astype(jnp.int32).reshape(1)
        j_idx = (2 * xi + yi).astype(jnp.int32).reshape(1)
        gs = [g.reshape((N_CHIPS,) + HALVED[w][1:]) for g, w in zip(g1, reduce)]
        outs = _causal_bwd_call(q, k, v, do, o, lse, ck_r, fox, scale, name + "_b", [g.astype(BF16) for g in gs])
        sc = [_chip_add(g, r, j_idx, name + "_chip_add_" + w).reshape(1, 2, HALVED[w][1] // 2, HALVED[w][2])
              for g, r, w in zip(gs, outs[-len(reduce):], reduce)]
        r1 = _pair_exchange(sc, name + "_pair_exchange")
        gh = [_pair_add(s, r, c_idx, name + "_pair_add_" + w)[0][0] for s, r, w in zip(sc, r1, reduce)]
        go = _half_exchange(gh, name + "_half_exchange")
        return (outs[0], outs[1], outs[2], ((outs[3] + outs[4]).reshape(h, n) if fox else None), dlate,
                list(zip(gh, go)))

    return _op(run_fwd, run_bwd)(q, k, v, c, late, sinks)


SWA_T = 128


def _swa_masks(qi):
    t = SWA_T
    r = lax.broadcasted_iota(jnp.int32, (t, 3 * t), 0)
    c = lax.broadcasted_iota(jnp.int32, (t, 3 * t), 1)
    seg0 = c < t
    seg1 = (c >= t) & (c < 2 * t)
    jp = jnp.maximum(qi - 1, 0)
    kpos = jnp.where(seg0, c, jnp.where(seg1, jp * t + c - t, qi * t + c - 2 * t))
    dist = qi * t + r - kpos
    band = (dist >= 0) & ((dist < WINDOW) | (kpos < N_META))
    valid = (seg0 & (kpos < N_META) & (qi >= 2)) | (jnp.logical_not(seg0) & band & (jnp.logical_not(seg1) | (qi >= 1)))
    return valid, dist.astype(F32)


def _swa_cat(ref, qi):
    t = SWA_T
    jp = jnp.maximum(qi - 1, 0)
    return jnp.concatenate([ref[0, 0:t, :], ref[0, pl.ds(pl.multiple_of(jp * t, t), t), :],
                            ref[0, pl.ds(pl.multiple_of(qi * t, t), t), :]], axis=0).astype(BF16)


def _swa_fwd_call(q, k, v, sinks, slopes, scale, name):
    hq, n, d = q.shape
    hkv = k.shape[0]
    g = hq // hkv
    t = SWA_T
    nq = n // t

    def body(q_ref, k_ref, v_ref, sink_ref, slope_ref, o_ref, lse_ref):
        grp = pl.program_id(0)
        qi = pl.program_id(1)
        valid, dist = _swa_masks(qi)
        kc = _swa_cat(k_ref, qi)
        vc = _swa_cat(v_ref, qi)
        qs = jnp.concatenate([q_ref[e] for e in range(g)], axis=0).astype(BF16)
        s_all = lax.dot_general(qs, kc, NT_DIMS, preferred_element_type=F32) * scale
        ps, ls, ms = [], [], []
        for e in range(g):
            hh = grp * g + e
            s = jnp.where(valid, s_all[e * t:(e + 1) * t] - slope_ref[hh] * dist, NEG)
            m = jnp.maximum(jnp.max(s, axis=1, keepdims=True), sink_ref[hh])
            p = jnp.exp(s - m)
            ls.append(jnp.sum(p, axis=1, keepdims=True) + jnp.exp(sink_ref[hh] - m))
            ms.append(m)
            ps.append(p.astype(BF16))
        acc = jnp.dot(jnp.concatenate(ps, axis=0), vc, preferred_element_type=F32)
        for e in range(g):
            o_ref[e] = acc[e * t:(e + 1) * t] / ls[e]
            lse_ref[e] = ms[e] + jnp.log(ls[e])

    return pl.pallas_call(
        body, name=name, grid=(hkv, nq),
        in_specs=[BS((g, t, d), lambda a, b: (a, b, 0)), BS((1, n, d), lambda a, b: (a, 0, 0)),
                  BS((1, n, d), lambda a, b: (a, 0, 0)), BS(memory_space=pltpu.SMEM), BS(memory_space=pltpu.SMEM)],
        out_specs=[BS((g, t, d), lambda a, b: (a, b, 0)), BS((g, t, 1), lambda a, b: (a, b, 0))],
        out_shape=[SDS((hq, n, d), F32), SDS((hq, n, 1), F32)],
        compiler_params=_cparams(("parallel", "parallel")))(q, k, v, sinks, slopes)


def _swa_bwd_call(q, k, v, o, lse, do, sinks, slopes, scale, name):
    hq, n, d = q.shape
    hkv = k.shape[0]
    g = hq // hkv
    t = SWA_T
    nq = n // t

    def body(q_ref, k_ref, v_ref, o_ref, lse_ref, do_ref, sink_ref, slope_ref, dq_ref, dk_ref, dv_ref, ds_ref):
        grp = pl.program_id(0)
        qi = pl.program_id(1)

        @pl.when(qi == 0)
        def _():
            dk_ref[...] = jnp.zeros_like(dk_ref)
            dv_ref[...] = jnp.zeros_like(dv_ref)
            ds_ref[...] = jnp.zeros_like(ds_ref)

        valid, dist = _swa_masks(qi)
        kc = _swa_cat(k_ref, qi)
        vc = _swa_cat(v_ref, qi)
        qs = jnp.concatenate([q_ref[e] for e in range(g)], axis=0).astype(BF16)
        dos = jnp.concatenate([do_ref[e] for e in range(g)], axis=0).astype(BF16)
        s_all = lax.dot_general(qs, kc, NT_DIMS, preferred_element_type=F32) * scale
        dp_all = lax.dot_general(dos, vc, NT_DIMS, preferred_element_type=F32)
        ps, dss = [], []
        for e in range(g):
            hh = grp * g + e
            lse_e = lse_ref[e]
            delta = jnp.sum(do_ref[e] * o_ref[e], axis=1, keepdims=True)
            s = s_all[e * t:(e + 1) * t] - slope_ref[hh] * dist
            p = jnp.where(valid, jnp.exp(s - lse_e), 0.0)
            ds = p * (dp_all[e * t:(e + 1) * t] - delta)
            ps.append(p.astype(BF16))
            dss.append((ds * scale).astype(BF16))
            ds_ref[e] += -jnp.sum(jnp.exp(sink_ref[hh] - lse_e) * delta)
        p_st = jnp.concatenate(ps, axis=0)
        ds_st = jnp.concatenate(dss, axis=0)
        dq = jnp.dot(ds_st, kc, preferred_element_type=F32)
        for e in range(g):
            dq_ref[e] = dq[e * t:(e + 1) * t]
        dkc = lax.dot_general(ds_st, qs, TN_DIMS, preferred_element_type=F32)
        dvc = lax.dot_general(p_st, dos, TN_DIMS, preferred_element_type=F32)
        jp = jnp.maximum(qi - 1, 0)
        for seg, off in enumerate((0, pl.multiple_of(jp * t, t), pl.multiple_of(qi * t, t))):
            dk_ref[0, pl.ds(off, t), :] += dkc[seg * t:(seg + 1) * t]
            dv_ref[0, pl.ds(off, t), :] += dvc[seg * t:(seg + 1) * t]

    tile = lambda a, b: (a, b, 0)
    whole = lambda a, b: (a, 0, 0)
    return pl.pallas_call(
        body, name=name, grid=(hkv, nq),
        in_specs=[BS((g, t, d), tile), BS((1, n, d), whole), BS((1, n, d), whole), BS((g, t, d), tile),
                  BS((g, t, 1), tile), BS((g, t, d), tile), BS(memory_space=pltpu.SMEM), BS(memory_space=pltpu.SMEM)],
        out_specs=[BS((g, t, d), tile), BS((1, n, d), whole), BS((1, n, d), whole), BS((g, 1, LANES), whole)],
        out_shape=[SDS((hq, n, d), F32), SDS((hkv, n, d), F32), SDS((hkv, n, d), F32), SDS((hq, 1, LANES), F32)],
        compiler_params=_cparams(("arbitrary", "arbitrary")))(q, k, v, o, lse, do, sinks, slopes)


def window_attention(q, k, v, sinks, slopes, scale, name):
    def run_fwd(q, k, v, sinks, slopes):
        o, lse = _swa_fwd_call(q, k, v, sinks, slopes, scale, name + "_f")
        return o, (q, k, v, sinks, slopes, o, lse)

    def run_bwd(res, do):
        q, k, v, sinks, slopes, o, lse = res
        dq, dk, dv, ds = _swa_bwd_call(q, k, v, o, lse, do, sinks, slopes, scale, name + "_b")
        return dq, dk, dv, ds[:, 0, 0], jnp.zeros_like(slopes)

    return _op(run_fwd, run_bwd)(q, k, v, sinks, slopes)


def _sigmoid(x):
    return 1.0 / (1.0 + jnp.exp(-x))


def _merge_fwd_call(gs, ys, name):
    n, c = ys[0].shape
    tr = _rows_tile(n, c, 1 << 20)

    def body(g0, g1, g2, y0, y1, y2, m_ref):
        m_ref[...] = (_sigmoid(g0[...]) * y0[...] + _sigmoid(g1[...]) * y1[...]) + _sigmoid(g2[...]) * y2[...]

    spec = BS((tr, c), lambda i: (i, 0))
    return pl.pallas_call(
        body, name=name, grid=(n // tr,), in_specs=[spec] * 6, out_specs=spec, out_shape=SDS((n, c), F32),
        compiler_params=_cparams(("parallel",)))(*gs, *ys)


def _merge_bwd_call(gs, ys, dm, name):
    n, c = ys[0].shape
    tr = _rows_tile(n, c, 1 << 20)

    def body(g0, g1, g2, y0, y1, y2, dm_ref, dg0, dg1, dg2, dy0, dy1, dy2):
        d = dm_ref[...]
        for g, y, dg, dy in ((g0, y0, dg0, dy0), (g1, y1, dg1, dy1), (g2, y2, dg2, dy2)):
            s = _sigmoid(g[...])
            dy[...] = d * s
            dg[...] = d * y[...] * (s * (1.0 - s))

    spec = BS((tr, c), lambda i: (i, 0))
    return pl.pallas_call(
        body, name=name, grid=(n // tr,), in_specs=[spec] * 7, out_specs=[spec] * 6,
        out_shape=[SDS((n, c), F32)] * 6, compiler_params=_cparams(("parallel",)))(*gs, *ys, dm)


def gated_merge(gs, ys, name):
    def fwd(gs, ys):
        return _merge_fwd_call(gs, ys, name + "_f"), (gs, ys)

    def bwd(res, dm):
        out = _merge_bwd_call(res[0], res[1], dm, name + "_b")
        return tuple(out[:3]), tuple(out[3:])

    return _op(fwd, bwd)(tuple(gs), tuple(ys))


CONV_TR = 264
CONV_TC = 1408


def _conv_tiles(n, f):
    tr = CONV_TR if n % CONV_TR == 0 else _div_tile(n, CONV_TR)
    tc = CONV_TC if f % CONV_TC == 0 else f
    return tr, tc


def _shift_down(cur, halo, first, tr):
    halo = jnp.where(first, 0.0, halo)
    row = lax.broadcasted_iota(jnp.int32, cur.shape, 0)
    h7, h6 = halo[7:8, :], halo[6:7, :]
    u1 = jnp.where(row == 0, h7, pltpu.roll(cur, 1, 0))
    u2 = jnp.where(row == 0, h6, jnp.where(row == 1, h7, pltpu.roll(cur, 2, 0)))
    return u1, u2


def _conv_lin(cur, u1, u2, w_ref, b_ref):
    return ((b_ref[...] + w_ref[0:1, :] * u2) + w_ref[1:2, :] * u1) + w_ref[2:3, :] * cur


def _conv_in_specs(tr, tc, nj):
    sub = tr // SUBLANES
    prev = lambda j, i: (jnp.maximum(i * sub - 1, 0), j)
    prev_v = lambda j, i: (jnp.maximum(i * sub - 1, 0), j + nj)
    return [BS((tr, tc), lambda j, i: (i, j)), BS((SUBLANES, tc), prev),
            BS((tr, tc), lambda j, i: (i, j + nj)), BS((SUBLANES, tc), prev_v),
            BS((3, tc), lambda j, i: (0, j)), BS((3, tc), lambda j, i: (0, j + nj)),
            BS((1, tc), lambda j, i: (0, j)), BS((1, tc), lambda j, i: (0, j + nj))]


def _conv_fwd_call(u, cw, cb, name):
    n, f2 = u.shape
    f = f2 // 2
    tr, tc = _conv_tiles(n, f)
    nj = f // tc

    def body(ug, ugh, uv, uvh, wg, wv, bg, bv, a_ref):
        first = pl.program_id(1) == 0
        g1, g2 = _shift_down(ug[...], ugh[...], first, tr)
        v1, v2 = _shift_down(uv[...], uvh[...], first, tr)
        cg = _conv_lin(ug[...], g1, g2, wg, bg)
        cv = _conv_lin(uv[...], v1, v2, wv, bv)
        a_ref[...] = cg * _sigmoid(cg) * cv

    return pl.pallas_call(
        body, name=name, grid=(nj, n // tr), in_specs=_conv_in_specs(tr, tc, nj),
        out_specs=BS((tr, tc), lambda j, i: (i, j)), out_shape=SDS((n, f), F32),
        compiler_params=_cparams(("parallel", "parallel")))(u, u, u, u, cw, cw, cb, cb)


def _conv_bwd_dc_call(u, cw, cb, da, name):
    n, f2 = u.shape
    f = f2 // 2
    tr, tc = _conv_tiles(n, f)
    nj = f // tc

    def body(ug, ugh, uv, uvh, wg, wv, bg, bv, da_ref, dc_ref, dw_ref, db_ref):
        first = pl.program_id(1) == 0
        g0, v0 = ug[...], uv[...]
        g1, g2 = _shift_down(g0, ugh[...], first, tr)
        v1, v2 = _shift_down(v0, uvh[...], first, tr)
        cg = _conv_lin(g0, g1, g2, wg, bg)
        cv = _conv_lin(v0, v1, v2, wv, bv)
        d = da_ref[...]
        s = _sigmoid(cg)
        dcg = d * cv * (s * (1.0 + cg * (1.0 - s)))
        dcv = d * (cg * s)
        dc_ref[0] = dcg
        dc_ref[1] = dcv

        @pl.when(first)
        def _():
            dw_ref[...] = jnp.zeros_like(dw_ref)
            db_ref[...] = jnp.zeros_like(db_ref)

        for p, dc, taps in ((0, dcg, (g2, g1, g0)), (1, dcv, (v2, v1, v0))):
            for t in range(3):
                dw_ref[p, t:t + 1, :] += jnp.sum(dc * taps[t], axis=0, keepdims=True)
            db_ref[p] += jnp.sum(dc, axis=0, keepdims=True)

    return pl.pallas_call(
        body, name=name, grid=(nj, n // tr),
        in_specs=_conv_in_specs(tr, tc, nj) + [BS((tr, tc), lambda j, i: (i, j))],
        out_specs=[BS((2, tr, tc), lambda j, i: (0, i, j)), BS((2, 3, tc), lambda j, i: (0, 0, j)),
                   BS((2, 1, tc), lambda j, i: (0, 0, j))],
        out_shape=[SDS((2, n, f), F32), SDS((2, 3, f), F32), SDS((2, 1, f), F32)],
        compiler_params=_cparams(("arbitrary", "arbitrary")))(u, u, u, u, cw, cw, cb, cb, da)


def _conv_bwd_du_call(dc, cw, name):
    _, n, f = dc.shape
    tr, tc = _conv_tiles(n, f)
    nj = f // tc
    ni = n // tr
    sub = tr // SUBLANES

    def body(c_ref, nx_ref, w_ref, du_ref):
        cur = c_ref[0]
        nxt = jnp.where(pl.program_id(2) == ni - 1, 0.0, nx_ref[0])
        row = lax.broadcasted_iota(jnp.int32, cur.shape, 0)
        n0, n1 = nxt[0:1, :], nxt[1:2, :]
        d1 = jnp.where(row == tr - 1, n0, pltpu.roll(cur, tr - 1, 0))
        d2 = jnp.where(row == tr - 1, n1, jnp.where(row == tr - 2, n0, pltpu.roll(cur, tr - 2, 0)))
        du_ref[...] = (w_ref[2:3, :] * cur + w_ref[1:2, :] * d1) + w_ref[0:1, :] * d2

    nxt_map = lambda p, j, i: (p, jnp.minimum((i + 1) * sub, n // SUBLANES - 1), j)
    return pl.pallas_call(
        body, name=name, grid=(2, nj, ni),
        in_specs=[BS((1, tr, tc), lambda p, j, i: (p, i, j)), BS((1, SUBLANES, tc), nxt_map),
                  BS((3, tc), lambda p, j, i: (0, p * nj + j))],
        out_specs=BS((tr, tc), lambda p, j, i: (i, p * nj + j)), out_shape=SDS((n, 2 * f), F32),
        compiler_params=_cparams(("parallel", "parallel", "parallel")))(dc, dc, cw)


def conv_glu(u, cw, cb, name):
    def fwd(u, cw, cb):
        return _conv_fwd_call(u, cw, cb, name + "_f"), (u, cw, cb)

    def bwd(res, da):
        u, cw, cb = res
        dc, dw, db = _conv_bwd_dc_call(u, cw, cb, da, name + "_bc")
        du = _conv_bwd_du_call(dc, cw, name + "_bu")
        return du, jnp.concatenate([dw[0], dw[1]], axis=-1), jnp.concatenate([db[0], db[1]], axis=-1)

    return _op(fwd, bwd)(u, cw, cb)


def _loss_call(y, t, n_real, name):
    n, c = y.shape
    tr = _rows_tile(n, c, 1 << 20)

    def body(y_ref, t_ref, dy_ref, l_ref):
        i = pl.program_id(0)
        row = i * tr + lax.broadcasted_iota(jnp.int32, (tr, c), 0)
        real = (row >= N_META) & (row < N_META + n_real)
        e = jnp.where(real, y_ref[...] - t_ref[...], 0.0)
        dy_ref[...] = e * (1.0 / c)

        @pl.when(i == 0)
        def _():
            l_ref[...] = jnp.zeros_like(l_ref)

        l_ref[...] += 0.5 * jnp.sum(jnp.sum(e * e, axis=-1, keepdims=True) * (1.0 / c), axis=0, keepdims=True)

    spec = BS((tr, c), lambda i: (i, 0))
    return pl.pallas_call(
        body, name=name, grid=(n // tr,), in_specs=[spec, spec],
        out_specs=[spec, BS((1, 1), lambda i: (0, 0))], out_shape=[SDS((n, c), F32), SDS((1, 1), F32)],
        compiler_params=_cparams(("arbitrary",)))(y, t)


def _to_heads(x, nh):
    n = x.shape[0]
    return x.reshape(n, nh, x.shape[1] // nh).transpose(1, 0, 2)


def _from_heads(x):
    h, n, d = x.shape
    return x.transpose(1, 0, 2).reshape(n, h * d)


def _head_norm(x, g, denom, name):
    h, n, d = x.shape
    return rms_norm(x.reshape(h * n, d), g, denom, name).reshape(h, n, d)


def _head_norm_rope(x, g, cos, sin, name):
    h, n, d = x.shape
    return rms_norm_rope(x.reshape(h * n, d), g, cos, sin, MLA_QK, name).reshape(h, n, d)


def _pad_in_cols(w):
    z = lambda k: jnp.zeros(w.shape[:-1] + (k,), w.dtype)
    return jnp.concatenate([w[..., :1544], z(120), w[..., 1544:1960], z(96), w[..., 1960:], z(128)], axis=-1)


def _pad_q_up(w):
    s = w.shape[:-1]
    w = w.reshape(s + (HEADS, MLA_QK))
    w = jnp.concatenate([w, jnp.zeros(s + (HEADS, LANES - MLA_QK), w.dtype)], axis=-1)
    return w.reshape(s + (HEADS * LANES,))


LAYERED = BIG + ("ffn_conv_w",)
BEFORE_MLA = ("w_in", "mla_w_q_up", "mla_w_kv_up")
AFTER_MLA = ("w_branch", "w_o", "ffn_w_up", "ffn_w_down")


def _assemble_layer(parts):
    out = {k: jnp.concatenate([v[i] for i in range(N_CHIPS)], axis=SHARD_AXIS[k] - 1) for k, v in parts.items()}
    if "w_in" in out:
        out["w_in"] = _pad_in_cols(out["w_in"])
    if "mla_w_q_up" in out:
        out["mla_w_q_up"] = _pad_q_up(out["mla_w_q_up"])
    return out


def _layer_parts(names, gathered):
    return {k: g.reshape((N_CHIPS,) + SHARD_SHAPE[k][1:]) for k, g in zip(names, gathered)}


def _rope_tables(n):
    half = MLA_ROPE // 2
    freqs = ROPE_THETA ** (-jnp.arange(half, dtype=F32) / half)
    ang = jnp.arange(n).astype(F32)[:, None] * freqs[None, :]
    cos, sin = jnp.cos(ang), jnp.sin(ang)
    one, zero = jnp.ones((n, MLA_NOPE), F32), jnp.zeros((n, MLA_NOPE), F32)
    tail1, tail0 = jnp.ones((n, LANES - MLA_QK), F32), jnp.zeros((n, LANES - MLA_QK), F32)
    return (jnp.concatenate([one, cos, cos, tail1], axis=1), jnp.concatenate([zero, sin, sin, tail0], axis=1))


def _pad_lanes(g, width):
    return jnp.concatenate([g, jnp.zeros((width - g.shape[0],), g.dtype)]).reshape(1, width)


PROJ_SEGMENTS = ((O_FQ, 512), (O_FK, 512), (O_FV, 512), (O_FF, HEADS), (O_CQ, MLA_Q_RANK), (O_CKV, MLA_KV_RANK),
                 (O_KR, MLA_ROPE), (O_SQ, 512), (O_SK, 128), (O_SV, 128), (O_G, D_MODEL), (O_G + D_MODEL, D_MODEL),
                 (O_G + 2 * D_MODEL, D_MODEL))


def _join_proj_call(parts, name):
    n = parts[0].shape[0]
    tr = _rows_tile(n, IN_PAD, 4 << 20)

    def body(*refs):
        o_ref = refs[-1]
        o_ref[...] = jnp.zeros_like(o_ref)
        for (s, w), r in zip(PROJ_SEGMENTS, refs[:-1]):
            o_ref[:, s:s + w] = r[...]

    return pl.pallas_call(
        body, name=name, grid=(n // tr,), in_specs=[BS((tr, w), lambda i: (i, 0)) for _, w in PROJ_SEGMENTS],
        out_specs=BS((tr, IN_PAD), lambda i: (i, 0)), out_shape=SDS((n, IN_PAD), F32),
        compiler_params=_cparams(("parallel",)))(*parts)


def _split_proj(proj, name):
    def fwd(x):
        return tuple(x[:, s:s + w] for s, w in PROJ_SEGMENTS), None

    def bwd(_, cts):
        return (_join_proj_call(cts, name + "_b"),)

    return _op(fwd, bwd)(proj)


def _trunk(eps, eps_cw, sinks, meta, small, x, w0, late):
    assert DEPTH == 2
    seq = x.shape[0]
    n = -(-(N_META + seq) // ROW_PAD) * ROW_PAD
    ew = _assemble_layer(eps)
    cos, sin = _rope_tables(n)
    slopes = jnp.exp2(-8.0 * jnp.arange(1, HEADS + 1, dtype=F32) / HEADS)
    h = jnp.concatenate([meta, x, jnp.zeros((n - N_META - seq, D_MODEL), F32)], axis=0)
    wb = w0
    for l in range(DEPTH):
        p = f"l{l}_"
        row = lambda name: small[name][l].reshape(1, -1)
        xn = rms_norm(h, row("norm1_g"), D_MODEL, p + "norm1")
        proj = linear(xn, wb["w_in"], ew["w_in"], p + "win")
        p_fq, p_fk, p_fv, p_ff, p_cq, p_ckv, p_kr, p_sq, p_sk, p_sv, g0, g1, g2 = _split_proj(proj, p + "split")
        fq = _head_norm(_to_heads(p_fq, HEADS), row("fox_q_g"), HEAD_DIM, p + "fqn")
        fk = _head_norm(_to_heads(p_fk, HEADS), row("fox_k_g"), HEAD_DIM, p + "fkn")
        fv = _to_heads(p_fv, HEADS)
        c = forget_cumsum(p_ff.T, small["fox_forget_b"][l].reshape(HEADS, 1), p + "fgate")
        if l == 0:
            rest = [a for k, a in zip(LAYERED, late) if k != "w_in"]
            out_a, got, carriers = causal_attention(fq, fk, fv, c, HEAD_DIM ** -0.5, p + "fox", rest, 0, BEFORE_MLA,
                                                    sinks[1])
            wb = dict(wb, **_assemble_layer(_layer_parts(LAYERED[1:], _fill_own(got, [a[0] for a in rest]))))
            ew1 = _assemble_layer(dict(zip(BEFORE_MLA, carriers)))
        else:
            out_a = causal_attention(fq, fk, fv, c, HEAD_DIM ** -0.5, p + "fox")
        cqn = rms_norm(p_cq, row("mla_q_a_g"), MLA_Q_RANK, p + "cqn")
        q = _to_heads(linear(cqn, wb["mla_w_q_up"], ew["mla_w_q_up"], p + "qup"), HEADS)
        q = _head_norm_rope(q, _pad_lanes(small["mla_q_g"][l], LANES), cos, sin, p + "mqn")
        ckvn = rms_norm(p_ckv, row("mla_kv_a_g"), MLA_KV_RANK, p + "ckvn")
        kv = _to_heads(linear(ckvn, wb["mla_w_kv_up"], ew["mla_w_kv_up"], p + "kvup"), HEADS)
        kr = jnp.broadcast_to(p_kr[None], (HEADS, n, MLA_ROPE))
        k = jnp.concatenate([kv[..., :MLA_NOPE], kr, jnp.zeros((HEADS, n, LANES - MLA_QK), F32)], axis=-1)
        k = _head_norm_rope(k, _pad_lanes(small["mla_k_g"][l], LANES), cos, sin, p + "mkn")
        if l == 0:
            out_b, got, carriers = causal_attention(q, k, kv[..., MLA_NOPE:], None, MLA_QK ** -0.5, p + "mla", late,
                                                    1, AFTER_MLA, sinks[0])
            w1 = _assemble_layer(_layer_parts(LAYERED, _fill_own(got, [a[1] for a in late])))
        else:
            out_b, carriers = causal_attention(q, k, kv[..., MLA_NOPE:], None, MLA_QK ** -0.5, p + "mla", None, 1,
                                               AFTER_MLA, sinks[2])
        ew = dict(ew, **_assemble_layer(dict(zip(AFTER_MLA, carriers))))
        sq = _head_norm(_to_heads(p_sq, HEADS), row("swa_q_g"), HEAD_DIM, p + "sqn")
        sk = _head_norm(_to_heads(p_sk, SWA_KV_HEADS), row("swa_k_g"), HEAD_DIM, p + "skn")
        sv = _to_heads(p_sv, SWA_KV_HEADS)
        out_c = window_attention(sq, sk, sv, small["swa_sinks"][l], slopes, HEAD_DIM ** -0.5, p + "swa")
        ys = [linear(_from_heads(o), wb["w_branch"][i], ew["w_branch"][i], p + f"br{i}")
              for i, o in enumerate((out_a, out_b, out_c))]
        merged = gated_merge([g0, g1, g2], ys, p + "merge")
        h = linear(merged, wb["w_o"], ew["w_o"], p + "wo", res=h)
        xn2 = rms_norm(h, row("norm2_g"), D_MODEL, p + "norm2")
        u = linear(xn2, wb["ffn_w_up"], ew["ffn_w_up"], p + "wup")
        act = conv_glu(u, lax.stop_gradient(wb["ffn_conv_w"]) + eps_cw[l], row("ffn_conv_b"), p + "conv")
        h = linear(act, wb["ffn_w_down"], ew["ffn_w_down"], p + "wdown", res=h)
        wb, ew = w1, ew1
    return h


def _local_step(x, target, w0, late, meta, small):
    seq = x.shape[0]
    eps = {k: jnp.zeros((N_CHIPS,) + SHARD_SHAPE[k][1:], F32) for k in BEFORE_MLA}
    eps_cw = jnp.zeros((DEPTH, 3, 2 * D_FF), F32)
    half = lambda k: jnp.zeros((HALVED[k][1] // 2, HALVED[k][2]), F32)
    sinks = tuple([(half(k), half(k)) for k in names] for names in (AFTER_MLA, BEFORE_MLA, AFTER_MLA))
    y, vjp = jax.vjp(lambda e, ec, sk, mt, s, xx: _trunk(e, ec, sk, mt, s, xx, w0, late),
                     eps, eps_cw, sinks, meta, small, x)
    n = y.shape[0]
    tpad = jnp.concatenate([jnp.zeros((N_META, D_MODEL), F32), target, jnp.zeros((n - N_META - seq, D_MODEL), F32)])
    dy, loss = _loss_call(y, tpad, seq, "loss")
    g_eps, g_cw, g_l1, g_meta, g_small, g_x = vjp(dy)
    return loss[0, 0], g_x, g_eps, g_l1, g_cw, g_meta, g_small


def _pack_rows(shapes, mult):
    total = sum(_size(s) for s in shapes)
    rows = -(-total // LANES)
    return -(-rows // mult) * mult


def _size(shape):
    n = 1
    for d in shape:
        n *= d
    return n


def _pack(arrs, rows, dtype):
    flat = [a.reshape(-1).astype(dtype) for a in arrs]
    used = sum(a.size for a in flat)
    flat.append(jnp.zeros((rows * LANES - used,), dtype))
    return jnp.concatenate(flat).reshape(rows, LANES)


def _unpack(p, shapes):
    flat = p.reshape(-1)
    out, off = [], 0
    for s in shapes:
        out.append(flat[off:off + _size(s)].reshape(s))
        off += _size(s)
    return out


MESH = pl.DeviceIdType.MESH
ANY = pl.BlockSpec(memory_space=pl.ANY)


def _me():
    return lax.axis_index("x"), lax.axis_index("y"), lax.axis_index("c")


def _remote(src, dst, send_sems, recv_sems, idx, dev):
    return pltpu.make_async_remote_copy(src_ref=src, dst_ref=dst, send_sem=send_sems.at[idx], recv_sem=recv_sems.at[idx],
                                        device_id=dev, device_id_type=MESH)


def _layer_gather(ins, outs, send_sems, recv_sems, layer, base):
    x, y, _ = _me()
    j = 2 * x + y
    sibling = (x, y, 1 - layer)
    chips = [(1 - x, y), (x, 1 - y), (1 - x, 1 - y)]

    def ici(p, r):
        cx, cy = chips[r]
        return _remote(ins[p].at[layer], outs[p].at[j], send_sems, recv_sems, base + 6 * p + r, (cx, cy, layer))

    def d2d(p, r):
        cx, cy = chips[r]
        blk = outs[p].at[2 * cx + cy]
        return _remote(blk, blk, send_sems, recv_sems, base + 6 * p + 3 + r, sibling)

    pairs = [(p, r) for p in range(len(ins)) for r in range(3)]

    def start():
        for p, r in pairs:
            ici(p, r).start()

    def forward():
        for p, r in pairs:
            ici(p, r).wait_recv()
            d2d(p, r).start()

    def drain():
        for p, r in pairs:
            ici(p, r).wait_send()
            d2d(p, r).wait_send()

    def receive():
        for p, r in pairs:
            d2d(p, r).wait_recv()

    return start, forward, drain, receive


def _fill_own(outs, own):
    j = 2 * lax.axis_index("x") + lax.axis_index("y")
    return [lax.dynamic_update_index_in_dim(o, a, j, 0) for o, a in zip(outs, own)]


def _gather_early(meta, arrs, name):
    npk = len(arrs)

    def body(*refs):
        m_in, ins = refs[0], refs[1:npk + 1]
        m_out, outs = refs[npk + 1], refs[npk + 2:2 * npk + 2]
        send_sems, recv_sems = refs[2 * npk + 2:]
        x, y, c = _me()
        j = 2 * x + y
        sibling = (x, y, 1 - c)
        chips = [(1 - x, y), (x, 1 - y), (1 - x, 1 - y)]
        start, forward, drain, receive = _layer_gather(ins, outs, send_sems, recv_sems, 0, 6)
        sends = []
        for r, (cx, cy) in enumerate(chips):
            cp = _remote(m_in.at[c], m_out.at[j, c], send_sems, recv_sems, r, (cx, cy, c))
            cp.start()
            sends.append(cp)
        pl.when(c == 0)(start)
        for r, (cx, cy) in enumerate(chips):
            blk = m_out.at[2 * cx + cy, c]
            _remote(blk, blk, send_sems, recv_sems, r, sibling).wait_recv()
            fw = _remote(blk, blk, send_sems, recv_sems, 3 + r, sibling)
            fw.start()
            sends.append(fw)
        for r, (cx, cy) in enumerate(chips):
            blk = m_out.at[2 * cx + cy, 1 - c]
            _remote(blk, blk, send_sems, recv_sems, 3 + r, sibling).wait_recv()
        for cp in sends:
            cp.wait_send()

        @pl.when(c == 0)
        def _():
            forward()
            drain()

        pl.when(c == 1)(receive)

    nsem = 6 + 6 * npk
    res = pl.pallas_call(
        body, name=name, in_specs=[ANY] * (npk + 1), out_specs=[ANY] * (npk + 1),
        out_shape=[SDS((N_CHIPS,) + meta.shape, meta.dtype)] + [SDS((N_CHIPS,) + a.shape[1:], a.dtype) for a in arrs],
        scratch_shapes=[pltpu.SemaphoreType.DMA((nsem,)), pltpu.SemaphoreType.DMA((nsem,))],
        compiler_params=pltpu.CompilerParams(has_side_effects=True))(meta, *arrs)
    return _fill_own(res[:1], [meta])[0], _fill_own(res[1:], [a[0] for a in arrs])


def _pair_exchange(gs, name):
    npk = len(gs)

    def body(*refs):
        ins, outs = refs[:npk], refs[npk:2 * npk]
        send_sems, recv_sems = refs[2 * npk:]
        x, y, c = _me()
        cps = [_remote(ins[p].at[:, 1 - c], outs[p], send_sems, recv_sems, p, (x, y, 1 - c)) for p in range(npk)]
        for cp in cps:
            cp.start()
        for cp in cps:
            cp.wait()

    return pl.pallas_call(
        body, name=name, in_specs=[ANY] * npk, out_specs=[ANY] * npk,
        out_shape=[SDS(g.shape[:1] + g.shape[2:], g.dtype) for g in gs],
        scratch_shapes=[pltpu.SemaphoreType.DMA((npk,)), pltpu.SemaphoreType.DMA((npk,))],
        compiler_params=pltpu.CompilerParams(has_side_effects=True))(*gs)


def _chip_exchange(ss, small, name):
    npk = len(ss)

    def body(*refs):
        ins, sm_ref = refs[:npk], refs[npk]
        outs, sa_ref = refs[npk + 1:2 * npk + 1], refs[2 * npk + 1]
        send_sems, recv_sems, loc_sem = refs[2 * npk + 2:]
        x, y, c = _me()
        me = 4 * x + 2 * y + c
        lc = pltpu.make_async_copy(sm_ref, sa_ref.at[me], loc_sem.at[0])
        lc.start()
        cps = []
        for p in range(npk):
            for r, (cx, cy) in enumerate([(1 - x, y), (x, 1 - y), (1 - x, 1 - y)]):
                cp = _remote(ins[p].at[2 * cx + cy], outs[p].at[r], send_sems, recv_sems, 3 * p + r, (cx, cy, c))
                cp.start()
                cps.append(cp)
        base = 3 * npk - 1
        for mask in range(1, N_DEV):
            px, py, pc = x ^ (mask >> 2), y ^ ((mask >> 1) & 1), c ^ (mask & 1)
            cp = _remote(sm_ref, sa_ref.at[me], send_sems, recv_sems, base + mask, (px, py, pc))
            cp.start()
            cps.append(cp)
        for p in range(npk):
            for r in range(3):
                _remote(outs[p].at[r], outs[p].at[r], send_sems, recv_sems, 3 * p + r, (x, y, c)).wait_recv()
        for mask in range(1, N_DEV):
            src = 4 * (x ^ (mask >> 2)) + 2 * (y ^ ((mask >> 1) & 1)) + (c ^ (mask & 1))
            _remote(sa_ref.at[src], sa_ref.at[src], send_sems, recv_sems, base + mask, (x, y, c)).wait_recv()
        for cp in cps:
            cp.wait_send()
        lc.wait()

    nsem = 3 * npk + N_DEV - 1
    res = pl.pallas_call(
        body, name=name, in_specs=[ANY] * (npk + 1), out_specs=[ANY] * (npk + 1),
        out_shape=[SDS((3,) + s.shape[1:], s.dtype) for s in ss] + [SDS((N_DEV,) + small.shape, small.dtype)],
        scratch_shapes=[pltpu.SemaphoreType.DMA((nsem,)), pltpu.SemaphoreType.DMA((nsem,)),
                        pltpu.SemaphoreType.DMA((1,))],
        compiler_params=pltpu.CompilerParams(has_side_effects=True))(*ss, small)
    return res[:npk], res[npk]


def _half_exchange(ghs, name):
    npk = len(ghs)

    def body(*refs):
        ins, outs = refs[:npk], refs[npk:2 * npk]
        send_sems, recv_sems = refs[2 * npk:]
        x, y, c = _me()
        cps = [_remote(ins[p], outs[p], send_sems, recv_sems, p, (x, y, 1 - c)) for p in range(npk)]
        for cp in cps:
            cp.start()
        for cp in cps:
            cp.wait()

    return pl.pallas_call(
        body, name=name, in_specs=[ANY] * npk, out_specs=[ANY] * npk, out_shape=[SDS(g.shape, g.dtype) for g in ghs],
        scratch_shapes=[pltpu.SemaphoreType.DMA((npk,)), pltpu.SemaphoreType.DMA((npk,))],
        compiler_params=pltpu.CompilerParams(has_side_effects=True))(*ghs)


def _add_tile(rows, cols):
    return _div_tile(rows, max(16, (1 << 19) // max(cols, LANES)), 16)


def _pair_add(g, r1, c_idx, name):
    _, rows, cols = r1.shape
    tr = _add_tile(rows, cols)

    def body(c_ref, g_ref, r_ref, o_ref, ob_ref):
        s = g_ref[0] + r_ref[...]
        o_ref[...] = s
        ob_ref[...] = s.astype(BF16)

    own = BS((1, tr, cols), lambda k, i, c: (k, i, 0))
    return pl.pallas_call(
        body, name=name,
        grid_spec=pltpu.PrefetchScalarGridSpec(
            num_scalar_prefetch=1, grid=(r1.shape[0], rows // tr),
            in_specs=[BS((1, 1, tr, cols), lambda k, i, c: (k, c[0], i, 0)), own], out_specs=[own, own]),
        out_shape=[SDS(r1.shape, F32), SDS(r1.shape, BF16)],
        compiler_params=_cparams(("parallel", "parallel")))(c_idx, g, r1)


def _chip_add(s1, r2, j_idx, name):
    _, rows, cols = s1.shape
    tr = _add_tile(rows, cols)

    def body(j_ref, s_ref, r_ref, o_ref):
        o_ref[...] = ((s_ref[0] + r_ref[0].astype(F32)) + r_ref[1].astype(F32)) + r_ref[2].astype(F32)

    return pl.pallas_call(
        body, name=name,
        grid_spec=pltpu.PrefetchScalarGridSpec(
            num_scalar_prefetch=1, grid=(rows // tr,),
            in_specs=[BS((1, tr, cols), lambda i, j: (j[0], i, 0)), BS((3, tr, cols), lambda i, j: (0, i, 0))],
            out_specs=BS((tr, cols), lambda i, j: (i, 0))),
        out_shape=SDS((rows, cols), F32), compiler_params=_cparams(("parallel",)))(j_idx, s1, r2)


def _adamw_math(w, g, m, v):
    m = ADAM_B1 * m + (1.0 - ADAM_B1) * g
    v = ADAM_B2 * v + (1.0 - ADAM_B2) * (g * g)
    m_hat = m / (1.0 - ADAM_B1 ** ADAM_STEP)
    v_hat = v / (1.0 - ADAM_B2 ** ADAM_STEP)
    delta = -ADAM_LR * (m_hat / (jnp.sqrt(v_hat) + ADAM_EPS) + ADAM_WD * w)
    return delta, m, v


def _adamw(w, gh, go, m, v, c_idx, name):
    _, rows, cols = w.shape
    tr = _add_tile(rows, cols)

    def body(c_ref, w_ref, gh_ref, go_ref, m_ref, v_ref, g_out, d_out, m_out, v_out):
        g = jnp.where(pl.program_id(0) == c_ref[0], gh_ref[...], go_ref[...])
        g_out[0] = g
        d_out[0], m_out[0], v_out[0] = _adamw_math(w_ref[0], g, m_ref[0], v_ref[0])

    full = BS((1, tr, cols), lambda hf, i, c: (hf, i, 0))
    half = BS((tr, cols), lambda hf, i, c: (i, 0))
    return pl.pallas_call(
        body, name=name,
        grid_spec=pltpu.PrefetchScalarGridSpec(
            num_scalar_prefetch=1, grid=(2, rows // tr), in_specs=[full, half, half, full, full],
            out_specs=[full] * 4),
        out_shape=[SDS(w.shape, F32)] * 4, compiler_params=_cparams(("parallel", "parallel")))(c_idx, w, gh, go, m, v)


def _adamw_layers(w, g0, g1, m, v, c_idx, name):
    _, _, rows, cols = w.shape
    tr = _add_tile(rows, cols)

    def body(c_ref, w_ref, h0_ref, o0_ref, h1_ref, o1_ref, m_ref, v_ref, g_out, d_out, m_out, v_out):
        mine = pl.program_id(1) == c_ref[0]
        g = jnp.where(pl.program_id(0) == 0, jnp.where(mine, h0_ref[...], o0_ref[...]),
                      jnp.where(mine, h1_ref[...], o1_ref[...]))
        g_out[0, 0] = g
        d_out[0, 0], m_out[0, 0], v_out[0, 0] = _adamw_math(w_ref[0, 0], g, m_ref[0, 0], v_ref[0, 0])

    full = BS((1, 1, tr, cols), lambda l, hf, i, c: (l, hf, i, 0))
    half = BS((tr, cols), lambda l, hf, i, c: (i, 0))
    return pl.pallas_call(
        body, name=name,
        grid_spec=pltpu.PrefetchScalarGridSpec(
            num_scalar_prefetch=1, grid=(2, 2, rows // tr), in_specs=[full, half, half, half, half, full, full],
            out_specs=[full] * 4),
        out_shape=[SDS(w.shape, F32)] * 4,
        compiler_params=_cparams(("parallel", "parallel", "parallel")))(c_idx, w, *g0, *g1, m, v)


def _sum_devices(sa, name):
    def body(sa_ref, g_out):
        g = sa_ref[0]
        for d in range(1, N_DEV):
            g = g + sa_ref[d]
        g_out[...] = g

    return pl.pallas_call(body, name=name, out_shape=SDS(sa.shape[1:], F32),
                          compiler_params=pltpu.CompilerParams(vmem_limit_bytes=VMEM_LIMIT))(sa)


def _adamw_small(ws, gs, ms, vs, name):
    k = len(ws)

    def body(*refs):
        ins, outs = refs[:4 * k], refs[4 * k:]
        for i in range(k):
            d, m, v = _adamw_math(ins[i][...], ins[k + i][...], ins[2 * k + i][...], ins[3 * k + i][...])
            outs[i][...], outs[k + i][...], outs[2 * k + i][...] = d, m, v

    return pl.pallas_call(body, name=name, out_shape=[SDS(w.shape, F32) for w in ws] * 3,
                          compiler_params=pltpu.CompilerParams(vmem_limit_bytes=VMEM_LIMIT))(*ws, *gs, *ms, *vs)


HALVED = {"w_in": (2, 1024, 1450), "mla_w_q_up": (2, 256, 192), "mla_w_kv_up": (2, 128, 256),
          "w_branch": (2, 1536, 256), "w_o": (2, 256, 1024), "ffn_w_up": (2, 1024, 1408),
          "ffn_w_down": (2, 704, 1024), "ffn_conv_w": (2, 3, 1408), "meta_tokens": (2, 8, 256)}
SMALL_SHAPE = {"norm1_g": (2, 1024), "fox_forget_b": (2, 8), "fox_q_g": (2, 64), "fox_k_g": (2, 64),
               "mla_q_a_g": (2, 256), "mla_kv_a_g": (2, 128), "mla_q_g": (2, 96), "mla_k_g": (2, 96),
               "swa_q_g": (2, 64), "swa_k_g": (2, 64), "swa_sinks": (2, 8), "norm2_g": (2, 1024),
               "ffn_conv_b": (2, 5632)}
SMALL_ROWS = _pack_rows([SMALL_SHAPE[k] for k in SMALL] + [(1,)], SUBLANES)


def kernel(x, meta_tokens, norm1_g, w_in, fox_forget_b, fox_q_g, fox_k_g, mla_q_a_g, mla_w_q_up, mla_kv_a_g, mla_w_kv_up, mla_q_g, mla_k_g, swa_q_g, swa_k_g, swa_sinks, w_branch, w_o, norm2_g, ffn_w_up, ffn_conv_w, ffn_conv_b, ffn_w_down, loss_target, m_meta_tokens, m_norm1_g, m_w_in, m_fox_forget_b, m_fox_q_g, m_fox_k_g, m_mla_q_a_g, m_mla_w_q_up, m_mla_kv_a_g, m_mla_w_kv_up, m_mla_q_g, m_mla_k_g, m_swa_q_g, m_swa_k_g, m_swa_sinks, m_w_branch, m_w_o, m_norm2_g, m_ffn_w_up, m_ffn_conv_w, m_ffn_conv_b, m_ffn_w_down, v_meta_tokens, v_norm1_g, v_w_in, v_fox_forget_b, v_fox_q_g, v_fox_k_g, v_mla_q_a_g, v_mla_w_q_up, v_mla_kv_a_g, v_mla_w_kv_up, v_mla_q_g, v_mla_k_g, v_swa_q_g, v_swa_k_g, v_swa_sinks, v_w_branch, v_w_o, v_norm2_g, v_ffn_w_up, v_ffn_conv_w, v_ffn_conv_b, v_ffn_w_down):
    w = dict(meta_tokens=meta_tokens, norm1_g=norm1_g, w_in=w_in, fox_forget_b=fox_forget_b, fox_q_g=fox_q_g,
             fox_k_g=fox_k_g, mla_q_a_g=mla_q_a_g, mla_w_q_up=mla_w_q_up, mla_kv_a_g=mla_kv_a_g,
             mla_w_kv_up=mla_w_kv_up, mla_q_g=mla_q_g, mla_k_g=mla_k_g, swa_q_g=swa_q_g, swa_k_g=swa_k_g,
             swa_sinks=swa_sinks, w_branch=w_branch, w_o=w_o, norm2_g=norm2_g, ffn_w_up=ffn_w_up,
             ffn_conv_w=ffn_conv_w, ffn_conv_b=ffn_conv_b, ffn_w_down=ffn_w_down)
    m = dict(meta_tokens=m_meta_tokens, norm1_g=m_norm1_g, w_in=m_w_in, fox_forget_b=m_fox_forget_b,
             fox_q_g=m_fox_q_g, fox_k_g=m_fox_k_g, mla_q_a_g=m_mla_q_a_g, mla_w_q_up=m_mla_w_q_up,
             mla_kv_a_g=m_mla_kv_a_g, mla_w_kv_up=m_mla_w_kv_up, mla_q_g=m_mla_q_g, mla_k_g=m_mla_k_g,
             swa_q_g=m_swa_q_g, swa_k_g=m_swa_k_g, swa_sinks=m_swa_sinks, w_branch=m_w_branch, w_o=m_w_o,
             norm2_g=m_norm2_g, ffn_w_up=m_ffn_w_up, ffn_conv_w=m_ffn_conv_w, ffn_conv_b=m_ffn_conv_b,
             ffn_w_down=m_ffn_w_down)
    v = dict(meta_tokens=v_meta_tokens, norm1_g=v_norm1_g, w_in=v_w_in, fox_forget_b=v_fox_forget_b,
             fox_q_g=v_fox_q_g, fox_k_g=v_fox_k_g, mla_q_a_g=v_mla_q_a_g, mla_w_q_up=v_mla_w_q_up,
             mla_kv_a_g=v_mla_kv_a_g, mla_w_kv_up=v_mla_w_kv_up, mla_q_g=v_mla_q_g, mla_k_g=v_mla_k_g,
             swa_q_g=v_swa_q_g, swa_k_g=v_swa_k_g, swa_sinks=v_swa_sinks, w_branch=v_w_branch, w_o=v_w_o,
             norm2_g=v_norm2_g, ffn_w_up=v_ffn_w_up, ffn_conv_w=v_ffn_conv_w, ffn_conv_b=v_ffn_conv_b,
             ffn_w_down=v_ffn_w_down)
    xi, yi, ci = _me()
    c_idx = ci.astype(jnp.int32).reshape(1)
    j_idx = (2 * xi + yi).astype(jnp.int32).reshape(1)

    sh_names = BIG + FINE
    local = {k: (w[k].astype(BF16) if k in BIG else w[k]).reshape(HALVED[k]) for k in sh_names}
    late = [local[k] for k in LAYERED]
    meta_g, early = _gather_early(local["meta_tokens"], [local["w_in"]], "gather_early")
    meta = jnp.concatenate([meta_g[i].reshape(SHARD_SHAPE["meta_tokens"]) for i in range(N_CHIPS)], axis=1)
    w0 = _assemble_layer(_layer_parts(("w_in",), early))
    small = {k: w[k] for k in SMALL}

    loss, g_x, g_eps, g_red, g_cw, g_meta, g_small = _local_step(x[0], loss_target[0], w0, late, meta, small)
    g_after0 = dict(zip(AFTER_MLA, g_red[0]))
    g_l1 = dict(zip(BEFORE_MLA + AFTER_MLA, g_red[1] + g_red[2]))

    quarter = {k: (2, HALVED[k][1] // 2, HALVED[k][2]) for k in BIG}
    last = BEFORE_MLA + FINE
    gs = [g_eps[k].reshape((N_CHIPS,) + quarter[k]) for k in BEFORE_MLA]
    for k, g in (("meta_tokens", g_meta), ("ffn_conv_w", g_cw)):
        gs.append(jnp.stack(jnp.split(g, N_CHIPS, axis=SHARD_AXIS[k])).reshape((N_CHIPS,) + HALVED[k]))
    spack = _pack([g_small[k] for k in SMALL] + [loss.reshape(1)], SMALL_ROWS, F32)
    r1 = _pair_exchange(gs, "grads_pair_exchange")
    s1 = [_pair_add(g, r, c_idx, "grads_pair_add_" + k) for g, r, k in zip(gs, r1, last)]
    r2, sa = _chip_exchange([s[1] for s in s1], spack, "grads_chip_exchange")
    gh = dict(zip(last, [_chip_add(s[0], r, j_idx, "grads_chip_add_" + k) for s, r, k in zip(s1, r2, last)]))
    go = dict(zip(last, _half_exchange([gh[k] for k in last], "grads_half_exchange")))

    grads, deltas, new_m, new_v = {}, {}, {}, {}
    for k in sh_names:
        if k in BIG:
            shp = (2,) + quarter[k]
            wk, mk, vk = w[k].reshape(shp), m[k].reshape(shp), v[k].reshape(shp)
            g0 = (gh[k], go[k]) if k in BEFORE_MLA else g_after0[k]
            outs = _adamw_layers(wk, g0, g_l1[k], mk, vk, c_idx, "adamw_" + k)
        else:
            outs = _adamw(w[k].reshape(HALVED[k]), gh[k], go[k], m[k].reshape(HALVED[k]), v[k].reshape(HALVED[k]),
                          c_idx, "adamw_" + k)
        for dst, o in zip((grads, deltas, new_m, new_v), outs):
            dst[k] = o.reshape(SHARD_SHAPE[k])
    sm_shapes = [SMALL_SHAPE[k] for k in SMALL] + [(1,)]
    g_sum = _unpack(_sum_devices(sa, "sum_small"), sm_shapes)
    res = _adamw_small([w[k] for k in SMALL], g_sum[:-1], [m[k] for k in SMALL], [v[k] for k in SMALL], "adamw_small")
    ns = len(SMALL)
    grads.update(zip(SMALL, g_sum[:-1]))
    for dst, vals in zip((deltas, new_m, new_v), (res[:ns], res[ns:2 * ns], res[2 * ns:])):
        dst.update(zip(SMALL, vals))
    total_loss = g_sum[-1][0]
    return (total_loss, g_x[None], *[grads[k] for k in WEIGHTS], *[deltas[k] for k in WEIGHTS],
            *[new_m[k] for k in WEIGHTS], *[new_v[k] for k in WEIGHTS])
```
